```python
import jax, jax.numpy as jnp
from jax import lax
import numpy as np

D_MODEL = 1024
BATCH = 8
SEQ = 8192
DEPTH = 4

CHUNK = 64
EXPAND = 2
D_MIX = EXPAND * D_MODEL
D_A = D_MIX // 2
D_B = D_MIX - D_A
D_C = D_MIX // 2
D_D = D_MIX - D_C
POOL_WINDOWS = (2, 4, 8, 16)
N_POOL_GROUPS = len(POOL_WINDOWS)
POOL_GROUP = D_A // N_POOL_GROUPS
SHORT_CONV = 3
SGU_BLOCK = 128
SGU_HEADS = 4
SGU_HEAD_DIM = D_C // SGU_HEADS
CONF_CONV = 31
IN_EVEN = 2 * D_A + 4 * D_B
IN_ODD = 3 * D_C + 3 * D_D
N_EVEN = (DEPTH + 1) // 2
N_ODD = DEPTH // 2
DEEPNORM_ALPHA = (2 * DEPTH) ** 0.25
DEEPNORM_BETA = (8 * DEPTH) ** -0.25
LN_EPS = 1e-5

kernel_name = "hybrid_pool_conv_sgu_conformer_deepnorm"


def layer_norm(x, g, b):
    xf = x.astype(jnp.float32)
    mu = jnp.mean(xf, axis=-1, keepdims=True)
    var = jnp.mean(jnp.square(xf - mu), axis=-1, keepdims=True)
    return ((xf - mu) * lax.rsqrt(var + LN_EPS) * g.astype(jnp.float32) + b.astype(jnp.float32)).astype(x.dtype)


def split_cols(z, sizes):
    points = list(np.cumsum(sizes)[:-1])
    return jnp.split(z, points, axis=-1)


def causal_depthwise_conv(z, w, b):
    k = w.shape[0]
    y = lax.conv_general_dilated(
        z, w[:, None, :].astype(z.dtype), window_strides=(1,), padding=[(k - 1, 0)],
        dimension_numbers=('NWC', 'WIO', 'NWC'), feature_group_count=z.shape[-1])
    return y + b.astype(z.dtype)


def multi_scale_pool(z):
    s = z.shape[1]
    zf = z.astype(jnp.float32)
    cs0 = jnp.pad(jnp.cumsum(zf, axis=1), ((0, 0), (1, 0), (0, 0)))
    pos = jnp.arange(1, s + 1, dtype=jnp.float32)
    means = []
    for g, w in enumerate(POOL_WINDOWS):
        cg = cs0[..., g * POOL_GROUP:(g + 1) * POOL_GROUP]
        lagged = jnp.pad(cg[:, :s + 1 - w], ((0, 0), (w - 1, 0), (0, 0)))
        count = jnp.minimum(pos, float(w))[None, :, None]
        means.append((cg[:, 1:] - lagged) / count)
    return (jnp.concatenate(means, axis=-1) - zf).astype(z.dtype)


def sgu_mask():
    idx = jnp.arange(SGU_BLOCK)
    return (idx[None, :] // CHUNK) <= (idx[:, None] // CHUNK)


def pool_conv_layer(x, w_in, w_out, pool_w, pool_scale, sconv_w, sconv_b):
    bsz, s, _ = x.shape
    xa, ga, h, bg, cg, gb = split_cols(x @ w_in, [D_A, D_A, D_B, D_B, D_B, D_B])
    pooled = multi_scale_pool(xa).reshape(bsz, s, N_POOL_GROUPS, POOL_GROUP)
    ya = jnp.einsum('bsgc,gcd->bsgd', pooled, pool_w).reshape(bsz, s, D_A) * pool_scale
    ya = ya * jax.nn.silu(ga)
    yb = bg * causal_depthwise_conv(cg * h, sconv_w, sconv_b)
    yb = yb * jax.nn.silu(gb)
    return jnp.concatenate([ya, yb], axis=-1) @ w_out


def sgu_conformer_layer(x, w_in, w_out, sgu_ln_g, sgu_ln_b, sgu_w, sgu_b,
                        dconv_w, dconv_b, dnorm_g, dnorm_b):
    bsz, s, _ = x.shape
    u, v, gc, a, bglu, gd = split_cols(x @ w_in, [D_C, D_C, D_C, D_D, D_D, D_D])
    v = layer_norm(v, sgu_ln_g, sgu_ln_b)
    v = v.reshape(bsz, s // SGU_BLOCK, SGU_BLOCK, SGU_HEADS, SGU_HEAD_DIM)
    ws = jnp.where(sgu_mask()[None], sgu_w, 0.0).astype(v.dtype)
    sv = jnp.einsum('hij,bnjhc->bnihc', ws, v) + jnp.transpose(sgu_b)[:, :, None].astype(v.dtype)
    yc = u * sv.reshape(bsz, s, D_C) * jax.nn.silu(gc)
    z = a * jax.nn.sigmoid(bglu)
    z = causal_depthwise_conv(z, dconv_w, dconv_b)
    z = jax.nn.silu(layer_norm(z, dnorm_g, dnorm_b))
    yd = z * jax.nn.silu(gd)
    return jnp.concatenate([yc, yd], axis=-1) @ w_out


def _fwd_setup_inputs(seed: int = 0) -> dict:
    key = jax.random.key(seed)
    ks = jax.random.split(key, 20)
    f32 = jnp.float32
    nrm = lambda k, shp, sc: (jax.random.normal(k, shp, f32) * sc).astype(f32)
    return {
        "x": nrm(ks[0], (BATCH, SEQ, D_MODEL), 1.0),
        "ln_g": 1.0 + nrm(ks[1], (DEPTH, D_MODEL), 0.02),
        "ln_b": nrm(ks[2], (DEPTH, D_MODEL), 0.02),
        "w_in_even": nrm(ks[3], (N_EVEN, D_MODEL, IN_EVEN), D_MODEL ** -0.5),
        "w_out_even": nrm(ks[4], (N_EVEN, D_MIX, D_MODEL), DEEPNORM_BETA * D_MIX ** -0.5),
        "pool_w": nrm(ks[5], (N_EVEN, N_POOL_GROUPS, POOL_GROUP, POOL_GROUP), POOL_GROUP ** -0.5),
        "pool_scale": 1.0 + nrm(ks[6], (N_EVEN, D_A), 0.1),
        "sconv_w": nrm(ks[7], (N_EVEN, SHORT_CONV, D_B), SHORT_CONV ** -0.5),
        "sconv_b": nrm(ks[8], (N_EVEN, D_B), 0.02),
        "w_in_odd": nrm(ks[9], (N_ODD, D_MODEL, IN_ODD), D_MODEL ** -0.5),
        "w_out_odd": nrm(ks[10], (N_ODD, D_MIX, D_MODEL), DEEPNORM_BETA * D_MIX ** -0.5),
        "sgu_ln_g": 1.0 + nrm(ks[11], (N_ODD, D_C), 0.02),
        "sgu_ln_b": nrm(ks[12], (N_ODD, D_C), 0.02),
        "sgu_w": nrm(ks[13], (N_ODD, SGU_HEADS, SGU_BLOCK, SGU_BLOCK), SGU_BLOCK ** -0.5),
        "sgu_b": 1.0 + nrm(ks[14], (N_ODD, SGU_HEADS, SGU_BLOCK), 0.01),
        "dconv_w": nrm(ks[15], (N_ODD, CONF_CONV, D_D), CONF_CONV ** -0.5),
        "dconv_b": nrm(ks[16], (N_ODD, D_D), 0.02),
        "dnorm_g": 1.0 + nrm(ks[17], (N_ODD, D_D), 0.02),
        "dnorm_b": nrm(ks[18], (N_ODD, D_D), 0.02),
    }


def _fwd_reference(x, ln_g, ln_b, w_in_even, w_out_even, pool_w, pool_scale, sconv_w, sconv_b,
              w_in_odd, w_out_odd, sgu_ln_g, sgu_ln_b, sgu_w, sgu_b,
              dconv_w, dconv_b, dnorm_g, dnorm_b):
    for layer in range(DEPTH):
        i = layer // 2
        if layer % 2 == 0:
            y = pool_conv_layer(x, w_in_even[i], w_out_even[i], pool_w[i], pool_scale[i],
                                sconv_w[i], sconv_b[i])
        else:
            y = sgu_conformer_layer(x, w_in_odd[i], w_out_odd[i], sgu_ln_g[i], sgu_ln_b[i],
                                    sgu_w[i], sgu_b[i], dconv_w[i], dconv_b[i],
                                    dnorm_g[i], dnorm_b[i])
        x = layer_norm(DEEPNORM_ALPHA * x + y, ln_g[layer], ln_b[layer])
    return x


import jax as _jax
import jax.numpy as _jnp

TWIN_FORMAT = 'train_step'
FWD_PARAMS = ['x', 'ln_g', 'ln_b', 'w_in_even', 'w_out_even', 'pool_w', 'pool_scale', 'sconv_w', 'sconv_b', 'w_in_odd', 'w_out_odd', 'sgu_ln_g', 'sgu_ln_b', 'sgu_w', 'sgu_b', 'dconv_w', 'dconv_b', 'dnorm_g', 'dnorm_b']
TWIN_WEIGHTS = ['ln_g', 'ln_b', 'w_in_even', 'w_out_even', 'pool_w', 'pool_scale', 'sconv_w', 'sconv_b', 'w_in_odd', 'w_out_odd', 'sgu_ln_g', 'sgu_ln_b', 'sgu_w', 'sgu_b', 'dconv_w', 'dconv_b', 'dnorm_g', 'dnorm_b']
TWIN_DIFF_INPUT = 'x'
TWIN_INPUTS = ['x', 'ln_g', 'ln_b', 'w_in_even', 'w_out_even', 'pool_w', 'pool_scale', 'sconv_w', 'sconv_b', 'w_in_odd', 'w_out_odd', 'sgu_ln_g', 'sgu_ln_b', 'sgu_w', 'sgu_b', 'dconv_w', 'dconv_b', 'dnorm_g', 'dnorm_b', 'loss_target', 'm_ln_g', 'm_ln_b', 'm_w_in_even', 'm_w_out_even', 'm_pool_w', 'm_pool_scale', 'm_sconv_w', 'm_sconv_b', 'm_w_in_odd', 'm_w_out_odd', 'm_sgu_ln_g', 'm_sgu_ln_b', 'm_sgu_w', 'm_sgu_b', 'm_dconv_w', 'm_dconv_b', 'm_dnorm_g', 'm_dnorm_b', 'v_ln_g', 'v_ln_b', 'v_w_in_even', 'v_w_out_even', 'v_pool_w', 'v_pool_scale', 'v_sconv_w', 'v_sconv_b', 'v_w_in_odd', 'v_w_out_odd', 'v_sgu_ln_g', 'v_sgu_ln_b', 'v_sgu_w', 'v_sgu_b', 'v_dconv_w', 'v_dconv_b', 'v_dnorm_g', 'v_dnorm_b']
TWIN_OUTPUTS = ['loss', 'grad_x', 'grad_ln_g', 'grad_ln_b', 'grad_w_in_even', 'grad_w_out_even', 'grad_pool_w', 'grad_pool_scale', 'grad_sconv_w', 'grad_sconv_b', 'grad_w_in_odd', 'grad_w_out_odd', 'grad_sgu_ln_g', 'grad_sgu_ln_b', 'grad_sgu_w', 'grad_sgu_b', 'grad_dconv_w', 'grad_dconv_b', 'grad_dnorm_g', 'grad_dnorm_b', 'delta_ln_g', 'delta_ln_b', 'delta_w_in_even', 'delta_w_out_even', 'delta_pool_w', 'delta_pool_scale', 'delta_sconv_w', 'delta_sconv_b', 'delta_w_in_odd', 'delta_w_out_odd', 'delta_sgu_ln_g', 'delta_sgu_ln_b', 'delta_sgu_w', 'delta_sgu_b', 'delta_dconv_w', 'delta_dconv_b', 'delta_dnorm_g', 'delta_dnorm_b', 'new_m_ln_g', 'new_m_ln_b', 'new_m_w_in_even', 'new_m_w_out_even', 'new_m_pool_w', 'new_m_pool_scale', 'new_m_sconv_w', 'new_m_sconv_b', 'new_m_w_in_odd', 'new_m_w_out_odd', 'new_m_sgu_ln_g', 'new_m_sgu_ln_b', 'new_m_sgu_w', 'new_m_sgu_b', 'new_m_dconv_w', 'new_m_dconv_b', 'new_m_dnorm_g', 'new_m_dnorm_b', 'new_v_ln_g', 'new_v_ln_b', 'new_v_w_in_even', 'new_v_w_out_even', 'new_v_pool_w', 'new_v_pool_scale', 'new_v_sconv_w', 'new_v_sconv_b', 'new_v_w_in_odd', 'new_v_w_out_odd', 'new_v_sgu_ln_g', 'new_v_sgu_ln_b', 'new_v_sgu_w', 'new_v_sgu_b', 'new_v_dconv_w', 'new_v_dconv_b', 'new_v_dnorm_g', 'new_v_dnorm_b']
TWIN_LEAF_KINDS = {'loss': 'loss', 'grad_x': 'grad_x', 'grad_ln_g': 'grad_w', 'grad_ln_b': 'grad_w', 'grad_w_in_even': 'grad_w', 'grad_w_out_even': 'grad_w', 'grad_pool_w': 'grad_w', 'grad_pool_scale': 'grad_w', 'grad_sconv_w': 'grad_w', 'grad_sconv_b': 'grad_w', 'grad_w_in_odd': 'grad_w', 'grad_w_out_odd': 'grad_w', 'grad_sgu_ln_g': 'grad_w', 'grad_sgu_ln_b': 'grad_w', 'grad_sgu_w': 'grad_w', 'grad_sgu_b': 'grad_w', 'grad_dconv_w': 'grad_w', 'grad_dconv_b': 'grad_w', 'grad_dnorm_g': 'grad_w', 'grad_dnorm_b': 'grad_w', 'delta_ln_g': 'delta_w', 'delta_ln_b': 'delta_w', 'delta_w_in_even': 'delta_w', 'delta_w_out_even': 'delta_w', 'delta_pool_w': 'delta_w', 'delta_pool_scale': 'delta_w', 'delta_sconv_w': 'delta_w', 'delta_sconv_b': 'delta_w', 'delta_w_in_odd': 'delta_w', 'delta_w_out_odd': 'delta_w', 'delta_sgu_ln_g': 'delta_w', 'delta_sgu_ln_b': 'delta_w', 'delta_sgu_w': 'delta_w', 'delta_sgu_b': 'delta_w', 'delta_dconv_w': 'delta_w', 'delta_dconv_b': 'delta_w', 'delta_dnorm_g': 'delta_w', 'delta_dnorm_b': 'delta_w', 'new_m_ln_g': 'new_m', 'new_m_ln_b': 'new_m', 'new_m_w_in_even': 'new_m', 'new_m_w_out_even': 'new_m', 'new_m_pool_w': 'new_m', 'new_m_pool_scale': 'new_m', 'new_m_sconv_w': 'new_m', 'new_m_sconv_b': 'new_m', 'new_m_w_in_odd': 'new_m', 'new_m_w_out_odd': 'new_m', 'new_m_sgu_ln_g': 'new_m', 'new_m_sgu_ln_b': 'new_m', 'new_m_sgu_w': 'new_m', 'new_m_sgu_b': 'new_m', 'new_m_dconv_w': 'new_m', 'new_m_dconv_b': 'new_m', 'new_m_dnorm_g': 'new_m', 'new_m_dnorm_b': 'new_m', 'new_v_ln_g': 'new_v', 'new_v_ln_b': 'new_v', 'new_v_w_in_even': 'new_v', 'new_v_w_out_even': 'new_v', 'new_v_pool_w': 'new_v', 'new_v_pool_scale': 'new_v', 'new_v_sconv_w': 'new_v', 'new_v_sconv_b': 'new_v', 'new_v_w_in_odd': 'new_v', 'new_v_w_out_odd': 'new_v', 'new_v_sgu_ln_g': 'new_v', 'new_v_sgu_ln_b': 'new_v', 'new_v_sgu_w': 'new_v', 'new_v_sgu_b': 'new_v', 'new_v_dconv_w': 'new_v', 'new_v_dconv_b': 'new_v', 'new_v_dnorm_g': 'new_v', 'new_v_dnorm_b': 'new_v'}


def _forward(args):
    return _fwd_reference(*[args[k] for k in FWD_PARAMS])


def _output_shape():
    def fwd():
        inp = _fwd_setup_inputs(0)
        return _fwd_reference(*[inp[k] for k in FWD_PARAMS])
    out = _jax.eval_shape(fwd)
    return out.shape, out.dtype

N_MICROBATCH = 1
ADAM_LR = 0.001
ADAM_B1 = 0.9
ADAM_B2 = 0.999
ADAM_EPS = 1e-08
ADAM_WD = 0.01
ADAM_STEP = 10
PER_EXAMPLE_BATCH_AXIS = {'x': 0, 'loss_target': 0}
SHARED_INPUTS = []
_WEIGHT_DTYPES = {'ln_g': _jnp.float32, 'ln_b': _jnp.float32, 'w_in_even': _jnp.float32, 'w_out_even': _jnp.float32, 'pool_w': _jnp.float32, 'pool_scale': _jnp.float32, 'sconv_w': _jnp.float32, 'sconv_b': _jnp.float32, 'w_in_odd': _jnp.float32, 'w_out_odd': _jnp.float32, 'sgu_ln_g': _jnp.float32, 'sgu_ln_b': _jnp.float32, 'sgu_w': _jnp.float32, 'sgu_b': _jnp.float32, 'dconv_w': _jnp.float32, 'dconv_b': _jnp.float32, 'dnorm_g': _jnp.float32, 'dnorm_b': _jnp.float32}
MOMENT_SCALE = {'ln_g': 3.210399e+01, 'ln_b': 1.476591e+00, 'w_in_even': 2.621659e-02, 'w_out_even': 8.569008e-02, 'pool_w': 2.382631e-02, 'pool_scale': 2.335981e-02, 'sconv_w': 2.776037e-02, 'sconv_b': 2.758229e-02, 'w_in_odd': 2.481728e-02, 'w_out_odd': 9.287155e-02, 'sgu_ln_g': 2.317044e-02, 'sgu_ln_b': 2.348823e-02, 'sgu_w': 3.258161e-02, 'sgu_b': 3.972792e-02, 'dconv_w': 1.651327e-02, 'dconv_b': 3.595550e-02, 'dnorm_g': 2.147890e-02, 'dnorm_b': 2.101713e-02}


def _to_microbatches(a, axis):
    t = _jnp.moveaxis(a, axis, 0)
    t = t.reshape((N_MICROBATCH, t.shape[0] // N_MICROBATCH) + t.shape[1:])
    return _jnp.moveaxis(t, 1, axis + 1)


def setup_inputs(seed: int = 0) -> dict:
    inp = _fwd_setup_inputs(seed)
    key = _jax.random.fold_in(_jax.random.key(seed), 7919)
    shape, _ = _output_shape()
    out = dict(inp)
    out["loss_target"] = _jax.random.normal(_jax.random.fold_in(key, 0), shape, _jnp.float32)
    for i, name in enumerate(TWIN_WEIGHTS):
        w = inp[name].astype(_jnp.float32)
        if MOMENT_SCALE is None:
            s = _jnp.sqrt(_jnp.mean(_jnp.square(w)) + 1e-30)
        else:
            s = MOMENT_SCALE[name]
        km, kv = _jax.random.split(_jax.random.fold_in(key, i + 1))
        out[name] = w
        out["m_" + name] = s * _jax.random.normal(km, w.shape, _jnp.float32)
        out["v_" + name] = (s * s) * _jax.random.uniform(kv, w.shape, _jnp.float32, 0.5, 1.5)
    if N_MICROBATCH > 1:
        for name, axis in PER_EXAMPLE_BATCH_AXIS.items():
            out[name] = _to_microbatches(out[name], axis)
    return {'x': out['x'], 'ln_g': out['ln_g'], 'ln_b': out['ln_b'], 'w_in_even': out['w_in_even'], 'w_out_even': out['w_out_even'], 'pool_w': out['pool_w'], 'pool_scale': out['pool_scale'], 'sconv_w': out['sconv_w'], 'sconv_b': out['sconv_b'], 'w_in_odd': out['w_in_odd'], 'w_out_odd': out['w_out_odd'], 'sgu_ln_g': out['sgu_ln_g'], 'sgu_ln_b': out['sgu_ln_b'], 'sgu_w': out['sgu_w'], 'sgu_b': out['sgu_b'], 'dconv_w': out['dconv_w'], 'dconv_b': out['dconv_b'], 'dnorm_g': out['dnorm_g'], 'dnorm_b': out['dnorm_b'], 'loss_target': out['loss_target'], 'm_ln_g': out['m_ln_g'], 'm_ln_b': out['m_ln_b'], 'm_w_in_even': out['m_w_in_even'], 'm_w_out_even': out['m_w_out_even'], 'm_pool_w': out['m_pool_w'], 'm_pool_scale': out['m_pool_scale'], 'm_sconv_w': out['m_sconv_w'], 'm_sconv_b': out['m_sconv_b'], 'm_w_in_odd': out['m_w_in_odd'], 'm_w_out_odd': out['m_w_out_odd'], 'm_sgu_ln_g': out['m_sgu_ln_g'], 'm_sgu_ln_b': out['m_sgu_ln_b'], 'm_sgu_w': out['m_sgu_w'], 'm_sgu_b': out['m_sgu_b'], 'm_dconv_w': out['m_dconv_w'], 'm_dconv_b': out['m_dconv_b'], 'm_dnorm_g': out['m_dnorm_g'], 'm_dnorm_b': out['m_dnorm_b'], 'v_ln_g': out['v_ln_g'], 'v_ln_b': out['v_ln_b'], 'v_w_in_even': out['v_w_in_even'], 'v_w_out_even': out['v_w_out_even'], 'v_pool_w': out['v_pool_w'], 'v_pool_scale': out['v_pool_scale'], 'v_sconv_w': out['v_sconv_w'], 'v_sconv_b': out['v_sconv_b'], 'v_w_in_odd': out['v_w_in_odd'], 'v_w_out_odd': out['v_w_out_odd'], 'v_sgu_ln_g': out['v_sgu_ln_g'], 'v_sgu_ln_b': out['v_sgu_ln_b'], 'v_sgu_w': out['v_sgu_w'], 'v_sgu_b': out['v_sgu_b'], 'v_dconv_w': out['v_dconv_w'], 'v_dconv_b': out['v_dconv_b'], 'v_dnorm_g': out['v_dnorm_g'], 'v_dnorm_b': out['v_dnorm_b']}


def _loss(weights, diff, rest, loss_target):
    with _jax.named_scope("forward"):
        args = {**rest, TWIN_DIFF_INPUT: diff, **{k: w.astype(_WEIGHT_DTYPES[k]) for k, w in weights.items()}}
        y = _forward(args)
    with _jax.named_scope("loss_head"):
        err = _jnp.square(y.astype(_jnp.float32) - loss_target)
        return 0.5 * _jnp.sum(_jnp.mean(err, axis=-1)) if err.ndim else 0.5 * err


def _adamw(w, g, m, v):
    m = ADAM_B1 * m + (1.0 - ADAM_B1) * g
    v = ADAM_B2 * v + (1.0 - ADAM_B2) * _jnp.square(g)
    m_hat = m / (1.0 - ADAM_B1 ** ADAM_STEP)
    v_hat = v / (1.0 - ADAM_B2 ** ADAM_STEP)
    delta = -ADAM_LR * (m_hat / (_jnp.sqrt(v_hat) + ADAM_EPS) + ADAM_WD * w)
    return delta, m, v


def reference(x, ln_g, ln_b, w_in_even, w_out_even, pool_w, pool_scale, sconv_w, sconv_b, w_in_odd, w_out_odd, sgu_ln_g, sgu_ln_b, sgu_w, sgu_b, dconv_w, dconv_b, dnorm_g, dnorm_b, loss_target, m_ln_g, m_ln_b, m_w_in_even, m_w_out_even, m_pool_w, m_pool_scale, m_sconv_w, m_sconv_b, m_w_in_odd, m_w_out_odd, m_sgu_ln_g, m_sgu_ln_b, m_sgu_w, m_sgu_b, m_dconv_w, m_dconv_b, m_dnorm_g, m_dnorm_b, v_ln_g, v_ln_b, v_w_in_even, v_w_out_even, v_pool_w, v_pool_scale, v_sconv_w, v_sconv_b, v_w_in_odd, v_w_out_odd, v_sgu_ln_g, v_sgu_ln_b, v_sgu_w, v_sgu_b, v_dconv_w, v_dconv_b, v_dnorm_g, v_dnorm_b):
    given = dict(x=x, ln_g=ln_g, ln_b=ln_b, w_in_even=w_in_even, w_out_even=w_out_even, pool_w=pool_w, pool_scale=pool_scale, sconv_w=sconv_w, sconv_b=sconv_b, w_in_odd=w_in_odd, w_out_odd=w_out_odd, sgu_ln_g=sgu_ln_g, sgu_ln_b=sgu_ln_b, sgu_w=sgu_w, sgu_b=sgu_b, dconv_w=dconv_w, dconv_b=dconv_b, dnorm_g=dnorm_g, dnorm_b=dnorm_b, loss_target=loss_target, m_ln_g=m_ln_g, m_ln_b=m_ln_b, m_w_in_even=m_w_in_even, m_w_out_even=m_w_out_even, m_pool_w=m_pool_w, m_pool_scale=m_pool_scale, m_sconv_w=m_sconv_w, m_sconv_b=m_sconv_b, m_w_in_odd=m_w_in_odd, m_w_out_odd=m_w_out_odd, m_sgu_ln_g=m_sgu_ln_g, m_sgu_ln_b=m_sgu_ln_b, m_sgu_w=m_sgu_w, m_sgu_b=m_sgu_b, m_dconv_w=m_dconv_w, m_dconv_b=m_dconv_b, m_dnorm_g=m_dnorm_g, m_dnorm_b=m_dnorm_b, v_ln_g=v_ln_g, v_ln_b=v_ln_b, v_w_in_even=v_w_in_even, v_w_out_even=v_w_out_even, v_pool_w=v_pool_w, v_pool_scale=v_pool_scale, v_sconv_w=v_sconv_w, v_sconv_b=v_sconv_b, v_w_in_odd=v_w_in_odd, v_w_out_odd=v_w_out_odd, v_sgu_ln_g=v_sgu_ln_g, v_sgu_ln_b=v_sgu_ln_b, v_sgu_w=v_sgu_w, v_sgu_b=v_sgu_b, v_dconv_w=v_dconv_w, v_dconv_b=v_dconv_b, v_dnorm_g=v_dnorm_g, v_dnorm_b=v_dnorm_b)
    weights = {n: given[n] for n in TWIN_WEIGHTS}
    shared = {n: given[n] for n in SHARED_INPUTS}
    per_example = {n: given[n] for n in ['x']}
    grad_fn = _jax.value_and_grad(_loss, argnums=(0, 1))

    def one_microbatch(ex, loss_target):
        ex = dict(ex)
        diff = ex.pop(TWIN_DIFF_INPUT)
        return grad_fn(weights, diff, {**shared, **ex}, loss_target)

    if N_MICROBATCH == 1:
        loss, (grad_w, grad_x) = one_microbatch(per_example, given["loss_target"])
    else:
        def body(carry, xs):
            loss_sum, grad_sum = carry
            l_k, (gw_k, gx_k) = one_microbatch(xs[0], xs[1])
            with _jax.named_scope("update"):
                return (loss_sum + l_k, _jax.tree.map(_jnp.add, grad_sum, gw_k)), gx_k

        init = (_jnp.zeros((), _jnp.float32), _jax.tree.map(_jnp.zeros_like, weights))
        (loss, grad_w), grad_x = _jax.lax.scan(body, init, (per_example, given["loss_target"]))
    with _jax.named_scope("update"):
        delta_w, new_m, new_v = {}, {}, {}
        for n in TWIN_WEIGHTS:
            delta_w[n], new_m[n], new_v[n] = _adamw(weights[n], grad_w[n], given["m_" + n], given["v_" + n])
    return (loss, grad_x, *[grad_w[n] for n in TWIN_WEIGHTS], *[delta_w[n] for n in TWIN_WEIGHTS],
            *[new_m[n] for n in TWIN_WEIGHTS], *[new_v[n] for n in TWIN_WEIGHTS])
```

```python
import jax
import jax.numpy as jnp
from jax import lax
from jax.experimental import pallas as pl
from jax.experimental.pallas import tpu as pltpu

F32 = jnp.float32
BF16 = jnp.bfloat16
MXU_DTYPE = BF16

D = 1024
DZ = 6144
DY = 2048
NQ = 4
WQ = DZ // NQ
RQ = DY // NQ
NL = 4
ALPHA = (2 * NL) ** 0.25
LN_EPS = 1e-5
CONV_K = 31
SHORT_K = 3
SGU_BLOCK = 128
HEAD = 256
POOL_HALO = 16
CONV_HALO = 32
LANE = 128
MIB = 1024 * 1024

ADAM_LR = 0.001
ADAM_B1 = 0.9
ADAM_B2 = 0.999
ADAM_EPS = 1e-08
ADAM_WD = 0.01
ADAM_STEP = 10

NN = ((1,), (0,))
NT = ((1,), (1,))
TN = ((0,), (0,))
MESH = pl.DeviceIdType.MESH
HBM_SPEC = pl.BlockSpec(memory_space=pltpu.HBM)

R_LN_G, R_LN_B, R_PSCALE, R_SCONV_B, R_SCONV_W = 0, 4, 8, 10, 12
R_SLN_G, R_SLN_B, R_DCONV_B, R_DN_G, R_DN_B, R_DCONV_W = 18, 20, 22, 24, 26, 28
R_SGU_B, R_SGU_W, R_POOL_W, R_SMALL = 90, 91, 219, 736
Q_SCONV_W, Q_SLN_G, Q_SLN_B, Q_DCONV_B, Q_DN_G, Q_DN_B, Q_DCONV_W, Q_POOL_W, Q_ROWS = 0, 6, 8, 10, 12, 14, 16, 80, 592


def _dot(a, b, dims):
    return lax.dot_general(a.astype(MXU_DTYPE), b.astype(MXU_DTYPE), (dims, ((), ())),
                           preferred_element_type=F32)


def _params(semantics=None, vmem_mib=48):
    return pltpu.CompilerParams(dimension_semantics=semantics, vmem_limit_bytes=vmem_mib * MIB)


def _sigmoid(v):
    return jax.nn.sigmoid(v)


def _silu_and_grad(v):
    s = _sigmoid(v)
    return v * s, s * (1.0 + v * (1.0 - s))


def _ln_stats(v):
    mu = jnp.mean(v, axis=-1, keepdims=True)
    vc = v - mu
    var = jnp.mean(vc * vc, axis=-1, keepdims=True)
    rstd = lax.rsqrt(var + LN_EPS)
    return vc * rstd, rstd


def _ln_bwd_rows(dxhat, xhat, rstd):
    m1 = jnp.mean(dxhat, axis=-1, keepdims=True)
    m2 = jnp.mean(dxhat * xhat, axis=-1, keepdims=True)
    return rstd * (dxhat - m1 - xhat * m2)


def _colsum(v):
    return jnp.sum(v, axis=0, keepdims=True)


def _mesh_pos():
    return lax.axis_index("x"), lax.axis_index("y"), lax.axis_index("c")


def _proj_in(xb, win_g, l):
    s = xb.shape[0]
    tm = min(s, 1024)

    def body(x_ref, w_ref, o_ref):
        o_ref[...] = _dot(x_ref[...], w_ref[...], NN)

    return pl.pallas_call(
        body, name=f"proj_in_{l}", grid=(NQ, s // tm),
        in_specs=[pl.BlockSpec((tm, D), lambda q, m: (m, 0)),
                  pl.BlockSpec((None, None, D, WQ), lambda q, m: (q, l, 0, 0))],
        out_specs=pl.BlockSpec((tm, WQ), lambda q, m: (m, q)),
        out_shape=jax.ShapeDtypeStruct((s, DZ), F32),
        compiler_params=_params(("arbitrary", "arbitrary")),
    )(xb, win_g)


def _proj_out_ln(ycat, wout_g, l, x, g, b):
    s = x.shape[0]
    tm = min(s, 512)

    def body(y_ref, w_ref, x_ref, g_ref, b_ref, xn_ref, xb_ref, xh_ref, rs_ref):
        y = _dot(y_ref[...], w_ref[...].reshape(DY, D), NN)
        xhat, rstd = _ln_stats(ALPHA * x_ref[...] + y)
        xn = xhat * g_ref[...] + b_ref[...]
        xn_ref[...] = xn
        xb_ref[...] = xn.astype(BF16)
        xh_ref[...] = xhat
        rs_ref[...] = rstd

    row = lambda m: (m, 0)
    fixed = lambda m: (0, 0)
    return pl.pallas_call(
        body, name=f"proj_out_ln_{l}", grid=(s // tm,),
        in_specs=[pl.BlockSpec((tm, DY), row),
                  pl.BlockSpec((NQ, None, RQ, D), lambda m: (0, l, 0, 0)),
                  pl.BlockSpec((tm, D), row), pl.BlockSpec((1, D), fixed), pl.BlockSpec((1, D), fixed)],
        out_specs=[pl.BlockSpec((tm, D), row), pl.BlockSpec((tm, D), row), pl.BlockSpec((tm, D), row),
                   pl.BlockSpec((tm, 1), row)],
        out_shape=[jax.ShapeDtypeStruct((s, D), F32), jax.ShapeDtypeStruct((s, D), BF16),
                   jax.ShapeDtypeStruct((s, D), F32), jax.ShapeDtypeStruct((s, 1), F32)],
        compiler_params=_params(("arbitrary",)),
    )(ycat, wout_g, x, g, b)


def _loss_grad(xl, target):
    s = xl.shape[0]
    ts = min(s, 512)

    def body(x_ref, t_ref, loss_ref, dx_ref):
        @pl.when(pl.program_id(0) == 0)
        def _():
            loss_ref[...] = jnp.zeros_like(loss_ref)
        err = x_ref[...] - t_ref[...]
        dx_ref[...] = err * (1.0 / D)
        loss_ref[...] += 0.5 * jnp.sum(jnp.mean(err * err, axis=-1, keepdims=True), axis=0, keepdims=True)

    row = lambda m: (m, 0)
    return pl.pallas_call(
        body, name="loss_grad", grid=(s // ts,),
        in_specs=[pl.BlockSpec((ts, D), row), pl.BlockSpec((ts, D), row)],
        out_specs=[pl.BlockSpec((1, 1), lambda m: (0, 0)), pl.BlockSpec((ts, D), row)],
        out_shape=[jax.ShapeDtypeStruct((1, 1), F32), jax.ShapeDtypeStruct((s, D), F32)],
        compiler_params=_params(("arbitrary",)),
    )(xl, target)


def _ln_bwd(dxn, xhat, rstd, g, l):
    s = dxn.shape[0]
    ts = min(s, 512)

    def body(d_ref, xh_ref, rs_ref, g_ref, dr_ref, drb_ref, dg_ref, db_ref):
        @pl.when(pl.program_id(0) == 0)
        def _():
            dg_ref[...] = jnp.zeros_like(dg_ref)
            db_ref[...] = jnp.zeros_like(db_ref)
        d = d_ref[...]
        xhat_v = xh_ref[...]
        dr = _ln_bwd_rows(d * g_ref[...], xhat_v, rs_ref[...])
        dr_ref[...] = dr
        drb_ref[...] = dr.astype(BF16)
        dg_ref[...] += _colsum(d * xhat_v)
        db_ref[...] += _colsum(d)

    row = lambda m: (m, 0)
    fixed = lambda m: (0, 0)
    return pl.pallas_call(
        body, name=f"ln_bwd_{l}", grid=(s // ts,),
        in_specs=[pl.BlockSpec((ts, D), row), pl.BlockSpec((ts, D), row), pl.BlockSpec((ts, 1), row),
                  pl.BlockSpec((1, D), fixed)],
        out_specs=[pl.BlockSpec((ts, D), row), pl.BlockSpec((ts, D), row), pl.BlockSpec((1, D), fixed),
                   pl.BlockSpec((1, D), fixed)],
        out_shape=[jax.ShapeDtypeStruct((s, D), F32), jax.ShapeDtypeStruct((s, D), BF16),
                   jax.ShapeDtypeStruct((1, D), F32), jax.ShapeDtypeStruct((1, D), F32)],
        compiler_params=_params(("arbitrary",)),
    )(dxn, xhat, rstd, g)


def _dycat(drb, wout_g, l):
    s = drb.shape[0]
    tm = min(s, 512)

    def body(d_ref, w_ref, o_ref):
        o_ref[...] = _dot(d_ref[...], w_ref[...].reshape(DY, D), NT)

    return pl.pallas_call(
        body, name=f"dycat_{l}", grid=(s // tm,),
        in_specs=[pl.BlockSpec((tm, D), lambda m: (m, 0)),
                  pl.BlockSpec((NQ, None, RQ, D), lambda m: (0, l, 0, 0))],
        out_specs=pl.BlockSpec((tm, DY), lambda m: (m, 0)),
        out_shape=jax.ShapeDtypeStruct((s, DY), F32),
        compiler_params=_params(("arbitrary",)),
    )(drb, wout_g)


def _dwout(ycat, drb, l):
    s = drb.shape[0]
    tk = min(s, 512)

    def body(y_ref, d_ref, o_ref):
        part = _dot(y_ref[...], d_ref[...], TN)

        @pl.when(pl.program_id(0) == 0)
        def _():
            o_ref[...] = part

        @pl.when(pl.program_id(0) > 0)
        def _():
            o_ref[...] += part

    return pl.pallas_call(
        body, name=f"dwout_{l}", grid=(s // tk,),
        in_specs=[pl.BlockSpec((tk, DY), lambda k: (k, 0)), pl.BlockSpec((tk, D), lambda k: (k, 0))],
        out_specs=pl.BlockSpec((DY, D), lambda k: (0, 0)),
        out_shape=jax.ShapeDtypeStruct((DY, D), F32),
        compiler_params=_params(("arbitrary",)),
    )(ycat, drb)


def _dwin(xb, dzb, l):
    s = xb.shape[0]
    tk = min(s, 1024)

    def body(x_ref, d_ref, o_ref):
        part = _dot(x_ref[...], d_ref[...], TN)

        @pl.when(pl.program_id(1) == 0)
        def _():
            o_ref[...] = part

        @pl.when(pl.program_id(1) > 0)
        def _():
            o_ref[...] += part

    return pl.pallas_call(
        body, name=f"dwin_{l}", grid=(NQ, s // tk),
        in_specs=[pl.BlockSpec((tk, D), lambda q, k: (k, 0)), pl.BlockSpec((tk, WQ), lambda q, k: (k, q))],
        out_specs=pl.BlockSpec((None, D, WQ), lambda q, k: (q, 0, 0)),
        out_shape=jax.ShapeDtypeStruct((NQ, D, WQ), F32),
        compiler_params=_params(("arbitrary", "arbitrary")),
    )(xb, dzb)


def _dx(dzb, win_g, l, dr):
    s = dzb.shape[0]
    tm = min(s, 1024)

    def body(d_ref, w_ref, r_ref, o_ref):
        part = _dot(d_ref[...], w_ref[...], NT)

        @pl.when(pl.program_id(1) == 0)
        def _():
            o_ref[...] = ALPHA * r_ref[...] + part

        @pl.when(pl.program_id(1) > 0)
        def _():
            o_ref[...] += part

    return pl.pallas_call(
        body, name=f"dx_{l}", grid=(s // tm, NQ),
        in_specs=[pl.BlockSpec((tm, WQ), lambda m, q: (m, q)),
                  pl.BlockSpec((None, None, D, WQ), lambda m, q: (q, l, 0, 0)),
                  pl.BlockSpec((tm, D), lambda m, q: (m, 0))],
        out_specs=pl.BlockSpec((tm, D), lambda m, q: (m, 0)),
        out_shape=jax.ShapeDtypeStruct((s, D), F32),
        compiler_params=_params(("arbitrary", "arbitrary")),
    )(dzb, win_g, dr)


def _window_sums(e, causal):
    n = e.shape[0]

    def shifted(a, k):
        return pltpu.roll(a, k if causal else n - k, axis=0)

    parts = []
    acc = e
    for step, k in enumerate((1, 2, 4, 8)):
        acc = acc + shifted(acc, k)
        parts.append(acc[:, 0:HEAD])
        if step < 3:
            acc = acc[:, HEAD:]
    return jnp.concatenate(parts, axis=1)


def _pool_counts(first_pos, rows):
    t1 = (lax.broadcasted_iota(jnp.int32, (rows, 1), 0) + first_pos + 1).astype(F32)
    lane = lax.broadcasted_iota(jnp.int32, (1, D), 1)
    win = jnp.where(lane < HEAD, 2.0, jnp.where(lane < 2 * HEAD, 4.0, jnp.where(lane < 3 * HEAD, 8.0, 16.0)))
    return jnp.minimum(t1, win)


def _group_dot(v, w_ref, dims):
    return jnp.concatenate(
        [_dot(v[:, g * HEAD:(g + 1) * HEAD], w_ref[g], dims) for g in range(4)], axis=1)


def _prev_index(ts, halo):
    return lambda i: (jnp.maximum(i * (ts // halo) - 1, 0), 0)


def _next_index(ts, halo, s):
    return lambda i: (jnp.minimum((i + 1) * (ts // halo), s // halo - 1), 0)


def _even_fwd(z, pool_w, pool_scale, sconv_w, sconv_b, l):
    s = z.shape[0]
    ts = min(s, 256)
    h = POOL_HALO

    def body(z_ref, zp_ref, pw_ref, ps_ref, cw_ref, cb_ref, o_ref):
        i = pl.program_id(0)
        inside = i > 0
        xa = z_ref[:, 0:D]
        xa_ext = jnp.concatenate([jnp.where(inside, zp_ref[:, 0:D], 0.0), xa], axis=0)
        sums = _window_sums(xa_ext, True)[h:]
        pooled = sums / _pool_counts(i * ts, ts) - xa
        p = _group_dot(pooled, pw_ref, NN)
        silu_ga, _ = _silu_and_grad(z_ref[:, D:2 * D])
        o_ref[:, 0:D] = (p * ps_ref[...] * silu_ga).astype(BF16)

        q_main = z_ref[:, 4 * D:5 * D] * z_ref[:, 2 * D:3 * D]
        q_prev = jnp.where(inside, zp_ref[:, 4 * D:5 * D] * zp_ref[:, 2 * D:3 * D], 0.0)
        q_ext = jnp.concatenate([q_prev, q_main], axis=0)
        cv = cw_ref[2:3, :] * q_main + cb_ref[...]
        cv = cv + cw_ref[1:2, :] * pltpu.roll(q_ext, 1, axis=0)[h:]
        cv = cv + cw_ref[0:1, :] * pltpu.roll(q_ext, 2, axis=0)[h:]
        silu_gb, _ = _silu_and_grad(z_ref[:, 5 * D:6 * D])
        o_ref[:, D:2 * D] = (z_ref[:, 3 * D:4 * D] * cv * silu_gb).astype(BF16)

    fixed2 = lambda i: (0, 0)
    return pl.pallas_call(
        body, name=f"even_fwd_{l}", grid=(s // ts,),
        in_specs=[pl.BlockSpec((ts, DZ), lambda i: (i, 0)), pl.BlockSpec((h, DZ), _prev_index(ts, h)),
                  pl.BlockSpec((4, HEAD, HEAD), lambda i: (0, 0, 0)), pl.BlockSpec((1, D), fixed2),
                  pl.BlockSpec((SHORT_K, D), fixed2), pl.BlockSpec((1, D), fixed2)],
        out_specs=pl.BlockSpec((ts, DY), lambda i: (i, 0)),
        out_shape=jax.ShapeDtypeStruct((s, DY), BF16),
        compiler_params=_params(("arbitrary",)),
    )(z, z, pool_w, pool_scale, sconv_w, sconv_b)


def _even_bwd(z, dy, pool_w, pool_wt, pool_scale, sconv_w, sconv_b, l):
    s = z.shape[0]
    ts = min(s, 256)
    h = POOL_HALO
    n_tiles = s // ts

    def body(z_ref, zp_ref, zn_ref, dy_ref, dyn_ref, pw_ref, pwt_ref, ps_ref, cw_ref, cb_ref,
             dz_ref, dpw_ref, dps_ref, dcw_ref, dcb_ref):
        i = pl.program_id(0)
        inside = i > 0
        more = i < n_tiles - 1

        @pl.when(i == 0)
        def _():
            dpw_ref[...] = jnp.zeros_like(dpw_ref)
            dps_ref[...] = jnp.zeros_like(dps_ref)
            dcw_ref[...] = jnp.zeros_like(dcw_ref)
            dcb_ref[...] = jnp.zeros_like(dcb_ref)

        def with_next(main_ref, next_ref, lo):
            return jnp.concatenate([main_ref[:, lo:lo + D], next_ref[:, lo:lo + D]], axis=0)

        xa_ext = jnp.concatenate(
            [jnp.where(inside, zp_ref[:, 0:D], 0.0), z_ref[:, 0:D], zn_ref[:, 0:D]], axis=0)
        counts = _pool_counts(i * ts, ts + h)
        pooled = _window_sums(xa_ext, True)[h:] / counts - xa_ext[h:]
        p = _group_dot(pooled, pw_ref, NN)
        silu_ga, dsilu_ga = _silu_and_grad(with_next(z_ref, zn_ref, D))
        d_ya = with_next(dy_ref, dyn_ref, 0)
        scale = ps_ref[...]
        d_p = d_ya * scale * silu_ga
        d_pooled = _group_dot(d_p, pwt_ref, NN)
        row = lax.broadcasted_iota(jnp.int32, (ts + h, 1), 0)
        d_pooled = jnp.where(jnp.logical_or(more, row < ts), d_pooled, 0.0)
        d_xa = _window_sums(d_pooled / counts, False)[:ts] - d_pooled[:ts]
        dz_ref[:, 0:D] = d_xa.astype(BF16)
        dz_ref[:, D:2 * D] = (d_ya[:ts] * p[:ts] * scale * dsilu_ga[:ts]).astype(BF16)
        dps_ref[...] += _colsum(d_ya[:ts] * p[:ts] * silu_ga[:ts])
        for g in range(4):
            cols = slice(g * HEAD, (g + 1) * HEAD)
            dpw_ref[g] += _dot(pooled[:ts, cols], d_p[:ts, cols], TN)

        cg = z_ref[:, 4 * D:5 * D]
        hh = z_ref[:, 2 * D:3 * D]
        bg = z_ref[:, 3 * D:4 * D]
        q_main = cg * hh
        q_prev = jnp.where(inside, zp_ref[:, 4 * D:5 * D] * zp_ref[:, 2 * D:3 * D], 0.0)
        q_ext = jnp.concatenate([q_prev, q_main], axis=0)
        q_1 = pltpu.roll(q_ext, 1, axis=0)[h:]
        q_2 = pltpu.roll(q_ext, 2, axis=0)[h:]
        cv = cw_ref[2:3, :] * q_main + cw_ref[1:2, :] * q_1 + cw_ref[0:1, :] * q_2 + cb_ref[...]
        silu_gb, dsilu_gb = _silu_and_grad(with_next(z_ref, zn_ref, 5 * D))
        d_yb = with_next(dy_ref, dyn_ref, D)
        d_cv = d_yb * with_next(z_ref, zn_ref, 3 * D) * silu_gb
        d_cv = jnp.where(jnp.logical_or(more, row < ts), d_cv, 0.0)
        d_cv0 = d_cv[:ts]
        n_ext = ts + h
        d_q = (cw_ref[2:3, :] * d_cv0 + cw_ref[1:2, :] * pltpu.roll(d_cv, n_ext - 1, axis=0)[:ts]
               + cw_ref[0:1, :] * pltpu.roll(d_cv, n_ext - 2, axis=0)[:ts])
        dz_ref[:, 2 * D:3 * D] = (d_q * cg).astype(BF16)
        dz_ref[:, 3 * D:4 * D] = (d_yb[:ts] * cv * silu_gb[:ts]).astype(BF16)
        dz_ref[:, 4 * D:5 * D] = (d_q * hh).astype(BF16)
        dz_ref[:, 5 * D:6 * D] = (d_yb[:ts] * bg * cv * dsilu_gb[:ts]).astype(BF16)
        dcb_ref[...] += _colsum(d_cv0)
        dcw_ref[2:3, :] += _colsum(d_cv0 * q_main)
        dcw_ref[1:2, :] += _colsum(d_cv0 * q_1)
        dcw_ref[0:1, :] += _colsum(d_cv0 * q_2)

    fixed2 = lambda i: (0, 0)
    fixed3 = lambda i: (0, 0, 0)
    return pl.pallas_call(
        body, name=f"even_bwd_{l}", grid=(n_tiles,),
        in_specs=[pl.BlockSpec((ts, DZ), lambda i: (i, 0)), pl.BlockSpec((h, DZ), _prev_index(ts, h)),
                  pl.BlockSpec((h, DZ), _next_index(ts, h, s)),
                  pl.BlockSpec((ts, DY), lambda i: (i, 0)), pl.BlockSpec((h, DY), _next_index(ts, h, s)),
                  pl.BlockSpec((4, HEAD, HEAD), fixed3), pl.BlockSpec((4, HEAD, HEAD), fixed3),
                  pl.BlockSpec((1, D), fixed2), pl.BlockSpec((SHORT_K, D), fixed2), pl.BlockSpec((1, D), fixed2)],
        out_specs=[pl.BlockSpec((ts, DZ), lambda i: (i, 0)), pl.BlockSpec((4, HEAD, HEAD), fixed3),
                   pl.BlockSpec((1, D), fixed2), pl.BlockSpec((SHORT_K, D), fixed2), pl.BlockSpec((1, D), fixed2)],
        out_shape=[jax.ShapeDtypeStruct((s, DZ), BF16), jax.ShapeDtypeStruct((4, HEAD, HEAD), F32),
                   jax.ShapeDtypeStruct((1, D), F32), jax.ShapeDtypeStruct((SHORT_K, D), F32),
                   jax.ShapeDtypeStruct((1, D), F32)],
        compiler_params=_params(("arbitrary",), 56),
    )(z, z, z, dy, dy, pool_w, pool_wt, pool_scale, sconv_w, sconv_b)


def _conv31(src_ref, w_ref, dst_ref, base, rows, causal):
    def block(cb, carry):
        cols = pl.ds(pl.multiple_of(cb * LANE, LANE), LANE)
        acc = None
        for d in range(CONV_K):
            start = base - d if causal else base + d
            term = w_ref[CONV_K - 1 - d:CONV_K - d, cols] * src_ref[pl.ds(start, rows), cols]
            acc = term if acc is None else acc + term
        dst_ref[pl.ds(0, rows), cols] = acc
        return carry

    lax.fori_loop(0, D // LANE, block, 0)


def _sgu_bias_rows(sgu_b):
    return jnp.repeat(jnp.transpose(sgu_b), HEAD, axis=1)


def _odd_fwd(z, sln_g, sln_b, ws, sbias, dconv_w, dconv_b, dn_g, dn_b, l):
    s = z.shape[0]
    ts = min(s, 256)
    h = CONV_HALO

    def body(z_ref, zp_ref, lg_ref, lb_ref, ws_ref, sb_ref, cw_ref, cb_ref, ng_ref, nb_ref, o_ref, zz_ref, zc_ref):
        i = pl.program_id(0)
        vhat, _ = _ln_stats(z_ref[:, D:2 * D])
        vn = (vhat * lg_ref[...] + lb_ref[...]).astype(MXU_DTYPE)
        silu_gc, _ = _silu_and_grad(z_ref[:, 2 * D:3 * D])
        for n in range(ts // SGU_BLOCK):
            rows = slice(n * SGU_BLOCK, (n + 1) * SGU_BLOCK)
            sv = jnp.concatenate(
                [_dot(ws_ref[hd], vn[rows, hd * HEAD:(hd + 1) * HEAD], NN) for hd in range(4)], axis=1)
            sv = sv + sb_ref[...]
            o_ref[rows, 0:D] = (z_ref[rows, 0:D] * sv * silu_gc[rows]).astype(BF16)

        zz_ref[0:h, :] = jnp.where(i > 0, zp_ref[:, 3 * D:4 * D] * _sigmoid(zp_ref[:, 4 * D:5 * D]), 0.0)
        zz_ref[h:h + ts, :] = z_ref[:, 3 * D:4 * D] * _sigmoid(z_ref[:, 4 * D:5 * D])
        _conv31(zz_ref, cw_ref, zc_ref, h, ts, True)
        zhat, _ = _ln_stats(zc_ref[...] + cb_ref[...])
        silu_zn, _ = _silu_and_grad(zhat * ng_ref[...] + nb_ref[...])
        silu_gd, _ = _silu_and_grad(z_ref[:, 5 * D:6 * D])
        o_ref[:, D:2 * D] = (silu_zn * silu_gd).astype(BF16)

    fixed2 = lambda i: (0, 0)
    vec = pl.BlockSpec((1, D), fixed2)
    return pl.pallas_call(
        body, name=f"odd_fwd_{l}", grid=(s // ts,),
        in_specs=[pl.BlockSpec((ts, DZ), lambda i: (i, 0)), pl.BlockSpec((h, DZ), _prev_index(ts, h)),
                  vec, vec, pl.BlockSpec((4, SGU_BLOCK, SGU_BLOCK), lambda i: (0, 0, 0)),
                  pl.BlockSpec((SGU_BLOCK, D), fixed2), pl.BlockSpec((CONV_K, D), fixed2), vec, vec, vec],
        out_specs=pl.BlockSpec((ts, DY), lambda i: (i, 0)),
        out_shape=jax.ShapeDtypeStruct((s, DY), BF16),
        scratch_shapes=[pltpu.VMEM((h + ts, D), F32), pltpu.VMEM((ts, D), F32)],
        compiler_params=_params(("arbitrary",)),
    )(z, z, sln_g, sln_b, ws, sbias, dconv_w, dconv_b, dn_g, dn_b)


def _odd_bwd(z, dy, sln_g, sln_b, ws, wst, sbias, dconv_w, dconv_b, dn_g, dn_b, l):
    s = z.shape[0]
    ts = min(s, 256)
    h = CONV_HALO
    n_tiles = s // ts
    te = ts + h

    def body(z_ref, zp_ref, zn_ref, dy_ref, dyn_ref, lg_ref, lb_ref, ws_ref, wst_ref, sb_ref, cw_ref, cb_ref,
             ng_ref, nb_ref, dz_ref, dlg_ref, dlb_ref, dws_ref, dsb_ref, dcw_ref, dcb_ref, dng_ref, dnb_ref,
             zz_ref, zc_ref, dzc_ref, dzz_ref):
        i = pl.program_id(0)
        more = i < n_tiles - 1

        @pl.when(i == 0)
        def _():
            for ref in (dlg_ref, dlb_ref, dws_ref, dsb_ref, dcw_ref, dcb_ref, dng_ref, dnb_ref):
                ref[...] = jnp.zeros_like(ref)

        vhat, rstd_v = _ln_stats(z_ref[:, D:2 * D])
        lg = lg_ref[...]
        vn = (vhat * lg + lb_ref[...]).astype(MXU_DTYPE)
        u = z_ref[:, 0:D]
        silu_gc, dsilu_gc = _silu_and_grad(z_ref[:, 2 * D:3 * D])
        d_yc = dy_ref[:, 0:D]
        d_yc_u = d_yc * u
        d_sv = d_yc_u * silu_gc
        d_svb = d_sv.astype(MXU_DTYPE)
        sv_rows = []
        dvn_rows = []
        dsb = None
        for n in range(ts // SGU_BLOCK):
            rows = slice(n * SGU_BLOCK, (n + 1) * SGU_BLOCK)
            sv_parts = []
            dvn_parts = []
            for hd in range(4):
                cols = slice(hd * HEAD, (hd + 1) * HEAD)
                sv_parts.append(_dot(ws_ref[hd], vn[rows, cols], NN))
                dvn_parts.append(_dot(wst_ref[hd], d_svb[rows, cols], NN))
                dws_ref[hd] += _dot(d_svb[rows, cols], vn[rows, cols], NT)
            sv_rows.append(jnp.concatenate(sv_parts, axis=1) + sb_ref[...])
            dvn_rows.append(jnp.concatenate(dvn_parts, axis=1))
            dsb = d_sv[rows] if dsb is None else dsb + d_sv[rows]
        dsb_ref[...] += dsb
        sv = jnp.concatenate(sv_rows, axis=0)
        d_vn = jnp.concatenate(dvn_rows, axis=0)
        dz_ref[:, 0:D] = (d_yc * sv * silu_gc).astype(BF16)
        dz_ref[:, D:2 * D] = _ln_bwd_rows(d_vn * lg, vhat, rstd_v).astype(BF16)
        dz_ref[:, 2 * D:3 * D] = (d_yc_u * sv * dsilu_gc).astype(BF16)
        dlg_ref[...] += _colsum(d_vn * vhat)
        dlb_ref[...] += _colsum(d_vn)

        def gate(ref):
            return ref[:, 3 * D:4 * D] * _sigmoid(ref[:, 4 * D:5 * D])

        zz_ref[0:h, :] = jnp.where(i > 0, gate(zp_ref), 0.0)
        zz_ref[h:h + ts, :] = gate(z_ref)
        zz_ref[h + ts:h + te, :] = gate(zn_ref)
        _conv31(zz_ref, cw_ref, zc_ref, h, te, True)
        zhat, rstd_z = _ln_stats(zc_ref[...] + cb_ref[...])
        ng = ng_ref[...]
        silu_zn, dsilu_zn = _silu_and_grad(zhat * ng + nb_ref[...])
        gd = jnp.concatenate([z_ref[:, 5 * D:6 * D], zn_ref[:, 5 * D:6 * D]], axis=0)
        silu_gd, dsilu_gd = _silu_and_grad(gd)
        d_yd = jnp.concatenate([dy_ref[:, D:2 * D], dyn_ref[:, D:2 * D]], axis=0)
        d_zn = d_yd * silu_gd * dsilu_zn
        d_zc = _ln_bwd_rows(d_zn * ng, zhat, rstd_z)
        row = lax.broadcasted_iota(jnp.int32, (te, 1), 0)
        d_zc = jnp.where(jnp.logical_or(more, row < ts), d_zc, 0.0)
        dzc_ref[...] = d_zc
        dz_ref[:, 5 * D:6 * D] = (d_yd[:ts] * silu_zn[:ts] * dsilu_gd[:ts]).astype(BF16)
        dng_ref[...] += _colsum(d_zn[:ts] * zhat[:ts])
        dnb_ref[...] += _colsum(d_zn[:ts])
        dcb_ref[...] += _colsum(d_zc[:ts])
        _conv31(dzc_ref, cw_ref, dzz_ref, 0, ts, False)

        def tap_grads(cb, carry):
            cols = pl.ds(pl.multiple_of(cb * LANE, LANE), LANE)
            d_blk = dzc_ref[pl.ds(0, ts), cols]
            for d in range(CONV_K):
                j = CONV_K - 1 - d
                dcw_ref[j:j + 1, cols] += _colsum(d_blk * zz_ref[pl.ds(h - d, ts), cols])
            return carry

        lax.fori_loop(0, D // LANE, tap_grads, 0)
        d_zz = dzz_ref[...]
        a = z_ref[:, 3 * D:4 * D]
        sig_b = _sigmoid(z_ref[:, 4 * D:5 * D])
        dz_ref[:, 3 * D:4 * D] = (d_zz * sig_b).astype(BF16)
        dz_ref[:, 4 * D:5 * D] = (d_zz * a * sig_b * (1.0 - sig_b)).astype(BF16)

    fixed2 = lambda i: (0, 0)
    fixed3 = lambda i: (0, 0, 0)
    vec = pl.BlockSpec((1, D), fixed2)
    mat = pl.BlockSpec((4, SGU_BLOCK, SGU_BLOCK), fixed3)
    vec_shape = jax.ShapeDtypeStruct((1, D), F32)
    return pl.pallas_call(
        body, name=f"odd_bwd_{l}", grid=(n_tiles,),
        in_specs=[pl.BlockSpec((ts, DZ), lambda i: (i, 0)), pl.BlockSpec((h, DZ), _prev_index(ts, h)),
                  pl.BlockSpec((h, DZ), _next_index(ts, h, s)),
                  pl.BlockSpec((ts, DY), lambda i: (i, 0)), pl.BlockSpec((h, DY), _next_index(ts, h, s)),
                  vec, vec, mat, mat, pl.BlockSpec((SGU_BLOCK, D), fixed2),
                  pl.BlockSpec((CONV_K, D), fixed2), vec, vec, vec],
        out_specs=[pl.BlockSpec((ts, DZ), lambda i: (i, 0)), vec, vec, mat, pl.BlockSpec((SGU_BLOCK, D), fixed2),
                   pl.BlockSpec((CONV_K, D), fixed2), vec, vec, vec],
        out_shape=[jax.ShapeDtypeStruct((s, DZ), BF16), vec_shape, vec_shape,
                   jax.ShapeDtypeStruct((4, SGU_BLOCK, SGU_BLOCK), F32), jax.ShapeDtypeStruct((SGU_BLOCK, D), F32),
                   jax.ShapeDtypeStruct((CONV_K, D), F32), vec_shape, vec_shape, vec_shape],
        scratch_shapes=[pltpu.VMEM((h + te, D), F32), pltpu.VMEM((te, D), F32), pltpu.VMEM((te, D), F32),
                        pltpu.VMEM((ts, D), F32)],
        compiler_params=_params(("arbitrary",), 56),
    )(z, z, z, dy, dy, sln_g, sln_b, ws, wst, sbias, dconv_w, dconv_b, dn_g, dn_b)


def _remote(src, dst, send_sems, recv_sems, k, to):
    return pltpu.make_async_remote_copy(src_ref=src, dst_ref=dst, send_sem=send_sems.at[k],
                                        recv_sem=recv_sems.at[k], device_id=to, device_id_type=MESH)


def _other_chips(x, y):
    return [(1 - x, y, 2 * (1 - x) + y), (x, 1 - y, 2 * x + 1 - y), (1 - x, 1 - y, 2 * (1 - x) + 1 - y)]


def _gather_weights(win_sh, wout_sh, small_sh):
    hi, ho = D // 2, RQ // 2

    def body(win, wout, small, win_g, wout_g, small_g, send_sems, recv_sems, local_sems):
        x, y, c = _mesh_pos()
        me = 2 * x + y
        sibling = (x, y, 1 - c)
        chips = _other_chips(x, y)

        def half_in(ref, half):
            return ref.at[:, pl.ds(half * hi, hi), :]

        def half_out(ref, half):
            return ref.at[:, pl.ds(half * ho, ho), :]

        local = [pltpu.make_async_copy(win, win_g.at[me], local_sems.at[0]),
                 pltpu.make_async_copy(wout, wout_g.at[me], local_sems.at[1]),
                 pltpu.make_async_copy(small, small_g.at[me], local_sems.at[2])]
        for cp in local:
            cp.start()
        sends = []
        for j, (cx, cy, _) in enumerate(chips):
            to = (cx, cy, c)
            sends.append(_remote(half_in(win, c), half_in(win_g.at[me], c), send_sems, recv_sems, j, to))
            sends.append(_remote(half_out(wout, c), half_out(wout_g.at[me], c), send_sems, recv_sems, 3 + j, to))
            sends.append(_remote(small, small_g.at[me], send_sems, recv_sems, 6 + j, to))
        for cp in sends:
            cp.start()
        passed = []
        for j, (_, _, q) in enumerate(chips):
            got_in = half_in(win_g.at[q], c)
            got_out = half_out(wout_g.at[q], c)
            _remote(got_in, got_in, send_sems, recv_sems, j, sibling).wait_recv()
            cp = _remote(got_in, got_in, send_sems, recv_sems, 9 + j, sibling)
            cp.start()
            passed.append(cp)
            _remote(got_out, got_out, send_sems, recv_sems, 3 + j, sibling).wait_recv()
            cp = _remote(got_out, got_out, send_sems, recv_sems, 12 + j, sibling)
            cp.start()
            passed.append(cp)
            _remote(small, small_g.at[q], send_sems, recv_sems, 6 + j, sibling).wait_recv()
        for j, (_, _, q) in enumerate(chips):
            from_in = half_in(win_g.at[q], 1 - c)
            from_out = half_out(wout_g.at[q], 1 - c)
            _remote(from_in, from_in, send_sems, recv_sems, 9 + j, sibling).wait_recv()
            _remote(from_out, from_out, send_sems, recv_sems, 12 + j, sibling).wait_recv()
        for cp in sends + passed:
            cp.wait_send()
        for cp in local:
            cp.wait()

    return pl.pallas_call(
        body, name="gather_weights",
        in_specs=[HBM_SPEC, HBM_SPEC, HBM_SPEC], out_specs=[HBM_SPEC, HBM_SPEC, HBM_SPEC],
        out_shape=[jax.ShapeDtypeStruct((NQ,) + win_sh.shape, win_sh.dtype),
                   jax.ShapeDtypeStruct((NQ,) + wout_sh.shape, wout_sh.dtype),
                   jax.ShapeDtypeStruct((NQ,) + small_sh.shape, small_sh.dtype)],
        scratch_shapes=[pltpu.SemaphoreType.DMA((15,)), pltpu.SemaphoreType.DMA((15,)),
                        pltpu.SemaphoreType.DMA((3,))],
    )(win_sh, wout_sh, small_sh)


def _allreduce_small(part):
    rows = part.shape[0]

    def body(p_ref, o_ref, all_ref, send_sems, recv_sems):
        x, y, c = _mesh_pos()
        me = 4 * x + 2 * y + c
        sibling = (x, y, 1 - c)
        chips = _other_chips(x, y)

        def slot(dev):
            return all_ref.at[dev]

        all_ref[me] = p_ref[...]
        sends = [_remote(p_ref, slot(me), send_sems, recv_sems, 0, sibling)]
        for j, (cx, cy, _) in enumerate(chips):
            sends.append(_remote(p_ref, slot(me), send_sems, recv_sems, 1 + j, (cx, cy, c)))
        for cp in sends:
            cp.start()
        passed = []
        for j, (cx, cy, _) in enumerate(chips):
            got = slot(4 * cx + 2 * cy + c)
            _remote(got, got, send_sems, recv_sems, 1 + j, sibling).wait_recv()
            cp = _remote(got, got, send_sems, recv_sems, 4 + j, sibling)
            cp.start()
            passed.append(cp)
        got = slot(4 * x + 2 * y + 1 - c)
        _remote(got, got, send_sems, recv_sems, 0, sibling).wait_recv()
        for j, (cx, cy, _) in enumerate(chips):
            got = slot(4 * cx + 2 * cy + 1 - c)
            _remote(got, got, send_sems, recv_sems, 4 + j, sibling).wait_recv()
        for cp in sends + passed:
            cp.wait_send()
        total = all_ref[0]
        for dev in range(1, 8):
            total = total + all_ref[dev]
        o_ref[...] = total

    return pl.pallas_call(
        body, name="allreduce_small",
        in_specs=[pl.BlockSpec(memory_space=pltpu.VMEM)], out_specs=pl.BlockSpec(memory_space=pltpu.VMEM),
        out_shape=jax.ShapeDtypeStruct(part.shape, F32),
        scratch_shapes=[pltpu.VMEM((8, rows, D), F32), pltpu.SemaphoreType.DMA((7,)),
                        pltpu.SemaphoreType.DMA((7,))],
        compiler_params=_params(None, 56),
    )(part)


def _pair_exchange(gin, gout):
    hi, ho = D // 2, RQ // 2

    def body(gin_ref, gout_ref, rin_ref, rout_ref, send_sems, recv_sems):
        x, y, c = _mesh_pos()
        sibling = (x, y, 1 - c)
        a = _remote(gin_ref.at[:, :, pl.ds((1 - c) * hi, hi), :], rin_ref, send_sems, recv_sems, 0, sibling)
        b = _remote(gout_ref.at[:, :, pl.ds((1 - c) * ho, ho), :], rout_ref, send_sems, recv_sems, 1, sibling)
        a.start()
        b.start()
        a.wait()
        b.wait()

    return pl.pallas_call(
        body, name="pair_exchange",
        in_specs=[HBM_SPEC, HBM_SPEC], out_specs=[HBM_SPEC, HBM_SPEC],
        out_shape=[jax.ShapeDtypeStruct((NQ, NL, hi, WQ), F32), jax.ShapeDtypeStruct((NQ, NL, ho, D), F32)],
        scratch_shapes=[pltpu.SemaphoreType.DMA((2,)), pltpu.SemaphoreType.DMA((2,))],
    )(gin, gout)


def _pair_sum(g, r, c_arr, name):
    nq, nl, rows, cols = r.shape
    tr = min(rows, 256)
    nt = rows // tr

    def body(c_ref, g_ref, r_ref, o_ref):
        o_ref[...] = g_ref[...] + r_ref[...]

    blk = (None, None, tr, cols)
    grid_spec = pltpu.PrefetchScalarGridSpec(
        num_scalar_prefetch=1, grid=(nq, nl, nt),
        in_specs=[pl.BlockSpec(blk, lambda q, l, t, c: (q, l, c[0] * nt + t, 0)),
                  pl.BlockSpec(blk, lambda q, l, t, c: (q, l, t, 0))],
        out_specs=pl.BlockSpec(blk, lambda q, l, t, c: (q, l, t, 0)))
    return pl.pallas_call(
        body, name=name, grid_spec=grid_spec, out_shape=jax.ShapeDtypeStruct(r.shape, F32),
        compiler_params=_params(("arbitrary",) * 3),
    )(c_arr, g, r)


def _chip_exchange(pin, pout):
    def body(pin_ref, pout_ref, rin_ref, rout_ref, send_sems, recv_sems):
        x, y, c = _mesh_pos()
        copies = []
        for j, (cx, cy, q) in enumerate(_other_chips(x, y)):
            to = (cx, cy, c)
            copies.append(_remote(pin_ref.at[q], rin_ref.at[j], send_sems, recv_sems, j, to))
            copies.append(_remote(pout_ref.at[q], rout_ref.at[j], send_sems, recv_sems, 3 + j, to))
        for cp in copies:
            cp.start()
        for cp in copies:
            cp.wait()

    return pl.pallas_call(
        body, name="chip_exchange",
        in_specs=[HBM_SPEC, HBM_SPEC], out_specs=[HBM_SPEC, HBM_SPEC],
        out_shape=[jax.ShapeDtypeStruct((3,) + pin.shape[1:], F32), jax.ShapeDtypeStruct((3,) + pout.shape[1:], F32)],
        scratch_shapes=[pltpu.SemaphoreType.DMA((6,)), pltpu.SemaphoreType.DMA((6,))],
    )(pin, pout)


def _chip_sum(p, r, q_arr, name):
    _, nl, rows, cols = p.shape
    tr = min(rows, 256)

    def body(q_ref, p_ref, r0_ref, r1_ref, r2_ref, o_ref):
        o_ref[...] = ((p_ref[...] + r0_ref[...]) + r1_ref[...]) + r2_ref[...]

    blk = (None, None, tr, cols)

    def peer(j):
        return pl.BlockSpec(blk, lambda l, t, q: (j, l, t, 0))

    grid_spec = pltpu.PrefetchScalarGridSpec(
        num_scalar_prefetch=1, grid=(nl, rows // tr),
        in_specs=[pl.BlockSpec(blk, lambda l, t, q: (q[0], l, t, 0)), peer(0), peer(1), peer(2)],
        out_specs=pl.BlockSpec((None, tr, cols), lambda l, t, q: (l, t, 0)))
    return pl.pallas_call(
        body, name=name, grid_spec=grid_spec, out_shape=jax.ShapeDtypeStruct((nl, rows, cols), F32),
        compiler_params=_params(("arbitrary",) * 2),
    )(q_arr, p, r, r, r)


def _pair_share(fin, fout):
    hi, ho = D // 2, RQ // 2

    def body(fin_ref, fout_ref, gin_ref, gout_ref, send_sems, recv_sems, local_sems):
        x, y, c = _mesh_pos()
        sibling = (x, y, 1 - c)
        mine_in = gin_ref.at[:, pl.ds(c * hi, hi), :]
        mine_out = gout_ref.at[:, pl.ds(c * ho, ho), :]
        local = [pltpu.make_async_copy(fin_ref, mine_in, local_sems.at[0]),
                 pltpu.make_async_copy(fout_ref, mine_out, local_sems.at[1])]
        sends = [_remote(fin_ref, mine_in, send_sems, recv_sems, 0, sibling),
                 _remote(fout_ref, mine_out, send_sems, recv_sems, 1, sibling)]
        for cp in local + sends:
            cp.start()
        theirs_in = gin_ref.at[:, pl.ds((1 - c) * hi, hi), :]
        theirs_out = gout_ref.at[:, pl.ds((1 - c) * ho, ho), :]
        _remote(fin_ref, theirs_in, send_sems, recv_sems, 0, sibling).wait_recv()
        _remote(fout_ref, theirs_out, send_sems, recv_sems, 1, sibling).wait_recv()
        for cp in sends:
            cp.wait_send()
        for cp in local:
            cp.wait()

    return pl.pallas_call(
        body, name="pair_share",
        in_specs=[HBM_SPEC, HBM_SPEC], out_specs=[HBM_SPEC, HBM_SPEC],
        out_shape=[jax.ShapeDtypeStruct((NL, D, WQ), F32), jax.ShapeDtypeStruct((NL, RQ, D), F32)],
        scratch_shapes=[pltpu.SemaphoreType.DMA((2,)), pltpu.SemaphoreType.DMA((2,)),
                        pltpu.SemaphoreType.DMA((2,))],
    )(fin, fout)


def _adamw(w, g, m, v, name):
    shape = w.shape
    w2, g2, m2, v2 = (t.reshape(-1, shape[-1]) for t in (w, g, m, v))
    rows, cols = w2.shape
    tr = 256 if rows % 256 == 0 else rows

    def body(w_ref, g_ref, m_ref, v_ref, d_ref, mo_ref, vo_ref):
        gv = g_ref[...]
        m_new = ADAM_B1 * m_ref[...] + (1.0 - ADAM_B1) * gv
        v_new = ADAM_B2 * v_ref[...] + (1.0 - ADAM_B2) * (gv * gv)
        m_hat = m_new / (1.0 - ADAM_B1 ** ADAM_STEP)
        v_hat = v_new / (1.0 - ADAM_B2 ** ADAM_STEP)
        d_ref[...] = -ADAM_LR * (m_hat / (jnp.sqrt(v_hat) + ADAM_EPS) + ADAM_WD * w_ref[...])
        mo_ref[...] = m_new
        vo_ref[...] = v_new

    blk = pl.BlockSpec((tr, cols), lambda i: (i, 0))
    out = jax.ShapeDtypeStruct((rows, cols), F32)
    d, mo, vo = pl.pallas_call(
        body, name=name, grid=(rows // tr,), in_specs=[blk] * 4, out_specs=[blk] * 3, out_shape=[out] * 3,
        compiler_params=_params(("arbitrary",)),
    )(w2, g2, m2, v2)
    return d.reshape(shape), mo.reshape(shape), vo.reshape(shape)


def _layer_slot(l):
    return (l % 2) * 2 + l // 2


def kernel(x, ln_g, ln_b, w_in_even, w_out_even, pool_w, pool_scale, sconv_w, sconv_b, w_in_odd, w_out_odd, sgu_ln_g, sgu_ln_b, sgu_w, sgu_b, dconv_w, dconv_b, dnorm_g, dnorm_b, loss_target, m_ln_g, m_ln_b, m_w_in_even, m_w_out_even, m_pool_w, m_pool_scale, m_sconv_w, m_sconv_b, m_w_in_odd, m_w_out_odd, m_sgu_ln_g, m_sgu_ln_b, m_sgu_w, m_sgu_b, m_dconv_w, m_dconv_b, m_dnorm_g, m_dnorm_b, v_ln_g, v_ln_b, v_w_in_even, v_w_out_even, v_pool_w, v_pool_scale, v_sconv_w, v_sconv_b, v_w_in_odd, v_w_out_odd, v_sgu_ln_g, v_sgu_ln_b, v_sgu_w, v_sgu_b, v_dconv_w, v_dconv_b, v_dnorm_g, v_dnorm_b):
    weights = dict(ln_g=ln_g, ln_b=ln_b, w_in_even=w_in_even, w_out_even=w_out_even, pool_w=pool_w,
                   pool_scale=pool_scale, sconv_w=sconv_w, sconv_b=sconv_b, w_in_odd=w_in_odd, w_out_odd=w_out_odd,
                   sgu_ln_g=sgu_ln_g, sgu_ln_b=sgu_ln_b, sgu_w=sgu_w, sgu_b=sgu_b, dconv_w=dconv_w,
                   dconv_b=dconv_b, dnorm_g=dnorm_g, dnorm_b=dnorm_b)
    moments_m = dict(ln_g=m_ln_g, ln_b=m_ln_b, w_in_even=m_w_in_even, w_out_even=m_w_out_even, pool_w=m_pool_w,
                     pool_scale=m_pool_scale, sconv_w=m_sconv_w, sconv_b=m_sconv_b, w_in_odd=m_w_in_odd,
                     w_out_odd=m_w_out_odd, sgu_ln_g=m_sgu_ln_g, sgu_ln_b=m_sgu_ln_b, sgu_w=m_sgu_w, sgu_b=m_sgu_b,
                     dconv_w=m_dconv_w, dconv_b=m_dconv_b, dnorm_g=m_dnorm_g, dnorm_b=m_dnorm_b)
    moments_v = dict(ln_g=v_ln_g, ln_b=v_ln_b, w_in_even=v_w_in_even, w_out_even=v_w_out_even, pool_w=v_pool_w,
                     pool_scale=v_pool_scale, sconv_w=v_sconv_w, sconv_b=v_sconv_b, w_in_odd=v_w_in_odd,
                     w_out_odd=v_w_out_odd, sgu_ln_g=v_sgu_ln_g, sgu_ln_b=v_sgu_ln_b, sgu_w=v_sgu_w, sgu_b=v_sgu_b,
                     dconv_w=v_dconv_w, dconv_b=v_dconv_b, dnorm_g=v_dnorm_g, dnorm_b=v_dnorm_b)
    names = list(weights)

    xd, yd, cd = _mesh_pos()
    chip = 2 * xd + yd
    c_arr = jnp.reshape(cd, (1,)).astype(jnp.int32)
    q_arr = jnp.reshape(chip, (1,)).astype(jnp.int32)

    win_sh = jnp.concatenate([w_in_even, w_in_odd], axis=0).astype(BF16)
    wout_sh = jnp.concatenate([w_out_even, w_out_odd], axis=0).astype(BF16)
    small_sh = jnp.concatenate(
        [sconv_w.reshape(6, HEAD), sgu_ln_g, sgu_ln_b, dconv_b, dnorm_g, dnorm_b, dconv_w.reshape(62, HEAD),
         jnp.zeros((2, HEAD), F32), pool_w.reshape(512, HEAD)], axis=0)
    win_g, wout_g, small_g = _gather_weights(win_sh, wout_sh, small_sh)

    def full_rows(lo, n):
        return jnp.transpose(small_g[:, lo:lo + n], (1, 0, 2)).reshape(n, D)

    sconv_w_f = full_rows(Q_SCONV_W, 6).reshape(2, SHORT_K, D)
    sln_g_f = full_rows(Q_SLN_G, 2)
    sln_b_f = full_rows(Q_SLN_B, 2)
    dconv_b_f = full_rows(Q_DCONV_B, 2)
    dn_g_f = full_rows(Q_DN_G, 2)
    dn_b_f = full_rows(Q_DN_B, 2)
    dconv_w_f = full_rows(Q_DCONV_W, 62).reshape(2, CONV_K, D)
    pool_w_f = jnp.transpose(small_g[:, Q_POOL_W:].reshape(NQ, 2, 4, 64, HEAD), (1, 2, 0, 3, 4)).reshape(2, 4, HEAD, HEAD)
    pool_w_b = pool_w_f.astype(BF16)
    pool_wt_b = jnp.swapaxes(pool_w_f, 2, 3).astype(BF16)
    idx = jnp.arange(SGU_BLOCK)
    mask = (idx[None, :] // 64) <= (idx[:, None] // 64)
    ws_f = jnp.where(mask[None, None], sgu_w, 0.0)
    ws_b = ws_f.astype(BF16)
    wst_b = jnp.swapaxes(ws_f, 2, 3).astype(BF16)

    def row(a, i):
        return a[i:i + 1]

    x_f = x[0]
    x_b = x_f.astype(BF16)
    saved = []
    for l in range(NL):
        i, slot = l // 2, _layer_slot(l)
        z = _proj_in(x_b, win_g, slot)
        if l % 2 == 0:
            ycat = _even_fwd(z, pool_w_b[i], row(pool_scale, i), sconv_w_f[i], row(sconv_b, i), l)
        else:
            ycat = _odd_fwd(z, row(sln_g_f, i), row(sln_b_f, i), ws_b[i], _sgu_bias_rows(sgu_b[i]),
                            dconv_w_f[i], row(dconv_b_f, i), row(dn_g_f, i), row(dn_b_f, i), l)
        x_next, x_next_b, xhat, rstd = _proj_out_ln(ycat, wout_g, slot, x_f, row(ln_g, l), row(ln_b, l))
        saved.append((x_b, z, ycat, xhat, rstd))
        x_f, x_b = x_next, x_next_b

    loss_part, dxn = _loss_grad(x_f, loss_target[0])
    loss = lax.psum(loss_part[0, 0], ("x", "y", "c"))

    small = {}
    gin = [None] * NL
    gout = [None] * NL
    d_ln_g = [None] * NL
    d_ln_b = [None] * NL
    for l in reversed(range(NL)):
        i, slot = l // 2, _layer_slot(l)
        xin_b, z, ycat, xhat, rstd = saved[l]
        dr, dr_b, d_ln_g[l], d_ln_b[l] = _ln_bwd(dxn, xhat, rstd, row(ln_g, l), l)
        dy = _dycat(dr_b, wout_g, slot)
        gout[slot] = _dwout(ycat, dr_b, l).reshape(NQ, RQ, D)
        if l % 2 == 0:
            dz, d_pw, d_ps, d_cw, d_cb = _even_bwd(z, dy, pool_w_b[i], pool_wt_b[i], row(pool_scale, i),
                                                   sconv_w_f[i], row(sconv_b, i), l)
            small[("pool_w", i)] = d_pw
            small[("pool_scale", i)] = d_ps
            small[("sconv_w", i)] = d_cw
            small[("sconv_b", i)] = d_cb
        else:
            dz, d_lg, d_lb, d_ws, d_sb, d_cw, d_cb, d_ng, d_nb = _odd_bwd(
                z, dy, row(sln_g_f, i), row(sln_b_f, i), ws_b[i], wst_b[i], _sgu_bias_rows(sgu_b[i]),
                dconv_w_f[i], row(dconv_b_f, i), row(dn_g_f, i), row(dn_b_f, i), l)
            small[("sgu_ln_g", i)] = d_lg
            small[("sgu_ln_b", i)] = d_lb
            small[("sgu_w", i)] = jnp.where(mask[None], d_ws, 0.0)
            small[("sgu_b", i)] = jnp.transpose(jnp.sum(d_sb.reshape(SGU_BLOCK, 4, HEAD), axis=-1))
            small[("dconv_w", i)] = d_cw
            small[("dconv_b", i)] = d_cb
            small[("dnorm_g", i)] = d_ng
            small[("dnorm_b", i)] = d_nb
        gin[slot] = _dwin(xin_b, dz, l)
        dxn = _dx(dz, win_g, slot, dr)
    grad_x = dxn[None]

    def both(name):
        return [small[(name, 0)], small[(name, 1)]]

    packed = jnp.concatenate(
        d_ln_g + d_ln_b + both("pool_scale") + both("sconv_b") + both("sconv_w") + both("sgu_ln_g")
        + both("sgu_ln_b") + both("dconv_b") + both("dnorm_g") + both("dnorm_b") + both("dconv_w")
        + [jnp.stack(both("sgu_b")).reshape(1, D), jnp.stack(both("sgu_w")).reshape(128, D),
           jnp.stack(both("pool_w")).reshape(512, D), jnp.zeros((R_SMALL - R_POOL_W - 512, D), F32)], axis=0)
    total = _allreduce_small(packed)

    def mine(a):
        return lax.dynamic_slice_in_dim(a, chip * HEAD, HEAD, axis=a.ndim - 1)

    grads = {
        "ln_g": total[R_LN_G:R_LN_G + 4],
        "ln_b": total[R_LN_B:R_LN_B + 4],
        "pool_scale": total[R_PSCALE:R_PSCALE + 2],
        "sconv_b": total[R_SCONV_B:R_SCONV_B + 2],
        "sconv_w": mine(total[R_SCONV_W:R_SCONV_W + 6].reshape(2, SHORT_K, D)),
        "sgu_ln_g": mine(total[R_SLN_G:R_SLN_G + 2]),
        "sgu_ln_b": mine(total[R_SLN_B:R_SLN_B + 2]),
        "dconv_b": mine(total[R_DCONV_B:R_DCONV_B + 2]),
        "dnorm_g": mine(total[R_DN_G:R_DN_G + 2]),
        "dnorm_b": mine(total[R_DN_B:R_DN_B + 2]),
        "dconv_w": mine(total[R_DCONV_W:R_DCONV_W + 62].reshape(2, CONV_K, D)),
        "sgu_b": total[R_SGU_B:R_SGU_B + 1].reshape(2, 4, SGU_BLOCK),
        "sgu_w": total[R_SGU_W:R_SGU_W + 128].reshape(2, 4, SGU_BLOCK, SGU_BLOCK),
        "pool_w": lax.dynamic_slice_in_dim(
            total[R_POOL_W:R_POOL_W + 512].reshape(2, 4, HEAD, HEAD), chip * 64, 64, axis=2),
    }

    gin_all = jnp.stack(gin, axis=1)
    gout_all = jnp.stack(gout, axis=1)
    rin, rout = _pair_exchange(gin_all, gout_all)
    pin = _pair_sum(gin_all, rin, c_arr, "pair_sum_in")
    pout = _pair_sum(gout_all, rout, c_arr, "pair_sum_out")
    r2in, r2out = _chip_exchange(pin, pout)
    fin = _chip_sum(pin, r2in, q_arr, "chip_sum_in")
    fout = _chip_sum(pout, r2out, q_arr, "chip_sum_out")
    gs_in, gs_out = _pair_share(fin, fout)
    grads["w_in_even"], grads["w_in_odd"] = gs_in[0:2], gs_in[2:4]
    grads["w_out_even"], grads["w_out_odd"] = gs_out[0:2], gs_out[2:4]

    deltas, new_m, new_v = {}, {}, {}
    for name in names:
        deltas[name], new_m[name], new_v[name] = _adamw(
            weights[name], grads[name], moments_m[name], moments_v[name], f"adamw_{name}")

    return (loss, grad_x, *[grads[n] for n in names], *[deltas[n] for n in names],
            *[new_m[n] for n in names], *[new_v[n] for n in names])
```

```python
import jax
import jax.numpy as jnp
from jax import lax
from jax.experimental import pallas as pl
from jax.experimental.pallas import tpu as pltpu

F32 = jnp.float32
BF16 = jnp.bfloat16
MXU_DTYPE = BF16

D = 1024
DZ = 6144
DY = 2048
NQ = 4
WQ = DZ // NQ
RQ = DY // NQ
NL = 4
ALPHA = (2 * NL) ** 0.25
LN_EPS = 1e-5
CONV_K = 31
SHORT_K = 3
SGU_BLOCK = 128
HEAD = 256
POOL_HALO = 16
CONV_HALO = 32
LANE = 128
MIB = 1024 * 1024

ADAM_LR = 0.001
ADAM_B1 = 0.9
ADAM_B2 = 0.999
ADAM_EPS = 1e-08
ADAM_WD = 0.01
ADAM_STEP = 10

NN = ((1,), (0,))
NT = ((1,), (1,))
TN = ((0,), (0,))
MESH = pl.DeviceIdType.MESH
HBM_SPEC = pl.BlockSpec(memory_space=pltpu.HBM)
SEM_SPEC = pl.BlockSpec(memory_space=pltpu.SEMAPHORE)
SIDE_EFFECT = pltpu.SideEffectType.DATAFLOW_SIDE_EFFECTING

R_LN_G, R_LN_B, R_PSCALE, R_SCONV_B, R_SCONV_W = 0, 4, 8, 10, 12
R_SLN_G, R_SLN_B, R_DCONV_B, R_DN_G, R_DN_B, R_DCONV_W = 18, 20, 22, 24, 26, 28
R_SGU_B, R_VECTORS = 90, 104
Q_SCONV_W, Q_SLN_G, Q_SLN_B, Q_DCONV_B, Q_DN_G, Q_DN_B, Q_DCONV_W, Q_POOL_W, Q_ROWS = 0, 6, 8, 10, 12, 14, 16, 80, 592


def _dot(a, b, dims):
    return lax.dot_general(a.astype(MXU_DTYPE), b.astype(MXU_DTYPE), (dims, ((), ())),
                           preferred_element_type=F32)


def _params(semantics=None, vmem_mib=48):
    return pltpu.CompilerParams(dimension_semantics=semantics, vmem_limit_bytes=vmem_mib * MIB)


def _sigmoid(v):
    return jax.nn.sigmoid(v)


def _silu_and_grad(v):
    s = _sigmoid(v)
    return v * s, s * (1.0 + v * (1.0 - s))


def _ln_stats(v):
    mu = jnp.mean(v, axis=-1, keepdims=True)
    vc = v - mu
    var = jnp.mean(vc * vc, axis=-1, keepdims=True)
    rstd = lax.rsqrt(var + LN_EPS)
    return vc * rstd, rstd


def _ln_bwd_rows(dxhat, xhat, rstd):
    m1 = jnp.mean(dxhat, axis=-1, keepdims=True)
    m2 = jnp.mean(dxhat * xhat, axis=-1, keepdims=True)
    return rstd * (dxhat - m1 - xhat * m2)


def _colsum(v):
    return jnp.sum(v, axis=0, keepdims=True)


def _adamw_math(w, g, m, v):
    m_new = ADAM_B1 * m + (1.0 - ADAM_B1) * g
    v_new = ADAM_B2 * v + (1.0 - ADAM_B2) * (g * g)
    m_hat = m_new / (1.0 - ADAM_B1 ** ADAM_STEP)
    v_hat = v_new / (1.0 - ADAM_B2 ** ADAM_STEP)
    return -ADAM_LR * (m_hat / (jnp.sqrt(v_hat) + ADAM_EPS) + ADAM_WD * w), m_new, v_new


def _mesh_pos():
    return lax.axis_index("x"), lax.axis_index("y"), lax.axis_index("c")


def _after_spec(after):
    return [] if after is None else [pl.BlockSpec(memory_space=pl.ANY)]


def _after_args(after):
    return [] if after is None else [after]


def _proj_in(xb, win_g, k, l, after=None):
    s = xb.shape[0]
    tm = min(s, 1024)

    def body(x_ref, w_ref, *rest):
        rest[-1][...] = _dot(x_ref[...], w_ref[...].reshape(D, WQ), NN)

    return pl.pallas_call(
        body, name=f"proj_in_{l}", grid=(NQ, s // tm),
        in_specs=[pl.BlockSpec((tm, D), lambda q, m: (m, 0)),
                  pl.BlockSpec((None, 2, None, D // 2, WQ), lambda q, m: (q, 0, k, 0, 0))] + _after_spec(after),
        out_specs=pl.BlockSpec((tm, WQ), lambda q, m: (m, q)),
        out_shape=jax.ShapeDtypeStruct((s, DZ), F32),
        compiler_params=_params(("arbitrary", "arbitrary")),
    )(xb, win_g, *_after_args(after))


def _proj_out_ln(ycat, wout_g, k, l, x, g, b):
    s = x.shape[0]
    tm = min(s, 512)

    def body(y_ref, w_ref, x_ref, g_ref, b_ref, xn_ref, xb_ref, xh_ref, rs_ref):
        y = _dot(y_ref[...], w_ref[...].reshape(DY, D), NN)
        xhat, rstd = _ln_stats(ALPHA * x_ref[...] + y)
        xn = xhat * g_ref[...] + b_ref[...]
        xn_ref[...] = xn
        xb_ref[...] = xn.astype(BF16)
        xh_ref[...] = xhat
        rs_ref[...] = rstd

    row = lambda m: (m, 0)
    fixed = lambda m: (0, 0)
    return pl.pallas_call(
        body, name=f"proj_out_ln_{l}", grid=(s // tm,),
        in_specs=[pl.BlockSpec((tm, DY), row),
                  pl.BlockSpec((NQ, 2, None, RQ // 2, D), lambda m: (0, 0, k, 0, 0)),
                  pl.BlockSpec((tm, D), row), pl.BlockSpec((1, D), fixed), pl.BlockSpec((1, D), fixed)],
        out_specs=[pl.BlockSpec((tm, D), row), pl.BlockSpec((tm, D), row), pl.BlockSpec((tm, D), row),
                   pl.BlockSpec((tm, 1), row)],
        out_shape=[jax.ShapeDtypeStruct((s, D), F32), jax.ShapeDtypeStruct((s, D), BF16),
                   jax.ShapeDtypeStruct((s, D), F32), jax.ShapeDtypeStruct((s, 1), F32)],
        compiler_params=_params(("arbitrary",)),
    )(ycat, wout_g, x, g, b)


def _loss_grad(xl, target):
    s = xl.shape[0]
    ts = min(s, 512)

    def body(x_ref, t_ref, loss_ref, dx_ref):
        @pl.when(pl.program_id(0) == 0)
        def _():
            loss_ref[...] = jnp.zeros_like(loss_ref)
        err = x_ref[...] - t_ref[...]
        dx_ref[...] = err * (1.0 / D)
        loss_ref[...] += 0.5 * jnp.sum(jnp.mean(err * err, axis=-1, keepdims=True), axis=0, keepdims=True)

    row = lambda m: (m, 0)
    return pl.pallas_call(
        body, name="loss_grad", grid=(s // ts,),
        in_specs=[pl.BlockSpec((ts, D), row), pl.BlockSpec((ts, D), row)],
        out_specs=[pl.BlockSpec((1, 1), lambda m: (0, 0)), pl.BlockSpec((ts, D), row)],
        out_shape=[jax.ShapeDtypeStruct((1, 1), F32), jax.ShapeDtypeStruct((s, D), F32)],
        compiler_params=_params(("arbitrary",)),
    )(xl, target)


def _ln_bwd(dxn, xhat, rstd, g, l, after=None):
    s = dxn.shape[0]
    ts = min(s, 512)

    def body(d_ref, xh_ref, rs_ref, g_ref, *rest):
        dr_ref, drb_ref, dg_ref, db_ref = rest[-4:]

        @pl.when(pl.program_id(0) == 0)
        def _():
            dg_ref[...] = jnp.zeros_like(dg_ref)
            db_ref[...] = jnp.zeros_like(db_ref)
        d = d_ref[...]
        xhat_v = xh_ref[...]
        dr = _ln_bwd_rows(d * g_ref[...], xhat_v, rs_ref[...])
        dr_ref[...] = dr
        drb_ref[...] = dr.astype(BF16)
        dg_ref[...] += _colsum(d * xhat_v)
        db_ref[...] += _colsum(d)

    row = lambda m: (m, 0)
    fixed = lambda m: (0, 0)
    return pl.pallas_call(
        body, name=f"ln_bwd_{l}", grid=(s // ts,),
        in_specs=[pl.BlockSpec((ts, D), row), pl.BlockSpec((ts, D), row), pl.BlockSpec((ts, 1), row),
                  pl.BlockSpec((1, D), fixed)] + _after_spec(after),
        out_specs=[pl.BlockSpec((ts, D), row), pl.BlockSpec((ts, D), row), pl.BlockSpec((1, D), fixed),
                   pl.BlockSpec((1, D), fixed)],
        out_shape=[jax.ShapeDtypeStruct((s, D), F32), jax.ShapeDtypeStruct((s, D), BF16),
                   jax.ShapeDtypeStruct((1, D), F32), jax.ShapeDtypeStruct((1, D), F32)],
        compiler_params=_params(("arbitrary",)),
    )(dxn, xhat, rstd, g, *_after_args(after))


def _dycat(drb, wout_g, k, l):
    s = drb.shape[0]
    tm = min(s, 512)

    def body(d_ref, w_ref, o_ref):
        o_ref[...] = _dot(d_ref[...], w_ref[...].reshape(DY, D), NT)

    return pl.pallas_call(
        body, name=f"dycat_{l}", grid=(s // tm,),
        in_specs=[pl.BlockSpec((tm, D), lambda m: (m, 0)),
                  pl.BlockSpec((NQ, 2, None, RQ // 2, D), lambda m: (0, 0, k, 0, 0))],
        out_specs=pl.BlockSpec((tm, DY), lambda m: (m, 0)),
        out_shape=jax.ShapeDtypeStruct((s, DY), F32),
        compiler_params=_params(("arbitrary",)),
    )(drb, wout_g)


def _dwout(ycat, drb, l):
    s = drb.shape[0]
    tk = min(s, 512)

    def body(y_ref, d_ref, o_ref):
        part = _dot(y_ref[...], d_ref[...], TN)

        @pl.when(pl.program_id(0) == 0)
        def _():
            o_ref[...] = part

        @pl.when(pl.program_id(0) > 0)
        def _():
            o_ref[...] += part

    return pl.pallas_call(
        body, name=f"dwout_{l}", grid=(s // tk,),
        in_specs=[pl.BlockSpec((tk, DY), lambda k: (k, 0)), pl.BlockSpec((tk, D), lambda k: (k, 0))],
        out_specs=pl.BlockSpec((DY, D), lambda k: (0, 0)),
        out_shape=jax.ShapeDtypeStruct((DY, D), F32),
        compiler_params=_params(("arbitrary",)),
    )(ycat, drb)


def _dwin(xb, dzb, l):
    s = xb.shape[0]
    tk = min(s, 1024)

    def body(x_ref, d_ref, o_ref):
        part = _dot(x_ref[...], d_ref[...], TN)

        @pl.when(pl.program_id(1) == 0)
        def _():
            o_ref[...] = part

        @pl.when(pl.program_id(1) > 0)
        def _():
            o_ref[...] += part

    return pl.pallas_call(
        body, name=f"dwin_{l}", grid=(NQ, s // tk),
        in_specs=[pl.BlockSpec((tk, D), lambda q, k: (k, 0)), pl.BlockSpec((tk, WQ), lambda q, k: (k, q))],
        out_specs=pl.BlockSpec((None, D, WQ), lambda q, k: (q, 0, 0)),
        out_shape=jax.ShapeDtypeStruct((NQ, D, WQ), F32),
        compiler_params=_params(("arbitrary", "arbitrary")),
    )(xb, dzb)


def _dx(dzb, win_g, k, l, dr):
    s = dzb.shape[0]
    tm = min(s, 1024)

    def body(d_ref, w_ref, r_ref, o_ref):
        part = _dot(d_ref[...], w_ref[...].reshape(D, WQ), NT)

        @pl.when(pl.program_id(1) == 0)
        def _():
            o_ref[...] = ALPHA * r_ref[...] + part

        @pl.when(pl.program_id(1) > 0)
        def _():
            o_ref[...] += part

    return pl.pallas_call(
        body, name=f"dx_{l}", grid=(s // tm, NQ),
        in_specs=[pl.BlockSpec((tm, WQ), lambda m, q: (m, q)),
                  pl.BlockSpec((None, 2, None, D // 2, WQ), lambda m, q: (q, 0, k, 0, 0)),
                  pl.BlockSpec((tm, D), lambda m, q: (m, 0))],
        out_specs=pl.BlockSpec((tm, D), lambda m, q: (m, 0)),
        out_shape=jax.ShapeDtypeStruct((s, D), F32),
        compiler_params=_params(("arbitrary", "arbitrary")),
    )(dzb, win_g, dr)


def _window_sums(e, causal):
    n = e.shape[0]

    def shifted(a, k):
        return pltpu.roll(a, k if causal else n - k, axis=0)

    parts = []
    acc = e
    for step, k in enumerate((1, 2, 4, 8)):
        acc = acc + shifted(acc, k)
        parts.append(acc[:, 0:HEAD])
        if step < 3:
            acc = acc[:, HEAD:]
    return jnp.concatenate(parts, axis=1)


def _pool_counts(first_pos, rows):
    t1 = (lax.broadcasted_iota(jnp.int32, (rows, 1), 0) + first_pos + 1).astype(F32)
    lane = lax.broadcasted_iota(jnp.int32, (1, D), 1)
    win = jnp.where(lane < HEAD, 2.0, jnp.where(lane < 2 * HEAD, 4.0, jnp.where(lane < 3 * HEAD, 8.0, 16.0)))
    return jnp.minimum(t1, win)


def _group_dot(v, w_ref, dims):
    return jnp.concatenate(
        [_dot(v[:, g * HEAD:(g + 1) * HEAD], w_ref[g], dims) for g in range(4)], axis=1)


def _prev_index(ts, halo):
    return lambda i: (jnp.maximum(i * (ts // halo) - 1, 0), 0)


def _next_index(ts, halo, s):
    return lambda i: (jnp.minimum((i + 1) * (ts // halo), s // halo - 1), 0)


def _even_fwd(z, pool_w, pool_scale, sconv_w, sconv_b, l):
    s = z.shape[0]
    ts = min(s, 256)
    h = POOL_HALO

    def body(z_ref, zp_ref, pw_ref, ps_ref, cw_ref, cb_ref, o_ref):
        i = pl.program_id(0)
        inside = i > 0
        xa = z_ref[:, 0:D]
        xa_ext = jnp.concatenate([jnp.where(inside, zp_ref[:, 0:D], 0.0), xa], axis=0)
        sums = _window_sums(xa_ext, True)[h:]
        pooled = sums / _pool_counts(i * ts, ts) - xa
        p = _group_dot(pooled, pw_ref, NN)
        silu_ga, _ = _silu_and_grad(z_ref[:, D:2 * D])
        o_ref[:, 0:D] = (p * ps_ref[...] * silu_ga).astype(BF16)

        q_main = z_ref[:, 4 * D:5 * D] * z_ref[:, 2 * D:3 * D]
        q_prev = jnp.where(inside, zp_ref[:, 4 * D:5 * D] * zp_ref[:, 2 * D:3 * D], 0.0)
        q_ext = jnp.concatenate([q_prev, q_main], axis=0)
        cv = cw_ref[2:3, :] * q_main + cb_ref[...]
        cv = cv + cw_ref[1:2, :] * pltpu.roll(q_ext, 1, axis=0)[h:]
        cv = cv + cw_ref[0:1, :] * pltpu.roll(q_ext, 2, axis=0)[h:]
        silu_gb, _ = _silu_and_grad(z_ref[:, 5 * D:6 * D])
        o_ref[:, D:2 * D] = (z_ref[:, 3 * D:4 * D] * cv * silu_gb).astype(BF16)

    fixed2 = lambda i: (0, 0)
    return pl.pallas_call(
        body, name=f"even_fwd_{l}", grid=(s // ts,),
        in_specs=[pl.BlockSpec((ts, DZ), lambda i: (i, 0)), pl.BlockSpec((h, DZ), _prev_index(ts, h)),
                  pl.BlockSpec((4, HEAD, HEAD), lambda i: (0, 0, 0)), pl.BlockSpec((1, D), fixed2),
                  pl.BlockSpec((SHORT_K, D), fixed2), pl.BlockSpec((1, D), fixed2)],
        out_specs=pl.BlockSpec((ts, DY), lambda i: (i, 0)),
        out_shape=jax.ShapeDtypeStruct((s, DY), BF16),
        compiler_params=_params(("arbitrary",)),
    )(z, z, pool_w, pool_scale, sconv_w, sconv_b)


def _even_bwd(z, dy, pool_w, pool_wt, pool_scale, sconv_w, sconv_b, l):
    s = z.shape[0]
    ts = min(s, 256)
    h = POOL_HALO
    n_tiles = s // ts

    def body(z_ref, zp_ref, zn_ref, dy_ref, dyn_ref, pw_ref, pwt_ref, ps_ref, cw_ref, cb_ref,
             dz_ref, dpw_ref, dps_ref, dcw_ref, dcb_ref):
        i = pl.program_id(0)
        inside = i > 0
        more = i < n_tiles - 1

        @pl.when(i == 0)
        def _():
            dpw_ref[...] = jnp.zeros_like(dpw_ref)
            dps_ref[...] = jnp.zeros_like(dps_ref)
            dcw_ref[...] = jnp.zeros_like(dcw_ref)
            dcb_ref[...] = jnp.zeros_like(dcb_ref)

        def with_next(main_ref, next_ref, lo):
            return jnp.concatenate([main_ref[:, lo:lo + D], next_ref[:, lo:lo + D]], axis=0)

        xa_ext = jnp.concatenate(
            [jnp.where(inside, zp_ref[:, 0:D], 0.0), z_ref[:, 0:D], zn_ref[:, 0:D]], axis=0)
        counts = _pool_counts(i * ts, ts + h)
        pooled = _window_sums(xa_ext, True)[h:] / counts - xa_ext[h:]
        p = _group_dot(pooled, pw_ref, NN)
        silu_ga, dsilu_ga = _silu_and_grad(with_next(z_ref, zn_ref, D))
        d_ya = with_next(dy_ref, dyn_ref, 0)
        scale = ps_ref[...]
        d_p = d_ya * scale * silu_ga
        d_pooled = _group_dot(d_p, pwt_ref, NN)
        row = lax.broadcasted_iota(jnp.int32, (ts + h, 1), 0)
        d_pooled = jnp.where(jnp.logical_or(more, row < ts), d_pooled, 0.0)
        d_xa = _window_sums(d_pooled / counts, False)[:ts] - d_pooled[:ts]
        dz_ref[:, 0:D] = d_xa.astype(BF16)
        dz_ref[:, D:2 * D] = (d_ya[:ts] * p[:ts] * scale * dsilu_ga[:ts]).astype(BF16)
        dps_ref[...] += _colsum(d_ya[:ts] * p[:ts] * silu_ga[:ts])
        for g in range(4):
            cols = slice(g * HEAD, (g + 1) * HEAD)
            dpw_ref[g] += _dot(pooled[:ts, cols], d_p[:ts, cols], TN)

        cg = z_ref[:, 4 * D:5 * D]
        hh = z_ref[:, 2 * D:3 * D]
        bg = z_ref[:, 3 * D:4 * D]
        q_main = cg * hh
        q_prev = jnp.where(inside, zp_ref[:, 4 * D:5 * D] * zp_ref[:, 2 * D:3 * D], 0.0)
        q_ext = jnp.concatenate([q_prev, q_main], axis=0)
        q_1 = pltpu.roll(q_ext, 1, axis=0)[h:]
        q_2 = pltpu.roll(q_ext, 2, axis=0)[h:]
        cv = cw_ref[2:3, :] * q_main + cw_ref[1:2, :] * q_1 + cw_ref[0:1, :] * q_2 + cb_ref[...]
        silu_gb, dsilu_gb = _silu_and_grad(with_next(z_ref, zn_ref, 5 * D))
        d_yb = with_next(dy_ref, dyn_ref, D)
        d_cv = d_yb * with_next(z_ref, zn_ref, 3 * D) * silu_gb
        d_cv = jnp.where(jnp.logical_or(more, row < ts), d_cv, 0.0)
        d_cv0 = d_cv[:ts]
        n_ext = ts + h
        d_q = (cw_ref[2:3, :] * d_cv0 + cw_ref[1:2, :] * pltpu.roll(d_cv, n_ext - 1, axis=0)[:ts]
               + cw_ref[0:1, :] * pltpu.roll(d_cv, n_ext - 2, axis=0)[:ts])
        dz_ref[:, 2 * D:3 * D] = (d_q * cg).astype(BF16)
        dz_ref[:, 3 * D:4 * D] = (d_yb[:ts] * cv * silu_gb[:ts]).astype(BF16)
        dz_ref[:, 4 * D:5 * D] = (d_q * hh).astype(BF16)
        dz_ref[:, 5 * D:6 * D] = (d_yb[:ts] * bg * cv * dsilu_gb[:ts]).astype(BF16)
        dcb_ref[...] += _colsum(d_cv0)
        dcw_ref[2:3, :] += _colsum(d_cv0 * q_main)
        dcw_ref[1:2, :] += _colsum(d_cv0 * q_1)
        dcw_ref[0:1, :] += _colsum(d_cv0 * q_2)

    fixed2 = lambda i: (0, 0)
    fixed3 = lambda i: (0, 0, 0)
    return pl.pallas_call(
        body, name=f"even_bwd_{l}", grid=(n_tiles,),
        in_specs=[pl.BlockSpec((ts, DZ), lambda i: (i, 0)), pl.BlockSpec((h, DZ), _prev_index(ts, h)),
                  pl.BlockSpec((h, DZ), _next_index(ts, h, s)),
                  pl.BlockSpec((ts, DY), lambda i: (i, 0)), pl.BlockSpec((h, DY), _next_index(ts, h, s)),
                  pl.BlockSpec((4, HEAD, HEAD), fixed3), pl.BlockSpec((4, HEAD, HEAD), fixed3),
                  pl.BlockSpec((1, D), fixed2), pl.BlockSpec((SHORT_K, D), fixed2), pl.BlockSpec((1, D), fixed2)],
        out_specs=[pl.BlockSpec((ts, DZ), lambda i: (i, 0)), pl.BlockSpec((4, HEAD, HEAD), fixed3),
                   pl.BlockSpec((1, D), fixed2), pl.BlockSpec((SHORT_K, D), fixed2), pl.BlockSpec((1, D), fixed2)],
        out_shape=[jax.ShapeDtypeStruct((s, DZ), BF16), jax.ShapeDtypeStruct((4, HEAD, HEAD), F32),
                   jax.ShapeDtypeStruct((1, D), F32), jax.ShapeDtypeStruct((SHORT_K, D), F32),
                   jax.ShapeDtypeStruct((1, D), F32)],
        compiler_params=_params(("arbitrary",), 56),
    )(z, z, z, dy, dy, pool_w, pool_wt, pool_scale, sconv_w, sconv_b)


def _conv31(src_ref, w_ref, dst_ref, base, rows, causal):
    def block(cb, carry):
        cols = pl.ds(pl.multiple_of(cb * LANE, LANE), LANE)
        acc = None
        for d in range(CONV_K):
            start = base - d if causal else base + d
            term = w_ref[CONV_K - 1 - d:CONV_K - d, cols] * src_ref[pl.ds(start, rows), cols]
            acc = term if acc is None else acc + term
        dst_ref[pl.ds(0, rows), cols] = acc
        return carry

    lax.fori_loop(0, D // LANE, block, 0)


def _sgu_bias_rows(sgu_b):
    return jnp.repeat(jnp.transpose(sgu_b), HEAD, axis=1)


def _odd_fwd(z, sln_g, sln_b, ws, sbias, dconv_w, dconv_b, dn_g, dn_b, l):
    s = z.shape[0]
    ts = min(s, 256)
    h = CONV_HALO

    def body(z_ref, zp_ref, lg_ref, lb_ref, ws_ref, sb_ref, cw_ref, cb_ref, ng_ref, nb_ref, o_ref, zz_ref, zc_ref):
        i = pl.program_id(0)
        vhat, _ = _ln_stats(z_ref[:, D:2 * D])
        vn = (vhat * lg_ref[...] + lb_ref[...]).astype(MXU_DTYPE)
        silu_gc, _ = _silu_and_grad(z_ref[:, 2 * D:3 * D])
        for n in range(ts // SGU_BLOCK):
            rows = slice(n * SGU_BLOCK, (n + 1) * SGU_BLOCK)
            sv = jnp.concatenate(
                [_dot(ws_ref[hd], vn[rows, hd * HEAD:(hd + 1) * HEAD], NN) for hd in range(4)], axis=1)
            sv = sv + sb_ref[...]
            o_ref[rows, 0:D] = (z_ref[rows, 0:D] * sv * silu_gc[rows]).astype(BF16)

        zz_ref[0:h, :] = jnp.where(i > 0, zp_ref[:, 3 * D:4 * D] * _sigmoid(zp_ref[:, 4 * D:5 * D]), 0.0)
        zz_ref[h:h + ts, :] = z_ref[:, 3 * D:4 * D] * _sigmoid(z_ref[:, 4 * D:5 * D])
        _conv31(zz_ref, cw_ref, zc_ref, h, ts, True)
        zhat, _ = _ln_stats(zc_ref[...] + cb_ref[...])
        silu_zn, _ = _silu_and_grad(zhat * ng_ref[...] + nb_ref[...])
        silu_gd, _ = _silu_and_grad(z_ref[:, 5 * D:6 * D])
        o_ref[:, D:2 * D] = (silu_zn * silu_gd).astype(BF16)

    fixed2 = lambda i: (0, 0)
    vec = pl.BlockSpec((1, D), fixed2)
    return pl.pallas_call(
        body, name=f"odd_fwd_{l}", grid=(s // ts,),
        in_specs=[pl.BlockSpec((ts, DZ), lambda i: (i, 0)), pl.BlockSpec((h, DZ), _prev_index(ts, h)),
                  vec, vec, pl.BlockSpec((4, SGU_BLOCK, SGU_BLOCK), lambda i: (0, 0, 0)),
                  pl.BlockSpec((SGU_BLOCK, D), fixed2), pl.BlockSpec((CONV_K, D), fixed2), vec, vec, vec],
        out_specs=pl.BlockSpec((ts, DY), lambda i: (i, 0)),
        out_shape=jax.ShapeDtypeStruct((s, DY), BF16),
        scratch_shapes=[pltpu.VMEM((h + ts, D), F32), pltpu.VMEM((ts, D), F32)],
        compiler_params=_params(("arbitrary",)),
    )(z, z, sln_g, sln_b, ws, sbias, dconv_w, dconv_b, dn_g, dn_b)


def _odd_bwd(z, dy, sln_g, sln_b, ws, wst, sbias, dconv_w, dconv_b, dn_g, dn_b, l):
    s = z.shape[0]
    ts = min(s, 256)
    h = CONV_HALO
    n_tiles = s // ts
    te = ts + h

    def body(z_ref, zp_ref, zn_ref, dy_ref, dyn_ref, lg_ref, lb_ref, ws_ref, wst_ref, sb_ref, cw_ref, cb_ref,
             ng_ref, nb_ref, dz_ref, dlg_ref, dlb_ref, dws_ref, dsb_ref, dcw_ref, dcb_ref, dng_ref, dnb_ref,
             zz_ref, zc_ref, dzc_ref, dzz_ref, dsb_acc):
        i = pl.program_id(0)
        more = i < n_tiles - 1

        @pl.when(i == 0)
        def _():
            for ref in (dlg_ref, dlb_ref, dws_ref, dsb_ref, dcw_ref, dcb_ref, dng_ref, dnb_ref, dsb_acc):
                ref[...] = jnp.zeros_like(ref)

        vhat, rstd_v = _ln_stats(z_ref[:, D:2 * D])
        lg = lg_ref[...]
        vn = (vhat * lg + lb_ref[...]).astype(MXU_DTYPE)
        u = z_ref[:, 0:D]
        silu_gc, dsilu_gc = _silu_and_grad(z_ref[:, 2 * D:3 * D])
        d_yc = dy_ref[:, 0:D]
        d_yc_u = d_yc * u
        d_sv = d_yc_u * silu_gc
        d_svb = d_sv.astype(MXU_DTYPE)
        sv_rows = []
        dvn_rows = []
        dsb = None
        for n in range(ts // SGU_BLOCK):
            rows = slice(n * SGU_BLOCK, (n + 1) * SGU_BLOCK)
            sv_parts = []
            dvn_parts = []
            for hd in range(4):
                cols = slice(hd * HEAD, (hd + 1) * HEAD)
                sv_parts.append(_dot(ws_ref[hd], vn[rows, cols], NN))
                dvn_parts.append(_dot(wst_ref[hd], d_svb[rows, cols], NN))
                dws_ref[hd] += _dot(d_svb[rows, cols], vn[rows, cols], NT)
            sv_rows.append(jnp.concatenate(sv_parts, axis=1) + sb_ref[...])
            dvn_rows.append(jnp.concatenate(dvn_parts, axis=1))
            dsb = d_sv[rows] if dsb is None else dsb + d_sv[rows]
        dsb_acc[...] += dsb

        @pl.when(i == n_tiles - 1)
        def _():
            for hd in range(4):
                blk = dsb_acc[:, hd * HEAD:(hd + 1) * HEAD]
                folded = blk[:, 0:LANE] + blk[:, LANE:HEAD]
                dsb_ref[hd:hd + 1, :] = _colsum(jnp.transpose(folded))
        sv = jnp.concatenate(sv_rows, axis=0)
        d_vn = jnp.concatenate(dvn_rows, axis=0)
        dz_ref[:, 0:D] = (d_yc * sv * silu_gc).astype(BF16)
        dz_ref[:, D:2 * D] = _ln_bwd_rows(d_vn * lg, vhat, rstd_v).astype(BF16)
        dz_ref[:, 2 * D:3 * D] = (d_yc_u * sv * dsilu_gc).astype(BF16)
        dlg_ref[...] += _colsum(d_vn * vhat)
        dlb_ref[...] += _colsum(d_vn)

        def gate(ref):
            return ref[:, 3 * D:4 * D] * _sigmoid(ref[:, 4 * D:5 * D])

        zz_ref[0:h, :] = jnp.where(i > 0, gate(zp_ref), 0.0)
        zz_ref[h:h + ts, :] = gate(z_ref)
        zz_ref[h + ts:h + te, :] = gate(zn_ref)
        _conv31(zz_ref, cw_ref, zc_ref, h, te, True)
        zhat, rstd_z = _ln_stats(zc_ref[...] + cb_ref[...])
        ng = ng_ref[...]
        silu_zn, dsilu_zn = _silu_and_grad(zhat * ng + nb_ref[...])
        gd = jnp.concatenate([z_ref[:, 5 * D:6 * D], zn_ref[:, 5 * D:6 * D]], axis=0)
        silu_gd, dsilu_gd = _silu_and_grad(gd)
        d_yd = jnp.concatenate([dy_ref[:, D:2 * D], dyn_ref[:, D:2 * D]], axis=0)
        d_zn = d_yd * silu_gd * dsilu_zn
        d_zc = _ln_bwd_rows(d_zn * ng, zhat, rstd_z)
        row = lax.broadcasted_iota(jnp.int32, (te, 1), 0)
        d_zc = jnp.where(jnp.logical_or(more, row < ts), d_zc, 0.0)
        dzc_ref[...] = d_zc
        dz_ref[:, 5 * D:6 * D] = (d_yd[:ts] * silu_zn[:ts] * dsilu_gd[:ts]).astype(BF16)
        dng_ref[...] += _colsum(d_zn[:ts] * zhat[:ts])
        dnb_ref[...] += _colsum(d_zn[:ts])
        dcb_ref[...] += _colsum(d_zc[:ts])
        _conv31(dzc_ref, cw_ref, dzz_ref, 0, ts, False)

        def tap_grads(cb, carry):
            cols = pl.ds(pl.multiple_of(cb * LANE, LANE), LANE)
            d_blk = dzc_ref[pl.ds(0, ts), cols]
            for d in range(CONV_K):
                j = CONV_K - 1 - d
                dcw_ref[j:j + 1, cols] += _colsum(d_blk * zz_ref[pl.ds(h - d, ts), cols])
            return carry

        lax.fori_loop(0, D // LANE, tap_grads, 0)
        d_zz = dzz_ref[...]
        a = z_ref[:, 3 * D:4 * D]
        sig_b = _sigmoid(z_ref[:, 4 * D:5 * D])
        dz_ref[:, 3 * D:4 * D] = (d_zz * sig_b).astype(BF16)
        dz_ref[:, 4 * D:5 * D] = (d_zz * a * sig_b * (1.0 - sig_b)).astype(BF16)

    fixed2 = lambda i: (0, 0)
    fixed3 = lambda i: (0, 0, 0)
    vec = pl.BlockSpec((1, D), fixed2)
    mat = pl.BlockSpec((4, SGU_BLOCK, SGU_BLOCK), fixed3)
    vec_shape = jax.ShapeDtypeStruct((1, D), F32)
    return pl.pallas_call(
        body, name=f"odd_bwd_{l}", grid=(n_tiles,),
        in_specs=[pl.BlockSpec((ts, DZ), lambda i: (i, 0)), pl.BlockSpec((h, DZ), _prev_index(ts, h)),
                  pl.BlockSpec((h, DZ), _next_index(ts, h, s)),
                  pl.BlockSpec((ts, DY), lambda i: (i, 0)), pl.BlockSpec((h, DY), _next_index(ts, h, s)),
                  vec, vec, mat, mat, pl.BlockSpec((SGU_BLOCK, D), fixed2),
                  pl.BlockSpec((CONV_K, D), fixed2), vec, vec, vec],
        out_specs=[pl.BlockSpec((ts, DZ), lambda i: (i, 0)), vec, vec, mat, pl.BlockSpec((4, SGU_BLOCK), fixed2),
                   pl.BlockSpec((CONV_K, D), fixed2), vec, vec, vec],
        out_shape=[jax.ShapeDtypeStruct((s, DZ), BF16), vec_shape, vec_shape,
                   jax.ShapeDtypeStruct((4, SGU_BLOCK, SGU_BLOCK), F32), jax.ShapeDtypeStruct((4, SGU_BLOCK), F32),
                   jax.ShapeDtypeStruct((CONV_K, D), F32), vec_shape, vec_shape, vec_shape],
        scratch_shapes=[pltpu.VMEM((h + te, D), F32), pltpu.VMEM((te, D), F32), pltpu.VMEM((te, D), F32),
                        pltpu.VMEM((ts, D), F32), pltpu.VMEM((SGU_BLOCK, D), F32)],
        compiler_params=_params(("arbitrary",), 56),
    )(z, z, z, dy, dy, sln_g, sln_b, ws, wst, sbias, dconv_w, dconv_b, dn_g, dn_b)


def _remote(src, dst, send_sems, recv_sems, k, to):
    return pltpu.make_async_remote_copy(src_ref=src, dst_ref=dst, send_sem=send_sems.at[k],
                                        recv_sem=recv_sems.at[k], device_id=to, device_id_type=MESH)


def _other_chips(x, y):
    return [(1 - x, y, 2 * (1 - x) + y), (x, 1 - y, 2 * x + 1 - y), (1 - x, 1 - y, 2 * (1 - x) + 1 - y)]


def _gather_weights(win_sh, wout_sh, small_sh):
    hi, ho = D // 2, RQ // 2

    def body(win, wout, small, win_g, wout_g, small_g, send_sems, recv_sems, local_sems):
        x, y, c = _mesh_pos()
        me = 2 * x + y
        sibling = (x, y, 1 - c)
        chips = _other_chips(x, y)

        local = [pltpu.make_async_copy(small, small_g.at[me], local_sems.at[0])]
        for half in range(2):
            local.append(pltpu.make_async_copy(win.at[:, pl.ds(half * hi, hi), :], win_g.at[me, half],
                                               local_sems.at[1 + half]))
            local.append(pltpu.make_async_copy(wout.at[:, pl.ds(half * ho, ho), :], wout_g.at[me, half],
                                               local_sems.at[3 + half]))
        for cp in local:
            cp.start()
        sends = []
        for j, (cx, cy, _) in enumerate(chips):
            to = (cx, cy, c)
            sends.append(_remote(win.at[:, pl.ds(c * hi, hi), :], win_g.at[me, c], send_sems, recv_sems, j, to))
            sends.append(_remote(wout.at[:, pl.ds(c * ho, ho), :], wout_g.at[me, c], send_sems, recv_sems, 3 + j, to))
            sends.append(_remote(small, small_g.at[me], send_sems, recv_sems, 6 + j, to))
        for cp in sends:
            cp.start()
        passed = []
        for j, (_, _, q) in enumerate(chips):
            got_in = win_g.at[q, c]
            got_out = wout_g.at[q, c]
            _remote(got_in, got_in, send_sems, recv_sems, j, sibling).wait_recv()
            cp = _remote(got_in, got_in, send_sems, recv_sems, 9 + j, sibling)
            cp.start()
            passed.append(cp)
            _remote(got_out, got_out, send_sems, recv_sems, 3 + j, sibling).wait_recv()
            cp = _remote(got_out, got_out, send_sems, recv_sems, 12 + j, sibling)
            cp.start()
            passed.append(cp)
            _remote(small, small_g.at[q], send_sems, recv_sems, 6 + j, sibling).wait_recv()
        for j, (_, _, q) in enumerate(chips):
            from_in = win_g.at[q, 1 - c]
            from_out = wout_g.at[q, 1 - c]
            _remote(from_in, from_in, send_sems, recv_sems, 9 + j, sibling).wait_recv()
            _remote(from_out, from_out, send_sems, recv_sems, 12 + j, sibling).wait_recv()
        for cp in sends + passed:
            cp.wait_send()
        for cp in local:
            cp.wait()

    slots = win_sh.shape[0]
    return pl.pallas_call(
        body, name="gather_weights",
        in_specs=[HBM_SPEC, HBM_SPEC, HBM_SPEC], out_specs=[HBM_SPEC, HBM_SPEC, HBM_SPEC],
        out_shape=[jax.ShapeDtypeStruct((NQ, 2, slots, hi, WQ), win_sh.dtype),
                   jax.ShapeDtypeStruct((NQ, 2, slots, ho, D), wout_sh.dtype),
                   jax.ShapeDtypeStruct((NQ,) + small_sh.shape, small_sh.dtype)],
        scratch_shapes=[pltpu.SemaphoreType.DMA((15,)), pltpu.SemaphoreType.DMA((15,)),
                        pltpu.SemaphoreType.DMA((5,))],
    )(win_sh, wout_sh, small_sh)


def _hbm(a):
    return pltpu.with_memory_space_constraint(a, pltpu.HBM)


def _split_start(body, name, sources, landings):
    n_src, n_land = len(sources), len(landings)
    n_buf = n_src + n_land

    def kernel_body(*refs):
        ins, outs = refs[:n_buf], refs[n_buf:]
        send_sems, recv_sems, token = outs[0], outs[1], outs[2 + n_buf]
        body(ins[:n_src], ins[n_src:], send_sems, recv_sems)
        token[...] = jnp.zeros_like(token)

    bufs = [_hbm(a) for a in sources] + [_hbm(lax.empty(s.shape, s.dtype)) for s in landings]
    n_sem = getattr(body, "n_copies")
    out = pl.pallas_call(
        kernel_body, name=name,
        out_shape=(pltpu.SemaphoreType.DMA((n_sem,)), pltpu.SemaphoreType.DMA((n_sem,)),
                   *[pltpu.HBM(b.shape, b.dtype) for b in bufs], jax.ShapeDtypeStruct((8, LANE), F32)),
        in_specs=(HBM_SPEC,) * n_buf,
        out_specs=(SEM_SPEC, SEM_SPEC, *([HBM_SPEC] * n_buf), pl.BlockSpec(memory_space=pltpu.VMEM)),
        input_output_aliases={k: 2 + k for k in range(n_buf)},
        compiler_params=pltpu.CompilerParams(has_side_effects=SIDE_EFFECT),
    )(*bufs)
    return out[0], out[1], list(out[2:2 + n_src]), list(out[2 + n_src:2 + n_buf]), out[2 + n_buf]


def _split_wait(body, name, send_sems, recv_sems, sources, landings, after):
    n_src, n_land = len(sources), len(landings)
    n_buf = n_src + n_land

    def kernel_body(*refs):
        ins = refs[:n_buf]
        body(ins[:n_src], ins[n_src:], refs[n_buf], refs[n_buf + 1])

    bufs = list(sources) + list(landings)
    out = pl.pallas_call(
        kernel_body, name=name,
        out_shape=tuple(pltpu.HBM(b.shape, b.dtype) for b in bufs),
        in_specs=(*([HBM_SPEC] * n_buf), SEM_SPEC, SEM_SPEC, pl.BlockSpec(memory_space=pl.ANY)),
        out_specs=(HBM_SPEC,) * n_buf,
        input_output_aliases={k: k for k in range(n_buf)},
        compiler_params=pltpu.CompilerParams(has_side_effects=SIDE_EFFECT),
    )(*bufs, send_sems, recv_sems, after)
    return list(out[:n_src]), list(out[n_src:])


def _gather_rest_copies(start):
    hi, ho = D // 2, RQ // 2

    def body(srcs, lands, send_sems, recv_sems):
        win, wout = srcs
        win_g, wout_g = lands
        x, y, c = _mesh_pos()
        me = 2 * x + y
        for j, (cx, cy, q) in enumerate(_other_chips(x, y)):
            to = (cx, cy, c)
            if start:
                _remote(win.at[:, pl.ds(c * hi, hi), :], win_g.at[me, c], send_sems, recv_sems, j, to).start()
                _remote(wout.at[:, pl.ds(c * ho, ho), :], wout_g.at[me, c], send_sems, recv_sems, 3 + j, to).start()
            else:
                a = _remote(win.at[:, pl.ds(c * hi, hi), :], win_g.at[q, c], send_sems, recv_sems, j, to)
                b = _remote(wout.at[:, pl.ds(c * ho, ho), :], wout_g.at[q, c], send_sems, recv_sems, 3 + j, to)
                for cp in (a, b):
                    cp.wait_send()
                    cp.wait_recv()

    body.n_copies = 6
    return body


def _gather_rest_forward(win_sh, wout_sh, win_g, wout_g):
    hi, ho = D // 2, RQ // 2

    def body(win, wout, win_in, wout_in, win_g, wout_g, send_sems, recv_sems, local_sems):
        del win_in, wout_in
        x, y, c = _mesh_pos()
        me = 2 * x + y
        sibling = (x, y, 1 - c)
        local = []
        for half in range(2):
            local.append(pltpu.make_async_copy(win.at[:, pl.ds(half * hi, hi), :], win_g.at[me, half],
                                               local_sems.at[half]))
            local.append(pltpu.make_async_copy(wout.at[:, pl.ds(half * ho, ho), :], wout_g.at[me, half],
                                               local_sems.at[2 + half]))
        for cp in local:
            cp.start()
        passed = []
        for j, (_, _, q) in enumerate(_other_chips(x, y)):
            got_in = win_g.at[q, c]
            got_out = wout_g.at[q, c]
            passed.append(_remote(got_in, got_in, send_sems, recv_sems, j, sibling))
            passed.append(_remote(got_out, got_out, send_sems, recv_sems, 3 + j, sibling))
        for cp in passed:
            cp.start()
        for j, (_, _, q) in enumerate(_other_chips(x, y)):
            from_in = win_g.at[q, 1 - c]
            from_out = wout_g.at[q, 1 - c]
            _remote(from_in, from_in, send_sems, recv_sems, j, sibling).wait_recv()
            _remote(from_out, from_out, send_sems, recv_sems, 3 + j, sibling).wait_recv()
        for cp in passed:
            cp.wait_send()
        for cp in local:
            cp.wait()

    return pl.pallas_call(
        body, name="gather_rest_forward",
        in_specs=[HBM_SPEC] * 4, out_specs=[HBM_SPEC] * 2,
        out_shape=[jax.ShapeDtypeStruct(win_g.shape, win_g.dtype), jax.ShapeDtypeStruct(wout_g.shape, wout_g.dtype)],
        input_output_aliases={2: 0, 3: 1},
        scratch_shapes=[pltpu.SemaphoreType.DMA((6,)), pltpu.SemaphoreType.DMA((6,)),
                        pltpu.SemaphoreType.DMA((4,))],
    )(win_sh, wout_sh, win_g, wout_g)


def _allreduce_small(groups, after):
    pieces = [p for _, _, members in groups for _, p in members]
    n_in, n_g = len(pieces), len(groups)

    def body(*refs):
        ins = refs[:n_in]
        outs = refs[-(2 * n_g + 2):-(n_g + 2)]
        alls = refs[-(n_g + 2):-2]
        send_sems, recv_sems = refs[-2:]
        x, y, c = _mesh_pos()
        me = 4 * x + 2 * y + c
        sibling = (x, y, 1 - c)
        chips = _other_chips(x, y)

        k = 0
        for (rows, cols, members), all_ref in zip(groups, alls):
            all_ref[me] = jnp.zeros((rows, cols), F32)
            for first, piece in members:
                n, width = piece.shape
                all_ref[me, first:first + n, 0:width] = ins[k][...]
                k += 1

        sends, passed = [], []
        for g, all_ref in enumerate(alls):
            sends.append(_remote(all_ref.at[me], all_ref.at[me], send_sems, recv_sems, 7 * g, sibling))
            for j, (cx, cy, _) in enumerate(chips):
                sends.append(_remote(all_ref.at[me], all_ref.at[me], send_sems, recv_sems, 7 * g + 1 + j, (cx, cy, c)))
        for cp in sends:
            cp.start()
        for j, (cx, cy, _) in enumerate(chips):
            for g, all_ref in enumerate(alls):
                got = all_ref.at[4 * cx + 2 * cy + c]
                _remote(got, got, send_sems, recv_sems, 7 * g + 1 + j, sibling).wait_recv()
                cp = _remote(got, got, send_sems, recv_sems, 7 * g + 4 + j, sibling)
                cp.start()
                passed.append(cp)
        for g, all_ref in enumerate(alls):
            got = all_ref.at[4 * x + 2 * y + 1 - c]
            _remote(got, got, send_sems, recv_sems, 7 * g, sibling).wait_recv()
            for j, (cx, cy, _) in enumerate(chips):
                got = all_ref.at[4 * cx + 2 * cy + 1 - c]
                _remote(got, got, send_sems, recv_sems, 7 * g + 4 + j, sibling).wait_recv()
        for cp in sends + passed:
            cp.wait_send()
        for o_ref, all_ref in zip(outs, alls):
            total = all_ref[0]
            for dev in range(1, 8):
                total = total + all_ref[dev]
            o_ref[...] = total

    vmem = pl.BlockSpec(memory_space=pltpu.VMEM)
    return pl.pallas_call(
        body, name="allreduce_small",
        in_specs=[vmem] * n_in + _after_spec(after),
        out_specs=[vmem] * n_g,
        out_shape=[jax.ShapeDtypeStruct((rows, cols), F32) for rows, cols, _ in groups],
        scratch_shapes=[pltpu.VMEM((8, rows, cols), F32) for rows, cols, _ in groups]
        + [pltpu.SemaphoreType.DMA((7 * n_g,)), pltpu.SemaphoreType.DMA((7 * n_g,))],
        compiler_params=_params(None, 56),
    )(*pieces, *_after_args(after))


def _pair_exchange(gin, gout, l):
    hi, ho = D // 2, RQ // 2

    def body(gin_ref, gout_ref, rin_ref, rout_ref, send_sems, recv_sems):
        x, y, c = _mesh_pos()
        sibling = (x, y, 1 - c)
        a = _remote(gin_ref.at[:, pl.ds((1 - c) * hi, hi), :], rin_ref, send_sems, recv_sems, 0, sibling)
        b = _remote(gout_ref.at[:, pl.ds((1 - c) * ho, ho), :], rout_ref, send_sems, recv_sems, 1, sibling)
        a.start()
        b.start()
        a.wait()
        b.wait()

    return pl.pallas_call(
        body, name=f"pair_exchange_{l}",
        in_specs=[HBM_SPEC, HBM_SPEC], out_specs=[HBM_SPEC, HBM_SPEC],
        out_shape=[jax.ShapeDtypeStruct((NQ, hi, WQ), F32), jax.ShapeDtypeStruct((NQ, ho, D), F32)],
        scratch_shapes=[pltpu.SemaphoreType.DMA((2,)), pltpu.SemaphoreType.DMA((2,))],
    )(gin, gout)


def _pair_sum(g, r, pos_arr, name):
    nq, rows, cols = r.shape
    tr = min(rows, 256)
    nt = rows // tr

    def body(pos_ref, g_ref, r_ref, ob_ref, own_ref):
        total = g_ref[...] + r_ref[...]
        ob_ref[...] = total.astype(BF16)

        @pl.when(pl.program_id(1) == pos_ref[1])
        def _():
            own_ref[...] = total

    blk = (None, tr, cols)
    grid_spec = pltpu.PrefetchScalarGridSpec(
        num_scalar_prefetch=1, grid=(nt, nq),
        in_specs=[pl.BlockSpec(blk, lambda t, q, pos: (q, pos[0] * nt + t, 0)),
                  pl.BlockSpec(blk, lambda t, q, pos: (q, t, 0))],
        out_specs=[pl.BlockSpec(blk, lambda t, q, pos: (q, t, 0)),
                   pl.BlockSpec((tr, cols), lambda t, q, pos: (t, 0))])
    return pl.pallas_call(
        body, name=name, grid_spec=grid_spec,
        out_shape=[jax.ShapeDtypeStruct(r.shape, BF16), jax.ShapeDtypeStruct((rows, cols), F32)],
        compiler_params=_params(("arbitrary",) * 2),
    )(pos_arr, g, r)


def _chip_copies(start):
    def body(srcs, lands, send_sems, recv_sems):
        pin, pout = srcs
        rin, rout = lands
        x, y, c = _mesh_pos()
        for j, (cx, cy, q) in enumerate(_other_chips(x, y)):
            to = (cx, cy, c)
            for k, (src, dst) in enumerate(((pin, rin), (pout, rout))):
                cp = _remote(src.at[q], dst.at[j], send_sems, recv_sems, 3 * k + j, to)
                if start:
                    cp.start()
                else:
                    cp.wait_send()
                    cp.wait_recv()

    body.n_copies = 6
    return body


def _chip_sum(own, r, name):
    rows, cols = own.shape
    tr = min(rows, 256)

    def body(p_ref, r0_ref, r1_ref, r2_ref, o_ref):
        o_ref[...] = ((p_ref[...] + r0_ref[...].astype(F32)) + r1_ref[...].astype(F32)) + r2_ref[...].astype(F32)

    def peer(j):
        return pl.BlockSpec((None, tr, cols), lambda t: (j, t, 0))

    return pl.pallas_call(
        body, name=name, grid=(rows // tr,),
        in_specs=[pl.BlockSpec((tr, cols), lambda t: (t, 0)), peer(0), peer(1), peer(2)],
        out_specs=pl.BlockSpec((tr, cols), lambda t: (t, 0)),
        out_shape=jax.ShapeDtypeStruct((rows, cols), F32),
        compiler_params=_params(("arbitrary",)),
    )(own, r, r, r)


def _pair_share(fin, fout, l):
    def body(fin_ref, fout_ref, gin_ref, gout_ref, send_sems, recv_sems, local_sems):
        x, y, c = _mesh_pos()
        sibling = (x, y, 1 - c)
        local = [pltpu.make_async_copy(fin_ref, gin_ref.at[c], local_sems.at[0]),
                 pltpu.make_async_copy(fout_ref, gout_ref.at[c], local_sems.at[1])]
        sends = [_remote(fin_ref, gin_ref.at[c], send_sems, recv_sems, 0, sibling),
                 _remote(fout_ref, gout_ref.at[c], send_sems, recv_sems, 1, sibling)]
        for cp in local + sends:
            cp.start()
        _remote(fin_ref, gin_ref.at[1 - c], send_sems, recv_sems, 0, sibling).wait_recv()
        _remote(fout_ref, gout_ref.at[1 - c], send_sems, recv_sems, 1, sibling).wait_recv()
        for cp in sends:
            cp.wait_send()
        for cp in local:
            cp.wait()

    return pl.pallas_call(
        body, name=f"pair_share_{l}",
        in_specs=[HBM_SPEC, HBM_SPEC], out_specs=[HBM_SPEC, HBM_SPEC],
        out_shape=[jax.ShapeDtypeStruct((2,) + fin.shape, F32), jax.ShapeDtypeStruct((2,) + fout.shape, F32)],
        scratch_shapes=[pltpu.SemaphoreType.DMA((2,)), pltpu.SemaphoreType.DMA((2,)),
                        pltpu.SemaphoreType.DMA((2,))],
    )(fin, fout)


def _adamw_large(w, m, v, g, i, prev, name):
    _, rows, cols = w.shape
    half = rows // 2
    tr = min(half, 256)
    nt = half // tr

    def body(w_ref, m_ref, v_ref, g_ref, *rest):
        go_ref, d_ref, mo_ref, vo_ref = rest[-4:]
        gv = g_ref[...]
        go_ref[...] = gv
        d_ref[...], mo_ref[...], vo_ref[...] = _adamw_math(w_ref[...], gv, m_ref[...], v_ref[...])

    full = pl.BlockSpec((None, tr, cols), lambda h, t: (i, h * nt + t, 0))
    out = jax.ShapeDtypeStruct(w.shape, F32)
    carried = [] if prev is None else list(prev)
    return pl.pallas_call(
        body, name=name, grid=(2, nt),
        in_specs=[full, full, full, pl.BlockSpec((None, tr, cols), lambda h, t: (h, t, 0))]
        + [pl.BlockSpec(memory_space=pl.ANY)] * len(carried),
        out_specs=[full] * 4, out_shape=[out] * 4,
        input_output_aliases={4 + k: k for k in range(len(carried))},
        compiler_params=_params(("arbitrary",) * 2),
    )(w, m, v, g, *carried)


def _adamw(w, g, m, v, name):
    shape = w.shape
    w2, g2, m2, v2 = (t.reshape(-1, shape[-1]) for t in (w, g, m, v))
    rows, cols = w2.shape
    tr = 256 if rows % 256 == 0 else rows

    def body(w_ref, g_ref, m_ref, v_ref, d_ref, mo_ref, vo_ref):
        d_ref[...], mo_ref[...], vo_ref[...] = _adamw_math(w_ref[...], g_ref[...], m_ref[...], v_ref[...])

    blk = pl.BlockSpec((tr, cols), lambda i: (i, 0))
    out = jax.ShapeDtypeStruct((rows, cols), F32)
    d, mo, vo = pl.pallas_call(
        body, name=name, grid=(rows // tr,), in_specs=[blk] * 4, out_specs=[blk] * 3, out_shape=[out] * 3,
        compiler_params=_params(("arbitrary",)),
    )(w2, g2, m2, v2)
    return d.reshape(shape), mo.reshape(shape), vo.reshape(shape)


def _layer_slot(l):
    return (l % 2) * 2 + l // 2


def kernel(x, ln_g, ln_b, w_in_even, w_out_even, pool_w, pool_scale, sconv_w, sconv_b, w_in_odd, w_out_odd, sgu_ln_g, sgu_ln_b, sgu_w, sgu_b, dconv_w, dconv_b, dnorm_g, dnorm_b, loss_target, m_ln_g, m_ln_b, m_w_in_even, m_w_out_even, m_pool_w, m_pool_scale, m_sconv_w, m_sconv_b, m_w_in_odd, m_w_out_odd, m_sgu_ln_g, m_sgu_ln_b, m_sgu_w, m_sgu_b, m_dconv_w, m_dconv_b, m_dnorm_g, m_dnorm_b, v_ln_g, v_ln_b, v_w_in_even, v_w_out_even, v_pool_w, v_pool_scale, v_sconv_w, v_sconv_b, v_w_in_odd, v_w_out_odd, v_sgu_ln_g, v_sgu_ln_b, v_sgu_w, v_sgu_b, v_dconv_w, v_dconv_b, v_dnorm_g, v_dnorm_b):
    weights = dict(ln_g=ln_g, ln_b=ln_b, w_in_even=w_in_even, w_out_even=w_out_even, pool_w=pool_w,
                   pool_scale=pool_scale, sconv_w=sconv_w, sconv_b=sconv_b, w_in_odd=w_in_odd, w_out_odd=w_out_odd,
                   sgu_ln_g=sgu_ln_g, sgu_ln_b=sgu_ln_b, sgu_w=sgu_w, sgu_b=sgu_b, dconv_w=dconv_w,
                   dconv_b=dconv_b, dnorm_g=dnorm_g, dnorm_b=dnorm_b)
    moments_m = dict(ln_g=m_ln_g, ln_b=m_ln_b, w_in_even=m_w_in_even, w_out_even=m_w_out_even, pool_w=m_pool_w,
                     pool_scale=m_pool_scale, sconv_w=m_sconv_w, sconv_b=m_sconv_b, w_in_odd=m_w_in_odd,
                     w_out_odd=m_w_out_odd, sgu_ln_g=m_sgu_ln_g, sgu_ln_b=m_sgu_ln_b, sgu_w=m_sgu_w, sgu_b=m_sgu_b,
                     dconv_w=m_dconv_w, dconv_b=m_dconv_b, dnorm_g=m_dnorm_g, dnorm_b=m_dnorm_b)
    moments_v = dict(ln_g=v_ln_g, ln_b=v_ln_b, w_in_even=v_w_in_even, w_out_even=v_w_out_even, pool_w=v_pool_w,
                     pool_scale=v_pool_scale, sconv_w=v_sconv_w, sconv_b=v_sconv_b, w_in_odd=v_w_in_odd,
                     w_out_odd=v_w_out_odd, sgu_ln_g=v_sgu_ln_g, sgu_ln_b=v_sgu_ln_b, sgu_w=v_sgu_w, sgu_b=v_sgu_b,
                     dconv_w=v_dconv_w, dconv_b=v_dconv_b, dnorm_g=v_dnorm_g, dnorm_b=v_dnorm_b)
    names = list(weights)

    xd, yd, cd = _mesh_pos()
    chip = 2 * xd + yd
    pos_arr = jnp.stack([cd, chip]).astype(jnp.int32)

    win_sh = jnp.concatenate([w_in_even, w_in_odd], axis=0).astype(BF16)
    wout_sh = jnp.concatenate([w_out_even, w_out_odd], axis=0).astype(BF16)
    small_sh = jnp.concatenate(
        [sconv_w.reshape(6, HEAD), sgu_ln_g, sgu_ln_b, dconv_b, dnorm_g, dnorm_b, dconv_w.reshape(62, HEAD),
         jnp.zeros((2, HEAD), F32), pool_w.reshape(512, HEAD)], axis=0)
    win_first, wout_first, small_g = _gather_weights(win_sh[0:1], wout_sh[0:1], small_sh)
    rest_shapes = [jax.ShapeDtypeStruct((NQ, 2, NL - 1, D // 2, WQ), BF16),
                   jax.ShapeDtypeStruct((NQ, 2, NL - 1, RQ // 2, D), BF16)]
    g_send, g_recv, g_srcs, g_lands, g_token = _split_start(
        _gather_rest_copies(True), "gather_rest_start", [win_sh[1:], wout_sh[1:]], rest_shapes)

    def layer_weights(slot):
        return (win_first, wout_first, 0) if slot == 0 else (win_rest, wout_rest, slot - 1)

    def full_rows(lo, n):
        return jnp.transpose(small_g[:, lo:lo + n], (1, 0, 2)).reshape(n, D)

    sconv_w_f = full_rows(Q_SCONV_W, 6).reshape(2, SHORT_K, D)
    sln_g_f = full_rows(Q_SLN_G, 2)
    sln_b_f = full_rows(Q_SLN_B, 2)
    dconv_b_f = full_rows(Q_DCONV_B, 2)
    dn_g_f = full_rows(Q_DN_G, 2)
    dn_b_f = full_rows(Q_DN_B, 2)
    dconv_w_f = full_rows(Q_DCONV_W, 62).reshape(2, CONV_K, D)
    pool_w_f = jnp.transpose(small_g[:, Q_POOL_W:].reshape(NQ, 2, 4, 64, HEAD), (1, 2, 0, 3, 4)).reshape(2, 4, HEAD, HEAD)
    pool_w_b = pool_w_f.astype(BF16)
    pool_wt_b = jnp.swapaxes(pool_w_f, 2, 3).astype(BF16)
    idx = jnp.arange(SGU_BLOCK)
    mask = (idx[None, :] // 64) <= (idx[:, None] // 64)
    ws_f = jnp.where(mask[None, None], sgu_w, 0.0)
    ws_b = ws_f.astype(BF16)
    wst_b = jnp.swapaxes(ws_f, 2, 3).astype(BF16)

    def row(a, i):
        return a[i:i + 1]

    x_f = x[0]
    x_b = x_f.astype(BF16)
    saved = []
    for l in range(NL):
        i, slot = l // 2, _layer_slot(l)
        if l == 1:
            g_srcs, g_lands = _split_wait(_gather_rest_copies(False), "gather_rest_wait", g_send, g_recv,
                                          g_srcs, g_lands, x_b)
            win_rest, wout_rest = _gather_rest_forward(g_srcs[0], g_srcs[1], g_lands[0], g_lands[1])
        win_g, wout_g, k = layer_weights(slot)
        z = _proj_in(x_b, win_g, k, l, g_token if l == 0 else None)
        if l % 2 == 0:
            ycat = _even_fwd(z, pool_w_b[i], row(pool_scale, i), sconv_w_f[i], row(sconv_b, i), l)
        else:
            ycat = _odd_fwd(z, row(sln_g_f, i), row(sln_b_f, i), ws_b[i], _sgu_bias_rows(sgu_b[i]),
                            dconv_w_f[i], row(dconv_b_f, i), row(dn_g_f, i), row(dn_b_f, i), l)
        x_next, x_next_b, xhat, rstd = _proj_out_ln(ycat, wout_g, k, l, x_f, row(ln_g, l), row(ln_b, l))
        saved.append((x_b, z, ycat, xhat, rstd))
        x_f, x_b = x_next, x_next_b

    loss_part, dxn = _loss_grad(x_f, loss_target[0])
    loss = lax.psum(loss_part[0, 0], ("x", "y", "c"))

    small = {}
    d_ln_g = [None] * NL
    d_ln_b = [None] * NL
    large = {"w_in_even": None, "w_out_even": None, "w_in_odd": None, "w_out_odd": None}
    pending = None
    token = None

    def finish(exchange, after):
        lx, send, recv, srcs, lands, own_in, own_out = exchange
        _, (r_in, r_out) = _split_wait(_chip_copies(False), f"chip_wait_{lx}", send, recv, srcs, lands, after)
        fin = _chip_sum(own_in, r_in, f"chip_sum_in_{lx}")
        fout = _chip_sum(own_out, r_out, f"chip_sum_out_{lx}")
        gs_in, gs_out = _pair_share(fin, fout, lx)
        kind = "even" if lx % 2 == 0 else "odd"
        for nm, gs in ((f"w_in_{kind}", gs_in), (f"w_out_{kind}", gs_out)):
            large[nm] = _adamw_large(weights[nm], moments_m[nm], moments_v[nm], gs, lx // 2, large[nm],
                                     f"adamw_{nm}_{lx // 2}")

    for l in reversed(range(NL)):
        i, slot = l // 2, _layer_slot(l)
        win_g, wout_g, k = layer_weights(slot)
        xin_b, z, ycat, xhat, rstd = saved[l]
        dr, dr_b, d_ln_g[l], d_ln_b[l] = _ln_bwd(dxn, xhat, rstd, row(ln_g, l), l, token)
        dy = _dycat(dr_b, wout_g, k, l)
        gout = _dwout(ycat, dr_b, l).reshape(NQ, RQ, D)
        if l % 2 == 0:
            dz, d_pw, d_ps, d_cw, d_cb = _even_bwd(z, dy, pool_w_b[i], pool_wt_b[i], row(pool_scale, i),
                                                   sconv_w_f[i], row(sconv_b, i), l)
            small[("pool_w", i)] = d_pw
            small[("pool_scale", i)] = d_ps
            small[("sconv_w", i)] = d_cw
            small[("sconv_b", i)] = d_cb
        else:
            dz, d_lg, d_lb, d_ws, d_sb, d_cw, d_cb, d_ng, d_nb = _odd_bwd(
                z, dy, row(sln_g_f, i), row(sln_b_f, i), ws_b[i], wst_b[i], _sgu_bias_rows(sgu_b[i]),
                dconv_w_f[i], row(dconv_b_f, i), row(dn_g_f, i), row(dn_b_f, i), l)
            small[("sgu_ln_g", i)] = d_lg
            small[("sgu_ln_b", i)] = d_lb
            small[("sgu_w", i)] = jnp.where(mask[None], d_ws, 0.0)
            small[("sgu_b", i)] = d_sb
            small[("dconv_w", i)] = d_cw
            small[("dconv_b", i)] = d_cb
            small[("dnorm_g", i)] = d_ng
            small[("dnorm_b", i)] = d_nb
        gin = _dwin(xin_b, dz, l)
        dxn = _dx(dz, win_g, k, l, dr)
        if pending is not None:
            finish(pending, dxn)
        rin, rout = _pair_exchange(gin, gout, l)
        pin_b, pin_own = _pair_sum(gin, rin, pos_arr, f"pair_sum_in_{l}")
        pout_b, pout_own = _pair_sum(gout, rout, pos_arr, f"pair_sum_out_{l}")
        send, recv, srcs, lands, token = _split_start(
            _chip_copies(True), f"chip_start_{l}", [pin_b, pout_b],
            [jax.ShapeDtypeStruct((3,) + pin_b.shape[1:], BF16), jax.ShapeDtypeStruct((3,) + pout_b.shape[1:], BF16)])
        pending = (l, send, recv, srcs, lands, pin_own, pout_own)
    grad_x = dxn[None]

    def both(name, first, step):
        return [(first, small[(name, 0)]), (first + step, small[(name, 1)])]

    vectors = ([(R_LN_G + l, d_ln_g[l]) for l in range(NL)] + [(R_LN_B + l, d_ln_b[l]) for l in range(NL)]
               + both("pool_scale", R_PSCALE, 1) + both("sconv_b", R_SCONV_B, 1) + both("sconv_w", R_SCONV_W, SHORT_K)
               + both("sgu_ln_g", R_SLN_G, 1) + both("sgu_ln_b", R_SLN_B, 1) + both("dconv_b", R_DCONV_B, 1)
               + both("dnorm_g", R_DN_G, 1) + both("dnorm_b", R_DN_B, 1) + both("dconv_w", R_DCONV_W, CONV_K)
               + both("sgu_b", R_SGU_B, 4))
    sgu_w_rows = 4 * SGU_BLOCK
    pool_w_rows = 4 * HEAD
    total, total_sgu_w, total_pool_w = _allreduce_small(
        [(R_VECTORS, D, vectors),
         (2 * sgu_w_rows, SGU_BLOCK, [(i * sgu_w_rows, small[("sgu_w", i)].reshape(sgu_w_rows, SGU_BLOCK)) for i in range(2)]),
         (2 * pool_w_rows, HEAD, [(i * pool_w_rows, small[("pool_w", i)].reshape(pool_w_rows, HEAD)) for i in range(2)])],
        token)
    finish(pending, total)

    def mine(a):
        return lax.dynamic_slice_in_dim(a, chip * HEAD, HEAD, axis=a.ndim - 1)

    grads = {
        "ln_g": total[R_LN_G:R_LN_G + 4],
        "ln_b": total[R_LN_B:R_LN_B + 4],
        "pool_scale": total[R_PSCALE:R_PSCALE + 2],
        "sconv_b": total[R_SCONV_B:R_SCONV_B + 2],
        "sconv_w": mine(total[R_SCONV_W:R_SCONV_W + 6].reshape(2, SHORT_K, D)),
        "sgu_ln_g": mine(total[R_SLN_G:R_SLN_G + 2]),
        "sgu_ln_b": mine(total[R_SLN_B:R_SLN_B + 2]),
        "dconv_b": mine(total[R_DCONV_B:R_DCONV_B + 2]),
        "dnorm_g": mine(total[R_DN_G:R_DN_G + 2]),
        "dnorm_b": mine(total[R_DN_B:R_DN_B + 2]),
        "dconv_w": mine(total[R_DCONV_W:R_DCONV_W + 62].reshape(2, CONV_K, D)),
        "sgu_b": total[R_SGU_B:R_SGU_B + 8, 0:SGU_BLOCK].reshape(2, 4, SGU_BLOCK),
        "sgu_w": total_sgu_w.reshape(2, 4, SGU_BLOCK, SGU_BLOCK),
        "pool_w": lax.dynamic_slice_in_dim(total_pool_w.reshape(2, 4, HEAD, HEAD), chip * 64, 64, axis=2),
    }

    deltas, new_m, new_v = {}, {}, {}
    for name in names:
        if name in large:
            grads[name], deltas[name], new_m[name], new_v[name] = large[name]
        else:
            deltas[name], new_m[name], new_v[name] = _adamw(
                weights[name], grads[name], moments_m[name], moments_v[name], f"adamw_{name}")

    return (loss, grad_x, *[grads[n] for n in names], *[deltas[n] for n in names],
            *[new_m[n] for n in names], *[new_v[n] for n in names])
```

```python
import jax
import jax.numpy as jnp
from jax import lax
from jax.experimental import pallas as pl
from jax.experimental.pallas import tpu as pltpu

F32 = jnp.float32
BF16 = jnp.bfloat16
MXU_DTYPE = BF16

D = 1024
DZ = 6144
DY = 2048
NQ = 4
WQ = DZ // NQ
RQ = DY // NQ
NL = 4
ALPHA = (2 * NL) ** 0.25
LN_EPS = 1e-5
CONV_K = 31
SHORT_K = 3
SGU_BLOCK = 128
HEAD = 256
POOL_HALO = 16
CONV_HALO = 32
LANE = 128
MIB = 1024 * 1024

ADAM_LR = 0.001
ADAM_B1 = 0.9
ADAM_B2 = 0.999
ADAM_EPS = 1e-08
ADAM_WD = 0.01
ADAM_STEP = 10

NN = ((1,), (0,))
NT = ((1,), (1,))
TN = ((0,), (0,))
MESH = pl.DeviceIdType.MESH
HBM_SPEC = pl.BlockSpec(memory_space=pltpu.HBM)
SEM_SPEC = pl.BlockSpec(memory_space=pltpu.SEMAPHORE)
SIDE_EFFECT = pltpu.SideEffectType.DATAFLOW_SIDE_EFFECTING

R_LN_G, R_LN_B, R_PSCALE, R_SCONV_B, R_SCONV_W = 0, 4, 8, 10, 12
R_SLN_G, R_SLN_B, R_DCONV_B, R_DN_G, R_DN_B, R_DCONV_W = 18, 20, 22, 24, 26, 28
R_SGU_B, R_VECTORS = 90, 104
Q_SCONV_W, Q_SLN_G, Q_SLN_B, Q_DCONV_B, Q_DN_G, Q_DN_B, Q_DCONV_W, Q_POOL_W, Q_ROWS = 0, 6, 8, 10, 12, 14, 16, 80, 592


def _dot(a, b, dims):
    return lax.dot_general(a.astype(MXU_DTYPE), b.astype(MXU_DTYPE), (dims, ((), ())),
                           preferred_element_type=F32)


def _params(semantics=None, vmem_mib=48):
    return pltpu.CompilerParams(dimension_semantics=semantics, vmem_limit_bytes=vmem_mib * MIB)


def _sigmoid(v):
    return jax.nn.sigmoid(v)


def _silu_and_grad(v):
    s = _sigmoid(v)
    return v * s, s * (1.0 + v * (1.0 - s))


def _ln_stats(v):
    mu = jnp.mean(v, axis=-1, keepdims=True)
    vc = v - mu
    var = jnp.mean(vc * vc, axis=-1, keepdims=True)
    rstd = lax.rsqrt(var + LN_EPS)
    return vc * rstd, rstd


def _ln_bwd_rows(dxhat, xhat, rstd):
    m1 = jnp.mean(dxhat, axis=-1, keepdims=True)
    m2 = jnp.mean(dxhat * xhat, axis=-1, keepdims=True)
    return rstd * (dxhat - m1 - xhat * m2)


def _colsum(v):
    return jnp.sum(v, axis=0, keepdims=True)


def _adamw_math(w, g, m, v):
    m_new = ADAM_B1 * m + (1.0 - ADAM_B1) * g
    v_new = ADAM_B2 * v + (1.0 - ADAM_B2) * (g * g)
    m_hat = m_new / (1.0 - ADAM_B1 ** ADAM_STEP)
    v_hat = v_new / (1.0 - ADAM_B2 ** ADAM_STEP)
    return -ADAM_LR * (m_hat / (jnp.sqrt(v_hat) + ADAM_EPS) + ADAM_WD * w), m_new, v_new


def _mesh_pos():
    return lax.axis_index("x"), lax.axis_index("y"), lax.axis_index("c")


def _after_spec(after):
    return [] if after is None else [pl.BlockSpec(memory_space=pl.ANY)]


def _after_args(after):
    return [] if after is None else [after]


def _proj_in(xb, win_g, k, l, after=None):
    s = xb.shape[0]
    tm = min(s, 1024)

    def body(x_ref, w_ref, *rest):
        rest[-1][...] = _dot(x_ref[...], w_ref[...].reshape(D, WQ), NN)

    return pl.pallas_call(
        body, name=f"proj_in_{l}", grid=(NQ, s // tm),
        in_specs=[pl.BlockSpec((tm, D), lambda q, m: (m, 0)),
                  pl.BlockSpec((None, 2, None, D // 2, WQ), lambda q, m: (q, 0, k, 0, 0))] + _after_spec(after),
        out_specs=pl.BlockSpec((tm, WQ), lambda q, m: (m, q)),
        out_shape=jax.ShapeDtypeStruct((s, DZ), F32),
        compiler_params=_params(("arbitrary", "arbitrary")),
    )(xb, win_g, *_after_args(after))


def _proj_out_ln(ycat, wout_g, k, l, x, g, b):
    s = x.shape[0]
    tm = min(s, 512)

    def body(y_ref, w_ref, x_ref, g_ref, b_ref, xn_ref, xb_ref, xh_ref, rs_ref):
        y = _dot(y_ref[...], w_ref[...].reshape(DY, D), NN)
        xhat, rstd = _ln_stats(ALPHA * x_ref[...] + y)
        xn = xhat * g_ref[...] + b_ref[...]
        xn_ref[...] = xn
        xb_ref[...] = xn.astype(BF16)
        xh_ref[...] = xhat
        rs_ref[...] = rstd

    row = lambda m: (m, 0)
    fixed = lambda m: (0, 0)
    return pl.pallas_call(
        body, name=f"proj_out_ln_{l}", grid=(s // tm,),
        in_specs=[pl.BlockSpec((tm, DY), row),
                  pl.BlockSpec((NQ, 2, None, RQ // 2, D), lambda m: (0, 0, k, 0, 0)),
                  pl.BlockSpec((tm, D), row), pl.BlockSpec((1, D), fixed), pl.BlockSpec((1, D), fixed)],
        out_specs=[pl.BlockSpec((tm, D), row), pl.BlockSpec((tm, D), row), pl.BlockSpec((tm, D), row),
                   pl.BlockSpec((tm, 1), row)],
        out_shape=[jax.ShapeDtypeStruct((s, D), F32), jax.ShapeDtypeStruct((s, D), BF16),
                   jax.ShapeDtypeStruct((s, D), F32), jax.ShapeDtypeStruct((s, 1), F32)],
        compiler_params=_params(("arbitrary",)),
    )(ycat, wout_g, x, g, b)


def _loss_grad(xl, target):
    s = xl.shape[0]
    ts = min(s, 512)

    def body(x_ref, t_ref, loss_ref, dx_ref):
        @pl.when(pl.program_id(0) == 0)
        def _():
            loss_ref[...] = jnp.zeros_like(loss_ref)
        err = x_ref[...] - t_ref[...]
        dx_ref[...] = err * (1.0 / D)
        loss_ref[...] += 0.5 * jnp.sum(jnp.mean(err * err, axis=-1, keepdims=True), axis=0, keepdims=True)

    row = lambda m: (m, 0)
    return pl.pallas_call(
        body, name="loss_grad", grid=(s // ts,),
        in_specs=[pl.BlockSpec((ts, D), row), pl.BlockSpec((ts, D), row)],
        out_specs=[pl.BlockSpec((1, 1), lambda m: (0, 0)), pl.BlockSpec((ts, D), row)],
        out_shape=[jax.ShapeDtypeStruct((1, 1), F32), jax.ShapeDtypeStruct((s, D), F32)],
        compiler_params=_params(("arbitrary",)),
    )(xl, target)


def _ln_bwd(dxn, xhat, rstd, g, l, after=None):
    s = dxn.shape[0]
    ts = min(s, 512)

    def body(d_ref, xh_ref, rs_ref, g_ref, *rest):
        dr_ref, drb_ref, dg_ref, db_ref = rest[-4:]

        @pl.when(pl.program_id(0) == 0)
        def _():
            dg_ref[...] = jnp.zeros_like(dg_ref)
            db_ref[...] = jnp.zeros_like(db_ref)
        d = d_ref[...]
        xhat_v = xh_ref[...]
        dr = _ln_bwd_rows(d * g_ref[...], xhat_v, rs_ref[...])
        dr_ref[...] = dr
        drb_ref[...] = dr.astype(BF16)
        dg_ref[...] += _colsum(d * xhat_v)
        db_ref[...] += _colsum(d)

    row = lambda m: (m, 0)
    fixed = lambda m: (0, 0)
    return pl.pallas_call(
        body, name=f"ln_bwd_{l}", grid=(s // ts,),
        in_specs=[pl.BlockSpec((ts, D), row), pl.BlockSpec((ts, D), row), pl.BlockSpec((ts, 1), row),
                  pl.BlockSpec((1, D), fixed)] + _after_spec(after),
        out_specs=[pl.BlockSpec((ts, D), row), pl.BlockSpec((ts, D), row), pl.BlockSpec((1, D), fixed),
                   pl.BlockSpec((1, D), fixed)],
        out_shape=[jax.ShapeDtypeStruct((s, D), F32), jax.ShapeDtypeStruct((s, D), BF16),
                   jax.ShapeDtypeStruct((1, D), F32), jax.ShapeDtypeStruct((1, D), F32)],
        compiler_params=_params(("arbitrary",)),
    )(dxn, xhat, rstd, g, *_after_args(after))


def _dycat(drb, wout_g, k, l):
    s = drb.shape[0]
    tm = min(s, 512)

    def body(d_ref, w_ref, o_ref):
        o_ref[...] = _dot(d_ref[...], w_ref[...].reshape(DY, D), NT)

    return pl.pallas_call(
        body, name=f"dycat_{l}", grid=(s // tm,),
        in_specs=[pl.BlockSpec((tm, D), lambda m: (m, 0)),
                  pl.BlockSpec((NQ, 2, None, RQ // 2, D), lambda m: (0, 0, k, 0, 0))],
        out_specs=pl.BlockSpec((tm, DY), lambda m: (m, 0)),
        out_shape=jax.ShapeDtypeStruct((s, DY), F32),
        compiler_params=_params(("arbitrary",)),
    )(drb, wout_g)


def _dwout(ycat, drb, l):
    s = drb.shape[0]
    tk = min(s, 512)

    def body(y_ref, d_ref, o_ref):
        part = _dot(y_ref[...], d_ref[...], TN)

        @pl.when(pl.program_id(0) == 0)
        def _():
            o_ref[...] = part

        @pl.when(pl.program_id(0) > 0)
        def _():
            o_ref[...] += part

    return pl.pallas_call(
        body, name=f"dwout_{l}", grid=(s // tk,),
        in_specs=[pl.BlockSpec((tk, DY), lambda k: (k, 0)), pl.BlockSpec((tk, D), lambda k: (k, 0))],
        out_specs=pl.BlockSpec((DY, D), lambda k: (0, 0)),
        out_shape=jax.ShapeDtypeStruct((DY, D), F32),
        compiler_params=_params(("arbitrary",)),
    )(ycat, drb)


def _dwin(xb, dzb, l):
    s = xb.shape[0]
    tk = min(s, 1024)

    def body(x_ref, d_ref, o_ref):
        part = _dot(x_ref[...], d_ref[...], TN)

        @pl.when(pl.program_id(1) == 0)
        def _():
            o_ref[...] = part

        @pl.when(pl.program_id(1) > 0)
        def _():
            o_ref[...] += part

    return pl.pallas_call(
        body, name=f"dwin_{l}", grid=(NQ, s // tk),
        in_specs=[pl.BlockSpec((tk, D), lambda q, k: (k, 0)), pl.BlockSpec((tk, WQ), lambda q, k: (k, q))],
        out_specs=pl.BlockSpec((None, D, WQ), lambda q, k: (q, 0, 0)),
        out_shape=jax.ShapeDtypeStruct((NQ, D, WQ), F32),
        compiler_params=_params(("arbitrary", "arbitrary")),
    )(xb, dzb)


def _dx(dzb, win_g, k, l, dr):
    s = dzb.shape[0]
    tm = min(s, 1024)

    def body(d_ref, w_ref, r_ref, o_ref):
        part = _dot(d_ref[...], w_ref[...].reshape(D, WQ), NT)

        @pl.when(pl.program_id(1) == 0)
        def _():
            o_ref[...] = ALPHA * r_ref[...] + part

        @pl.when(pl.program_id(1) > 0)
        def _():
            o_ref[...] += part

    return pl.pallas_call(
        body, name=f"dx_{l}", grid=(s // tm, NQ),
        in_specs=[pl.BlockSpec((tm, WQ), lambda m, q: (m, q)),
                  pl.BlockSpec((None, 2, None, D // 2, WQ), lambda m, q: (q, 0, k, 0, 0)),
                  pl.BlockSpec((tm, D), lambda m, q: (m, 0))],
        out_specs=pl.BlockSpec((tm, D), lambda m, q: (m, 0)),
        out_shape=jax.ShapeDtypeStruct((s, D), F32),
        compiler_params=_params(("arbitrary", "arbitrary")),
    )(dzb, win_g, dr)


def _window_sums(e, causal):
    n = e.shape[0]

    def shifted(a, k):
        return pltpu.roll(a, k if causal else n - k, axis=0)

    parts = []
    acc = e
    for step, k in enumerate((1, 2, 4, 8)):
        acc = acc + shifted(acc, k)
        parts.append(acc[:, 0:HEAD])
        if step < 3:
            acc = acc[:, HEAD:]
    return jnp.concatenate(parts, axis=1)


def _pool_counts(first_pos, rows):
    t1 = (lax.broadcasted_iota(jnp.int32, (rows, 1), 0) + first_pos + 1).astype(F32)
    lane = lax.broadcasted_iota(jnp.int32, (1, D), 1)
    win = jnp.where(lane < HEAD, 2.0, jnp.where(lane < 2 * HEAD, 4.0, jnp.where(lane < 3 * HEAD, 8.0, 16.0)))
    return jnp.minimum(t1, win)


def _group_dot(v, w_ref, dims):
    return jnp.concatenate(
        [_dot(v[:, g * HEAD:(g + 1) * HEAD], w_ref[g], dims) for g in range(4)], axis=1)


def _prev_index(ts, halo):
    return lambda i: (jnp.maximum(i * (ts // halo) - 1, 0), 0)


def _next_index(ts, halo, s):
    return lambda i: (jnp.minimum((i + 1) * (ts // halo), s // halo - 1), 0)


def _even_fwd(z, pool_w, pool_scale, sconv_w, sconv_b, l):
    s = z.shape[0]
    ts = min(s, 256)
    h = POOL_HALO

    def body(z_ref, zp_ref, pw_ref, ps_ref, cw_ref, cb_ref, o_ref):
        i = pl.program_id(0)
        inside = i > 0
        xa = z_ref[:, 0:D]
        xa_ext = jnp.concatenate([jnp.where(inside, zp_ref[:, 0:D], 0.0), xa], axis=0)
        sums = _window_sums(xa_ext, True)[h:]
        pooled = sums / _pool_counts(i * ts, ts) - xa
        p = _group_dot(pooled, pw_ref, NN)
        silu_ga, _ = _silu_and_grad(z_ref[:, D:2 * D])
        o_ref[:, 0:D] = (p * ps_ref[...] * silu_ga).astype(BF16)

        q_main = z_ref[:, 4 * D:5 * D] * z_ref[:, 2 * D:3 * D]
        q_prev = jnp.where(inside, zp_ref[:, 4 * D:5 * D] * zp_ref[:, 2 * D:3 * D], 0.0)
        q_ext = jnp.concatenate([q_prev, q_main], axis=0)
        cv = cw_ref[2:3, :] * q_main + cb_ref[...]
        cv = cv + cw_ref[1:2, :] * pltpu.roll(q_ext, 1, axis=0)[h:]
        cv = cv + cw_ref[0:1, :] * pltpu.roll(q_ext, 2, axis=0)[h:]
        silu_gb, _ = _silu_and_grad(z_ref[:, 5 * D:6 * D])
        o_ref[:, D:2 * D] = (z_ref[:, 3 * D:4 * D] * cv * silu_gb).astype(BF16)

    fixed2 = lambda i: (0, 0)
    return pl.pallas_call(
        body, name=f"even_fwd_{l}", grid=(s // ts,),
        in_specs=[pl.BlockSpec((ts, DZ), lambda i: (i, 0)), pl.BlockSpec((h, DZ), _prev_index(ts, h)),
                  pl.BlockSpec((4, HEAD, HEAD), lambda i: (0, 0, 0)), pl.BlockSpec((1, D), fixed2),
                  pl.BlockSpec((SHORT_K, D), fixed2), pl.BlockSpec((1, D), fixed2)],
        out_specs=pl.BlockSpec((ts, DY), lambda i: (i, 0)),
        out_shape=jax.ShapeDtypeStruct((s, DY), BF16),
        compiler_params=_params(("arbitrary",)),
    )(z, z, pool_w, pool_scale, sconv_w, sconv_b)


def _even_bwd(z, dy, pool_w, pool_wt, pool_scale, sconv_w, sconv_b, l):
    s = z.shape[0]
    ts = min(s, 256)
    h = POOL_HALO
    n_tiles = s // ts

    def body(z_ref, zp_ref, zn_ref, dy_ref, dyn_ref, pw_ref, pwt_ref, ps_ref, cw_ref, cb_ref,
             dz_ref, dpw_ref, dps_ref, dcw_ref, dcb_ref):
        i = pl.program_id(0)
        inside = i > 0
        more = i < n_tiles - 1

        @pl.when(i == 0)
        def _():
            dpw_ref[...] = jnp.zeros_like(dpw_ref)
            dps_ref[...] = jnp.zeros_like(dps_ref)
            dcw_ref[...] = jnp.zeros_like(dcw_ref)
            dcb_ref[...] = jnp.zeros_like(dcb_ref)

        def with_next(main_ref, next_ref, lo):
            return jnp.concatenate([main_ref[:, lo:lo + D], next_ref[:, lo:lo + D]], axis=0)

        xa_ext = jnp.concatenate(
            [jnp.where(inside, zp_ref[:, 0:D], 0.0), z_ref[:, 0:D], zn_ref[:, 0:D]], axis=0)
        counts = _pool_counts(i * ts, ts + h)
        pooled = _window_sums(xa_ext, True)[h:] / counts - xa_ext[h:]
        p = _group_dot(pooled, pw_ref, NN)
        silu_ga, dsilu_ga = _silu_and_grad(with_next(z_ref, zn_ref, D))
        d_ya = with_next(dy_ref, dyn_ref, 0)
        scale = ps_ref[...]
        d_p = d_ya * scale * silu_ga
        d_pooled = _group_dot(d_p, pwt_ref, NN)
        row = lax.broadcasted_iota(jnp.int32, (ts + h, 1), 0)
        d_pooled = jnp.where(jnp.logical_or(more, row < ts), d_pooled, 0.0)
        d_xa = _window_sums(d_pooled / counts, False)[:ts] - d_pooled[:ts]
        dz_ref[:, 0:D] = d_xa.astype(BF16)
        dz_ref[:, D:2 * D] = (d_ya[:ts] * p[:ts] * scale * dsilu_ga[:ts]).astype(BF16)
        dps_ref[...] += _colsum(d_ya[:ts] * p[:ts] * silu_ga[:ts])
        for g in range(4):
            cols = slice(g * HEAD, (g + 1) * HEAD)
            dpw_ref[g] += _dot(pooled[:ts, cols], d_p[:ts, cols], TN)

        cg = z_ref[:, 4 * D:5 * D]
        hh = z_ref[:, 2 * D:3 * D]
        bg = z_ref[:, 3 * D:4 * D]
        q_main = cg * hh
        q_prev = jnp.where(inside, zp_ref[:, 4 * D:5 * D] * zp_ref[:, 2 * D:3 * D], 0.0)
        q_ext = jnp.concatenate([q_prev, q_main], axis=0)
        q_1 = pltpu.roll(q_ext, 1, axis=0)[h:]
        q_2 = pltpu.roll(q_ext, 2, axis=0)[h:]
        cv = cw_ref[2:3, :] * q_main + cw_ref[1:2, :] * q_1 + cw_ref[0:1, :] * q_2 + cb_ref[...]
        silu_gb, dsilu_gb = _silu_and_grad(with_next(z_ref, zn_ref, 5 * D))
        d_yb = with_next(dy_ref, dyn_ref, D)
        d_cv = d_yb * with_next(z_ref, zn_ref, 3 * D) * silu_gb
        d_cv = jnp.where(jnp.logical_or(more, row < ts), d_cv, 0.0)
        d_cv0 = d_cv[:ts]
        n_ext = ts + h
        d_q = (cw_ref[2:3, :] * d_cv0 + cw_ref[1:2, :] * pltpu.roll(d_cv, n_ext - 1, axis=0)[:ts]
               + cw_ref[0:1, :] * pltpu.roll(d_cv, n_ext - 2, axis=0)[:ts])
        dz_ref[:, 2 * D:3 * D] = (d_q * cg).astype(BF16)
        dz_ref[:, 3 * D:4 * D] = (d_yb[:ts] * cv * silu_gb[:ts]).astype(BF16)
        dz_ref[:, 4 * D:5 * D] = (d_q * hh).astype(BF16)
        dz_ref[:, 5 * D:6 * D] = (d_yb[:ts] * bg * cv * dsilu_gb[:ts]).astype(BF16)
        dcb_ref[...] += _colsum(d_cv0)
        dcw_ref[2:3, :] += _colsum(d_cv0 * q_main)
        dcw_ref[1:2, :] += _colsum(d_cv0 * q_1)
        dcw_ref[0:1, :] += _colsum(d_cv0 * q_2)

    fixed2 = lambda i: (0, 0)
    fixed3 = lambda i: (0, 0, 0)
    return pl.pallas_call(
        body, name=f"even_bwd_{l}", grid=(n_tiles,),
        in_specs=[pl.BlockSpec((ts, DZ), lambda i: (i, 0)), pl.BlockSpec((h, DZ), _prev_index(ts, h)),
                  pl.BlockSpec((h, DZ), _next_index(ts, h, s)),
                  pl.BlockSpec((ts, DY), lambda i: (i, 0)), pl.BlockSpec((h, DY), _next_index(ts, h, s)),
                  pl.BlockSpec((4, HEAD, HEAD), fixed3), pl.BlockSpec((4, HEAD, HEAD), fixed3),
                  pl.BlockSpec((1, D), fixed2), pl.BlockSpec((SHORT_K, D), fixed2), pl.BlockSpec((1, D), fixed2)],
        out_specs=[pl.BlockSpec((ts, DZ), lambda i: (i, 0)), pl.BlockSpec((4, HEAD, HEAD), fixed3),
                   pl.BlockSpec((1, D), fixed2), pl.BlockSpec((SHORT_K, D), fixed2), pl.BlockSpec((1, D), fixed2)],
        out_shape=[jax.ShapeDtypeStruct((s, DZ), BF16), jax.ShapeDtypeStruct((4, HEAD, HEAD), F32),
                   jax.ShapeDtypeStruct((1, D), F32), jax.ShapeDtypeStruct((SHORT_K, D), F32),
                   jax.ShapeDtypeStruct((1, D), F32)],
        compiler_params=_params(("arbitrary",), 56),
    )(z, z, z, dy, dy, pool_w, pool_wt, pool_scale, sconv_w, sconv_b)


def _conv31(src_ref, w_ref, dst_ref, base, rows, causal):
    def block(cb, carry):
        cols = pl.ds(pl.multiple_of(cb * LANE, LANE), LANE)
        acc = None
        for d in range(CONV_K):
            start = base - d if causal else base + d
            term = w_ref[CONV_K - 1 - d:CONV_K - d, cols] * src_ref[pl.ds(start, rows), cols]
            acc = term if acc is None else acc + term
        dst_ref[pl.ds(0, rows), cols] = acc
        return carry

    lax.fori_loop(0, D // LANE, block, 0)


def _sgu_bias_rows(sgu_b):
    return jnp.repeat(jnp.transpose(sgu_b), HEAD, axis=1)


def _odd_fwd(z, sln_g, sln_b, ws, sbias, dconv_w, dconv_b, dn_g, dn_b, l):
    s = z.shape[0]
    ts = min(s, 256)
    h = CONV_HALO

    def body(z_ref, zp_ref, lg_ref, lb_ref, ws_ref, sb_ref, cw_ref, cb_ref, ng_ref, nb_ref, o_ref, zz_ref, zc_ref):
        i = pl.program_id(0)
        vhat, _ = _ln_stats(z_ref[:, D:2 * D])
        vn = (vhat * lg_ref[...] + lb_ref[...]).astype(MXU_DTYPE)
        silu_gc, _ = _silu_and_grad(z_ref[:, 2 * D:3 * D])
        for n in range(ts // SGU_BLOCK):
            rows = slice(n * SGU_BLOCK, (n + 1) * SGU_BLOCK)
            sv = jnp.concatenate(
                [_dot(ws_ref[hd], vn[rows, hd * HEAD:(hd + 1) * HEAD], NN) for hd in range(4)], axis=1)
            sv = sv + sb_ref[...]
            o_ref[rows, 0:D] = (z_ref[rows, 0:D] * sv * silu_gc[rows]).astype(BF16)

        zz_ref[0:h, :] = jnp.where(i > 0, zp_ref[:, 3 * D:4 * D] * _sigmoid(zp_ref[:, 4 * D:5 * D]), 0.0)
        zz_ref[h:h + ts, :] = z_ref[:, 3 * D:4 * D] * _sigmoid(z_ref[:, 4 * D:5 * D])
        _conv31(zz_ref, cw_ref, zc_ref, h, ts, True)
        zhat, _ = _ln_stats(zc_ref[...] + cb_ref[...])
        silu_zn, _ = _silu_and_grad(zhat * ng_ref[...] + nb_ref[...])
        silu_gd, _ = _silu_and_grad(z_ref[:, 5 * D:6 * D])
        o_ref[:, D:2 * D] = (silu_zn * silu_gd).astype(BF16)

    fixed2 = lambda i: (0, 0)
    vec = pl.BlockSpec((1, D), fixed2)
    return pl.pallas_call(
        body, name=f"odd_fwd_{l}", grid=(s // ts,),
        in_specs=[pl.BlockSpec((ts, DZ), lambda i: (i, 0)), pl.BlockSpec((h, DZ), _prev_index(ts, h)),
                  vec, vec, pl.BlockSpec((4, SGU_BLOCK, SGU_BLOCK), lambda i: (0, 0, 0)),
                  pl.BlockSpec((SGU_BLOCK, D), fixed2), pl.BlockSpec((CONV_K, D), fixed2), vec, vec, vec],
        out_specs=pl.BlockSpec((ts, DY), lambda i: (i, 0)),
        out_shape=jax.ShapeDtypeStruct((s, DY), BF16),
        scratch_shapes=[pltpu.VMEM((h + ts, D), F32), pltpu.VMEM((ts, D), F32)],
        compiler_params=_params(("arbitrary",)),
    )(z, z, sln_g, sln_b, ws, sbias, dconv_w, dconv_b, dn_g, dn_b)


def _odd_bwd(z, dy, sln_g, sln_b, ws, wst, sbias, dconv_w, dconv_b, dn_g, dn_b, l):
    s = z.shape[0]
    ts = min(s, 256)
    h = CONV_HALO
    n_tiles = s // ts
    te = ts + h

    def body(z_ref, zp_ref, zn_ref, dy_ref, dyn_ref, lg_ref, lb_ref, ws_ref, wst_ref, sb_ref, cw_ref, cb_ref,
             ng_ref, nb_ref, dz_ref, dlg_ref, dlb_ref, dws_ref, dsb_ref, dcw_ref, dcb_ref, dng_ref, dnb_ref,
             zz_ref, zc_ref, dzc_ref, dzz_ref, dsb_acc):
        i = pl.program_id(0)
        more = i < n_tiles - 1

        @pl.when(i == 0)
        def _():
            for ref in (dlg_ref, dlb_ref, dws_ref, dsb_ref, dcw_ref, dcb_ref, dng_ref, dnb_ref, dsb_acc):
                ref[...] = jnp.zeros_like(ref)

        vhat, rstd_v = _ln_stats(z_ref[:, D:2 * D])
        lg = lg_ref[...]
        vn = (vhat * lg + lb_ref[...]).astype(MXU_DTYPE)
        u = z_ref[:, 0:D]
        silu_gc, dsilu_gc = _silu_and_grad(z_ref[:, 2 * D:3 * D])
        d_yc = dy_ref[:, 0:D]
        d_yc_u = d_yc * u
        d_sv = d_yc_u * silu_gc
        d_svb = d_sv.astype(MXU_DTYPE)
        sv_rows = []
        dvn_rows = []
        dsb = None
        for n in range(ts // SGU_BLOCK):
            rows = slice(n * SGU_BLOCK, (n + 1) * SGU_BLOCK)
            sv_parts = []
            dvn_parts = []
            for hd in range(4):
                cols = slice(hd * HEAD, (hd + 1) * HEAD)
                sv_parts.append(_dot(ws_ref[hd], vn[rows, cols], NN))
                dvn_parts.append(_dot(wst_ref[hd], d_svb[rows, cols], NN))
                dws_ref[hd] += _dot(d_svb[rows, cols], vn[rows, cols], NT)
            sv_rows.append(jnp.concatenate(sv_parts, axis=1) + sb_ref[...])
            dvn_rows.append(jnp.concatenate(dvn_parts, axis=1))
            dsb = d_sv[rows] if dsb is None else dsb + d_sv[rows]
        dsb_acc[...] += dsb

        @pl.when(i == n_tiles - 1)
        def _():
            for hd in range(4):
                blk = dsb_acc[:, hd * HEAD:(hd + 1) * HEAD]
                folded = blk[:, 0:LANE] + blk[:, LANE:HEAD]
                dsb_ref[hd:hd + 1, :] = _colsum(jnp.transpose(folded))
        sv = jnp.concatenate(sv_rows, axis=0)
        d_vn = jnp.concatenate(dvn_rows, axis=0)
        dz_ref[:, 0:D] = (d_yc * sv * silu_gc).astype(BF16)
        dz_ref[:, D:2 * D] = _ln_bwd_rows(d_vn * lg, vhat, rstd_v).astype(BF16)
        dz_ref[:, 2 * D:3 * D] = (d_yc_u * sv * dsilu_gc).astype(BF16)
        dlg_ref[...] += _colsum(d_vn * vhat)
        dlb_ref[...] += _colsum(d_vn)

        def gate(ref):
            return ref[:, 3 * D:4 * D] * _sigmoid(ref[:, 4 * D:5 * D])

        zz_ref[0:h, :] = jnp.where(i > 0, gate(zp_ref), 0.0)
        zz_ref[h:h + ts, :] = gate(z_ref)
        zz_ref[h + ts:h + te, :] = gate(zn_ref)
        _conv31(zz_ref, cw_ref, zc_ref, h, te, True)
        zhat, rstd_z = _ln_stats(zc_ref[...] + cb_ref[...])
        ng = ng_ref[...]
        silu_zn, dsilu_zn = _silu_and_grad(zhat * ng + nb_ref[...])
        gd = jnp.concatenate([z_ref[:, 5 * D:6 * D], zn_ref[:, 5 * D:6 * D]], axis=0)
        silu_gd, dsilu_gd = _silu_and_grad(gd)
        d_yd = jnp.concatenate([dy_ref[:, D:2 * D], dyn_ref[:, D:2 * D]], axis=0)
        d_zn = d_yd * silu_gd * dsilu_zn
        d_zc = _ln_bwd_rows(d_zn * ng, zhat, rstd_z)
        row = lax.broadcasted_iota(jnp.int32, (te, 1), 0)
        d_zc = jnp.where(jnp.logical_or(more, row < ts), d_zc, 0.0)
        dzc_ref[...] = d_zc
        dz_ref[:, 5 * D:6 * D] = (d_yd[:ts] * silu_zn[:ts] * dsilu_gd[:ts]).astype(BF16)
        dng_ref[...] += _colsum(d_zn[:ts] * zhat[:ts])
        dnb_ref[...] += _colsum(d_zn[:ts])
        dcb_ref[...] += _colsum(d_zc[:ts])
        _conv31(dzc_ref, cw_ref, dzz_ref, 0, ts, False)

        def tap_grads(cb, carry):
            cols = pl.ds(pl.multiple_of(cb * LANE, LANE), LANE)
            d_blk = dzc_ref[pl.ds(0, ts), cols]
            for d in range(CONV_K):
                j = CONV_K - 1 - d
                dcw_ref[j:j + 1, cols] += _colsum(d_blk * zz_ref[pl.ds(h - d, ts), cols])
            return carry

        lax.fori_loop(0, D // LANE, tap_grads, 0)
        d_zz = dzz_ref[...]
        a = z_ref[:, 3 * D:4 * D]
        sig_b = _sigmoid(z_ref[:, 4 * D:5 * D])
        dz_ref[:, 3 * D:4 * D] = (d_zz * sig_b).astype(BF16)
        dz_ref[:, 4 * D:5 * D] = (d_zz * a * sig_b * (1.0 - sig_b)).astype(BF16)

    fixed2 = lambda i: (0, 0)
    fixed3 = lambda i: (0, 0, 0)
    vec = pl.BlockSpec((1, D), fixed2)
    mat = pl.BlockSpec((4, SGU_BLOCK, SGU_BLOCK), fixed3)
    vec_shape = jax.ShapeDtypeStruct((1, D), F32)
    return pl.pallas_call(
        body, name=f"odd_bwd_{l}", grid=(n_tiles,),
        in_specs=[pl.BlockSpec((ts, DZ), lambda i: (i, 0)), pl.BlockSpec((h, DZ), _prev_index(ts, h)),
                  pl.BlockSpec((h, DZ), _next_index(ts, h, s)),
                  pl.BlockSpec((ts, DY), lambda i: (i, 0)), pl.BlockSpec((h, DY), _next_index(ts, h, s)),
                  vec, vec, mat, mat, pl.BlockSpec((SGU_BLOCK, D), fixed2),
                  pl.BlockSpec((CONV_K, D), fixed2), vec, vec, vec],
        out_specs=[pl.BlockSpec((ts, DZ), lambda i: (i, 0)), vec, vec, mat, pl.BlockSpec((4, SGU_BLOCK), fixed2),
                   pl.BlockSpec((CONV_K, D), fixed2), vec, vec, vec],
        out_shape=[jax.ShapeDtypeStruct((s, DZ), BF16), vec_shape, vec_shape,
                   jax.ShapeDtypeStruct((4, SGU_BLOCK, SGU_BLOCK), F32), jax.ShapeDtypeStruct((4, SGU_BLOCK), F32),
                   jax.ShapeDtypeStruct((CONV_K, D), F32), vec_shape, vec_shape, vec_shape],
        scratch_shapes=[pltpu.VMEM((h + te, D), F32), pltpu.VMEM((te, D), F32), pltpu.VMEM((te, D), F32),
                        pltpu.VMEM((ts, D), F32), pltpu.VMEM((SGU_BLOCK, D), F32)],
        compiler_params=_params(("arbitrary",), 56),
    )(z, z, z, dy, dy, sln_g, sln_b, ws, wst, sbias, dconv_w, dconv_b, dn_g, dn_b)


def _remote(src, dst, send_sems, recv_sems, k, to):
    return pltpu.make_async_remote_copy(src_ref=src, dst_ref=dst, send_sem=send_sems.at[k],
                                        recv_sem=recv_sems.at[k], device_id=to, device_id_type=MESH)


def _other_chips(x, y):
    return [(1 - x, y, 2 * (1 - x) + y), (x, 1 - y, 2 * x + 1 - y), (1 - x, 1 - y, 2 * (1 - x) + 1 - y)]


def _cast_own(w_stack, pos_arr, name):
    slots, rows, cols = w_stack.shape
    half = rows // 2

    def body(pos_ref, w_ref, o_ref):
        o_ref[...] = w_ref[...].astype(BF16)

    grid_spec = pltpu.PrefetchScalarGridSpec(
        num_scalar_prefetch=1, grid=(slots, 2),
        in_specs=[pl.BlockSpec((None, half, cols), lambda s, h, pos: (s, h, 0))],
        out_specs=pl.BlockSpec((None, None, None, half, cols), lambda s, h, pos: (pos[1], h, s, 0, 0)))
    return pl.pallas_call(
        body, name=name, grid_spec=grid_spec, out_shape=jax.ShapeDtypeStruct((NQ, 2, slots, half, cols), BF16),
        compiler_params=_params(("arbitrary",) * 2),
    )(pos_arr, w_stack)


def _gather_weights(win_g, wout_g, small_sh):
    def body(win_in, wout_in, small, win_g, wout_g, small_g, send_sems, recv_sems):
        del win_in, wout_in
        x, y, c = _mesh_pos()
        me = 2 * x + y
        sibling = (x, y, 1 - c)
        chips = _other_chips(x, y)

        sends = []
        for j, (cx, cy, _) in enumerate(chips):
            to = (cx, cy, c)
            sends.append(_remote(win_g.at[me, c], win_g.at[me, c], send_sems, recv_sems, j, to))
            sends.append(_remote(wout_g.at[me, c], wout_g.at[me, c], send_sems, recv_sems, 3 + j, to))
            sends.append(_remote(small, small_g.at[me], send_sems, recv_sems, 6 + j, to))
        for cp in sends:
            cp.start()
        passed = []
        for j, (_, _, q) in enumerate(chips):
            got_in = win_g.at[q, c]
            got_out = wout_g.at[q, c]
            _remote(got_in, got_in, send_sems, recv_sems, j, sibling).wait_recv()
            cp = _remote(got_in, got_in, send_sems, recv_sems, 9 + j, sibling)
            cp.start()
            passed.append(cp)
            _remote(got_out, got_out, send_sems, recv_sems, 3 + j, sibling).wait_recv()
            cp = _remote(got_out, got_out, send_sems, recv_sems, 12 + j, sibling)
            cp.start()
            passed.append(cp)
            _remote(small, small_g.at[q], send_sems, recv_sems, 6 + j, sibling).wait_recv()
        for j, (_, _, q) in enumerate(chips):
            from_in = win_g.at[q, 1 - c]
            from_out = wout_g.at[q, 1 - c]
            _remote(from_in, from_in, send_sems, recv_sems, 9 + j, sibling).wait_recv()
            _remote(from_out, from_out, send_sems, recv_sems, 12 + j, sibling).wait_recv()
        for cp in sends + passed:
            cp.wait_send()

    return pl.pallas_call(
        body, name="gather_weights",
        in_specs=[HBM_SPEC, HBM_SPEC, HBM_SPEC], out_specs=[HBM_SPEC, HBM_SPEC, HBM_SPEC],
        out_shape=[jax.ShapeDtypeStruct(win_g.shape, win_g.dtype), jax.ShapeDtypeStruct(wout_g.shape, wout_g.dtype),
                   jax.ShapeDtypeStruct((NQ,) + small_sh.shape, small_sh.dtype)],
        input_output_aliases={0: 0, 1: 1},
        scratch_shapes=[pltpu.SemaphoreType.DMA((15,)), pltpu.SemaphoreType.DMA((15,))],
    )(win_g, wout_g, small_sh)


def _hbm(a):
    return pltpu.with_memory_space_constraint(a, pltpu.HBM)


def _split_start(body, name, sources, landings):
    n_src, n_land = len(sources), len(landings)
    n_buf = n_src + n_land

    def kernel_body(*refs):
        ins, outs = refs[:n_buf], refs[n_buf:]
        send_sems, recv_sems, token = outs[0], outs[1], outs[2 + n_buf]
        body(ins[:n_src], ins[n_src:], send_sems, recv_sems)
        token[...] = jnp.zeros_like(token)

    bufs = [_hbm(a) for a in sources] + [
        _hbm(lax.empty(s.shape, s.dtype) if isinstance(s, jax.ShapeDtypeStruct) else s) for s in landings]
    n_sem = getattr(body, "n_copies")
    out = pl.pallas_call(
        kernel_body, name=name,
        out_shape=(pltpu.SemaphoreType.DMA((n_sem,)), pltpu.SemaphoreType.DMA((n_sem,)),
                   *[pltpu.HBM(b.shape, b.dtype) for b in bufs], jax.ShapeDtypeStruct((8, LANE), F32)),
        in_specs=(HBM_SPEC,) * n_buf,
        out_specs=(SEM_SPEC, SEM_SPEC, *([HBM_SPEC] * n_buf), pl.BlockSpec(memory_space=pltpu.VMEM)),
        input_output_aliases={k: 2 + k for k in range(n_buf)},
        compiler_params=pltpu.CompilerParams(has_side_effects=SIDE_EFFECT),
    )(*bufs)
    return out[0], out[1], list(out[2:2 + n_src]), list(out[2 + n_src:2 + n_buf]), out[2 + n_buf]


def _split_wait(body, name, send_sems, recv_sems, sources, landings, after):
    n_src, n_land = len(sources), len(landings)
    n_buf = n_src + n_land

    def kernel_body(*refs):
        ins = refs[:n_buf]
        body(ins[:n_src], ins[n_src:], refs[n_buf], refs[n_buf + 1])

    bufs = list(sources) + list(landings)
    out = pl.pallas_call(
        kernel_body, name=name,
        out_shape=tuple(pltpu.HBM(b.shape, b.dtype) for b in bufs),
        in_specs=(*([HBM_SPEC] * n_buf), SEM_SPEC, SEM_SPEC, pl.BlockSpec(memory_space=pl.ANY)),
        out_specs=(HBM_SPEC,) * n_buf,
        input_output_aliases={k: k for k in range(n_buf)},
        compiler_params=pltpu.CompilerParams(has_side_effects=SIDE_EFFECT),
    )(*bufs, send_sems, recv_sems, after)
    return list(out[:n_src]), list(out[n_src:])


def _gather_rest_copies(start):
    def body(srcs, lands, send_sems, recv_sems):
        del srcs
        x, y, c = _mesh_pos()
        me = 2 * x + y
        for j, (cx, cy, q) in enumerate(_other_chips(x, y)):
            to = (cx, cy, c)
            for k, gathered in enumerate(lands):
                if start:
                    _remote(gathered.at[me, c], gathered.at[me, c], send_sems, recv_sems, 3 * k + j, to).start()
                else:
                    cp = _remote(gathered.at[me, c], gathered.at[q, c], send_sems, recv_sems, 3 * k + j, to)
                    cp.wait_send()
                    cp.wait_recv()

    body.n_copies = 6
    return body


def _gather_rest_forward(win_g, wout_g):
    def body(win_in, wout_in, win_g, wout_g, send_sems, recv_sems):
        del win_in, wout_in
        x, y, c = _mesh_pos()
        sibling = (x, y, 1 - c)
        passed = []
        for j, (_, _, q) in enumerate(_other_chips(x, y)):
            got_in = win_g.at[q, c]
            got_out = wout_g.at[q, c]
            passed.append(_remote(got_in, got_in, send_sems, recv_sems, j, sibling))
            passed.append(_remote(got_out, got_out, send_sems, recv_sems, 3 + j, sibling))
        for cp in passed:
            cp.start()
        for j, (_, _, q) in enumerate(_other_chips(x, y)):
            from_in = win_g.at[q, 1 - c]
            from_out = wout_g.at[q, 1 - c]
            _remote(from_in, from_in, send_sems, recv_sems, j, sibling).wait_recv()
            _remote(from_out, from_out, send_sems, recv_sems, 3 + j, sibling).wait_recv()
        for cp in passed:
            cp.wait_send()

    return pl.pallas_call(
        body, name="gather_rest_forward",
        in_specs=[HBM_SPEC] * 2, out_specs=[HBM_SPEC] * 2,
        out_shape=[jax.ShapeDtypeStruct(win_g.shape, win_g.dtype), jax.ShapeDtypeStruct(wout_g.shape, wout_g.dtype)],
        input_output_aliases={0: 0, 1: 1},
        scratch_shapes=[pltpu.SemaphoreType.DMA((6,)), pltpu.SemaphoreType.DMA((6,))],
    )(win_g, wout_g)


def _allreduce_small(groups, after):
    pieces = [p for _, _, members in groups for _, p in members]
    n_in, n_g = len(pieces), len(groups)

    def body(*refs):
        ins = refs[:n_in]
        outs = refs[-(2 * n_g + 2):-(n_g + 2)]
        alls = refs[-(n_g + 2):-2]
        send_sems, recv_sems = refs[-2:]
        x, y, c = _mesh_pos()
        me = 4 * x + 2 * y + c
        sibling = (x, y, 1 - c)
        chips = _other_chips(x, y)

        k = 0
        for (rows, cols, members), all_ref in zip(groups, alls):
            all_ref[me] = jnp.zeros((rows, cols), F32)
            for first, piece in members:
                n, width = piece.shape
                all_ref[me, first:first + n, 0:width] = ins[k][...]
                k += 1

        sends, passed = [], []
        for g, all_ref in enumerate(alls):
            sends.append(_remote(all_ref.at[me], all_ref.at[me], send_sems, recv_sems, 7 * g, sibling))
            for j, (cx, cy, _) in enumerate(chips):
                sends.append(_remote(all_ref.at[me], all_ref.at[me], send_sems, recv_sems, 7 * g + 1 + j, (cx, cy, c)))
        for cp in sends:
            cp.start()
        for j, (cx, cy, _) in enumerate(chips):
            for g, all_ref in enumerate(alls):
                got = all_ref.at[4 * cx + 2 * cy + c]
                _remote(got, got, send_sems, recv_sems, 7 * g + 1 + j, sibling).wait_recv()
                cp = _remote(got, got, send_sems, recv_sems, 7 * g + 4 + j, sibling)
                cp.start()
                passed.append(cp)
        for g, all_ref in enumerate(alls):
            got = all_ref.at[4 * x + 2 * y + 1 - c]
            _remote(got, got, send_sems, recv_sems, 7 * g, sibling).wait_recv()
            for j, (cx, cy, _) in enumerate(chips):
                got = all_ref.at[4 * cx + 2 * cy + 1 - c]
                _remote(got, got, send_sems, recv_sems, 7 * g + 4 + j, sibling).wait_recv()
        for cp in sends + passed:
            cp.wait_send()
        for o_ref, all_ref in zip(outs, alls):
            total = all_ref[0]
            for dev in range(1, 8):
                total = total + all_ref[dev]
            o_ref[...] = total

    vmem = pl.BlockSpec(memory_space=pltpu.VMEM)
    return pl.pallas_call(
        body, name="allreduce_small",
        in_specs=[vmem] * n_in + _after_spec(after),
        out_specs=[vmem] * n_g,
        out_shape=[jax.ShapeDtypeStruct((rows, cols), F32) for rows, cols, _ in groups],
        scratch_shapes=[pltpu.VMEM((8, rows, cols), F32) for rows, cols, _ in groups]
        + [pltpu.SemaphoreType.DMA((7 * n_g,)), pltpu.SemaphoreType.DMA((7 * n_g,))],
        compiler_params=_params(None, 56),
    )(*pieces, *_after_args(after))


def _pair_exchange(gin, gout, l):
    hi, ho = D // 2, RQ // 2

    def body(gin_ref, gout_ref, rin_ref, rout_ref, send_sems, recv_sems):
        x, y, c = _mesh_pos()
        sibling = (x, y, 1 - c)
        a = _remote(gin_ref.at[:, pl.ds((1 - c) * hi, hi), :], rin_ref, send_sems, recv_sems, 0, sibling)
        b = _remote(gout_ref.at[:, pl.ds((1 - c) * ho, ho), :], rout_ref, send_sems, recv_sems, 1, sibling)
        a.start()
        b.start()
        a.wait()
        b.wait()

    return pl.pallas_call(
        body, name=f"pair_exchange_{l}",
        in_specs=[HBM_SPEC, HBM_SPEC], out_specs=[HBM_SPEC, HBM_SPEC],
        out_shape=[jax.ShapeDtypeStruct((NQ, hi, WQ), F32), jax.ShapeDtypeStruct((NQ, ho, D), F32)],
        scratch_shapes=[pltpu.SemaphoreType.DMA((2,)), pltpu.SemaphoreType.DMA((2,))],
    )(gin, gout)


def _pair_sum(g, r, pos_arr, name):
    nq, rows, cols = r.shape
    tr = min(rows, 256)
    nt = rows // tr

    def body(pos_ref, g_ref, r_ref, ob_ref, own_ref):
        total = g_ref[...] + r_ref[...]
        ob_ref[...] = total.astype(BF16)

        @pl.when(pl.program_id(1) == pos_ref[1])
        def _():
            own_ref[...] = total

    blk = (None, tr, cols)
    grid_spec = pltpu.PrefetchScalarGridSpec(
        num_scalar_prefetch=1, grid=(nt, nq),
        in_specs=[pl.BlockSpec(blk, lambda t, q, pos: (q, pos[0] * nt + t, 0)),
                  pl.BlockSpec(blk, lambda t, q, pos: (q, t, 0))],
        out_specs=[pl.BlockSpec(blk, lambda t, q, pos: (q, t, 0)),
                   pl.BlockSpec((tr, cols), lambda t, q, pos: (t, 0))])
    return pl.pallas_call(
        body, name=name, grid_spec=grid_spec,
        out_shape=[jax.ShapeDtypeStruct(r.shape, BF16), jax.ShapeDtypeStruct((rows, cols), F32)],
        compiler_params=_params(("arbitrary",) * 2),
    )(pos_arr, g, r)


def _chip_copies(start):
    def body(srcs, lands, send_sems, recv_sems):
        pin, pout = srcs
        rin, rout = lands
        x, y, c = _mesh_pos()
        for j, (cx, cy, q) in enumerate(_other_chips(x, y)):
            to = (cx, cy, c)
            for k, (src, dst) in enumerate(((pin, rin), (pout, rout))):
                cp = _remote(src.at[q], dst.at[j], send_sems, recv_sems, 3 * k + j, to)
                if start:
                    cp.start()
                else:
                    cp.wait_send()
                    cp.wait_recv()

    body.n_copies = 6
    return body


def _chip_sum(own, r, pos_arr, name):
    rows, cols = own.shape
    tr = min(rows, 256)

    def body(pos_ref, p_ref, r0_ref, r1_ref, r2_ref, o_ref):
        o_ref[...] = ((p_ref[...] + r0_ref[...].astype(F32)) + r1_ref[...].astype(F32)) + r2_ref[...].astype(F32)

    def peer(j):
        return pl.BlockSpec((None, tr, cols), lambda t, pos: (j, t, 0))

    grid_spec = pltpu.PrefetchScalarGridSpec(
        num_scalar_prefetch=1, grid=(rows // tr,),
        in_specs=[pl.BlockSpec((tr, cols), lambda t, pos: (t, 0)), peer(0), peer(1), peer(2)],
        out_specs=pl.BlockSpec((None, tr, cols), lambda t, pos: (pos[0], t, 0)))
    return pl.pallas_call(
        body, name=name, grid_spec=grid_spec, out_shape=jax.ShapeDtypeStruct((2, rows, cols), F32),
        compiler_params=_params(("arbitrary",)),
    )(pos_arr, own, r, r, r)


def _pair_share(gin, gout, l):
    def body(gin_in, gout_in, gin_ref, gout_ref, send_sems, recv_sems):
        del gin_in, gout_in
        x, y, c = _mesh_pos()
        sibling = (x, y, 1 - c)
        sends = [_remote(gin_ref.at[c], gin_ref.at[c], send_sems, recv_sems, 0, sibling),
                 _remote(gout_ref.at[c], gout_ref.at[c], send_sems, recv_sems, 1, sibling)]
        for cp in sends:
            cp.start()
        _remote(gin_ref.at[1 - c], gin_ref.at[1 - c], send_sems, recv_sems, 0, sibling).wait_recv()
        _remote(gout_ref.at[1 - c], gout_ref.at[1 - c], send_sems, recv_sems, 1, sibling).wait_recv()
        for cp in sends:
            cp.wait_send()

    return pl.pallas_call(
        body, name=f"pair_share_{l}",
        in_specs=[HBM_SPEC, HBM_SPEC], out_specs=[HBM_SPEC, HBM_SPEC],
        out_shape=[jax.ShapeDtypeStruct(gin.shape, F32), jax.ShapeDtypeStruct(gout.shape, F32)],
        input_output_aliases={0: 0, 1: 1},
        scratch_shapes=[pltpu.SemaphoreType.DMA((2,)), pltpu.SemaphoreType.DMA((2,))],
    )(gin, gout)


def _adamw_large(w, m, v, g, i, prev, name):
    _, rows, cols = w.shape
    half = rows // 2
    tr = min(half, 256)
    nt = half // tr

    def body(w_ref, m_ref, v_ref, g_ref, *rest):
        go_ref, d_ref, mo_ref, vo_ref = rest[-4:]
        gv = g_ref[...]
        go_ref[...] = gv
        d_ref[...], mo_ref[...], vo_ref[...] = _adamw_math(w_ref[...], gv, m_ref[...], v_ref[...])

    full = pl.BlockSpec((None, tr, cols), lambda h, t: (i, h * nt + t, 0))
    out = jax.ShapeDtypeStruct(w.shape, F32)
    carried = [] if prev is None else list(prev)
    return pl.pallas_call(
        body, name=name, grid=(2, nt),
        in_specs=[full, full, full, pl.BlockSpec((None, tr, cols), lambda h, t: (h, t, 0))]
        + [pl.BlockSpec(memory_space=pl.ANY)] * len(carried),
        out_specs=[full] * 4, out_shape=[out] * 4,
        input_output_aliases={4 + k: k for k in range(len(carried))},
        compiler_params=_params(("arbitrary",) * 2),
    )(w, m, v, g, *carried)


def _adamw(w, g, m, v, name):
    shape = w.shape
    w2, g2, m2, v2 = (t.reshape(-1, shape[-1]) for t in (w, g, m, v))
    rows, cols = w2.shape
    tr = 256 if rows % 256 == 0 else rows

    def body(w_ref, g_ref, m_ref, v_ref, d_ref, mo_ref, vo_ref):
        d_ref[...], mo_ref[...], vo_ref[...] = _adamw_math(w_ref[...], g_ref[...], m_ref[...], v_ref[...])

    blk = pl.BlockSpec((tr, cols), lambda i: (i, 0))
    out = jax.ShapeDtypeStruct((rows, cols), F32)
    d, mo, vo = pl.pallas_call(
        body, name=name, grid=(rows // tr,), in_specs=[blk] * 4, out_specs=[blk] * 3, out_shape=[out] * 3,
        compiler_params=_params(("arbitrary",)),
    )(w2, g2, m2, v2)
    return d.reshape(shape), mo.reshape(shape), vo.reshape(shape)


def _layer_slot(l):
    return (l % 2) * 2 + l // 2


def kernel(x, ln_g, ln_b, w_in_even, w_out_even, pool_w, pool_scale, sconv_w, sconv_b, w_in_odd, w_out_odd, sgu_ln_g, sgu_ln_b, sgu_w, sgu_b, dconv_w, dconv_b, dnorm_g, dnorm_b, loss_target, m_ln_g, m_ln_b, m_w_in_even, m_w_out_even, m_pool_w, m_pool_scale, m_sconv_w, m_sconv_b, m_w_in_odd, m_w_out_odd, m_sgu_ln_g, m_sgu_ln_b, m_sgu_w, m_sgu_b, m_dconv_w, m_dconv_b, m_dnorm_g, m_dnorm_b, v_ln_g, v_ln_b, v_w_in_even, v_w_out_even, v_pool_w, v_pool_scale, v_sconv_w, v_sconv_b, v_w_in_odd, v_w_out_odd, v_sgu_ln_g, v_sgu_ln_b, v_sgu_w, v_sgu_b, v_dconv_w, v_dconv_b, v_dnorm_g, v_dnorm_b):
    weights = dict(ln_g=ln_g, ln_b=ln_b, w_in_even=w_in_even, w_out_even=w_out_even, pool_w=pool_w,
                   pool_scale=pool_scale, sconv_w=sconv_w, sconv_b=sconv_b, w_in_odd=w_in_odd, w_out_odd=w_out_odd,
                   sgu_ln_g=sgu_ln_g, sgu_ln_b=sgu_ln_b, sgu_w=sgu_w, sgu_b=sgu_b, dconv_w=dconv_w,
                   dconv_b=dconv_b, dnorm_g=dnorm_g, dnorm_b=dnorm_b)
    moments_m = dict(ln_g=m_ln_g, ln_b=m_ln_b, w_in_even=m_w_in_even, w_out_even=m_w_out_even, pool_w=m_pool_w,
                     pool_scale=m_pool_scale, sconv_w=m_sconv_w, sconv_b=m_sconv_b, w_in_odd=m_w_in_odd,
                     w_out_odd=m_w_out_odd, sgu_ln_g=m_sgu_ln_g, sgu_ln_b=m_sgu_ln_b, sgu_w=m_sgu_w, sgu_b=m_sgu_b,
                     dconv_w=m_dconv_w, dconv_b=m_dconv_b, dnorm_g=m_dnorm_g, dnorm_b=m_dnorm_b)
    moments_v = dict(ln_g=v_ln_g, ln_b=v_ln_b, w_in_even=v_w_in_even, w_out_even=v_w_out_even, pool_w=v_pool_w,
                     pool_scale=v_pool_scale, sconv_w=v_sconv_w, sconv_b=v_sconv_b, w_in_odd=v_w_in_odd,
                     w_out_odd=v_w_out_odd, sgu_ln_g=v_sgu_ln_g, sgu_ln_b=v_sgu_ln_b, sgu_w=v_sgu_w, sgu_b=v_sgu_b,
                     dconv_w=v_dconv_w, dconv_b=v_dconv_b, dnorm_g=v_dnorm_g, dnorm_b=v_dnorm_b)
    names = list(weights)

    xd, yd, cd = _mesh_pos()
    chip = 2 * xd + yd
    pos_arr = jnp.stack([cd, chip]).astype(jnp.int32)

    small_sh = jnp.concatenate(
        [sconv_w.reshape(6, HEAD), sgu_ln_g, sgu_ln_b, dconv_b, dnorm_g, dnorm_b, dconv_w.reshape(62, HEAD),
         jnp.zeros((2, HEAD), F32), pool_w.reshape(512, HEAD)], axis=0)
    win_first, wout_first, small_g = _gather_weights(
        _cast_own(w_in_even[0:1], pos_arr, "cast_win_first"), _cast_own(w_out_even[0:1], pos_arr, "cast_wout_first"),
        small_sh)
    small_g = lax.dynamic_update_slice(small_g, small_sh[None], (chip, 0, 0))
    later_in = jnp.concatenate([w_in_even[1:2], w_in_odd], axis=0)
    later_out = jnp.concatenate([w_out_even[1:2], w_out_odd], axis=0)
    g_send, g_recv, _, g_lands, g_token = _split_start(
        _gather_rest_copies(True), "gather_rest_start", [],
        [_cast_own(later_in, pos_arr, "cast_win_rest"), _cast_own(later_out, pos_arr, "cast_wout_rest")])

    def layer_weights(slot):
        return (win_first, wout_first, 0) if slot == 0 else (win_rest, wout_rest, slot - 1)

    def full_rows(lo, n):
        return jnp.transpose(small_g[:, lo:lo + n], (1, 0, 2)).reshape(n, D)

    sconv_w_f = full_rows(Q_SCONV_W, 6).reshape(2, SHORT_K, D)
    sln_g_f = full_rows(Q_SLN_G, 2)
    sln_b_f = full_rows(Q_SLN_B, 2)
    dconv_b_f = full_rows(Q_DCONV_B, 2)
    dn_g_f = full_rows(Q_DN_G, 2)
    dn_b_f = full_rows(Q_DN_B, 2)
    dconv_w_f = full_rows(Q_DCONV_W, 62).reshape(2, CONV_K, D)
    pool_w_f = jnp.transpose(small_g[:, Q_POOL_W:].reshape(NQ, 2, 4, 64, HEAD), (1, 2, 0, 3, 4)).reshape(2, 4, HEAD, HEAD)
    pool_w_b = pool_w_f.astype(BF16)
    pool_wt_b = jnp.swapaxes(pool_w_f, 2, 3).astype(BF16)
    idx = jnp.arange(SGU_BLOCK)
    mask = (idx[None, :] // 64) <= (idx[:, None] // 64)
    ws_f = jnp.where(mask[None, None], sgu_w, 0.0)
    ws_b = ws_f.astype(BF16)
    wst_b = jnp.swapaxes(ws_f, 2, 3).astype(BF16)

    def row(a, i):
        return a[i:i + 1]

    x_f = x[0]
    x_b = x_f.astype(BF16)
    saved = []
    for l in range(NL):
        i, slot = l // 2, _layer_slot(l)
        if l == 1:
            _, g_lands = _split_wait(_gather_rest_copies(False), "gather_rest_wait", g_send, g_recv, [], g_lands, x_b)
            win_rest, wout_rest = _gather_rest_forward(g_lands[0], g_lands[1])
        win_g, wout_g, k = layer_weights(slot)
        z = _proj_in(x_b, win_g, k, l, g_token if l == 0 else None)
        if l % 2 == 0:
            ycat = _even_fwd(z, pool_w_b[i], row(pool_scale, i), sconv_w_f[i], row(sconv_b, i), l)
        else:
            ycat = _odd_fwd(z, row(sln_g_f, i), row(sln_b_f, i), ws_b[i], _sgu_bias_rows(sgu_b[i]),
                            dconv_w_f[i], row(dconv_b_f, i), row(dn_g_f, i), row(dn_b_f, i), l)
        x_next, x_next_b, xhat, rstd = _proj_out_ln(ycat, wout_g, k, l, x_f, row(ln_g, l), row(ln_b, l))
        saved.append((x_b, z, ycat, xhat, rstd))
        x_f, x_b = x_next, x_next_b

    loss_part, dxn = _loss_grad(x_f, loss_target[0])
    loss = lax.psum(loss_part[0, 0], ("x", "y", "c"))

    small = {}
    d_ln_g = [None] * NL
    d_ln_b = [None] * NL
    large = {"w_in_even": None, "w_out_even": None, "w_in_odd": None, "w_out_odd": None}
    pending = None
    token = None

    def finish(exchange, after):
        lx, send, recv, srcs, lands, own_in, own_out = exchange
        _, (r_in, r_out) = _split_wait(_chip_copies(False), f"chip_wait_{lx}", send, recv, srcs, lands, after)
        fin = _chip_sum(own_in, r_in, pos_arr, f"chip_sum_in_{lx}")
        fout = _chip_sum(own_out, r_out, pos_arr, f"chip_sum_out_{lx}")
        gs_in, gs_out = _pair_share(fin, fout, lx)
        kind = "even" if lx % 2 == 0 else "odd"
        for nm, gs in ((f"w_in_{kind}", gs_in), (f"w_out_{kind}", gs_out)):
            large[nm] = _adamw_large(weights[nm], moments_m[nm], moments_v[nm], gs, lx // 2, large[nm],
                                     f"adamw_{nm}_{lx // 2}")

    for l in reversed(range(NL)):
        i, slot = l // 2, _layer_slot(l)
        win_g, wout_g, k = layer_weights(slot)
        xin_b, z, ycat, xhat, rstd = saved[l]
        dr, dr_b, d_ln_g[l], d_ln_b[l] = _ln_bwd(dxn, xhat, rstd, row(ln_g, l), l, token)
        dy = _dycat(dr_b, wout_g, k, l)
        gout = _dwout(ycat, dr_b, l).reshape(NQ, RQ, D)
        if l % 2 == 0:
            dz, d_pw, d_ps, d_cw, d_cb = _even_bwd(z, dy, pool_w_b[i], pool_wt_b[i], row(pool_scale, i),
                                                   sconv_w_f[i], row(sconv_b, i), l)
            small[("pool_w", i)] = d_pw
            small[("pool_scale", i)] = d_ps
            small[("sconv_w", i)] = d_cw
            small[("sconv_b", i)] = d_cb
        else:
            dz, d_lg, d_lb, d_ws, d_sb, d_cw, d_cb, d_ng, d_nb = _odd_bwd(
                z, dy, row(sln_g_f, i), row(sln_b_f, i), ws_b[i], wst_b[i], _sgu_bias_rows(sgu_b[i]),
                dconv_w_f[i], row(dconv_b_f, i), row(dn_g_f, i), row(dn_b_f, i), l)
            small[("sgu_ln_g", i)] = d_lg
            small[("sgu_ln_b", i)] = d_lb
            small[("sgu_w", i)] = jnp.where(mask[None], d_ws, 0.0)
            small[("sgu_b", i)] = d_sb
            small[("dconv_w", i)] = d_cw
            small[("dconv_b", i)] = d_cb
            small[("dnorm_g", i)] = d_ng
            small[("dnorm_b", i)] = d_nb
        gin = _dwin(xin_b, dz, l)
        dxn = _dx(dz, win_g, k, l, dr)
        if pending is not None:
            finish(pending, dxn)
        rin, rout = _pair_exchange(gin, gout, l)
        pin_b, pin_own = _pair_sum(gin, rin, pos_arr, f"pair_sum_in_{l}")
        pout_b, pout_own = _pair_sum(gout, rout, pos_arr, f"pair_sum_out_{l}")
        send, recv, srcs, lands, token = _split_start(
            _chip_copies(True), f"chip_start_{l}", [pin_b, pout_b],
            [jax.ShapeDtypeStruct((3,) + pin_b.shape[1:], BF16), jax.ShapeDtypeStruct((3,) + pout_b.shape[1:], BF16)])
        pending = (l, send, recv, srcs, lands, pin_own, pout_own)
    grad_x = dxn[None]

    def both(name, first, step):
        return [(first, small[(name, 0)]), (first + step, small[(name, 1)])]

    vectors = ([(R_LN_G + l, d_ln_g[l]) for l in range(NL)] + [(R_LN_B + l, d_ln_b[l]) for l in range(NL)]
               + both("pool_scale", R_PSCALE, 1) + both("sconv_b", R_SCONV_B, 1) + both("sconv_w", R_SCONV_W, SHORT_K)
               + both("sgu_ln_g", R_SLN_G, 1) + both("sgu_ln_b", R_SLN_B, 1) + both("dconv_b", R_DCONV_B, 1)
               + both("dnorm_g", R_DN_G, 1) + both("dnorm_b", R_DN_B, 1) + both("dconv_w", R_DCONV_W, CONV_K)
               + both("sgu_b", R_SGU_B, 4))
    sgu_w_rows = 4 * SGU_BLOCK
    pool_w_rows = 4 * HEAD
    total, total_sgu_w, total_pool_w = _allreduce_small(
        [(R_VECTORS, D, vectors),
         (2 * sgu_w_rows, SGU_BLOCK, [(i * sgu_w_rows, small[("sgu_w", i)].reshape(sgu_w_rows, SGU_BLOCK)) for i in range(2)]),
         (2 * pool_w_rows, HEAD, [(i * pool_w_rows, small[("pool_w", i)].reshape(pool_w_rows, HEAD)) for i in range(2)])],
        token)
    finish(pending, total)

    def mine(a):
        return lax.dynamic_slice_in_dim(a, chip * HEAD, HEAD, axis=a.ndim - 1)

    grads = {
        "ln_g": total[R_LN_G:R_LN_G + 4],
        "ln_b": total[R_LN_B:R_LN_B + 4],
        "pool_scale": total[R_PSCALE:R_PSCALE + 2],
        "sconv_b": total[R_SCONV_B:R_SCONV_B + 2],
        "sconv_w": mine(total[R_SCONV_W:R_SCONV_W + 6].reshape(2, SHORT_K, D)),
        "sgu_ln_g": mine(total[R_SLN_G:R_SLN_G + 2]),
        "sgu_ln_b": mine(total[R_SLN_B:R_SLN_B + 2]),
        "dconv_b": mine(total[R_DCONV_B:R_DCONV_B + 2]),
        "dnorm_g": mine(total[R_DN_G:R_DN_G + 2]),
        "dnorm_b": mine(total[R_DN_B:R_DN_B + 2]),
        "dconv_w": mine(total[R_DCONV_W:R_DCONV_W + 62].reshape(2, CONV_K, D)),
        "sgu_b": total[R_SGU_B:R_SGU_B + 8, 0:SGU_BLOCK].reshape(2, 4, SGU_BLOCK),
        "sgu_w": total_sgu_w.reshape(2, 4, SGU_BLOCK, SGU_BLOCK),
        "pool_w": lax.dynamic_slice_in_dim(total_pool_w.reshape(2, 4, HEAD, HEAD), chip * 64, 64, axis=2),
    }

    deltas, new_m, new_v = {}, {}, {}
    for name in names:
        if name in large:
            grads[name], deltas[name], new_m[name], new_v[name] = large[name]
        else:
            deltas[name], new_m[name], new_v[name] = _adamw(
                weights[name], grads[name], moments_m[name], moments_v[name], f"adamw_{name}")

    return (loss, grad_x, *[grads[n] for n in names], *[deltas[n] for n in names],
            *[new_m[n] for n in names], *[new_v[n] for n in names])
```

```python
import jax
import jax.numpy as jnp
from jax import lax
from jax.experimental import pallas as pl
from jax.experimental.pallas import tpu as pltpu

F32 = jnp.float32
BF16 = jnp.bfloat16
MXU_DTYPE = BF16

D = 1024
DZ = 6144
DY = 2048
NQ = 4
WQ = DZ // NQ
RQ = DY // NQ
NL = 4
ALPHA = (2 * NL) ** 0.25
LN_EPS = 1e-5
CONV_K = 31
SHORT_K = 3
SGU_BLOCK = 128
HEAD = 256
POOL_HALO = 16
CONV_HALO = 32
LANE = 128
SUBLANES = 8
CONV_ROWS = 32
LANE_BLOCKS = 8
MIB = 1024 * 1024

ADAM_LR = 0.001
ADAM_B1 = 0.9
ADAM_B2 = 0.999
ADAM_EPS = 1e-08
ADAM_WD = 0.01
ADAM_STEP = 10

NN = ((1,), (0,))
NT = ((1,), (1,))
TN = ((0,), (0,))
MESH = pl.DeviceIdType.MESH
HBM_SPEC = pl.BlockSpec(memory_space=pltpu.HBM)
SEM_SPEC = pl.BlockSpec(memory_space=pltpu.SEMAPHORE)
SIDE_EFFECT = pltpu.SideEffectType.DATAFLOW_SIDE_EFFECTING

R_LN_G, R_LN_B, R_PSCALE, R_SCONV_B, R_SCONV_W = 0, 4, 8, 10, 12
R_SLN_G, R_SLN_B, R_DCONV_B, R_DN_G, R_DN_B, R_DCONV_W = 18, 20, 22, 24, 26, 28
R_SGU_B, R_VECTORS = 90, 104
Q_SCONV_W, Q_SLN_G, Q_SLN_B, Q_DCONV_B, Q_DN_G, Q_DN_B, Q_DCONV_W, Q_POOL_W, Q_ROWS = 0, 6, 8, 10, 12, 14, 16, 80, 592


def _dot(a, b, dims):
    return lax.dot_general(a.astype(MXU_DTYPE), b.astype(MXU_DTYPE), (dims, ((), ())),
                           preferred_element_type=F32)


def _params(semantics=None, vmem_mib=48):
    return pltpu.CompilerParams(dimension_semantics=semantics, vmem_limit_bytes=vmem_mib * MIB)


def _sigmoid(v):
    return jax.nn.sigmoid(v)


def _silu_and_grad(v):
    s = _sigmoid(v)
    return v * s, s * (1.0 + v * (1.0 - s))


def _ln_stats(v):
    mu = jnp.mean(v, axis=-1, keepdims=True)
    vc = v - mu
    var = jnp.mean(vc * vc, axis=-1, keepdims=True)
    rstd = lax.rsqrt(var + LN_EPS)
    return vc * rstd, rstd


def _ln_bwd_rows(dxhat, xhat, rstd):
    m1 = jnp.mean(dxhat, axis=-1, keepdims=True)
    m2 = jnp.mean(dxhat * xhat, axis=-1, keepdims=True)
    return rstd * (dxhat - m1 - xhat * m2)


def _colsum(v):
    return jnp.sum(v, axis=0, keepdims=True)


def _adamw_math(w, g, m, v):
    m_new = ADAM_B1 * m + (1.0 - ADAM_B1) * g
    v_new = ADAM_B2 * v + (1.0 - ADAM_B2) * (g * g)
    m_hat = m_new / (1.0 - ADAM_B1 ** ADAM_STEP)
    v_hat = v_new / (1.0 - ADAM_B2 ** ADAM_STEP)
    return -ADAM_LR * (m_hat / (jnp.sqrt(v_hat) + ADAM_EPS) + ADAM_WD * w), m_new, v_new


def _mesh_pos():
    return lax.axis_index("x"), lax.axis_index("y"), lax.axis_index("c")


def _after_spec(after):
    return [] if after is None else [pl.BlockSpec(memory_space=pl.ANY)]


def _after_args(after):
    return [] if after is None else [after]


def _proj_in(xb, win_g, k, l, after=None):
    s = xb.shape[0]
    tm = min(s, 1024)

    def body(x_ref, w_ref, *rest):
        rest[-1][...] = _dot(x_ref[...], w_ref[...].reshape(D, WQ), NN)

    return pl.pallas_call(
        body, name=f"proj_in_{l}", grid=(NQ, s // tm),
        in_specs=[pl.BlockSpec((tm, D), lambda q, m: (m, 0)),
                  pl.BlockSpec((None, 2, None, D // 2, WQ), lambda q, m: (q, 0, k, 0, 0))] + _after_spec(after),
        out_specs=pl.BlockSpec((tm, WQ), lambda q, m: (m, q)),
        out_shape=jax.ShapeDtypeStruct((s, DZ), F32),
        compiler_params=_params(("arbitrary", "arbitrary")),
    )(xb, win_g, *_after_args(after))


def _proj_out_ln(ycat, wout_g, k, l, x, g, b):
    s = x.shape[0]
    tm = min(s, 512)

    def body(y_ref, w_ref, x_ref, g_ref, b_ref, xn_ref, xb_ref, xh_ref, rs_ref):
        y = _dot(y_ref[...], w_ref[...].reshape(DY, D), NN)
        xhat, rstd = _ln_stats(ALPHA * x_ref[...] + y)
        xn = xhat * g_ref[...] + b_ref[...]
        xn_ref[...] = xn
        xb_ref[...] = xn.astype(BF16)
        xh_ref[...] = xhat
        rs_ref[...] = rstd

    row = lambda m: (m, 0)
    fixed = lambda m: (0, 0)
    return pl.pallas_call(
        body, name=f"proj_out_ln_{l}", grid=(s // tm,),
        in_specs=[pl.BlockSpec((tm, DY), row),
                  pl.BlockSpec((NQ, 2, None, RQ // 2, D), lambda m: (0, 0, k, 0, 0)),
                  pl.BlockSpec((tm, D), row), pl.BlockSpec((1, D), fixed), pl.BlockSpec((1, D), fixed)],
        out_specs=[pl.BlockSpec((tm, D), row), pl.BlockSpec((tm, D), row), pl.BlockSpec((tm, D), row),
                   pl.BlockSpec((tm, 1), row)],
        out_shape=[jax.ShapeDtypeStruct((s, D), F32), jax.ShapeDtypeStruct((s, D), BF16),
                   jax.ShapeDtypeStruct((s, D), F32), jax.ShapeDtypeStruct((s, 1), F32)],
        compiler_params=_params(("arbitrary",)),
    )(ycat, wout_g, x, g, b)


def _loss_grad(xl, target):
    s = xl.shape[0]
    ts = min(s, 512)

    def body(x_ref, t_ref, loss_ref, dx_ref):
        @pl.when(pl.program_id(0) == 0)
        def _():
            loss_ref[...] = jnp.zeros_like(loss_ref)
        err = x_ref[...] - t_ref[...]
        dx_ref[...] = err * (1.0 / D)
        loss_ref[...] += 0.5 * jnp.sum(jnp.mean(err * err, axis=-1, keepdims=True), axis=0, keepdims=True)

    row = lambda m: (m, 0)
    return pl.pallas_call(
        body, name="loss_grad", grid=(s // ts,),
        in_specs=[pl.BlockSpec((ts, D), row), pl.BlockSpec((ts, D), row)],
        out_specs=[pl.BlockSpec((1, 1), lambda m: (0, 0)), pl.BlockSpec((ts, D), row)],
        out_shape=[jax.ShapeDtypeStruct((1, 1), F32), jax.ShapeDtypeStruct((s, D), F32)],
        compiler_params=_params(("arbitrary",)),
    )(xl, target)


def _ln_bwd(dxn, xhat, rstd, g, l, after=None):
    s = dxn.shape[0]
    ts = min(s, 512)

    def body(d_ref, xh_ref, rs_ref, g_ref, *rest):
        dr_ref, drb_ref, dg_ref, db_ref = rest[-4:]

        @pl.when(pl.program_id(0) == 0)
        def _():
            dg_ref[...] = jnp.zeros_like(dg_ref)
            db_ref[...] = jnp.zeros_like(db_ref)
        d = d_ref[...]
        xhat_v = xh_ref[...]
        dr = _ln_bwd_rows(d * g_ref[...], xhat_v, rs_ref[...])
        dr_ref[...] = dr
        drb_ref[...] = dr.astype(BF16)
        dg_ref[...] += _colsum(d * xhat_v)
        db_ref[...] += _colsum(d)

    row = lambda m: (m, 0)
    fixed = lambda m: (0, 0)
    return pl.pallas_call(
        body, name=f"ln_bwd_{l}", grid=(s // ts,),
        in_specs=[pl.BlockSpec((ts, D), row), pl.BlockSpec((ts, D), row), pl.BlockSpec((ts, 1), row),
                  pl.BlockSpec((1, D), fixed)] + _after_spec(after),
        out_specs=[pl.BlockSpec((ts, D), row), pl.BlockSpec((ts, D), row), pl.BlockSpec((1, D), fixed),
                   pl.BlockSpec((1, D), fixed)],
        out_shape=[jax.ShapeDtypeStruct((s, D), F32), jax.ShapeDtypeStruct((s, D), BF16),
                   jax.ShapeDtypeStruct((1, D), F32), jax.ShapeDtypeStruct((1, D), F32)],
        compiler_params=_params(("arbitrary",)),
    )(dxn, xhat, rstd, g, *_after_args(after))


def _dycat(drb, wout_g, k, l):
    s = drb.shape[0]
    tm = min(s, 512)

    def body(d_ref, w_ref, o_ref):
        o_ref[...] = _dot(d_ref[...], w_ref[...].reshape(DY, D), NT)

    return pl.pallas_call(
        body, name=f"dycat_{l}", grid=(s // tm,),
        in_specs=[pl.BlockSpec((tm, D), lambda m: (m, 0)),
                  pl.BlockSpec((NQ, 2, None, RQ // 2, D), lambda m: (0, 0, k, 0, 0))],
        out_specs=pl.BlockSpec((tm, DY), lambda m: (m, 0)),
        out_shape=jax.ShapeDtypeStruct((s, DY), F32),
        compiler_params=_params(("arbitrary",)),
    )(drb, wout_g)


def _dwout(ycat, drb, l):
    s = drb.shape[0]
    tk = min(s, 512)

    def body(y_ref, d_ref, o_ref):
        part = _dot(y_ref[...], d_ref[...], TN)

        @pl.when(pl.program_id(0) == 0)
        def _():
            o_ref[...] = part

        @pl.when(pl.program_id(0) > 0)
        def _():
            o_ref[...] += part

    return pl.pallas_call(
        body, name=f"dwout_{l}", grid=(s // tk,),
        in_specs=[pl.BlockSpec((tk, DY), lambda k: (k, 0)), pl.BlockSpec((tk, D), lambda k: (k, 0))],
        out_specs=pl.BlockSpec((DY, D), lambda k: (0, 0)),
        out_shape=jax.ShapeDtypeStruct((DY, D), F32),
        compiler_params=_params(("arbitrary",)),
    )(ycat, drb)


def _dwin(xb, dzb, l):
    s = xb.shape[0]
    tk = min(s, 1024)

    def body(x_ref, d_ref, o_ref):
        part = _dot(x_ref[...], d_ref[...], TN)

        @pl.when(pl.program_id(1) == 0)
        def _():
            o_ref[...] = part

        @pl.when(pl.program_id(1) > 0)
        def _():
            o_ref[...] += part

    return pl.pallas_call(
        body, name=f"dwin_{l}", grid=(NQ, s // tk),
        in_specs=[pl.BlockSpec((tk, D), lambda q, k: (k, 0)), pl.BlockSpec((tk, WQ), lambda q, k: (k, q))],
        out_specs=pl.BlockSpec((None, D, WQ), lambda q, k: (q, 0, 0)),
        out_shape=jax.ShapeDtypeStruct((NQ, D, WQ), F32),
        compiler_params=_params(("arbitrary", "arbitrary")),
    )(xb, dzb)


def _dx(dzb, win_g, k, l, dr):
    s = dzb.shape[0]
    tm = min(s, 1024)

    def body(d_ref, w_ref, r_ref, o_ref):
        part = _dot(d_ref[...], w_ref[...].reshape(D, WQ), NT)

        @pl.when(pl.program_id(1) == 0)
        def _():
            o_ref[...] = ALPHA * r_ref[...] + part

        @pl.when(pl.program_id(1) > 0)
        def _():
            o_ref[...] += part

    return pl.pallas_call(
        body, name=f"dx_{l}", grid=(s // tm, NQ),
        in_specs=[pl.BlockSpec((tm, WQ), lambda m, q: (m, q)),
                  pl.BlockSpec((None, 2, None, D // 2, WQ), lambda m, q: (q, 0, k, 0, 0)),
                  pl.BlockSpec((tm, D), lambda m, q: (m, 0))],
        out_specs=pl.BlockSpec((tm, D), lambda m, q: (m, 0)),
        out_shape=jax.ShapeDtypeStruct((s, D), F32),
        compiler_params=_params(("arbitrary", "arbitrary")),
    )(dzb, win_g, dr)


def _window_sums(e, causal):
    n = e.shape[0]

    def shifted(a, k):
        return pltpu.roll(a, k if causal else n - k, axis=0)

    parts = []
    acc = e
    for step, k in enumerate((1, 2, 4, 8)):
        acc = acc + shifted(acc, k)
        parts.append(acc[:, 0:HEAD])
        if step < 3:
            acc = acc[:, HEAD:]
    return jnp.concatenate(parts, axis=1)


def _pool_counts(first_pos, rows):
    t1 = (lax.broadcasted_iota(jnp.int32, (rows, 1), 0) + first_pos + 1).astype(F32)
    lane = lax.broadcasted_iota(jnp.int32, (1, D), 1)
    win = jnp.where(lane < HEAD, 2.0, jnp.where(lane < 2 * HEAD, 4.0, jnp.where(lane < 3 * HEAD, 8.0, 16.0)))
    return jnp.minimum(t1, win)


def _group_dot(v, w_ref, dims):
    return jnp.concatenate(
        [_dot(v[:, g * HEAD:(g + 1) * HEAD], w_ref[g], dims) for g in range(4)], axis=1)


def _prev_index(ts, halo):
    return lambda i: (jnp.maximum(i * (ts // halo) - 1, 0), 0)


def _next_index(ts, halo, s):
    return lambda i: (jnp.minimum((i + 1) * (ts // halo), s // halo - 1), 0)


def _even_fwd(z, pool_w, pool_scale, sconv_w, sconv_b, l):
    s = z.shape[0]
    ts = min(s, 256)
    h = POOL_HALO

    def body(z_ref, zp_ref, pw_ref, ps_ref, cw_ref, cb_ref, o_ref):
        i = pl.program_id(0)
        inside = i > 0
        xa = z_ref[:, 0:D]
        xa_ext = jnp.concatenate([jnp.where(inside, zp_ref[:, 0:D], 0.0), xa], axis=0)
        sums = _window_sums(xa_ext, True)[h:]
        pooled = sums / _pool_counts(i * ts, ts) - xa
        p = _group_dot(pooled, pw_ref, NN)
        silu_ga, _ = _silu_and_grad(z_ref[:, D:2 * D])
        o_ref[:, 0:D] = (p * ps_ref[...] * silu_ga).astype(BF16)

        q_main = z_ref[:, 4 * D:5 * D] * z_ref[:, 2 * D:3 * D]
        q_prev = jnp.where(inside, zp_ref[:, 4 * D:5 * D] * zp_ref[:, 2 * D:3 * D], 0.0)
        q_ext = jnp.concatenate([q_prev, q_main], axis=0)
        cv = cw_ref[2:3, :] * q_main + cb_ref[...]
        cv = cv + cw_ref[1:2, :] * pltpu.roll(q_ext, 1, axis=0)[h:]
        cv = cv + cw_ref[0:1, :] * pltpu.roll(q_ext, 2, axis=0)[h:]
        silu_gb, _ = _silu_and_grad(z_ref[:, 5 * D:6 * D])
        o_ref[:, D:2 * D] = (z_ref[:, 3 * D:4 * D] * cv * silu_gb).astype(BF16)

    fixed2 = lambda i: (0, 0)
    return pl.pallas_call(
        body, name=f"even_fwd_{l}", grid=(s // ts,),
        in_specs=[pl.BlockSpec((ts, DZ), lambda i: (i, 0)), pl.BlockSpec((h, DZ), _prev_index(ts, h)),
                  pl.BlockSpec((4, HEAD, HEAD), lambda i: (0, 0, 0)), pl.BlockSpec((1, D), fixed2),
                  pl.BlockSpec((SHORT_K, D), fixed2), pl.BlockSpec((1, D), fixed2)],
        out_specs=pl.BlockSpec((ts, DY), lambda i: (i, 0)),
        out_shape=jax.ShapeDtypeStruct((s, DY), BF16),
        compiler_params=_params(("arbitrary",)),
    )(z, z, pool_w, pool_scale, sconv_w, sconv_b)


def _even_bwd(z, dy, pool_w, pool_wt, pool_scale, sconv_w, sconv_b, l):
    s = z.shape[0]
    ts = min(s, 256)
    h = POOL_HALO
    n_tiles = s // ts

    def body(z_ref, zp_ref, zn_ref, dy_ref, dyn_ref, pw_ref, pwt_ref, ps_ref, cw_ref, cb_ref,
             dz_ref, dpw_ref, dps_ref, dcw_ref, dcb_ref):
        i = pl.program_id(0)
        inside = i > 0
        more = i < n_tiles - 1

        @pl.when(i == 0)
        def _():
            dpw_ref[...] = jnp.zeros_like(dpw_ref)
            dps_ref[...] = jnp.zeros_like(dps_ref)
            dcw_ref[...] = jnp.zeros_like(dcw_ref)
            dcb_ref[...] = jnp.zeros_like(dcb_ref)

        def with_next(main_ref, next_ref, lo):
            return jnp.concatenate([main_ref[:, lo:lo + D], next_ref[:, lo:lo + D]], axis=0)

        xa_ext = jnp.concatenate(
            [jnp.where(inside, zp_ref[:, 0:D], 0.0), z_ref[:, 0:D], zn_ref[:, 0:D]], axis=0)
        counts = _pool_counts(i * ts, ts + h)
        pooled = _window_sums(xa_ext, True)[h:] / counts - xa_ext[h:]
        p = _group_dot(pooled, pw_ref, NN)
        silu_ga, dsilu_ga = _silu_and_grad(with_next(z_ref, zn_ref, D))
        d_ya = with_next(dy_ref, dyn_ref, 0)
        scale = ps_ref[...]
        d_p = d_ya * scale * silu_ga
        d_pooled = _group_dot(d_p, pwt_ref, NN)
        row = lax.broadcasted_iota(jnp.int32, (ts + h, 1), 0)
        d_pooled = jnp.where(jnp.logical_or(more, row < ts), d_pooled, 0.0)
        d_xa = _window_sums(d_pooled / counts, False)[:ts] - d_pooled[:ts]
        dz_ref[:, 0:D] = d_xa.astype(BF16)
        dz_ref[:, D:2 * D] = (d_ya[:ts] * p[:ts] * scale * dsilu_ga[:ts]).astype(BF16)
        dps_ref[...] += _colsum(d_ya[:ts] * p[:ts] * silu_ga[:ts])
        for g in range(4):
            cols = slice(g * HEAD, (g + 1) * HEAD)
            dpw_ref[g] += _dot(pooled[:ts, cols], d_p[:ts, cols], TN)

        cg = z_ref[:, 4 * D:5 * D]
        hh = z_ref[:, 2 * D:3 * D]
        bg = z_ref[:, 3 * D:4 * D]
        q_main = cg * hh
        q_prev = jnp.where(inside, zp_ref[:, 4 * D:5 * D] * zp_ref[:, 2 * D:3 * D], 0.0)
        q_ext = jnp.concatenate([q_prev, q_main], axis=0)
        q_1 = pltpu.roll(q_ext, 1, axis=0)[h:]
        q_2 = pltpu.roll(q_ext, 2, axis=0)[h:]
        cv = cw_ref[2:3, :] * q_main + cw_ref[1:2, :] * q_1 + cw_ref[0:1, :] * q_2 + cb_ref[...]
        silu_gb, dsilu_gb = _silu_and_grad(with_next(z_ref, zn_ref, 5 * D))
        d_yb = with_next(dy_ref, dyn_ref, D)
        d_cv = d_yb * with_next(z_ref, zn_ref, 3 * D) * silu_gb
        d_cv = jnp.where(jnp.logical_or(more, row < ts), d_cv, 0.0)
        d_cv0 = d_cv[:ts]
        n_ext = ts + h
        d_q = (cw_ref[2:3, :] * d_cv0 + cw_ref[1:2, :] * pltpu.roll(d_cv, n_ext - 1, axis=0)[:ts]
               + cw_ref[0:1, :] * pltpu.roll(d_cv, n_ext - 2, axis=0)[:ts])
        dz_ref[:, 2 * D:3 * D] = (d_q * cg).astype(BF16)
        dz_ref[:, 3 * D:4 * D] = (d_yb[:ts] * cv * silu_gb[:ts]).astype(BF16)
        dz_ref[:, 4 * D:5 * D] = (d_q * hh).astype(BF16)
        dz_ref[:, 5 * D:6 * D] = (d_yb[:ts] * bg * cv * dsilu_gb[:ts]).astype(BF16)
        dcb_ref[...] += _colsum(d_cv0)
        dcw_ref[2:3, :] += _colsum(d_cv0 * q_main)
        dcw_ref[1:2, :] += _colsum(d_cv0 * q_1)
        dcw_ref[0:1, :] += _colsum(d_cv0 * q_2)

    fixed2 = lambda i: (0, 0)
    fixed3 = lambda i: (0, 0, 0)
    return pl.pallas_call(
        body, name=f"even_bwd_{l}", grid=(n_tiles,),
        in_specs=[pl.BlockSpec((ts, DZ), lambda i: (i, 0)), pl.BlockSpec((h, DZ), _prev_index(ts, h)),
                  pl.BlockSpec((h, DZ), _next_index(ts, h, s)),
                  pl.BlockSpec((ts, DY), lambda i: (i, 0)), pl.BlockSpec((h, DY), _next_index(ts, h, s)),
                  pl.BlockSpec((4, HEAD, HEAD), fixed3), pl.BlockSpec((4, HEAD, HEAD), fixed3),
                  pl.BlockSpec((1, D), fixed2), pl.BlockSpec((SHORT_K, D), fixed2), pl.BlockSpec((1, D), fixed2)],
        out_specs=[pl.BlockSpec((ts, DZ), lambda i: (i, 0)), pl.BlockSpec((4, HEAD, HEAD), fixed3),
                   pl.BlockSpec((1, D), fixed2), pl.BlockSpec((SHORT_K, D), fixed2), pl.BlockSpec((1, D), fixed2)],
        out_shape=[jax.ShapeDtypeStruct((s, DZ), BF16), jax.ShapeDtypeStruct((4, HEAD, HEAD), F32),
                   jax.ShapeDtypeStruct((1, D), F32), jax.ShapeDtypeStruct((SHORT_K, D), F32),
                   jax.ShapeDtypeStruct((1, D), F32)],
        compiler_params=_params(("arbitrary",), 56),
    )(z, z, z, dy, dy, pool_w, pool_wt, pool_scale, sconv_w, sconv_b)


def _to_blocks(ref, r0, val):
    n = val.shape[0]
    for cb in range(LANE_BLOCKS):
        ref[cb, r0:r0 + n, :] = val[:, cb * LANE:(cb + 1) * LANE]


def _from_blocks(ref):
    return jnp.concatenate([ref[cb] for cb in range(LANE_BLOCKS)], axis=1)


def _shift_copies(src_ref, sh_ref, n, causal):
    def block(cb, carry):
        for b in range(1, SUBLANES):
            if causal:
                sh_ref[cb, b - 1, SUBLANES:n, :] = src_ref[cb, SUBLANES - b:n - b, :]
            else:
                sh_ref[cb, b - 1, 0:n - SUBLANES, :] = src_ref[cb, b:n - SUBLANES + b, :]
        return carry

    lax.fori_loop(0, LANE_BLOCKS, block, 0)


def _tap(src_ref, sh_ref, cb, first, n, d, causal):
    whole, b = (d // SUBLANES) * SUBLANES, d % SUBLANES
    start = first - whole if causal else first + whole
    if b == 0:
        return src_ref[cb, start:start + n, :]
    return sh_ref[cb, b - 1, start:start + n, :]


def _chunk_rows(rows, most):
    return max(n for n in range(CONV_ROWS, most + 1, CONV_ROWS) if rows % n == 0)


def _conv31(src_ref, sh_ref, w_ref, dst_ref, base, rows, causal):
    n = _chunk_rows(rows, 4 * CONV_ROWS)

    def block(cb, carry):
        for r0 in range(0, rows, n):
            acc = None
            for d in range(CONV_K):
                term = w_ref[cb, CONV_K - 1 - d:CONV_K - d, :] * _tap(src_ref, sh_ref, cb, base + r0, n, d, causal)
                acc = term if acc is None else acc + term
            dst_ref[cb, r0:r0 + n, :] = acc
        return carry

    lax.fori_loop(0, LANE_BLOCKS, block, 0)


def _conv31_tap_grads(d_ref, src_ref, sh_ref, dw_ref, base, rows):
    n = _chunk_rows(rows, 2 * CONV_ROWS)

    def block(cb, carry):
        sums = [None] * CONV_K
        for r0 in range(0, rows, n):
            d_blk = d_ref[cb, r0:r0 + n, :]
            for d in range(CONV_K):
                prod = d_blk * _tap(src_ref, sh_ref, cb, base + r0, n, d, True)
                part = prod[0:SUBLANES]
                for k in range(1, n // SUBLANES):
                    part = part + prod[k * SUBLANES:(k + 1) * SUBLANES]
                sums[d] = part if sums[d] is None else sums[d] + part
        for d in range(CONV_K):
            j = CONV_K - 1 - d
            dw_ref[cb, j:j + 1, :] += _colsum(sums[d])
        return carry

    lax.fori_loop(0, LANE_BLOCKS, block, 0)


def _sgu_bias_rows(sgu_b):
    return jnp.repeat(jnp.transpose(sgu_b), HEAD, axis=1)


def _odd_fwd(z, sln_g, sln_b, ws, sbias, dconv_w, dconv_b, dn_g, dn_b, l):
    s = z.shape[0]
    ts = min(s, 256)
    h = CONV_HALO

    def body(z_ref, zp_ref, lg_ref, lb_ref, ws_ref, sb_ref, cw_ref, cb_ref, ng_ref, nb_ref, o_ref, zz_ref, zc_ref,
             sh_ref):
        i = pl.program_id(0)
        vhat, _ = _ln_stats(z_ref[:, D:2 * D])
        vn = (vhat * lg_ref[...] + lb_ref[...]).astype(MXU_DTYPE)
        silu_gc, _ = _silu_and_grad(z_ref[:, 2 * D:3 * D])
        for n in range(ts // SGU_BLOCK):
            rows = slice(n * SGU_BLOCK, (n + 1) * SGU_BLOCK)
            sv = jnp.concatenate(
                [_dot(ws_ref[hd], vn[rows, hd * HEAD:(hd + 1) * HEAD], NN) for hd in range(4)], axis=1)
            sv = sv + sb_ref[...]
            o_ref[rows, 0:D] = (z_ref[rows, 0:D] * sv * silu_gc[rows]).astype(BF16)

        _to_blocks(zz_ref, 0, jnp.where(i > 0, zp_ref[:, 3 * D:4 * D] * _sigmoid(zp_ref[:, 4 * D:5 * D]), 0.0))
        _to_blocks(zz_ref, h, z_ref[:, 3 * D:4 * D] * _sigmoid(z_ref[:, 4 * D:5 * D]))
        _shift_copies(zz_ref, sh_ref, h + ts, True)
        _conv31(zz_ref, sh_ref, cw_ref, zc_ref, h, ts, True)
        zhat, _ = _ln_stats(_from_blocks(zc_ref) + cb_ref[...])
        silu_zn, _ = _silu_and_grad(zhat * ng_ref[...] + nb_ref[...])
        silu_gd, _ = _silu_and_grad(z_ref[:, 5 * D:6 * D])
        o_ref[:, D:2 * D] = (silu_zn * silu_gd).astype(BF16)

    fixed2 = lambda i: (0, 0)
    vec = pl.BlockSpec((1, D), fixed2)
    return pl.pallas_call(
        body, name=f"odd_fwd_{l}", grid=(s // ts,),
        in_specs=[pl.BlockSpec((ts, DZ), lambda i: (i, 0)), pl.BlockSpec((h, DZ), _prev_index(ts, h)),
                  vec, vec, pl.BlockSpec((4, SGU_BLOCK, SGU_BLOCK), lambda i: (0, 0, 0)),
                  pl.BlockSpec((SGU_BLOCK, D), fixed2), pl.BlockSpec((LANE_BLOCKS, CONV_K, LANE), lambda i: (0, 0, 0)),
                  vec, vec, vec],
        out_specs=pl.BlockSpec((ts, DY), lambda i: (i, 0)),
        out_shape=jax.ShapeDtypeStruct((s, DY), BF16),
        scratch_shapes=[pltpu.VMEM((LANE_BLOCKS, h + ts, LANE), F32), pltpu.VMEM((LANE_BLOCKS, ts, LANE), F32),
                        pltpu.VMEM((LANE_BLOCKS, SUBLANES - 1, h + ts, LANE), F32)],
        compiler_params=_params(("arbitrary",)),
    )(z, z, sln_g, sln_b, ws, sbias, dconv_w, dconv_b, dn_g, dn_b)


def _odd_bwd(z, dy, sln_g, sln_b, ws, wst, sbias, dconv_w, dconv_b, dn_g, dn_b, l):
    s = z.shape[0]
    ts = min(s, 256)
    h = CONV_HALO
    n_tiles = s // ts
    te = ts + h

    def body(z_ref, zp_ref, zn_ref, dy_ref, dyn_ref, lg_ref, lb_ref, ws_ref, wst_ref, sb_ref, cw_ref, cb_ref,
             ng_ref, nb_ref, dz_ref, dlg_ref, dlb_ref, dws_ref, dsb_ref, dcw_ref, dcb_ref, dng_ref, dnb_ref,
             zz_ref, zc_ref, dzc_ref, dzz_ref, dsb_acc, sh_ref):
        i = pl.program_id(0)
        more = i < n_tiles - 1

        @pl.when(i == 0)
        def _():
            for ref in (dlg_ref, dlb_ref, dws_ref, dsb_ref, dcw_ref, dcb_ref, dng_ref, dnb_ref, dsb_acc):
                ref[...] = jnp.zeros_like(ref)

        vhat, rstd_v = _ln_stats(z_ref[:, D:2 * D])
        lg = lg_ref[...]
        vn = (vhat * lg + lb_ref[...]).astype(MXU_DTYPE)
        u = z_ref[:, 0:D]
        silu_gc, dsilu_gc = _silu_and_grad(z_ref[:, 2 * D:3 * D])
        d_yc = dy_ref[:, 0:D]
        d_yc_u = d_yc * u
        d_sv = d_yc_u * silu_gc
        d_svb = d_sv.astype(MXU_DTYPE)
        sv_rows = []
        dvn_rows = []
        dsb = None
        for n in range(ts // SGU_BLOCK):
            rows = slice(n * SGU_BLOCK, (n + 1) * SGU_BLOCK)
            sv_parts = []
            dvn_parts = []
            for hd in range(4):
                cols = slice(hd * HEAD, (hd + 1) * HEAD)
                sv_parts.append(_dot(ws_ref[hd], vn[rows, cols], NN))
                dvn_parts.append(_dot(wst_ref[hd], d_svb[rows, cols], NN))
                dws_ref[hd] += _dot(d_svb[rows, cols], vn[rows, cols], NT)
            sv_rows.append(jnp.concatenate(sv_parts, axis=1) + sb_ref[...])
            dvn_rows.append(jnp.concatenate(dvn_parts, axis=1))
            dsb = d_sv[rows] if dsb is None else dsb + d_sv[rows]
        dsb_acc[...] += dsb

        @pl.when(i == n_tiles - 1)
        def _():
            for hd in range(4):
                blk = dsb_acc[:, hd * HEAD:(hd + 1) * HEAD]
                folded = blk[:, 0:LANE] + blk[:, LANE:HEAD]
                dsb_ref[hd:hd + 1, :] = _colsum(jnp.transpose(folded))
        sv = jnp.concatenate(sv_rows, axis=0)
        d_vn = jnp.concatenate(dvn_rows, axis=0)
        dz_ref[:, 0:D] = (d_yc * sv * silu_gc).astype(BF16)
        dz_ref[:, D:2 * D] = _ln_bwd_rows(d_vn * lg, vhat, rstd_v).astype(BF16)
        dz_ref[:, 2 * D:3 * D] = (d_yc_u * sv * dsilu_gc).astype(BF16)
        dlg_ref[...] += _colsum(d_vn * vhat)
        dlb_ref[...] += _colsum(d_vn)

        def gate(ref):
            return ref[:, 3 * D:4 * D] * _sigmoid(ref[:, 4 * D:5 * D])

        _to_blocks(zz_ref, 0, jnp.where(i > 0, gate(zp_ref), 0.0))
        _to_blocks(zz_ref, h, gate(z_ref))
        _to_blocks(zz_ref, h + ts, gate(zn_ref))
        _shift_copies(zz_ref, sh_ref, h + te, True)
        _conv31(zz_ref, sh_ref, cw_ref, zc_ref, h, te, True)
        zhat, rstd_z = _ln_stats(_from_blocks(zc_ref) + cb_ref[...])
        ng = ng_ref[...]
        silu_zn, dsilu_zn = _silu_and_grad(zhat * ng + nb_ref[...])
        gd = jnp.concatenate([z_ref[:, 5 * D:6 * D], zn_ref[:, 5 * D:6 * D]], axis=0)
        silu_gd, dsilu_gd = _silu_and_grad(gd)
        d_yd = jnp.concatenate([dy_ref[:, D:2 * D], dyn_ref[:, D:2 * D]], axis=0)
        d_zn = d_yd * silu_gd * dsilu_zn
        d_zc = _ln_bwd_rows(d_zn * ng, zhat, rstd_z)
        row = lax.broadcasted_iota(jnp.int32, (te, 1), 0)
        d_zc = jnp.where(jnp.logical_or(more, row < ts), d_zc, 0.0)
        _to_blocks(dzc_ref, 0, d_zc)
        dz_ref[:, 5 * D:6 * D] = (d_yd[:ts] * silu_zn[:ts] * dsilu_gd[:ts]).astype(BF16)
        dng_ref[...] += _colsum(d_zn[:ts] * zhat[:ts])
        dnb_ref[...] += _colsum(d_zn[:ts])
        dcb_ref[...] += _colsum(d_zc[:ts])
        _conv31_tap_grads(dzc_ref, zz_ref, sh_ref, dcw_ref, h, ts)
        _shift_copies(dzc_ref, sh_ref, te, False)
        _conv31(dzc_ref, sh_ref, cw_ref, dzz_ref, 0, ts, False)
        d_zz = _from_blocks(dzz_ref)
        a = z_ref[:, 3 * D:4 * D]
        sig_b = _sigmoid(z_ref[:, 4 * D:5 * D])
        dz_ref[:, 3 * D:4 * D] = (d_zz * sig_b).astype(BF16)
        dz_ref[:, 4 * D:5 * D] = (d_zz * a * sig_b * (1.0 - sig_b)).astype(BF16)

    fixed2 = lambda i: (0, 0)
    fixed3 = lambda i: (0, 0, 0)
    vec = pl.BlockSpec((1, D), fixed2)
    mat = pl.BlockSpec((4, SGU_BLOCK, SGU_BLOCK), fixed3)
    vec_shape = jax.ShapeDtypeStruct((1, D), F32)
    return pl.pallas_call(
        body, name=f"odd_bwd_{l}", grid=(n_tiles,),
        in_specs=[pl.BlockSpec((ts, DZ), lambda i: (i, 0)), pl.BlockSpec((h, DZ), _prev_index(ts, h)),
                  pl.BlockSpec((h, DZ), _next_index(ts, h, s)),
                  pl.BlockSpec((ts, DY), lambda i: (i, 0)), pl.BlockSpec((h, DY), _next_index(ts, h, s)),
                  vec, vec, mat, mat, pl.BlockSpec((SGU_BLOCK, D), fixed2),
                  pl.BlockSpec((LANE_BLOCKS, CONV_K, LANE), fixed3), vec, vec, vec],
        out_specs=[pl.BlockSpec((ts, DZ), lambda i: (i, 0)), vec, vec, mat, pl.BlockSpec((4, SGU_BLOCK), fixed2),
                   pl.BlockSpec((LANE_BLOCKS, CONV_K, LANE), fixed3), vec, vec, vec],
        out_shape=[jax.ShapeDtypeStruct((s, DZ), BF16), vec_shape, vec_shape,
                   jax.ShapeDtypeStruct((4, SGU_BLOCK, SGU_BLOCK), F32), jax.ShapeDtypeStruct((4, SGU_BLOCK), F32),
                   jax.ShapeDtypeStruct((LANE_BLOCKS, CONV_K, LANE), F32), vec_shape, vec_shape, vec_shape],
        scratch_shapes=[pltpu.VMEM((LANE_BLOCKS, h + te, LANE), F32), pltpu.VMEM((LANE_BLOCKS, te, LANE), F32),
                        pltpu.VMEM((LANE_BLOCKS, te, LANE), F32), pltpu.VMEM((LANE_BLOCKS, ts, LANE), F32),
                        pltpu.VMEM((SGU_BLOCK, D), F32),
                        pltpu.VMEM((LANE_BLOCKS, SUBLANES - 1, h + te, LANE), F32)],
        compiler_params=_params(("arbitrary",), 60),
    )(z, z, z, dy, dy, sln_g, sln_b, ws, wst, sbias, dconv_w, dconv_b, dn_g, dn_b)


def _remote(src, dst, send_sems, recv_sems, k, to):
    return pltpu.make_async_remote_copy(src_ref=src, dst_ref=dst, send_sem=send_sems.at[k],
                                        recv_sem=recv_sems.at[k], device_id=to, device_id_type=MESH)


def _other_chips(x, y):
    return [(1 - x, y, 2 * (1 - x) + y), (x, 1 - y, 2 * x + 1 - y), (1 - x, 1 - y, 2 * (1 - x) + 1 - y)]


def _cast_own(w_stack, pos_arr, name):
    slots, rows, cols = w_stack.shape
    half = rows // 2

    def body(pos_ref, w_ref, o_ref):
        o_ref[...] = w_ref[...].astype(BF16)

    grid_spec = pltpu.PrefetchScalarGridSpec(
        num_scalar_prefetch=1, grid=(slots, 2),
        in_specs=[pl.BlockSpec((None, half, cols), lambda s, h, pos: (s, h, 0))],
        out_specs=pl.BlockSpec((None, None, None, half, cols), lambda s, h, pos: (pos[1], h, s, 0, 0)))
    return pl.pallas_call(
        body, name=name, grid_spec=grid_spec, out_shape=jax.ShapeDtypeStruct((NQ, 2, slots, half, cols), BF16),
        compiler_params=_params(("arbitrary",) * 2),
    )(pos_arr, w_stack)


def _gather_weights(win_g, wout_g, small_sh):
    def body(win_in, wout_in, small, win_g, wout_g, small_g, send_sems, recv_sems):
        del win_in, wout_in
        x, y, c = _mesh_pos()
        me = 2 * x + y
        sibling = (x, y, 1 - c)
        chips = _other_chips(x, y)

        sends = []
        for j, (cx, cy, _) in enumerate(chips):
            to = (cx, cy, c)
            sends.append(_remote(win_g.at[me, c], win_g.at[me, c], send_sems, recv_sems, j, to))
            sends.append(_remote(wout_g.at[me, c], wout_g.at[me, c], send_sems, recv_sems, 3 + j, to))
            sends.append(_remote(small, small_g.at[me], send_sems, recv_sems, 6 + j, to))
        for cp in sends:
            cp.start()
        passed = []
        for j, (_, _, q) in enumerate(chips):
            got_in = win_g.at[q, c]
            got_out = wout_g.at[q, c]
            _remote(got_in, got_in, send_sems, recv_sems, j, sibling).wait_recv()
            cp = _remote(got_in, got_in, send_sems, recv_sems, 9 + j, sibling)
            cp.start()
            passed.append(cp)
            _remote(got_out, got_out, send_sems, recv_sems, 3 + j, sibling).wait_recv()
            cp = _remote(got_out, got_out, send_sems, recv_sems, 12 + j, sibling)
            cp.start()
            passed.append(cp)
            _remote(small, small_g.at[q], send_sems, recv_sems, 6 + j, sibling).wait_recv()
        for j, (_, _, q) in enumerate(chips):
            from_in = win_g.at[q, 1 - c]
            from_out = wout_g.at[q, 1 - c]
            _remote(from_in, from_in, send_sems, recv_sems, 9 + j, sibling).wait_recv()
            _remote(from_out, from_out, send_sems, recv_sems, 12 + j, sibling).wait_recv()
        for cp in sends + passed:
            cp.wait_send()

    return pl.pallas_call(
        body, name="gather_weights",
        in_specs=[HBM_SPEC, HBM_SPEC, HBM_SPEC], out_specs=[HBM_SPEC, HBM_SPEC, HBM_SPEC],
        out_shape=[jax.ShapeDtypeStruct(win_g.shape, win_g.dtype), jax.ShapeDtypeStruct(wout_g.shape, wout_g.dtype),
                   jax.ShapeDtypeStruct((NQ,) + small_sh.shape, small_sh.dtype)],
        input_output_aliases={0: 0, 1: 1},
        scratch_shapes=[pltpu.SemaphoreType.DMA((15,)), pltpu.SemaphoreType.DMA((15,))],
    )(win_g, wout_g, small_sh)


def _hbm(a):
    return pltpu.with_memory_space_constraint(a, pltpu.HBM)


def _split_start(body, name, sources, landings):
    n_src, n_land = len(sources), len(landings)
    n_buf = n_src + n_land

    def kernel_body(*refs):
        ins, outs = refs[:n_buf], refs[n_buf:]
        send_sems, recv_sems, token = outs[0], outs[1], outs[2 + n_buf]
        body(ins[:n_src], ins[n_src:], send_sems, recv_sems)
        token[...] = jnp.zeros_like(token)

    bufs = [_hbm(a) for a in sources] + [
        _hbm(lax.empty(s.shape, s.dtype) if isinstance(s, jax.ShapeDtypeStruct) else s) for s in landings]
    n_sem = getattr(body, "n_copies")
    out = pl.pallas_call(
        kernel_body, name=name,
        out_shape=(pltpu.SemaphoreType.DMA((n_sem,)), pltpu.SemaphoreType.DMA((n_sem,)),
                   *[pltpu.HBM(b.shape, b.dtype) for b in bufs], jax.ShapeDtypeStruct((8, LANE), F32)),
        in_specs=(HBM_SPEC,) * n_buf,
        out_specs=(SEM_SPEC, SEM_SPEC, *([HBM_SPEC] * n_buf), pl.BlockSpec(memory_space=pltpu.VMEM)),
        input_output_aliases={k: 2 + k for k in range(n_buf)},
        compiler_params=pltpu.CompilerParams(has_side_effects=SIDE_EFFECT),
    )(*bufs)
    return out[0], out[1], list(out[2:2 + n_src]), list(out[2 + n_src:2 + n_buf]), out[2 + n_buf]


def _split_wait(body, name, send_sems, recv_sems, sources, landings, after):
    n_src, n_land = len(sources), len(landings)
    n_buf = n_src + n_land

    def kernel_body(*refs):
        ins = refs[:n_buf]
        body(ins[:n_src], ins[n_src:], refs[n_buf], refs[n_buf + 1])

    bufs = list(sources) + list(landings)
    out = pl.pallas_call(
        kernel_body, name=name,
        out_shape=tuple(pltpu.HBM(b.shape, b.dtype) for b in bufs),
        in_specs=(*([HBM_SPEC] * n_buf), SEM_SPEC, SEM_SPEC, pl.BlockSpec(memory_space=pl.ANY)),
        out_specs=(HBM_SPEC,) * n_buf,
        input_output_aliases={k: k for k in range(n_buf)},
        compiler_params=pltpu.CompilerParams(has_side_effects=SIDE_EFFECT),
    )(*bufs, send_sems, recv_sems, after)
    return list(out[:n_src]), list(out[n_src:])


def _gather_rest_copies(start):
    def body(srcs, lands, send_sems, recv_sems):
        del srcs
        x, y, c = _mesh_pos()
        me = 2 * x + y
        for j, (cx, cy, q) in enumerate(_other_chips(x, y)):
            to = (cx, cy, c)
            for k, gathered in enumerate(lands):
                if start:
                    _remote(gathered.at[me, c], gathered.at[me, c], send_sems, recv_sems, 3 * k + j, to).start()
                else:
                    cp = _remote(gathered.at[me, c], gathered.at[q, c], send_sems, recv_sems, 3 * k + j, to)
                    cp.wait_send()
                    cp.wait_recv()

    body.n_copies = 6
    return body


def _gather_rest_forward(win_g, wout_g):
    def body(win_in, wout_in, win_g, wout_g, send_sems, recv_sems):
        del win_in, wout_in
        x, y, c = _mesh_pos()
        sibling = (x, y, 1 - c)
        passed = []
        for j, (_, _, q) in enumerate(_other_chips(x, y)):
            got_in = win_g.at[q, c]
            got_out = wout_g.at[q, c]
            passed.append(_remote(got_in, got_in, send_sems, recv_sems, j, sibling))
            passed.append(_remote(got_out, got_out, send_sems, recv_sems, 3 + j, sibling))
        for cp in passed:
            cp.start()
        for j, (_, _, q) in enumerate(_other_chips(x, y)):
            from_in = win_g.at[q, 1 - c]
            from_out = wout_g.at[q, 1 - c]
            _remote(from_in, from_in, send_sems, recv_sems, j, sibling).wait_recv()
            _remote(from_out, from_out, send_sems, recv_sems, 3 + j, sibling).wait_recv()
        for cp in passed:
            cp.wait_send()

    return pl.pallas_call(
        body, name="gather_rest_forward",
        in_specs=[HBM_SPEC] * 2, out_specs=[HBM_SPEC] * 2,
        out_shape=[jax.ShapeDtypeStruct(win_g.shape, win_g.dtype), jax.ShapeDtypeStruct(wout_g.shape, wout_g.dtype)],
        input_output_aliases={0: 0, 1: 1},
        scratch_shapes=[pltpu.SemaphoreType.DMA((6,)), pltpu.SemaphoreType.DMA((6,))],
    )(win_g, wout_g)


def _allreduce_small(groups, after):
    pieces = [p for _, _, members in groups for _, p in members]
    n_in, n_g = len(pieces), len(groups)

    def body(*refs):
        ins = refs[:n_in]
        outs = refs[-(2 * n_g + 2):-(n_g + 2)]
        alls = refs[-(n_g + 2):-2]
        send_sems, recv_sems = refs[-2:]
        x, y, c = _mesh_pos()
        me = 4 * x + 2 * y + c
        sibling = (x, y, 1 - c)
        chips = _other_chips(x, y)

        k = 0
        for (rows, cols, members), all_ref in zip(groups, alls):
            all_ref[me] = jnp.zeros((rows, cols), F32)
            for first, piece in members:
                n, width = piece.shape
                all_ref[me, first:first + n, 0:width] = ins[k][...]
                k += 1

        sends, passed = [], []
        for g, all_ref in enumerate(alls):
            sends.append(_remote(all_ref.at[me], all_ref.at[me], send_sems, recv_sems, 7 * g, sibling))
            for j, (cx, cy, _) in enumerate(chips):
                sends.append(_remote(all_ref.at[me], all_ref.at[me], send_sems, recv_sems, 7 * g + 1 + j, (cx, cy, c)))
        for cp in sends:
            cp.start()
        for j, (cx, cy, _) in enumerate(chips):
            for g, all_ref in enumerate(alls):
                got = all_ref.at[4 * cx + 2 * cy + c]
                _remote(got, got, send_sems, recv_sems, 7 * g + 1 + j, sibling).wait_recv()
                cp = _remote(got, got, send_sems, recv_sems, 7 * g + 4 + j, sibling)
                cp.start()
                passed.append(cp)
        for g, all_ref in enumerate(alls):
            got = all_ref.at[4 * x + 2 * y + 1 - c]
            _remote(got, got, send_sems, recv_sems, 7 * g, sibling).wait_recv()
            for j, (cx, cy, _) in enumerate(chips):
                got = all_ref.at[4 * cx + 2 * cy + 1 - c]
                _remote(got, got, send_sems, recv_sems, 7 * g + 4 + j, sibling).wait_recv()
        for cp in sends + passed:
            cp.wait_send()
        for o_ref, all_ref in zip(outs, alls):
            total = all_ref[0]
            for dev in range(1, 8):
                total = total + all_ref[dev]
            o_ref[...] = total

    vmem = pl.BlockSpec(memory_space=pltpu.VMEM)
    return pl.pallas_call(
        body, name="allreduce_small",
        in_specs=[vmem] * n_in + _after_spec(after),
        out_specs=[vmem] * n_g,
        out_shape=[jax.ShapeDtypeStruct((rows, cols), F32) for rows, cols, _ in groups],
        scratch_shapes=[pltpu.VMEM((8, rows, cols), F32) for rows, cols, _ in groups]
        + [pltpu.SemaphoreType.DMA((7 * n_g,)), pltpu.SemaphoreType.DMA((7 * n_g,))],
        compiler_params=_params(None, 56),
    )(*pieces, *_after_args(after))


def _pair_exchange(gin, gout, l):
    hi, ho = D // 2, RQ // 2

    def body(gin_ref, gout_ref, rin_ref, rout_ref, send_sems, recv_sems):
        x, y, c = _mesh_pos()
        sibling = (x, y, 1 - c)
        a = _remote(gin_ref.at[:, pl.ds((1 - c) * hi, hi), :], rin_ref, send_sems, recv_sems, 0, sibling)
        b = _remote(gout_ref.at[:, pl.ds((1 - c) * ho, ho), :], rout_ref, send_sems, recv_sems, 1, sibling)
        a.start()
        b.start()
        a.wait()
        b.wait()

    return pl.pallas_call(
        body, name=f"pair_exchange_{l}",
        in_specs=[HBM_SPEC, HBM_SPEC], out_specs=[HBM_SPEC, HBM_SPEC],
        out_shape=[jax.ShapeDtypeStruct((NQ, hi, WQ), F32), jax.ShapeDtypeStruct((NQ, ho, D), F32)],
        scratch_shapes=[pltpu.SemaphoreType.DMA((2,)), pltpu.SemaphoreType.DMA((2,))],
    )(gin, gout)


def _pair_sum(g, r, pos_arr, name):
    nq, rows, cols = r.shape
    tr = min(rows, 256)
    nt = rows // tr

    def body(pos_ref, g_ref, r_ref, ob_ref, own_ref):
        total = g_ref[...] + r_ref[...]
        ob_ref[...] = total.astype(BF16)

        @pl.when(pl.program_id(1) == pos_ref[1])
        def _():
            own_ref[...] = total

    blk = (None, tr, cols)
    grid_spec = pltpu.PrefetchScalarGridSpec(
        num_scalar_prefetch=1, grid=(nt, nq),
        in_specs=[pl.BlockSpec(blk, lambda t, q, pos: (q, pos[0] * nt + t, 0)),
                  pl.BlockSpec(blk, lambda t, q, pos: (q, t, 0))],
        out_specs=[pl.BlockSpec(blk, lambda t, q, pos: (q, t, 0)),
                   pl.BlockSpec((tr, cols), lambda t, q, pos: (t, 0))])
    return pl.pallas_call(
        body, name=name, grid_spec=grid_spec,
        out_shape=[jax.ShapeDtypeStruct(r.shape, BF16), jax.ShapeDtypeStruct((rows, cols), F32)],
        compiler_params=_params(("arbitrary",) * 2),
    )(pos_arr, g, r)


def _chip_copies(start):
    def body(srcs, lands, send_sems, recv_sems):
        pin, pout = srcs
        rin, rout = lands
        x, y, c = _mesh_pos()
        for j, (cx, cy, q) in enumerate(_other_chips(x, y)):
            to = (cx, cy, c)
            for k, (src, dst) in enumerate(((pin, rin), (pout, rout))):
                cp = _remote(src.at[q], dst.at[j], send_sems, recv_sems, 3 * k + j, to)
                if start:
                    cp.start()
                else:
                    cp.wait_send()
                    cp.wait_recv()

    body.n_copies = 6
    return body


def _chip_sum(own, r, pos_arr, name):
    rows, cols = own.shape
    tr = min(rows, 256)

    def body(pos_ref, p_ref, r0_ref, r1_ref, r2_ref, o_ref):
        o_ref[...] = ((p_ref[...] + r0_ref[...].astype(F32)) + r1_ref[...].astype(F32)) + r2_ref[...].astype(F32)

    def peer(j):
        return pl.BlockSpec((None, tr, cols), lambda t, pos: (j, t, 0))

    grid_spec = pltpu.PrefetchScalarGridSpec(
        num_scalar_prefetch=1, grid=(rows // tr,),
        in_specs=[pl.BlockSpec((tr, cols), lambda t, pos: (t, 0)), peer(0), peer(1), peer(2)],
        out_specs=pl.BlockSpec((None, tr, cols), lambda t, pos: (pos[0], t, 0)))
    return pl.pallas_call(
        body, name=name, grid_spec=grid_spec, out_shape=jax.ShapeDtypeStruct((2, rows, cols), F32),
        compiler_params=_params(("arbitrary",)),
    )(pos_arr, own, r, r, r)


def _pair_share(gin, gout, l):
    def body(gin_in, gout_in, gin_ref, gout_ref, send_sems, recv_sems):
        del gin_in, gout_in
        x, y, c = _mesh_pos()
        sibling = (x, y, 1 - c)
        sends = [_remote(gin_ref.at[c], gin_ref.at[c], send_sems, recv_sems, 0, sibling),
                 _remote(gout_ref.at[c], gout_ref.at[c], send_sems, recv_sems, 1, sibling)]
        for cp in sends:
            cp.start()
        _remote(gin_ref.at[1 - c], gin_ref.at[1 - c], send_sems, recv_sems, 0, sibling).wait_recv()
        _remote(gout_ref.at[1 - c], gout_ref.at[1 - c], send_sems, recv_sems, 1, sibling).wait_recv()
        for cp in sends:
            cp.wait_send()

    return pl.pallas_call(
        body, name=f"pair_share_{l}",
        in_specs=[HBM_SPEC, HBM_SPEC], out_specs=[HBM_SPEC, HBM_SPEC],
        out_shape=[jax.ShapeDtypeStruct(gin.shape, F32), jax.ShapeDtypeStruct(gout.shape, F32)],
        input_output_aliases={0: 0, 1: 1},
        scratch_shapes=[pltpu.SemaphoreType.DMA((2,)), pltpu.SemaphoreType.DMA((2,))],
    )(gin, gout)


def _adamw_large(w, m, v, g, i, prev, name):
    _, rows, cols = w.shape
    half = rows // 2
    tr = min(half, 256)
    nt = half // tr

    def body(w_ref, m_ref, v_ref, g_ref, *rest):
        go_ref, d_ref, mo_ref, vo_ref = rest[-4:]
        gv = g_ref[...]
        go_ref[...] = gv
        d_ref[...], mo_ref[...], vo_ref[...] = _adamw_math(w_ref[...], gv, m_ref[...], v_ref[...])

    full = pl.BlockSpec((None, tr, cols), lambda h, t: (i, h * nt + t, 0))
    out = jax.ShapeDtypeStruct(w.shape, F32)
    carried = [] if prev is None else list(prev)
    return pl.pallas_call(
        body, name=name, grid=(2, nt),
        in_specs=[full, full, full, pl.BlockSpec((None, tr, cols), lambda h, t: (h, t, 0))]
        + [pl.BlockSpec(memory_space=pl.ANY)] * len(carried),
        out_specs=[full] * 4, out_shape=[out] * 4,
        input_output_aliases={4 + k: k for k in range(len(carried))},
        compiler_params=_params(("arbitrary",) * 2),
    )(w, m, v, g, *carried)


def _adamw(w, g, m, v, name):
    shape = w.shape
    w2, g2, m2, v2 = (t.reshape(-1, shape[-1]) for t in (w, g, m, v))
    rows, cols = w2.shape
    tr = 256 if rows % 256 == 0 else rows

    def body(w_ref, g_ref, m_ref, v_ref, d_ref, mo_ref, vo_ref):
        d_ref[...], mo_ref[...], vo_ref[...] = _adamw_math(w_ref[...], g_ref[...], m_ref[...], v_ref[...])

    blk = pl.BlockSpec((tr, cols), lambda i: (i, 0))
    out = jax.ShapeDtypeStruct((rows, cols), F32)
    d, mo, vo = pl.pallas_call(
        body, name=name, grid=(rows // tr,), in_specs=[blk] * 4, out_specs=[blk] * 3, out_shape=[out] * 3,
        compiler_params=_params(("arbitrary",)),
    )(w2, g2, m2, v2)
    return d.reshape(shape), mo.reshape(shape), vo.reshape(shape)


def _layer_slot(l):
    return (l % 2) * 2 + l // 2


def kernel(x, ln_g, ln_b, w_in_even, w_out_even, pool_w, pool_scale, sconv_w, sconv_b, w_in_odd, w_out_odd, sgu_ln_g, sgu_ln_b, sgu_w, sgu_b, dconv_w, dconv_b, dnorm_g, dnorm_b, loss_target, m_ln_g, m_ln_b, m_w_in_even, m_w_out_even, m_pool_w, m_pool_scale, m_sconv_w, m_sconv_b, m_w_in_odd, m_w_out_odd, m_sgu_ln_g, m_sgu_ln_b, m_sgu_w, m_sgu_b, m_dconv_w, m_dconv_b, m_dnorm_g, m_dnorm_b, v_ln_g, v_ln_b, v_w_in_even, v_w_out_even, v_pool_w, v_pool_scale, v_sconv_w, v_sconv_b, v_w_in_odd, v_w_out_odd, v_sgu_ln_g, v_sgu_ln_b, v_sgu_w, v_sgu_b, v_dconv_w, v_dconv_b, v_dnorm_g, v_dnorm_b):
    weights = dict(ln_g=ln_g, ln_b=ln_b, w_in_even=w_in_even, w_out_even=w_out_even, pool_w=pool_w,
                   pool_scale=pool_scale, sconv_w=sconv_w, sconv_b=sconv_b, w_in_odd=w_in_odd, w_out_odd=w_out_odd,
                   sgu_ln_g=sgu_ln_g, sgu_ln_b=sgu_ln_b, sgu_w=sgu_w, sgu_b=sgu_b, dconv_w=dconv_w,
                   dconv_b=dconv_b, dnorm_g=dnorm_g, dnorm_b=dnorm_b)
    moments_m = dict(ln_g=m_ln_g, ln_b=m_ln_b, w_in_even=m_w_in_even, w_out_even=m_w_out_even, pool_w=m_pool_w,
                     pool_scale=m_pool_scale, sconv_w=m_sconv_w, sconv_b=m_sconv_b, w_in_odd=m_w_in_odd,
                     w_out_odd=m_w_out_odd, sgu_ln_g=m_sgu_ln_g, sgu_ln_b=m_sgu_ln_b, sgu_w=m_sgu_w, sgu_b=m_sgu_b,
                     dconv_w=m_dconv_w, dconv_b=m_dconv_b, dnorm_g=m_dnorm_g, dnorm_b=m_dnorm_b)
    moments_v = dict(ln_g=v_ln_g, ln_b=v_ln_b, w_in_even=v_w_in_even, w_out_even=v_w_out_even, pool_w=v_pool_w,
                     pool_scale=v_pool_scale, sconv_w=v_sconv_w, sconv_b=v_sconv_b, w_in_odd=v_w_in_odd,
                     w_out_odd=v_w_out_odd, sgu_ln_g=v_sgu_ln_g, sgu_ln_b=v_sgu_ln_b, sgu_w=v_sgu_w, sgu_b=v_sgu_b,
                     dconv_w=v_dconv_w, dconv_b=v_dconv_b, dnorm_g=v_dnorm_g, dnorm_b=v_dnorm_b)
    names = list(weights)

    xd, yd, cd = _mesh_pos()
    chip = 2 * xd + yd
    pos_arr = jnp.stack([cd, chip]).astype(jnp.int32)

    small_sh = jnp.concatenate(
        [sconv_w.reshape(6, HEAD), sgu_ln_g, sgu_ln_b, dconv_b, dnorm_g, dnorm_b, dconv_w.reshape(62, HEAD),
         jnp.zeros((2, HEAD), F32), pool_w.reshape(512, HEAD)], axis=0)
    win_first, wout_first, small_g = _gather_weights(
        _cast_own(w_in_even[0:1], pos_arr, "cast_win_first"), _cast_own(w_out_even[0:1], pos_arr, "cast_wout_first"),
        small_sh)
    small_g = lax.dynamic_update_slice(small_g, small_sh[None], (chip, 0, 0))
    later_in = jnp.concatenate([w_in_even[1:2], w_in_odd], axis=0)
    later_out = jnp.concatenate([w_out_even[1:2], w_out_odd], axis=0)
    g_send, g_recv, _, g_lands, g_token = _split_start(
        _gather_rest_copies(True), "gather_rest_start", [],
        [_cast_own(later_in, pos_arr, "cast_win_rest"), _cast_own(later_out, pos_arr, "cast_wout_rest")])

    def layer_weights(slot):
        return (win_first, wout_first, 0) if slot == 0 else (win_rest, wout_rest, slot - 1)

    def full_rows(lo, n):
        return jnp.transpose(small_g[:, lo:lo + n], (1, 0, 2)).reshape(n, D)

    sconv_w_f = full_rows(Q_SCONV_W, 6).reshape(2, SHORT_K, D)
    sln_g_f = full_rows(Q_SLN_G, 2)
    sln_b_f = full_rows(Q_SLN_B, 2)
    dconv_b_f = full_rows(Q_DCONV_B, 2)
    dn_g_f = full_rows(Q_DN_G, 2)
    dn_b_f = full_rows(Q_DN_B, 2)
    dconv_w_f = jnp.transpose(full_rows(Q_DCONV_W, 62).reshape(2, CONV_K, LANE_BLOCKS, LANE), (0, 2, 1, 3))
    pool_w_f = jnp.transpose(small_g[:, Q_POOL_W:].reshape(NQ, 2, 4, 64, HEAD), (1, 2, 0, 3, 4)).reshape(2, 4, HEAD, HEAD)
    pool_w_b = pool_w_f.astype(BF16)
    pool_wt_b = jnp.swapaxes(pool_w_f, 2, 3).astype(BF16)
    idx = jnp.arange(SGU_BLOCK)
    mask = (idx[None, :] // 64) <= (idx[:, None] // 64)
    ws_f = jnp.where(mask[None, None], sgu_w, 0.0)
    ws_b = ws_f.astype(BF16)
    wst_b = jnp.swapaxes(ws_f, 2, 3).astype(BF16)

    def row(a, i):
        return a[i:i + 1]

    x_f = x[0]
    x_b = x_f.astype(BF16)
    saved = []
    for l in range(NL):
        i, slot = l // 2, _layer_slot(l)
        if l == 1:
            _, g_lands = _split_wait(_gather_rest_copies(False), "gather_rest_wait", g_send, g_recv, [], g_lands, x_b)
            win_rest, wout_rest = _gather_rest_forward(g_lands[0], g_lands[1])
        win_g, wout_g, k = layer_weights(slot)
        z = _proj_in(x_b, win_g, k, l, g_token if l == 0 else None)
        if l % 2 == 0:
            ycat = _even_fwd(z, pool_w_b[i], row(pool_scale, i), sconv_w_f[i], row(sconv_b, i), l)
        else:
            ycat = _odd_fwd(z, row(sln_g_f, i), row(sln_b_f, i), ws_b[i], _sgu_bias_rows(sgu_b[i]),
                            dconv_w_f[i], row(dconv_b_f, i), row(dn_g_f, i), row(dn_b_f, i), l)
        x_next, x_next_b, xhat, rstd = _proj_out_ln(ycat, wout_g, k, l, x_f, row(ln_g, l), row(ln_b, l))
        saved.append((x_b, z, ycat, xhat, rstd))
        x_f, x_b = x_next, x_next_b

    loss_part, dxn = _loss_grad(x_f, loss_target[0])
    loss = lax.psum(loss_part[0, 0], ("x", "y", "c"))

    small = {}
    d_ln_g = [None] * NL
    d_ln_b = [None] * NL
    large = {"w_in_even": None, "w_out_even": None, "w_in_odd": None, "w_out_odd": None}
    pending = None
    token = None

    def finish(exchange, after):
        lx, send, recv, srcs, lands, own_in, own_out = exchange
        _, (r_in, r_out) = _split_wait(_chip_copies(False), f"chip_wait_{lx}", send, recv, srcs, lands, after)
        fin = _chip_sum(own_in, r_in, pos_arr, f"chip_sum_in_{lx}")
        fout = _chip_sum(own_out, r_out, pos_arr, f"chip_sum_out_{lx}")
        gs_in, gs_out = _pair_share(fin, fout, lx)
        kind = "even" if lx % 2 == 0 else "odd"
        for nm, gs in ((f"w_in_{kind}", gs_in), (f"w_out_{kind}", gs_out)):
            large[nm] = _adamw_large(weights[nm], moments_m[nm], moments_v[nm], gs, lx // 2, large[nm],
                                     f"adamw_{nm}_{lx // 2}")

    for l in reversed(range(NL)):
        i, slot = l // 2, _layer_slot(l)
        win_g, wout_g, k = layer_weights(slot)
        xin_b, z, ycat, xhat, rstd = saved[l]
        dr, dr_b, d_ln_g[l], d_ln_b[l] = _ln_bwd(dxn, xhat, rstd, row(ln_g, l), l, token)
        dy = _dycat(dr_b, wout_g, k, l)
        gout = _dwout(ycat, dr_b, l).reshape(NQ, RQ, D)
        if l % 2 == 0:
            dz, d_pw, d_ps, d_cw, d_cb = _even_bwd(z, dy, pool_w_b[i], pool_wt_b[i], row(pool_scale, i),
                                                   sconv_w_f[i], row(sconv_b, i), l)
            small[("pool_w", i)] = d_pw
            small[("pool_scale", i)] = d_ps
            small[("sconv_w", i)] = d_cw
            small[("sconv_b", i)] = d_cb
        else:
            dz, d_lg, d_lb, d_ws, d_sb, d_cw, d_cb, d_ng, d_nb = _odd_bwd(
                z, dy, row(sln_g_f, i), row(sln_b_f, i), ws_b[i], wst_b[i], _sgu_bias_rows(sgu_b[i]),
                dconv_w_f[i], row(dconv_b_f, i), row(dn_g_f, i), row(dn_b_f, i), l)
            small[("sgu_ln_g", i)] = d_lg
            small[("sgu_ln_b", i)] = d_lb
            small[("sgu_w", i)] = jnp.where(mask[None], d_ws, 0.0)
            small[("sgu_b", i)] = d_sb
            small[("dconv_w", i)] = jnp.transpose(d_cw, (1, 0, 2)).reshape(CONV_K, D)
            small[("dconv_b", i)] = d_cb
            small[("dnorm_g", i)] = d_ng
            small[("dnorm_b", i)] = d_nb
        gin = _dwin(xin_b, dz, l)
        dxn = _dx(dz, win_g, k, l, dr)
        if pending is not None:
            finish(pending, dxn)
        rin, rout = _pair_exchange(gin, gout, l)
        pin_b, pin_own = _pair_sum(gin, rin, pos_arr, f"pair_sum_in_{l}")
        pout_b, pout_own = _pair_sum(gout, rout, pos_arr, f"pair_sum_out_{l}")
        send, recv, srcs, lands, token = _split_start(
            _chip_copies(True), f"chip_start_{l}", [pin_b, pout_b],
            [jax.ShapeDtypeStruct((3,) + pin_b.shape[1:], BF16), jax.ShapeDtypeStruct((3,) + pout_b.shape[1:], BF16)])
        pending = (l, send, recv, srcs, lands, pin_own, pout_own)
    grad_x = dxn[None]

    def both(name, first, step):
        return [(first, small[(name, 0)]), (first + step, small[(name, 1)])]

    vectors = ([(R_LN_G + l, d_ln_g[l]) for l in range(NL)] + [(R_LN_B + l, d_ln_b[l]) for l in range(NL)]
               + both("pool_scale", R_PSCALE, 1) + both("sconv_b", R_SCONV_B, 1) + both("sconv_w", R_SCONV_W, SHORT_K)
               + both("sgu_ln_g", R_SLN_G, 1) + both("sgu_ln_b", R_SLN_B, 1) + both("dconv_b", R_DCONV_B, 1)
               + both("dnorm_g", R_DN_G, 1) + both("dnorm_b", R_DN_B, 1) + both("dconv_w", R_DCONV_W, CONV_K)
               + both("sgu_b", R_SGU_B, 4))
    sgu_w_rows = 4 * SGU_BLOCK
    pool_w_rows = 4 * HEAD
    total, total_sgu_w, total_pool_w = _allreduce_small(
        [(R_VECTORS, D, vectors),
         (2 * sgu_w_rows, SGU_BLOCK, [(i * sgu_w_rows, small[("sgu_w", i)].reshape(sgu_w_rows, SGU_BLOCK)) for i in range(2)]),
         (2 * pool_w_rows, HEAD, [(i * pool_w_rows, small[("pool_w", i)].reshape(pool_w_rows, HEAD)) for i in range(2)])],
        token)
    finish(pending, total)

    def mine(a):
        return lax.dynamic_slice_in_dim(a, chip * HEAD, HEAD, axis=a.ndim - 1)

    grads = {
        "ln_g": total[R_LN_G:R_LN_G + 4],
        "ln_b": total[R_LN_B:R_LN_B + 4],
        "pool_scale": total[R_PSCALE:R_PSCALE + 2],
        "sconv_b": total[R_SCONV_B:R_SCONV_B + 2],
        "sconv_w": mine(total[R_SCONV_W:R_SCONV_W + 6].reshape(2, SHORT_K, D)),
        "sgu_ln_g": mine(total[R_SLN_G:R_SLN_G + 2]),
        "sgu_ln_b": mine(total[R_SLN_B:R_SLN_B + 2]),
        "dconv_b": mine(total[R_DCONV_B:R_DCONV_B + 2]),
        "dnorm_g": mine(total[R_DN_G:R_DN_G + 2]),
        "dnorm_b": mine(total[R_DN_B:R_DN_B + 2]),
        "dconv_w": mine(total[R_DCONV_W:R_DCONV_W + 62].reshape(2, CONV_K, D)),
        "sgu_b": total[R_SGU_B:R_SGU_B + 8, 0:SGU_BLOCK].reshape(2, 4, SGU_BLOCK),
        "sgu_w": total_sgu_w.reshape(2, 4, SGU_BLOCK, SGU_BLOCK),
        "pool_w": lax.dynamic_slice_in_dim(total_pool_w.reshape(2, 4, HEAD, HEAD), chip * 64, 64, axis=2),
    }

    deltas, new_m, new_v = {}, {}, {}
    for name in names:
        if name in large:
            grads[name], deltas[name], new_m[name], new_v[name] = large[name]
        else:
            deltas[name], new_m[name], new_v[name] = _adamw(
                weights[name], grads[name], moments_m[name], moments_v[name], f"adamw_{name}")

    return (loss, grad_x, *[grads[n] for n in names], *[deltas[n] for n in names],
            *[new_m[n] for n in names], *[new_v[n] for n in names])
```

```python
import jax
import jax.numpy as jnp
from jax import lax
from jax.experimental import pallas as pl
from jax.experimental.pallas import tpu as pltpu

F32 = jnp.float32
BF16 = jnp.bfloat16
MXU_DTYPE = BF16

D = 1024
DZ = 6144
DY = 2048
NQ = 4
WQ = DZ // NQ
RQ = DY // NQ
NL = 4
ALPHA = (2 * NL) ** 0.25
LN_EPS = 1e-5
CONV_K = 31
SHORT_K = 3
SGU_BLOCK = 128
HEAD = 256
POOL_HALO = 16
CONV_HALO = 32
LANE = 128
SUBLANES = 8
CONV_ROWS = 32
LANE_BLOCKS = 8
MIB = 1024 * 1024

ADAM_LR = 0.001
ADAM_B1 = 0.9
ADAM_B2 = 0.999
ADAM_EPS = 1e-08
ADAM_WD = 0.01
ADAM_STEP = 10

NN = ((1,), (0,))
NT = ((1,), (1,))
TN = ((0,), (0,))
MESH = pl.DeviceIdType.MESH
HBM_SPEC = pl.BlockSpec(memory_space=pltpu.HBM)
SEM_SPEC = pl.BlockSpec(memory_space=pltpu.SEMAPHORE)
SIDE_EFFECT = pltpu.SideEffectType.DATAFLOW_SIDE_EFFECTING

R_LN_G, R_LN_B, R_PSCALE, R_SCONV_B, R_SCONV_W = 0, 4, 8, 10, 12
R_SLN_G, R_SLN_B, R_DCONV_B, R_DN_G, R_DN_B, R_DCONV_W = 18, 20, 22, 24, 26, 28
R_SGU_B, R_VECTORS = 90, 104
Q_SCONV_W, Q_SLN_G, Q_SLN_B, Q_DCONV_B, Q_DN_G, Q_DN_B, Q_DCONV_W, Q_POOL_W, Q_ROWS = 0, 6, 8, 10, 12, 14, 16, 80, 592


def _dot(a, b, dims):
    return lax.dot_general(a.astype(MXU_DTYPE), b.astype(MXU_DTYPE), (dims, ((), ())),
                           preferred_element_type=F32)


def _params(semantics=None, vmem_mib=48):
    return pltpu.CompilerParams(dimension_semantics=semantics, vmem_limit_bytes=vmem_mib * MIB)


def _sigmoid(v):
    return jax.nn.sigmoid(v)


def _silu_and_grad(v):
    s = _sigmoid(v)
    return v * s, s * (1.0 + v * (1.0 - s))


def _ln_stats(v):
    mu = jnp.mean(v, axis=-1, keepdims=True)
    vc = v - mu
    var = jnp.mean(vc * vc, axis=-1, keepdims=True)
    rstd = lax.rsqrt(var + LN_EPS)
    return vc * rstd, rstd


def _ln_bwd_rows(dxhat, xhat, rstd):
    m1 = jnp.mean(dxhat, axis=-1, keepdims=True)
    m2 = jnp.mean(dxhat * xhat, axis=-1, keepdims=True)
    return rstd * (dxhat - m1 - xhat * m2)


def _colsum(v):
    return jnp.sum(v, axis=0, keepdims=True)


def _adamw_math(w, g, m, v):
    m_new = ADAM_B1 * m + (1.0 - ADAM_B1) * g
    v_new = ADAM_B2 * v + (1.0 - ADAM_B2) * (g * g)
    m_hat = m_new / (1.0 - ADAM_B1 ** ADAM_STEP)
    v_hat = v_new / (1.0 - ADAM_B2 ** ADAM_STEP)
    return -ADAM_LR * (m_hat / (jnp.sqrt(v_hat) + ADAM_EPS) + ADAM_WD * w), m_new, v_new


def _mesh_pos():
    return lax.axis_index("x"), lax.axis_index("y"), lax.axis_index("c")


def _after_spec(after):
    return [] if after is None else [pl.BlockSpec(memory_space=pl.ANY)]


def _after_args(after):
    return [] if after is None else [after]


def _proj_in(xb, win_g, k, l, after=None):
    s = xb.shape[0]
    tm = min(s, 1024)

    def body(x_ref, w_ref, *rest):
        rest[-1][...] = _dot(x_ref[...], w_ref[...].reshape(D, WQ), NN)

    return pl.pallas_call(
        body, name=f"proj_in_{l}", grid=(NQ, s // tm),
        in_specs=[pl.BlockSpec((tm, D), lambda q, m: (m, 0)),
                  pl.BlockSpec((None, 2, None, D // 2, WQ), lambda q, m: (q, 0, k, 0, 0))] + _after_spec(after),
        out_specs=pl.BlockSpec((tm, WQ), lambda q, m: (m, q)),
        out_shape=jax.ShapeDtypeStruct((s, DZ), F32),
        compiler_params=_params(("arbitrary", "arbitrary")),
    )(xb, win_g, *_after_args(after))


def _proj_out_ln(ycat, wout_g, k, l, x, g, b):
    s = x.shape[0]
    tm = min(s, 512)

    def body(y_ref, w_ref, x_ref, g_ref, b_ref, xn_ref, xb_ref, xh_ref, rs_ref):
        y = _dot(y_ref[...], w_ref[...].reshape(DY, D), NN)
        xhat, rstd = _ln_stats(ALPHA * x_ref[...] + y)
        xn = xhat * g_ref[...] + b_ref[...]
        xn_ref[...] = xn
        xb_ref[...] = xn.astype(BF16)
        xh_ref[...] = xhat
        rs_ref[...] = rstd

    row = lambda m: (m, 0)
    fixed = lambda m: (0, 0)
    return pl.pallas_call(
        body, name=f"proj_out_ln_{l}", grid=(s // tm,),
        in_specs=[pl.BlockSpec((tm, DY), row),
                  pl.BlockSpec((NQ, 2, None, RQ // 2, D), lambda m: (0, 0, k, 0, 0)),
                  pl.BlockSpec((tm, D), row), pl.BlockSpec((1, D), fixed), pl.BlockSpec((1, D), fixed)],
        out_specs=[pl.BlockSpec((tm, D), row), pl.BlockSpec((tm, D), row), pl.BlockSpec((tm, D), row),
                   pl.BlockSpec((tm, 1), row)],
        out_shape=[jax.ShapeDtypeStruct((s, D), F32), jax.ShapeDtypeStruct((s, D), BF16),
                   jax.ShapeDtypeStruct((s, D), F32), jax.ShapeDtypeStruct((s, 1), F32)],
        compiler_params=_params(("arbitrary",)),
    )(ycat, wout_g, x, g, b)


def _loss_grad(xl, target):
    s = xl.shape[0]
    ts = min(s, 512)

    def body(x_ref, t_ref, loss_ref, dx_ref):
        @pl.when(pl.program_id(0) == 0)
        def _():
            loss_ref[...] = jnp.zeros_like(loss_ref)
        err = x_ref[...] - t_ref[...]
        dx_ref[...] = err * (1.0 / D)
        loss_ref[...] += 0.5 * jnp.sum(jnp.mean(err * err, axis=-1, keepdims=True), axis=0, keepdims=True)

    row = lambda m: (m, 0)
    return pl.pallas_call(
        body, name="loss_grad", grid=(s // ts,),
        in_specs=[pl.BlockSpec((ts, D), row), pl.BlockSpec((ts, D), row)],
        out_specs=[pl.BlockSpec((1, 1), lambda m: (0, 0)), pl.BlockSpec((ts, D), row)],
        out_shape=[jax.ShapeDtypeStruct((1, 1), F32), jax.ShapeDtypeStruct((s, D), F32)],
        compiler_params=_params(("arbitrary",)),
    )(xl, target)


def _ln_bwd(dxn, xhat, rstd, g, l, after=None):
    s = dxn.shape[0]
    ts = min(s, 512)

    def body(d_ref, xh_ref, rs_ref, g_ref, *rest):
        dr_ref, drb_ref, dg_ref, db_ref = rest[-4:]

        @pl.when(pl.program_id(0) == 0)
        def _():
            dg_ref[...] = jnp.zeros_like(dg_ref)
            db_ref[...] = jnp.zeros_like(db_ref)
        d = d_ref[...]
        xhat_v = xh_ref[...]
        dr = _ln_bwd_rows(d * g_ref[...], xhat_v, rs_ref[...])
        dr_ref[...] = dr
        drb_ref[...] = dr.astype(BF16)
        dg_ref[...] += _colsum(d * xhat_v)
        db_ref[...] += _colsum(d)

    row = lambda m: (m, 0)
    fixed = lambda m: (0, 0)
    return pl.pallas_call(
        body, name=f"ln_bwd_{l}", grid=(s // ts,),
        in_specs=[pl.BlockSpec((ts, D), row), pl.BlockSpec((ts, D), row), pl.BlockSpec((ts, 1), row),
                  pl.BlockSpec((1, D), fixed)] + _after_spec(after),
        out_specs=[pl.BlockSpec((ts, D), row), pl.BlockSpec((ts, D), row), pl.BlockSpec((1, D), fixed),
                   pl.BlockSpec((1, D), fixed)],
        out_shape=[jax.ShapeDtypeStruct((s, D), F32), jax.ShapeDtypeStruct((s, D), BF16),
                   jax.ShapeDtypeStruct((1, D), F32), jax.ShapeDtypeStruct((1, D), F32)],
        compiler_params=_params(("arbitrary",)),
    )(dxn, xhat, rstd, g, *_after_args(after))


def _dycat(drb, wout_g, k, l):
    s = drb.shape[0]
    tm = min(s, 512)

    def body(d_ref, w_ref, o_ref):
        o_ref[...] = _dot(d_ref[...], w_ref[...].reshape(DY, D), NT)

    return pl.pallas_call(
        body, name=f"dycat_{l}", grid=(s // tm,),
        in_specs=[pl.BlockSpec((tm, D), lambda m: (m, 0)),
                  pl.BlockSpec((NQ, 2, None, RQ // 2, D), lambda m: (0, 0, k, 0, 0))],
        out_specs=pl.BlockSpec((tm, DY), lambda m: (m, 0)),
        out_shape=jax.ShapeDtypeStruct((s, DY), F32),
        compiler_params=_params(("arbitrary",)),
    )(drb, wout_g)


def _dwout(ycat, drb, l):
    s = drb.shape[0]
    tk = min(s, 1024)

    def body(y_ref, d_ref, o_ref):
        @pl.when(pl.program_id(0) == 0)
        def _():
            o_ref[...] = jnp.zeros_like(o_ref)

        o_ref[...] += _dot(y_ref[...], d_ref[...], TN)

    return pl.pallas_call(
        body, name=f"dwout_{l}", grid=(s // tk,),
        in_specs=[pl.BlockSpec((tk, DY), lambda k: (k, 0)), pl.BlockSpec((tk, D), lambda k: (k, 0))],
        out_specs=pl.BlockSpec((DY, D), lambda k: (0, 0)),
        out_shape=jax.ShapeDtypeStruct((DY, D), F32),
        compiler_params=_params(("arbitrary",)),
    )(ycat, drb)


def _dwin(xb, dzb, l):
    s = xb.shape[0]
    tk = min(s, 1024)

    def body(x_ref, d_ref, o_ref):
        @pl.when(pl.program_id(1) == 0)
        def _():
            o_ref[...] = jnp.zeros_like(o_ref)

        o_ref[...] += _dot(x_ref[...], d_ref[...], TN)

    return pl.pallas_call(
        body, name=f"dwin_{l}", grid=(NQ, s // tk),
        in_specs=[pl.BlockSpec((tk, D), lambda q, k: (k, 0)), pl.BlockSpec((tk, WQ), lambda q, k: (k, q))],
        out_specs=pl.BlockSpec((None, D, WQ), lambda q, k: (q, 0, 0)),
        out_shape=jax.ShapeDtypeStruct((NQ, D, WQ), F32),
        compiler_params=_params(("arbitrary", "arbitrary")),
    )(xb, dzb)


def _dx(dzb, win_g, k, l, dr, after=None):
    s = dzb.shape[0]
    tm = min(s, 1024)

    def body(d_ref, w_ref, r_ref, *rest):
        o_ref = rest[-1]

        @pl.when(pl.program_id(1) == 0)
        def _():
            o_ref[...] = ALPHA * r_ref[...]

        o_ref[...] += _dot(d_ref[...], w_ref[...].reshape(D, WQ), NT)

    return pl.pallas_call(
        body, name=f"dx_{l}", grid=(s // tm, NQ),
        in_specs=[pl.BlockSpec((tm, WQ), lambda m, q: (m, q)),
                  pl.BlockSpec((None, 2, None, D // 2, WQ), lambda m, q: (q, 0, k, 0, 0)),
                  pl.BlockSpec((tm, D), lambda m, q: (m, 0))] + _after_spec(after),
        out_specs=pl.BlockSpec((tm, D), lambda m, q: (m, 0)),
        out_shape=jax.ShapeDtypeStruct((s, D), F32),
        compiler_params=_params(("arbitrary", "arbitrary")),
    )(dzb, win_g, dr, *_after_args(after))


def _window_sums(e, causal):
    n = e.shape[0]

    def shifted(a, k):
        return pltpu.roll(a, k if causal else n - k, axis=0)

    parts = []
    acc = e
    for step, k in enumerate((1, 2, 4, 8)):
        acc = acc + shifted(acc, k)
        parts.append(acc[:, 0:HEAD])
        if step < 3:
            acc = acc[:, HEAD:]
    return jnp.concatenate(parts, axis=1)


def _pool_counts(first_pos, rows):
    t1 = (lax.broadcasted_iota(jnp.int32, (rows, 1), 0) + first_pos + 1).astype(F32)
    lane = lax.broadcasted_iota(jnp.int32, (1, D), 1)
    win = jnp.where(lane < HEAD, 2.0, jnp.where(lane < 2 * HEAD, 4.0, jnp.where(lane < 3 * HEAD, 8.0, 16.0)))
    return jnp.minimum(t1, win)


def _group_dot(v, w_ref, dims):
    return jnp.concatenate(
        [_dot(v[:, g * HEAD:(g + 1) * HEAD], w_ref[g], dims) for g in range(4)], axis=1)


def _prev_index(ts, halo):
    return lambda i: (jnp.maximum(i * (ts // halo) - 1, 0), 0)


def _next_index(ts, halo, s):
    return lambda i: (jnp.minimum((i + 1) * (ts // halo), s // halo - 1), 0)


def _even_fwd(z, pool_w, pool_scale, sconv_w, sconv_b, l):
    s = z.shape[0]
    ts = min(s, 256)
    h = POOL_HALO

    def body(z_ref, zp_ref, pw_ref, ps_ref, cw_ref, cb_ref, o_ref):
        i = pl.program_id(0)
        inside = i > 0
        xa = z_ref[:, 0:D]
        xa_ext = jnp.concatenate([jnp.where(inside, zp_ref[:, 0:D], 0.0), xa], axis=0)
        sums = _window_sums(xa_ext, True)[h:]
        pooled = sums / _pool_counts(i * ts, ts) - xa
        p = _group_dot(pooled, pw_ref, NN)
        silu_ga, _ = _silu_and_grad(z_ref[:, D:2 * D])
        o_ref[:, 0:D] = (p * ps_ref[...] * silu_ga).astype(BF16)

        q_main = z_ref[:, 4 * D:5 * D] * z_ref[:, 2 * D:3 * D]
        q_prev = jnp.where(inside, zp_ref[:, 4 * D:5 * D] * zp_ref[:, 2 * D:3 * D], 0.0)
        q_ext = jnp.concatenate([q_prev, q_main], axis=0)
        cv = cw_ref[2:3, :] * q_main + cb_ref[...]
        cv = cv + cw_ref[1:2, :] * pltpu.roll(q_ext, 1, axis=0)[h:]
        cv = cv + cw_ref[0:1, :] * pltpu.roll(q_ext, 2, axis=0)[h:]
        silu_gb, _ = _silu_and_grad(z_ref[:, 5 * D:6 * D])
        o_ref[:, D:2 * D] = (z_ref[:, 3 * D:4 * D] * cv * silu_gb).astype(BF16)

    fixed2 = lambda i: (0, 0)
    return pl.pallas_call(
        body, name=f"even_fwd_{l}", grid=(s // ts,),
        in_specs=[pl.BlockSpec((ts, DZ), lambda i: (i, 0)), pl.BlockSpec((h, DZ), _prev_index(ts, h)),
                  pl.BlockSpec((4, HEAD, HEAD), lambda i: (0, 0, 0)), pl.BlockSpec((1, D), fixed2),
                  pl.BlockSpec((SHORT_K, D), fixed2), pl.BlockSpec((1, D), fixed2)],
        out_specs=pl.BlockSpec((ts, DY), lambda i: (i, 0)),
        out_shape=jax.ShapeDtypeStruct((s, DY), BF16),
        compiler_params=_params(("arbitrary",)),
    )(z, z, pool_w, pool_scale, sconv_w, sconv_b)


def _even_bwd(z, dy, pool_w, pool_wt, pool_scale, sconv_w, sconv_b, l):
    s = z.shape[0]
    ts = min(s, 256)
    h = POOL_HALO
    n_tiles = s // ts

    def body(z_ref, zp_ref, zn_ref, dy_ref, dyn_ref, pw_ref, pwt_ref, ps_ref, cw_ref, cb_ref,
             dz_ref, dpw_ref, dps_ref, dcw_ref, dcb_ref):
        i = pl.program_id(0)
        inside = i > 0
        more = i < n_tiles - 1

        @pl.when(i == 0)
        def _():
            dpw_ref[...] = jnp.zeros_like(dpw_ref)
            dps_ref[...] = jnp.zeros_like(dps_ref)
            dcw_ref[...] = jnp.zeros_like(dcw_ref)
            dcb_ref[...] = jnp.zeros_like(dcb_ref)

        def with_next(main_ref, next_ref, lo):
            return jnp.concatenate([main_ref[:, lo:lo + D], next_ref[:, lo:lo + D]], axis=0)

        xa_ext = jnp.concatenate(
            [jnp.where(inside, zp_ref[:, 0:D], 0.0), z_ref[:, 0:D], zn_ref[:, 0:D]], axis=0)
        counts = _pool_counts(i * ts, ts + h)
        pooled = _window_sums(xa_ext, True)[h:] / counts - xa_ext[h:]
        p = _group_dot(pooled, pw_ref, NN)
        silu_ga, dsilu_ga = _silu_and_grad(with_next(z_ref, zn_ref, D))
        d_ya = with_next(dy_ref, dyn_ref, 0)
        scale = ps_ref[...]
        d_p = d_ya * scale * silu_ga
        d_pooled = _group_dot(d_p, pwt_ref, NN)
        row = lax.broadcasted_iota(jnp.int32, (ts + h, 1), 0)
        d_pooled = jnp.where(jnp.logical_or(more, row < ts), d_pooled, 0.0)
        d_xa = _window_sums(d_pooled / counts, False)[:ts] - d_pooled[:ts]
        dz_ref[:, 0:D] = d_xa.astype(BF16)
        dz_ref[:, D:2 * D] = (d_ya[:ts] * p[:ts] * scale * dsilu_ga[:ts]).astype(BF16)
        dps_ref[...] += _colsum(d_ya[:ts] * p[:ts] * silu_ga[:ts])
        for g in range(4):
            cols = slice(g * HEAD, (g + 1) * HEAD)
            dpw_ref[g] += _dot(pooled[:ts, cols], d_p[:ts, cols], TN)

        cg = z_ref[:, 4 * D:5 * D]
        hh = z_ref[:, 2 * D:3 * D]
        bg = z_ref[:, 3 * D:4 * D]
        q_main = cg * hh
        q_prev = jnp.where(inside, zp_ref[:, 4 * D:5 * D] * zp_ref[:, 2 * D:3 * D], 0.0)
        q_ext = jnp.concatenate([q_prev, q_main], axis=0)
        q_1 = pltpu.roll(q_ext, 1, axis=0)[h:]
        q_2 = pltpu.roll(q_ext, 2, axis=0)[h:]
        cv = cw_ref[2:3, :] * q_main + cw_ref[1:2, :] * q_1 + cw_ref[0:1, :] * q_2 + cb_ref[...]
        silu_gb, dsilu_gb = _silu_and_grad(with_next(z_ref, zn_ref, 5 * D))
        d_yb = with_next(dy_ref, dyn_ref, D)
        d_cv = d_yb * with_next(z_ref, zn_ref, 3 * D) * silu_gb
        d_cv = jnp.where(jnp.logical_or(more, row < ts), d_cv, 0.0)
        d_cv0 = d_cv[:ts]
        n_ext = ts + h
        d_q = (cw_ref[2:3, :] * d_cv0 + cw_ref[1:2, :] * pltpu.roll(d_cv, n_ext - 1, axis=0)[:ts]
               + cw_ref[0:1, :] * pltpu.roll(d_cv, n_ext - 2, axis=0)[:ts])
        dz_ref[:, 2 * D:3 * D] = (d_q * cg).astype(BF16)
        dz_ref[:, 3 * D:4 * D] = (d_yb[:ts] * cv * silu_gb[:ts]).astype(BF16)
        dz_ref[:, 4 * D:5 * D] = (d_q * hh).astype(BF16)
        dz_ref[:, 5 * D:6 * D] = (d_yb[:ts] * bg * cv * dsilu_gb[:ts]).astype(BF16)
        dcb_ref[...] += _colsum(d_cv0)
        dcw_ref[2:3, :] += _colsum(d_cv0 * q_main)
        dcw_ref[1:2, :] += _colsum(d_cv0 * q_1)
        dcw_ref[0:1, :] += _colsum(d_cv0 * q_2)

    fixed2 = lambda i: (0, 0)
    fixed3 = lambda i: (0, 0, 0)
    return pl.pallas_call(
        body, name=f"even_bwd_{l}", grid=(n_tiles,),
        in_specs=[pl.BlockSpec((ts, DZ), lambda i: (i, 0)), pl.BlockSpec((h, DZ), _prev_index(ts, h)),
                  pl.BlockSpec((h, DZ), _next_index(ts, h, s)),
                  pl.BlockSpec((ts, DY), lambda i: (i, 0)), pl.BlockSpec((h, DY), _next_index(ts, h, s)),
                  pl.BlockSpec((4, HEAD, HEAD), fixed3), pl.BlockSpec((4, HEAD, HEAD), fixed3),
                  pl.BlockSpec((1, D), fixed2), pl.BlockSpec((SHORT_K, D), fixed2), pl.BlockSpec((1, D), fixed2)],
        out_specs=[pl.BlockSpec((ts, DZ), lambda i: (i, 0)), pl.BlockSpec((4, HEAD, HEAD), fixed3),
                   pl.BlockSpec((1, D), fixed2), pl.BlockSpec((SHORT_K, D), fixed2), pl.BlockSpec((1, D), fixed2)],
        out_shape=[jax.ShapeDtypeStruct((s, DZ), BF16), jax.ShapeDtypeStruct((4, HEAD, HEAD), F32),
                   jax.ShapeDtypeStruct((1, D), F32), jax.ShapeDtypeStruct((SHORT_K, D), F32),
                   jax.ShapeDtypeStruct((1, D), F32)],
        compiler_params=_params(("arbitrary",), 56),
    )(z, z, z, dy, dy, pool_w, pool_wt, pool_scale, sconv_w, sconv_b)


def _to_blocks(ref, r0, val):
    n = val.shape[0]
    for cb in range(LANE_BLOCKS):
        ref[cb, r0:r0 + n, :] = val[:, cb * LANE:(cb + 1) * LANE]


def _from_blocks(ref):
    return jnp.concatenate([ref[cb] for cb in range(LANE_BLOCKS)], axis=1)


def _shift_copies(src_ref, sh_ref, n, causal):
    def block(cb, carry):
        for b in range(1, SUBLANES):
            if causal:
                sh_ref[cb, b - 1, SUBLANES:n, :] = src_ref[cb, SUBLANES - b:n - b, :]
            else:
                sh_ref[cb, b - 1, 0:n - SUBLANES, :] = src_ref[cb, b:n - SUBLANES + b, :]
        return carry

    lax.fori_loop(0, LANE_BLOCKS, block, 0)


def _tap(src_ref, sh_ref, cb, first, n, d, causal):
    whole, b = (d // SUBLANES) * SUBLANES, d % SUBLANES
    start = first - whole if causal else first + whole
    if b == 0:
        return src_ref[cb, start:start + n, :]
    return sh_ref[cb, b - 1, start:start + n, :]


def _chunk_rows(rows, most):
    return max(n for n in range(CONV_ROWS, most + 1, CONV_ROWS) if rows % n == 0)


def _conv31(src_ref, sh_ref, w_ref, dst_ref, base, rows, causal):
    n = _chunk_rows(rows, 4 * CONV_ROWS)

    def block(cb, carry):
        for r0 in range(0, rows, n):
            acc = None
            for d in range(CONV_K):
                term = w_ref[cb, CONV_K - 1 - d:CONV_K - d, :] * _tap(src_ref, sh_ref, cb, base + r0, n, d, causal)
                acc = term if acc is None else acc + term
            dst_ref[cb, r0:r0 + n, :] = acc
        return carry

    lax.fori_loop(0, LANE_BLOCKS, block, 0)


def _conv31_tap_grads(d_ref, src_ref, sh_ref, dw_ref, base, rows):
    n = _chunk_rows(rows, 2 * CONV_ROWS)

    def block(cb, carry):
        sums = [None] * CONV_K
        for r0 in range(0, rows, n):
            d_blk = d_ref[cb, r0:r0 + n, :]
            for d in range(CONV_K):
                prod = d_blk * _tap(src_ref, sh_ref, cb, base + r0, n, d, True)
                part = prod[0:SUBLANES]
                for k in range(1, n // SUBLANES):
                    part = part + prod[k * SUBLANES:(k + 1) * SUBLANES]
                sums[d] = part if sums[d] is None else sums[d] + part
        for d in range(CONV_K):
            j = CONV_K - 1 - d
            dw_ref[cb, j:j + 1, :] += _colsum(sums[d])
        return carry

    lax.fori_loop(0, LANE_BLOCKS, block, 0)


def _sgu_bias_rows(sgu_b):
    return jnp.repeat(jnp.transpose(sgu_b), HEAD, axis=1)


def _odd_fwd(z, sln_g, sln_b, ws, sbias, dconv_w, dconv_b, dn_g, dn_b, l):
    s = z.shape[0]
    ts = min(s, 256)
    h = CONV_HALO

    def body(z_ref, zp_ref, lg_ref, lb_ref, ws_ref, sb_ref, cw_ref, cb_ref, ng_ref, nb_ref, o_ref, zz_ref, zc_ref,
             sh_ref):
        i = pl.program_id(0)
        vhat, _ = _ln_stats(z_ref[:, D:2 * D])
        vn = (vhat * lg_ref[...] + lb_ref[...]).astype(MXU_DTYPE)
        silu_gc, _ = _silu_and_grad(z_ref[:, 2 * D:3 * D])
        for n in range(ts // SGU_BLOCK):
            rows = slice(n * SGU_BLOCK, (n + 1) * SGU_BLOCK)
            sv = jnp.concatenate(
                [_dot(ws_ref[hd], vn[rows, hd * HEAD:(hd + 1) * HEAD], NN) for hd in range(4)], axis=1)
            sv = sv + sb_ref[...]
            o_ref[rows, 0:D] = (z_ref[rows, 0:D] * sv * silu_gc[rows]).astype(BF16)

        _to_blocks(zz_ref, 0, jnp.where(i > 0, zp_ref[:, 3 * D:4 * D] * _sigmoid(zp_ref[:, 4 * D:5 * D]), 0.0))
        _to_blocks(zz_ref, h, z_ref[:, 3 * D:4 * D] * _sigmoid(z_ref[:, 4 * D:5 * D]))
        _shift_copies(zz_ref, sh_ref, h + ts, True)
        _conv31(zz_ref, sh_ref, cw_ref, zc_ref, h, ts, True)
        zhat, _ = _ln_stats(_from_blocks(zc_ref) + cb_ref[...])
        silu_zn, _ = _silu_and_grad(zhat * ng_ref[...] + nb_ref[...])
        silu_gd, _ = _silu_and_grad(z_ref[:, 5 * D:6 * D])
        o_ref[:, D:2 * D] = (silu_zn * silu_gd).astype(BF16)

    fixed2 = lambda i: (0, 0)
    vec = pl.BlockSpec((1, D), fixed2)
    return pl.pallas_call(
        body, name=f"odd_fwd_{l}", grid=(s // ts,),
        in_specs=[pl.BlockSpec((ts, DZ), lambda i: (i, 0)), pl.BlockSpec((h, DZ), _prev_index(ts, h)),
                  vec, vec, pl.BlockSpec((4, SGU_BLOCK, SGU_BLOCK), lambda i: (0, 0, 0)),
                  pl.BlockSpec((SGU_BLOCK, D), fixed2), pl.BlockSpec((LANE_BLOCKS, CONV_K, LANE), lambda i: (0, 0, 0)),
                  vec, vec, vec],
        out_specs=pl.BlockSpec((ts, DY), lambda i: (i, 0)),
        out_shape=jax.ShapeDtypeStruct((s, DY), BF16),
        scratch_shapes=[pltpu.VMEM((LANE_BLOCKS, h + ts, LANE), F32), pltpu.VMEM((LANE_BLOCKS, ts, LANE), F32),
                        pltpu.VMEM((LANE_BLOCKS, SUBLANES - 1, h + ts, LANE), F32)],
        compiler_params=_params(("arbitrary",)),
    )(z, z, sln_g, sln_b, ws, sbias, dconv_w, dconv_b, dn_g, dn_b)


def _odd_bwd(z, dy, sln_g, sln_b, ws, wst, sbias, dconv_w, dconv_b, dn_g, dn_b, l):
    s = z.shape[0]
    ts = min(s, 256)
    h = CONV_HALO
    n_tiles = s // ts
    te = ts + h

    def body(z_ref, zp_ref, zn_ref, dy_ref, dyn_ref, lg_ref, lb_ref, ws_ref, wst_ref, sb_ref, cw_ref, cb_ref,
             ng_ref, nb_ref, dz_ref, dlg_ref, dlb_ref, dws_ref, dsb_ref, dcw_ref, dcb_ref, dng_ref, dnb_ref,
             zz_ref, zc_ref, dzc_ref, dzz_ref, dsb_acc, sh_ref):
        i = pl.program_id(0)
        more = i < n_tiles - 1

        @pl.when(i == 0)
        def _():
            for ref in (dlg_ref, dlb_ref, dws_ref, dsb_ref, dcw_ref, dcb_ref, dng_ref, dnb_ref, dsb_acc):
                ref[...] = jnp.zeros_like(ref)

        vhat, rstd_v = _ln_stats(z_ref[:, D:2 * D])
        lg = lg_ref[...]
        vn = (vhat * lg + lb_ref[...]).astype(MXU_DTYPE)
        u = z_ref[:, 0:D]
        silu_gc, dsilu_gc = _silu_and_grad(z_ref[:, 2 * D:3 * D])
        d_yc = dy_ref[:, 0:D]
        d_yc_u = d_yc * u
        d_sv = d_yc_u * silu_gc
        d_svb = d_sv.astype(MXU_DTYPE)
        sv_rows = []
        dvn_rows = []
        dsb = None
        for n in range(ts // SGU_BLOCK):
            rows = slice(n * SGU_BLOCK, (n + 1) * SGU_BLOCK)
            sv_parts = []
            dvn_parts = []
            for hd in range(4):
                cols = slice(hd * HEAD, (hd + 1) * HEAD)
                sv_parts.append(_dot(ws_ref[hd], vn[rows, cols], NN))
                dvn_parts.append(_dot(wst_ref[hd], d_svb[rows, cols], NN))
                dws_ref[hd] += _dot(d_svb[rows, cols], vn[rows, cols], NT)
            sv_rows.append(jnp.concatenate(sv_parts, axis=1) + sb_ref[...])
            dvn_rows.append(jnp.concatenate(dvn_parts, axis=1))
            dsb = d_sv[rows] if dsb is None else dsb + d_sv[rows]
        dsb_acc[...] += dsb

        @pl.when(i == n_tiles - 1)
        def _():
            for hd in range(4):
                blk = dsb_acc[:, hd * HEAD:(hd + 1) * HEAD]
                folded = blk[:, 0:LANE] + blk[:, LANE:HEAD]
                dsb_ref[hd:hd + 1, :] = _colsum(jnp.transpose(folded))
        sv = jnp.concatenate(sv_rows, axis=0)
        d_vn = jnp.concatenate(dvn_rows, axis=0)
        dz_ref[:, 0:D] = (d_yc * sv * silu_gc).astype(BF16)
        dz_ref[:, D:2 * D] = _ln_bwd_rows(d_vn * lg, vhat, rstd_v).astype(BF16)
        dz_ref[:, 2 * D:3 * D] = (d_yc_u * sv * dsilu_gc).astype(BF16)
        dlg_ref[...] += _colsum(d_vn * vhat)
        dlb_ref[...] += _colsum(d_vn)

        def gate(ref):
            return ref[:, 3 * D:4 * D] * _sigmoid(ref[:, 4 * D:5 * D])

        _to_blocks(zz_ref, 0, jnp.where(i > 0, gate(zp_ref), 0.0))
        _to_blocks(zz_ref, h, gate(z_ref))
        _to_blocks(zz_ref, h + ts, gate(zn_ref))
        _shift_copies(zz_ref, sh_ref, h + te, True)
        _conv31(zz_ref, sh_ref, cw_ref, zc_ref, h, te, True)
        zhat, rstd_z = _ln_stats(_from_blocks(zc_ref) + cb_ref[...])
        ng = ng_ref[...]
        silu_zn, dsilu_zn = _silu_and_grad(zhat * ng + nb_ref[...])
        gd = jnp.concatenate([z_ref[:, 5 * D:6 * D], zn_ref[:, 5 * D:6 * D]], axis=0)
        silu_gd, dsilu_gd = _silu_and_grad(gd)
        d_yd = jnp.concatenate([dy_ref[:, D:2 * D], dyn_ref[:, D:2 * D]], axis=0)
        d_zn = d_yd * silu_gd * dsilu_zn
        d_zc = _ln_bwd_rows(d_zn * ng, zhat, rstd_z)
        row = lax.broadcasted_iota(jnp.int32, (te, 1), 0)
        d_zc = jnp.where(jnp.logical_or(more, row < ts), d_zc, 0.0)
        _to_blocks(dzc_ref, 0, d_zc)
        dz_ref[:, 5 * D:6 * D] = (d_yd[:ts] * silu_zn[:ts] * dsilu_gd[:ts]).astype(BF16)
        dng_ref[...] += _colsum(d_zn[:ts] * zhat[:ts])
        dnb_ref[...] += _colsum(d_zn[:ts])
        dcb_ref[...] += _colsum(d_zc[:ts])
        _conv31_tap_grads(dzc_ref, zz_ref, sh_ref, dcw_ref, h, ts)
        _shift_copies(dzc_ref, sh_ref, te, False)
        _conv31(dzc_ref, sh_ref, cw_ref, dzz_ref, 0, ts, False)
        d_zz = _from_blocks(dzz_ref)
        a = z_ref[:, 3 * D:4 * D]
        sig_b = _sigmoid(z_ref[:, 4 * D:5 * D])
        dz_ref[:, 3 * D:4 * D] = (d_zz * sig_b).astype(BF16)
        dz_ref[:, 4 * D:5 * D] = (d_zz * a * sig_b * (1.0 - sig_b)).astype(BF16)

    fixed2 = lambda i: (0, 0)
    fixed3 = lambda i: (0, 0, 0)
    vec = pl.BlockSpec((1, D), fixed2)
    mat = pl.BlockSpec((4, SGU_BLOCK, SGU_BLOCK), fixed3)
    vec_shape = jax.ShapeDtypeStruct((1, D), F32)
    return pl.pallas_call(
        body, name=f"odd_bwd_{l}", grid=(n_tiles,),
        in_specs=[pl.BlockSpec((ts, DZ), lambda i: (i, 0)), pl.BlockSpec((h, DZ), _prev_index(ts, h)),
                  pl.BlockSpec((h, DZ), _next_index(ts, h, s)),
                  pl.BlockSpec((ts, DY), lambda i: (i, 0)), pl.BlockSpec((h, DY), _next_index(ts, h, s)),
                  vec, vec, mat, mat, pl.BlockSpec((SGU_BLOCK, D), fixed2),
                  pl.BlockSpec((LANE_BLOCKS, CONV_K, LANE), fixed3), vec, vec, vec],
        out_specs=[pl.BlockSpec((ts, DZ), lambda i: (i, 0)), vec, vec, mat, pl.BlockSpec((4, SGU_BLOCK), fixed2),
                   pl.BlockSpec((LANE_BLOCKS, CONV_K, LANE), fixed3), vec, vec, vec],
        out_shape=[jax.ShapeDtypeStruct((s, DZ), BF16), vec_shape, vec_shape,
                   jax.ShapeDtypeStruct((4, SGU_BLOCK, SGU_BLOCK), F32), jax.ShapeDtypeStruct((4, SGU_BLOCK), F32),
                   jax.ShapeDtypeStruct((LANE_BLOCKS, CONV_K, LANE), F32), vec_shape, vec_shape, vec_shape],
        scratch_shapes=[pltpu.VMEM((LANE_BLOCKS, h + te, LANE), F32), pltpu.VMEM((LANE_BLOCKS, te, LANE), F32),
                        pltpu.VMEM((LANE_BLOCKS, te, LANE), F32), pltpu.VMEM((LANE_BLOCKS, ts, LANE), F32),
                        pltpu.VMEM((SGU_BLOCK, D), F32),
                        pltpu.VMEM((LANE_BLOCKS, SUBLANES - 1, h + te, LANE), F32)],
        compiler_params=_params(("arbitrary",), 60),
    )(z, z, z, dy, dy, sln_g, sln_b, ws, wst, sbias, dconv_w, dconv_b, dn_g, dn_b)


def _remote(src, dst, send_sems, recv_sems, k, to):
    return pltpu.make_async_remote_copy(src_ref=src, dst_ref=dst, send_sem=send_sems.at[k],
                                        recv_sem=recv_sems.at[k], device_id=to, device_id_type=MESH)


def _other_chips(x, y):
    return [(1 - x, y, 2 * (1 - x) + y), (x, 1 - y, 2 * x + 1 - y), (1 - x, 1 - y, 2 * (1 - x) + 1 - y)]


def _cast_own(w_stack, pos_arr, name):
    slots, rows, cols = w_stack.shape
    half = rows // 2

    def body(pos_ref, w_ref, o_ref):
        o_ref[...] = w_ref[...].astype(BF16)

    grid_spec = pltpu.PrefetchScalarGridSpec(
        num_scalar_prefetch=1, grid=(slots, 2),
        in_specs=[pl.BlockSpec((None, half, cols), lambda s, h, pos: (s, h, 0))],
        out_specs=pl.BlockSpec((None, None, None, half, cols), lambda s, h, pos: (pos[1], h, s, 0, 0)))
    return pl.pallas_call(
        body, name=name, grid_spec=grid_spec, out_shape=jax.ShapeDtypeStruct((NQ, 2, slots, half, cols), BF16),
        compiler_params=_params(("arbitrary",) * 2),
    )(pos_arr, w_stack)


def _gather_weights(win_g, wout_g, small_sh):
    def body(win_in, wout_in, small, win_g, wout_g, small_g, send_sems, recv_sems):
        del win_in, wout_in
        x, y, c = _mesh_pos()
        me = 2 * x + y
        sibling = (x, y, 1 - c)
        chips = _other_chips(x, y)

        sends = []
        for j, (cx, cy, _) in enumerate(chips):
            to = (cx, cy, c)
            sends.append(_remote(win_g.at[me, c], win_g.at[me, c], send_sems, recv_sems, j, to))
            sends.append(_remote(wout_g.at[me, c], wout_g.at[me, c], send_sems, recv_sems, 3 + j, to))
            sends.append(_remote(small, small_g.at[me], send_sems, recv_sems, 6 + j, to))
        for cp in sends:
            cp.start()
        passed = []
        for j, (_, _, q) in enumerate(chips):
            got_in = win_g.at[q, c]
            got_out = wout_g.at[q, c]
            _remote(got_in, got_in, send_sems, recv_sems, j, sibling).wait_recv()
            cp = _remote(got_in, got_in, send_sems, recv_sems, 9 + j, sibling)
            cp.start()
            passed.append(cp)
            _remote(got_out, got_out, send_sems, recv_sems, 3 + j, sibling).wait_recv()
            cp = _remote(got_out, got_out, send_sems, recv_sems, 12 + j, sibling)
            cp.start()
            passed.append(cp)
            _remote(small, small_g.at[q], send_sems, recv_sems, 6 + j, sibling).wait_recv()
        for j, (_, _, q) in enumerate(chips):
            from_in = win_g.at[q, 1 - c]
            from_out = wout_g.at[q, 1 - c]
            _remote(from_in, from_in, send_sems, recv_sems, 9 + j, sibling).wait_recv()
            _remote(from_out, from_out, send_sems, recv_sems, 12 + j, sibling).wait_recv()
        for cp in sends + passed:
            cp.wait_send()

    return pl.pallas_call(
        body, name="gather_weights",
        in_specs=[HBM_SPEC, HBM_SPEC, HBM_SPEC], out_specs=[HBM_SPEC, HBM_SPEC, HBM_SPEC],
        out_shape=[jax.ShapeDtypeStruct(win_g.shape, win_g.dtype), jax.ShapeDtypeStruct(wout_g.shape, wout_g.dtype),
                   jax.ShapeDtypeStruct((NQ,) + small_sh.shape, small_sh.dtype)],
        input_output_aliases={0: 0, 1: 1},
        scratch_shapes=[pltpu.SemaphoreType.DMA((15,)), pltpu.SemaphoreType.DMA((15,))],
    )(win_g, wout_g, small_sh)


def _hbm(a):
    return pltpu.with_memory_space_constraint(a, pltpu.HBM)


def _split_start(body, name, sources, landings):
    n_src, n_land = len(sources), len(landings)
    n_buf = n_src + n_land

    def kernel_body(*refs):
        ins, outs = refs[:n_buf], refs[n_buf:]
        send_sems, recv_sems, token = outs[0], outs[1], outs[2 + n_buf]
        body(ins[:n_src], ins[n_src:], send_sems, recv_sems)
        token[...] = jnp.zeros_like(token)

    bufs = [_hbm(a) for a in sources] + [
        _hbm(lax.empty(s.shape, s.dtype) if isinstance(s, jax.ShapeDtypeStruct) else s) for s in landings]
    n_sem = getattr(body, "n_copies")
    out = pl.pallas_call(
        kernel_body, name=name,
        out_shape=(pltpu.SemaphoreType.DMA((n_sem,)), pltpu.SemaphoreType.DMA((n_sem,)),
                   *[pltpu.HBM(b.shape, b.dtype) for b in bufs], jax.ShapeDtypeStruct((8, LANE), F32)),
        in_specs=(HBM_SPEC,) * n_buf,
        out_specs=(SEM_SPEC, SEM_SPEC, *([HBM_SPEC] * n_buf), pl.BlockSpec(memory_space=pltpu.VMEM)),
        input_output_aliases={k: 2 + k for k in range(n_buf)},
        compiler_params=pltpu.CompilerParams(has_side_effects=SIDE_EFFECT),
    )(*bufs)
    return out[0], out[1], list(out[2:2 + n_src]), list(out[2 + n_src:2 + n_buf]), out[2 + n_buf]


def _split_wait(body, name, send_sems, recv_sems, sources, landings, after):
    n_src, n_land = len(sources), len(landings)
    n_buf = n_src + n_land

    def kernel_body(*refs):
        ins = refs[:n_buf]
        body(ins[:n_src], ins[n_src:], refs[n_buf], refs[n_buf + 1])

    bufs = list(sources) + list(landings)
    out = pl.pallas_call(
        kernel_body, name=name,
        out_shape=tuple(pltpu.HBM(b.shape, b.dtype) for b in bufs),
        in_specs=(*([HBM_SPEC] * n_buf), SEM_SPEC, SEM_SPEC, pl.BlockSpec(memory_space=pl.ANY)),
        out_specs=(HBM_SPEC,) * n_buf,
        input_output_aliases={k: k for k in range(n_buf)},
        compiler_params=pltpu.CompilerParams(has_side_effects=SIDE_EFFECT),
    )(*bufs, send_sems, recv_sems, after)
    return list(out[:n_src]), list(out[n_src:])


def _gather_rest_copies(start):
    def body(srcs, lands, send_sems, recv_sems):
        del srcs
        x, y, c = _mesh_pos()
        me = 2 * x + y
        for j, (cx, cy, q) in enumerate(_other_chips(x, y)):
            to = (cx, cy, c)
            for k, gathered in enumerate(lands):
                if start:
                    _remote(gathered.at[me, c], gathered.at[me, c], send_sems, recv_sems, 3 * k + j, to).start()
                else:
                    cp = _remote(gathered.at[me, c], gathered.at[q, c], send_sems, recv_sems, 3 * k + j, to)
                    cp.wait_send()
                    cp.wait_recv()

    body.n_copies = 6
    return body


def _gather_rest_forward(win_g, wout_g):
    def body(win_in, wout_in, win_g, wout_g, send_sems, recv_sems):
        del win_in, wout_in
        x, y, c = _mesh_pos()
        sibling = (x, y, 1 - c)
        passed = []
        for j, (_, _, q) in enumerate(_other_chips(x, y)):
            got_in = win_g.at[q, c]
            got_out = wout_g.at[q, c]
            passed.append(_remote(got_in, got_in, send_sems, recv_sems, j, sibling))
            passed.append(_remote(got_out, got_out, send_sems, recv_sems, 3 + j, sibling))
        for cp in passed:
            cp.start()
        for j, (_, _, q) in enumerate(_other_chips(x, y)):
            from_in = win_g.at[q, 1 - c]
            from_out = wout_g.at[q, 1 - c]
            _remote(from_in, from_in, send_sems, recv_sems, j, sibling).wait_recv()
            _remote(from_out, from_out, send_sems, recv_sems, 3 + j, sibling).wait_recv()
        for cp in passed:
            cp.wait_send()

    return pl.pallas_call(
        body, name="gather_rest_forward",
        in_specs=[HBM_SPEC] * 2, out_specs=[HBM_SPEC] * 2,
        out_shape=[jax.ShapeDtypeStruct(win_g.shape, win_g.dtype), jax.ShapeDtypeStruct(wout_g.shape, wout_g.dtype)],
        input_output_aliases={0: 0, 1: 1},
        scratch_shapes=[pltpu.SemaphoreType.DMA((6,)), pltpu.SemaphoreType.DMA((6,))],
    )(win_g, wout_g)


def _allreduce_small(groups, after):
    pieces = [p for _, _, _, members in groups for _, p in members]
    n_in, n_g = len(pieces), len(groups)

    def body(*refs):
        ins = refs[:n_in]
        outs = refs[-(2 * n_g + 2):-(n_g + 2)]
        alls = refs[-(n_g + 2):-2]
        send_sems, recv_sems = refs[-2:]
        x, y, c = _mesh_pos()
        me = 4 * x + 2 * y + c
        sibling = (x, y, 1 - c)
        chips = _other_chips(x, y)

        k = 0
        for (rows, cols, dtype, members), all_ref in zip(groups, alls):
            if any(piece.shape[1] < cols for _, piece in members) or sum(p.shape[0] for _, p in members) < rows:
                all_ref[me] = jnp.zeros((rows, cols), dtype)
            for first, piece in members:
                n, width = piece.shape
                all_ref[me, first:first + n, 0:width] = ins[k][...].astype(dtype)
                k += 1

        sends, passed = [], []
        for g, all_ref in enumerate(alls):
            sends.append(_remote(all_ref.at[me], all_ref.at[me], send_sems, recv_sems, 7 * g, sibling))
            for j, (cx, cy, _) in enumerate(chips):
                sends.append(_remote(all_ref.at[me], all_ref.at[me], send_sems, recv_sems, 7 * g + 1 + j, (cx, cy, c)))
        for cp in sends:
            cp.start()
        for j, (cx, cy, _) in enumerate(chips):
            for g, all_ref in enumerate(alls):
                got = all_ref.at[4 * cx + 2 * cy + c]
                _remote(got, got, send_sems, recv_sems, 7 * g + 1 + j, sibling).wait_recv()
                cp = _remote(got, got, send_sems, recv_sems, 7 * g + 4 + j, sibling)
                cp.start()
                passed.append(cp)
        for g, all_ref in enumerate(alls):
            got = all_ref.at[4 * x + 2 * y + 1 - c]
            _remote(got, got, send_sems, recv_sems, 7 * g, sibling).wait_recv()
            for j, (cx, cy, _) in enumerate(chips):
                got = all_ref.at[4 * cx + 2 * cy + 1 - c]
                _remote(got, got, send_sems, recv_sems, 7 * g + 4 + j, sibling).wait_recv()
        for cp in sends + passed:
            cp.wait_send()
        for o_ref, all_ref in zip(outs, alls):
            total = all_ref[0].astype(F32)
            for dev in range(1, 8):
                total = total + all_ref[dev].astype(F32)
            o_ref[...] = total

    vmem = pl.BlockSpec(memory_space=pltpu.VMEM)
    return pl.pallas_call(
        body, name="allreduce_small",
        in_specs=[vmem] * n_in + _after_spec(after),
        out_specs=[vmem] * n_g,
        out_shape=[jax.ShapeDtypeStruct((rows, cols), F32) for rows, cols, _, _ in groups],
        scratch_shapes=[pltpu.VMEM((8, rows, cols), dtype) for rows, cols, dtype, _ in groups]
        + [pltpu.SemaphoreType.DMA((7 * n_g,)), pltpu.SemaphoreType.DMA((7 * n_g,))],
        compiler_params=_params(None, 56),
    )(*pieces, *_after_args(after))


def _pair_copies(start):
    def body(srcs, lands, send_sems, recv_sems):
        x, y, c = _mesh_pos()
        sibling = (x, y, 1 - c)
        for k, (g_ref, r_ref) in enumerate(zip(srcs, lands)):
            half = g_ref.shape[1] // 2
            cp = _remote(g_ref.at[:, pl.ds((1 - c) * half, half), :], r_ref, send_sems, recv_sems, k, sibling)
            if start:
                cp.start()
            else:
                cp.wait_send()
                cp.wait_recv()

    body.n_copies = 2
    return body


def _pair_sum(g, r, pos_arr, name):
    nq, rows, cols = r.shape
    tr = min(rows, 256)
    nt = rows // tr

    def body(pos_ref, g_ref, r_ref, ob_ref, own_ref):
        total = g_ref[...] + r_ref[...]
        ob_ref[...] = total.astype(BF16)

        @pl.when(pl.program_id(1) == pos_ref[1])
        def _():
            own_ref[...] = total

    blk = (None, tr, cols)
    grid_spec = pltpu.PrefetchScalarGridSpec(
        num_scalar_prefetch=1, grid=(nt, nq),
        in_specs=[pl.BlockSpec(blk, lambda t, q, pos: (q, pos[0] * nt + t, 0)),
                  pl.BlockSpec(blk, lambda t, q, pos: (q, t, 0))],
        out_specs=[pl.BlockSpec(blk, lambda t, q, pos: (q, t, 0)),
                   pl.BlockSpec((tr, cols), lambda t, q, pos: (t, 0))])
    return pl.pallas_call(
        body, name=name, grid_spec=grid_spec,
        out_shape=[jax.ShapeDtypeStruct(r.shape, BF16), jax.ShapeDtypeStruct((rows, cols), F32)],
        compiler_params=_params(("arbitrary",) * 2),
    )(pos_arr, g, r)


def _chip_copies(start):
    def body(srcs, lands, send_sems, recv_sems):
        pin, pout = srcs
        rin, rout = lands
        x, y, c = _mesh_pos()
        for j, (cx, cy, q) in enumerate(_other_chips(x, y)):
            to = (cx, cy, c)
            for k, (src, dst) in enumerate(((pin, rin), (pout, rout))):
                cp = _remote(src.at[q], dst.at[j], send_sems, recv_sems, 3 * k + j, to)
                if start:
                    cp.start()
                else:
                    cp.wait_send()
                    cp.wait_recv()

    body.n_copies = 6
    return body


def _chip_sum(own, r, pos_arr, name):
    rows, cols = own.shape
    tr = min(rows, 256)

    def body(pos_ref, p_ref, r0_ref, r1_ref, r2_ref, o_ref):
        o_ref[...] = ((p_ref[...] + r0_ref[...].astype(F32)) + r1_ref[...].astype(F32)) + r2_ref[...].astype(F32)

    def peer(j):
        return pl.BlockSpec((None, tr, cols), lambda t, pos: (j, t, 0))

    grid_spec = pltpu.PrefetchScalarGridSpec(
        num_scalar_prefetch=1, grid=(rows // tr,),
        in_specs=[pl.BlockSpec((tr, cols), lambda t, pos: (t, 0)), peer(0), peer(1), peer(2)],
        out_specs=pl.BlockSpec((None, tr, cols), lambda t, pos: (pos[0], t, 0)))
    return pl.pallas_call(
        body, name=name, grid_spec=grid_spec, out_shape=jax.ShapeDtypeStruct((2, rows, cols), F32),
        compiler_params=_params(("arbitrary",)),
    )(pos_arr, own, r, r, r)


def _pair_share(gin, gout, l):
    def body(gin_in, gout_in, gin_ref, gout_ref, send_sems, recv_sems):
        del gin_in, gout_in
        x, y, c = _mesh_pos()
        sibling = (x, y, 1 - c)
        sends = [_remote(gin_ref.at[c], gin_ref.at[c], send_sems, recv_sems, 0, sibling),
                 _remote(gout_ref.at[c], gout_ref.at[c], send_sems, recv_sems, 1, sibling)]
        for cp in sends:
            cp.start()
        _remote(gin_ref.at[1 - c], gin_ref.at[1 - c], send_sems, recv_sems, 0, sibling).wait_recv()
        _remote(gout_ref.at[1 - c], gout_ref.at[1 - c], send_sems, recv_sems, 1, sibling).wait_recv()
        for cp in sends:
            cp.wait_send()

    return pl.pallas_call(
        body, name=f"pair_share_{l}",
        in_specs=[HBM_SPEC, HBM_SPEC], out_specs=[HBM_SPEC, HBM_SPEC],
        out_shape=[jax.ShapeDtypeStruct(gin.shape, F32), jax.ShapeDtypeStruct(gout.shape, F32)],
        input_output_aliases={0: 0, 1: 1},
        scratch_shapes=[pltpu.SemaphoreType.DMA((2,)), pltpu.SemaphoreType.DMA((2,))],
    )(gin, gout)


def _adamw_large(w, m, v, g, i, prev, name):
    _, rows, cols = w.shape
    half = rows // 2
    tr = min(half, 256)
    nt = half // tr

    def body(w_ref, m_ref, v_ref, g_ref, *rest):
        go_ref, d_ref, mo_ref, vo_ref = rest[-4:]
        gv = g_ref[...]
        go_ref[...] = gv
        d_ref[...], mo_ref[...], vo_ref[...] = _adamw_math(w_ref[...], gv, m_ref[...], v_ref[...])

    full = pl.BlockSpec((None, tr, cols), lambda h, t: (i, h * nt + t, 0))
    out = jax.ShapeDtypeStruct(w.shape, F32)
    carried = [] if prev is None else list(prev)
    return pl.pallas_call(
        body, name=name, grid=(2, nt),
        in_specs=[full, full, full, pl.BlockSpec((None, tr, cols), lambda h, t: (h, t, 0))]
        + [pl.BlockSpec(memory_space=pl.ANY)] * len(carried),
        out_specs=[full] * 4, out_shape=[out] * 4,
        input_output_aliases={4 + k: k for k in range(len(carried))},
        compiler_params=_params(("arbitrary",) * 2),
    )(w, m, v, g, *carried)


def _adamw(w, g, m, v, name):
    shape = w.shape
    w2, g2, m2, v2 = (t.reshape(-1, shape[-1]) for t in (w, g, m, v))
    rows, cols = w2.shape
    tr = 256 if rows % 256 == 0 else rows

    def body(w_ref, g_ref, m_ref, v_ref, d_ref, mo_ref, vo_ref):
        d_ref[...], mo_ref[...], vo_ref[...] = _adamw_math(w_ref[...], g_ref[...], m_ref[...], v_ref[...])

    blk = pl.BlockSpec((tr, cols), lambda i: (i, 0))
    out = jax.ShapeDtypeStruct((rows, cols), F32)
    d, mo, vo = pl.pallas_call(
        body, name=name, grid=(rows // tr,), in_specs=[blk] * 4, out_specs=[blk] * 3, out_shape=[out] * 3,
        compiler_params=_params(("arbitrary",)),
    )(w2, g2, m2, v2)
    return d.reshape(shape), mo.reshape(shape), vo.reshape(shape)


def _layer_slot(l):
    return (l % 2) * 2 + l // 2


def kernel(x, ln_g, ln_b, w_in_even, w_out_even, pool_w, pool_scale, sconv_w, sconv_b, w_in_odd, w_out_odd, sgu_ln_g, sgu_ln_b, sgu_w, sgu_b, dconv_w, dconv_b, dnorm_g, dnorm_b, loss_target, m_ln_g, m_ln_b, m_w_in_even, m_w_out_even, m_pool_w, m_pool_scale, m_sconv_w, m_sconv_b, m_w_in_odd, m_w_out_odd, m_sgu_ln_g, m_sgu_ln_b, m_sgu_w, m_sgu_b, m_dconv_w, m_dconv_b, m_dnorm_g, m_dnorm_b, v_ln_g, v_ln_b, v_w_in_even, v_w_out_even, v_pool_w, v_pool_scale, v_sconv_w, v_sconv_b, v_w_in_odd, v_w_out_odd, v_sgu_ln_g, v_sgu_ln_b, v_sgu_w, v_sgu_b, v_dconv_w, v_dconv_b, v_dnorm_g, v_dnorm_b):
    weights = dict(ln_g=ln_g, ln_b=ln_b, w_in_even=w_in_even, w_out_even=w_out_even, pool_w=pool_w,
                   pool_scale=pool_scale, sconv_w=sconv_w, sconv_b=sconv_b, w_in_odd=w_in_odd, w_out_odd=w_out_odd,
                   sgu_ln_g=sgu_ln_g, sgu_ln_b=sgu_ln_b, sgu_w=sgu_w, sgu_b=sgu_b, dconv_w=dconv_w,
                   dconv_b=dconv_b, dnorm_g=dnorm_g, dnorm_b=dnorm_b)
    moments_m = dict(ln_g=m_ln_g, ln_b=m_ln_b, w_in_even=m_w_in_even, w_out_even=m_w_out_even, pool_w=m_pool_w,
                     pool_scale=m_pool_scale, sconv_w=m_sconv_w, sconv_b=m_sconv_b, w_in_odd=m_w_in_odd,
                     w_out_odd=m_w_out_odd, sgu_ln_g=m_sgu_ln_g, sgu_ln_b=m_sgu_ln_b, sgu_w=m_sgu_w, sgu_b=m_sgu_b,
                     dconv_w=m_dconv_w, dconv_b=m_dconv_b, dnorm_g=m_dnorm_g, dnorm_b=m_dnorm_b)
    moments_v = dict(ln_g=v_ln_g, ln_b=v_ln_b, w_in_even=v_w_in_even, w_out_even=v_w_out_even, pool_w=v_pool_w,
                     pool_scale=v_pool_scale, sconv_w=v_sconv_w, sconv_b=v_sconv_b, w_in_odd=v_w_in_odd,
                     w_out_odd=v_w_out_odd, sgu_ln_g=v_sgu_ln_g, sgu_ln_b=v_sgu_ln_b, sgu_w=v_sgu_w, sgu_b=v_sgu_b,
                     dconv_w=v_dconv_w, dconv_b=v_dconv_b, dnorm_g=v_dnorm_g, dnorm_b=v_dnorm_b)
    names = list(weights)

    xd, yd, cd = _mesh_pos()
    chip = 2 * xd + yd
    pos_arr = jnp.stack([cd, chip]).astype(jnp.int32)

    small_sh = jnp.concatenate(
        [sconv_w.reshape(6, HEAD), sgu_ln_g, sgu_ln_b, dconv_b, dnorm_g, dnorm_b, dconv_w.reshape(62, HEAD),
         jnp.zeros((2, HEAD), F32), pool_w.reshape(512, HEAD)], axis=0)
    win_first, wout_first, small_g = _gather_weights(
        _cast_own(w_in_even[0:1], pos_arr, "cast_win_first"), _cast_own(w_out_even[0:1], pos_arr, "cast_wout_first"),
        small_sh)
    small_g = lax.dynamic_update_slice(small_g, small_sh[None], (chip, 0, 0))
    later_in = jnp.concatenate([w_in_even[1:2], w_in_odd], axis=0)
    later_out = jnp.concatenate([w_out_even[1:2], w_out_odd], axis=0)
    g_send, g_recv, _, g_lands, g_token = _split_start(
        _gather_rest_copies(True), "gather_rest_start", [],
        [_cast_own(later_in, pos_arr, "cast_win_rest"), _cast_own(later_out, pos_arr, "cast_wout_rest")])

    def layer_weights(slot):
        return (win_first, wout_first, 0) if slot == 0 else (win_rest, wout_rest, slot - 1)

    def full_rows(lo, n):
        return jnp.transpose(small_g[:, lo:lo + n], (1, 0, 2)).reshape(n, D)

    sconv_w_f = full_rows(Q_SCONV_W, 6).reshape(2, SHORT_K, D)
    sln_g_f = full_rows(Q_SLN_G, 2)
    sln_b_f = full_rows(Q_SLN_B, 2)
    dconv_b_f = full_rows(Q_DCONV_B, 2)
    dn_g_f = full_rows(Q_DN_G, 2)
    dn_b_f = full_rows(Q_DN_B, 2)
    dconv_w_f = jnp.transpose(full_rows(Q_DCONV_W, 62).reshape(2, CONV_K, LANE_BLOCKS, LANE), (0, 2, 1, 3))
    pool_w_f = jnp.transpose(small_g[:, Q_POOL_W:].reshape(NQ, 2, 4, 64, HEAD), (1, 2, 0, 3, 4)).reshape(2, 4, HEAD, HEAD)
    pool_w_b = pool_w_f.astype(BF16)
    pool_wt_b = jnp.swapaxes(pool_w_f, 2, 3).astype(BF16)
    idx = jnp.arange(SGU_BLOCK)
    mask = (idx[None, :] // 64) <= (idx[:, None] // 64)
    ws_f = jnp.where(mask[None, None], sgu_w, 0.0)
    ws_b = ws_f.astype(BF16)
    wst_b = jnp.swapaxes(ws_f, 2, 3).astype(BF16)

    def row(a, i):
        return a[i:i + 1]

    x_f = x[0]
    x_b = x_f.astype(BF16)
    saved = []
    for l in range(NL):
        i, slot = l // 2, _layer_slot(l)
        if l == 1:
            _, g_lands = _split_wait(_gather_rest_copies(False), "gather_rest_wait", g_send, g_recv, [], g_lands, x_b)
            win_rest, wout_rest = _gather_rest_forward(g_lands[0], g_lands[1])
        win_g, wout_g, k = layer_weights(slot)
        z = _proj_in(x_b, win_g, k, l, g_token if l == 0 else None)
        if l % 2 == 0:
            ycat = _even_fwd(z, pool_w_b[i], row(pool_scale, i), sconv_w_f[i], row(sconv_b, i), l)
        else:
            ycat = _odd_fwd(z, row(sln_g_f, i), row(sln_b_f, i), ws_b[i], _sgu_bias_rows(sgu_b[i]),
                            dconv_w_f[i], row(dconv_b_f, i), row(dn_g_f, i), row(dn_b_f, i), l)
        x_next, x_next_b, xhat, rstd = _proj_out_ln(ycat, wout_g, k, l, x_f, row(ln_g, l), row(ln_b, l))
        saved.append((x_b, z, ycat, xhat, rstd))
        x_f, x_b = x_next, x_next_b

    loss_part, dxn = _loss_grad(x_f, loss_target[0])
    loss = lax.psum(loss_part[0, 0], ("x", "y", "c"))

    small = {}
    d_ln_g = [None] * NL
    d_ln_b = [None] * NL
    large = {"w_in_even": None, "w_out_even": None, "w_in_odd": None, "w_out_odd": None}
    pending = None
    token = None

    def finish(exchange, after):
        lx, send, recv, srcs, lands, own_in, own_out = exchange
        _, (r_in, r_out) = _split_wait(_chip_copies(False), f"chip_wait_{lx}", send, recv, srcs, lands, after)
        fin = _chip_sum(own_in, r_in, pos_arr, f"chip_sum_in_{lx}")
        fout = _chip_sum(own_out, r_out, pos_arr, f"chip_sum_out_{lx}")
        gs_in, gs_out = _pair_share(fin, fout, lx)
        kind = "even" if lx % 2 == 0 else "odd"
        for nm, gs in ((f"w_in_{kind}", gs_in), (f"w_out_{kind}", gs_out)):
            large[nm] = _adamw_large(weights[nm], moments_m[nm], moments_v[nm], gs, lx // 2, large[nm],
                                     f"adamw_{nm}_{lx // 2}")

    for l in reversed(range(NL)):
        i, slot = l // 2, _layer_slot(l)
        win_g, wout_g, k = layer_weights(slot)
        xin_b, z, ycat, xhat, rstd = saved[l]
        dr, dr_b, d_ln_g[l], d_ln_b[l] = _ln_bwd(dxn, xhat, rstd, row(ln_g, l), l, token)
        dy = _dycat(dr_b, wout_g, k, l)
        gout = _dwout(ycat, dr_b, l).reshape(NQ, RQ, D)
        if l % 2 == 0:
            dz, d_pw, d_ps, d_cw, d_cb = _even_bwd(z, dy, pool_w_b[i], pool_wt_b[i], row(pool_scale, i),
                                                   sconv_w_f[i], row(sconv_b, i), l)
            small[("pool_w", i)] = d_pw
            small[("pool_scale", i)] = d_ps
            small[("sconv_w", i)] = d_cw
            small[("sconv_b", i)] = d_cb
        else:
            dz, d_lg, d_lb, d_ws, d_sb, d_cw, d_cb, d_ng, d_nb = _odd_bwd(
                z, dy, row(sln_g_f, i), row(sln_b_f, i), ws_b[i], wst_b[i], _sgu_bias_rows(sgu_b[i]),
                dconv_w_f[i], row(dconv_b_f, i), row(dn_g_f, i), row(dn_b_f, i), l)
            small[("sgu_ln_g", i)] = d_lg
            small[("sgu_ln_b", i)] = d_lb
            small[("sgu_w", i)] = jnp.where(mask[None], d_ws, 0.0)
            small[("sgu_b", i)] = d_sb
            small[("dconv_w", i)] = jnp.transpose(d_cw, (1, 0, 2)).reshape(CONV_K, D)
            small[("dconv_b", i)] = d_cb
            small[("dnorm_g", i)] = d_ng
            small[("dnorm_b", i)] = d_nb
        gin = _dwin(xin_b, dz, l)
        p_send, p_recv, p_srcs, p_lands, p_token = _split_start(
            _pair_copies(True), f"pair_start_{l}", [gin, gout],
            [jax.ShapeDtypeStruct((NQ, D // 2, WQ), F32), jax.ShapeDtypeStruct((NQ, RQ // 2, D), F32)])
        dxn = _dx(dz, win_g, k, l, dr, p_token)
        if pending is not None:
            finish(pending, dxn)
        (gin, gout), (rin, rout) = _split_wait(_pair_copies(False), f"pair_wait_{l}", p_send, p_recv,
                                               p_srcs, p_lands, dxn)
        pin_b, pin_own = _pair_sum(gin, rin, pos_arr, f"pair_sum_in_{l}")
        pout_b, pout_own = _pair_sum(gout, rout, pos_arr, f"pair_sum_out_{l}")
        send, recv, srcs, lands, token = _split_start(
            _chip_copies(True), f"chip_start_{l}", [pin_b, pout_b],
            [jax.ShapeDtypeStruct((3,) + pin_b.shape[1:], BF16), jax.ShapeDtypeStruct((3,) + pout_b.shape[1:], BF16)])
        pending = (l, send, recv, srcs, lands, pin_own, pout_own)
    grad_x = dxn[None]

    def both(name, first, step):
        return [(first, small[(name, 0)]), (first + step, small[(name, 1)])]

    vectors = ([(R_LN_G + l, d_ln_g[l]) for l in range(NL)] + [(R_LN_B + l, d_ln_b[l]) for l in range(NL)]
               + both("pool_scale", R_PSCALE, 1) + both("sconv_b", R_SCONV_B, 1) + both("sconv_w", R_SCONV_W, SHORT_K)
               + both("sgu_ln_g", R_SLN_G, 1) + both("sgu_ln_b", R_SLN_B, 1) + both("dconv_b", R_DCONV_B, 1)
               + both("dnorm_g", R_DN_G, 1) + both("dnorm_b", R_DN_B, 1) + both("dconv_w", R_DCONV_W, CONV_K)
               + both("sgu_b", R_SGU_B, 4))
    sgu_w_rows = 4 * SGU_BLOCK
    pool_w_rows = 4 * HEAD
    total, total_sgu_w, total_pool_w = _allreduce_small(
        [(R_VECTORS, D, F32, vectors),
         (2 * sgu_w_rows, SGU_BLOCK, BF16,
          [(i * sgu_w_rows, small[("sgu_w", i)].reshape(sgu_w_rows, SGU_BLOCK)) for i in range(2)]),
         (2 * pool_w_rows, HEAD, BF16,
          [(i * pool_w_rows, small[("pool_w", i)].reshape(pool_w_rows, HEAD)) for i in range(2)])],
        token)
    finish(pending, total)

    def mine(a):
        return lax.dynamic_slice_in_dim(a, chip * HEAD, HEAD, axis=a.ndim - 1)

    grads = {
        "ln_g": total[R_LN_G:R_LN_G + 4],
        "ln_b": total[R_LN_B:R_LN_B + 4],
        "pool_scale": total[R_PSCALE:R_PSCALE + 2],
        "sconv_b": total[R_SCONV_B:R_SCONV_B + 2],
        "sconv_w": mine(total[R_SCONV_W:R_SCONV_W + 6].reshape(2, SHORT_K, D)),
        "sgu_ln_g": mine(total[R_SLN_G:R_SLN_G + 2]),
        "sgu_ln_b": mine(total[R_SLN_B:R_SLN_B + 2]),
        "dconv_b": mine(total[R_DCONV_B:R_DCONV_B + 2]),
        "dnorm_g": mine(total[R_DN_G:R_DN_G + 2]),
        "dnorm_b": mine(total[R_DN_B:R_DN_B + 2]),
        "dconv_w": mine(total[R_DCONV_W:R_DCONV_W + 62].reshape(2, CONV_K, D)),
        "sgu_b": total[R_SGU_B:R_SGU_B + 8, 0:SGU_BLOCK].reshape(2, 4, SGU_BLOCK),
        "sgu_w": total_sgu_w.reshape(2, 4, SGU_BLOCK, SGU_BLOCK),
        "pool_w": lax.dynamic_slice_in_dim(total_pool_w.reshape(2, 4, HEAD, HEAD), chip * 64, 64, axis=2),
    }

    deltas, new_m, new_v = {}, {}, {}
    for name in names:
        if name in large:
            grads[name], deltas[name], new_m[name], new_v[name] = large[name]
        else:
            deltas[name], new_m[name], new_v[name] = _adamw(
                weights[name], grads[name], moments_m[name], moments_v[name], f"adamw_{name}")

    return (loss, grad_x, *[grads[n] for n in names], *[deltas[n] for n in names],
            *[new_m[n] for n in names], *[new_v[n] for n in names])
```

```python
import jax
import jax.numpy as jnp
from jax import lax
from jax.experimental import pallas as pl
from jax.experimental.pallas import tpu as pltpu

F32 = jnp.float32
BF16 = jnp.bfloat16
MXU_DTYPE = BF16

D = 1024
DZ = 6144
DY = 2048
NQ = 4
WQ = DZ // NQ
RQ = DY // NQ
NL = 4
ALPHA = (2 * NL) ** 0.25
LN_EPS = 1e-5
CONV_K = 31
SHORT_K = 3
SGU_BLOCK = 128
HEAD = 256
POOL_HALO = 16
CONV_HALO = 32
LANE = 128
SUBLANES = 8
CONV_ROWS = 32
LANE_BLOCKS = 8
MIB = 1024 * 1024

ADAM_LR = 0.001
ADAM_B1 = 0.9
ADAM_B2 = 0.999
ADAM_EPS = 1e-08
ADAM_WD = 0.01
ADAM_STEP = 10

NN = ((1,), (0,))
NT = ((1,), (1,))
TN = ((0,), (0,))
MESH = pl.DeviceIdType.MESH
HBM_SPEC = pl.BlockSpec(memory_space=pltpu.HBM)
SEM_SPEC = pl.BlockSpec(memory_space=pltpu.SEMAPHORE)
SIDE_EFFECT = pltpu.SideEffectType.DATAFLOW_SIDE_EFFECTING

R_LN_G, R_LN_B, R_PSCALE, R_SCONV_B, R_SCONV_W = 0, 4, 8, 10, 12
R_SLN_G, R_SLN_B, R_DCONV_B, R_DN_G, R_DN_B, R_DCONV_W = 18, 20, 22, 24, 26, 28
R_SGU_B, R_VECTORS = 90, 104
Q_SCONV_W, Q_SLN_G, Q_SLN_B, Q_DCONV_B, Q_DN_G, Q_DN_B, Q_DCONV_W, Q_POOL_W, Q_ROWS = 0, 6, 8, 10, 12, 14, 16, 80, 592


def _dot(a, b, dims):
    return lax.dot_general(a.astype(MXU_DTYPE), b.astype(MXU_DTYPE), (dims, ((), ())),
                           preferred_element_type=F32)


def _params(semantics=None, vmem_mib=48):
    return pltpu.CompilerParams(dimension_semantics=semantics, vmem_limit_bytes=vmem_mib * MIB)


def _sigmoid(v):
    return 0.5 * jnp.tanh(0.5 * v) + 0.5


def _silu_and_grad(v):
    s = _sigmoid(v)
    return v * s, s * (1.0 + v * (1.0 - s))


def _ln_stats(v):
    mu = jnp.mean(v, axis=-1, keepdims=True)
    vc = v - mu
    var = jnp.mean(vc * vc, axis=-1, keepdims=True)
    rstd = lax.rsqrt(var + LN_EPS)
    return vc * rstd, rstd


def _ln_bwd_rows(dxhat, xhat, rstd):
    m1 = jnp.mean(dxhat, axis=-1, keepdims=True)
    m2 = jnp.mean(dxhat * xhat, axis=-1, keepdims=True)
    return rstd * (dxhat - m1 - xhat * m2)


def _colsum(v):
    return jnp.sum(v, axis=0, keepdims=True)


def _adamw_math(w, g, m, v):
    m_new = ADAM_B1 * m + (1.0 - ADAM_B1) * g
    v_new = ADAM_B2 * v + (1.0 - ADAM_B2) * (g * g)
    m_hat = m_new / (1.0 - ADAM_B1 ** ADAM_STEP)
    v_hat = v_new / (1.0 - ADAM_B2 ** ADAM_STEP)
    return -ADAM_LR * (m_hat / (jnp.sqrt(v_hat) + ADAM_EPS) + ADAM_WD * w), m_new, v_new


def _mesh_pos():
    return lax.axis_index("x"), lax.axis_index("y"), lax.axis_index("c")


def _after_spec(after):
    return [] if after is None else [pl.BlockSpec(memory_space=pl.ANY)]


def _after_args(after):
    return [] if after is None else [after]


def _proj_in(xb, win_g, k, l, after=None):
    s = xb.shape[0]
    tm = min(s, 1024)

    def body(x_ref, w_ref, *rest):
        rest[-1][...] = _dot(x_ref[...], w_ref[...].reshape(D, WQ), NN)

    return pl.pallas_call(
        body, name=f"proj_in_{l}", grid=(NQ, s // tm),
        in_specs=[pl.BlockSpec((tm, D), lambda q, m: (m, 0)),
                  pl.BlockSpec((None, 2, None, D // 2, WQ), lambda q, m: (q, 0, k, 0, 0))] + _after_spec(after),
        out_specs=pl.BlockSpec((tm, WQ), lambda q, m: (m, q)),
        out_shape=jax.ShapeDtypeStruct((s, DZ), F32),
        compiler_params=_params(("arbitrary", "arbitrary")),
    )(xb, win_g, *_after_args(after))


def _proj_out_ln(ycat, wout_g, k, l, x, g, b):
    s = x.shape[0]
    tm = min(s, 512)

    def body(y_ref, w_ref, x_ref, g_ref, b_ref, xn_ref, xb_ref, xh_ref, rs_ref):
        y = _dot(y_ref[...], w_ref[...].reshape(DY, D), NN)
        xhat, rstd = _ln_stats(ALPHA * x_ref[...] + y)
        xn = xhat * g_ref[...] + b_ref[...]
        xn_ref[...] = xn
        xb_ref[...] = xn.astype(BF16)
        xh_ref[...] = xhat
        rs_ref[...] = rstd

    row = lambda m: (m, 0)
    fixed = lambda m: (0, 0)
    return pl.pallas_call(
        body, name=f"proj_out_ln_{l}", grid=(s // tm,),
        in_specs=[pl.BlockSpec((tm, DY), row),
                  pl.BlockSpec((NQ, 2, None, RQ // 2, D), lambda m: (0, 0, k, 0, 0)),
                  pl.BlockSpec((tm, D), row), pl.BlockSpec((1, D), fixed), pl.BlockSpec((1, D), fixed)],
        out_specs=[pl.BlockSpec((tm, D), row), pl.BlockSpec((tm, D), row), pl.BlockSpec((tm, D), row),
                   pl.BlockSpec((tm, 1), row)],
        out_shape=[jax.ShapeDtypeStruct((s, D), F32), jax.ShapeDtypeStruct((s, D), BF16),
                   jax.ShapeDtypeStruct((s, D), F32), jax.ShapeDtypeStruct((s, 1), F32)],
        compiler_params=_params(("arbitrary",)),
    )(ycat, wout_g, x, g, b)


def _loss_grad(xl, target):
    s = xl.shape[0]
    ts = min(s, 512)

    def body(x_ref, t_ref, loss_ref, dx_ref):
        @pl.when(pl.program_id(0) == 0)
        def _():
            loss_ref[...] = jnp.zeros_like(loss_ref)
        err = x_ref[...] - t_ref[...]
        dx_ref[...] = err * (1.0 / D)
        loss_ref[...] += 0.5 * jnp.sum(jnp.mean(err * err, axis=-1, keepdims=True), axis=0, keepdims=True)

    row = lambda m: (m, 0)
    return pl.pallas_call(
        body, name="loss_grad", grid=(s // ts,),
        in_specs=[pl.BlockSpec((ts, D), row), pl.BlockSpec((ts, D), row)],
        out_specs=[pl.BlockSpec((1, 1), lambda m: (0, 0)), pl.BlockSpec((ts, D), row)],
        out_shape=[jax.ShapeDtypeStruct((1, 1), F32), jax.ShapeDtypeStruct((s, D), F32)],
        compiler_params=_params(("arbitrary",)),
    )(xl, target)


def _ln_bwd(dxn, xhat, rstd, g, l, after=None):
    s = dxn.shape[0]
    ts = min(s, 512)

    def body(d_ref, xh_ref, rs_ref, g_ref, *rest):
        dr_ref, drb_ref, dg_ref, db_ref = rest[-4:]

        @pl.when(pl.program_id(0) == 0)
        def _():
            dg_ref[...] = jnp.zeros_like(dg_ref)
            db_ref[...] = jnp.zeros_like(db_ref)
        d = d_ref[...]
        xhat_v = xh_ref[...]
        dr = _ln_bwd_rows(d * g_ref[...], xhat_v, rs_ref[...])
        dr_ref[...] = dr
        drb_ref[...] = dr.astype(BF16)
        dg_ref[...] += _colsum(d * xhat_v)
        db_ref[...] += _colsum(d)

    row = lambda m: (m, 0)
    fixed = lambda m: (0, 0)
    return pl.pallas_call(
        body, name=f"ln_bwd_{l}", grid=(s // ts,),
        in_specs=[pl.BlockSpec((ts, D), row), pl.BlockSpec((ts, D), row), pl.BlockSpec((ts, 1), row),
                  pl.BlockSpec((1, D), fixed)] + _after_spec(after),
        out_specs=[pl.BlockSpec((ts, D), row), pl.BlockSpec((ts, D), row), pl.BlockSpec((1, D), fixed),
                   pl.BlockSpec((1, D), fixed)],
        out_shape=[jax.ShapeDtypeStruct((s, D), F32), jax.ShapeDtypeStruct((s, D), BF16),
                   jax.ShapeDtypeStruct((1, D), F32), jax.ShapeDtypeStruct((1, D), F32)],
        compiler_params=_params(("arbitrary",)),
    )(dxn, xhat, rstd, g, *_after_args(after))


def _dycat(drb, wout_g, k, l):
    s = drb.shape[0]
    tm = min(s, 512)

    def body(d_ref, w_ref, o_ref):
        o_ref[...] = _dot(d_ref[...], w_ref[...].reshape(DY, D), NT)

    return pl.pallas_call(
        body, name=f"dycat_{l}", grid=(s // tm,),
        in_specs=[pl.BlockSpec((tm, D), lambda m: (m, 0)),
                  pl.BlockSpec((NQ, 2, None, RQ // 2, D), lambda m: (0, 0, k, 0, 0))],
        out_specs=pl.BlockSpec((tm, DY), lambda m: (m, 0)),
        out_shape=jax.ShapeDtypeStruct((s, DY), F32),
        compiler_params=_params(("arbitrary",)),
    )(drb, wout_g)


def _dwout(ycat, drb, l):
    s = drb.shape[0]
    tk = min(s, 1024)

    def body(y_ref, d_ref, o_ref):
        @pl.when(pl.program_id(0) == 0)
        def _():
            o_ref[...] = jnp.zeros_like(o_ref)

        o_ref[...] += _dot(y_ref[...], d_ref[...], TN)

    return pl.pallas_call(
        body, name=f"dwout_{l}", grid=(s // tk,),
        in_specs=[pl.BlockSpec((tk, DY), lambda k: (k, 0)), pl.BlockSpec((tk, D), lambda k: (k, 0))],
        out_specs=pl.BlockSpec((DY, D), lambda k: (0, 0)),
        out_shape=jax.ShapeDtypeStruct((DY, D), F32),
        compiler_params=_params(("arbitrary",)),
    )(ycat, drb)


def _dwin(xb, dzb, l):
    s = xb.shape[0]
    tk = min(s, 1024)

    def body(x_ref, d_ref, o_ref):
        @pl.when(pl.program_id(1) == 0)
        def _():
            o_ref[...] = jnp.zeros_like(o_ref)

        o_ref[...] += _dot(x_ref[...], d_ref[...], TN)

    return pl.pallas_call(
        body, name=f"dwin_{l}", grid=(NQ, s // tk),
        in_specs=[pl.BlockSpec((tk, D), lambda q, k: (k, 0)), pl.BlockSpec((tk, WQ), lambda q, k: (k, q))],
        out_specs=pl.BlockSpec((None, D, WQ), lambda q, k: (q, 0, 0)),
        out_shape=jax.ShapeDtypeStruct((NQ, D, WQ), F32),
        compiler_params=_params(("arbitrary", "arbitrary")),
    )(xb, dzb)


def _dx(dzb, win_g, k, l, dr, after=None):
    s = dzb.shape[0]
    tm = min(s, 1024)

    def body(d_ref, w_ref, r_ref, *rest):
        o_ref = rest[-1]

        @pl.when(pl.program_id(1) == 0)
        def _():
            o_ref[...] = ALPHA * r_ref[...]

        o_ref[...] += _dot(d_ref[...], w_ref[...].reshape(D, WQ), NT)

    return pl.pallas_call(
        body, name=f"dx_{l}", grid=(s // tm, NQ),
        in_specs=[pl.BlockSpec((tm, WQ), lambda m, q: (m, q)),
                  pl.BlockSpec((None, 2, None, D // 2, WQ), lambda m, q: (q, 0, k, 0, 0)),
                  pl.BlockSpec((tm, D), lambda m, q: (m, 0))] + _after_spec(after),
        out_specs=pl.BlockSpec((tm, D), lambda m, q: (m, 0)),
        out_shape=jax.ShapeDtypeStruct((s, D), F32),
        compiler_params=_params(("arbitrary", "arbitrary")),
    )(dzb, win_g, dr, *_after_args(after))


def _window_sums(e, causal):
    n = e.shape[0]

    def shifted(a, k):
        return pltpu.roll(a, k if causal else n - k, axis=0)

    parts = []
    acc = e
    for step, k in enumerate((1, 2, 4, 8)):
        acc = acc + shifted(acc, k)
        parts.append(acc[:, 0:HEAD])
        if step < 3:
            acc = acc[:, HEAD:]
    return jnp.concatenate(parts, axis=1)


def _pool_inv_counts(first_pos, rows):
    t1 = (lax.broadcasted_iota(jnp.int32, (rows, 1), 0) + first_pos + 1).astype(F32)
    lane = lax.broadcasted_iota(jnp.int32, (1, D), 1)
    win = jnp.where(lane < HEAD, 2.0, jnp.where(lane < 2 * HEAD, 4.0, jnp.where(lane < 3 * HEAD, 8.0, 16.0)))
    return jnp.where(t1 < win, 1.0 / t1, 1.0 / win)


def _group_dot(v, w_ref, dims):
    return jnp.concatenate(
        [_dot(v[:, g * HEAD:(g + 1) * HEAD], w_ref[g], dims) for g in range(4)], axis=1)


def _prev_index(ts, halo):
    return lambda i: (jnp.maximum(i * (ts // halo) - 1, 0), 0)


def _next_index(ts, halo, s):
    return lambda i: (jnp.minimum((i + 1) * (ts // halo), s // halo - 1), 0)


def _even_fwd(z, pool_w, pool_scale, sconv_w, sconv_b, l):
    s = z.shape[0]
    ts = min(s, 256)
    h = POOL_HALO

    def body(z_ref, zp_ref, pw_ref, ps_ref, cw_ref, cb_ref, o_ref):
        i = pl.program_id(0)
        inside = i > 0
        xa = z_ref[:, 0:D]
        xa_ext = jnp.concatenate([jnp.where(inside, zp_ref[:, 0:D], 0.0), xa], axis=0)
        sums = _window_sums(xa_ext, True)[h:]
        pooled = sums * _pool_inv_counts(i * ts, ts) - xa
        p = _group_dot(pooled, pw_ref, NN)
        silu_ga, _ = _silu_and_grad(z_ref[:, D:2 * D])
        o_ref[:, 0:D] = (p * ps_ref[...] * silu_ga).astype(BF16)

        q_main = z_ref[:, 4 * D:5 * D] * z_ref[:, 2 * D:3 * D]
        q_prev = jnp.where(inside, zp_ref[:, 4 * D:5 * D] * zp_ref[:, 2 * D:3 * D], 0.0)
        q_ext = jnp.concatenate([q_prev, q_main], axis=0)
        cv = cw_ref[2:3, :] * q_main + cb_ref[...]
        cv = cv + cw_ref[1:2, :] * pltpu.roll(q_ext, 1, axis=0)[h:]
        cv = cv + cw_ref[0:1, :] * pltpu.roll(q_ext, 2, axis=0)[h:]
        silu_gb, _ = _silu_and_grad(z_ref[:, 5 * D:6 * D])
        o_ref[:, D:2 * D] = (z_ref[:, 3 * D:4 * D] * cv * silu_gb).astype(BF16)

    fixed2 = lambda i: (0, 0)
    return pl.pallas_call(
        body, name=f"even_fwd_{l}", grid=(s // ts,),
        in_specs=[pl.BlockSpec((ts, DZ), lambda i: (i, 0)), pl.BlockSpec((h, DZ), _prev_index(ts, h)),
                  pl.BlockSpec((4, HEAD, HEAD), lambda i: (0, 0, 0)), pl.BlockSpec((1, D), fixed2),
                  pl.BlockSpec((SHORT_K, D), fixed2), pl.BlockSpec((1, D), fixed2)],
        out_specs=pl.BlockSpec((ts, DY), lambda i: (i, 0)),
        out_shape=jax.ShapeDtypeStruct((s, DY), BF16),
        compiler_params=_params(("arbitrary",)),
    )(z, z, pool_w, pool_scale, sconv_w, sconv_b)


def _even_bwd(z, dy, pool_w, pool_wt, pool_scale, sconv_w, sconv_b, l):
    s = z.shape[0]
    ts = min(s, 256)
    h = POOL_HALO
    n_tiles = s // ts

    def body(z_ref, zp_ref, zn_ref, dy_ref, dyn_ref, pw_ref, pwt_ref, ps_ref, cw_ref, cb_ref,
             dz_ref, dpw_ref, dps_ref, dcw_ref, dcb_ref):
        i = pl.program_id(0)
        inside = i > 0
        more = i < n_tiles - 1

        @pl.when(i == 0)
        def _():
            dpw_ref[...] = jnp.zeros_like(dpw_ref)
            dps_ref[...] = jnp.zeros_like(dps_ref)
            dcw_ref[...] = jnp.zeros_like(dcw_ref)
            dcb_ref[...] = jnp.zeros_like(dcb_ref)

        def with_next(main_ref, next_ref, lo):
            return jnp.concatenate([main_ref[:, lo:lo + D], next_ref[:, lo:lo + D]], axis=0)

        xa_ext = jnp.concatenate(
            [jnp.where(inside, zp_ref[:, 0:D], 0.0), z_ref[:, 0:D], zn_ref[:, 0:D]], axis=0)
        inv_counts = _pool_inv_counts(i * ts, ts + h)
        pooled = _window_sums(xa_ext, True)[h:] * inv_counts - xa_ext[h:]
        p = _group_dot(pooled, pw_ref, NN)
        silu_ga, dsilu_ga = _silu_and_grad(with_next(z_ref, zn_ref, D))
        d_ya = with_next(dy_ref, dyn_ref, 0)
        scale = ps_ref[...]
        d_p = d_ya * scale * silu_ga
        d_pooled = _group_dot(d_p, pwt_ref, NN)
        row = lax.broadcasted_iota(jnp.int32, (ts + h, 1), 0)
        d_pooled = jnp.where(jnp.logical_or(more, row < ts), d_pooled, 0.0)
        d_xa = _window_sums(d_pooled * inv_counts, False)[:ts] - d_pooled[:ts]
        dz_ref[:, 0:D] = d_xa.astype(BF16)
        dz_ref[:, D:2 * D] = (d_ya[:ts] * p[:ts] * scale * dsilu_ga[:ts]).astype(BF16)
        dps_ref[...] += _colsum(d_ya[:ts] * p[:ts] * silu_ga[:ts])
        for g in range(4):
            cols = slice(g * HEAD, (g + 1) * HEAD)
            dpw_ref[g] += _dot(pooled[:ts, cols], d_p[:ts, cols], TN)

        cg = z_ref[:, 4 * D:5 * D]
        hh = z_ref[:, 2 * D:3 * D]
        bg = z_ref[:, 3 * D:4 * D]
        q_main = cg * hh
        q_prev = jnp.where(inside, zp_ref[:, 4 * D:5 * D] * zp_ref[:, 2 * D:3 * D], 0.0)
        q_ext = jnp.concatenate([q_prev, q_main], axis=0)
        q_1 = pltpu.roll(q_ext, 1, axis=0)[h:]
        q_2 = pltpu.roll(q_ext, 2, axis=0)[h:]
        cv = cw_ref[2:3, :] * q_main + cw_ref[1:2, :] * q_1 + cw_ref[0:1, :] * q_2 + cb_ref[...]
        silu_gb, dsilu_gb = _silu_and_grad(with_next(z_ref, zn_ref, 5 * D))
        d_yb = with_next(dy_ref, dyn_ref, D)
        d_cv = d_yb * with_next(z_ref, zn_ref, 3 * D) * silu_gb
        d_cv = jnp.where(jnp.logical_or(more, row < ts), d_cv, 0.0)
        d_cv0 = d_cv[:ts]
        n_ext = ts + h
        d_q = (cw_ref[2:3, :] * d_cv0 + cw_ref[1:2, :] * pltpu.roll(d_cv, n_ext - 1, axis=0)[:ts]
               + cw_ref[0:1, :] * pltpu.roll(d_cv, n_ext - 2, axis=0)[:ts])
        dz_ref[:, 2 * D:3 * D] = (d_q * cg).astype(BF16)
        dz_ref[:, 3 * D:4 * D] = (d_yb[:ts] * cv * silu_gb[:ts]).astype(BF16)
        dz_ref[:, 4 * D:5 * D] = (d_q * hh).astype(BF16)
        dz_ref[:, 5 * D:6 * D] = (d_yb[:ts] * bg * cv * dsilu_gb[:ts]).astype(BF16)
        dcb_ref[...] += _colsum(d_cv0)
        dcw_ref[2:3, :] += _colsum(d_cv0 * q_main)
        dcw_ref[1:2, :] += _colsum(d_cv0 * q_1)
        dcw_ref[0:1, :] += _colsum(d_cv0 * q_2)

    fixed2 = lambda i: (0, 0)
    fixed3 = lambda i: (0, 0, 0)
    return pl.pallas_call(
        body, name=f"even_bwd_{l}", grid=(n_tiles,),
        in_specs=[pl.BlockSpec((ts, DZ), lambda i: (i, 0)), pl.BlockSpec((h, DZ), _prev_index(ts, h)),
                  pl.BlockSpec((h, DZ), _next_index(ts, h, s)),
                  pl.BlockSpec((ts, DY), lambda i: (i, 0)), pl.BlockSpec((h, DY), _next_index(ts, h, s)),
                  pl.BlockSpec((4, HEAD, HEAD), fixed3), pl.BlockSpec((4, HEAD, HEAD), fixed3),
                  pl.BlockSpec((1, D), fixed2), pl.BlockSpec((SHORT_K, D), fixed2), pl.BlockSpec((1, D), fixed2)],
        out_specs=[pl.BlockSpec((ts, DZ), lambda i: (i, 0)), pl.BlockSpec((4, HEAD, HEAD), fixed3),
                   pl.BlockSpec((1, D), fixed2), pl.BlockSpec((SHORT_K, D), fixed2), pl.BlockSpec((1, D), fixed2)],
        out_shape=[jax.ShapeDtypeStruct((s, DZ), BF16), jax.ShapeDtypeStruct((4, HEAD, HEAD), F32),
                   jax.ShapeDtypeStruct((1, D), F32), jax.ShapeDtypeStruct((SHORT_K, D), F32),
                   jax.ShapeDtypeStruct((1, D), F32)],
        compiler_params=_params(("arbitrary",), 56),
    )(z, z, z, dy, dy, pool_w, pool_wt, pool_scale, sconv_w, sconv_b)


def _to_blocks(ref, r0, val):
    n = val.shape[0]
    for cb in range(LANE_BLOCKS):
        ref[cb, r0:r0 + n, :] = val[:, cb * LANE:(cb + 1) * LANE]


def _from_blocks(ref):
    return jnp.concatenate([ref[cb] for cb in range(LANE_BLOCKS)], axis=1)


def _shift_copies(src_ref, sh_ref, n, causal):
    def block(cb, carry):
        for b in range(1, SUBLANES):
            if causal:
                sh_ref[cb, b - 1, SUBLANES:n, :] = src_ref[cb, SUBLANES - b:n - b, :]
            else:
                sh_ref[cb, b - 1, 0:n - SUBLANES, :] = src_ref[cb, b:n - SUBLANES + b, :]
        return carry

    lax.fori_loop(0, LANE_BLOCKS, block, 0)


def _tap(src_ref, sh_ref, cb, first, n, d, causal):
    whole, b = (d // SUBLANES) * SUBLANES, d % SUBLANES
    start = first - whole if causal else first + whole
    if b == 0:
        return src_ref[cb, start:start + n, :]
    return sh_ref[cb, b - 1, start:start + n, :]


def _chunk_rows(rows, most):
    return max(n for n in range(CONV_ROWS, most + 1, CONV_ROWS) if rows % n == 0)


def _conv31(src_ref, sh_ref, w_ref, dst_ref, base, rows, causal):
    n = _chunk_rows(rows, 4 * CONV_ROWS)

    def block(cb, carry):
        for r0 in range(0, rows, n):
            acc = None
            for d in range(CONV_K):
                term = w_ref[cb, CONV_K - 1 - d:CONV_K - d, :] * _tap(src_ref, sh_ref, cb, base + r0, n, d, causal)
                acc = term if acc is None else acc + term
            dst_ref[cb, r0:r0 + n, :] = acc
        return carry

    lax.fori_loop(0, LANE_BLOCKS, block, 0)


def _conv31_tap_grads(d_ref, src_ref, sh_ref, dw_ref, base, rows):
    n = _chunk_rows(rows, 2 * CONV_ROWS)

    def block(cb, carry):
        sums = [None] * CONV_K
        for r0 in range(0, rows, n):
            d_blk = d_ref[cb, r0:r0 + n, :]
            for d in range(CONV_K):
                prod = d_blk * _tap(src_ref, sh_ref, cb, base + r0, n, d, True)
                part = prod[0:SUBLANES]
                for k in range(1, n // SUBLANES):
                    part = part + prod[k * SUBLANES:(k + 1) * SUBLANES]
                sums[d] = part if sums[d] is None else sums[d] + part
        for d in range(CONV_K):
            j = CONV_K - 1 - d
            dw_ref[cb, j:j + 1, :] += _colsum(sums[d])
        return carry

    lax.fori_loop(0, LANE_BLOCKS, block, 0)


def _sgu_bias_rows(sgu_b):
    return jnp.repeat(jnp.transpose(sgu_b), HEAD, axis=1)


def _odd_fwd(z, sln_g, sln_b, ws, sbias, dconv_w, dconv_b, dn_g, dn_b, l):
    s = z.shape[0]
    ts = min(s, 256)
    h = CONV_HALO

    def body(z_ref, zp_ref, lg_ref, lb_ref, ws_ref, sb_ref, cw_ref, cb_ref, ng_ref, nb_ref, o_ref, zh_ref, rz_ref,
             zz_ref, zc_ref, sh_ref):
        i = pl.program_id(0)
        vhat, _ = _ln_stats(z_ref[:, D:2 * D])
        vn = (vhat * lg_ref[...] + lb_ref[...]).astype(MXU_DTYPE)
        silu_gc, _ = _silu_and_grad(z_ref[:, 2 * D:3 * D])
        for n in range(ts // SGU_BLOCK):
            rows = slice(n * SGU_BLOCK, (n + 1) * SGU_BLOCK)
            sv = jnp.concatenate(
                [_dot(ws_ref[hd], vn[rows, hd * HEAD:(hd + 1) * HEAD], NN) for hd in range(4)], axis=1)
            sv = sv + sb_ref[...]
            o_ref[rows, 0:D] = (z_ref[rows, 0:D] * sv * silu_gc[rows]).astype(BF16)

        _to_blocks(zz_ref, 0, jnp.where(i > 0, zp_ref[:, 3 * D:4 * D] * _sigmoid(zp_ref[:, 4 * D:5 * D]), 0.0))
        _to_blocks(zz_ref, h, z_ref[:, 3 * D:4 * D] * _sigmoid(z_ref[:, 4 * D:5 * D]))
        _shift_copies(zz_ref, sh_ref, h + ts, True)
        _conv31(zz_ref, sh_ref, cw_ref, zc_ref, h, ts, True)
        zhat, rstd_z = _ln_stats(_from_blocks(zc_ref) + cb_ref[...])
        zh_ref[...] = zhat
        rz_ref[...] = rstd_z
        silu_zn, _ = _silu_and_grad(zhat * ng_ref[...] + nb_ref[...])
        silu_gd, _ = _silu_and_grad(z_ref[:, 5 * D:6 * D])
        o_ref[:, D:2 * D] = (silu_zn * silu_gd).astype(BF16)

    fixed2 = lambda i: (0, 0)
    vec = pl.BlockSpec((1, D), fixed2)
    return pl.pallas_call(
        body, name=f"odd_fwd_{l}", grid=(s // ts,),
        in_specs=[pl.BlockSpec((ts, DZ), lambda i: (i, 0)), pl.BlockSpec((h, DZ), _prev_index(ts, h)),
                  vec, vec, pl.BlockSpec((4, SGU_BLOCK, SGU_BLOCK), lambda i: (0, 0, 0)),
                  pl.BlockSpec((SGU_BLOCK, D), fixed2), pl.BlockSpec((LANE_BLOCKS, CONV_K, LANE), lambda i: (0, 0, 0)),
                  vec, vec, vec],
        out_specs=[pl.BlockSpec((ts, DY), lambda i: (i, 0)), pl.BlockSpec((ts, D), lambda i: (i, 0)),
                   pl.BlockSpec((ts, 1), lambda i: (i, 0))],
        out_shape=[jax.ShapeDtypeStruct((s, DY), BF16), jax.ShapeDtypeStruct((s, D), F32),
                   jax.ShapeDtypeStruct((s, 1), F32)],
        scratch_shapes=[pltpu.VMEM((LANE_BLOCKS, h + ts, LANE), F32), pltpu.VMEM((LANE_BLOCKS, ts, LANE), F32),
                        pltpu.VMEM((LANE_BLOCKS, SUBLANES - 1, h + ts, LANE), F32)],
        compiler_params=_params(("arbitrary",)),
    )(z, z, sln_g, sln_b, ws, sbias, dconv_w, dconv_b, dn_g, dn_b)


def _odd_bwd(z, dy, zhat_s, rstd_s, sln_g, sln_b, ws, wst, sbias, dconv_w, dn_g, dn_b, l):
    s = z.shape[0]
    ts = min(s, 256)
    h = CONV_HALO
    n_tiles = s // ts
    te = ts + h

    def body(z_ref, zp_ref, zn_ref, dy_ref, dyn_ref, zh_ref, zhn_ref, rz_ref, rzn_ref, lg_ref, lb_ref, ws_ref,
             wst_ref, sb_ref, cw_ref, ng_ref, nb_ref, dz_ref, dlg_ref, dlb_ref, dws_ref, dsb_ref, dcw_ref, dcb_ref,
             dng_ref, dnb_ref, zz_ref, dzc_ref, dzz_ref, dsb_acc, sh_ref):
        i = pl.program_id(0)
        more = i < n_tiles - 1

        @pl.when(i == 0)
        def _():
            for ref in (dlg_ref, dlb_ref, dws_ref, dsb_ref, dcw_ref, dcb_ref, dng_ref, dnb_ref, dsb_acc):
                ref[...] = jnp.zeros_like(ref)

        vhat, rstd_v = _ln_stats(z_ref[:, D:2 * D])
        lg = lg_ref[...]
        vn = (vhat * lg + lb_ref[...]).astype(MXU_DTYPE)
        u = z_ref[:, 0:D]
        silu_gc, dsilu_gc = _silu_and_grad(z_ref[:, 2 * D:3 * D])
        d_yc = dy_ref[:, 0:D]
        d_yc_u = d_yc * u
        d_sv = d_yc_u * silu_gc
        d_svb = d_sv.astype(MXU_DTYPE)
        sv_rows = []
        dvn_rows = []
        dsb = None
        for n in range(ts // SGU_BLOCK):
            rows = slice(n * SGU_BLOCK, (n + 1) * SGU_BLOCK)
            sv_parts = []
            dvn_parts = []
            for hd in range(4):
                cols = slice(hd * HEAD, (hd + 1) * HEAD)
                sv_parts.append(_dot(ws_ref[hd], vn[rows, cols], NN))
                dvn_parts.append(_dot(wst_ref[hd], d_svb[rows, cols], NN))
                dws_ref[hd] += _dot(d_svb[rows, cols], vn[rows, cols], NT)
            sv_rows.append(jnp.concatenate(sv_parts, axis=1) + sb_ref[...])
            dvn_rows.append(jnp.concatenate(dvn_parts, axis=1))
            dsb = d_sv[rows] if dsb is None else dsb + d_sv[rows]
        dsb_acc[...] += dsb

        @pl.when(i == n_tiles - 1)
        def _():
            for hd in range(4):
                blk = dsb_acc[:, hd * HEAD:(hd + 1) * HEAD]
                folded = blk[:, 0:LANE] + blk[:, LANE:HEAD]
                dsb_ref[hd:hd + 1, :] = _colsum(jnp.transpose(folded))
        sv = jnp.concatenate(sv_rows, axis=0)
        d_vn = jnp.concatenate(dvn_rows, axis=0)
        dz_ref[:, 0:D] = (d_yc * sv * silu_gc).astype(BF16)
        dz_ref[:, D:2 * D] = _ln_bwd_rows(d_vn * lg, vhat, rstd_v).astype(BF16)
        dz_ref[:, 2 * D:3 * D] = (d_yc_u * sv * dsilu_gc).astype(BF16)
        dlg_ref[...] += _colsum(d_vn * vhat)
        dlb_ref[...] += _colsum(d_vn)

        def gate(ref):
            return ref[:, 3 * D:4 * D] * _sigmoid(ref[:, 4 * D:5 * D])

        _to_blocks(zz_ref, 0, jnp.where(i > 0, gate(zp_ref), 0.0))
        _to_blocks(zz_ref, h, gate(z_ref))
        _shift_copies(zz_ref, sh_ref, h + ts, True)
        zhat = jnp.concatenate([zh_ref[...], zhn_ref[...]], axis=0)
        rstd_z = jnp.concatenate([rz_ref[...], rzn_ref[...]], axis=0)
        ng = ng_ref[...]
        silu_zn, dsilu_zn = _silu_and_grad(zhat * ng + nb_ref[...])
        gd = jnp.concatenate([z_ref[:, 5 * D:6 * D], zn_ref[:, 5 * D:6 * D]], axis=0)
        silu_gd, dsilu_gd = _silu_and_grad(gd)
        d_yd = jnp.concatenate([dy_ref[:, D:2 * D], dyn_ref[:, D:2 * D]], axis=0)
        d_zn = d_yd * silu_gd * dsilu_zn
        d_zc = _ln_bwd_rows(d_zn * ng, zhat, rstd_z)
        row = lax.broadcasted_iota(jnp.int32, (te, 1), 0)
        d_zc = jnp.where(jnp.logical_or(more, row < ts), d_zc, 0.0)
        _to_blocks(dzc_ref, 0, d_zc)
        dz_ref[:, 5 * D:6 * D] = (d_yd[:ts] * silu_zn[:ts] * dsilu_gd[:ts]).astype(BF16)
        dng_ref[...] += _colsum(d_zn[:ts] * zhat[:ts])
        dnb_ref[...] += _colsum(d_zn[:ts])
        dcb_ref[...] += _colsum(d_zc[:ts])
        _conv31_tap_grads(dzc_ref, zz_ref, sh_ref, dcw_ref, h, ts)
        _shift_copies(dzc_ref, sh_ref, te, False)
        _conv31(dzc_ref, sh_ref, cw_ref, dzz_ref, 0, ts, False)
        d_zz = _from_blocks(dzz_ref)
        a = z_ref[:, 3 * D:4 * D]
        sig_b = _sigmoid(z_ref[:, 4 * D:5 * D])
        dz_ref[:, 3 * D:4 * D] = (d_zz * sig_b).astype(BF16)
        dz_ref[:, 4 * D:5 * D] = (d_zz * a * sig_b * (1.0 - sig_b)).astype(BF16)

    fixed2 = lambda i: (0, 0)
    fixed3 = lambda i: (0, 0, 0)
    vec = pl.BlockSpec((1, D), fixed2)
    mat = pl.BlockSpec((4, SGU_BLOCK, SGU_BLOCK), fixed3)
    vec_shape = jax.ShapeDtypeStruct((1, D), F32)
    return pl.pallas_call(
        body, name=f"odd_bwd_{l}", grid=(n_tiles,),
        in_specs=[pl.BlockSpec((ts, DZ), lambda i: (i, 0)), pl.BlockSpec((h, DZ), _prev_index(ts, h)),
                  pl.BlockSpec((h, DZ), _next_index(ts, h, s)),
                  pl.BlockSpec((ts, DY), lambda i: (i, 0)), pl.BlockSpec((h, DY), _next_index(ts, h, s)),
                  pl.BlockSpec((ts, D), lambda i: (i, 0)), pl.BlockSpec((h, D), _next_index(ts, h, s)),
                  pl.BlockSpec((ts, 1), lambda i: (i, 0)), pl.BlockSpec((h, 1), _next_index(ts, h, s)),
                  vec, vec, mat, mat, pl.BlockSpec((SGU_BLOCK, D), fixed2),
                  pl.BlockSpec((LANE_BLOCKS, CONV_K, LANE), fixed3), vec, vec],
        out_specs=[pl.BlockSpec((ts, DZ), lambda i: (i, 0)), vec, vec, mat, pl.BlockSpec((4, SGU_BLOCK), fixed2),
                   pl.BlockSpec((LANE_BLOCKS, CONV_K, LANE), fixed3), vec, vec, vec],
        out_shape=[jax.ShapeDtypeStruct((s, DZ), BF16), vec_shape, vec_shape,
                   jax.ShapeDtypeStruct((4, SGU_BLOCK, SGU_BLOCK), F32), jax.ShapeDtypeStruct((4, SGU_BLOCK), F32),
                   jax.ShapeDtypeStruct((LANE_BLOCKS, CONV_K, LANE), F32), vec_shape, vec_shape, vec_shape],
        scratch_shapes=[pltpu.VMEM((LANE_BLOCKS, h + ts, LANE), F32), pltpu.VMEM((LANE_BLOCKS, te, LANE), F32),
                        pltpu.VMEM((LANE_BLOCKS, ts, LANE), F32), pltpu.VMEM((SGU_BLOCK, D), F32),
                        pltpu.VMEM((LANE_BLOCKS, SUBLANES - 1, te, LANE), F32)],
        compiler_params=_params(("arbitrary",), 60),
    )(z, z, z, dy, dy, zhat_s, zhat_s, rstd_s, rstd_s, sln_g, sln_b, ws, wst, sbias, dconv_w, dn_g, dn_b)


def _remote(src, dst, send_sems, recv_sems, k, to):
    return pltpu.make_async_remote_copy(src_ref=src, dst_ref=dst, send_sem=send_sems.at[k],
                                        recv_sem=recv_sems.at[k], device_id=to, device_id_type=MESH)


def _other_chips(x, y):
    return [(1 - x, y, 2 * (1 - x) + y), (x, 1 - y, 2 * x + 1 - y), (1 - x, 1 - y, 2 * (1 - x) + 1 - y)]


def _cast_own(w_stack, pos_arr, name):
    slots, rows, cols = w_stack.shape
    half = rows // 2

    def body(pos_ref, w_ref, o_ref):
        o_ref[...] = w_ref[...].astype(BF16)

    grid_spec = pltpu.PrefetchScalarGridSpec(
        num_scalar_prefetch=1, grid=(slots, 2),
        in_specs=[pl.BlockSpec((None, half, cols), lambda s, h, pos: (s, h, 0))],
        out_specs=pl.BlockSpec((None, None, None, half, cols), lambda s, h, pos: (pos[1], h, s, 0, 0)))
    return pl.pallas_call(
        body, name=name, grid_spec=grid_spec, out_shape=jax.ShapeDtypeStruct((NQ, 2, slots, half, cols), BF16),
        compiler_params=_params(("arbitrary",) * 2),
    )(pos_arr, w_stack)


def _gather_weights(win_g, wout_g, small_sh):
    def body(win_in, wout_in, small, win_g, wout_g, small_g, send_sems, recv_sems):
        del win_in, wout_in
        x, y, c = _mesh_pos()
        me = 2 * x + y
        sibling = (x, y, 1 - c)
        chips = _other_chips(x, y)

        sends = []
        for j, (cx, cy, _) in enumerate(chips):
            to = (cx, cy, c)
            sends.append(_remote(win_g.at[me, c], win_g.at[me, c], send_sems, recv_sems, j, to))
            sends.append(_remote(wout_g.at[me, c], wout_g.at[me, c], send_sems, recv_sems, 3 + j, to))
            sends.append(_remote(small, small_g.at[me], send_sems, recv_sems, 6 + j, to))
        for cp in sends:
            cp.start()
        passed = []
        for j, (_, _, q) in enumerate(chips):
            got_in = win_g.at[q, c]
            got_out = wout_g.at[q, c]
            _remote(got_in, got_in, send_sems, recv_sems, j, sibling).wait_recv()
            cp = _remote(got_in, got_in, send_sems, recv_sems, 9 + j, sibling)
            cp.start()
            passed.append(cp)
            _remote(got_out, got_out, send_sems, recv_sems, 3 + j, sibling).wait_recv()
            cp = _remote(got_out, got_out, send_sems, recv_sems, 12 + j, sibling)
            cp.start()
            passed.append(cp)
            _remote(small, small_g.at[q], send_sems, recv_sems, 6 + j, sibling).wait_recv()
        for j, (_, _, q) in enumerate(chips):
            from_in = win_g.at[q, 1 - c]
            from_out = wout_g.at[q, 1 - c]
            _remote(from_in, from_in, send_sems, recv_sems, 9 + j, sibling).wait_recv()
            _remote(from_out, from_out, send_sems, recv_sems, 12 + j, sibling).wait_recv()
        for cp in sends + passed:
            cp.wait_send()

    return pl.pallas_call(
        body, name="gather_weights",
        in_specs=[HBM_SPEC, HBM_SPEC, HBM_SPEC], out_specs=[HBM_SPEC, HBM_SPEC, HBM_SPEC],
        out_shape=[jax.ShapeDtypeStruct(win_g.shape, win_g.dtype), jax.ShapeDtypeStruct(wout_g.shape, wout_g.dtype),
                   jax.ShapeDtypeStruct((NQ,) + small_sh.shape, small_sh.dtype)],
        input_output_aliases={0: 0, 1: 1},
        scratch_shapes=[pltpu.SemaphoreType.DMA((15,)), pltpu.SemaphoreType.DMA((15,))],
    )(win_g, wout_g, small_sh)


def _hbm(a):
    return pltpu.with_memory_space_constraint(a, pltpu.HBM)


def _split_start(body, name, sources, landings):
    n_src, n_land = len(sources), len(landings)
    n_buf = n_src + n_land

    def kernel_body(*refs):
        ins, outs = refs[:n_buf], refs[n_buf:]
        send_sems, recv_sems, token = outs[0], outs[1], outs[2 + n_buf]
        body(ins[:n_src], ins[n_src:], send_sems, recv_sems)
        token[...] = jnp.zeros_like(token)

    bufs = [_hbm(a) for a in sources] + [
        _hbm(lax.empty(s.shape, s.dtype) if isinstance(s, jax.ShapeDtypeStruct) else s) for s in landings]
    n_sem = getattr(body, "n_copies")
    out = pl.pallas_call(
        kernel_body, name=name,
        out_shape=(pltpu.SemaphoreType.DMA((n_sem,)), pltpu.SemaphoreType.DMA((n_sem,)),
                   *[pltpu.HBM(b.shape, b.dtype) for b in bufs], jax.ShapeDtypeStruct((8, LANE), F32)),
        in_specs=(HBM_SPEC,) * n_buf,
        out_specs=(SEM_SPEC, SEM_SPEC, *([HBM_SPEC] * n_buf), pl.BlockSpec(memory_space=pltpu.VMEM)),
        input_output_aliases={k: 2 + k for k in range(n_buf)},
        compiler_params=pltpu.CompilerParams(has_side_effects=SIDE_EFFECT),
    )(*bufs)
    return out[0], out[1], list(out[2:2 + n_src]), list(out[2 + n_src:2 + n_buf]), out[2 + n_buf]


def _split_wait(body, name, send_sems, recv_sems, sources, landings, after):
    n_src, n_land = len(sources), len(landings)
    n_buf = n_src + n_land

    def kernel_body(*refs):
        ins = refs[:n_buf]
        body(ins[:n_src], ins[n_src:], refs[n_buf], refs[n_buf + 1])

    bufs = list(sources) + list(landings)
    out = pl.pallas_call(
        kernel_body, name=name,
        out_shape=tuple(pltpu.HBM(b.shape, b.dtype) for b in bufs),
        in_specs=(*([HBM_SPEC] * n_buf), SEM_SPEC, SEM_SPEC, pl.BlockSpec(memory_space=pl.ANY)),
        out_specs=(HBM_SPEC,) * n_buf,
        input_output_aliases={k: k for k in range(n_buf)},
        compiler_params=pltpu.CompilerParams(has_side_effects=SIDE_EFFECT),
    )(*bufs, send_sems, recv_sems, after)
    return list(out[:n_src]), list(out[n_src:])


def _gather_rest_copies(start):
    def body(srcs, lands, send_sems, recv_sems):
        del srcs
        x, y, c = _mesh_pos()
        me = 2 * x + y
        for j, (cx, cy, q) in enumerate(_other_chips(x, y)):
            to = (cx, cy, c)
            for k, gathered in enumerate(lands):
                if start:
                    _remote(gathered.at[me, c], gathered.at[me, c], send_sems, recv_sems, 3 * k + j, to).start()
                else:
                    cp = _remote(gathered.at[me, c], gathered.at[q, c], send_sems, recv_sems, 3 * k + j, to)
                    cp.wait_send()
                    cp.wait_recv()

    body.n_copies = 6
    return body


def _gather_rest_forward(win_g, wout_g):
    def body(win_in, wout_in, win_g, wout_g, send_sems, recv_sems):
        del win_in, wout_in
        x, y, c = _mesh_pos()
        sibling = (x, y, 1 - c)
        passed = []
        for j, (_, _, q) in enumerate(_other_chips(x, y)):
            got_in = win_g.at[q, c]
            got_out = wout_g.at[q, c]
            passed.append(_remote(got_in, got_in, send_sems, recv_sems, j, sibling))
            passed.append(_remote(got_out, got_out, send_sems, recv_sems, 3 + j, sibling))
        for cp in passed:
            cp.start()
        for j, (_, _, q) in enumerate(_other_chips(x, y)):
            from_in = win_g.at[q, 1 - c]
            from_out = wout_g.at[q, 1 - c]
            _remote(from_in, from_in, send_sems, recv_sems, j, sibling).wait_recv()
            _remote(from_out, from_out, send_sems, recv_sems, 3 + j, sibling).wait_recv()
        for cp in passed:
            cp.wait_send()

    return pl.pallas_call(
        body, name="gather_rest_forward",
        in_specs=[HBM_SPEC] * 2, out_specs=[HBM_SPEC] * 2,
        out_shape=[jax.ShapeDtypeStruct(win_g.shape, win_g.dtype), jax.ShapeDtypeStruct(wout_g.shape, wout_g.dtype)],
        input_output_aliases={0: 0, 1: 1},
        scratch_shapes=[pltpu.SemaphoreType.DMA((6,)), pltpu.SemaphoreType.DMA((6,))],
    )(win_g, wout_g)


def _allreduce_small(groups, after):
    pieces = [p for _, _, _, members in groups for _, p in members]
    n_in, n_g = len(pieces), len(groups)

    def body(*refs):
        ins = refs[:n_in]
        outs = refs[-(2 * n_g + 2):-(n_g + 2)]
        alls = refs[-(n_g + 2):-2]
        send_sems, recv_sems = refs[-2:]
        x, y, c = _mesh_pos()
        me = 4 * x + 2 * y + c
        sibling = (x, y, 1 - c)
        chips = _other_chips(x, y)

        k = 0
        for (rows, cols, dtype, members), all_ref in zip(groups, alls):
            if any(piece.shape[1] < cols for _, piece in members) or sum(p.shape[0] for _, p in members) < rows:
                all_ref[me] = jnp.zeros((rows, cols), dtype)
            for first, piece in members:
                n, width = piece.shape
                all_ref[me, first:first + n, 0:width] = ins[k][...].astype(dtype)
                k += 1

        sends, passed = [], []
        for g, all_ref in enumerate(alls):
            sends.append(_remote(all_ref.at[me], all_ref.at[me], send_sems, recv_sems, 7 * g, sibling))
            for j, (cx, cy, _) in enumerate(chips):
                sends.append(_remote(all_ref.at[me], all_ref.at[me], send_sems, recv_sems, 7 * g + 1 + j, (cx, cy, c)))
        for cp in sends:
            cp.start()
        for j, (cx, cy, _) in enumerate(chips):
            for g, all_ref in enumerate(alls):
                got = all_ref.at[4 * cx + 2 * cy + c]
                _remote(got, got, send_sems, recv_sems, 7 * g + 1 + j, sibling).wait_recv()
                cp = _remote(got, got, send_sems, recv_sems, 7 * g + 4 + j, sibling)
                cp.start()
                passed.append(cp)
        for g, all_ref in enumerate(alls):
            got = all_ref.at[4 * x + 2 * y + 1 - c]
            _remote(got, got, send_sems, recv_sems, 7 * g, sibling).wait_recv()
            for j, (cx, cy, _) in enumerate(chips):
                got = all_ref.at[4 * cx + 2 * cy + 1 - c]
                _remote(got, got, send_sems, recv_sems, 7 * g + 4 + j, sibling).wait_recv()
        for cp in sends + passed:
            cp.wait_send()
        for o_ref, all_ref in zip(outs, alls):
            total = all_ref[0].astype(F32)
            for dev in range(1, 8):
                total = total + all_ref[dev].astype(F32)
            o_ref[...] = total

    vmem = pl.BlockSpec(memory_space=pltpu.VMEM)
    return pl.pallas_call(
        body, name="allreduce_small",
        in_specs=[vmem] * n_in + _after_spec(after),
        out_specs=[vmem] * n_g,
        out_shape=[jax.ShapeDtypeStruct((rows, cols), F32) for rows, cols, _, _ in groups],
        scratch_shapes=[pltpu.VMEM((8, rows, cols), dtype) for rows, cols, dtype, _ in groups]
        + [pltpu.SemaphoreType.DMA((7 * n_g,)), pltpu.SemaphoreType.DMA((7 * n_g,))],
        compiler_params=_params(None, 56),
    )(*pieces, *_after_args(after))


def _pair_copies(start):
    def body(srcs, lands, send_sems, recv_sems):
        x, y, c = _mesh_pos()
        sibling = (x, y, 1 - c)
        for k, (g_ref, r_ref) in enumerate(zip(srcs, lands)):
            half = g_ref.shape[1] // 2
            cp = _remote(g_ref.at[:, pl.ds((1 - c) * half, half), :], r_ref, send_sems, recv_sems, k, sibling)
            if start:
                cp.start()
            else:
                cp.wait_send()
                cp.wait_recv()

    body.n_copies = 2
    return body


def _pair_sum(g, r, pos_arr, name):
    nq, rows, cols = r.shape
    tr = min(rows, 256)
    nt = rows // tr

    def body(pos_ref, g_ref, r_ref, ob_ref, own_ref):
        total = g_ref[...] + r_ref[...]
        ob_ref[...] = total.astype(BF16)

        @pl.when(pl.program_id(1) == pos_ref[1])
        def _():
            own_ref[...] = total

    blk = (None, tr, cols)
    grid_spec = pltpu.PrefetchScalarGridSpec(
        num_scalar_prefetch=1, grid=(nt, nq),
        in_specs=[pl.BlockSpec(blk, lambda t, q, pos: (q, pos[0] * nt + t, 0)),
                  pl.BlockSpec(blk, lambda t, q, pos: (q, t, 0))],
        out_specs=[pl.BlockSpec(blk, lambda t, q, pos: (q, t, 0)),
                   pl.BlockSpec((tr, cols), lambda t, q, pos: (t, 0))])
    return pl.pallas_call(
        body, name=name, grid_spec=grid_spec,
        out_shape=[jax.ShapeDtypeStruct(r.shape, BF16), jax.ShapeDtypeStruct((rows, cols), F32)],
        compiler_params=_params(("arbitrary",) * 2),
    )(pos_arr, g, r)


def _chip_copies(start):
    def body(srcs, lands, send_sems, recv_sems):
        pin, pout = srcs
        rin, rout = lands
        x, y, c = _mesh_pos()
        for j, (cx, cy, q) in enumerate(_other_chips(x, y)):
            to = (cx, cy, c)
            for k, (src, dst) in enumerate(((pin, rin), (pout, rout))):
                cp = _remote(src.at[q], dst.at[j], send_sems, recv_sems, 3 * k + j, to)
                if start:
                    cp.start()
                else:
                    cp.wait_send()
                    cp.wait_recv()

    body.n_copies = 6
    return body


def _chip_sum(own, r, pos_arr, name):
    rows, cols = own.shape
    tr = min(rows, 256)

    def body(pos_ref, p_ref, r0_ref, r1_ref, r2_ref, o_ref):
        o_ref[...] = ((p_ref[...] + r0_ref[...].astype(F32)) + r1_ref[...].astype(F32)) + r2_ref[...].astype(F32)

    def peer(j):
        return pl.BlockSpec((None, tr, cols), lambda t, pos: (j, t, 0))

    grid_spec = pltpu.PrefetchScalarGridSpec(
        num_scalar_prefetch=1, grid=(rows // tr,),
        in_specs=[pl.BlockSpec((tr, cols), lambda t, pos: (t, 0)), peer(0), peer(1), peer(2)],
        out_specs=pl.BlockSpec((None, tr, cols), lambda t, pos: (pos[0], t, 0)))
    return pl.pallas_call(
        body, name=name, grid_spec=grid_spec, out_shape=jax.ShapeDtypeStruct((2, rows, cols), F32),
        compiler_params=_params(("arbitrary",)),
    )(pos_arr, own, r, r, r)


def _pair_share(gin, gout, l):
    def body(gin_in, gout_in, gin_ref, gout_ref, send_sems, recv_sems):
        del gin_in, gout_in
        x, y, c = _mesh_pos()
        sibling = (x, y, 1 - c)
        sends = [_remote(gin_ref.at[c], gin_ref.at[c], send_sems, recv_sems, 0, sibling),
                 _remote(gout_ref.at[c], gout_ref.at[c], send_sems, recv_sems, 1, sibling)]
        for cp in sends:
            cp.start()
        _remote(gin_ref.at[1 - c], gin_ref.at[1 - c], send_sems, recv_sems, 0, sibling).wait_recv()
        _remote(gout_ref.at[1 - c], gout_ref.at[1 - c], send_sems, recv_sems, 1, sibling).wait_recv()
        for cp in sends:
            cp.wait_send()

    return pl.pallas_call(
        body, name=f"pair_share_{l}",
        in_specs=[HBM_SPEC, HBM_SPEC], out_specs=[HBM_SPEC, HBM_SPEC],
        out_shape=[jax.ShapeDtypeStruct(gin.shape, F32), jax.ShapeDtypeStruct(gout.shape, F32)],
        input_output_aliases={0: 0, 1: 1},
        scratch_shapes=[pltpu.SemaphoreType.DMA((2,)), pltpu.SemaphoreType.DMA((2,))],
    )(gin, gout)


def _adamw_large(w, m, v, g, i, prev, name):
    _, rows, cols = w.shape
    half = rows // 2
    tr = min(half, 256)
    nt = half // tr

    def body(w_ref, m_ref, v_ref, g_ref, *rest):
        go_ref, d_ref, mo_ref, vo_ref = rest[-4:]
        gv = g_ref[...]
        go_ref[...] = gv
        d_ref[...], mo_ref[...], vo_ref[...] = _adamw_math(w_ref[...], gv, m_ref[...], v_ref[...])

    full = pl.BlockSpec((None, tr, cols), lambda h, t: (i, h * nt + t, 0))
    out = jax.ShapeDtypeStruct(w.shape, F32)
    carried = [] if prev is None else list(prev)
    return pl.pallas_call(
        body, name=name, grid=(2, nt),
        in_specs=[full, full, full, pl.BlockSpec((None, tr, cols), lambda h, t: (h, t, 0))]
        + [pl.BlockSpec(memory_space=pl.ANY)] * len(carried),
        out_specs=[full] * 4, out_shape=[out] * 4,
        input_output_aliases={4 + k: k for k in range(len(carried))},
        compiler_params=_params(("arbitrary",) * 2),
    )(w, m, v, g, *carried)


def _adamw(w, g, m, v, name):
    shape = w.shape
    w2, g2, m2, v2 = (t.reshape(-1, shape[-1]) for t in (w, g, m, v))
    rows, cols = w2.shape
    tr = 256 if rows % 256 == 0 else rows

    def body(w_ref, g_ref, m_ref, v_ref, d_ref, mo_ref, vo_ref):
        d_ref[...], mo_ref[...], vo_ref[...] = _adamw_math(w_ref[...], g_ref[...], m_ref[...], v_ref[...])

    blk = pl.BlockSpec((tr, cols), lambda i: (i, 0))
    out = jax.ShapeDtypeStruct((rows, cols), F32)
    d, mo, vo = pl.pallas_call(
        body, name=name, grid=(rows // tr,), in_specs=[blk] * 4, out_specs=[blk] * 3, out_shape=[out] * 3,
        compiler_params=_params(("arbitrary",)),
    )(w2, g2, m2, v2)
    return d.reshape(shape), mo.reshape(shape), vo.reshape(shape)


def _layer_slot(l):
    return (l % 2) * 2 + l // 2


def kernel(x, ln_g, ln_b, w_in_even, w_out_even, pool_w, pool_scale, sconv_w, sconv_b, w_in_odd, w_out_odd, sgu_ln_g, sgu_ln_b, sgu_w, sgu_b, dconv_w, dconv_b, dnorm_g, dnorm_b, loss_target, m_ln_g, m_ln_b, m_w_in_even, m_w_out_even, m_pool_w, m_pool_scale, m_sconv_w, m_sconv_b, m_w_in_odd, m_w_out_odd, m_sgu_ln_g, m_sgu_ln_b, m_sgu_w, m_sgu_b, m_dconv_w, m_dconv_b, m_dnorm_g, m_dnorm_b, v_ln_g, v_ln_b, v_w_in_even, v_w_out_even, v_pool_w, v_pool_scale, v_sconv_w, v_sconv_b, v_w_in_odd, v_w_out_odd, v_sgu_ln_g, v_sgu_ln_b, v_sgu_w, v_sgu_b, v_dconv_w, v_dconv_b, v_dnorm_g, v_dnorm_b):
    weights = dict(ln_g=ln_g, ln_b=ln_b, w_in_even=w_in_even, w_out_even=w_out_even, pool_w=pool_w,
                   pool_scale=pool_scale, sconv_w=sconv_w, sconv_b=sconv_b, w_in_odd=w_in_odd, w_out_odd=w_out_odd,
                   sgu_ln_g=sgu_ln_g, sgu_ln_b=sgu_ln_b, sgu_w=sgu_w, sgu_b=sgu_b, dconv_w=dconv_w,
                   dconv_b=dconv_b, dnorm_g=dnorm_g, dnorm_b=dnorm_b)
    moments_m = dict(ln_g=m_ln_g, ln_b=m_ln_b, w_in_even=m_w_in_even, w_out_even=m_w_out_even, pool_w=m_pool_w,
                     pool_scale=m_pool_scale, sconv_w=m_sconv_w, sconv_b=m_sconv_b, w_in_odd=m_w_in_odd,
                     w_out_odd=m_w_out_odd, sgu_ln_g=m_sgu_ln_g, sgu_ln_b=m_sgu_ln_b, sgu_w=m_sgu_w, sgu_b=m_sgu_b,
                     dconv_w=m_dconv_w, dconv_b=m_dconv_b, dnorm_g=m_dnorm_g, dnorm_b=m_dnorm_b)
    moments_v = dict(ln_g=v_ln_g, ln_b=v_ln_b, w_in_even=v_w_in_even, w_out_even=v_w_out_even, pool_w=v_pool_w,
                     pool_scale=v_pool_scale, sconv_w=v_sconv_w, sconv_b=v_sconv_b, w_in_odd=v_w_in_odd,
                     w_out_odd=v_w_out_odd, sgu_ln_g=v_sgu_ln_g, sgu_ln_b=v_sgu_ln_b, sgu_w=v_sgu_w, sgu_b=v_sgu_b,
                     dconv_w=v_dconv_w, dconv_b=v_dconv_b, dnorm_g=v_dnorm_g, dnorm_b=v_dnorm_b)
    names = list(weights)

    xd, yd, cd = _mesh_pos()
    chip = 2 * xd + yd
    pos_arr = jnp.stack([cd, chip]).astype(jnp.int32)

    small_sh = jnp.concatenate(
        [sconv_w.reshape(6, HEAD), sgu_ln_g, sgu_ln_b, dconv_b, dnorm_g, dnorm_b, dconv_w.reshape(62, HEAD),
         jnp.zeros((2, HEAD), F32), pool_w.reshape(512, HEAD)], axis=0)
    win_first, wout_first, small_g = _gather_weights(
        _cast_own(w_in_even[0:1], pos_arr, "cast_win_first"), _cast_own(w_out_even[0:1], pos_arr, "cast_wout_first"),
        small_sh)
    small_g = lax.dynamic_update_slice(small_g, small_sh[None], (chip, 0, 0))
    later_in = jnp.concatenate([w_in_even[1:2], w_in_odd], axis=0)
    later_out = jnp.concatenate([w_out_even[1:2], w_out_odd], axis=0)
    g_send, g_recv, _, g_lands, g_token = _split_start(
        _gather_rest_copies(True), "gather_rest_start", [],
        [_cast_own(later_in, pos_arr, "cast_win_rest"), _cast_own(later_out, pos_arr, "cast_wout_rest")])

    def layer_weights(slot):
        return (win_first, wout_first, 0) if slot == 0 else (win_rest, wout_rest, slot - 1)

    def full_rows(lo, n):
        return jnp.transpose(small_g[:, lo:lo + n], (1, 0, 2)).reshape(n, D)

    sconv_w_f = full_rows(Q_SCONV_W, 6).reshape(2, SHORT_K, D)
    sln_g_f = full_rows(Q_SLN_G, 2)
    sln_b_f = full_rows(Q_SLN_B, 2)
    dconv_b_f = full_rows(Q_DCONV_B, 2)
    dn_g_f = full_rows(Q_DN_G, 2)
    dn_b_f = full_rows(Q_DN_B, 2)
    dconv_w_f = jnp.transpose(full_rows(Q_DCONV_W, 62).reshape(2, CONV_K, LANE_BLOCKS, LANE), (0, 2, 1, 3))
    pool_w_f = jnp.transpose(small_g[:, Q_POOL_W:].reshape(NQ, 2, 4, 64, HEAD), (1, 2, 0, 3, 4)).reshape(2, 4, HEAD, HEAD)
    pool_w_b = pool_w_f.astype(BF16)
    pool_wt_b = jnp.swapaxes(pool_w_f, 2, 3).astype(BF16)
    idx = jnp.arange(SGU_BLOCK)
    mask = (idx[None, :] // 64) <= (idx[:, None] // 64)
    ws_f = jnp.where(mask[None, None], sgu_w, 0.0)
    ws_b = ws_f.astype(BF16)
    wst_b = jnp.swapaxes(ws_f, 2, 3).astype(BF16)

    def row(a, i):
        return a[i:i + 1]

    x_f = x[0]
    x_b = x_f.astype(BF16)
    saved = []
    conv_saved = {}
    for l in range(NL):
        i, slot = l // 2, _layer_slot(l)
        if l == 1:
            _, g_lands = _split_wait(_gather_rest_copies(False), "gather_rest_wait", g_send, g_recv, [], g_lands, x_b)
            win_rest, wout_rest = _gather_rest_forward(g_lands[0], g_lands[1])
        win_g, wout_g, k = layer_weights(slot)
        z = _proj_in(x_b, win_g, k, l, g_token if l == 0 else None)
        if l % 2 == 0:
            ycat = _even_fwd(z, pool_w_b[i], row(pool_scale, i), sconv_w_f[i], row(sconv_b, i), l)
        else:
            ycat, conv_hat, conv_rstd = _odd_fwd(
                z, row(sln_g_f, i), row(sln_b_f, i), ws_b[i], _sgu_bias_rows(sgu_b[i]),
                dconv_w_f[i], row(dconv_b_f, i), row(dn_g_f, i), row(dn_b_f, i), l)
            conv_saved[l] = (conv_hat, conv_rstd)
        x_next, x_next_b, xhat, rstd = _proj_out_ln(ycat, wout_g, k, l, x_f, row(ln_g, l), row(ln_b, l))
        saved.append((x_b, z, ycat, xhat, rstd))
        x_f, x_b = x_next, x_next_b

    loss_part, dxn = _loss_grad(x_f, loss_target[0])
    loss = lax.psum(loss_part[0, 0], ("x", "y", "c"))

    small = {}
    d_ln_g = [None] * NL
    d_ln_b = [None] * NL
    large = {"w_in_even": None, "w_out_even": None, "w_in_odd": None, "w_out_odd": None}
    pending = None
    token = None

    def finish(exchange, after):
        lx, send, recv, srcs, lands, own_in, own_out = exchange
        _, (r_in, r_out) = _split_wait(_chip_copies(False), f"chip_wait_{lx}", send, recv, srcs, lands, after)
        fin = _chip_sum(own_in, r_in, pos_arr, f"chip_sum_in_{lx}")
        fout = _chip_sum(own_out, r_out, pos_arr, f"chip_sum_out_{lx}")
        gs_in, gs_out = _pair_share(fin, fout, lx)
        kind = "even" if lx % 2 == 0 else "odd"
        for nm, gs in ((f"w_in_{kind}", gs_in), (f"w_out_{kind}", gs_out)):
            large[nm] = _adamw_large(weights[nm], moments_m[nm], moments_v[nm], gs, lx // 2, large[nm],
                                     f"adamw_{nm}_{lx // 2}")

    for l in reversed(range(NL)):
        i, slot = l // 2, _layer_slot(l)
        win_g, wout_g, k = layer_weights(slot)
        xin_b, z, ycat, xhat, rstd = saved[l]
        dr, dr_b, d_ln_g[l], d_ln_b[l] = _ln_bwd(dxn, xhat, rstd, row(ln_g, l), l, token)
        dy = _dycat(dr_b, wout_g, k, l)
        gout = _dwout(ycat, dr_b, l).reshape(NQ, RQ, D)
        if l % 2 == 0:
            dz, d_pw, d_ps, d_cw, d_cb = _even_bwd(z, dy, pool_w_b[i], pool_wt_b[i], row(pool_scale, i),
                                                   sconv_w_f[i], row(sconv_b, i), l)
            small[("pool_w", i)] = d_pw
            small[("pool_scale", i)] = d_ps
            small[("sconv_w", i)] = d_cw
            small[("sconv_b", i)] = d_cb
        else:
            dz, d_lg, d_lb, d_ws, d_sb, d_cw, d_cb, d_ng, d_nb = _odd_bwd(
                z, dy, *conv_saved[l], row(sln_g_f, i), row(sln_b_f, i), ws_b[i], wst_b[i],
                _sgu_bias_rows(sgu_b[i]), dconv_w_f[i], row(dn_g_f, i), row(dn_b_f, i), l)
            small[("sgu_ln_g", i)] = d_lg
            small[("sgu_ln_b", i)] = d_lb
            small[("sgu_w", i)] = jnp.where(mask[None], d_ws, 0.0)
            small[("sgu_b", i)] = d_sb
            small[("dconv_w", i)] = jnp.transpose(d_cw, (1, 0, 2)).reshape(CONV_K, D)
            small[("dconv_b", i)] = d_cb
            small[("dnorm_g", i)] = d_ng
            small[("dnorm_b", i)] = d_nb
        gin = _dwin(xin_b, dz, l)
        p_send, p_recv, p_srcs, p_lands, p_token = _split_start(
            _pair_copies(True), f"pair_start_{l}", [gin, gout],
            [jax.ShapeDtypeStruct((NQ, D // 2, WQ), F32), jax.ShapeDtypeStruct((NQ, RQ // 2, D), F32)])
        dxn = _dx(dz, win_g, k, l, dr, p_token)
        if pending is not None:
            finish(pending, dxn)
        (gin, gout), (rin, rout) = _split_wait(_pair_copies(False), f"pair_wait_{l}", p_send, p_recv,
                                               p_srcs, p_lands, dxn)
        pin_b, pin_own = _pair_sum(gin, rin, pos_arr, f"pair_sum_in_{l}")
        pout_b, pout_own = _pair_sum(gout, rout, pos_arr, f"pair_sum_out_{l}")
        send, recv, srcs, lands, token = _split_start(
            _chip_copies(True), f"chip_start_{l}", [pin_b, pout_b],
            [jax.ShapeDtypeStruct((3,) + pin_b.shape[1:], BF16), jax.ShapeDtypeStruct((3,) + pout_b.shape[1:], BF16)])
        pending = (l, send, recv, srcs, lands, pin_own, pout_own)
    grad_x = dxn[None]

    def both(name, first, step):
        return [(first, small[(name, 0)]), (first + step, small[(name, 1)])]

    vectors = ([(R_LN_G + l, d_ln_g[l]) for l in range(NL)] + [(R_LN_B + l, d_ln_b[l]) for l in range(NL)]
               + both("pool_scale", R_PSCALE, 1) + both("sconv_b", R_SCONV_B, 1) + both("sconv_w", R_SCONV_W, SHORT_K)
               + both("sgu_ln_g", R_SLN_G, 1) + both("sgu_ln_b", R_SLN_B, 1) + both("dconv_b", R_DCONV_B, 1)
               + both("dnorm_g", R_DN_G, 1) + both("dnorm_b", R_DN_B, 1) + both("dconv_w", R_DCONV_W, CONV_K)
               + both("sgu_b", R_SGU_B, 4))
    sgu_w_rows = 4 * SGU_BLOCK
    pool_w_rows = 4 * HEAD
    total, total_sgu_w, total_pool_w = _allreduce_small(
        [(R_VECTORS, D, F32, vectors),
         (2 * sgu_w_rows, SGU_BLOCK, BF16,
          [(i * sgu_w_rows, small[("sgu_w", i)].reshape(sgu_w_rows, SGU_BLOCK)) for i in range(2)]),
         (2 * pool_w_rows, HEAD, BF16,
          [(i * pool_w_rows, small[("pool_w", i)].reshape(pool_w_rows, HEAD)) for i in range(2)])],
        token)
    finish(pending, total)

    def mine(a):
        return lax.dynamic_slice_in_dim(a, chip * HEAD, HEAD, axis=a.ndim - 1)

    grads = {
        "ln_g": total[R_LN_G:R_LN_G + 4],
        "ln_b": total[R_LN_B:R_LN_B + 4],
        "pool_scale": total[R_PSCALE:R_PSCALE + 2],
        "sconv_b": total[R_SCONV_B:R_SCONV_B + 2],
        "sconv_w": mine(total[R_SCONV_W:R_SCONV_W + 6].reshape(2, SHORT_K, D)),
        "sgu_ln_g": mine(total[R_SLN_G:R_SLN_G + 2]),
        "sgu_ln_b": mine(total[R_SLN_B:R_SLN_B + 2]),
        "dconv_b": mine(total[R_DCONV_B:R_DCONV_B + 2]),
        "dnorm_g": mine(total[R_DN_G:R_DN_G + 2]),
        "dnorm_b": mine(total[R_DN_B:R_DN_B + 2]),
        "dconv_w": mine(total[R_DCONV_W:R_DCONV_W + 62].reshape(2, CONV_K, D)),
        "sgu_b": total[R_SGU_B:R_SGU_B + 8, 0:SGU_BLOCK].reshape(2, 4, SGU_BLOCK),
        "sgu_w": total_sgu_w.reshape(2, 4, SGU_BLOCK, SGU_BLOCK),
        "pool_w": lax.dynamic_slice_in_dim(total_pool_w.reshape(2, 4, HEAD, HEAD), chip * 64, 64, axis=2),
    }

    deltas, new_m, new_v = {}, {}, {}
    for name in names:
        if name in large:
            grads[name], deltas[name], new_m[name], new_v[name] = large[name]
        else:
            deltas[name], new_m[name], new_v[name] = _adamw(
                weights[name], grads[name], moments_m[name], moments_v[name], f"adamw_{name}")

    return (loss, grad_x, *[grads[n] for n in names], *[deltas[n] for n in names],
            *[new_m[n] for n in names], *[new_v[n] for n in names])
```

```python
import jax
import jax.numpy as jnp
from jax import lax
from jax.experimental import pallas as pl
from jax.experimental.pallas import tpu as pltpu

F32 = jnp.float32
BF16 = jnp.bfloat16
MXU_DTYPE = BF16

D = 1024
DZ = 6144
DY = 2048
NQ = 4
WQ = DZ // NQ
RQ = DY // NQ
NL = 4
ALPHA = (2 * NL) ** 0.25
LN_EPS = 1e-5
CONV_K = 31
SHORT_K = 3
SGU_BLOCK = 128
HEAD = 256
POOL_HALO = 16
CONV_HALO = 32
LANE = 128
SUBLANES = 8
CONV_ROWS = 32
LANE_BLOCKS = 8
MIB = 1024 * 1024

ADAM_LR = 0.001
ADAM_B1 = 0.9
ADAM_B2 = 0.999
ADAM_EPS = 1e-08
ADAM_WD = 0.01
ADAM_STEP = 10

NN = ((1,), (0,))
NT = ((1,), (1,))
TN = ((0,), (0,))
MESH = pl.DeviceIdType.MESH
HBM_SPEC = pl.BlockSpec(memory_space=pltpu.HBM)
SEM_SPEC = pl.BlockSpec(memory_space=pltpu.SEMAPHORE)
SIDE_EFFECT = pltpu.SideEffectType.DATAFLOW_SIDE_EFFECTING

R_LN_G, R_LN_B, R_PSCALE, R_SCONV_B, R_SCONV_W = 0, 4, 8, 10, 12
R_SLN_G, R_SLN_B, R_DCONV_B, R_DN_G, R_DN_B, R_DCONV_W = 18, 20, 22, 24, 26, 28
R_SGU_B, R_VECTORS = 90, 104
Q_SCONV_W, Q_SLN_G, Q_SLN_B, Q_DCONV_B, Q_DN_G, Q_DN_B, Q_DCONV_W, Q_POOL_W, Q_ROWS = 0, 6, 8, 10, 12, 14, 16, 80, 592


def _dot(a, b, dims):
    return lax.dot_general(a.astype(MXU_DTYPE), b.astype(MXU_DTYPE), (dims, ((), ())),
                           preferred_element_type=F32)


def _params(semantics=None, vmem_mib=48):
    return pltpu.CompilerParams(dimension_semantics=semantics, vmem_limit_bytes=vmem_mib * MIB)


def _sigmoid(v):
    return 0.5 * jnp.tanh(0.5 * v) + 0.5


def _silu_and_grad(v):
    s = _sigmoid(v)
    return v * s, s * (1.0 + v * (1.0 - s))


def _ln_stats(v):
    mu = jnp.mean(v, axis=-1, keepdims=True)
    vc = v - mu
    var = jnp.mean(vc * vc, axis=-1, keepdims=True)
    rstd = lax.rsqrt(var + LN_EPS)
    return vc * rstd, rstd


def _ln_bwd_rows(dxhat, xhat, rstd):
    m1 = jnp.mean(dxhat, axis=-1, keepdims=True)
    m2 = jnp.mean(dxhat * xhat, axis=-1, keepdims=True)
    return rstd * (dxhat - m1 - xhat * m2)


def _colsum(v):
    return jnp.sum(v, axis=0, keepdims=True)


def _adamw_math(w, g, m, v):
    m_new = ADAM_B1 * m + (1.0 - ADAM_B1) * g
    v_new = ADAM_B2 * v + (1.0 - ADAM_B2) * (g * g)
    m_hat = m_new / (1.0 - ADAM_B1 ** ADAM_STEP)
    v_hat = v_new / (1.0 - ADAM_B2 ** ADAM_STEP)
    return -ADAM_LR * (m_hat / (jnp.sqrt(v_hat) + ADAM_EPS) + ADAM_WD * w), m_new, v_new


def _mesh_pos():
    return lax.axis_index("x"), lax.axis_index("y"), lax.axis_index("c")


def _after_spec(after):
    return [] if after is None else [pl.BlockSpec(memory_space=pl.ANY)]


def _after_args(after):
    return [] if after is None else [after]


def _proj_in(xb, win_g, k, l, after=None):
    s = xb.shape[0]
    tm = min(s, 1024)

    def body(x_ref, w_ref, *rest):
        rest[-1][...] = _dot(x_ref[...], w_ref[...].reshape(D, WQ), NN)

    return pl.pallas_call(
        body, name=f"proj_in_{l}", grid=(NQ, s // tm),
        in_specs=[pl.BlockSpec((tm, D), lambda q, m: (m, 0)),
                  pl.BlockSpec((None, 2, None, D // 2, WQ), lambda q, m: (q, 0, k, 0, 0))] + _after_spec(after),
        out_specs=pl.BlockSpec((tm, WQ), lambda q, m: (m, q)),
        out_shape=jax.ShapeDtypeStruct((s, DZ), F32),
        compiler_params=_params(("arbitrary", "arbitrary")),
    )(xb, win_g, *_after_args(after))


def _proj_out_ln(ycat, wout_g, k, l, x, g, b):
    s = x.shape[0]
    tm = min(s, 512)

    def body(y_ref, w_ref, x_ref, g_ref, b_ref, xn_ref, xb_ref, xh_ref, rs_ref):
        y = _dot(y_ref[...], w_ref[...].reshape(DY, D), NN)
        xhat, rstd = _ln_stats(ALPHA * x_ref[...] + y)
        xn = xhat * g_ref[...] + b_ref[...]
        xn_ref[...] = xn
        xb_ref[...] = xn.astype(BF16)
        xh_ref[...] = xhat
        rs_ref[...] = rstd

    row = lambda m: (m, 0)
    fixed = lambda m: (0, 0)
    return pl.pallas_call(
        body, name=f"proj_out_ln_{l}", grid=(s // tm,),
        in_specs=[pl.BlockSpec((tm, DY), row),
                  pl.BlockSpec((NQ, 2, None, RQ // 2, D), lambda m: (0, 0, k, 0, 0)),
                  pl.BlockSpec((tm, D), row), pl.BlockSpec((1, D), fixed), pl.BlockSpec((1, D), fixed)],
        out_specs=[pl.BlockSpec((tm, D), row), pl.BlockSpec((tm, D), row), pl.BlockSpec((tm, D), row),
                   pl.BlockSpec((tm, 1), row)],
        out_shape=[jax.ShapeDtypeStruct((s, D), F32), jax.ShapeDtypeStruct((s, D), BF16),
                   jax.ShapeDtypeStruct((s, D), F32), jax.ShapeDtypeStruct((s, 1), F32)],
        compiler_params=_params(("arbitrary",)),
    )(ycat, wout_g, x, g, b)


def _loss_grad(xl, target):
    s = xl.shape[0]
    ts = min(s, 512)

    def body(x_ref, t_ref, loss_ref, dx_ref):
        @pl.when(pl.program_id(0) == 0)
        def _():
            loss_ref[...] = jnp.zeros_like(loss_ref)
        err = x_ref[...] - t_ref[...]
        dx_ref[...] = err * (1.0 / D)
        loss_ref[...] += 0.5 * jnp.sum(jnp.mean(err * err, axis=-1, keepdims=True), axis=0, keepdims=True)

    row = lambda m: (m, 0)
    return pl.pallas_call(
        body, name="loss_grad", grid=(s // ts,),
        in_specs=[pl.BlockSpec((ts, D), row), pl.BlockSpec((ts, D), row)],
        out_specs=[pl.BlockSpec((1, 1), lambda m: (0, 0)), pl.BlockSpec((ts, D), row)],
        out_shape=[jax.ShapeDtypeStruct((1, 1), F32), jax.ShapeDtypeStruct((s, D), F32)],
        compiler_params=_params(("arbitrary",)),
    )(xl, target)


def _ln_bwd(dxn, xhat, rstd, g, l, after=None):
    s = dxn.shape[0]
    ts = min(s, 512)

    def body(d_ref, xh_ref, rs_ref, g_ref, *rest):
        dr_ref, drb_ref, dg_ref, db_ref = rest[-4:]

        @pl.when(pl.program_id(0) == 0)
        def _():
            dg_ref[...] = jnp.zeros_like(dg_ref)
            db_ref[...] = jnp.zeros_like(db_ref)
        d = d_ref[...]
        xhat_v = xh_ref[...]
        dr = _ln_bwd_rows(d * g_ref[...], xhat_v, rs_ref[...])
        dr_ref[...] = dr
        drb_ref[...] = dr.astype(BF16)
        dg_ref[...] += _colsum(d * xhat_v)
        db_ref[...] += _colsum(d)

    row = lambda m: (m, 0)
    fixed = lambda m: (0, 0)
    return pl.pallas_call(
        body, name=f"ln_bwd_{l}", grid=(s // ts,),
        in_specs=[pl.BlockSpec((ts, D), row), pl.BlockSpec((ts, D), row), pl.BlockSpec((ts, 1), row),
                  pl.BlockSpec((1, D), fixed)] + _after_spec(after),
        out_specs=[pl.BlockSpec((ts, D), row), pl.BlockSpec((ts, D), row), pl.BlockSpec((1, D), fixed),
                   pl.BlockSpec((1, D), fixed)],
        out_shape=[jax.ShapeDtypeStruct((s, D), F32), jax.ShapeDtypeStruct((s, D), BF16),
                   jax.ShapeDtypeStruct((1, D), F32), jax.ShapeDtypeStruct((1, D), F32)],
        compiler_params=_params(("arbitrary",)),
    )(dxn, xhat, rstd, g, *_after_args(after))


def _dycat(drb, wout_g, k, l):
    s = drb.shape[0]
    tm = min(s, 512)

    def body(d_ref, w_ref, o_ref):
        o_ref[...] = _dot(d_ref[...], w_ref[...].reshape(DY, D), NT)

    return pl.pallas_call(
        body, name=f"dycat_{l}", grid=(s // tm,),
        in_specs=[pl.BlockSpec((tm, D), lambda m: (m, 0)),
                  pl.BlockSpec((NQ, 2, None, RQ // 2, D), lambda m: (0, 0, k, 0, 0))],
        out_specs=pl.BlockSpec((tm, DY), lambda m: (m, 0)),
        out_shape=jax.ShapeDtypeStruct((s, DY), F32),
        compiler_params=_params(("arbitrary",)),
    )(drb, wout_g)


def _dwout(ycat, drb, l):
    s = drb.shape[0]
    tk = min(s, 1024)

    def body(y_ref, d_ref, o_ref):
        @pl.when(pl.program_id(0) == 0)
        def _():
            o_ref[...] = jnp.zeros_like(o_ref)

        o_ref[...] += _dot(y_ref[...], d_ref[...], TN)

    return pl.pallas_call(
        body, name=f"dwout_{l}", grid=(s // tk,),
        in_specs=[pl.BlockSpec((tk, DY), lambda k: (k, 0)), pl.BlockSpec((tk, D), lambda k: (k, 0))],
        out_specs=pl.BlockSpec((DY, D), lambda k: (0, 0)),
        out_shape=jax.ShapeDtypeStruct((DY, D), F32),
        compiler_params=_params(("arbitrary",)),
    )(ycat, drb)


def _dwin(xb, dzb, l):
    s = xb.shape[0]
    tk = min(s, 1024)

    def body(x_ref, d_ref, o_ref):
        @pl.when(pl.program_id(1) == 0)
        def _():
            o_ref[...] = jnp.zeros_like(o_ref)

        o_ref[...] += _dot(x_ref[...], d_ref[...], TN)

    return pl.pallas_call(
        body, name=f"dwin_{l}", grid=(NQ, s // tk),
        in_specs=[pl.BlockSpec((tk, D), lambda q, k: (k, 0)), pl.BlockSpec((tk, WQ), lambda q, k: (k, q))],
        out_specs=pl.BlockSpec((None, D, WQ), lambda q, k: (q, 0, 0)),
        out_shape=jax.ShapeDtypeStruct((NQ, D, WQ), F32),
        compiler_params=_params(("arbitrary", "arbitrary")),
    )(xb, dzb)


def _dx(dzb, win_g, k, l, dr, after=None):
    s = dzb.shape[0]
    tm = min(s, 1024)

    def body(d_ref, w_ref, r_ref, *rest):
        o_ref = rest[-1]

        @pl.when(pl.program_id(1) == 0)
        def _():
            o_ref[...] = ALPHA * r_ref[...]

        o_ref[...] += _dot(d_ref[...], w_ref[...].reshape(D, WQ), NT)

    return pl.pallas_call(
        body, name=f"dx_{l}", grid=(s // tm, NQ),
        in_specs=[pl.BlockSpec((tm, WQ), lambda m, q: (m, q)),
                  pl.BlockSpec((None, 2, None, D // 2, WQ), lambda m, q: (q, 0, k, 0, 0)),
                  pl.BlockSpec((tm, D), lambda m, q: (m, 0))] + _after_spec(after),
        out_specs=pl.BlockSpec((tm, D), lambda m, q: (m, 0)),
        out_shape=jax.ShapeDtypeStruct((s, D), F32),
        compiler_params=_params(("arbitrary", "arbitrary")),
    )(dzb, win_g, dr, *_after_args(after))


class _RowShifts:
    def __init__(self, ref, most_rows):
        assert ref.shape[0] == most_rows + 2 * SUBLANES
        self.ref = ref
        ref[...] = jnp.zeros(ref.shape, F32)

    def put(self, block):
        self.rows = block.shape[0]
        self.ref[SUBLANES:SUBLANES + self.rows, :] = block

    def get(self, k, causal):
        start = SUBLANES - k if causal else SUBLANES + k
        return self.ref[start:start + self.rows, :]

    def window_sums(self, block, steps, causal):
        acc = block
        for k in (1, 2, 4, 8)[:steps]:
            self.put(acc)
            acc = acc + self.get(k, causal)
        return acc


def _inv_positions(first_pos, rows):
    t1 = (lax.broadcasted_iota(jnp.int32, (rows, 1), 0) + first_pos + 1).astype(F32)
    return t1, 1.0 / t1


def _pool_inv(positions, cb):
    t1, inv_t1 = positions
    window = float(2 << (cb // 2))
    return jnp.where(t1 < window, inv_t1, 1.0 / window)


def _prev_index(ts, halo):
    return lambda i: (jnp.maximum(i * (ts // halo) - 1, 0), 0)


def _next_index(ts, halo, s):
    return lambda i: (jnp.minimum((i + 1) * (ts // halo), s // halo - 1), 0)


def _even_fwd(z, pool_w, pool_scale, sconv_w, sconv_b, l):
    s = z.shape[0]
    ts = min(s, 256)
    h = POOL_HALO

    def body(z_ref, zp_ref, pw_ref, ps_ref, cw_ref, cb_ref, o_ref, shift_ref, pooled_ref):
        i = pl.program_id(0)
        inside = i > 0
        shifts = _RowShifts(shift_ref, h + ts)
        positions = _inv_positions(i * ts, ts)

        for cb in range(LANE_BLOCKS):
            cols = slice(cb * LANE, (cb + 1) * LANE)

            def section(ref, k):
                return ref[:, k * D + cb * LANE:k * D + (cb + 1) * LANE]

            xa = section(z_ref, 0)
            sums = shifts.window_sums(jnp.concatenate([jnp.where(inside, section(zp_ref, 0), 0.0), xa], axis=0),
                                      cb // 2 + 1, True)
            pooled_ref[:, cols] = (sums[h:] * _pool_inv(positions, cb) - xa).astype(MXU_DTYPE)

            q_prev = jnp.where(inside, section(zp_ref, 4) * section(zp_ref, 2), 0.0)
            q_main = section(z_ref, 4) * section(z_ref, 2)
            shifts.put(jnp.concatenate([q_prev, q_main], axis=0))
            cv = (cw_ref[2:3, cols] * q_main + cw_ref[1:2, cols] * shifts.get(1, True)[h:]
                  + cw_ref[0:1, cols] * shifts.get(2, True)[h:] + cb_ref[:, cols])
            silu_gb, _ = _silu_and_grad(section(z_ref, 5))
            o_ref[:, D + cb * LANE:D + (cb + 1) * LANE] = (section(z_ref, 3) * cv * silu_gb).astype(BF16)

        for g in range(4):
            cols = slice(g * HEAD, (g + 1) * HEAD)
            p = _dot(pooled_ref[:, cols], pw_ref[g], NN)
            silu_ga, _ = _silu_and_grad(z_ref[:, D + g * HEAD:D + (g + 1) * HEAD])
            o_ref[:, cols] = (p * ps_ref[:, cols] * silu_ga).astype(BF16)

    fixed2 = lambda i: (0, 0)
    return pl.pallas_call(
        body, name=f"even_fwd_{l}", grid=(s // ts,),
        in_specs=[pl.BlockSpec((ts, DZ), lambda i: (i, 0)), pl.BlockSpec((h, DZ), _prev_index(ts, h)),
                  pl.BlockSpec((4, HEAD, HEAD), lambda i: (0, 0, 0)), pl.BlockSpec((1, D), fixed2),
                  pl.BlockSpec((SHORT_K, D), fixed2), pl.BlockSpec((1, D), fixed2)],
        out_specs=pl.BlockSpec((ts, DY), lambda i: (i, 0)),
        out_shape=jax.ShapeDtypeStruct((s, DY), BF16),
        scratch_shapes=[pltpu.VMEM((h + ts + 2 * SUBLANES, LANE), F32), pltpu.VMEM((ts, D), MXU_DTYPE)],
        compiler_params=_params(("arbitrary",)),
    )(z, z, pool_w, pool_scale, sconv_w, sconv_b)


def _even_bwd(z, dy, pool_w, pool_wt, pool_scale, sconv_w, sconv_b, l):
    s = z.shape[0]
    ts = min(s, 256)
    h = POOL_HALO
    n_tiles = s // ts

    def body(z_ref, zp_ref, zn_ref, dy_ref, dyn_ref, pw_ref, pwt_ref, ps_ref, cw_ref, cb_ref,
             dz_ref, dpw_ref, dps_ref, dcw_ref, dcb_ref, shift_ref, pooled_ref, p_ref, dp_ref, dpooled_ref):
        i = pl.program_id(0)
        inside = i > 0

        @pl.when(i == 0)
        def _():
            dpw_ref[...] = jnp.zeros_like(dpw_ref)
            dps_ref[...] = jnp.zeros_like(dps_ref)
            dcw_ref[...] = jnp.zeros_like(dcw_ref)
            dcb_ref[...] = jnp.zeros_like(dcb_ref)

        shifts = _RowShifts(shift_ref, ts + h)
        positions = _inv_positions(i * ts, ts + h)
        row = lax.broadcasted_iota(jnp.int32, (ts + h, 1), 0)
        live = jnp.logical_or(i < n_tiles - 1, row < ts)

        def section(ref, k, cb):
            return ref[:, k * D + cb * LANE:k * D + (cb + 1) * LANE]

        def with_next(main_ref, next_ref, k, cb):
            return jnp.concatenate([section(main_ref, k, cb), section(next_ref, k, cb)], axis=0)

        for cb in range(LANE_BLOCKS):
            xa = section(z_ref, 0, cb)
            sums = shifts.window_sums(
                jnp.concatenate([jnp.where(inside, section(zp_ref, 0, cb), 0.0), xa], axis=0), cb // 2 + 1, True)
            pooled_ref[:, cb * LANE:(cb + 1) * LANE] = (
                sums[h:] * _pool_inv(positions, cb)[:ts] - xa).astype(MXU_DTYPE)
        for g in range(4):
            cols = slice(g * HEAD, (g + 1) * HEAD)
            p_ref[:, cols] = _dot(pooled_ref[:, cols], pw_ref[g], NN)
        for cb in range(LANE_BLOCKS):
            cols = slice(cb * LANE, (cb + 1) * LANE)
            silu_ga, dsilu_ga = _silu_and_grad(with_next(z_ref, zn_ref, 1, cb))
            d_ya = with_next(dy_ref, dyn_ref, 0, cb)
            scale = ps_ref[:, cols]
            dp_ref[:, cols] = (d_ya * scale * silu_ga).astype(MXU_DTYPE)
            d_ya_p = d_ya[:ts] * p_ref[:, cols]
            dz_ref[:, D + cb * LANE:D + (cb + 1) * LANE] = (d_ya_p * scale * dsilu_ga[:ts]).astype(BF16)
            dps_ref[:, cols] += _colsum(d_ya_p * silu_ga[:ts])
        for g in range(4):
            cols = slice(g * HEAD, (g + 1) * HEAD)
            dpooled_ref[:, cols] = _dot(dp_ref[:, cols], pwt_ref[g], NN)
            dpw_ref[g] += _dot(pooled_ref[:, cols], dp_ref[0:ts, cols], TN)
        for cb in range(LANE_BLOCKS):
            d_pooled = jnp.where(live, dpooled_ref[:, cb * LANE:(cb + 1) * LANE], 0.0)
            sums = shifts.window_sums(d_pooled * _pool_inv(positions, cb), cb // 2 + 1, False)
            dz_ref[:, cb * LANE:(cb + 1) * LANE] = (sums[:ts] - d_pooled[:ts]).astype(BF16)

        for cb in range(LANE_BLOCKS):
            cols = slice(cb * LANE, (cb + 1) * LANE)
            cg = section(z_ref, 4, cb)
            hh = section(z_ref, 2, cb)
            bg = section(z_ref, 3, cb)
            q_main = cg * hh
            q_prev = jnp.where(inside, section(zp_ref, 4, cb) * section(zp_ref, 2, cb), 0.0)
            shifts.put(jnp.concatenate([q_prev, q_main], axis=0))
            q_1 = shifts.get(1, True)[h:]
            q_2 = shifts.get(2, True)[h:]
            w0, w1, w2 = cw_ref[0:1, cols], cw_ref[1:2, cols], cw_ref[2:3, cols]
            cv = w2 * q_main + w1 * q_1 + w0 * q_2 + cb_ref[:, cols]
            silu_gb, dsilu_gb = _silu_and_grad(with_next(z_ref, zn_ref, 5, cb))
            d_yb = with_next(dy_ref, dyn_ref, 1, cb)
            d_cv = jnp.where(live, d_yb * with_next(z_ref, zn_ref, 3, cb) * silu_gb, 0.0)
            d_cv0 = d_cv[:ts]
            shifts.put(d_cv)
            d_q = w2 * d_cv0 + w1 * shifts.get(1, False)[:ts] + w0 * shifts.get(2, False)[:ts]
            d_yb_cv = d_yb[:ts] * cv

            def store(k, val):
                dz_ref[:, k * D + cb * LANE:k * D + (cb + 1) * LANE] = val.astype(BF16)

            store(2, d_q * cg)
            store(3, d_yb_cv * silu_gb[:ts])
            store(4, d_q * hh)
            store(5, d_yb_cv * bg * dsilu_gb[:ts])
            dcb_ref[:, cols] += _colsum(d_cv0)
            dcw_ref[2:3, cols] += _colsum(d_cv0 * q_main)
            dcw_ref[1:2, cols] += _colsum(d_cv0 * q_1)
            dcw_ref[0:1, cols] += _colsum(d_cv0 * q_2)

    fixed2 = lambda i: (0, 0)
    fixed3 = lambda i: (0, 0, 0)
    return pl.pallas_call(
        body, name=f"even_bwd_{l}", grid=(n_tiles,),
        in_specs=[pl.BlockSpec((ts, DZ), lambda i: (i, 0)), pl.BlockSpec((h, DZ), _prev_index(ts, h)),
                  pl.BlockSpec((h, DZ), _next_index(ts, h, s)),
                  pl.BlockSpec((ts, DY), lambda i: (i, 0)), pl.BlockSpec((h, DY), _next_index(ts, h, s)),
                  pl.BlockSpec((4, HEAD, HEAD), fixed3), pl.BlockSpec((4, HEAD, HEAD), fixed3),
                  pl.BlockSpec((1, D), fixed2), pl.BlockSpec((SHORT_K, D), fixed2), pl.BlockSpec((1, D), fixed2)],
        out_specs=[pl.BlockSpec((ts, DZ), lambda i: (i, 0)), pl.BlockSpec((4, HEAD, HEAD), fixed3),
                   pl.BlockSpec((1, D), fixed2), pl.BlockSpec((SHORT_K, D), fixed2), pl.BlockSpec((1, D), fixed2)],
        out_shape=[jax.ShapeDtypeStruct((s, DZ), BF16), jax.ShapeDtypeStruct((4, HEAD, HEAD), F32),
                   jax.ShapeDtypeStruct((1, D), F32), jax.ShapeDtypeStruct((SHORT_K, D), F32),
                   jax.ShapeDtypeStruct((1, D), F32)],
        scratch_shapes=[pltpu.VMEM((ts + h + 2 * SUBLANES, LANE), F32), pltpu.VMEM((ts, D), MXU_DTYPE),
                        pltpu.VMEM((ts, D), F32), pltpu.VMEM((ts + h, D), MXU_DTYPE), pltpu.VMEM((ts + h, D), F32)],
        compiler_params=_params(("arbitrary",), 56),
    )(z, z, z, dy, dy, pool_w, pool_wt, pool_scale, sconv_w, sconv_b)


def _to_blocks(ref, r0, val):
    n = val.shape[0]
    for cb in range(LANE_BLOCKS):
        ref[cb, r0:r0 + n, :] = val[:, cb * LANE:(cb + 1) * LANE]


def _from_blocks(ref):
    return jnp.concatenate([ref[cb] for cb in range(LANE_BLOCKS)], axis=1)


def _shift_copies(src_ref, sh_ref, n, causal):
    def block(cb, carry):
        for b in range(1, SUBLANES):
            if causal:
                sh_ref[cb, b - 1, SUBLANES:n, :] = src_ref[cb, SUBLANES - b:n - b, :]
            else:
                sh_ref[cb, b - 1, 0:n - SUBLANES, :] = src_ref[cb, b:n - SUBLANES + b, :]
        return carry

    lax.fori_loop(0, LANE_BLOCKS, block, 0)


def _tap(src_ref, sh_ref, cb, first, n, d, causal):
    whole, b = (d // SUBLANES) * SUBLANES, d % SUBLANES
    start = first - whole if causal else first + whole
    if b == 0:
        return src_ref[cb, start:start + n, :]
    return sh_ref[cb, b - 1, start:start + n, :]


def _chunk_rows(rows, most):
    return max(n for n in range(CONV_ROWS, most + 1, CONV_ROWS) if rows % n == 0)


def _conv31(src_ref, sh_ref, w_ref, dst_ref, base, rows, causal):
    n = _chunk_rows(rows, 4 * CONV_ROWS)

    def block(cb, carry):
        for r0 in range(0, rows, n):
            acc = None
            for d in range(CONV_K):
                term = w_ref[cb, CONV_K - 1 - d:CONV_K - d, :] * _tap(src_ref, sh_ref, cb, base + r0, n, d, causal)
                acc = term if acc is None else acc + term
            dst_ref[cb, r0:r0 + n, :] = acc
        return carry

    lax.fori_loop(0, LANE_BLOCKS, block, 0)


def _conv31_tap_grads(d_ref, src_ref, sh_ref, dw_ref, base, rows):
    n = _chunk_rows(rows, 2 * CONV_ROWS)

    def block(cb, carry):
        sums = [None] * CONV_K
        for r0 in range(0, rows, n):
            d_blk = d_ref[cb, r0:r0 + n, :]
            for d in range(CONV_K):
                prod = d_blk * _tap(src_ref, sh_ref, cb, base + r0, n, d, True)
                part = prod[0:SUBLANES]
                for k in range(1, n // SUBLANES):
                    part = part + prod[k * SUBLANES:(k + 1) * SUBLANES]
                sums[d] = part if sums[d] is None else sums[d] + part
        for d in range(CONV_K):
            j = CONV_K - 1 - d
            dw_ref[cb, j:j + 1, :] += _colsum(sums[d])
        return carry

    lax.fori_loop(0, LANE_BLOCKS, block, 0)


def _sgu_bias_rows(sgu_b):
    return jnp.repeat(jnp.transpose(sgu_b), HEAD, axis=1)


def _odd_fwd(z, sln_g, sln_b, ws, sbias, dconv_w, dconv_b, dn_g, dn_b, l):
    s = z.shape[0]
    ts = min(s, 256)
    h = CONV_HALO

    def body(z_ref, zp_ref, lg_ref, lb_ref, ws_ref, sb_ref, cw_ref, cb_ref, ng_ref, nb_ref, o_ref, zh_ref, rz_ref,
             zz_ref, zc_ref, sh_ref):
        i = pl.program_id(0)
        vhat, _ = _ln_stats(z_ref[:, D:2 * D])
        vn = (vhat * lg_ref[...] + lb_ref[...]).astype(MXU_DTYPE)
        silu_gc, _ = _silu_and_grad(z_ref[:, 2 * D:3 * D])
        for n in range(ts // SGU_BLOCK):
            rows = slice(n * SGU_BLOCK, (n + 1) * SGU_BLOCK)
            sv = jnp.concatenate(
                [_dot(ws_ref[hd], vn[rows, hd * HEAD:(hd + 1) * HEAD], NN) for hd in range(4)], axis=1)
            sv = sv + sb_ref[...]
            o_ref[rows, 0:D] = (z_ref[rows, 0:D] * sv * silu_gc[rows]).astype(BF16)

        _to_blocks(zz_ref, 0, jnp.where(i > 0, zp_ref[:, 3 * D:4 * D] * _sigmoid(zp_ref[:, 4 * D:5 * D]), 0.0))
        _to_blocks(zz_ref, h, z_ref[:, 3 * D:4 * D] * _sigmoid(z_ref[:, 4 * D:5 * D]))
        _shift_copies(zz_ref, sh_ref, h + ts, True)
        _conv31(zz_ref, sh_ref, cw_ref, zc_ref, h, ts, True)
        zhat, rstd_z = _ln_stats(_from_blocks(zc_ref) + cb_ref[...])
        zh_ref[...] = zhat
        rz_ref[...] = rstd_z
        silu_zn, _ = _silu_and_grad(zhat * ng_ref[...] + nb_ref[...])
        silu_gd, _ = _silu_and_grad(z_ref[:, 5 * D:6 * D])
        o_ref[:, D:2 * D] = (silu_zn * silu_gd).astype(BF16)

    fixed2 = lambda i: (0, 0)
    vec = pl.BlockSpec((1, D), fixed2)
    return pl.pallas_call(
        body, name=f"odd_fwd_{l}", grid=(s // ts,),
        in_specs=[pl.BlockSpec((ts, DZ), lambda i: (i, 0)), pl.BlockSpec((h, DZ), _prev_index(ts, h)),
                  vec, vec, pl.BlockSpec((4, SGU_BLOCK, SGU_BLOCK), lambda i: (0, 0, 0)),
                  pl.BlockSpec((SGU_BLOCK, D), fixed2), pl.BlockSpec((LANE_BLOCKS, CONV_K, LANE), lambda i: (0, 0, 0)),
                  vec, vec, vec],
        out_specs=[pl.BlockSpec((ts, DY), lambda i: (i, 0)), pl.BlockSpec((ts, D), lambda i: (i, 0)),
                   pl.BlockSpec((ts, 1), lambda i: (i, 0))],
        out_shape=[jax.ShapeDtypeStruct((s, DY), BF16), jax.ShapeDtypeStruct((s, D), F32),
                   jax.ShapeDtypeStruct((s, 1), F32)],
        scratch_shapes=[pltpu.VMEM((LANE_BLOCKS, h + ts, LANE), F32), pltpu.VMEM((LANE_BLOCKS, ts, LANE), F32),
                        pltpu.VMEM((LANE_BLOCKS, SUBLANES - 1, h + ts, LANE), F32)],
        compiler_params=_params(("arbitrary",)),
    )(z, z, sln_g, sln_b, ws, sbias, dconv_w, dconv_b, dn_g, dn_b)


def _odd_bwd(z, dy, zhat_s, rstd_s, sln_g, sln_b, ws, wst, sbias, dconv_w, dn_g, dn_b, l):
    s = z.shape[0]
    ts = min(s, 256)
    h = CONV_HALO
    n_tiles = s // ts
    te = ts + h

    def body(z_ref, zp_ref, zn_ref, dy_ref, dyn_ref, zh_ref, zhn_ref, rz_ref, rzn_ref, lg_ref, lb_ref, ws_ref,
             wst_ref, sb_ref, cw_ref, ng_ref, nb_ref, dz_ref, dlg_ref, dlb_ref, dws_ref, dsb_ref, dcw_ref, dcb_ref,
             dng_ref, dnb_ref, zz_ref, dzc_ref, dzz_ref, dsb_acc, sh_ref):
        i = pl.program_id(0)
        more = i < n_tiles - 1

        @pl.when(i == 0)
        def _():
            for ref in (dlg_ref, dlb_ref, dws_ref, dsb_ref, dcw_ref, dcb_ref, dng_ref, dnb_ref, dsb_acc):
                ref[...] = jnp.zeros_like(ref)

        vhat, rstd_v = _ln_stats(z_ref[:, D:2 * D])
        lg = lg_ref[...]
        vn = (vhat * lg + lb_ref[...]).astype(MXU_DTYPE)
        u = z_ref[:, 0:D]
        silu_gc, dsilu_gc = _silu_and_grad(z_ref[:, 2 * D:3 * D])
        d_yc = dy_ref[:, 0:D]
        d_yc_u = d_yc * u
        d_sv = d_yc_u * silu_gc
        d_svb = d_sv.astype(MXU_DTYPE)
        sv_rows = []
        dvn_rows = []
        dsb = None
        for n in range(ts // SGU_BLOCK):
            rows = slice(n * SGU_BLOCK, (n + 1) * SGU_BLOCK)
            sv_parts = []
            dvn_parts = []
            for hd in range(4):
                cols = slice(hd * HEAD, (hd + 1) * HEAD)
                sv_parts.append(_dot(ws_ref[hd], vn[rows, cols], NN))
                dvn_parts.append(_dot(wst_ref[hd], d_svb[rows, cols], NN))
                dws_ref[hd] += _dot(d_svb[rows, cols], vn[rows, cols], NT)
            sv_rows.append(jnp.concatenate(sv_parts, axis=1) + sb_ref[...])
            dvn_rows.append(jnp.concatenate(dvn_parts, axis=1))
            dsb = d_sv[rows] if dsb is None else dsb + d_sv[rows]
        dsb_acc[...] += dsb

        @pl.when(i == n_tiles - 1)
        def _():
            for hd in range(4):
                blk = dsb_acc[:, hd * HEAD:(hd + 1) * HEAD]
                folded = blk[:, 0:LANE] + blk[:, LANE:HEAD]
                dsb_ref[hd:hd + 1, :] = _colsum(jnp.transpose(folded))
        sv = jnp.concatenate(sv_rows, axis=0)
        d_vn = jnp.concatenate(dvn_rows, axis=0)
        dz_ref[:, 0:D] = (d_yc * sv * silu_gc).astype(BF16)
        dz_ref[:, D:2 * D] = _ln_bwd_rows(d_vn * lg, vhat, rstd_v).astype(BF16)
        dz_ref[:, 2 * D:3 * D] = (d_yc_u * sv * dsilu_gc).astype(BF16)
        dlg_ref[...] += _colsum(d_vn * vhat)
        dlb_ref[...] += _colsum(d_vn)

        def gate(ref):
            return ref[:, 3 * D:4 * D] * _sigmoid(ref[:, 4 * D:5 * D])

        _to_blocks(zz_ref, 0, jnp.where(i > 0, gate(zp_ref), 0.0))
        _to_blocks(zz_ref, h, gate(z_ref))
        _shift_copies(zz_ref, sh_ref, h + ts, True)
        zhat = jnp.concatenate([zh_ref[...], zhn_ref[...]], axis=0)
        rstd_z = jnp.concatenate([rz_ref[...], rzn_ref[...]], axis=0)
        ng = ng_ref[...]
        silu_zn, dsilu_zn = _silu_and_grad(zhat * ng + nb_ref[...])
        gd = jnp.concatenate([z_ref[:, 5 * D:6 * D], zn_ref[:, 5 * D:6 * D]], axis=0)
        silu_gd, dsilu_gd = _silu_and_grad(gd)
        d_yd = jnp.concatenate([dy_ref[:, D:2 * D], dyn_ref[:, D:2 * D]], axis=0)
        d_zn = d_yd * silu_gd * dsilu_zn
        d_zc = _ln_bwd_rows(d_zn * ng, zhat, rstd_z)
        row = lax.broadcasted_iota(jnp.int32, (te, 1), 0)
        d_zc = jnp.where(jnp.logical_or(more, row < ts), d_zc, 0.0)
        _to_blocks(dzc_ref, 0, d_zc)
        dz_ref[:, 5 * D:6 * D] = (d_yd[:ts] * silu_zn[:ts] * dsilu_gd[:ts]).astype(BF16)
        dng_ref[...] += _colsum(d_zn[:ts] * zhat[:ts])
        dnb_ref[...] += _colsum(d_zn[:ts])
        dcb_ref[...] += _colsum(d_zc[:ts])
        _conv31_tap_grads(dzc_ref, zz_ref, sh_ref, dcw_ref, h, ts)
        _shift_copies(dzc_ref, sh_ref, te, False)
        _conv31(dzc_ref, sh_ref, cw_ref, dzz_ref, 0, ts, False)
        d_zz = _from_blocks(dzz_ref)
        a = z_ref[:, 3 * D:4 * D]
        sig_b = _sigmoid(z_ref[:, 4 * D:5 * D])
        dz_ref[:, 3 * D:4 * D] = (d_zz * sig_b).astype(BF16)
        dz_ref[:, 4 * D:5 * D] = (d_zz * a * sig_b * (1.0 - sig_b)).astype(BF16)

    fixed2 = lambda i: (0, 0)
    fixed3 = lambda i: (0, 0, 0)
    vec = pl.BlockSpec((1, D), fixed2)
    mat = pl.BlockSpec((4, SGU_BLOCK, SGU_BLOCK), fixed3)
    vec_shape = jax.ShapeDtypeStruct((1, D), F32)
    return pl.pallas_call(
        body, name=f"odd_bwd_{l}", grid=(n_tiles,),
        in_specs=[pl.BlockSpec((ts, DZ), lambda i: (i, 0)), pl.BlockSpec((h, DZ), _prev_index(ts, h)),
                  pl.BlockSpec((h, DZ), _next_index(ts, h, s)),
                  pl.BlockSpec((ts, DY), lambda i: (i, 0)), pl.BlockSpec((h, DY), _next_index(ts, h, s)),
                  pl.BlockSpec((ts, D), lambda i: (i, 0)), pl.BlockSpec((h, D), _next_index(ts, h, s)),
                  pl.BlockSpec((ts, 1), lambda i: (i, 0)), pl.BlockSpec((h, 1), _next_index(ts, h, s)),
                  vec, vec, mat, mat, pl.BlockSpec((SGU_BLOCK, D), fixed2),
                  pl.BlockSpec((LANE_BLOCKS, CONV_K, LANE), fixed3), vec, vec],
        out_specs=[pl.BlockSpec((ts, DZ), lambda i: (i, 0)), vec, vec, mat, pl.BlockSpec((4, SGU_BLOCK), fixed2),
                   pl.BlockSpec((LANE_BLOCKS, CONV_K, LANE), fixed3), vec, vec, vec],
        out_shape=[jax.ShapeDtypeStruct((s, DZ), BF16), vec_shape, vec_shape,
                   jax.ShapeDtypeStruct((4, SGU_BLOCK, SGU_BLOCK), F32), jax.ShapeDtypeStruct((4, SGU_BLOCK), F32),
                   jax.ShapeDtypeStruct((LANE_BLOCKS, CONV_K, LANE), F32), vec_shape, vec_shape, vec_shape],
        scratch_shapes=[pltpu.VMEM((LANE_BLOCKS, h + ts, LANE), F32), pltpu.VMEM((LANE_BLOCKS, te, LANE), F32),
                        pltpu.VMEM((LANE_BLOCKS, ts, LANE), F32), pltpu.VMEM((SGU_BLOCK, D), F32),
                        pltpu.VMEM((LANE_BLOCKS, SUBLANES - 1, te, LANE), F32)],
        compiler_params=_params(("arbitrary",), 60),
    )(z, z, z, dy, dy, zhat_s, zhat_s, rstd_s, rstd_s, sln_g, sln_b, ws, wst, sbias, dconv_w, dn_g, dn_b)


def _remote(src, dst, send_sems, recv_sems, k, to):
    return pltpu.make_async_remote_copy(src_ref=src, dst_ref=dst, send_sem=send_sems.at[k],
                                        recv_sem=recv_sems.at[k], device_id=to, device_id_type=MESH)


def _other_chips(x, y):
    return [(1 - x, y, 2 * (1 - x) + y), (x, 1 - y, 2 * x + 1 - y), (1 - x, 1 - y, 2 * (1 - x) + 1 - y)]


def _cast_own(w_stack, pos_arr, name):
    slots, rows, cols = w_stack.shape
    half = rows // 2

    def body(pos_ref, w_ref, o_ref):
        o_ref[...] = w_ref[...].astype(BF16)

    grid_spec = pltpu.PrefetchScalarGridSpec(
        num_scalar_prefetch=1, grid=(slots, 2),
        in_specs=[pl.BlockSpec((None, half, cols), lambda s, h, pos: (s, h, 0))],
        out_specs=pl.BlockSpec((None, None, None, half, cols), lambda s, h, pos: (pos[1], h, s, 0, 0)))
    return pl.pallas_call(
        body, name=name, grid_spec=grid_spec, out_shape=jax.ShapeDtypeStruct((NQ, 2, slots, half, cols), BF16),
        compiler_params=_params(("arbitrary",) * 2),
    )(pos_arr, w_stack)


def _gather_weights(win_g, wout_g, small_sh):
    def body(win_in, wout_in, small, win_g, wout_g, small_g, send_sems, recv_sems):
        del win_in, wout_in
        x, y, c = _mesh_pos()
        me = 2 * x + y
        sibling = (x, y, 1 - c)
        chips = _other_chips(x, y)

        sends = []
        for j, (cx, cy, _) in enumerate(chips):
            to = (cx, cy, c)
            sends.append(_remote(win_g.at[me, c], win_g.at[me, c], send_sems, recv_sems, j, to))
            sends.append(_remote(wout_g.at[me, c], wout_g.at[me, c], send_sems, recv_sems, 3 + j, to))
            sends.append(_remote(small, small_g.at[me], send_sems, recv_sems, 6 + j, to))
        for cp in sends:
            cp.start()
        passed = []
        for j, (_, _, q) in enumerate(chips):
            got_in = win_g.at[q, c]
            got_out = wout_g.at[q, c]
            _remote(got_in, got_in, send_sems, recv_sems, j, sibling).wait_recv()
            cp = _remote(got_in, got_in, send_sems, recv_sems, 9 + j, sibling)
            cp.start()
            passed.append(cp)
            _remote(got_out, got_out, send_sems, recv_sems, 3 + j, sibling).wait_recv()
            cp = _remote(got_out, got_out, send_sems, recv_sems, 12 + j, sibling)
            cp.start()
            passed.append(cp)
            _remote(small, small_g.at[q], send_sems, recv_sems, 6 + j, sibling).wait_recv()
        for j, (_, _, q) in enumerate(chips):
            from_in = win_g.at[q, 1 - c]
            from_out = wout_g.at[q, 1 - c]
            _remote(from_in, from_in, send_sems, recv_sems, 9 + j, sibling).wait_recv()
            _remote(from_out, from_out, send_sems, recv_sems, 12 + j, sibling).wait_recv()
        for cp in sends + passed:
            cp.wait_send()

    return pl.pallas_call(
        body, name="gather_weights",
        in_specs=[HBM_SPEC, HBM_SPEC, HBM_SPEC], out_specs=[HBM_SPEC, HBM_SPEC, HBM_SPEC],
        out_shape=[jax.ShapeDtypeStruct(win_g.shape, win_g.dtype), jax.ShapeDtypeStruct(wout_g.shape, wout_g.dtype),
                   jax.ShapeDtypeStruct((NQ,) + small_sh.shape, small_sh.dtype)],
        input_output_aliases={0: 0, 1: 1},
        scratch_shapes=[pltpu.SemaphoreType.DMA((15,)), pltpu.SemaphoreType.DMA((15,))],
    )(win_g, wout_g, small_sh)


def _hbm(a):
    return pltpu.with_memory_space_constraint(a, pltpu.HBM)


def _split_start(body, name, sources, landings):
    n_src, n_land = len(sources), len(landings)
    n_buf = n_src + n_land

    def kernel_body(*refs):
        ins, outs = refs[:n_buf], refs[n_buf:]
        send_sems, recv_sems, token = outs[0], outs[1], outs[2 + n_buf]
        body(ins[:n_src], ins[n_src:], send_sems, recv_sems)
        token[...] = jnp.zeros_like(token)

    bufs = [_hbm(a) for a in sources] + [
        _hbm(lax.empty(s.shape, s.dtype) if isinstance(s, jax.ShapeDtypeStruct) else s) for s in landings]
    n_sem = getattr(body, "n_copies")
    out = pl.pallas_call(
        kernel_body, name=name,
        out_shape=(pltpu.SemaphoreType.DMA((n_sem,)), pltpu.SemaphoreType.DMA((n_sem,)),
                   *[pltpu.HBM(b.shape, b.dtype) for b in bufs], jax.ShapeDtypeStruct((8, LANE), F32)),
        in_specs=(HBM_SPEC,) * n_buf,
        out_specs=(SEM_SPEC, SEM_SPEC, *([HBM_SPEC] * n_buf), pl.BlockSpec(memory_space=pltpu.VMEM)),
        input_output_aliases={k: 2 + k for k in range(n_buf)},
        compiler_params=pltpu.CompilerParams(has_side_effects=SIDE_EFFECT),
    )(*bufs)
    return out[0], out[1], list(out[2:2 + n_src]), list(out[2 + n_src:2 + n_buf]), out[2 + n_buf]


def _split_wait(body, name, send_sems, recv_sems, sources, landings, after):
    n_src, n_land = len(sources), len(landings)
    n_buf = n_src + n_land

    def kernel_body(*refs):
        ins = refs[:n_buf]
        body(ins[:n_src], ins[n_src:], refs[n_buf], refs[n_buf + 1])

    bufs = list(sources) + list(landings)
    out = pl.pallas_call(
        kernel_body, name=name,
        out_shape=tuple(pltpu.HBM(b.shape, b.dtype) for b in bufs),
        in_specs=(*([HBM_SPEC] * n_buf), SEM_SPEC, SEM_SPEC, pl.BlockSpec(memory_space=pl.ANY)),
        out_specs=(HBM_SPEC,) * n_buf,
        input_output_aliases={k: k for k in range(n_buf)},
        compiler_params=pltpu.CompilerParams(has_side_effects=SIDE_EFFECT),
    )(*bufs, send_sems, recv_sems, after)
    return list(out[:n_src]), list(out[n_src:])


def _gather_rest_copies(start):
    def body(srcs, lands, send_sems, recv_sems):
        del srcs
        x, y, c = _mesh_pos()
        me = 2 * x + y
        for j, (cx, cy, q) in enumerate(_other_chips(x, y)):
            to = (cx, cy, c)
            for k, gathered in enumerate(lands):
                if start:
                    _remote(gathered.at[me, c], gathered.at[me, c], send_sems, recv_sems, 3 * k + j, to).start()
                else:
                    cp = _remote(gathered.at[me, c], gathered.at[q, c], send_sems, recv_sems, 3 * k + j, to)
                    cp.wait_send()
                    cp.wait_recv()

    body.n_copies = 6
    return body


def _gather_rest_forward(win_g, wout_g):
    def body(win_in, wout_in, win_g, wout_g, send_sems, recv_sems):
        del win_in, wout_in
        x, y, c = _mesh_pos()
        sibling = (x, y, 1 - c)
        passed = []
        for j, (_, _, q) in enumerate(_other_chips(x, y)):
            got_in = win_g.at[q, c]
            got_out = wout_g.at[q, c]
            passed.append(_remote(got_in, got_in, send_sems, recv_sems, j, sibling))
            passed.append(_remote(got_out, got_out, send_sems, recv_sems, 3 + j, sibling))
        for cp in passed:
            cp.start()
        for j, (_, _, q) in enumerate(_other_chips(x, y)):
            from_in = win_g.at[q, 1 - c]
            from_out = wout_g.at[q, 1 - c]
            _remote(from_in, from_in, send_sems, recv_sems, j, sibling).wait_recv()
            _remote(from_out, from_out, send_sems, recv_sems, 3 + j, sibling).wait_recv()
        for cp in passed:
            cp.wait_send()

    return pl.pallas_call(
        body, name="gather_rest_forward",
        in_specs=[HBM_SPEC] * 2, out_specs=[HBM_SPEC] * 2,
        out_shape=[jax.ShapeDtypeStruct(win_g.shape, win_g.dtype), jax.ShapeDtypeStruct(wout_g.shape, wout_g.dtype)],
        input_output_aliases={0: 0, 1: 1},
        scratch_shapes=[pltpu.SemaphoreType.DMA((6,)), pltpu.SemaphoreType.DMA((6,))],
    )(win_g, wout_g)


def _allreduce_small(groups, after):
    pieces = [p for _, _, _, members in groups for _, p in members]
    n_in, n_g = len(pieces), len(groups)

    def body(*refs):
        ins = refs[:n_in]
        outs = refs[-(2 * n_g + 2):-(n_g + 2)]
        alls = refs[-(n_g + 2):-2]
        send_sems, recv_sems = refs[-2:]
        x, y, c = _mesh_pos()
        me = 4 * x + 2 * y + c
        sibling = (x, y, 1 - c)
        chips = _other_chips(x, y)

        k = 0
        for (rows, cols, dtype, members), all_ref in zip(groups, alls):
            if any(piece.shape[1] < cols for _, piece in members) or sum(p.shape[0] for _, p in members) < rows:
                all_ref[me] = jnp.zeros((rows, cols), dtype)
            for first, piece in members:
                n, width = piece.shape
                all_ref[me, first:first + n, 0:width] = ins[k][...].astype(dtype)
                k += 1

        sends, passed = [], []
        for g, all_ref in enumerate(alls):
            sends.append(_remote(all_ref.at[me], all_ref.at[me], send_sems, recv_sems, 7 * g, sibling))
            for j, (cx, cy, _) in enumerate(chips):
                sends.append(_remote(all_ref.at[me], all_ref.at[me], send_sems, recv_sems, 7 * g + 1 + j, (cx, cy, c)))
        for cp in sends:
            cp.start()
        for j, (cx, cy, _) in enumerate(chips):
            for g, all_ref in enumerate(alls):
                got = all_ref.at[4 * cx + 2 * cy + c]
                _remote(got, got, send_sems, recv_sems, 7 * g + 1 + j, sibling).wait_recv()
                cp = _remote(got, got, send_sems, recv_sems, 7 * g + 4 + j, sibling)
                cp.start()
                passed.append(cp)
        for g, all_ref in enumerate(alls):
            got = all_ref.at[4 * x + 2 * y + 1 - c]
            _remote(got, got, send_sems, recv_sems, 7 * g, sibling).wait_recv()
            for j, (cx, cy, _) in enumerate(chips):
                got = all_ref.at[4 * cx + 2 * cy + 1 - c]
                _remote(got, got, send_sems, recv_sems, 7 * g + 4 + j, sibling).wait_recv()
        for cp in sends + passed:
            cp.wait_send()
        for o_ref, all_ref in zip(outs, alls):
            total = all_ref[0].astype(F32)
            for dev in range(1, 8):
                total = total + all_ref[dev].astype(F32)
            o_ref[...] = total

    vmem = pl.BlockSpec(memory_space=pltpu.VMEM)
    return pl.pallas_call(
        body, name="allreduce_small",
        in_specs=[vmem] * n_in + _after_spec(after),
        out_specs=[vmem] * n_g,
        out_shape=[jax.ShapeDtypeStruct((rows, cols), F32) for rows, cols, _, _ in groups],
        scratch_shapes=[pltpu.VMEM((8, rows, cols), dtype) for rows, cols, dtype, _ in groups]
        + [pltpu.SemaphoreType.DMA((7 * n_g,)), pltpu.SemaphoreType.DMA((7 * n_g,))],
        compiler_params=_params(None, 56),
    )(*pieces, *_after_args(after))


def _pair_copies(start):
    def body(srcs, lands, send_sems, recv_sems):
        x, y, c = _mesh_pos()
        sibling = (x, y, 1 - c)
        for k, (g_ref, r_ref) in enumerate(zip(srcs, lands)):
            half = g_ref.shape[1] // 2
            cp = _remote(g_ref.at[:, pl.ds((1 - c) * half, half), :], r_ref, send_sems, recv_sems, k, sibling)
            if start:
                cp.start()
            else:
                cp.wait_send()
                cp.wait_recv()

    body.n_copies = 2
    return body


def _pair_sum(g, r, pos_arr, name):
    nq, rows, cols = r.shape
    tr = min(rows, 256)
    nt = rows // tr

    def body(pos_ref, g_ref, r_ref, ob_ref, own_ref):
        total = g_ref[...] + r_ref[...]
        ob_ref[...] = total.astype(BF16)

        @pl.when(pl.program_id(1) == pos_ref[1])
        def _():
            own_ref[...] = total

    blk = (None, tr, cols)
    grid_spec = pltpu.PrefetchScalarGridSpec(
        num_scalar_prefetch=1, grid=(nt, nq),
        in_specs=[pl.BlockSpec(blk, lambda t, q, pos: (q, pos[0] * nt + t, 0)),
                  pl.BlockSpec(blk, lambda t, q, pos: (q, t, 0))],
        out_specs=[pl.BlockSpec(blk, lambda t, q, pos: (q, t, 0)),
                   pl.BlockSpec((tr, cols), lambda t, q, pos: (t, 0))])
    return pl.pallas_call(
        body, name=name, grid_spec=grid_spec,
        out_shape=[jax.ShapeDtypeStruct(r.shape, BF16), jax.ShapeDtypeStruct((rows, cols), F32)],
        compiler_params=_params(("arbitrary",) * 2),
    )(pos_arr, g, r)


def _chip_copies(start):
    def body(srcs, lands, send_sems, recv_sems):
        pin, pout = srcs
        rin, rout = lands
        x, y, c = _mesh_pos()
        for j, (cx, cy, q) in enumerate(_other_chips(x, y)):
            to = (cx, cy, c)
            for k, (src, dst) in enumerate(((pin, rin), (pout, rout))):
                cp = _remote(src.at[q], dst.at[j], send_sems, recv_sems, 3 * k + j, to)
                if start:
                    cp.start()
                else:
                    cp.wait_send()
                    cp.wait_recv()

    body.n_copies = 6
    return body


def _chip_sum(own, r, pos_arr, name):
    rows, cols = own.shape
    tr = min(rows, 256)

    def body(pos_ref, p_ref, r0_ref, r1_ref, r2_ref, o_ref):
        o_ref[...] = ((p_ref[...] + r0_ref[...].astype(F32)) + r1_ref[...].astype(F32)) + r2_ref[...].astype(F32)

    def peer(j):
        return pl.BlockSpec((None, tr, cols), lambda t, pos: (j, t, 0))

    grid_spec = pltpu.PrefetchScalarGridSpec(
        num_scalar_prefetch=1, grid=(rows // tr,),
        in_specs=[pl.BlockSpec((tr, cols), lambda t, pos: (t, 0)), peer(0), peer(1), peer(2)],
        out_specs=pl.BlockSpec((None, tr, cols), lambda t, pos: (pos[0], t, 0)))
    return pl.pallas_call(
        body, name=name, grid_spec=grid_spec, out_shape=jax.ShapeDtypeStruct((2, rows, cols), F32),
        compiler_params=_params(("arbitrary",)),
    )(pos_arr, own, r, r, r)


def _pair_share(gin, gout, l):
    def body(gin_in, gout_in, gin_ref, gout_ref, send_sems, recv_sems):
        del gin_in, gout_in
        x, y, c = _mesh_pos()
        sibling = (x, y, 1 - c)
        sends = [_remote(gin_ref.at[c], gin_ref.at[c], send_sems, recv_sems, 0, sibling),
                 _remote(gout_ref.at[c], gout_ref.at[c], send_sems, recv_sems, 1, sibling)]
        for cp in sends:
            cp.start()
        _remote(gin_ref.at[1 - c], gin_ref.at[1 - c], send_sems, recv_sems, 0, sibling).wait_recv()
        _remote(gout_ref.at[1 - c], gout_ref.at[1 - c], send_sems, recv_sems, 1, sibling).wait_recv()
        for cp in sends:
            cp.wait_send()

    return pl.pallas_call(
        body, name=f"pair_share_{l}",
        in_specs=[HBM_SPEC, HBM_SPEC], out_specs=[HBM_SPEC, HBM_SPEC],
        out_shape=[jax.ShapeDtypeStruct(gin.shape, F32), jax.ShapeDtypeStruct(gout.shape, F32)],
        input_output_aliases={0: 0, 1: 1},
        scratch_shapes=[pltpu.SemaphoreType.DMA((2,)), pltpu.SemaphoreType.DMA((2,))],
    )(gin, gout)


def _adamw_large(w, m, v, g, i, prev, name):
    _, rows, cols = w.shape
    half = rows // 2
    tr = min(half, 256)
    nt = half // tr

    def body(w_ref, m_ref, v_ref, g_ref, *rest):
        go_ref, d_ref, mo_ref, vo_ref = rest[-4:]
        gv = g_ref[...]
        go_ref[...] = gv
        d_ref[...], mo_ref[...], vo_ref[...] = _adamw_math(w_ref[...], gv, m_ref[...], v_ref[...])

    full = pl.BlockSpec((None, tr, cols), lambda h, t: (i, h * nt + t, 0))
    out = jax.ShapeDtypeStruct(w.shape, F32)
    carried = [] if prev is None else list(prev)
    return pl.pallas_call(
        body, name=name, grid=(2, nt),
        in_specs=[full, full, full, pl.BlockSpec((None, tr, cols), lambda h, t: (h, t, 0))]
        + [pl.BlockSpec(memory_space=pl.ANY)] * len(carried),
        out_specs=[full] * 4, out_shape=[out] * 4,
        input_output_aliases={4 + k: k for k in range(len(carried))},
        compiler_params=_params(("arbitrary",) * 2),
    )(w, m, v, g, *carried)


def _adamw(w, g, m, v, name):
    shape = w.shape
    w2, g2, m2, v2 = (t.reshape(-1, shape[-1]) for t in (w, g, m, v))
    rows, cols = w2.shape
    tr = 256 if rows % 256 == 0 else rows

    def body(w_ref, g_ref, m_ref, v_ref, d_ref, mo_ref, vo_ref):
        d_ref[...], mo_ref[...], vo_ref[...] = _adamw_math(w_ref[...], g_ref[...], m_ref[...], v_ref[...])

    blk = pl.BlockSpec((tr, cols), lambda i: (i, 0))
    out = jax.ShapeDtypeStruct((rows, cols), F32)
    d, mo, vo = pl.pallas_call(
        body, name=name, grid=(rows // tr,), in_specs=[blk] * 4, out_specs=[blk] * 3, out_shape=[out] * 3,
        compiler_params=_params(("arbitrary",)),
    )(w2, g2, m2, v2)
    return d.reshape(shape), mo.reshape(shape), vo.reshape(shape)


def _layer_slot(l):
    return (l % 2) * 2 + l // 2


def kernel(x, ln_g, ln_b, w_in_even, w_out_even, pool_w, pool_scale, sconv_w, sconv_b, w_in_odd, w_out_odd, sgu_ln_g, sgu_ln_b, sgu_w, sgu_b, dconv_w, dconv_b, dnorm_g, dnorm_b, loss_target, m_ln_g, m_ln_b, m_w_in_even, m_w_out_even, m_pool_w, m_pool_scale, m_sconv_w, m_sconv_b, m_w_in_odd, m_w_out_odd, m_sgu_ln_g, m_sgu_ln_b, m_sgu_w, m_sgu_b, m_dconv_w, m_dconv_b, m_dnorm_g, m_dnorm_b, v_ln_g, v_ln_b, v_w_in_even, v_w_out_even, v_pool_w, v_pool_scale, v_sconv_w, v_sconv_b, v_w_in_odd, v_w_out_odd, v_sgu_ln_g, v_sgu_ln_b, v_sgu_w, v_sgu_b, v_dconv_w, v_dconv_b, v_dnorm_g, v_dnorm_b):
    weights = dict(ln_g=ln_g, ln_b=ln_b, w_in_even=w_in_even, w_out_even=w_out_even, pool_w=pool_w,
                   pool_scale=pool_scale, sconv_w=sconv_w, sconv_b=sconv_b, w_in_odd=w_in_odd, w_out_odd=w_out_odd,
                   sgu_ln_g=sgu_ln_g, sgu_ln_b=sgu_ln_b, sgu_w=sgu_w, sgu_b=sgu_b, dconv_w=dconv_w,
                   dconv_b=dconv_b, dnorm_g=dnorm_g, dnorm_b=dnorm_b)
    moments_m = dict(ln_g=m_ln_g, ln_b=m_ln_b, w_in_even=m_w_in_even, w_out_even=m_w_out_even, pool_w=m_pool_w,
                     pool_scale=m_pool_scale, sconv_w=m_sconv_w, sconv_b=m_sconv_b, w_in_odd=m_w_in_odd,
                     w_out_odd=m_w_out_odd, sgu_ln_g=m_sgu_ln_g, sgu_ln_b=m_sgu_ln_b, sgu_w=m_sgu_w, sgu_b=m_sgu_b,
                     dconv_w=m_dconv_w, dconv_b=m_dconv_b, dnorm_g=m_dnorm_g, dnorm_b=m_dnorm_b)
    moments_v = dict(ln_g=v_ln_g, ln_b=v_ln_b, w_in_even=v_w_in_even, w_out_even=v_w_out_even, pool_w=v_pool_w,
                     pool_scale=v_pool_scale, sconv_w=v_sconv_w, sconv_b=v_sconv_b, w_in_odd=v_w_in_odd,
                     w_out_odd=v_w_out_odd, sgu_ln_g=v_sgu_ln_g, sgu_ln_b=v_sgu_ln_b, sgu_w=v_sgu_w, sgu_b=v_sgu_b,
                     dconv_w=v_dconv_w, dconv_b=v_dconv_b, dnorm_g=v_dnorm_g, dnorm_b=v_dnorm_b)
    names = list(weights)

    xd, yd, cd = _mesh_pos()
    chip = 2 * xd + yd
    pos_arr = jnp.stack([cd, chip]).astype(jnp.int32)

    small_sh = jnp.concatenate(
        [sconv_w.reshape(6, HEAD), sgu_ln_g, sgu_ln_b, dconv_b, dnorm_g, dnorm_b, dconv_w.reshape(62, HEAD),
         jnp.zeros((2, HEAD), F32), pool_w.reshape(512, HEAD)], axis=0)
    win_first, wout_first, small_g = _gather_weights(
        _cast_own(w_in_even[0:1], pos_arr, "cast_win_first"), _cast_own(w_out_even[0:1], pos_arr, "cast_wout_first"),
        small_sh)
    small_g = lax.dynamic_update_slice(small_g, small_sh[None], (chip, 0, 0))
    later_in = jnp.concatenate([w_in_even[1:2], w_in_odd], axis=0)
    later_out = jnp.concatenate([w_out_even[1:2], w_out_odd], axis=0)
    g_send, g_recv, _, g_lands, g_token = _split_start(
        _gather_rest_copies(True), "gather_rest_start", [],
        [_cast_own(later_in, pos_arr, "cast_win_rest"), _cast_own(later_out, pos_arr, "cast_wout_rest")])

    def layer_weights(slot):
        return (win_first, wout_first, 0) if slot == 0 else (win_rest, wout_rest, slot - 1)

    def full_rows(lo, n):
        return jnp.transpose(small_g[:, lo:lo + n], (1, 0, 2)).reshape(n, D)

    sconv_w_f = full_rows(Q_SCONV_W, 6).reshape(2, SHORT_K, D)
    sln_g_f = full_rows(Q_SLN_G, 2)
    sln_b_f = full_rows(Q_SLN_B, 2)
    dconv_b_f = full_rows(Q_DCONV_B, 2)
    dn_g_f = full_rows(Q_DN_G, 2)
    dn_b_f = full_rows(Q_DN_B, 2)
    dconv_w_f = jnp.transpose(full_rows(Q_DCONV_W, 62).reshape(2, CONV_K, LANE_BLOCKS, LANE), (0, 2, 1, 3))
    pool_w_f = jnp.transpose(small_g[:, Q_POOL_W:].reshape(NQ, 2, 4, 64, HEAD), (1, 2, 0, 3, 4)).reshape(2, 4, HEAD, HEAD)
    pool_w_b = pool_w_f.astype(BF16)
    pool_wt_b = jnp.swapaxes(pool_w_f, 2, 3).astype(BF16)
    idx = jnp.arange(SGU_BLOCK)
    mask = (idx[None, :] // 64) <= (idx[:, None] // 64)
    ws_f = jnp.where(mask[None, None], sgu_w, 0.0)
    ws_b = ws_f.astype(BF16)
    wst_b = jnp.swapaxes(ws_f, 2, 3).astype(BF16)

    def row(a, i):
        return a[i:i + 1]

    x_f = x[0]
    x_b = x_f.astype(BF16)
    saved = []
    conv_saved = {}
    for l in range(NL):
        i, slot = l // 2, _layer_slot(l)
        if l == 1:
            _, g_lands = _split_wait(_gather_rest_copies(False), "gather_rest_wait", g_send, g_recv, [], g_lands, x_b)
            win_rest, wout_rest = _gather_rest_forward(g_lands[0], g_lands[1])
        win_g, wout_g, k = layer_weights(slot)
        z = _proj_in(x_b, win_g, k, l, g_token if l == 0 else None)
        if l % 2 == 0:
            ycat = _even_fwd(z, pool_w_b[i], row(pool_scale, i), sconv_w_f[i], row(sconv_b, i), l)
        else:
            ycat, conv_hat, conv_rstd = _odd_fwd(
                z, row(sln_g_f, i), row(sln_b_f, i), ws_b[i], _sgu_bias_rows(sgu_b[i]),
                dconv_w_f[i], row(dconv_b_f, i), row(dn_g_f, i), row(dn_b_f, i), l)
            conv_saved[l] = (conv_hat, conv_rstd)
        x_next, x_next_b, xhat, rstd = _proj_out_ln(ycat, wout_g, k, l, x_f, row(ln_g, l), row(ln_b, l))
        saved.append((x_b, z, ycat, xhat, rstd))
        x_f, x_b = x_next, x_next_b

    loss_part, dxn = _loss_grad(x_f, loss_target[0])
    loss = lax.psum(loss_part[0, 0], ("x", "y", "c"))

    small = {}
    d_ln_g = [None] * NL
    d_ln_b = [None] * NL
    large = {"w_in_even": None, "w_out_even": None, "w_in_odd": None, "w_out_odd": None}
    pending = None
    token = None

    def finish(exchange, after):
        lx, send, recv, srcs, lands, own_in, own_out = exchange
        _, (r_in, r_out) = _split_wait(_chip_copies(False), f"chip_wait_{lx}", send, recv, srcs, lands, after)
        fin = _chip_sum(own_in, r_in, pos_arr, f"chip_sum_in_{lx}")
        fout = _chip_sum(own_out, r_out, pos_arr, f"chip_sum_out_{lx}")
        gs_in, gs_out = _pair_share(fin, fout, lx)
        kind = "even" if lx % 2 == 0 else "odd"
        for nm, gs in ((f"w_in_{kind}", gs_in), (f"w_out_{kind}", gs_out)):
            large[nm] = _adamw_large(weights[nm], moments_m[nm], moments_v[nm], gs, lx // 2, large[nm],
                                     f"adamw_{nm}_{lx // 2}")

    for l in reversed(range(NL)):
        i, slot = l // 2, _layer_slot(l)
        win_g, wout_g, k = layer_weights(slot)
        xin_b, z, ycat, xhat, rstd = saved[l]
        dr, dr_b, d_ln_g[l], d_ln_b[l] = _ln_bwd(dxn, xhat, rstd, row(ln_g, l), l, token)
        dy = _dycat(dr_b, wout_g, k, l)
        gout = _dwout(ycat, dr_b, l).reshape(NQ, RQ, D)
        if l % 2 == 0:
            dz, d_pw, d_ps, d_cw, d_cb = _even_bwd(z, dy, pool_w_b[i], pool_wt_b[i], row(pool_scale, i),
                                                   sconv_w_f[i], row(sconv_b, i), l)
            small[("pool_w", i)] = d_pw
            small[("pool_scale", i)] = d_ps
            small[("sconv_w", i)] = d_cw
            small[("sconv_b", i)] = d_cb
        else:
            dz, d_lg, d_lb, d_ws, d_sb, d_cw, d_cb, d_ng, d_nb = _odd_bwd(
                z, dy, *conv_saved[l], row(sln_g_f, i), row(sln_b_f, i), ws_b[i], wst_b[i],
                _sgu_bias_rows(sgu_b[i]), dconv_w_f[i], row(dn_g_f, i), row(dn_b_f, i), l)
            small[("sgu_ln_g", i)] = d_lg
            small[("sgu_ln_b", i)] = d_lb
            small[("sgu_w", i)] = jnp.where(mask[None], d_ws, 0.0)
            small[("sgu_b", i)] = d_sb
            small[("dconv_w", i)] = jnp.transpose(d_cw, (1, 0, 2)).reshape(CONV_K, D)
            small[("dconv_b", i)] = d_cb
            small[("dnorm_g", i)] = d_ng
            small[("dnorm_b", i)] = d_nb
        gin = _dwin(xin_b, dz, l)
        p_send, p_recv, p_srcs, p_lands, p_token = _split_start(
            _pair_copies(True), f"pair_start_{l}", [gin, gout],
            [jax.ShapeDtypeStruct((NQ, D // 2, WQ), F32), jax.ShapeDtypeStruct((NQ, RQ // 2, D), F32)])
        dxn = _dx(dz, win_g, k, l, dr, p_token)
        if pending is not None:
            finish(pending, dxn)
        (gin, gout), (rin, rout) = _split_wait(_pair_copies(False), f"pair_wait_{l}", p_send, p_recv,
                                               p_srcs, p_lands, dxn)
        pin_b, pin_own = _pair_sum(gin, rin, pos_arr, f"pair_sum_in_{l}")
        pout_b, pout_own = _pair_sum(gout, rout, pos_arr, f"pair_sum_out_{l}")
        send, recv, srcs, lands, token = _split_start(
            _chip_copies(True), f"chip_start_{l}", [pin_b, pout_b],
            [jax.ShapeDtypeStruct((3,) + pin_b.shape[1:], BF16), jax.ShapeDtypeStruct((3,) + pout_b.shape[1:], BF16)])
        pending = (l, send, recv, srcs, lands, pin_own, pout_own)
    grad_x = dxn[None]

    def both(name, first, step):
        return [(first, small[(name, 0)]), (first + step, small[(name, 1)])]

    vectors = ([(R_LN_G + l, d_ln_g[l]) for l in range(NL)] + [(R_LN_B + l, d_ln_b[l]) for l in range(NL)]
               + both("pool_scale", R_PSCALE, 1) + both("sconv_b", R_SCONV_B, 1) + both("sconv_w", R_SCONV_W, SHORT_K)
               + both("sgu_ln_g", R_SLN_G, 1) + both("sgu_ln_b", R_SLN_B, 1) + both("dconv_b", R_DCONV_B, 1)
               + both("dnorm_g", R_DN_G, 1) + both("dnorm_b", R_DN_B, 1) + both("dconv_w", R_DCONV_W, CONV_K)
               + both("sgu_b", R_SGU_B, 4))
    sgu_w_rows = 4 * SGU_BLOCK
    pool_w_rows = 4 * HEAD
    total, total_sgu_w, total_pool_w = _allreduce_small(
        [(R_VECTORS, D, F32, vectors),
         (2 * sgu_w_rows, SGU_BLOCK, BF16,
          [(i * sgu_w_rows, small[("sgu_w", i)].reshape(sgu_w_rows, SGU_BLOCK)) for i in range(2)]),
         (2 * pool_w_rows, HEAD, BF16,
          [(i * pool_w_rows, small[("pool_w", i)].reshape(pool_w_rows, HEAD)) for i in range(2)])],
        token)
    finish(pending, total)

    def mine(a):
        return lax.dynamic_slice_in_dim(a, chip * HEAD, HEAD, axis=a.ndim - 1)

    grads = {
        "ln_g": total[R_LN_G:R_LN_G + 4],
        "ln_b": total[R_LN_B:R_LN_B + 4],
        "pool_scale": total[R_PSCALE:R_PSCALE + 2],
        "sconv_b": total[R_SCONV_B:R_SCONV_B + 2],
        "sconv_w": mine(total[R_SCONV_W:R_SCONV_W + 6].reshape(2, SHORT_K, D)),
        "sgu_ln_g": mine(total[R_SLN_G:R_SLN_G + 2]),
        "sgu_ln_b": mine(total[R_SLN_B:R_SLN_B + 2]),
        "dconv_b": mine(total[R_DCONV_B:R_DCONV_B + 2]),
        "dnorm_g": mine(total[R_DN_G:R_DN_G + 2]),
        "dnorm_b": mine(total[R_DN_B:R_DN_B + 2]),
        "dconv_w": mine(total[R_DCONV_W:R_DCONV_W + 62].reshape(2, CONV_K, D)),
        "sgu_b": total[R_SGU_B:R_SGU_B + 8, 0:SGU_BLOCK].reshape(2, 4, SGU_BLOCK),
        "sgu_w": total_sgu_w.reshape(2, 4, SGU_BLOCK, SGU_BLOCK),
        "pool_w": lax.dynamic_slice_in_dim(total_pool_w.reshape(2, 4, HEAD, HEAD), chip * 64, 64, axis=2),
    }

    deltas, new_m, new_v = {}, {}, {}
    for name in names:
        if name in large:
            grads[name], deltas[name], new_m[name], new_v[name] = large[name]
        else:
            deltas[name], new_m[name], new_v[name] = _adamw(
                weights[name], grads[name], moments_m[name], moments_v[name], f"adamw_{name}")

    return (loss, grad_x, *[grads[n] for n in names], *[deltas[n] for n in names],
            *[new_m[n] for n in names], *[new_v[n] for n in names])
```

```python
import jax
import jax.numpy as jnp
from jax import lax
from jax.experimental import pallas as pl
from jax.experimental.pallas import tpu as pltpu

F32 = jnp.float32
BF16 = jnp.bfloat16
MXU_DTYPE = BF16

D = 1024
DZ = 6144
DY = 2048
NQ = 4
WQ = DZ // NQ
RQ = DY // NQ
NL = 4
ALPHA = (2 * NL) ** 0.25
LN_EPS = 1e-5
CONV_K = 31
SHORT_K = 3
SGU_BLOCK = 128
HEAD = 256
POOL_HALO = 16
CONV_HALO = 32
LANE = 128
SUBLANES = 8
CONV_ROWS = 32
LANE_BLOCKS = 8
MIB = 1024 * 1024

ADAM_LR = 0.001
ADAM_B1 = 0.9
ADAM_B2 = 0.999
ADAM_EPS = 1e-08
ADAM_WD = 0.01
ADAM_STEP = 10

NN = ((1,), (0,))
NT = ((1,), (1,))
TN = ((0,), (0,))
MESH = pl.DeviceIdType.MESH
HBM_SPEC = pl.BlockSpec(memory_space=pltpu.HBM)
SEM_SPEC = pl.BlockSpec(memory_space=pltpu.SEMAPHORE)
SIDE_EFFECT = pltpu.SideEffectType.DATAFLOW_SIDE_EFFECTING

R_LN_G, R_LN_B, R_PSCALE, R_SCONV_B, R_SCONV_W = 0, 4, 8, 10, 12
R_SLN_G, R_SLN_B, R_DCONV_B, R_DN_G, R_DN_B, R_DCONV_W = 18, 20, 22, 24, 26, 28
R_SGU_B, R_VECTORS = 90, 104
Q_SCONV_W, Q_SLN_G, Q_SLN_B, Q_DCONV_B, Q_DN_G, Q_DN_B, Q_DCONV_W, Q_POOL_W, Q_ROWS = 0, 6, 8, 10, 12, 14, 16, 80, 592


def _dot(a, b, dims):
    return lax.dot_general(a.astype(MXU_DTYPE), b.astype(MXU_DTYPE), (dims, ((), ())),
                           preferred_element_type=F32)


def _params(semantics=None, vmem_mib=48):
    return pltpu.CompilerParams(dimension_semantics=semantics, vmem_limit_bytes=vmem_mib * MIB)


def _sigmoid(v):
    return 0.5 * jnp.tanh(0.5 * v) + 0.5


def _silu_and_grad(v):
    s = _sigmoid(v)
    return v * s, s * (1.0 + v * (1.0 - s))


def _ln_stats(v):
    mu = jnp.mean(v, axis=-1, keepdims=True)
    vc = v - mu
    var = jnp.mean(vc * vc, axis=-1, keepdims=True)
    rstd = lax.rsqrt(var + LN_EPS)
    return vc * rstd, rstd


def _ln_bwd_rows(dxhat, xhat, rstd):
    m1 = jnp.mean(dxhat, axis=-1, keepdims=True)
    m2 = jnp.mean(dxhat * xhat, axis=-1, keepdims=True)
    return rstd * (dxhat - m1 - xhat * m2)


def _colsum(v):
    return jnp.sum(v, axis=0, keepdims=True)


def _section(ref, k):
    return ref[:, k * D:(k + 1) * D].astype(F32)


def _adamw_math(w, g, m, v):
    m_new = ADAM_B1 * m + (1.0 - ADAM_B1) * g
    v_new = ADAM_B2 * v + (1.0 - ADAM_B2) * (g * g)
    m_hat = m_new / (1.0 - ADAM_B1 ** ADAM_STEP)
    v_hat = v_new / (1.0 - ADAM_B2 ** ADAM_STEP)
    return -ADAM_LR * (m_hat / (jnp.sqrt(v_hat) + ADAM_EPS) + ADAM_WD * w), m_new, v_new


def _mesh_pos():
    return lax.axis_index("x"), lax.axis_index("y"), lax.axis_index("c")


def _after_spec(after):
    return [] if after is None else [pl.BlockSpec(memory_space=pl.ANY)]


def _after_args(after):
    return [] if after is None else [after]


def _proj_in(xb, win_g, k, l, after=None):
    s = xb.shape[0]
    tm = min(s, 1024)

    def body(x_ref, w_ref, *rest):
        rest[-1][...] = _dot(x_ref[...], w_ref[...].reshape(D, WQ), NN).astype(BF16)

    return pl.pallas_call(
        body, name=f"proj_in_{l}", grid=(NQ, s // tm),
        in_specs=[pl.BlockSpec((tm, D), lambda q, m: (m, 0)),
                  pl.BlockSpec((None, 2, None, D // 2, WQ), lambda q, m: (q, 0, k, 0, 0))] + _after_spec(after),
        out_specs=pl.BlockSpec((tm, WQ), lambda q, m: (m, q)),
        out_shape=jax.ShapeDtypeStruct((s, DZ), BF16),
        compiler_params=_params(("arbitrary", "arbitrary")),
    )(xb, win_g, *_after_args(after))


def _proj_out_ln(ycat, wout_g, k, l, res, res_g, res_b, g, b):
    s = res.shape[0]
    tm = min(s, 512)

    def body(y_ref, w_ref, r_ref, rg_ref, rb_ref, g_ref, b_ref, xb_ref, xh_ref, rs_ref):
        y = _dot(y_ref[...], w_ref[...].reshape(DY, D), NN)
        x = r_ref[...] * rg_ref[...] + rb_ref[...]
        xhat, rstd = _ln_stats(ALPHA * x + y)
        xb_ref[...] = (xhat * g_ref[...] + b_ref[...]).astype(BF16)
        xh_ref[...] = xhat
        rs_ref[...] = rstd

    row = lambda m: (m, 0)
    vec = pl.BlockSpec((1, D), lambda m: (0, 0))
    return pl.pallas_call(
        body, name=f"proj_out_ln_{l}", grid=(s // tm,),
        in_specs=[pl.BlockSpec((tm, DY), row),
                  pl.BlockSpec((NQ, 2, None, RQ // 2, D), lambda m: (0, 0, k, 0, 0)),
                  pl.BlockSpec((tm, D), row), vec, vec, vec, vec],
        out_specs=[pl.BlockSpec((tm, D), row), pl.BlockSpec((tm, D), row), pl.BlockSpec((tm, 1), row)],
        out_shape=[jax.ShapeDtypeStruct((s, D), BF16), jax.ShapeDtypeStruct((s, D), F32),
                   jax.ShapeDtypeStruct((s, 1), F32)],
        compiler_params=_params(("arbitrary",)),
    )(ycat, wout_g, res, res_g, res_b, g, b)


def _loss_ln_bwd(xhat, rstd, g, b, target, l):
    s = xhat.shape[0]
    ts = min(s, 512)

    def body(xh_ref, rs_ref, g_ref, b_ref, t_ref, loss_ref, dr_ref, drb_ref, dg_ref, db_ref):
        @pl.when(pl.program_id(0) == 0)
        def _():
            loss_ref[...] = jnp.zeros_like(loss_ref)
            dg_ref[...] = jnp.zeros_like(dg_ref)
            db_ref[...] = jnp.zeros_like(db_ref)
        xhat_v = xh_ref[...]
        gain = g_ref[...]
        err = (xhat_v * gain + b_ref[...]) - t_ref[...]
        loss_ref[...] += 0.5 * jnp.sum(jnp.mean(err * err, axis=-1, keepdims=True), axis=0, keepdims=True)
        d = err * (1.0 / D)
        dr = _ln_bwd_rows(d * gain, xhat_v, rs_ref[...])
        dr_ref[...] = dr
        drb_ref[...] = dr.astype(BF16)
        dg_ref[...] += _colsum(d * xhat_v)
        db_ref[...] += _colsum(d)

    row = lambda m: (m, 0)
    fixed = lambda m: (0, 0)
    vec = pl.BlockSpec((1, D), fixed)
    return pl.pallas_call(
        body, name=f"loss_ln_bwd_{l}", grid=(s // ts,),
        in_specs=[pl.BlockSpec((ts, D), row), pl.BlockSpec((ts, 1), row), vec, vec, pl.BlockSpec((ts, D), row)],
        out_specs=[pl.BlockSpec((1, 1), fixed), pl.BlockSpec((ts, D), row), pl.BlockSpec((ts, D), row), vec, vec],
        out_shape=[jax.ShapeDtypeStruct((1, 1), F32), jax.ShapeDtypeStruct((s, D), F32),
                   jax.ShapeDtypeStruct((s, D), BF16), jax.ShapeDtypeStruct((1, D), F32),
                   jax.ShapeDtypeStruct((1, D), F32)],
        compiler_params=_params(("arbitrary",)),
    )(xhat, rstd, g, b, target)


def _ln_bwd(dxn, xhat, rstd, g, l, after=None):
    s = dxn.shape[0]
    ts = min(s, 512)

    def body(d_ref, xh_ref, rs_ref, g_ref, *rest):
        dr_ref, drb_ref, dg_ref, db_ref = rest[-4:]

        @pl.when(pl.program_id(0) == 0)
        def _():
            dg_ref[...] = jnp.zeros_like(dg_ref)
            db_ref[...] = jnp.zeros_like(db_ref)
        d = d_ref[...]
        xhat_v = xh_ref[...]
        dr = _ln_bwd_rows(d * g_ref[...], xhat_v, rs_ref[...])
        dr_ref[...] = dr
        drb_ref[...] = dr.astype(BF16)
        dg_ref[...] += _colsum(d * xhat_v)
        db_ref[...] += _colsum(d)

    row = lambda m: (m, 0)
    fixed = lambda m: (0, 0)
    return pl.pallas_call(
        body, name=f"ln_bwd_{l}", grid=(s // ts,),
        in_specs=[pl.BlockSpec((ts, D), row), pl.BlockSpec((ts, D), row), pl.BlockSpec((ts, 1), row),
                  pl.BlockSpec((1, D), fixed)] + _after_spec(after),
        out_specs=[pl.BlockSpec((ts, D), row), pl.BlockSpec((ts, D), row), pl.BlockSpec((1, D), fixed),
                   pl.BlockSpec((1, D), fixed)],
        out_shape=[jax.ShapeDtypeStruct((s, D), F32), jax.ShapeDtypeStruct((s, D), BF16),
                   jax.ShapeDtypeStruct((1, D), F32), jax.ShapeDtypeStruct((1, D), F32)],
        compiler_params=_params(("arbitrary",)),
    )(dxn, xhat, rstd, g, *_after_args(after))


def _dycat(drb, wout_g, k, l):
    s = drb.shape[0]
    tm = min(s, 512)

    def body(d_ref, w_ref, o_ref):
        o_ref[...] = _dot(d_ref[...], w_ref[...].reshape(DY, D), NT).astype(BF16)

    return pl.pallas_call(
        body, name=f"dycat_{l}", grid=(s // tm,),
        in_specs=[pl.BlockSpec((tm, D), lambda m: (m, 0)),
                  pl.BlockSpec((NQ, 2, None, RQ // 2, D), lambda m: (0, 0, k, 0, 0))],
        out_specs=pl.BlockSpec((tm, DY), lambda m: (m, 0)),
        out_shape=jax.ShapeDtypeStruct((s, DY), BF16),
        compiler_params=_params(("arbitrary",)),
    )(drb, wout_g)


def _dwout(ycat, drb, l):
    s = drb.shape[0]
    tk = min(s, 1024)

    def body(y_ref, d_ref, o_ref):
        @pl.when(pl.program_id(0) == 0)
        def _():
            o_ref[...] = jnp.zeros_like(o_ref)

        o_ref[...] += _dot(y_ref[...], d_ref[...], TN)

    return pl.pallas_call(
        body, name=f"dwout_{l}", grid=(s // tk,),
        in_specs=[pl.BlockSpec((tk, DY), lambda k: (k, 0)), pl.BlockSpec((tk, D), lambda k: (k, 0))],
        out_specs=pl.BlockSpec((DY, D), lambda k: (0, 0)),
        out_shape=jax.ShapeDtypeStruct((DY, D), F32),
        compiler_params=_params(("arbitrary",)),
    )(ycat, drb)


def _dwin(xb, dzb, l):
    s = xb.shape[0]
    tk = min(s, 1024)

    def body(x_ref, d_ref, o_ref):
        @pl.when(pl.program_id(1) == 0)
        def _():
            o_ref[...] = jnp.zeros_like(o_ref)

        o_ref[...] += _dot(x_ref[...], d_ref[...], TN)

    return pl.pallas_call(
        body, name=f"dwin_{l}", grid=(NQ, s // tk),
        in_specs=[pl.BlockSpec((tk, D), lambda q, k: (k, 0)), pl.BlockSpec((tk, WQ), lambda q, k: (k, q))],
        out_specs=pl.BlockSpec((None, D, WQ), lambda q, k: (q, 0, 0)),
        out_shape=jax.ShapeDtypeStruct((NQ, D, WQ), F32),
        compiler_params=_params(("arbitrary", "arbitrary")),
    )(xb, dzb)


def _dx(dzb, win_g, k, l, dr, after=None):
    s = dzb.shape[0]
    tm = min(s, 1024)

    def body(d_ref, w_ref, r_ref, *rest):
        o_ref = rest[-1]

        @pl.when(pl.program_id(1) == 0)
        def _():
            o_ref[...] = ALPHA * r_ref[...]

        o_ref[...] += _dot(d_ref[...], w_ref[...].reshape(D, WQ), NT)

    return pl.pallas_call(
        body, name=f"dx_{l}", grid=(s // tm, NQ),
        in_specs=[pl.BlockSpec((tm, WQ), lambda m, q: (m, q)),
                  pl.BlockSpec((None, 2, None, D // 2, WQ), lambda m, q: (q, 0, k, 0, 0)),
                  pl.BlockSpec((tm, D), lambda m, q: (m, 0))] + _after_spec(after),
        out_specs=pl.BlockSpec((tm, D), lambda m, q: (m, 0)),
        out_shape=jax.ShapeDtypeStruct((s, D), F32),
        compiler_params=_params(("arbitrary", "arbitrary")),
    )(dzb, win_g, dr, *_after_args(after))


class _RowShifts:
    def __init__(self, ref, most_rows):
        assert ref.shape[0] == most_rows + 2 * SUBLANES
        self.ref = ref
        ref[...] = jnp.zeros(ref.shape, F32)

    def put(self, block):
        self.rows = block.shape[0]
        self.ref[SUBLANES:SUBLANES + self.rows, :] = block

    def get(self, k, causal):
        start = SUBLANES - k if causal else SUBLANES + k
        return self.ref[start:start + self.rows, :]

    def window_sums(self, block, steps, causal):
        acc = block
        for k in (1, 2, 4, 8)[:steps]:
            self.put(acc)
            acc = acc + self.get(k, causal)
        return acc


def _inv_positions(first_pos, rows):
    t1 = (lax.broadcasted_iota(jnp.int32, (rows, 1), 0) + first_pos + 1).astype(F32)
    return t1, 1.0 / t1


def _pool_inv(positions, cb):
    t1, inv_t1 = positions
    window = float(2 << (cb // 2))
    return jnp.where(t1 < window, inv_t1, 1.0 / window)


def _prev_index(ts, halo):
    return lambda i: (jnp.maximum(i * (ts // halo) - 1, 0), 0)


def _next_index(ts, halo, s):
    return lambda i: (jnp.minimum((i + 1) * (ts // halo), s // halo - 1), 0)


def _even_fwd(z, pool_w, pool_scale, sconv_w, sconv_b, l):
    s = z.shape[0]
    ts = min(s, 256)
    h = POOL_HALO

    def body(z_ref, zp_ref, pw_ref, ps_ref, cw_ref, cb_ref, o_ref, shift_ref, pooled_ref):
        i = pl.program_id(0)
        inside = i > 0
        shifts = _RowShifts(shift_ref, h + ts)
        positions = _inv_positions(i * ts, ts)

        for cb in range(LANE_BLOCKS):
            cols = slice(cb * LANE, (cb + 1) * LANE)

            def section(ref, k):
                return ref[:, k * D + cb * LANE:k * D + (cb + 1) * LANE].astype(F32)

            xa = section(z_ref, 0)
            sums = shifts.window_sums(jnp.concatenate([jnp.where(inside, section(zp_ref, 0), 0.0), xa], axis=0),
                                      cb // 2 + 1, True)
            pooled_ref[:, cols] = (sums[h:] * _pool_inv(positions, cb) - xa).astype(MXU_DTYPE)

            q_prev = jnp.where(inside, section(zp_ref, 4) * section(zp_ref, 2), 0.0)
            q_main = section(z_ref, 4) * section(z_ref, 2)
            shifts.put(jnp.concatenate([q_prev, q_main], axis=0))
            cv = (cw_ref[2:3, cols] * q_main + cw_ref[1:2, cols] * shifts.get(1, True)[h:]
                  + cw_ref[0:1, cols] * shifts.get(2, True)[h:] + cb_ref[:, cols])
            silu_gb, _ = _silu_and_grad(section(z_ref, 5))
            o_ref[:, D + cb * LANE:D + (cb + 1) * LANE] = (section(z_ref, 3) * cv * silu_gb).astype(BF16)

        for g in range(4):
            cols = slice(g * HEAD, (g + 1) * HEAD)
            p = _dot(pooled_ref[:, cols], pw_ref[g], NN)
            silu_ga, _ = _silu_and_grad(z_ref[:, D + g * HEAD:D + (g + 1) * HEAD].astype(F32))
            o_ref[:, cols] = (p * ps_ref[:, cols] * silu_ga).astype(BF16)

    fixed2 = lambda i: (0, 0)
    return pl.pallas_call(
        body, name=f"even_fwd_{l}", grid=(s // ts,),
        in_specs=[pl.BlockSpec((ts, DZ), lambda i: (i, 0)), pl.BlockSpec((h, DZ), _prev_index(ts, h)),
                  pl.BlockSpec((4, HEAD, HEAD), lambda i: (0, 0, 0)), pl.BlockSpec((1, D), fixed2),
                  pl.BlockSpec((SHORT_K, D), fixed2), pl.BlockSpec((1, D), fixed2)],
        out_specs=pl.BlockSpec((ts, DY), lambda i: (i, 0)),
        out_shape=jax.ShapeDtypeStruct((s, DY), BF16),
        scratch_shapes=[pltpu.VMEM((h + ts + 2 * SUBLANES, LANE), F32), pltpu.VMEM((ts, D), MXU_DTYPE)],
        compiler_params=_params(("arbitrary",)),
    )(z, z, pool_w, pool_scale, sconv_w, sconv_b)


def _even_bwd(z, dy, pool_w, pool_wt, pool_scale, sconv_w, sconv_b, l):
    s = z.shape[0]
    ts = min(s, 256)
    h = POOL_HALO
    n_tiles = s // ts

    def body(z_ref, zp_ref, zn_ref, dy_ref, dyn_ref, pw_ref, pwt_ref, ps_ref, cw_ref, cb_ref,
             dz_ref, dpw_ref, dps_ref, dcw_ref, dcb_ref, shift_ref, pooled_ref, p_ref, dp_ref, dpooled_ref):
        i = pl.program_id(0)
        inside = i > 0

        @pl.when(i == 0)
        def _():
            dpw_ref[...] = jnp.zeros_like(dpw_ref)
            dps_ref[...] = jnp.zeros_like(dps_ref)
            dcw_ref[...] = jnp.zeros_like(dcw_ref)
            dcb_ref[...] = jnp.zeros_like(dcb_ref)

        shifts = _RowShifts(shift_ref, ts + h)
        positions = _inv_positions(i * ts, ts + h)
        row = lax.broadcasted_iota(jnp.int32, (ts + h, 1), 0)
        live = jnp.logical_or(i < n_tiles - 1, row < ts)

        def section(ref, k, cb):
            return ref[:, k * D + cb * LANE:k * D + (cb + 1) * LANE].astype(F32)

        def with_next(main_ref, next_ref, k, cb):
            return jnp.concatenate([section(main_ref, k, cb), section(next_ref, k, cb)], axis=0)

        for cb in range(LANE_BLOCKS):
            xa = section(z_ref, 0, cb)
            sums = shifts.window_sums(
                jnp.concatenate([jnp.where(inside, section(zp_ref, 0, cb), 0.0), xa], axis=0), cb // 2 + 1, True)
            pooled_ref[:, cb * LANE:(cb + 1) * LANE] = (
                sums[h:] * _pool_inv(positions, cb)[:ts] - xa).astype(MXU_DTYPE)
        for g in range(4):
            cols = slice(g * HEAD, (g + 1) * HEAD)
            p_ref[:, cols] = _dot(pooled_ref[:, cols], pw_ref[g], NN)
        for cb in range(LANE_BLOCKS):
            cols = slice(cb * LANE, (cb + 1) * LANE)
            silu_ga, dsilu_ga = _silu_and_grad(with_next(z_ref, zn_ref, 1, cb))
            d_ya = with_next(dy_ref, dyn_ref, 0, cb)
            scale = ps_ref[:, cols]
            dp_ref[:, cols] = (d_ya * scale * silu_ga).astype(MXU_DTYPE)
            d_ya_p = d_ya[:ts] * p_ref[:, cols]
            dz_ref[:, D + cb * LANE:D + (cb + 1) * LANE] = (d_ya_p * scale * dsilu_ga[:ts]).astype(BF16)
            dps_ref[:, cols] += _colsum(d_ya_p * silu_ga[:ts])
        for g in range(4):
            cols = slice(g * HEAD, (g + 1) * HEAD)
            dpooled_ref[:, cols] = _dot(dp_ref[:, cols], pwt_ref[g], NN)
            dpw_ref[g] += _dot(pooled_ref[:, cols], dp_ref[0:ts, cols], TN)
        for cb in range(LANE_BLOCKS):
            d_pooled = jnp.where(live, dpooled_ref[:, cb * LANE:(cb + 1) * LANE], 0.0)
            sums = shifts.window_sums(d_pooled * _pool_inv(positions, cb), cb // 2 + 1, False)
            dz_ref[:, cb * LANE:(cb + 1) * LANE] = (sums[:ts] - d_pooled[:ts]).astype(BF16)

        for cb in range(LANE_BLOCKS):
            cols = slice(cb * LANE, (cb + 1) * LANE)
            cg = section(z_ref, 4, cb)
            hh = section(z_ref, 2, cb)
            bg = section(z_ref, 3, cb)
            q_main = cg * hh
            q_prev = jnp.where(inside, section(zp_ref, 4, cb) * section(zp_ref, 2, cb), 0.0)
            shifts.put(jnp.concatenate([q_prev, q_main], axis=0))
            q_1 = shifts.get(1, True)[h:]
            q_2 = shifts.get(2, True)[h:]
            w0, w1, w2 = cw_ref[0:1, cols], cw_ref[1:2, cols], cw_ref[2:3, cols]
            cv = w2 * q_main + w1 * q_1 + w0 * q_2 + cb_ref[:, cols]
            silu_gb, dsilu_gb = _silu_and_grad(with_next(z_ref, zn_ref, 5, cb))
            d_yb = with_next(dy_ref, dyn_ref, 1, cb)
            d_cv = jnp.where(live, d_yb * with_next(z_ref, zn_ref, 3, cb) * silu_gb, 0.0)
            d_cv0 = d_cv[:ts]
            shifts.put(d_cv)
            d_q = w2 * d_cv0 + w1 * shifts.get(1, False)[:ts] + w0 * shifts.get(2, False)[:ts]
            d_yb_cv = d_yb[:ts] * cv

            def store(k, val):
                dz_ref[:, k * D + cb * LANE:k * D + (cb + 1) * LANE] = val.astype(BF16)

            store(2, d_q * cg)
            store(3, d_yb_cv * silu_gb[:ts])
            store(4, d_q * hh)
            store(5, d_yb_cv * bg * dsilu_gb[:ts])
            dcb_ref[:, cols] += _colsum(d_cv0)
            dcw_ref[2:3, cols] += _colsum(d_cv0 * q_main)
            dcw_ref[1:2, cols] += _colsum(d_cv0 * q_1)
            dcw_ref[0:1, cols] += _colsum(d_cv0 * q_2)

    fixed2 = lambda i: (0, 0)
    fixed3 = lambda i: (0, 0, 0)
    return pl.pallas_call(
        body, name=f"even_bwd_{l}", grid=(n_tiles,),
        in_specs=[pl.BlockSpec((ts, DZ), lambda i: (i, 0)), pl.BlockSpec((h, DZ), _prev_index(ts, h)),
                  pl.BlockSpec((h, DZ), _next_index(ts, h, s)),
                  pl.BlockSpec((ts, DY), lambda i: (i, 0)), pl.BlockSpec((h, DY), _next_index(ts, h, s)),
                  pl.BlockSpec((4, HEAD, HEAD), fixed3), pl.BlockSpec((4, HEAD, HEAD), fixed3),
                  pl.BlockSpec((1, D), fixed2), pl.BlockSpec((SHORT_K, D), fixed2), pl.BlockSpec((1, D), fixed2)],
        out_specs=[pl.BlockSpec((ts, DZ), lambda i: (i, 0)), pl.BlockSpec((4, HEAD, HEAD), fixed3),
                   pl.BlockSpec((1, D), fixed2), pl.BlockSpec((SHORT_K, D), fixed2), pl.BlockSpec((1, D), fixed2)],
        out_shape=[jax.ShapeDtypeStruct((s, DZ), BF16), jax.ShapeDtypeStruct((4, HEAD, HEAD), F32),
                   jax.ShapeDtypeStruct((1, D), F32), jax.ShapeDtypeStruct((SHORT_K, D), F32),
                   jax.ShapeDtypeStruct((1, D), F32)],
        scratch_shapes=[pltpu.VMEM((ts + h + 2 * SUBLANES, LANE), F32), pltpu.VMEM((ts, D), MXU_DTYPE),
                        pltpu.VMEM((ts, D), F32), pltpu.VMEM((ts + h, D), MXU_DTYPE), pltpu.VMEM((ts + h, D), F32)],
        compiler_params=_params(("arbitrary",), 56),
    )(z, z, z, dy, dy, pool_w, pool_wt, pool_scale, sconv_w, sconv_b)


def _to_blocks(ref, r0, val):
    n = val.shape[0]
    for cb in range(LANE_BLOCKS):
        ref[cb, r0:r0 + n, :] = val[:, cb * LANE:(cb + 1) * LANE]


def _from_blocks(ref):
    return jnp.concatenate([ref[cb] for cb in range(LANE_BLOCKS)], axis=1)


def _shift_copies(src_ref, sh_ref, n, causal):
    def block(cb, carry):
        for b in range(1, SUBLANES):
            if causal:
                sh_ref[cb, b - 1, SUBLANES:n, :] = src_ref[cb, SUBLANES - b:n - b, :]
            else:
                sh_ref[cb, b - 1, 0:n - SUBLANES, :] = src_ref[cb, b:n - SUBLANES + b, :]
        return carry

    lax.fori_loop(0, LANE_BLOCKS, block, 0)


def _tap(src_ref, sh_ref, cb, first, n, d, causal):
    whole, b = (d // SUBLANES) * SUBLANES, d % SUBLANES
    start = first - whole if causal else first + whole
    if b == 0:
        return src_ref[cb, start:start + n, :]
    return sh_ref[cb, b - 1, start:start + n, :]


def _chunk_rows(rows, most):
    return max(n for n in range(CONV_ROWS, most + 1, CONV_ROWS) if rows % n == 0)


def _conv31(src_ref, sh_ref, w_ref, dst_ref, base, rows, causal):
    n = _chunk_rows(rows, 4 * CONV_ROWS)

    def block(cb, carry):
        for r0 in range(0, rows, n):
            acc = None
            for d in range(CONV_K):
                term = w_ref[cb, CONV_K - 1 - d:CONV_K - d, :] * _tap(src_ref, sh_ref, cb, base + r0, n, d, causal)
                acc = term if acc is None else acc + term
            dst_ref[cb, r0:r0 + n, :] = acc
        return carry

    lax.fori_loop(0, LANE_BLOCKS, block, 0)


def _conv31_tap_grads(d_ref, src_ref, sh_ref, dw_ref, base, rows):
    n = _chunk_rows(rows, 2 * CONV_ROWS)

    def block(cb, carry):
        sums = [None] * CONV_K
        for r0 in range(0, rows, n):
            d_blk = d_ref[cb, r0:r0 + n, :]
            for d in range(CONV_K):
                prod = d_blk * _tap(src_ref, sh_ref, cb, base + r0, n, d, True)
                part = prod[0:SUBLANES]
                for k in range(1, n // SUBLANES):
                    part = part + prod[k * SUBLANES:(k + 1) * SUBLANES]
                sums[d] = part if sums[d] is None else sums[d] + part
        for d in range(CONV_K):
            j = CONV_K - 1 - d
            dw_ref[cb, j:j + 1, :] += _colsum(sums[d])
        return carry

    lax.fori_loop(0, LANE_BLOCKS, block, 0)


def _sgu_bias_rows(sgu_b):
    return jnp.repeat(jnp.transpose(sgu_b), HEAD, axis=1)


def _odd_fwd(z, sln_g, sln_b, ws, sbias, dconv_w, dconv_b, dn_g, dn_b, l):
    s = z.shape[0]
    ts = min(s, 256)
    h = CONV_HALO

    def body(z_ref, zp_ref, lg_ref, lb_ref, ws_ref, sb_ref, cw_ref, cb_ref, ng_ref, nb_ref, o_ref, zh_ref, rz_ref,
             zz_ref, zc_ref, sh_ref):
        i = pl.program_id(0)
        vhat, _ = _ln_stats(_section(z_ref, 1))
        vn = (vhat * lg_ref[...] + lb_ref[...]).astype(MXU_DTYPE)
        silu_gc, _ = _silu_and_grad(_section(z_ref, 2))
        for n in range(ts // SGU_BLOCK):
            rows = slice(n * SGU_BLOCK, (n + 1) * SGU_BLOCK)
            sv = jnp.concatenate(
                [_dot(ws_ref[hd], vn[rows, hd * HEAD:(hd + 1) * HEAD], NN) for hd in range(4)], axis=1)
            sv = sv + sb_ref[...]
            o_ref[rows, 0:D] = (z_ref[rows, 0:D].astype(F32) * sv * silu_gc[rows]).astype(BF16)

        _to_blocks(zz_ref, 0, jnp.where(i > 0, _section(zp_ref, 3) * _sigmoid(_section(zp_ref, 4)), 0.0))
        _to_blocks(zz_ref, h, _section(z_ref, 3) * _sigmoid(_section(z_ref, 4)))
        _shift_copies(zz_ref, sh_ref, h + ts, True)
        _conv31(zz_ref, sh_ref, cw_ref, zc_ref, h, ts, True)
        zhat, rstd_z = _ln_stats(_from_blocks(zc_ref) + cb_ref[...])
        zh_ref[...] = zhat
        rz_ref[...] = rstd_z
        silu_zn, _ = _silu_and_grad(zhat * ng_ref[...] + nb_ref[...])
        silu_gd, _ = _silu_and_grad(_section(z_ref, 5))
        o_ref[:, D:2 * D] = (silu_zn * silu_gd).astype(BF16)

    fixed2 = lambda i: (0, 0)
    vec = pl.BlockSpec((1, D), fixed2)
    return pl.pallas_call(
        body, name=f"odd_fwd_{l}", grid=(s // ts,),
        in_specs=[pl.BlockSpec((ts, DZ), lambda i: (i, 0)), pl.BlockSpec((h, DZ), _prev_index(ts, h)),
                  vec, vec, pl.BlockSpec((4, SGU_BLOCK, SGU_BLOCK), lambda i: (0, 0, 0)),
                  pl.BlockSpec((SGU_BLOCK, D), fixed2), pl.BlockSpec((LANE_BLOCKS, CONV_K, LANE), lambda i: (0, 0, 0)),
                  vec, vec, vec],
        out_specs=[pl.BlockSpec((ts, DY), lambda i: (i, 0)), pl.BlockSpec((ts, D), lambda i: (i, 0)),
                   pl.BlockSpec((ts, 1), lambda i: (i, 0))],
        out_shape=[jax.ShapeDtypeStruct((s, DY), BF16), jax.ShapeDtypeStruct((s, D), F32),
                   jax.ShapeDtypeStruct((s, 1), F32)],
        scratch_shapes=[pltpu.VMEM((LANE_BLOCKS, h + ts, LANE), F32), pltpu.VMEM((LANE_BLOCKS, ts, LANE), F32),
                        pltpu.VMEM((LANE_BLOCKS, SUBLANES - 1, h + ts, LANE), F32)],
        compiler_params=_params(("arbitrary",)),
    )(z, z, sln_g, sln_b, ws, sbias, dconv_w, dconv_b, dn_g, dn_b)


def _odd_bwd(z, dy, zhat_s, rstd_s, sln_g, sln_b, ws, wst, sbias, dconv_w, dn_g, dn_b, l):
    s = z.shape[0]
    ts = min(s, 256)
    h = CONV_HALO
    n_tiles = s // ts
    te = ts + h

    def body(z_ref, zp_ref, zn_ref, dy_ref, dyn_ref, zh_ref, zhn_ref, rz_ref, rzn_ref, lg_ref, lb_ref, ws_ref,
             wst_ref, sb_ref, cw_ref, ng_ref, nb_ref, dz_ref, dlg_ref, dlb_ref, dws_ref, dsb_ref, dcw_ref, dcb_ref,
             dng_ref, dnb_ref, zz_ref, dzc_ref, dzz_ref, dsb_acc, sh_ref):
        i = pl.program_id(0)
        more = i < n_tiles - 1

        @pl.when(i == 0)
        def _():
            for ref in (dlg_ref, dlb_ref, dws_ref, dsb_ref, dcw_ref, dcb_ref, dng_ref, dnb_ref, dsb_acc):
                ref[...] = jnp.zeros_like(ref)

        vhat, rstd_v = _ln_stats(_section(z_ref, 1))
        lg = lg_ref[...]
        vn = (vhat * lg + lb_ref[...]).astype(MXU_DTYPE)
        u = _section(z_ref, 0)
        silu_gc, dsilu_gc = _silu_and_grad(_section(z_ref, 2))
        d_yc = _section(dy_ref, 0)
        d_yc_u = d_yc * u
        d_sv = d_yc_u * silu_gc
        d_svb = d_sv.astype(MXU_DTYPE)
        sv_rows = []
        dvn_rows = []
        dsb = None
        for n in range(ts // SGU_BLOCK):
            rows = slice(n * SGU_BLOCK, (n + 1) * SGU_BLOCK)
            sv_parts = []
            dvn_parts = []
            for hd in range(4):
                cols = slice(hd * HEAD, (hd + 1) * HEAD)
                sv_parts.append(_dot(ws_ref[hd], vn[rows, cols], NN))
                dvn_parts.append(_dot(wst_ref[hd], d_svb[rows, cols], NN))
                dws_ref[hd] += _dot(d_svb[rows, cols], vn[rows, cols], NT)
            sv_rows.append(jnp.concatenate(sv_parts, axis=1) + sb_ref[...])
            dvn_rows.append(jnp.concatenate(dvn_parts, axis=1))
            dsb = d_sv[rows] if dsb is None else dsb + d_sv[rows]
        dsb_acc[...] += dsb

        @pl.when(i == n_tiles - 1)
        def _():
            for hd in range(4):
                blk = dsb_acc[:, hd * HEAD:(hd + 1) * HEAD]
                folded = blk[:, 0:LANE] + blk[:, LANE:HEAD]
                dsb_ref[hd:hd + 1, :] = _colsum(jnp.transpose(folded))
        sv = jnp.concatenate(sv_rows, axis=0)
        d_vn = jnp.concatenate(dvn_rows, axis=0)
        dz_ref[:, 0:D] = (d_yc * sv * silu_gc).astype(BF16)
        dz_ref[:, D:2 * D] = _ln_bwd_rows(d_vn * lg, vhat, rstd_v).astype(BF16)
        dz_ref[:, 2 * D:3 * D] = (d_yc_u * sv * dsilu_gc).astype(BF16)
        dlg_ref[...] += _colsum(d_vn * vhat)
        dlb_ref[...] += _colsum(d_vn)

        def gate(ref):
            return _section(ref, 3) * _sigmoid(_section(ref, 4))

        _to_blocks(zz_ref, 0, jnp.where(i > 0, gate(zp_ref), 0.0))
        _to_blocks(zz_ref, h, gate(z_ref))
        _shift_copies(zz_ref, sh_ref, h + ts, True)
        zhat = jnp.concatenate([zh_ref[...], zhn_ref[...]], axis=0)
        rstd_z = jnp.concatenate([rz_ref[...], rzn_ref[...]], axis=0)
        ng = ng_ref[...]
        silu_zn, dsilu_zn = _silu_and_grad(zhat * ng + nb_ref[...])
        gd = jnp.concatenate([_section(z_ref, 5), _section(zn_ref, 5)], axis=0)
        silu_gd, dsilu_gd = _silu_and_grad(gd)
        d_yd = jnp.concatenate([_section(dy_ref, 1), _section(dyn_ref, 1)], axis=0)
        d_zn = d_yd * silu_gd * dsilu_zn
        d_zc = _ln_bwd_rows(d_zn * ng, zhat, rstd_z)
        row = lax.broadcasted_iota(jnp.int32, (te, 1), 0)
        d_zc = jnp.where(jnp.logical_or(more, row < ts), d_zc, 0.0)
        _to_blocks(dzc_ref, 0, d_zc)
        dz_ref[:, 5 * D:6 * D] = (d_yd[:ts] * silu_zn[:ts] * dsilu_gd[:ts]).astype(BF16)
        dng_ref[...] += _colsum(d_zn[:ts] * zhat[:ts])
        dnb_ref[...] += _colsum(d_zn[:ts])
        dcb_ref[...] += _colsum(d_zc[:ts])
        _conv31_tap_grads(dzc_ref, zz_ref, sh_ref, dcw_ref, h, ts)
        _shift_copies(dzc_ref, sh_ref, te, False)
        _conv31(dzc_ref, sh_ref, cw_ref, dzz_ref, 0, ts, False)
        d_zz = _from_blocks(dzz_ref)
        a = _section(z_ref, 3)
        sig_b = _sigmoid(_section(z_ref, 4))
        dz_ref[:, 3 * D:4 * D] = (d_zz * sig_b).astype(BF16)
        dz_ref[:, 4 * D:5 * D] = (d_zz * a * sig_b * (1.0 - sig_b)).astype(BF16)

    fixed2 = lambda i: (0, 0)
    fixed3 = lambda i: (0, 0, 0)
    vec = pl.BlockSpec((1, D), fixed2)
    mat = pl.BlockSpec((4, SGU_BLOCK, SGU_BLOCK), fixed3)
    vec_shape = jax.ShapeDtypeStruct((1, D), F32)
    return pl.pallas_call(
        body, name=f"odd_bwd_{l}", grid=(n_tiles,),
        in_specs=[pl.BlockSpec((ts, DZ), lambda i: (i, 0)), pl.BlockSpec((h, DZ), _prev_index(ts, h)),
                  pl.BlockSpec((h, DZ), _next_index(ts, h, s)),
                  pl.BlockSpec((ts, DY), lambda i: (i, 0)), pl.BlockSpec((h, DY), _next_index(ts, h, s)),
                  pl.BlockSpec((ts, D), lambda i: (i, 0)), pl.BlockSpec((h, D), _next_index(ts, h, s)),
                  pl.BlockSpec((ts, 1), lambda i: (i, 0)), pl.BlockSpec((h, 1), _next_index(ts, h, s)),
                  vec, vec, mat, mat, pl.BlockSpec((SGU_BLOCK, D), fixed2),
                  pl.BlockSpec((LANE_BLOCKS, CONV_K, LANE), fixed3), vec, vec],
        out_specs=[pl.BlockSpec((ts, DZ), lambda i: (i, 0)), vec, vec, mat, pl.BlockSpec((4, SGU_BLOCK), fixed2),
                   pl.BlockSpec((LANE_BLOCKS, CONV_K, LANE), fixed3), vec, vec, vec],
        out_shape=[jax.ShapeDtypeStruct((s, DZ), BF16), vec_shape, vec_shape,
                   jax.ShapeDtypeStruct((4, SGU_BLOCK, SGU_BLOCK), F32), jax.ShapeDtypeStruct((4, SGU_BLOCK), F32),
                   jax.ShapeDtypeStruct((LANE_BLOCKS, CONV_K, LANE), F32), vec_shape, vec_shape, vec_shape],
        scratch_shapes=[pltpu.VMEM((LANE_BLOCKS, h + ts, LANE), F32), pltpu.VMEM((LANE_BLOCKS, te, LANE), F32),
                        pltpu.VMEM((LANE_BLOCKS, ts, LANE), F32), pltpu.VMEM((SGU_BLOCK, D), F32),
                        pltpu.VMEM((LANE_BLOCKS, SUBLANES - 1, te, LANE), F32)],
        compiler_params=_params(("arbitrary",), 60),
    )(z, z, z, dy, dy, zhat_s, zhat_s, rstd_s, rstd_s, sln_g, sln_b, ws, wst, sbias, dconv_w, dn_g, dn_b)


def _remote(src, dst, send_sems, recv_sems, k, to):
    return pltpu.make_async_remote_copy(src_ref=src, dst_ref=dst, send_sem=send_sems.at[k],
                                        recv_sem=recv_sems.at[k], device_id=to, device_id_type=MESH)


def _other_chips(x, y):
    return [(1 - x, y, 2 * (1 - x) + y), (x, 1 - y, 2 * x + 1 - y), (1 - x, 1 - y, 2 * (1 - x) + 1 - y)]


def _cast_own(w_stack, pos_arr, name):
    slots, rows, cols = w_stack.shape
    half = rows // 2

    def body(pos_ref, w_ref, o_ref):
        o_ref[...] = w_ref[...].astype(BF16)

    grid_spec = pltpu.PrefetchScalarGridSpec(
        num_scalar_prefetch=1, grid=(slots, 2),
        in_specs=[pl.BlockSpec((None, half, cols), lambda s, h, pos: (s, h, 0))],
        out_specs=pl.BlockSpec((None, None, None, half, cols), lambda s, h, pos: (pos[1], h, s, 0, 0)))
    return pl.pallas_call(
        body, name=name, grid_spec=grid_spec, out_shape=jax.ShapeDtypeStruct((NQ, 2, slots, half, cols), BF16),
        compiler_params=_params(("arbitrary",) * 2),
    )(pos_arr, w_stack)


def _gather_weights(win_g, wout_g, small_sh):
    def body(win_in, wout_in, small, win_g, wout_g, small_g, send_sems, recv_sems):
        del win_in, wout_in
        x, y, c = _mesh_pos()
        me = 2 * x + y
        sibling = (x, y, 1 - c)
        chips = _other_chips(x, y)

        sends = []
        for j, (cx, cy, _) in enumerate(chips):
            to = (cx, cy, c)
            sends.append(_remote(win_g.at[me, c], win_g.at[me, c], send_sems, recv_sems, j, to))
            sends.append(_remote(wout_g.at[me, c], wout_g.at[me, c], send_sems, recv_sems, 3 + j, to))
            sends.append(_remote(small, small_g.at[me], send_sems, recv_sems, 6 + j, to))
        for cp in sends:
            cp.start()
        passed = []
        for j, (_, _, q) in enumerate(chips):
            got_in = win_g.at[q, c]
            got_out = wout_g.at[q, c]
            _remote(got_in, got_in, send_sems, recv_sems, j, sibling).wait_recv()
            cp = _remote(got_in, got_in, send_sems, recv_sems, 9 + j, sibling)
            cp.start()
            passed.append(cp)
            _remote(got_out, got_out, send_sems, recv_sems, 3 + j, sibling).wait_recv()
            cp = _remote(got_out, got_out, send_sems, recv_sems, 12 + j, sibling)
            cp.start()
            passed.append(cp)
            _remote(small, small_g.at[q], send_sems, recv_sems, 6 + j, sibling).wait_recv()
        for j, (_, _, q) in enumerate(chips):
            from_in = win_g.at[q, 1 - c]
            from_out = wout_g.at[q, 1 - c]
            _remote(from_in, from_in, send_sems, recv_sems, 9 + j, sibling).wait_recv()
            _remote(from_out, from_out, send_sems, recv_sems, 12 + j, sibling).wait_recv()
        for cp in sends + passed:
            cp.wait_send()

    return pl.pallas_call(
        body, name="gather_weights",
        in_specs=[HBM_SPEC, HBM_SPEC, HBM_SPEC], out_specs=[HBM_SPEC, HBM_SPEC, HBM_SPEC],
        out_shape=[jax.ShapeDtypeStruct(win_g.shape, win_g.dtype), jax.ShapeDtypeStruct(wout_g.shape, wout_g.dtype),
                   jax.ShapeDtypeStruct((NQ,) + small_sh.shape, small_sh.dtype)],
        input_output_aliases={0: 0, 1: 1},
        scratch_shapes=[pltpu.SemaphoreType.DMA((15,)), pltpu.SemaphoreType.DMA((15,))],
    )(win_g, wout_g, small_sh)


def _hbm(a):
    return pltpu.with_memory_space_constraint(a, pltpu.HBM)


def _split_start(body, name, sources, landings):
    n_src, n_land = len(sources), len(landings)
    n_buf = n_src + n_land

    def kernel_body(*refs):
        ins, outs = refs[:n_buf], refs[n_buf:]
        send_sems, recv_sems, token = outs[0], outs[1], outs[2 + n_buf]
        body(ins[:n_src], ins[n_src:], send_sems, recv_sems)
        token[...] = jnp.zeros_like(token)

    bufs = [_hbm(a) for a in sources] + [
        _hbm(lax.empty(s.shape, s.dtype) if isinstance(s, jax.ShapeDtypeStruct) else s) for s in landings]
    n_sem = getattr(body, "n_copies")
    out = pl.pallas_call(
        kernel_body, name=name,
        out_shape=(pltpu.SemaphoreType.DMA((n_sem,)), pltpu.SemaphoreType.DMA((n_sem,)),
                   *[pltpu.HBM(b.shape, b.dtype) for b in bufs], jax.ShapeDtypeStruct((8, LANE), F32)),
        in_specs=(HBM_SPEC,) * n_buf,
        out_specs=(SEM_SPEC, SEM_SPEC, *([HBM_SPEC] * n_buf), pl.BlockSpec(memory_space=pltpu.VMEM)),
        input_output_aliases={k: 2 + k for k in range(n_buf)},
        compiler_params=pltpu.CompilerParams(has_side_effects=SIDE_EFFECT),
    )(*bufs)
    return out[0], out[1], list(out[2:2 + n_src]), list(out[2 + n_src:2 + n_buf]), out[2 + n_buf]


def _split_wait(body, name, send_sems, recv_sems, sources, landings, after):
    n_src, n_land = len(sources), len(landings)
    n_buf = n_src + n_land

    def kernel_body(*refs):
        ins = refs[:n_buf]
        body(ins[:n_src], ins[n_src:], refs[n_buf], refs[n_buf + 1])

    bufs = list(sources) + list(landings)
    out = pl.pallas_call(
        kernel_body, name=name,
        out_shape=tuple(pltpu.HBM(b.shape, b.dtype) for b in bufs),
        in_specs=(*([HBM_SPEC] * n_buf), SEM_SPEC, SEM_SPEC, pl.BlockSpec(memory_space=pl.ANY)),
        out_specs=(HBM_SPEC,) * n_buf,
        input_output_aliases={k: k for k in range(n_buf)},
        compiler_params=pltpu.CompilerParams(has_side_effects=SIDE_EFFECT),
    )(*bufs, send_sems, recv_sems, after)
    return list(out[:n_src]), list(out[n_src:])


def _gather_rest_copies(start):
    def body(srcs, lands, send_sems, recv_sems):
        del srcs
        x, y, c = _mesh_pos()
        me = 2 * x + y
        for j, (cx, cy, q) in enumerate(_other_chips(x, y)):
            to = (cx, cy, c)
            for k, gathered in enumerate(lands):
                if start:
                    _remote(gathered.at[me, c], gathered.at[me, c], send_sems, recv_sems, 3 * k + j, to).start()
                else:
                    cp = _remote(gathered.at[me, c], gathered.at[q, c], send_sems, recv_sems, 3 * k + j, to)
                    cp.wait_send()
                    cp.wait_recv()

    body.n_copies = 6
    return body


def _gather_rest_forward(win_g, wout_g):
    def body(win_in, wout_in, win_g, wout_g, send_sems, recv_sems):
        del win_in, wout_in
        x, y, c = _mesh_pos()
        sibling = (x, y, 1 - c)
        passed = []
        for j, (_, _, q) in enumerate(_other_chips(x, y)):
            got_in = win_g.at[q, c]
            got_out = wout_g.at[q, c]
            passed.append(_remote(got_in, got_in, send_sems, recv_sems, j, sibling))
            passed.append(_remote(got_out, got_out, send_sems, recv_sems, 3 + j, sibling))
        for cp in passed:
            cp.start()
        for j, (_, _, q) in enumerate(_other_chips(x, y)):
            from_in = win_g.at[q, 1 - c]
            from_out = wout_g.at[q, 1 - c]
            _remote(from_in, from_in, send_sems, recv_sems, j, sibling).wait_recv()
            _remote(from_out, from_out, send_sems, recv_sems, 3 + j, sibling).wait_recv()
        for cp in passed:
            cp.wait_send()

    return pl.pallas_call(
        body, name="gather_rest_forward",
        in_specs=[HBM_SPEC] * 2, out_specs=[HBM_SPEC] * 2,
        out_shape=[jax.ShapeDtypeStruct(win_g.shape, win_g.dtype), jax.ShapeDtypeStruct(wout_g.shape, wout_g.dtype)],
        input_output_aliases={0: 0, 1: 1},
        scratch_shapes=[pltpu.SemaphoreType.DMA((6,)), pltpu.SemaphoreType.DMA((6,))],
    )(win_g, wout_g)


def _allreduce_small(groups, after):
    pieces = [p for _, _, _, members in groups for _, p in members]
    n_in, n_g = len(pieces), len(groups)

    def body(*refs):
        ins = refs[:n_in]
        outs = refs[-(2 * n_g + 2):-(n_g + 2)]
        alls = refs[-(n_g + 2):-2]
        send_sems, recv_sems = refs[-2:]
        x, y, c = _mesh_pos()
        me = 4 * x + 2 * y + c
        sibling = (x, y, 1 - c)
        chips = _other_chips(x, y)

        k = 0
        for (rows, cols, dtype, members), all_ref in zip(groups, alls):
            if any(piece.shape[1] < cols for _, piece in members) or sum(p.shape[0] for _, p in members) < rows:
                all_ref[me] = jnp.zeros((rows, cols), dtype)
            for first, piece in members:
                n, width = piece.shape
                all_ref[me, first:first + n, 0:width] = ins[k][...].astype(dtype)
                k += 1

        sends, passed = [], []
        for g, all_ref in enumerate(alls):
            sends.append(_remote(all_ref.at[me], all_ref.at[me], send_sems, recv_sems, 7 * g, sibling))
            for j, (cx, cy, _) in enumerate(chips):
                sends.append(_remote(all_ref.at[me], all_ref.at[me], send_sems, recv_sems, 7 * g + 1 + j, (cx, cy, c)))
        for cp in sends:
            cp.start()
        for j, (cx, cy, _) in enumerate(chips):
            for g, all_ref in enumerate(alls):
                got = all_ref.at[4 * cx + 2 * cy + c]
                _remote(got, got, send_sems, recv_sems, 7 * g + 1 + j, sibling).wait_recv()
                cp = _remote(got, got, send_sems, recv_sems, 7 * g + 4 + j, sibling)
                cp.start()
                passed.append(cp)
        for g, all_ref in enumerate(alls):
            got = all_ref.at[4 * x + 2 * y + 1 - c]
            _remote(got, got, send_sems, recv_sems, 7 * g, sibling).wait_recv()
            for j, (cx, cy, _) in enumerate(chips):
                got = all_ref.at[4 * cx + 2 * cy + 1 - c]
                _remote(got, got, send_sems, recv_sems, 7 * g + 4 + j, sibling).wait_recv()
        for cp in sends + passed:
            cp.wait_send()
        for o_ref, all_ref in zip(outs, alls):
            total = all_ref[0].astype(F32)
            for dev in range(1, 8):
                total = total + all_ref[dev].astype(F32)
            o_ref[...] = total

    vmem = pl.BlockSpec(memory_space=pltpu.VMEM)
    return pl.pallas_call(
        body, name="allreduce_small",
        in_specs=[vmem] * n_in + _after_spec(after),
        out_specs=[vmem] * n_g,
        out_shape=[jax.ShapeDtypeStruct((rows, cols), F32) for rows, cols, _, _ in groups],
        scratch_shapes=[pltpu.VMEM((8, rows, cols), dtype) for rows, cols, dtype, _ in groups]
        + [pltpu.SemaphoreType.DMA((7 * n_g,)), pltpu.SemaphoreType.DMA((7 * n_g,))],
        compiler_params=_params(None, 56),
    )(*pieces, *_after_args(after))


def _pair_copies(start):
    def body(srcs, lands, send_sems, recv_sems):
        x, y, c = _mesh_pos()
        sibling = (x, y, 1 - c)
        for k, (g_ref, r_ref) in enumerate(zip(srcs, lands)):
            half = g_ref.shape[1] // 2
            cp = _remote(g_ref.at[:, pl.ds((1 - c) * half, half), :], r_ref, send_sems, recv_sems, k, sibling)
            if start:
                cp.start()
            else:
                cp.wait_send()
                cp.wait_recv()

    body.n_copies = 2
    return body


def _pair_sum(g, r, pos_arr, name):
    nq, rows, cols = r.shape
    tr = min(rows, 256)
    nt = rows // tr

    def body(pos_ref, g_ref, r_ref, ob_ref, own_ref):
        total = g_ref[...] + r_ref[...]
        ob_ref[...] = total.astype(BF16)

        @pl.when(pl.program_id(1) == pos_ref[1])
        def _():
            own_ref[...] = total

    blk = (None, tr, cols)
    grid_spec = pltpu.PrefetchScalarGridSpec(
        num_scalar_prefetch=1, grid=(nt, nq),
        in_specs=[pl.BlockSpec(blk, lambda t, q, pos: (q, pos[0] * nt + t, 0)),
                  pl.BlockSpec(blk, lambda t, q, pos: (q, t, 0))],
        out_specs=[pl.BlockSpec(blk, lambda t, q, pos: (q, t, 0)),
                   pl.BlockSpec((tr, cols), lambda t, q, pos: (t, 0))])
    return pl.pallas_call(
        body, name=name, grid_spec=grid_spec,
        out_shape=[jax.ShapeDtypeStruct(r.shape, BF16), jax.ShapeDtypeStruct((rows, cols), F32)],
        compiler_params=_params(("arbitrary",) * 2),
    )(pos_arr, g, r)


def _chip_copies(start):
    def body(srcs, lands, send_sems, recv_sems):
        pin, pout = srcs
        rin, rout = lands
        x, y, c = _mesh_pos()
        for j, (cx, cy, q) in enumerate(_other_chips(x, y)):
            to = (cx, cy, c)
            for k, (src, dst) in enumerate(((pin, rin), (pout, rout))):
                cp = _remote(src.at[q], dst.at[j], send_sems, recv_sems, 3 * k + j, to)
                if start:
                    cp.start()
                else:
                    cp.wait_send()
                    cp.wait_recv()

    body.n_copies = 6
    return body


def _chip_sum(own, r, pos_arr, name):
    rows, cols = own.shape
    tr = min(rows, 256)

    def body(pos_ref, p_ref, r0_ref, r1_ref, r2_ref, o_ref):
        o_ref[...] = ((p_ref[...] + r0_ref[...].astype(F32)) + r1_ref[...].astype(F32)) + r2_ref[...].astype(F32)

    def peer(j):
        return pl.BlockSpec((None, tr, cols), lambda t, pos: (j, t, 0))

    grid_spec = pltpu.PrefetchScalarGridSpec(
        num_scalar_prefetch=1, grid=(rows // tr,),
        in_specs=[pl.BlockSpec((tr, cols), lambda t, pos: (t, 0)), peer(0), peer(1), peer(2)],
        out_specs=pl.BlockSpec((None, tr, cols), lambda t, pos: (pos[0], t, 0)))
    return pl.pallas_call(
        body, name=name, grid_spec=grid_spec, out_shape=jax.ShapeDtypeStruct((2, rows, cols), F32),
        compiler_params=_params(("arbitrary",)),
    )(pos_arr, own, r, r, r)


def _pair_share(gin, gout, l):
    def body(gin_in, gout_in, gin_ref, gout_ref, send_sems, recv_sems):
        del gin_in, gout_in
        x, y, c = _mesh_pos()
        sibling = (x, y, 1 - c)
        sends = [_remote(gin_ref.at[c], gin_ref.at[c], send_sems, recv_sems, 0, sibling),
                 _remote(gout_ref.at[c], gout_ref.at[c], send_sems, recv_sems, 1, sibling)]
        for cp in sends:
            cp.start()
        _remote(gin_ref.at[1 - c], gin_ref.at[1 - c], send_sems, recv_sems, 0, sibling).wait_recv()
        _remote(gout_ref.at[1 - c], gout_ref.at[1 - c], send_sems, recv_sems, 1, sibling).wait_recv()
        for cp in sends:
            cp.wait_send()

    return pl.pallas_call(
        body, name=f"pair_share_{l}",
        in_specs=[HBM_SPEC, HBM_SPEC], out_specs=[HBM_SPEC, HBM_SPEC],
        out_shape=[jax.ShapeDtypeStruct(gin.shape, F32), jax.ShapeDtypeStruct(gout.shape, F32)],
        input_output_aliases={0: 0, 1: 1},
        scratch_shapes=[pltpu.SemaphoreType.DMA((2,)), pltpu.SemaphoreType.DMA((2,))],
    )(gin, gout)


def _adamw_large(w, m, v, g, i, prev, name):
    _, rows, cols = w.shape
    half = rows // 2
    tr = min(half, 256)
    nt = half // tr

    def body(w_ref, m_ref, v_ref, g_ref, *rest):
        go_ref, d_ref, mo_ref, vo_ref = rest[-4:]
        gv = g_ref[...]
        go_ref[...] = gv
        d_ref[...], mo_ref[...], vo_ref[...] = _adamw_math(w_ref[...], gv, m_ref[...], v_ref[...])

    full = pl.BlockSpec((None, tr, cols), lambda h, t: (i, h * nt + t, 0))
    out = jax.ShapeDtypeStruct(w.shape, F32)
    carried = [] if prev is None else list(prev)
    return pl.pallas_call(
        body, name=name, grid=(2, nt),
        in_specs=[full, full, full, pl.BlockSpec((None, tr, cols), lambda h, t: (h, t, 0))]
        + [pl.BlockSpec(memory_space=pl.ANY)] * len(carried),
        out_specs=[full] * 4, out_shape=[out] * 4,
        input_output_aliases={4 + k: k for k in range(len(carried))},
        compiler_params=_params(("arbitrary",) * 2),
    )(w, m, v, g, *carried)


def _adamw(w, g, m, v, name):
    shape = w.shape
    w2, g2, m2, v2 = (t.reshape(-1, shape[-1]) for t in (w, g, m, v))
    rows, cols = w2.shape
    tr = 256 if rows % 256 == 0 else rows

    def body(w_ref, g_ref, m_ref, v_ref, d_ref, mo_ref, vo_ref):
        d_ref[...], mo_ref[...], vo_ref[...] = _adamw_math(w_ref[...], g_ref[...], m_ref[...], v_ref[...])

    blk = pl.BlockSpec((tr, cols), lambda i: (i, 0))
    out = jax.ShapeDtypeStruct((rows, cols), F32)
    d, mo, vo = pl.pallas_call(
        body, name=name, grid=(rows // tr,), in_specs=[blk] * 4, out_specs=[blk] * 3, out_shape=[out] * 3,
        compiler_params=_params(("arbitrary",)),
    )(w2, g2, m2, v2)
    return d.reshape(shape), mo.reshape(shape), vo.reshape(shape)


def _layer_slot(l):
    return (l % 2) * 2 + l // 2


def kernel(x, ln_g, ln_b, w_in_even, w_out_even, pool_w, pool_scale, sconv_w, sconv_b, w_in_odd, w_out_odd, sgu_ln_g, sgu_ln_b, sgu_w, sgu_b, dconv_w, dconv_b, dnorm_g, dnorm_b, loss_target, m_ln_g, m_ln_b, m_w_in_even, m_w_out_even, m_pool_w, m_pool_scale, m_sconv_w, m_sconv_b, m_w_in_odd, m_w_out_odd, m_sgu_ln_g, m_sgu_ln_b, m_sgu_w, m_sgu_b, m_dconv_w, m_dconv_b, m_dnorm_g, m_dnorm_b, v_ln_g, v_ln_b, v_w_in_even, v_w_out_even, v_pool_w, v_pool_scale, v_sconv_w, v_sconv_b, v_w_in_odd, v_w_out_odd, v_sgu_ln_g, v_sgu_ln_b, v_sgu_w, v_sgu_b, v_dconv_w, v_dconv_b, v_dnorm_g, v_dnorm_b):
    weights = dict(ln_g=ln_g, ln_b=ln_b, w_in_even=w_in_even, w_out_even=w_out_even, pool_w=pool_w,
                   pool_scale=pool_scale, sconv_w=sconv_w, sconv_b=sconv_b, w_in_odd=w_in_odd, w_out_odd=w_out_odd,
                   sgu_ln_g=sgu_ln_g, sgu_ln_b=sgu_ln_b, sgu_w=sgu_w, sgu_b=sgu_b, dconv_w=dconv_w,
                   dconv_b=dconv_b, dnorm_g=dnorm_g, dnorm_b=dnorm_b)
    moments_m = dict(ln_g=m_ln_g, ln_b=m_ln_b, w_in_even=m_w_in_even, w_out_even=m_w_out_even, pool_w=m_pool_w,
                     pool_scale=m_pool_scale, sconv_w=m_sconv_w, sconv_b=m_sconv_b, w_in_odd=m_w_in_odd,
                     w_out_odd=m_w_out_odd, sgu_ln_g=m_sgu_ln_g, sgu_ln_b=m_sgu_ln_b, sgu_w=m_sgu_w, sgu_b=m_sgu_b,
                     dconv_w=m_dconv_w, dconv_b=m_dconv_b, dnorm_g=m_dnorm_g, dnorm_b=m_dnorm_b)
    moments_v = dict(ln_g=v_ln_g, ln_b=v_ln_b, w_in_even=v_w_in_even, w_out_even=v_w_out_even, pool_w=v_pool_w,
                     pool_scale=v_pool_scale, sconv_w=v_sconv_w, sconv_b=v_sconv_b, w_in_odd=v_w_in_odd,
                     w_out_odd=v_w_out_odd, sgu_ln_g=v_sgu_ln_g, sgu_ln_b=v_sgu_ln_b, sgu_w=v_sgu_w, sgu_b=v_sgu_b,
                     dconv_w=v_dconv_w, dconv_b=v_dconv_b, dnorm_g=v_dnorm_g, dnorm_b=v_dnorm_b)
    names = list(weights)

    xd, yd, cd = _mesh_pos()
    chip = 2 * xd + yd
    pos_arr = jnp.stack([cd, chip]).astype(jnp.int32)

    small_sh = jnp.concatenate(
        [sconv_w.reshape(6, HEAD), sgu_ln_g, sgu_ln_b, dconv_b, dnorm_g, dnorm_b, dconv_w.reshape(62, HEAD),
         jnp.zeros((2, HEAD), F32), pool_w.reshape(512, HEAD)], axis=0)
    win_first, wout_first, small_g = _gather_weights(
        _cast_own(w_in_even[0:1], pos_arr, "cast_win_first"), _cast_own(w_out_even[0:1], pos_arr, "cast_wout_first"),
        small_sh)
    small_g = lax.dynamic_update_slice(small_g, small_sh[None], (chip, 0, 0))
    later_in = jnp.concatenate([w_in_even[1:2], w_in_odd], axis=0)
    later_out = jnp.concatenate([w_out_even[1:2], w_out_odd], axis=0)
    g_send, g_recv, _, g_lands, g_token = _split_start(
        _gather_rest_copies(True), "gather_rest_start", [],
        [_cast_own(later_in, pos_arr, "cast_win_rest"), _cast_own(later_out, pos_arr, "cast_wout_rest")])

    def layer_weights(slot):
        return (win_first, wout_first, 0) if slot == 0 else (win_rest, wout_rest, slot - 1)

    def full_rows(lo, n):
        return jnp.transpose(small_g[:, lo:lo + n], (1, 0, 2)).reshape(n, D)

    sconv_w_f = full_rows(Q_SCONV_W, 6).reshape(2, SHORT_K, D)
    sln_g_f = full_rows(Q_SLN_G, 2)
    sln_b_f = full_rows(Q_SLN_B, 2)
    dconv_b_f = full_rows(Q_DCONV_B, 2)
    dn_g_f = full_rows(Q_DN_G, 2)
    dn_b_f = full_rows(Q_DN_B, 2)
    dconv_w_f = jnp.transpose(full_rows(Q_DCONV_W, 62).reshape(2, CONV_K, LANE_BLOCKS, LANE), (0, 2, 1, 3))
    pool_w_f = jnp.transpose(small_g[:, Q_POOL_W:].reshape(NQ, 2, 4, 64, HEAD), (1, 2, 0, 3, 4)).reshape(2, 4, HEAD, HEAD)
    pool_w_b = pool_w_f.astype(BF16)
    pool_wt_b = jnp.swapaxes(pool_w_f, 2, 3).astype(BF16)
    idx = jnp.arange(SGU_BLOCK)
    mask = (idx[None, :] // 64) <= (idx[:, None] // 64)
    ws_f = jnp.where(mask[None, None], sgu_w, 0.0)
    ws_b = ws_f.astype(BF16)
    wst_b = jnp.swapaxes(ws_f, 2, 3).astype(BF16)

    def row(a, i):
        return a[i:i + 1]

    residual = (x[0], jnp.ones((1, D), F32), jnp.zeros((1, D), F32))
    x_b = x[0].astype(BF16)
    saved = []
    conv_saved = {}
    for l in range(NL):
        i, slot = l // 2, _layer_slot(l)
        if l == 1:
            _, g_lands = _split_wait(_gather_rest_copies(False), "gather_rest_wait", g_send, g_recv, [], g_lands, x_b)
            win_rest, wout_rest = _gather_rest_forward(g_lands[0], g_lands[1])
        win_g, wout_g, k = layer_weights(slot)
        z = _proj_in(x_b, win_g, k, l, g_token if l == 0 else None)
        if l % 2 == 0:
            ycat = _even_fwd(z, pool_w_b[i], row(pool_scale, i), sconv_w_f[i], row(sconv_b, i), l)
        else:
            ycat, conv_hat, conv_rstd = _odd_fwd(
                z, row(sln_g_f, i), row(sln_b_f, i), ws_b[i], _sgu_bias_rows(sgu_b[i]),
                dconv_w_f[i], row(dconv_b_f, i), row(dn_g_f, i), row(dn_b_f, i), l)
            conv_saved[l] = (conv_hat, conv_rstd)
        x_next_b, xhat, rstd = _proj_out_ln(ycat, wout_g, k, l, *residual, row(ln_g, l), row(ln_b, l))
        saved.append((x_b, z, ycat, xhat, rstd))
        residual = (xhat, row(ln_g, l), row(ln_b, l))
        x_b = x_next_b

    small = {}
    d_ln_g = [None] * NL
    d_ln_b = [None] * NL
    large = {"w_in_even": None, "w_out_even": None, "w_in_odd": None, "w_out_odd": None}
    pending = None
    token = None

    def finish(exchange, after):
        lx, send, recv, srcs, lands, own_in, own_out = exchange
        _, (r_in, r_out) = _split_wait(_chip_copies(False), f"chip_wait_{lx}", send, recv, srcs, lands, after)
        fin = _chip_sum(own_in, r_in, pos_arr, f"chip_sum_in_{lx}")
        fout = _chip_sum(own_out, r_out, pos_arr, f"chip_sum_out_{lx}")
        gs_in, gs_out = _pair_share(fin, fout, lx)
        kind = "even" if lx % 2 == 0 else "odd"
        for nm, gs in ((f"w_in_{kind}", gs_in), (f"w_out_{kind}", gs_out)):
            large[nm] = _adamw_large(weights[nm], moments_m[nm], moments_v[nm], gs, lx // 2, large[nm],
                                     f"adamw_{nm}_{lx // 2}")

    for l in reversed(range(NL)):
        i, slot = l // 2, _layer_slot(l)
        win_g, wout_g, k = layer_weights(slot)
        xin_b, z, ycat, xhat, rstd = saved[l]
        if l == NL - 1:
            loss_part, dr, dr_b, d_ln_g[l], d_ln_b[l] = _loss_ln_bwd(
                xhat, rstd, row(ln_g, l), row(ln_b, l), loss_target[0], l)
            loss = lax.psum(loss_part[0, 0], ("x", "y", "c"))
        else:
            dr, dr_b, d_ln_g[l], d_ln_b[l] = _ln_bwd(dxn, xhat, rstd, row(ln_g, l), l, token)
        dy = _dycat(dr_b, wout_g, k, l)
        gout = _dwout(ycat, dr_b, l).reshape(NQ, RQ, D)
        if l % 2 == 0:
            dz, d_pw, d_ps, d_cw, d_cb = _even_bwd(z, dy, pool_w_b[i], pool_wt_b[i], row(pool_scale, i),
                                                   sconv_w_f[i], row(sconv_b, i), l)
            small[("pool_w", i)] = d_pw
            small[("pool_scale", i)] = d_ps
            small[("sconv_w", i)] = d_cw
            small[("sconv_b", i)] = d_cb
        else:
            dz, d_lg, d_lb, d_ws, d_sb, d_cw, d_cb, d_ng, d_nb = _odd_bwd(
                z, dy, *conv_saved[l], row(sln_g_f, i), row(sln_b_f, i), ws_b[i], wst_b[i],
                _sgu_bias_rows(sgu_b[i]), dconv_w_f[i], row(dn_g_f, i), row(dn_b_f, i), l)
            small[("sgu_ln_g", i)] = d_lg
            small[("sgu_ln_b", i)] = d_lb
            small[("sgu_w", i)] = jnp.where(mask[None], d_ws, 0.0)
            small[("sgu_b", i)] = d_sb
            small[("dconv_w", i)] = jnp.transpose(d_cw, (1, 0, 2)).reshape(CONV_K, D)
            small[("dconv_b", i)] = d_cb
            small[("dnorm_g", i)] = d_ng
            small[("dnorm_b", i)] = d_nb
        gin = _dwin(xin_b, dz, l)
        p_send, p_recv, p_srcs, p_lands, p_token = _split_start(
            _pair_copies(True), f"pair_start_{l}", [gin, gout],
            [jax.ShapeDtypeStruct((NQ, D // 2, WQ), F32), jax.ShapeDtypeStruct((NQ, RQ // 2, D), F32)])
        dxn = _dx(dz, win_g, k, l, dr, p_token)
        if pending is not None:
            finish(pending, dxn)
        (gin, gout), (rin, rout) = _split_wait(_pair_copies(False), f"pair_wait_{l}", p_send, p_recv,
                                               p_srcs, p_lands, dxn)
        pin_b, pin_own = _pair_sum(gin, rin, pos_arr, f"pair_sum_in_{l}")
        pout_b, pout_own = _pair_sum(gout, rout, pos_arr, f"pair_sum_out_{l}")
        send, recv, srcs, lands, token = _split_start(
            _chip_copies(True), f"chip_start_{l}", [pin_b, pout_b],
            [jax.ShapeDtypeStruct((3,) + pin_b.shape[1:], BF16), jax.ShapeDtypeStruct((3,) + pout_b.shape[1:], BF16)])
        pending = (l, send, recv, srcs, lands, pin_own, pout_own)
    grad_x = dxn[None]

    def both(name, first, step):
        return [(first, small[(name, 0)]), (first + step, small[(name, 1)])]

    vectors = ([(R_LN_G + l, d_ln_g[l]) for l in range(NL)] + [(R_LN_B + l, d_ln_b[l]) for l in range(NL)]
               + both("pool_scale", R_PSCALE, 1) + both("sconv_b", R_SCONV_B, 1) + both("sconv_w", R_SCONV_W, SHORT_K)
               + both("sgu_ln_g", R_SLN_G, 1) + both("sgu_ln_b", R_SLN_B, 1) + both("dconv_b", R_DCONV_B, 1)
               + both("dnorm_g", R_DN_G, 1) + both("dnorm_b", R_DN_B, 1) + both("dconv_w", R_DCONV_W, CONV_K)
               + both("sgu_b", R_SGU_B, 4))
    sgu_w_rows = 4 * SGU_BLOCK
    pool_w_rows = 4 * HEAD
    total, total_sgu_w, total_pool_w = _allreduce_small(
        [(R_VECTORS, D, F32, vectors),
         (2 * sgu_w_rows, SGU_BLOCK, BF16,
          [(i * sgu_w_rows, small[("sgu_w", i)].reshape(sgu_w_rows, SGU_BLOCK)) for i in range(2)]),
         (2 * pool_w_rows, HEAD, BF16,
          [(i * pool_w_rows, small[("pool_w", i)].reshape(pool_w_rows, HEAD)) for i in range(2)])],
        token)
    finish(pending, total)

    def mine(a):
        return lax.dynamic_slice_in_dim(a, chip * HEAD, HEAD, axis=a.ndim - 1)

    grads = {
        "ln_g": total[R_LN_G:R_LN_G + 4],
        "ln_b": total[R_LN_B:R_LN_B + 4],
        "pool_scale": total[R_PSCALE:R_PSCALE + 2],
        "sconv_b": total[R_SCONV_B:R_SCONV_B + 2],
        "sconv_w": mine(total[R_SCONV_W:R_SCONV_W + 6].reshape(2, SHORT_K, D)),
        "sgu_ln_g": mine(total[R_SLN_G:R_SLN_G + 2]),
        "sgu_ln_b": mine(total[R_SLN_B:R_SLN_B + 2]),
        "dconv_b": mine(total[R_DCONV_B:R_DCONV_B + 2]),
        "dnorm_g": mine(total[R_DN_G:R_DN_G + 2]),
        "dnorm_b": mine(total[R_DN_B:R_DN_B + 2]),
        "dconv_w": mine(total[R_DCONV_W:R_DCONV_W + 62].reshape(2, CONV_K, D)),
        "sgu_b": total[R_SGU_B:R_SGU_B + 8, 0:SGU_BLOCK].reshape(2, 4, SGU_BLOCK),
        "sgu_w": total_sgu_w.reshape(2, 4, SGU_BLOCK, SGU_BLOCK),
        "pool_w": lax.dynamic_slice_in_dim(total_pool_w.reshape(2, 4, HEAD, HEAD), chip * 64, 64, axis=2),
    }

    deltas, new_m, new_v = {}, {}, {}
    for name in names:
        if name in large:
            grads[name], deltas[name], new_m[name], new_v[name] = large[name]
        else:
            deltas[name], new_m[name], new_v[name] = _adamw(
                weights[name], grads[name], moments_m[name], moments_v[name], f"adamw_{name}")

    return (loss, grad_x, *[grads[n] for n in names], *[deltas[n] for n in names],
            *[new_m[n] for n in names], *[new_v[n] for n in names])
```

```python
import jax
import jax.numpy as jnp
from jax import lax
from jax.experimental import pallas as pl
from jax.experimental.pallas import tpu as pltpu

F32 = jnp.float32
BF16 = jnp.bfloat16
MXU_DTYPE = BF16

D = 1024
DZ = 6144
DY = 2048
NQ = 4
WQ = DZ // NQ
RQ = DY // NQ
NL = 4
ALPHA = (2 * NL) ** 0.25
LN_EPS = 1e-5
CONV_K = 31
SHORT_K = 3
SGU_BLOCK = 128
HEAD = 256
POOL_HALO = 16
CONV_HALO = 32
LANE = 128
SUBLANES = 8
CONV_ROWS = 32
LANE_BLOCKS = 8
MIB = 1024 * 1024

ADAM_LR = 0.001
ADAM_B1 = 0.9
ADAM_B2 = 0.999
ADAM_EPS = 1e-08
ADAM_WD = 0.01
ADAM_STEP = 10

NN = ((1,), (0,))
NT = ((1,), (1,))
TN = ((0,), (0,))
MESH = pl.DeviceIdType.MESH
HBM_SPEC = pl.BlockSpec(memory_space=pltpu.HBM)
SEM_SPEC = pl.BlockSpec(memory_space=pltpu.SEMAPHORE)
SIDE_EFFECT = pltpu.SideEffectType.DATAFLOW_SIDE_EFFECTING

R_LN_G, R_LN_B, R_PSCALE, R_SCONV_B, R_SCONV_W = 0, 4, 8, 10, 12
R_SLN_G, R_SLN_B, R_DCONV_B, R_DN_G, R_DN_B, R_DCONV_W = 18, 20, 22, 24, 26, 28
R_SGU_B, R_VECTORS = 90, 104
Q_SCONV_W, Q_SLN_G, Q_SLN_B, Q_DCONV_B, Q_DN_G, Q_DN_B, Q_DCONV_W, Q_POOL_W, Q_ROWS = 0, 6, 8, 10, 12, 14, 16, 80, 592


def _dot(a, b, dims):
    return lax.dot_general(a.astype(MXU_DTYPE), b.astype(MXU_DTYPE), (dims, ((), ())),
                           preferred_element_type=F32)


def _params(semantics=None, vmem_mib=48):
    return pltpu.CompilerParams(dimension_semantics=semantics, vmem_limit_bytes=vmem_mib * MIB)


def _sigmoid(v):
    return 0.5 * jnp.tanh(0.5 * v) + 0.5


def _silu_and_grad(v):
    s = _sigmoid(v)
    return v * s, s * (1.0 + v * (1.0 - s))


def _ln_stats(v):
    mu = jnp.mean(v, axis=-1, keepdims=True)
    vc = v - mu
    var = jnp.mean(vc * vc, axis=-1, keepdims=True)
    rstd = lax.rsqrt(var + LN_EPS)
    return vc * rstd, rstd


def _ln_bwd_rows(dxhat, xhat, rstd):
    m1 = jnp.mean(dxhat, axis=-1, keepdims=True)
    m2 = jnp.mean(dxhat * xhat, axis=-1, keepdims=True)
    return rstd * (dxhat - m1 - xhat * m2)


def _colsum(v):
    return jnp.sum(v, axis=0, keepdims=True)


def _section(ref, k):
    return ref[:, k * D:(k + 1) * D].astype(F32)


def _adamw_math(w, g, m, v):
    m_new = ADAM_B1 * m + (1.0 - ADAM_B1) * g
    v_new = ADAM_B2 * v + (1.0 - ADAM_B2) * (g * g)
    m_hat = m_new / (1.0 - ADAM_B1 ** ADAM_STEP)
    v_hat = v_new / (1.0 - ADAM_B2 ** ADAM_STEP)
    return -ADAM_LR * (m_hat / (jnp.sqrt(v_hat) + ADAM_EPS) + ADAM_WD * w), m_new, v_new


def _mesh_pos():
    return lax.axis_index("x"), lax.axis_index("y"), lax.axis_index("c")


def _after_spec(after):
    return [] if after is None else [pl.BlockSpec(memory_space=pl.ANY)]


def _after_args(after):
    return [] if after is None else [after]


def _proj_in(xb, win_g, k, l, after=None):
    s = xb.shape[0]
    tm = min(s, 1024)

    def body(x_ref, w_ref, *rest):
        rest[-1][...] = _dot(x_ref[...], w_ref[...].reshape(D, WQ), NN).astype(BF16)

    return pl.pallas_call(
        body, name=f"proj_in_{l}", grid=(NQ, s // tm),
        in_specs=[pl.BlockSpec((tm, D), lambda q, m: (m, 0)),
                  pl.BlockSpec((None, 2, None, D // 2, WQ), lambda q, m: (q, 0, k, 0, 0))] + _after_spec(after),
        out_specs=pl.BlockSpec((tm, WQ), lambda q, m: (m, q)),
        out_shape=jax.ShapeDtypeStruct((s, DZ), BF16),
        compiler_params=_params(("arbitrary", "arbitrary")),
    )(xb, win_g, *_after_args(after))


def _proj_out_ln(ycat, wout_g, k, l, res, res_g, res_b, g, b):
    s = res.shape[0]
    tm = min(s, 512)

    def body(y_ref, w_ref, r_ref, rg_ref, rb_ref, g_ref, b_ref, xb_ref, xh_ref, rs_ref):
        y = _dot(y_ref[...], w_ref[...].reshape(DY, D), NN)
        x = r_ref[...] * rg_ref[...] + rb_ref[...]
        xhat, rstd = _ln_stats(ALPHA * x + y)
        xb_ref[...] = (xhat * g_ref[...] + b_ref[...]).astype(BF16)
        xh_ref[...] = xhat
        rs_ref[...] = rstd

    row = lambda m: (m, 0)
    vec = pl.BlockSpec((1, D), lambda m: (0, 0))
    return pl.pallas_call(
        body, name=f"proj_out_ln_{l}", grid=(s // tm,),
        in_specs=[pl.BlockSpec((tm, DY), row),
                  pl.BlockSpec((NQ, 2, None, RQ // 2, D), lambda m: (0, 0, k, 0, 0)),
                  pl.BlockSpec((tm, D), row), vec, vec, vec, vec],
        out_specs=[pl.BlockSpec((tm, D), row), pl.BlockSpec((tm, D), row), pl.BlockSpec((tm, 1), row)],
        out_shape=[jax.ShapeDtypeStruct((s, D), BF16), jax.ShapeDtypeStruct((s, D), F32),
                   jax.ShapeDtypeStruct((s, 1), F32)],
        compiler_params=_params(("arbitrary",)),
    )(ycat, wout_g, res, res_g, res_b, g, b)


def _loss_ln_bwd(xhat, rstd, g, b, target, l):
    s = xhat.shape[0]
    ts = min(s, 512)

    def body(xh_ref, rs_ref, g_ref, b_ref, t_ref, loss_ref, dr_ref, drb_ref, dg_ref, db_ref):
        @pl.when(pl.program_id(0) == 0)
        def _():
            loss_ref[...] = jnp.zeros_like(loss_ref)
            dg_ref[...] = jnp.zeros_like(dg_ref)
            db_ref[...] = jnp.zeros_like(db_ref)
        xhat_v = xh_ref[...]
        gain = g_ref[...]
        err = (xhat_v * gain + b_ref[...]) - t_ref[...]
        loss_ref[...] += 0.5 * jnp.sum(jnp.mean(err * err, axis=-1, keepdims=True), axis=0, keepdims=True)
        d = err * (1.0 / D)
        dr = _ln_bwd_rows(d * gain, xhat_v, rs_ref[...])
        dr_ref[...] = dr
        drb_ref[...] = dr.astype(BF16)
        dg_ref[...] += _colsum(d * xhat_v)
        db_ref[...] += _colsum(d)

    row = lambda m: (m, 0)
    fixed = lambda m: (0, 0)
    vec = pl.BlockSpec((1, D), fixed)
    return pl.pallas_call(
        body, name=f"loss_ln_bwd_{l}", grid=(s // ts,),
        in_specs=[pl.BlockSpec((ts, D), row), pl.BlockSpec((ts, 1), row), vec, vec, pl.BlockSpec((ts, D), row)],
        out_specs=[pl.BlockSpec((1, 1), fixed), pl.BlockSpec((ts, D), row), pl.BlockSpec((ts, D), row), vec, vec],
        out_shape=[jax.ShapeDtypeStruct((1, 1), F32), jax.ShapeDtypeStruct((s, D), F32),
                   jax.ShapeDtypeStruct((s, D), BF16), jax.ShapeDtypeStruct((1, D), F32),
                   jax.ShapeDtypeStruct((1, D), F32)],
        compiler_params=_params(("arbitrary",)),
    )(xhat, rstd, g, b, target)


def _ln_bwd(dxn, xhat, rstd, g, l, after=None):
    s = dxn.shape[0]
    ts = min(s, 512)

    def body(d_ref, xh_ref, rs_ref, g_ref, *rest):
        dr_ref, drb_ref, dg_ref, db_ref = rest[-4:]

        @pl.when(pl.program_id(0) == 0)
        def _():
            dg_ref[...] = jnp.zeros_like(dg_ref)
            db_ref[...] = jnp.zeros_like(db_ref)
        d = d_ref[...]
        xhat_v = xh_ref[...]
        dr = _ln_bwd_rows(d * g_ref[...], xhat_v, rs_ref[...])
        dr_ref[...] = dr
        drb_ref[...] = dr.astype(BF16)
        dg_ref[...] += _colsum(d * xhat_v)
        db_ref[...] += _colsum(d)

    row = lambda m: (m, 0)
    fixed = lambda m: (0, 0)
    return pl.pallas_call(
        body, name=f"ln_bwd_{l}", grid=(s // ts,),
        in_specs=[pl.BlockSpec((ts, D), row), pl.BlockSpec((ts, D), row), pl.BlockSpec((ts, 1), row),
                  pl.BlockSpec((1, D), fixed)] + _after_spec(after),
        out_specs=[pl.BlockSpec((ts, D), row), pl.BlockSpec((ts, D), row), pl.BlockSpec((1, D), fixed),
                   pl.BlockSpec((1, D), fixed)],
        out_shape=[jax.ShapeDtypeStruct((s, D), F32), jax.ShapeDtypeStruct((s, D), BF16),
                   jax.ShapeDtypeStruct((1, D), F32), jax.ShapeDtypeStruct((1, D), F32)],
        compiler_params=_params(("arbitrary",)),
    )(dxn, xhat, rstd, g, *_after_args(after))


def _dycat(drb, wout_g, k, l):
    s = drb.shape[0]
    tm = min(s, 512)

    def body(d_ref, w_ref, o_ref):
        o_ref[...] = _dot(d_ref[...], w_ref[...].reshape(DY, D), NT).astype(BF16)

    return pl.pallas_call(
        body, name=f"dycat_{l}", grid=(s // tm,),
        in_specs=[pl.BlockSpec((tm, D), lambda m: (m, 0)),
                  pl.BlockSpec((NQ, 2, None, RQ // 2, D), lambda m: (0, 0, k, 0, 0))],
        out_specs=pl.BlockSpec((tm, DY), lambda m: (m, 0)),
        out_shape=jax.ShapeDtypeStruct((s, DY), BF16),
        compiler_params=_params(("arbitrary",)),
    )(drb, wout_g)


def _dwout(ycat, drb, l):
    s = drb.shape[0]
    tk = min(s, 1024)

    def body(y_ref, d_ref, o_ref):
        @pl.when(pl.program_id(0) == 0)
        def _():
            o_ref[...] = jnp.zeros_like(o_ref)

        o_ref[...] += _dot(y_ref[...], d_ref[...], TN)

    return pl.pallas_call(
        body, name=f"dwout_{l}", grid=(s // tk,),
        in_specs=[pl.BlockSpec((tk, DY), lambda k: (k, 0)), pl.BlockSpec((tk, D), lambda k: (k, 0))],
        out_specs=pl.BlockSpec((DY, D), lambda k: (0, 0)),
        out_shape=jax.ShapeDtypeStruct((DY, D), F32),
        compiler_params=_params(("arbitrary",)),
    )(ycat, drb)


def _dwin(xb, dzb, l):
    s = xb.shape[0]
    tk = min(s, 1024)

    def body(x_ref, d_ref, o_ref):
        @pl.when(pl.program_id(1) == 0)
        def _():
            o_ref[...] = jnp.zeros_like(o_ref)

        o_ref[...] += _dot(x_ref[...], d_ref[...], TN)

    return pl.pallas_call(
        body, name=f"dwin_{l}", grid=(NQ, s // tk),
        in_specs=[pl.BlockSpec((tk, D), lambda q, k: (k, 0)), pl.BlockSpec((tk, WQ), lambda q, k: (k, q))],
        out_specs=pl.BlockSpec((None, D, WQ), lambda q, k: (q, 0, 0)),
        out_shape=jax.ShapeDtypeStruct((NQ, D, WQ), F32),
        compiler_params=_params(("arbitrary", "arbitrary")),
    )(xb, dzb)


def _dx(dzb, win_g, k, l, dr, after=None):
    s = dzb.shape[0]
    tm = min(s, 1024)

    def body(d_ref, w_ref, r_ref, *rest):
        o_ref = rest[-1]

        @pl.when(pl.program_id(1) == 0)
        def _():
            o_ref[...] = ALPHA * r_ref[...]

        o_ref[...] += _dot(d_ref[...], w_ref[...].reshape(D, WQ), NT)

    return pl.pallas_call(
        body, name=f"dx_{l}", grid=(s // tm, NQ),
        in_specs=[pl.BlockSpec((tm, WQ), lambda m, q: (m, q)),
                  pl.BlockSpec((None, 2, None, D // 2, WQ), lambda m, q: (q, 0, k, 0, 0)),
                  pl.BlockSpec((tm, D), lambda m, q: (m, 0))] + _after_spec(after),
        out_specs=pl.BlockSpec((tm, D), lambda m, q: (m, 0)),
        out_shape=jax.ShapeDtypeStruct((s, D), F32),
        compiler_params=_params(("arbitrary", "arbitrary")),
    )(dzb, win_g, dr, *_after_args(after))


class _RowShifts:
    def __init__(self, ref, most_rows):
        assert ref.shape[0] == most_rows + 2 * SUBLANES
        self.ref = ref
        ref[...] = jnp.zeros(ref.shape, F32)

    def put(self, block):
        self.rows = block.shape[0]
        self.ref[SUBLANES:SUBLANES + self.rows, :] = block

    def get(self, k, causal):
        start = SUBLANES - k if causal else SUBLANES + k
        return self.ref[start:start + self.rows, :]

    def window_sums(self, block, steps, causal):
        acc = block
        for k in (1, 2, 4, 8)[:steps]:
            self.put(acc)
            acc = acc + self.get(k, causal)
        return acc


def _inv_positions(first_pos, rows):
    t1 = (lax.broadcasted_iota(jnp.int32, (rows, 1), 0) + first_pos + 1).astype(F32)
    return t1, 1.0 / t1


def _pool_inv(positions, cb):
    t1, inv_t1 = positions
    window = float(2 << (cb // 2))
    return jnp.where(t1 < window, inv_t1, 1.0 / window)


def _prev_index(ts, halo):
    return lambda i: (jnp.maximum(i * (ts // halo) - 1, 0), 0)


def _next_index(ts, halo, s):
    return lambda i: (jnp.minimum((i + 1) * (ts // halo), s // halo - 1), 0)


def _even_fwd(z, pool_w, pool_scale, sconv_w, sconv_b, l):
    s = z.shape[0]
    ts = min(s, 256)
    h = POOL_HALO

    def body(z_ref, zp_ref, pw_ref, ps_ref, cw_ref, cb_ref, o_ref, shift_ref, pooled_ref):
        i = pl.program_id(0)
        inside = i > 0
        shifts = _RowShifts(shift_ref, h + ts)
        positions = _inv_positions(i * ts, ts)

        for cb in range(LANE_BLOCKS):
            cols = slice(cb * LANE, (cb + 1) * LANE)

            def section(ref, k):
                return ref[:, k * D + cb * LANE:k * D + (cb + 1) * LANE].astype(F32)

            xa = section(z_ref, 0)
            sums = shifts.window_sums(jnp.concatenate([jnp.where(inside, section(zp_ref, 0), 0.0), xa], axis=0),
                                      cb // 2 + 1, True)
            pooled_ref[:, cols] = (sums[h:] * _pool_inv(positions, cb) - xa).astype(MXU_DTYPE)

            q_prev = jnp.where(inside, section(zp_ref, 4) * section(zp_ref, 2), 0.0)
            q_main = section(z_ref, 4) * section(z_ref, 2)
            shifts.put(jnp.concatenate([q_prev, q_main], axis=0))
            cv = (cw_ref[2:3, cols] * q_main + cw_ref[1:2, cols] * shifts.get(1, True)[h:]
                  + cw_ref[0:1, cols] * shifts.get(2, True)[h:] + cb_ref[:, cols])
            silu_gb, _ = _silu_and_grad(section(z_ref, 5))
            o_ref[:, D + cb * LANE:D + (cb + 1) * LANE] = (section(z_ref, 3) * cv * silu_gb).astype(BF16)

        for g in range(4):
            cols = slice(g * HEAD, (g + 1) * HEAD)
            p = _dot(pooled_ref[:, cols], pw_ref[g], NN)
            silu_ga, _ = _silu_and_grad(z_ref[:, D + g * HEAD:D + (g + 1) * HEAD].astype(F32))
            o_ref[:, cols] = (p * ps_ref[:, cols] * silu_ga).astype(BF16)

    fixed2 = lambda i: (0, 0)
    return pl.pallas_call(
        body, name=f"even_fwd_{l}", grid=(s // ts,),
        in_specs=[pl.BlockSpec((ts, DZ), lambda i: (i, 0)), pl.BlockSpec((h, DZ), _prev_index(ts, h)),
                  pl.BlockSpec((4, HEAD, HEAD), lambda i: (0, 0, 0)), pl.BlockSpec((1, D), fixed2),
                  pl.BlockSpec((SHORT_K, D), fixed2), pl.BlockSpec((1, D), fixed2)],
        out_specs=pl.BlockSpec((ts, DY), lambda i: (i, 0)),
        out_shape=jax.ShapeDtypeStruct((s, DY), BF16),
        scratch_shapes=[pltpu.VMEM((h + ts + 2 * SUBLANES, LANE), F32), pltpu.VMEM((ts, D), MXU_DTYPE)],
        compiler_params=_params(("arbitrary",)),
    )(z, z, pool_w, pool_scale, sconv_w, sconv_b)


def _even_bwd(z, dy, pool_w, pool_wt, pool_scale, sconv_w, sconv_b, l):
    s = z.shape[0]
    ts = min(s, 256)
    h = POOL_HALO
    n_tiles = s // ts

    def body(z_ref, zp_ref, zn_ref, dy_ref, dyn_ref, pw_ref, pwt_ref, ps_ref, cw_ref, cb_ref,
             dz_ref, dpw_ref, dps_ref, dcw_ref, dcb_ref, shift_ref, pooled_ref, p_ref, dp_ref, dpooled_ref):
        i = pl.program_id(0)
        inside = i > 0

        @pl.when(i == 0)
        def _():
            dpw_ref[...] = jnp.zeros_like(dpw_ref)
            dps_ref[...] = jnp.zeros_like(dps_ref)
            dcw_ref[...] = jnp.zeros_like(dcw_ref)
            dcb_ref[...] = jnp.zeros_like(dcb_ref)

        shifts = _RowShifts(shift_ref, ts + h)
        positions = _inv_positions(i * ts, ts + h)
        row = lax.broadcasted_iota(jnp.int32, (ts + h, 1), 0)
        live = jnp.logical_or(i < n_tiles - 1, row < ts)

        def section(ref, k, cb):
            return ref[:, k * D + cb * LANE:k * D + (cb + 1) * LANE].astype(F32)

        def with_next(main_ref, next_ref, k, cb):
            return jnp.concatenate([section(main_ref, k, cb), section(next_ref, k, cb)], axis=0)

        for cb in range(LANE_BLOCKS):
            xa = section(z_ref, 0, cb)
            sums = shifts.window_sums(
                jnp.concatenate([jnp.where(inside, section(zp_ref, 0, cb), 0.0), xa], axis=0), cb // 2 + 1, True)
            pooled_ref[:, cb * LANE:(cb + 1) * LANE] = (
                sums[h:] * _pool_inv(positions, cb)[:ts] - xa).astype(MXU_DTYPE)
        for g in range(4):
            cols = slice(g * HEAD, (g + 1) * HEAD)
            p_ref[:, cols] = _dot(pooled_ref[:, cols], pw_ref[g], NN)
        for cb in range(LANE_BLOCKS):
            cols = slice(cb * LANE, (cb + 1) * LANE)
            silu_ga, dsilu_ga = _silu_and_grad(with_next(z_ref, zn_ref, 1, cb))
            d_ya = with_next(dy_ref, dyn_ref, 0, cb)
            scale = ps_ref[:, cols]
            dp_ref[:, cols] = (d_ya * scale * silu_ga).astype(MXU_DTYPE)
            d_ya_p = d_ya[:ts] * p_ref[:, cols]
            dz_ref[:, D + cb * LANE:D + (cb + 1) * LANE] = (d_ya_p * scale * dsilu_ga[:ts]).astype(BF16)
            dps_ref[:, cols] += _colsum(d_ya_p * silu_ga[:ts])
        for g in range(4):
            cols = slice(g * HEAD, (g + 1) * HEAD)
            dpooled_ref[:, cols] = _dot(dp_ref[:, cols], pwt_ref[g], NN)
            dpw_ref[g] += _dot(pooled_ref[:, cols], dp_ref[0:ts, cols], TN)
        for cb in range(LANE_BLOCKS):
            d_pooled = jnp.where(live, dpooled_ref[:, cb * LANE:(cb + 1) * LANE], 0.0)
            sums = shifts.window_sums(d_pooled * _pool_inv(positions, cb), cb // 2 + 1, False)
            dz_ref[:, cb * LANE:(cb + 1) * LANE] = (sums[:ts] - d_pooled[:ts]).astype(BF16)

        for cb in range(LANE_BLOCKS):
            cols = slice(cb * LANE, (cb + 1) * LANE)
            cg = section(z_ref, 4, cb)
            hh = section(z_ref, 2, cb)
            bg = section(z_ref, 3, cb)
            q_main = cg * hh
            q_prev = jnp.where(inside, section(zp_ref, 4, cb) * section(zp_ref, 2, cb), 0.0)
            shifts.put(jnp.concatenate([q_prev, q_main], axis=0))
            q_1 = shifts.get(1, True)[h:]
            q_2 = shifts.get(2, True)[h:]
            w0, w1, w2 = cw_ref[0:1, cols], cw_ref[1:2, cols], cw_ref[2:3, cols]
            cv = w2 * q_main + w1 * q_1 + w0 * q_2 + cb_ref[:, cols]
            silu_gb, dsilu_gb = _silu_and_grad(with_next(z_ref, zn_ref, 5, cb))
            d_yb = with_next(dy_ref, dyn_ref, 1, cb)
            d_cv = jnp.where(live, d_yb * with_next(z_ref, zn_ref, 3, cb) * silu_gb, 0.0)
            d_cv0 = d_cv[:ts]
            shifts.put(d_cv)
            d_q = w2 * d_cv0 + w1 * shifts.get(1, False)[:ts] + w0 * shifts.get(2, False)[:ts]
            d_yb_cv = d_yb[:ts] * cv

            def store(k, val):
                dz_ref[:, k * D + cb * LANE:k * D + (cb + 1) * LANE] = val.astype(BF16)

            store(2, d_q * cg)
            store(3, d_yb_cv * silu_gb[:ts])
            store(4, d_q * hh)
            store(5, d_yb_cv * bg * dsilu_gb[:ts])
            dcb_ref[:, cols] += _colsum(d_cv0)
            dcw_ref[2:3, cols] += _colsum(d_cv0 * q_main)
            dcw_ref[1:2, cols] += _colsum(d_cv0 * q_1)
            dcw_ref[0:1, cols] += _colsum(d_cv0 * q_2)

    fixed2 = lambda i: (0, 0)
    fixed3 = lambda i: (0, 0, 0)
    return pl.pallas_call(
        body, name=f"even_bwd_{l}", grid=(n_tiles,),
        in_specs=[pl.BlockSpec((ts, DZ), lambda i: (i, 0)), pl.BlockSpec((h, DZ), _prev_index(ts, h)),
                  pl.BlockSpec((h, DZ), _next_index(ts, h, s)),
                  pl.BlockSpec((ts, DY), lambda i: (i, 0)), pl.BlockSpec((h, DY), _next_index(ts, h, s)),
                  pl.BlockSpec((4, HEAD, HEAD), fixed3), pl.BlockSpec((4, HEAD, HEAD), fixed3),
                  pl.BlockSpec((1, D), fixed2), pl.BlockSpec((SHORT_K, D), fixed2), pl.BlockSpec((1, D), fixed2)],
        out_specs=[pl.BlockSpec((ts, DZ), lambda i: (i, 0)), pl.BlockSpec((4, HEAD, HEAD), fixed3),
                   pl.BlockSpec((1, D), fixed2), pl.BlockSpec((SHORT_K, D), fixed2), pl.BlockSpec((1, D), fixed2)],
        out_shape=[jax.ShapeDtypeStruct((s, DZ), BF16), jax.ShapeDtypeStruct((4, HEAD, HEAD), F32),
                   jax.ShapeDtypeStruct((1, D), F32), jax.ShapeDtypeStruct((SHORT_K, D), F32),
                   jax.ShapeDtypeStruct((1, D), F32)],
        scratch_shapes=[pltpu.VMEM((ts + h + 2 * SUBLANES, LANE), F32), pltpu.VMEM((ts, D), MXU_DTYPE),
                        pltpu.VMEM((ts, D), F32), pltpu.VMEM((ts + h, D), MXU_DTYPE), pltpu.VMEM((ts + h, D), F32)],
        compiler_params=_params(("arbitrary",), 56),
    )(z, z, z, dy, dy, pool_w, pool_wt, pool_scale, sconv_w, sconv_b)


def _to_blocks(ref, r0, val):
    n = val.shape[0]
    for cb in range(LANE_BLOCKS):
        ref[cb, r0:r0 + n, :] = val[:, cb * LANE:(cb + 1) * LANE]


def _from_blocks(ref):
    return jnp.concatenate([ref[cb] for cb in range(LANE_BLOCKS)], axis=1)


def _shift_copies(src_ref, sh_ref, n, causal):
    def block(cb, carry):
        for b in range(1, SUBLANES):
            if causal:
                sh_ref[cb, b - 1, SUBLANES:n, :] = src_ref[cb, SUBLANES - b:n - b, :]
            else:
                sh_ref[cb, b - 1, 0:n - SUBLANES, :] = src_ref[cb, b:n - SUBLANES + b, :]
        return carry

    lax.fori_loop(0, LANE_BLOCKS, block, 0)


def _tap(src_ref, sh_ref, cb, first, n, d, causal):
    whole, b = (d // SUBLANES) * SUBLANES, d % SUBLANES
    start = first - whole if causal else first + whole
    if b == 0:
        return src_ref[cb, start:start + n, :]
    return sh_ref[cb, b - 1, start:start + n, :]


def _chunk_rows(rows, most):
    return max(n for n in range(CONV_ROWS, most + 1, CONV_ROWS) if rows % n == 0)


def _conv31(src_ref, sh_ref, w_ref, dst_ref, base, rows, causal):
    n = _chunk_rows(rows, 4 * CONV_ROWS)

    def block(cb, carry):
        for r0 in range(0, rows, n):
            acc = None
            for d in range(CONV_K):
                term = w_ref[cb, CONV_K - 1 - d:CONV_K - d, :] * _tap(src_ref, sh_ref, cb, base + r0, n, d, causal)
                acc = term if acc is None else acc + term
            dst_ref[cb, r0:r0 + n, :] = acc
        return carry

    lax.fori_loop(0, LANE_BLOCKS, block, 0)


def _conv31_tap_grads(d_ref, src_ref, sh_ref, dw_ref, base, rows):
    n = _chunk_rows(rows, 2 * CONV_ROWS)

    def block(cb, carry):
        sums = [None] * CONV_K
        for r0 in range(0, rows, n):
            d_blk = d_ref[cb, r0:r0 + n, :]
            for d in range(CONV_K):
                prod = d_blk * _tap(src_ref, sh_ref, cb, base + r0, n, d, True)
                part = prod[0:SUBLANES]
                for k in range(1, n // SUBLANES):
                    part = part + prod[k * SUBLANES:(k + 1) * SUBLANES]
                sums[d] = part if sums[d] is None else sums[d] + part
        for d in range(CONV_K):
            j = CONV_K - 1 - d
            dw_ref[cb, j:j + 1, :] += _colsum(sums[d])
        return carry

    lax.fori_loop(0, LANE_BLOCKS, block, 0)


def _sgu_bias_rows(sgu_b):
    return jnp.repeat(jnp.transpose(sgu_b), HEAD, axis=1)


def _odd_fwd(z, sln_g, sln_b, ws, sbias, dconv_w, dconv_b, dn_g, dn_b, l):
    s = z.shape[0]
    ts = min(s, 256)
    h = CONV_HALO

    def body(z_ref, zp_ref, lg_ref, lb_ref, ws_ref, sb_ref, cw_ref, cb_ref, ng_ref, nb_ref, o_ref, zh_ref, rz_ref,
             zz_ref, zc_ref, sh_ref):
        i = pl.program_id(0)
        vhat, _ = _ln_stats(_section(z_ref, 1))
        vn = (vhat * lg_ref[...] + lb_ref[...]).astype(MXU_DTYPE)
        silu_gc, _ = _silu_and_grad(_section(z_ref, 2))
        for n in range(ts // SGU_BLOCK):
            rows = slice(n * SGU_BLOCK, (n + 1) * SGU_BLOCK)
            sv = jnp.concatenate(
                [_dot(ws_ref[hd], vn[rows, hd * HEAD:(hd + 1) * HEAD], NN) for hd in range(4)], axis=1)
            sv = sv + sb_ref[...]
            o_ref[rows, 0:D] = (z_ref[rows, 0:D].astype(F32) * sv * silu_gc[rows]).astype(BF16)

        _to_blocks(zz_ref, 0, jnp.where(i > 0, _section(zp_ref, 3) * _sigmoid(_section(zp_ref, 4)), 0.0))
        _to_blocks(zz_ref, h, _section(z_ref, 3) * _sigmoid(_section(z_ref, 4)))
        _shift_copies(zz_ref, sh_ref, h + ts, True)
        _conv31(zz_ref, sh_ref, cw_ref, zc_ref, h, ts, True)
        zhat, rstd_z = _ln_stats(_from_blocks(zc_ref) + cb_ref[...])
        zh_ref[...] = zhat
        rz_ref[...] = rstd_z
        silu_zn, _ = _silu_and_grad(zhat * ng_ref[...] + nb_ref[...])
        silu_gd, _ = _silu_and_grad(_section(z_ref, 5))
        o_ref[:, D:2 * D] = (silu_zn * silu_gd).astype(BF16)

    fixed2 = lambda i: (0, 0)
    vec = pl.BlockSpec((1, D), fixed2)
    return pl.pallas_call(
        body, name=f"odd_fwd_{l}", grid=(s // ts,),
        in_specs=[pl.BlockSpec((ts, DZ), lambda i: (i, 0)), pl.BlockSpec((h, DZ), _prev_index(ts, h)),
                  vec, vec, pl.BlockSpec((4, SGU_BLOCK, SGU_BLOCK), lambda i: (0, 0, 0)),
                  pl.BlockSpec((SGU_BLOCK, D), fixed2), pl.BlockSpec((LANE_BLOCKS, CONV_K, LANE), lambda i: (0, 0, 0)),
                  vec, vec, vec],
        out_specs=[pl.BlockSpec((ts, DY), lambda i: (i, 0)), pl.BlockSpec((ts, D), lambda i: (i, 0)),
                   pl.BlockSpec((ts, 1), lambda i: (i, 0))],
        out_shape=[jax.ShapeDtypeStruct((s, DY), BF16), jax.ShapeDtypeStruct((s, D), F32),
                   jax.ShapeDtypeStruct((s, 1), F32)],
        scratch_shapes=[pltpu.VMEM((LANE_BLOCKS, h + ts, LANE), F32), pltpu.VMEM((LANE_BLOCKS, ts, LANE), F32),
                        pltpu.VMEM((LANE_BLOCKS, SUBLANES - 1, h + ts, LANE), F32)],
        compiler_params=_params(("arbitrary",)),
    )(z, z, sln_g, sln_b, ws, sbias, dconv_w, dconv_b, dn_g, dn_b)


def _odd_bwd(z, dy, zhat_s, rstd_s, sln_g, sln_b, ws, wst, sbias, dconv_w, dn_g, dn_b, l):
    s = z.shape[0]
    ts = min(s, 256)
    h = CONV_HALO
    n_tiles = s // ts
    te = ts + h

    def body(z_ref, zp_ref, zn_ref, dy_ref, dyn_ref, zh_ref, zhn_ref, rz_ref, rzn_ref, lg_ref, lb_ref, ws_ref,
             wst_ref, sb_ref, cw_ref, ng_ref, nb_ref, dz_ref, dlg_ref, dlb_ref, dws_ref, dsb_ref, dcw_ref, dcb_ref,
             dng_ref, dnb_ref, zz_ref, dzc_ref, dzz_ref, dsb_acc, sh_ref):
        i = pl.program_id(0)
        more = i < n_tiles - 1

        @pl.when(i == 0)
        def _():
            for ref in (dlg_ref, dlb_ref, dws_ref, dsb_ref, dcw_ref, dcb_ref, dng_ref, dnb_ref, dsb_acc):
                ref[...] = jnp.zeros_like(ref)

        vhat, rstd_v = _ln_stats(_section(z_ref, 1))
        lg = lg_ref[...]
        vn = (vhat * lg + lb_ref[...]).astype(MXU_DTYPE)
        u = _section(z_ref, 0)
        silu_gc, dsilu_gc = _silu_and_grad(_section(z_ref, 2))
        d_yc = _section(dy_ref, 0)
        d_yc_u = d_yc * u
        d_sv = d_yc_u * silu_gc
        d_svb = d_sv.astype(MXU_DTYPE)
        sv_rows = []
        dvn_rows = []
        dsb = None
        for n in range(ts // SGU_BLOCK):
            rows = slice(n * SGU_BLOCK, (n + 1) * SGU_BLOCK)
            sv_parts = []
            dvn_parts = []
            for hd in range(4):
                cols = slice(hd * HEAD, (hd + 1) * HEAD)
                sv_parts.append(_dot(ws_ref[hd], vn[rows, cols], NN))
                dvn_parts.append(_dot(wst_ref[hd], d_svb[rows, cols], NN))
                dws_ref[hd] += _dot(d_svb[rows, cols], vn[rows, cols], NT)
            sv_rows.append(jnp.concatenate(sv_parts, axis=1) + sb_ref[...])
            dvn_rows.append(jnp.concatenate(dvn_parts, axis=1))
            dsb = d_sv[rows] if dsb is None else dsb + d_sv[rows]
        dsb_acc[...] += dsb

        @pl.when(i == n_tiles - 1)
        def _():
            for hd in range(4):
                blk = dsb_acc[:, hd * HEAD:(hd + 1) * HEAD]
                folded = blk[:, 0:LANE] + blk[:, LANE:HEAD]
                dsb_ref[hd:hd + 1, :] = _colsum(jnp.transpose(folded))
        sv = jnp.concatenate(sv_rows, axis=0)
        d_vn = jnp.concatenate(dvn_rows, axis=0)
        dz_ref[:, 0:D] = (d_yc * sv * silu_gc).astype(BF16)
        dz_ref[:, D:2 * D] = _ln_bwd_rows(d_vn * lg, vhat, rstd_v).astype(BF16)
        dz_ref[:, 2 * D:3 * D] = (d_yc_u * sv * dsilu_gc).astype(BF16)
        dlg_ref[...] += _colsum(d_vn * vhat)
        dlb_ref[...] += _colsum(d_vn)

        def gate(ref):
            return _section(ref, 3) * _sigmoid(_section(ref, 4))

        _to_blocks(zz_ref, 0, jnp.where(i > 0, gate(zp_ref), 0.0))
        _to_blocks(zz_ref, h, gate(z_ref))
        _shift_copies(zz_ref, sh_ref, h + ts, True)
        zhat = jnp.concatenate([zh_ref[...], zhn_ref[...]], axis=0)
        rstd_z = jnp.concatenate([rz_ref[...], rzn_ref[...]], axis=0)
        ng = ng_ref[...]
        silu_zn, dsilu_zn = _silu_and_grad(zhat * ng + nb_ref[...])
        gd = jnp.concatenate([_section(z_ref, 5), _section(zn_ref, 5)], axis=0)
        silu_gd, dsilu_gd = _silu_and_grad(gd)
        d_yd = jnp.concatenate([_section(dy_ref, 1), _section(dyn_ref, 1)], axis=0)
        d_zn = d_yd * silu_gd * dsilu_zn
        d_zc = _ln_bwd_rows(d_zn * ng, zhat, rstd_z)
        row = lax.broadcasted_iota(jnp.int32, (te, 1), 0)
        d_zc = jnp.where(jnp.logical_or(more, row < ts), d_zc, 0.0)
        _to_blocks(dzc_ref, 0, d_zc)
        dz_ref[:, 5 * D:6 * D] = (d_yd[:ts] * silu_zn[:ts] * dsilu_gd[:ts]).astype(BF16)
        dng_ref[...] += _colsum(d_zn[:ts] * zhat[:ts])
        dnb_ref[...] += _colsum(d_zn[:ts])
        dcb_ref[...] += _colsum(d_zc[:ts])
        _conv31_tap_grads(dzc_ref, zz_ref, sh_ref, dcw_ref, h, ts)
        _shift_copies(dzc_ref, sh_ref, te, False)
        _conv31(dzc_ref, sh_ref, cw_ref, dzz_ref, 0, ts, False)
        d_zz = _from_blocks(dzz_ref)
        a = _section(z_ref, 3)
        sig_b = _sigmoid(_section(z_ref, 4))
        dz_ref[:, 3 * D:4 * D] = (d_zz * sig_b).astype(BF16)
        dz_ref[:, 4 * D:5 * D] = (d_zz * a * sig_b * (1.0 - sig_b)).astype(BF16)

    fixed2 = lambda i: (0, 0)
    fixed3 = lambda i: (0, 0, 0)
    vec = pl.BlockSpec((1, D), fixed2)
    mat = pl.BlockSpec((4, SGU_BLOCK, SGU_BLOCK), fixed3)
    vec_shape = jax.ShapeDtypeStruct((1, D), F32)
    return pl.pallas_call(
        body, name=f"odd_bwd_{l}", grid=(n_tiles,),
        in_specs=[pl.BlockSpec((ts, DZ), lambda i: (i, 0)), pl.BlockSpec((h, DZ), _prev_index(ts, h)),
                  pl.BlockSpec((h, DZ), _next_index(ts, h, s)),
                  pl.BlockSpec((ts, DY), lambda i: (i, 0)), pl.BlockSpec((h, DY), _next_index(ts, h, s)),
                  pl.BlockSpec((ts, D), lambda i: (i, 0)), pl.BlockSpec((h, D), _next_index(ts, h, s)),
                  pl.BlockSpec((ts, 1), lambda i: (i, 0)), pl.BlockSpec((h, 1), _next_index(ts, h, s)),
                  vec, vec, mat, mat, pl.BlockSpec((SGU_BLOCK, D), fixed2),
                  pl.BlockSpec((LANE_BLOCKS, CONV_K, LANE), fixed3), vec, vec],
        out_specs=[pl.BlockSpec((ts, DZ), lambda i: (i, 0)), vec, vec, mat, pl.BlockSpec((4, SGU_BLOCK), fixed2),
                   pl.BlockSpec((LANE_BLOCKS, CONV_K, LANE), fixed3), vec, vec, vec],
        out_shape=[jax.ShapeDtypeStruct((s, DZ), BF16), vec_shape, vec_shape,
                   jax.ShapeDtypeStruct((4, SGU_BLOCK, SGU_BLOCK), F32), jax.ShapeDtypeStruct((4, SGU_BLOCK), F32),
                   jax.ShapeDtypeStruct((LANE_BLOCKS, CONV_K, LANE), F32), vec_shape, vec_shape, vec_shape],
        scratch_shapes=[pltpu.VMEM((LANE_BLOCKS, h + ts, LANE), F32), pltpu.VMEM((LANE_BLOCKS, te, LANE), F32),
                        pltpu.VMEM((LANE_BLOCKS, ts, LANE), F32), pltpu.VMEM((SGU_BLOCK, D), F32),
                        pltpu.VMEM((LANE_BLOCKS, SUBLANES - 1, te, LANE), F32)],
        compiler_params=_params(("arbitrary",), 60),
    )(z, z, z, dy, dy, zhat_s, zhat_s, rstd_s, rstd_s, sln_g, sln_b, ws, wst, sbias, dconv_w, dn_g, dn_b)


def _remote(src, dst, send_sems, recv_sems, k, to):
    return pltpu.make_async_remote_copy(src_ref=src, dst_ref=dst, send_sem=send_sems.at[k],
                                        recv_sem=recv_sems.at[k], device_id=to, device_id_type=MESH)


def _other_chips(x, y):
    return [(1 - x, y, 2 * (1 - x) + y), (x, 1 - y, 2 * x + 1 - y), (1 - x, 1 - y, 2 * (1 - x) + 1 - y)]


def _cast_own(w_stack, pos_arr, name):
    slots, rows, cols = w_stack.shape
    half = rows // 2

    def body(pos_ref, w_ref, o_ref):
        o_ref[...] = w_ref[...].astype(BF16)

    grid_spec = pltpu.PrefetchScalarGridSpec(
        num_scalar_prefetch=1, grid=(slots, 2),
        in_specs=[pl.BlockSpec((None, half, cols), lambda s, h, pos: (s, h, 0))],
        out_specs=pl.BlockSpec((None, None, None, half, cols), lambda s, h, pos: (pos[1], h, s, 0, 0)))
    return pl.pallas_call(
        body, name=name, grid_spec=grid_spec, out_shape=jax.ShapeDtypeStruct((NQ, 2, slots, half, cols), BF16),
        compiler_params=_params(("arbitrary",) * 2),
    )(pos_arr, w_stack)


def _gather_weights(win_g, wout_g, small_sh):
    def body(win_in, wout_in, small, win_g, wout_g, small_g, send_sems, recv_sems):
        del win_in, wout_in
        x, y, c = _mesh_pos()
        me = 2 * x + y
        sibling = (x, y, 1 - c)
        chips = _other_chips(x, y)

        sends = []
        for j, (cx, cy, _) in enumerate(chips):
            to = (cx, cy, c)
            sends.append(_remote(win_g.at[me, c], win_g.at[me, c], send_sems, recv_sems, j, to))
            sends.append(_remote(wout_g.at[me, c], wout_g.at[me, c], send_sems, recv_sems, 3 + j, to))
            sends.append(_remote(small, small_g.at[me], send_sems, recv_sems, 6 + j, to))
        for cp in sends:
            cp.start()
        passed = []
        for j, (_, _, q) in enumerate(chips):
            got_in = win_g.at[q, c]
            got_out = wout_g.at[q, c]
            _remote(got_in, got_in, send_sems, recv_sems, j, sibling).wait_recv()
            cp = _remote(got_in, got_in, send_sems, recv_sems, 9 + j, sibling)
            cp.start()
            passed.append(cp)
            _remote(got_out, got_out, send_sems, recv_sems, 3 + j, sibling).wait_recv()
            cp = _remote(got_out, got_out, send_sems, recv_sems, 12 + j, sibling)
            cp.start()
            passed.append(cp)
            _remote(small, small_g.at[q], send_sems, recv_sems, 6 + j, sibling).wait_recv()
        for j, (_, _, q) in enumerate(chips):
            from_in = win_g.at[q, 1 - c]
            from_out = wout_g.at[q, 1 - c]
            _remote(from_in, from_in, send_sems, recv_sems, 9 + j, sibling).wait_recv()
            _remote(from_out, from_out, send_sems, recv_sems, 12 + j, sibling).wait_recv()
        for cp in sends + passed:
            cp.wait_send()

    return pl.pallas_call(
        body, name="gather_weights",
        in_specs=[HBM_SPEC, HBM_SPEC, HBM_SPEC], out_specs=[HBM_SPEC, HBM_SPEC, HBM_SPEC],
        out_shape=[jax.ShapeDtypeStruct(win_g.shape, win_g.dtype), jax.ShapeDtypeStruct(wout_g.shape, wout_g.dtype),
                   jax.ShapeDtypeStruct((NQ,) + small_sh.shape, small_sh.dtype)],
        input_output_aliases={0: 0, 1: 1},
        scratch_shapes=[pltpu.SemaphoreType.DMA((15,)), pltpu.SemaphoreType.DMA((15,))],
    )(win_g, wout_g, small_sh)


def _hbm(a):
    return pltpu.with_memory_space_constraint(a, pltpu.HBM)


def _split_start(body, name, sources, landings):
    n_src, n_land = len(sources), len(landings)
    n_buf = n_src + n_land

    def kernel_body(*refs):
        ins, outs = refs[:n_buf], refs[n_buf:]
        send_sems, recv_sems, token = outs[0], outs[1], outs[2 + n_buf]
        body(ins[:n_src], ins[n_src:], send_sems, recv_sems)
        token[...] = jnp.zeros_like(token)

    bufs = [_hbm(a) for a in sources] + [
        _hbm(lax.empty(s.shape, s.dtype) if isinstance(s, jax.ShapeDtypeStruct) else s) for s in landings]
    n_sem = getattr(body, "n_copies")
    out = pl.pallas_call(
        kernel_body, name=name,
        out_shape=(pltpu.SemaphoreType.DMA((n_sem,)), pltpu.SemaphoreType.DMA((n_sem,)),
                   *[pltpu.HBM(b.shape, b.dtype) for b in bufs], jax.ShapeDtypeStruct((8, LANE), F32)),
        in_specs=(HBM_SPEC,) * n_buf,
        out_specs=(SEM_SPEC, SEM_SPEC, *([HBM_SPEC] * n_buf), pl.BlockSpec(memory_space=pltpu.VMEM)),
        input_output_aliases={k: 2 + k for k in range(n_buf)},
        compiler_params=pltpu.CompilerParams(has_side_effects=SIDE_EFFECT),
    )(*bufs)
    return out[0], out[1], list(out[2:2 + n_src]), list(out[2 + n_src:2 + n_buf]), out[2 + n_buf]


def _split_wait(body, name, send_sems, recv_sems, sources, landings, after):
    n_src, n_land = len(sources), len(landings)
    n_buf = n_src + n_land

    def kernel_body(*refs):
        ins = refs[:n_buf]
        body(ins[:n_src], ins[n_src:], refs[n_buf], refs[n_buf + 1])

    bufs = list(sources) + list(landings)
    out = pl.pallas_call(
        kernel_body, name=name,
        out_shape=tuple(pltpu.HBM(b.shape, b.dtype) for b in bufs),
        in_specs=(*([HBM_SPEC] * n_buf), SEM_SPEC, SEM_SPEC, pl.BlockSpec(memory_space=pl.ANY)),
        out_specs=(HBM_SPEC,) * n_buf,
        input_output_aliases={k: k for k in range(n_buf)},
        compiler_params=pltpu.CompilerParams(has_side_effects=SIDE_EFFECT),
    )(*bufs, send_sems, recv_sems, after)
    return list(out[:n_src]), list(out[n_src:])


def _gather_rest_copies(start):
    def body(srcs, lands, send_sems, recv_sems):
        del srcs
        x, y, c = _mesh_pos()
        me = 2 * x + y
        for j, (cx, cy, q) in enumerate(_other_chips(x, y)):
            to = (cx, cy, c)
            for k, gathered in enumerate(lands):
                if start:
                    _remote(gathered.at[me, c], gathered.at[me, c], send_sems, recv_sems, 3 * k + j, to).start()
                else:
                    cp = _remote(gathered.at[me, c], gathered.at[q, c], send_sems, recv_sems, 3 * k + j, to)
                    cp.wait_send()
                    cp.wait_recv()

    body.n_copies = 6
    return body


def _gather_rest_forward(win_g, wout_g):
    def body(win_in, wout_in, win_g, wout_g, send_sems, recv_sems):
        del win_in, wout_in
        x, y, c = _mesh_pos()
        sibling = (x, y, 1 - c)
        passed = []
        for j, (_, _, q) in enumerate(_other_chips(x, y)):
            got_in = win_g.at[q, c]
            got_out = wout_g.at[q, c]
            passed.append(_remote(got_in, got_in, send_sems, recv_sems, j, sibling))
            passed.append(_remote(got_out, got_out, send_sems, recv_sems, 3 + j, sibling))
        for cp in passed:
            cp.start()
        for j, (_, _, q) in enumerate(_other_chips(x, y)):
            from_in = win_g.at[q, 1 - c]
            from_out = wout_g.at[q, 1 - c]
            _remote(from_in, from_in, send_sems, recv_sems, j, sibling).wait_recv()
            _remote(from_out, from_out, send_sems, recv_sems, 3 + j, sibling).wait_recv()
        for cp in passed:
            cp.wait_send()

    return pl.pallas_call(
        body, name="gather_rest_forward",
        in_specs=[HBM_SPEC] * 2, out_specs=[HBM_SPEC] * 2,
        out_shape=[jax.ShapeDtypeStruct(win_g.shape, win_g.dtype), jax.ShapeDtypeStruct(wout_g.shape, wout_g.dtype)],
        input_output_aliases={0: 0, 1: 1},
        scratch_shapes=[pltpu.SemaphoreType.DMA((6,)), pltpu.SemaphoreType.DMA((6,))],
    )(win_g, wout_g)


def _allreduce_small(groups, after):
    pieces = [p for _, _, _, members in groups for _, p in members]
    n_in, n_g = len(pieces), len(groups)

    def body(*refs):
        ins = refs[:n_in]
        outs = refs[-(2 * n_g + 2):-(n_g + 2)]
        alls = refs[-(n_g + 2):-2]
        send_sems, recv_sems = refs[-2:]
        x, y, c = _mesh_pos()
        me = 4 * x + 2 * y + c
        sibling = (x, y, 1 - c)
        chips = _other_chips(x, y)

        k = 0
        for (rows, cols, dtype, members), all_ref in zip(groups, alls):
            if any(piece.shape[1] < cols for _, piece in members) or sum(p.shape[0] for _, p in members) < rows:
                all_ref[me] = jnp.zeros((rows, cols), dtype)
            for first, piece in members:
                n, width = piece.shape
                all_ref[me, first:first + n, 0:width] = ins[k][...].astype(dtype)
                k += 1

        sends, passed = [], []
        for g, all_ref in enumerate(alls):
            sends.append(_remote(all_ref.at[me], all_ref.at[me], send_sems, recv_sems, 7 * g, sibling))
            for j, (cx, cy, _) in enumerate(chips):
                sends.append(_remote(all_ref.at[me], all_ref.at[me], send_sems, recv_sems, 7 * g + 1 + j, (cx, cy, c)))
        for cp in sends:
            cp.start()
        for j, (cx, cy, _) in enumerate(chips):
            for g, all_ref in enumerate(alls):
                got = all_ref.at[4 * cx + 2 * cy + c]
                _remote(got, got, send_sems, recv_sems, 7 * g + 1 + j, sibling).wait_recv()
                cp = _remote(got, got, send_sems, recv_sems, 7 * g + 4 + j, sibling)
                cp.start()
                passed.append(cp)
        for g, all_ref in enumerate(alls):
            got = all_ref.at[4 * x + 2 * y + 1 - c]
            _remote(got, got, send_sems, recv_sems, 7 * g, sibling).wait_recv()
            for j, (cx, cy, _) in enumerate(chips):
                got = all_ref.at[4 * cx + 2 * cy + 1 - c]
                _remote(got, got, send_sems, recv_sems, 7 * g + 4 + j, sibling).wait_recv()
        for cp in sends + passed:
            cp.wait_send()
        for o_ref, all_ref in zip(outs, alls):
            total = all_ref[0].astype(F32)
            for dev in range(1, 8):
                total = total + all_ref[dev].astype(F32)
            o_ref[...] = total

    vmem = pl.BlockSpec(memory_space=pltpu.VMEM)
    return pl.pallas_call(
        body, name="allreduce_small",
        in_specs=[vmem] * n_in + _after_spec(after),
        out_specs=[vmem] * n_g,
        out_shape=[jax.ShapeDtypeStruct((rows, cols), F32) for rows, cols, _, _ in groups],
        scratch_shapes=[pltpu.VMEM((8, rows, cols), dtype) for rows, cols, dtype, _ in groups]
        + [pltpu.SemaphoreType.DMA((7 * n_g,)), pltpu.SemaphoreType.DMA((7 * n_g,))],
        compiler_params=_params(None, 56),
    )(*pieces, *_after_args(after))


def _pair_copies(start):
    def body(srcs, lands, send_sems, recv_sems):
        x, y, c = _mesh_pos()
        sibling = (x, y, 1 - c)
        for k, (g_ref, r_ref) in enumerate(zip(srcs, lands)):
            half = g_ref.shape[1] // 2
            cp = _remote(g_ref.at[:, pl.ds((1 - c) * half, half), :], r_ref, send_sems, recv_sems, k, sibling)
            if start:
                cp.start()
            else:
                cp.wait_send()
                cp.wait_recv()

    body.n_copies = 2
    return body


def _pair_sum(g, r, pos_arr, name):
    nq, rows, cols = r.shape
    tr = min(rows, 256)
    nt = rows // tr

    def body(pos_ref, g_ref, r_ref, ob_ref, own_ref):
        total = g_ref[...] + r_ref[...]
        ob_ref[...] = total.astype(BF16)

        @pl.when(pl.program_id(1) == pos_ref[1])
        def _():
            own_ref[...] = total

    blk = (None, tr, cols)
    grid_spec = pltpu.PrefetchScalarGridSpec(
        num_scalar_prefetch=1, grid=(nt, nq),
        in_specs=[pl.BlockSpec(blk, lambda t, q, pos: (q, pos[0] * nt + t, 0)),
                  pl.BlockSpec(blk, lambda t, q, pos: (q, t, 0))],
        out_specs=[pl.BlockSpec(blk, lambda t, q, pos: (q, t, 0)),
                   pl.BlockSpec((tr, cols), lambda t, q, pos: (t, 0))])
    return pl.pallas_call(
        body, name=name, grid_spec=grid_spec,
        out_shape=[jax.ShapeDtypeStruct(r.shape, BF16), jax.ShapeDtypeStruct((rows, cols), F32)],
        compiler_params=_params(("arbitrary",) * 2),
    )(pos_arr, g, r)


def _chip_copies(start):
    def body(srcs, lands, send_sems, recv_sems):
        pin, pout = srcs
        rin, rout = lands
        x, y, c = _mesh_pos()
        for j, (cx, cy, q) in enumerate(_other_chips(x, y)):
            to = (cx, cy, c)
            for k, (src, dst) in enumerate(((pin, rin), (pout, rout))):
                cp = _remote(src.at[q], dst.at[j], send_sems, recv_sems, 3 * k + j, to)
                if start:
                    cp.start()
                else:
                    cp.wait_send()
                    cp.wait_recv()

    body.n_copies = 6
    return body


def _chip_sum(own, r, pos_arr, name):
    rows, cols = own.shape
    tr = min(rows, 256)

    def body(pos_ref, p_ref, r0_ref, r1_ref, r2_ref, o_ref):
        o_ref[...] = ((p_ref[...] + r0_ref[...].astype(F32)) + r1_ref[...].astype(F32)) + r2_ref[...].astype(F32)

    def peer(j):
        return pl.BlockSpec((None, tr, cols), lambda t, pos: (j, t, 0))

    grid_spec = pltpu.PrefetchScalarGridSpec(
        num_scalar_prefetch=1, grid=(rows // tr,),
        in_specs=[pl.BlockSpec((tr, cols), lambda t, pos: (t, 0)), peer(0), peer(1), peer(2)],
        out_specs=pl.BlockSpec((None, tr, cols), lambda t, pos: (pos[0], t, 0)))
    return pl.pallas_call(
        body, name=name, grid_spec=grid_spec, out_shape=jax.ShapeDtypeStruct((2, rows, cols), F32),
        compiler_params=_params(("arbitrary",)),
    )(pos_arr, own, r, r, r)


def _pair_share(gin, gout, l):
    def body(gin_in, gout_in, gin_ref, gout_ref, send_sems, recv_sems):
        del gin_in, gout_in
        x, y, c = _mesh_pos()
        sibling = (x, y, 1 - c)
        sends = [_remote(gin_ref.at[c], gin_ref.at[c], send_sems, recv_sems, 0, sibling),
                 _remote(gout_ref.at[c], gout_ref.at[c], send_sems, recv_sems, 1, sibling)]
        for cp in sends:
            cp.start()
        _remote(gin_ref.at[1 - c], gin_ref.at[1 - c], send_sems, recv_sems, 0, sibling).wait_recv()
        _remote(gout_ref.at[1 - c], gout_ref.at[1 - c], send_sems, recv_sems, 1, sibling).wait_recv()
        for cp in sends:
            cp.wait_send()

    return pl.pallas_call(
        body, name=f"pair_share_{l}",
        in_specs=[HBM_SPEC, HBM_SPEC], out_specs=[HBM_SPEC, HBM_SPEC],
        out_shape=[jax.ShapeDtypeStruct(gin.shape, F32), jax.ShapeDtypeStruct(gout.shape, F32)],
        input_output_aliases={0: 0, 1: 1},
        scratch_shapes=[pltpu.SemaphoreType.DMA((2,)), pltpu.SemaphoreType.DMA((2,))],
    )(gin, gout)


def _adamw_large(w, m, v, g, i, prev, name):
    _, rows, cols = w.shape
    half = rows // 2
    tr = min(half, 256)
    nt = half // tr

    def body(w_ref, m_ref, v_ref, g_ref, *rest):
        go_ref, d_ref, mo_ref, vo_ref = rest[-4:]
        gv = g_ref[...]
        go_ref[...] = gv
        d_ref[...], mo_ref[...], vo_ref[...] = _adamw_math(w_ref[...], gv, m_ref[...], v_ref[...])

    full = pl.BlockSpec((None, tr, cols), lambda h, t: (i, h * nt + t, 0))
    out = jax.ShapeDtypeStruct(w.shape, F32)
    carried = [] if prev is None else list(prev)
    return pl.pallas_call(
        body, name=name, grid=(2, nt),
        in_specs=[full, full, full, pl.BlockSpec((None, tr, cols), lambda h, t: (h, t, 0))]
        + [pl.BlockSpec(memory_space=pl.ANY)] * len(carried),
        out_specs=[full] * 4, out_shape=[out] * 4,
        input_output_aliases={4 + k: k for k in range(len(carried))},
        compiler_params=_params(("arbitrary",) * 2),
    )(w, m, v, g, *carried)


def _adamw(w, g, m, v, name):
    shape = w.shape
    w2, g2, m2, v2 = (t.reshape(-1, shape[-1]) for t in (w, g, m, v))
    rows, cols = w2.shape
    tr = 256 if rows % 256 == 0 else rows

    def body(w_ref, g_ref, m_ref, v_ref, d_ref, mo_ref, vo_ref):
        d_ref[...], mo_ref[...], vo_ref[...] = _adamw_math(w_ref[...], g_ref[...], m_ref[...], v_ref[...])

    blk = pl.BlockSpec((tr, cols), lambda i: (i, 0))
    out = jax.ShapeDtypeStruct((rows, cols), F32)
    d, mo, vo = pl.pallas_call(
        body, name=name, grid=(rows // tr,), in_specs=[blk] * 4, out_specs=[blk] * 3, out_shape=[out] * 3,
        compiler_params=_params(("arbitrary",)),
    )(w2, g2, m2, v2)
    return d.reshape(shape), mo.reshape(shape), vo.reshape(shape)


def _layer_slot(l):
    return (l % 2) * 2 + l // 2


def kernel(x, ln_g, ln_b, w_in_even, w_out_even, pool_w, pool_scale, sconv_w, sconv_b, w_in_odd, w_out_odd, sgu_ln_g, sgu_ln_b, sgu_w, sgu_b, dconv_w, dconv_b, dnorm_g, dnorm_b, loss_target, m_ln_g, m_ln_b, m_w_in_even, m_w_out_even, m_pool_w, m_pool_scale, m_sconv_w, m_sconv_b, m_w_in_odd, m_w_out_odd, m_sgu_ln_g, m_sgu_ln_b, m_sgu_w, m_sgu_b, m_dconv_w, m_dconv_b, m_dnorm_g, m_dnorm_b, v_ln_g, v_ln_b, v_w_in_even, v_w_out_even, v_pool_w, v_pool_scale, v_sconv_w, v_sconv_b, v_w_in_odd, v_w_out_odd, v_sgu_ln_g, v_sgu_ln_b, v_sgu_w, v_sgu_b, v_dconv_w, v_dconv_b, v_dnorm_g, v_dnorm_b):
    weights = dict(ln_g=ln_g, ln_b=ln_b, w_in_even=w_in_even, w_out_even=w_out_even, pool_w=pool_w,
                   pool_scale=pool_scale, sconv_w=sconv_w, sconv_b=sconv_b, w_in_odd=w_in_odd, w_out_odd=w_out_odd,
                   sgu_ln_g=sgu_ln_g, sgu_ln_b=sgu_ln_b, sgu_w=sgu_w, sgu_b=sgu_b, dconv_w=dconv_w,
                   dconv_b=dconv_b, dnorm_g=dnorm_g, dnorm_b=dnorm_b)
    moments_m = dict(ln_g=m_ln_g, ln_b=m_ln_b, w_in_even=m_w_in_even, w_out_even=m_w_out_even, pool_w=m_pool_w,
                     pool_scale=m_pool_scale, sconv_w=m_sconv_w, sconv_b=m_sconv_b, w_in_odd=m_w_in_odd,
                     w_out_odd=m_w_out_odd, sgu_ln_g=m_sgu_ln_g, sgu_ln_b=m_sgu_ln_b, sgu_w=m_sgu_w, sgu_b=m_sgu_b,
                     dconv_w=m_dconv_w, dconv_b=m_dconv_b, dnorm_g=m_dnorm_g, dnorm_b=m_dnorm_b)
    moments_v = dict(ln_g=v_ln_g, ln_b=v_ln_b, w_in_even=v_w_in_even, w_out_even=v_w_out_even, pool_w=v_pool_w,
                     pool_scale=v_pool_scale, sconv_w=v_sconv_w, sconv_b=v_sconv_b, w_in_odd=v_w_in_odd,
                     w_out_odd=v_w_out_odd, sgu_ln_g=v_sgu_ln_g, sgu_ln_b=v_sgu_ln_b, sgu_w=v_sgu_w, sgu_b=v_sgu_b,
                     dconv_w=v_dconv_w, dconv_b=v_dconv_b, dnorm_g=v_dnorm_g, dnorm_b=v_dnorm_b)
    names = list(weights)

    xd, yd, cd = _mesh_pos()
    chip = 2 * xd + yd
    pos_arr = jnp.stack([cd, chip]).astype(jnp.int32)

    small_sh = jnp.concatenate(
        [sconv_w.reshape(6, HEAD), sgu_ln_g, sgu_ln_b, dconv_b, dnorm_g, dnorm_b, dconv_w.reshape(62, HEAD),
         jnp.zeros((2, HEAD), F32), pool_w.reshape(512, HEAD)], axis=0)
    win_first, wout_first, small_g = _gather_weights(
        _cast_own(w_in_even[0:1], pos_arr, "cast_win_first"), _cast_own(w_out_even[0:1], pos_arr, "cast_wout_first"),
        small_sh)
    small_g = lax.dynamic_update_slice(small_g, small_sh[None], (chip, 0, 0))
    later_in = jnp.concatenate([w_in_even[1:2], w_in_odd], axis=0)
    later_out = jnp.concatenate([w_out_even[1:2], w_out_odd], axis=0)
    g_send, g_recv, _, g_lands, g_token = _split_start(
        _gather_rest_copies(True), "gather_rest_start", [],
        [_cast_own(later_in, pos_arr, "cast_win_rest"), _cast_own(later_out, pos_arr, "cast_wout_rest")])

    def layer_weights(slot):
        return (win_first, wout_first, 0) if slot == 0 else (win_rest, wout_rest, slot - 1)

    def full_rows(lo, n):
        return jnp.transpose(small_g[:, lo:lo + n], (1, 0, 2)).reshape(n, D)

    sconv_w_f = full_rows(Q_SCONV_W, 6).reshape(2, SHORT_K, D)
    sln_g_f = full_rows(Q_SLN_G, 2)
    sln_b_f = full_rows(Q_SLN_B, 2)
    dconv_b_f = full_rows(Q_DCONV_B, 2)
    dn_g_f = full_rows(Q_DN_G, 2)
    dn_b_f = full_rows(Q_DN_B, 2)
    dconv_w_f = jnp.transpose(full_rows(Q_DCONV_W, 62).reshape(2, CONV_K, LANE_BLOCKS, LANE), (0, 2, 1, 3))
    pool_w_f = jnp.transpose(small_g[:, Q_POOL_W:].reshape(NQ, 2, 4, 64, HEAD), (1, 2, 0, 3, 4)).reshape(2, 4, HEAD, HEAD)
    pool_w_b = pool_w_f.astype(BF16)
    pool_wt_b = jnp.swapaxes(pool_w_f, 2, 3).astype(BF16)
    idx = jnp.arange(SGU_BLOCK)
    mask = (idx[None, :] // 64) <= (idx[:, None] // 64)
    ws_f = jnp.where(mask[None, None], sgu_w, 0.0)
    ws_b = ws_f.astype(BF16)
    wst_b = jnp.swapaxes(ws_f, 2, 3).astype(BF16)

    def row(a, i):
        return a[i:i + 1]

    residual = (x[0], jnp.ones((1, D), F32), jnp.zeros((1, D), F32))
    x_b = x[0].astype(BF16)
    saved = []
    conv_saved = {}
    for l in range(NL):
        i, slot = l // 2, _layer_slot(l)
        if l == 1:
            _, g_lands = _split_wait(_gather_rest_copies(False), "gather_rest_wait", g_send, g_recv, [], g_lands, x_b)
            win_rest, wout_rest = _gather_rest_forward(g_lands[0], g_lands[1])
        win_g, wout_g, k = layer_weights(slot)
        z = _proj_in(x_b, win_g, k, l, g_token if l == 0 else None)
        if l % 2 == 0:
            ycat = _even_fwd(z, pool_w_b[i], row(pool_scale, i), sconv_w_f[i], row(sconv_b, i), l)
        else:
            ycat, conv_hat, conv_rstd = _odd_fwd(
                z, row(sln_g_f, i), row(sln_b_f, i), ws_b[i], _sgu_bias_rows(sgu_b[i]),
                dconv_w_f[i], row(dconv_b_f, i), row(dn_g_f, i), row(dn_b_f, i), l)
            conv_saved[l] = (conv_hat, conv_rstd)
        x_next_b, xhat, rstd = _proj_out_ln(ycat, wout_g, k, l, *residual, row(ln_g, l), row(ln_b, l))
        saved.append((x_b, z, ycat, xhat, rstd))
        residual = (xhat, row(ln_g, l), row(ln_b, l))
        x_b = x_next_b

    small = {}
    d_ln_g = [None] * NL
    d_ln_b = [None] * NL
    large = {"w_in_even": None, "w_out_even": None, "w_in_odd": None, "w_out_odd": None}
    pending = None
    token = None

    def finish(exchange, after):
        lx, send, recv, srcs, lands, own_in, own_out = exchange
        _, (r_in, r_out) = _split_wait(_chip_copies(False), f"chip_wait_{lx}", send, recv, srcs, lands, after)
        fin = _chip_sum(own_in, r_in, pos_arr, f"chip_sum_in_{lx}")
        fout = _chip_sum(own_out, r_out, pos_arr, f"chip_sum_out_{lx}")
        gs_in, gs_out = _pair_share(fin, fout, lx)
        kind = "even" if lx % 2 == 0 else "odd"
        for nm, gs in ((f"w_in_{kind}", gs_in), (f"w_out_{kind}", gs_out)):
            large[nm] = _adamw_large(weights[nm], moments_m[nm], moments_v[nm], gs, lx // 2, large[nm],
                                     f"adamw_{nm}_{lx // 2}")
        return large[f"w_out_{kind}"][0]

    for l in reversed(range(NL)):
        i, slot = l // 2, _layer_slot(l)
        win_g, wout_g, k = layer_weights(slot)
        xin_b, z, ycat, xhat, rstd = saved[l]
        if l == NL - 1:
            loss_part, dr, dr_b, d_ln_g[l], d_ln_b[l] = _loss_ln_bwd(
                xhat, rstd, row(ln_g, l), row(ln_b, l), loss_target[0], l)
            loss = lax.psum(loss_part[0, 0], ("x", "y", "c"))
        else:
            dr, dr_b, d_ln_g[l], d_ln_b[l] = _ln_bwd(dxn, xhat, rstd, row(ln_g, l), l, token)
        dy = _dycat(dr_b, wout_g, k, l)
        gout = _dwout(ycat, dr_b, l).reshape(NQ, RQ, D)
        if l % 2 == 0:
            dz, d_pw, d_ps, d_cw, d_cb = _even_bwd(z, dy, pool_w_b[i], pool_wt_b[i], row(pool_scale, i),
                                                   sconv_w_f[i], row(sconv_b, i), l)
            small[("pool_w", i)] = d_pw
            small[("pool_scale", i)] = d_ps
            small[("sconv_w", i)] = d_cw
            small[("sconv_b", i)] = d_cb
        else:
            dz, d_lg, d_lb, d_ws, d_sb, d_cw, d_cb, d_ng, d_nb = _odd_bwd(
                z, dy, *conv_saved[l], row(sln_g_f, i), row(sln_b_f, i), ws_b[i], wst_b[i],
                _sgu_bias_rows(sgu_b[i]), dconv_w_f[i], row(dn_g_f, i), row(dn_b_f, i), l)
            small[("sgu_ln_g", i)] = d_lg
            small[("sgu_ln_b", i)] = d_lb
            small[("sgu_w", i)] = jnp.where(mask[None], d_ws, 0.0)
            small[("sgu_b", i)] = d_sb
            small[("dconv_w", i)] = jnp.transpose(d_cw, (1, 0, 2)).reshape(CONV_K, D)
            small[("dconv_b", i)] = d_cb
            small[("dnorm_g", i)] = d_ng
            small[("dnorm_b", i)] = d_nb
        gin = _dwin(xin_b, dz, l)
        p_send, p_recv, p_srcs, p_lands, p_token = _split_start(
            _pair_copies(True), f"pair_start_{l}", [gin, gout],
            [jax.ShapeDtypeStruct((NQ, D // 2, WQ), F32), jax.ShapeDtypeStruct((NQ, RQ // 2, D), F32)])
        if l > 0:
            dxn = _dx(dz, win_g, k, l, dr, p_token)
            if pending is not None:
                finish(pending, dxn)
            wait_after = dxn
        else:
            wait_after = finish(pending, p_token)
        (gin, gout), (rin, rout) = _split_wait(_pair_copies(False), f"pair_wait_{l}", p_send, p_recv,
                                               p_srcs, p_lands, wait_after)
        pin_b, pin_own = _pair_sum(gin, rin, pos_arr, f"pair_sum_in_{l}")
        pout_b, pout_own = _pair_sum(gout, rout, pos_arr, f"pair_sum_out_{l}")
        send, recv, srcs, lands, token = _split_start(
            _chip_copies(True), f"chip_start_{l}", [pin_b, pout_b],
            [jax.ShapeDtypeStruct((3,) + pin_b.shape[1:], BF16), jax.ShapeDtypeStruct((3,) + pout_b.shape[1:], BF16)])
        if l == 0:
            dxn = _dx(dz, win_g, k, l, dr, token)
        pending = (l, send, recv, srcs, lands, pin_own, pout_own)
    grad_x = dxn[None]

    def both(name, first, step):
        return [(first, small[(name, 0)]), (first + step, small[(name, 1)])]

    vectors = ([(R_LN_G + l, d_ln_g[l]) for l in range(NL)] + [(R_LN_B + l, d_ln_b[l]) for l in range(NL)]
               + both("pool_scale", R_PSCALE, 1) + both("sconv_b", R_SCONV_B, 1) + both("sconv_w", R_SCONV_W, SHORT_K)
               + both("sgu_ln_g", R_SLN_G, 1) + both("sgu_ln_b", R_SLN_B, 1) + both("dconv_b", R_DCONV_B, 1)
               + both("dnorm_g", R_DN_G, 1) + both("dnorm_b", R_DN_B, 1) + both("dconv_w", R_DCONV_W, CONV_K)
               + both("sgu_b", R_SGU_B, 4))
    sgu_w_rows = 4 * SGU_BLOCK
    pool_w_rows = 4 * HEAD
    total, total_sgu_w, total_pool_w = _allreduce_small(
        [(R_VECTORS, D, F32, vectors),
         (2 * sgu_w_rows, SGU_BLOCK, BF16,
          [(i * sgu_w_rows, small[("sgu_w", i)].reshape(sgu_w_rows, SGU_BLOCK)) for i in range(2)]),
         (2 * pool_w_rows, HEAD, BF16,
          [(i * pool_w_rows, small[("pool_w", i)].reshape(pool_w_rows, HEAD)) for i in range(2)])],
        dxn)
    finish(pending, total)

    def mine(a):
        return lax.dynamic_slice_in_dim(a, chip * HEAD, HEAD, axis=a.ndim - 1)

    grads = {
        "ln_g": total[R_LN_G:R_LN_G + 4],
        "ln_b": total[R_LN_B:R_LN_B + 4],
        "pool_scale": total[R_PSCALE:R_PSCALE + 2],
        "sconv_b": total[R_SCONV_B:R_SCONV_B + 2],
        "sconv_w": mine(total[R_SCONV_W:R_SCONV_W + 6].reshape(2, SHORT_K, D)),
        "sgu_ln_g": mine(total[R_SLN_G:R_SLN_G + 2]),
        "sgu_ln_b": mine(total[R_SLN_B:R_SLN_B + 2]),
        "dconv_b": mine(total[R_DCONV_B:R_DCONV_B + 2]),
        "dnorm_g": mine(total[R_DN_G:R_DN_G + 2]),
        "dnorm_b": mine(total[R_DN_B:R_DN_B + 2]),
        "dconv_w": mine(total[R_DCONV_W:R_DCONV_W + 62].reshape(2, CONV_K, D)),
        "sgu_b": total[R_SGU_B:R_SGU_B + 8, 0:SGU_BLOCK].reshape(2, 4, SGU_BLOCK),
        "sgu_w": total_sgu_w.reshape(2, 4, SGU_BLOCK, SGU_BLOCK),
        "pool_w": lax.dynamic_slice_in_dim(total_pool_w.reshape(2, 4, HEAD, HEAD), chip * 64, 64, axis=2),
    }

    deltas, new_m, new_v = {}, {}, {}
    for name in names:
        if name in large:
            grads[name], deltas[name], new_m[name], new_v[name] = large[name]
        else:
            deltas[name], new_m[name], new_v[name] = _adamw(
                weights[name], grads[name], moments_m[name], moments_v[name], f"adamw_{name}")

    return (loss, grad_x, *[grads[n] for n in names], *[deltas[n] for n in names],
            *[new_m[n] for n in names], *[new_v[n] for n in names])
```

```python
import jax
import jax.numpy as jnp
from jax import lax
from jax.experimental import pallas as pl
from jax.experimental.pallas import tpu as pltpu

F32 = jnp.float32
BF16 = jnp.bfloat16
MXU_DTYPE = BF16

D = 1024
DZ = 6144
DY = 2048
NQ = 4
WQ = DZ // NQ
RQ = DY // NQ
NL = 4
ALPHA = (2 * NL) ** 0.25
LN_EPS = 1e-5
CONV_K = 31
SHORT_K = 3
SGU_BLOCK = 128
HEAD = 256
POOL_HALO = 16
CONV_HALO = 32
LANE = 128
SUBLANES = 8
CONV_ROWS = 32
LANE_BLOCKS = 8
MIB = 1024 * 1024

ADAM_LR = 0.001
ADAM_B1 = 0.9
ADAM_B2 = 0.999
ADAM_EPS = 1e-08
ADAM_WD = 0.01
ADAM_STEP = 10

NN = ((1,), (0,))
NT = ((1,), (1,))
TN = ((0,), (0,))
MESH = pl.DeviceIdType.MESH
HBM_SPEC = pl.BlockSpec(memory_space=pltpu.HBM)
SEM_SPEC = pl.BlockSpec(memory_space=pltpu.SEMAPHORE)
SIDE_EFFECT = pltpu.SideEffectType.DATAFLOW_SIDE_EFFECTING

R_LN_G, R_LN_B, R_PSCALE, R_SCONV_B, R_SCONV_W = 0, 4, 8, 10, 12
R_SLN_G, R_SLN_B, R_DCONV_B, R_DN_G, R_DN_B, R_DCONV_W = 18, 20, 22, 24, 26, 28
R_SGU_B, R_VECTORS = 90, 104
Q_SCONV_W, Q_SLN_G, Q_SLN_B, Q_DCONV_B, Q_DN_G, Q_DN_B, Q_DCONV_W, Q_POOL_W, Q_ROWS = 0, 6, 8, 10, 12, 14, 16, 80, 592


def _dot(a, b, dims):
    return lax.dot_general(a.astype(MXU_DTYPE), b.astype(MXU_DTYPE), (dims, ((), ())),
                           preferred_element_type=F32)


def _params(semantics=None, vmem_mib=48):
    return pltpu.CompilerParams(dimension_semantics=semantics, vmem_limit_bytes=vmem_mib * MIB)


def _sigmoid(v):
    return 0.5 * jnp.tanh(0.5 * v) + 0.5


def _silu_and_grad(v):
    s = _sigmoid(v)
    return v * s, s * (1.0 + v * (1.0 - s))


def _ln_stats(v):
    mu = jnp.mean(v, axis=-1, keepdims=True)
    vc = v - mu
    var = jnp.mean(vc * vc, axis=-1, keepdims=True)
    rstd = lax.rsqrt(var + LN_EPS)
    return vc * rstd, rstd


def _ln_bwd_rows(dxhat, xhat, rstd):
    m1 = jnp.mean(dxhat, axis=-1, keepdims=True)
    m2 = jnp.mean(dxhat * xhat, axis=-1, keepdims=True)
    return rstd * (dxhat - m1 - xhat * m2)


def _colsum(v):
    return jnp.sum(v, axis=0, keepdims=True)


def _section(ref, k):
    return ref[:, k * D:(k + 1) * D].astype(F32)


def _adamw_math(w, g, m, v):
    m_new = ADAM_B1 * m + (1.0 - ADAM_B1) * g
    v_new = ADAM_B2 * v + (1.0 - ADAM_B2) * (g * g)
    m_hat = m_new / (1.0 - ADAM_B1 ** ADAM_STEP)
    v_hat = v_new / (1.0 - ADAM_B2 ** ADAM_STEP)
    return -ADAM_LR * (m_hat / (jnp.sqrt(v_hat) + ADAM_EPS) + ADAM_WD * w), m_new, v_new


def _mesh_pos():
    return lax.axis_index("x"), lax.axis_index("y"), lax.axis_index("c")


def _after_spec(after):
    return [] if after is None else [pl.BlockSpec(memory_space=pl.ANY)]


def _after_args(after):
    return [] if after is None else [after]


def _proj_in(xb, win_g, k, l, after=None):
    s = xb.shape[0]
    tm = min(s, 1024)

    def body(x_ref, w_ref, *rest):
        rest[-1][...] = _dot(x_ref[...], w_ref[...].reshape(D, WQ), NN).astype(BF16)

    return pl.pallas_call(
        body, name=f"proj_in_{l}", grid=(NQ, s // tm),
        in_specs=[pl.BlockSpec((tm, D), lambda q, m: (m, 0)),
                  pl.BlockSpec((None, 2, None, D // 2, WQ), lambda q, m: (q, 0, k, 0, 0))] + _after_spec(after),
        out_specs=pl.BlockSpec((tm, WQ), lambda q, m: (m, q)),
        out_shape=jax.ShapeDtypeStruct((s, DZ), BF16),
        compiler_params=_params(("arbitrary", "arbitrary")),
    )(xb, win_g, *_after_args(after))


def _proj_out_ln(ycat, wout_g, k, l, res, res_g, res_b, g, b):
    s = res.shape[0]
    tm = min(s, 512)

    def body(y_ref, w_ref, r_ref, rg_ref, rb_ref, g_ref, b_ref, xb_ref, xh_ref, rs_ref):
        y = _dot(y_ref[...], w_ref[...].reshape(DY, D), NN)
        x = r_ref[...] * rg_ref[...] + rb_ref[...]
        xhat, rstd = _ln_stats(ALPHA * x + y)
        xb_ref[...] = (xhat * g_ref[...] + b_ref[...]).astype(BF16)
        xh_ref[...] = xhat
        rs_ref[...] = rstd

    row = lambda m: (m, 0)
    vec = pl.BlockSpec((1, D), lambda m: (0, 0))
    return pl.pallas_call(
        body, name=f"proj_out_ln_{l}", grid=(s // tm,),
        in_specs=[pl.BlockSpec((tm, DY), row),
                  pl.BlockSpec((NQ, 2, None, RQ // 2, D), lambda m: (0, 0, k, 0, 0)),
                  pl.BlockSpec((tm, D), row), vec, vec, vec, vec],
        out_specs=[pl.BlockSpec((tm, D), row), pl.BlockSpec((tm, D), row), pl.BlockSpec((tm, 1), row)],
        out_shape=[jax.ShapeDtypeStruct((s, D), BF16), jax.ShapeDtypeStruct((s, D), F32),
                   jax.ShapeDtypeStruct((s, 1), F32)],
        compiler_params=_params(("arbitrary",)),
    )(ycat, wout_g, res, res_g, res_b, g, b)


def _loss_ln_bwd(xhat, rstd, g, b, target, l):
    s = xhat.shape[0]
    ts = min(s, 512)

    def body(xh_ref, rs_ref, g_ref, b_ref, t_ref, loss_ref, dr_ref, drb_ref, dg_ref, db_ref):
        @pl.when(pl.program_id(0) == 0)
        def _():
            loss_ref[...] = jnp.zeros_like(loss_ref)
            dg_ref[...] = jnp.zeros_like(dg_ref)
            db_ref[...] = jnp.zeros_like(db_ref)
        xhat_v = xh_ref[...]
        gain = g_ref[...]
        err = (xhat_v * gain + b_ref[...]) - t_ref[...]
        loss_ref[...] += 0.5 * jnp.sum(jnp.mean(err * err, axis=-1, keepdims=True), axis=0, keepdims=True)
        d = err * (1.0 / D)
        dr = _ln_bwd_rows(d * gain, xhat_v, rs_ref[...])
        dr_ref[...] = dr
        drb_ref[...] = dr.astype(BF16)
        dg_ref[...] += _colsum(d * xhat_v)
        db_ref[...] += _colsum(d)

    row = lambda m: (m, 0)
    fixed = lambda m: (0, 0)
    vec = pl.BlockSpec((1, D), fixed)
    return pl.pallas_call(
        body, name=f"loss_ln_bwd_{l}", grid=(s // ts,),
        in_specs=[pl.BlockSpec((ts, D), row), pl.BlockSpec((ts, 1), row), vec, vec, pl.BlockSpec((ts, D), row)],
        out_specs=[pl.BlockSpec((1, 1), fixed), pl.BlockSpec((ts, D), row), pl.BlockSpec((ts, D), row), vec, vec],
        out_shape=[jax.ShapeDtypeStruct((1, 1), F32), jax.ShapeDtypeStruct((s, D), F32),
                   jax.ShapeDtypeStruct((s, D), BF16), jax.ShapeDtypeStruct((1, D), F32),
                   jax.ShapeDtypeStruct((1, D), F32)],
        compiler_params=_params(("arbitrary",)),
    )(xhat, rstd, g, b, target)


def _ln_bwd(dxn, xhat, rstd, g, l, after=None):
    s = dxn.shape[0]
    ts = min(s, 512)

    def body(d_ref, xh_ref, rs_ref, g_ref, *rest):
        dr_ref, drb_ref, dg_ref, db_ref = rest[-4:]

        @pl.when(pl.program_id(0) == 0)
        def _():
            dg_ref[...] = jnp.zeros_like(dg_ref)
            db_ref[...] = jnp.zeros_like(db_ref)
        d = d_ref[...]
        xhat_v = xh_ref[...]
        dr = _ln_bwd_rows(d * g_ref[...], xhat_v, rs_ref[...])
        dr_ref[...] = dr
        drb_ref[...] = dr.astype(BF16)
        dg_ref[...] += _colsum(d * xhat_v)
        db_ref[...] += _colsum(d)

    row = lambda m: (m, 0)
    fixed = lambda m: (0, 0)
    return pl.pallas_call(
        body, name=f"ln_bwd_{l}", grid=(s // ts,),
        in_specs=[pl.BlockSpec((ts, D), row), pl.BlockSpec((ts, D), row), pl.BlockSpec((ts, 1), row),
                  pl.BlockSpec((1, D), fixed)] + _after_spec(after),
        out_specs=[pl.BlockSpec((ts, D), row), pl.BlockSpec((ts, D), row), pl.BlockSpec((1, D), fixed),
                   pl.BlockSpec((1, D), fixed)],
        out_shape=[jax.ShapeDtypeStruct((s, D), F32), jax.ShapeDtypeStruct((s, D), BF16),
                   jax.ShapeDtypeStruct((1, D), F32), jax.ShapeDtypeStruct((1, D), F32)],
        compiler_params=_params(("arbitrary",)),
    )(dxn, xhat, rstd, g, *_after_args(after))


def _dycat(drb, wout_g, k, l):
    s = drb.shape[0]
    tm = min(s, 512)

    def body(d_ref, w_ref, o_ref):
        o_ref[...] = _dot(d_ref[...], w_ref[...].reshape(DY, D), NT).astype(BF16)

    return pl.pallas_call(
        body, name=f"dycat_{l}", grid=(s // tm,),
        in_specs=[pl.BlockSpec((tm, D), lambda m: (m, 0)),
                  pl.BlockSpec((NQ, 2, None, RQ // 2, D), lambda m: (0, 0, k, 0, 0))],
        out_specs=pl.BlockSpec((tm, DY), lambda m: (m, 0)),
        out_shape=jax.ShapeDtypeStruct((s, DY), BF16),
        compiler_params=_params(("arbitrary",)),
    )(drb, wout_g)


def _dwout(ycat, drb, l):
    s = drb.shape[0]
    tk = min(s, 1024)

    def body(y_ref, d_ref, o_ref):
        @pl.when(pl.program_id(0) == 0)
        def _():
            o_ref[...] = jnp.zeros_like(o_ref)

        o_ref[...] += _dot(y_ref[...], d_ref[...], TN)

    return pl.pallas_call(
        body, name=f"dwout_{l}", grid=(s // tk,),
        in_specs=[pl.BlockSpec((tk, DY), lambda k: (k, 0)), pl.BlockSpec((tk, D), lambda k: (k, 0))],
        out_specs=pl.BlockSpec((DY, D), lambda k: (0, 0)),
        out_shape=jax.ShapeDtypeStruct((DY, D), F32),
        compiler_params=_params(("arbitrary",)),
    )(ycat, drb)


def _dwin(xb, dzb, l):
    s = xb.shape[0]
    tk = min(s, 1024)

    def body(x_ref, d_ref, o_ref):
        @pl.when(pl.program_id(1) == 0)
        def _():
            o_ref[...] = jnp.zeros_like(o_ref)

        o_ref[...] += _dot(x_ref[...], d_ref[...], TN)

    return pl.pallas_call(
        body, name=f"dwin_{l}", grid=(NQ, s // tk),
        in_specs=[pl.BlockSpec((tk, D), lambda q, k: (k, 0)), pl.BlockSpec((tk, WQ), lambda q, k: (k, q))],
        out_specs=pl.BlockSpec((None, D, WQ), lambda q, k: (q, 0, 0)),
        out_shape=jax.ShapeDtypeStruct((NQ, D, WQ), F32),
        compiler_params=_params(("arbitrary", "arbitrary")),
    )(xb, dzb)


def _dx(dzb, win_g, k, l, dr, after=None):
    s = dzb.shape[0]
    tm = min(s, 1024)

    def body(d_ref, w_ref, r_ref, *rest):
        o_ref = rest[-1]

        @pl.when(pl.program_id(1) == 0)
        def _():
            o_ref[...] = ALPHA * r_ref[...]

        o_ref[...] += _dot(d_ref[...], w_ref[...].reshape(D, WQ), NT)

    return pl.pallas_call(
        body, name=f"dx_{l}", grid=(s // tm, NQ),
        in_specs=[pl.BlockSpec((tm, WQ), lambda m, q: (m, q)),
                  pl.BlockSpec((None, 2, None, D // 2, WQ), lambda m, q: (q, 0, k, 0, 0)),
                  pl.BlockSpec((tm, D), lambda m, q: (m, 0))] + _after_spec(after),
        out_specs=pl.BlockSpec((tm, D), lambda m, q: (m, 0)),
        out_shape=jax.ShapeDtypeStruct((s, D), F32),
        compiler_params=_params(("arbitrary", "arbitrary")),
    )(dzb, win_g, dr, *_after_args(after))


class _RowShifts:
    def __init__(self, ref, most_rows):
        assert ref.shape[0] == most_rows + 2 * SUBLANES
        self.ref = ref
        ref[...] = jnp.zeros(ref.shape, F32)

    def put(self, block):
        self.rows = block.shape[0]
        self.ref[SUBLANES:SUBLANES + self.rows, :] = block

    def get(self, k, causal):
        start = SUBLANES - k if causal else SUBLANES + k
        return self.ref[start:start + self.rows, :]

    def window_sums(self, block, steps, causal):
        acc = block
        for k in (1, 2, 4, 8)[:steps]:
            self.put(acc)
            acc = acc + self.get(k, causal)
        return acc


def _inv_positions(first_pos, rows):
    t1 = (lax.broadcasted_iota(jnp.int32, (rows, 1), 0) + first_pos + 1).astype(F32)
    return t1, 1.0 / t1


def _pool_inv(positions, cb):
    t1, inv_t1 = positions
    window = float(2 << (cb // 2))
    return jnp.where(t1 < window, inv_t1, 1.0 / window)


def _prev_index(ts, halo):
    return lambda i: (jnp.maximum(i * (ts // halo) - 1, 0), 0)


def _next_index(ts, halo, s):
    return lambda i: (jnp.minimum((i + 1) * (ts // halo), s // halo - 1), 0)


def _even_fwd(z, pool_w, pool_scale, sconv_w, sconv_b, l):
    s = z.shape[0]
    ts = min(s, 256)
    h = POOL_HALO

    def body(z_ref, zp_ref, pw_ref, ps_ref, cw_ref, cb_ref, o_ref, shift_ref, pooled_ref):
        i = pl.program_id(0)
        inside = i > 0
        shifts = _RowShifts(shift_ref, h + ts)
        positions = _inv_positions(i * ts, ts)

        for cb in range(LANE_BLOCKS):
            cols = slice(cb * LANE, (cb + 1) * LANE)

            def section(ref, k):
                return ref[:, k * D + cb * LANE:k * D + (cb + 1) * LANE].astype(F32)

            xa = section(z_ref, 0)
            sums = shifts.window_sums(jnp.concatenate([jnp.where(inside, section(zp_ref, 0), 0.0), xa], axis=0),
                                      cb // 2 + 1, True)
            pooled_ref[:, cols] = (sums[h:] * _pool_inv(positions, cb) - xa).astype(MXU_DTYPE)

            q_prev = jnp.where(inside, section(zp_ref, 4) * section(zp_ref, 2), 0.0)
            q_main = section(z_ref, 4) * section(z_ref, 2)
            shifts.put(jnp.concatenate([q_prev, q_main], axis=0))
            cv = (cw_ref[2:3, cols] * q_main + cw_ref[1:2, cols] * shifts.get(1, True)[h:]
                  + cw_ref[0:1, cols] * shifts.get(2, True)[h:] + cb_ref[:, cols])
            silu_gb, _ = _silu_and_grad(section(z_ref, 5))
            o_ref[:, D + cb * LANE:D + (cb + 1) * LANE] = (section(z_ref, 3) * cv * silu_gb).astype(BF16)

        for g in range(4):
            cols = slice(g * HEAD, (g + 1) * HEAD)
            p = _dot(pooled_ref[:, cols], pw_ref[g], NN)
            silu_ga, _ = _silu_and_grad(z_ref[:, D + g * HEAD:D + (g + 1) * HEAD].astype(F32))
            o_ref[:, cols] = (p * ps_ref[:, cols] * silu_ga).astype(BF16)

    fixed2 = lambda i: (0, 0)
    return pl.pallas_call(
        body, name=f"even_fwd_{l}", grid=(s // ts,),
        in_specs=[pl.BlockSpec((ts, DZ), lambda i: (i, 0)), pl.BlockSpec((h, DZ), _prev_index(ts, h)),
                  pl.BlockSpec((4, HEAD, HEAD), lambda i: (0, 0, 0)), pl.BlockSpec((1, D), fixed2),
                  pl.BlockSpec((SHORT_K, D), fixed2), pl.BlockSpec((1, D), fixed2)],
        out_specs=pl.BlockSpec((ts, DY), lambda i: (i, 0)),
        out_shape=jax.ShapeDtypeStruct((s, DY), BF16),
        scratch_shapes=[pltpu.VMEM((h + ts + 2 * SUBLANES, LANE), F32), pltpu.VMEM((ts, D), MXU_DTYPE)],
        compiler_params=_params(("arbitrary",)),
    )(z, z, pool_w, pool_scale, sconv_w, sconv_b)


def _even_bwd(z, dy, pool_w, pool_wt, pool_scale, sconv_w, sconv_b, l):
    s = z.shape[0]
    ts = min(s, 256)
    h = POOL_HALO
    n_tiles = s // ts

    def body(z_ref, zp_ref, zn_ref, dy_ref, dyn_ref, pw_ref, pwt_ref, ps_ref, cw_ref, cb_ref,
             dz_ref, dpw_ref, dps_ref, dcw_ref, dcb_ref, shift_ref, pooled_ref, p_ref, dp_ref, dpooled_ref,
             shift2_ref):
        i = pl.program_id(0)
        inside = i > 0

        @pl.when(i == 0)
        def _():
            dpw_ref[...] = jnp.zeros_like(dpw_ref)
            dps_ref[...] = jnp.zeros_like(dps_ref)
            dcw_ref[...] = jnp.zeros_like(dcw_ref)
            dcb_ref[...] = jnp.zeros_like(dcb_ref)

        shifts = _RowShifts(shift_ref, ts + h)
        positions = _inv_positions(i * ts, ts + h)
        row = lax.broadcasted_iota(jnp.int32, (ts + h, 1), 0)
        live = jnp.logical_or(i < n_tiles - 1, row < ts)

        def section(ref, k, cb):
            return ref[:, k * D + cb * LANE:k * D + (cb + 1) * LANE].astype(F32)

        for cb in range(LANE_BLOCKS):
            xa = section(z_ref, 0, cb)
            sums = shifts.window_sums(
                jnp.concatenate([jnp.where(inside, section(zp_ref, 0, cb), 0.0), xa], axis=0), cb // 2 + 1, True)
            pooled_ref[:, cb * LANE:(cb + 1) * LANE] = (
                sums[h:] * _pool_inv(positions, cb)[:ts] - xa).astype(MXU_DTYPE)
        for g in range(4):
            cols = slice(g * HEAD, (g + 1) * HEAD)
            p_ref[:, cols] = _dot(pooled_ref[:, cols], pw_ref[g], NN)
        rc = CONV_ROWS

        def block(ref, k, cb, r0, n):
            return ref[r0:r0 + n, k * D + cb * LANE:k * D + (cb + 1) * LANE].astype(F32)

        def fold(v):
            part = v[0:SUBLANES]
            for j in range(1, v.shape[0] // SUBLANES):
                part = part + v[j * SUBLANES:(j + 1) * SUBLANES]
            return part

        def added(total, v):
            return fold(v) if total is None else total + fold(v)

        for cb in range(LANE_BLOCKS):
            cols = slice(cb * LANE, (cb + 1) * LANE)
            scale = ps_ref[:, cols]
            total = None
            for r0 in range(0, ts, rc):
                silu_ga, dsilu_ga = _silu_and_grad(block(z_ref, 1, cb, r0, rc))
                d_ya = block(dy_ref, 0, cb, r0, rc)
                dp_ref[r0:r0 + rc, cols] = (d_ya * scale * silu_ga).astype(MXU_DTYPE)
                d_ya_p = d_ya * p_ref[r0:r0 + rc, cols]
                dz_ref[r0:r0 + rc, D + cb * LANE:D + (cb + 1) * LANE] = (d_ya_p * scale * dsilu_ga).astype(BF16)
                total = added(total, d_ya_p * silu_ga)
            silu_ga, _ = _silu_and_grad(block(zn_ref, 1, cb, 0, h))
            dp_ref[ts:ts + h, cols] = (block(dyn_ref, 0, cb, 0, h) * scale * silu_ga).astype(MXU_DTYPE)
            dps_ref[:, cols] += _colsum(total)
        for g in range(4):
            cols = slice(g * HEAD, (g + 1) * HEAD)
            dpooled_ref[:, cols] = _dot(dp_ref[:, cols], pwt_ref[g], NN)
            dpw_ref[g] += _dot(pooled_ref[:, cols], dp_ref[0:ts, cols], TN)
        for cb in range(LANE_BLOCKS):
            d_pooled = jnp.where(live, dpooled_ref[:, cb * LANE:(cb + 1) * LANE], 0.0)
            sums = shifts.window_sums(d_pooled * _pool_inv(positions, cb), cb // 2 + 1, False)
            dz_ref[:, cb * LANE:(cb + 1) * LANE] = (sums[:ts] - d_pooled[:ts]).astype(BF16)

        more = i < n_tiles - 1
        q0 = SUBLANES + h
        for cb in range(LANE_BLOCKS):
            cols = slice(cb * LANE, (cb + 1) * LANE)
            w0, w1, w2 = cw_ref[0:1, cols], cw_ref[1:2, cols], cw_ref[2:3, cols]
            bias = cb_ref[:, cols]
            shift_ref[SUBLANES:q0, :] = jnp.where(inside, block(zp_ref, 4, cb, 0, h) * block(zp_ref, 2, cb, 0, h), 0.0)
            for r0 in range(0, ts, rc):
                shift_ref[q0 + r0:q0 + r0 + rc, :] = block(z_ref, 4, cb, r0, rc) * block(z_ref, 2, cb, r0, rc)
                silu_gb, _ = _silu_and_grad(block(z_ref, 5, cb, r0, rc))
                shift2_ref[SUBLANES + r0:SUBLANES + r0 + rc, :] = (
                    block(dy_ref, 1, cb, r0, rc) * block(z_ref, 3, cb, r0, rc) * silu_gb)
            silu_gb, _ = _silu_and_grad(block(zn_ref, 5, cb, 0, h))
            shift2_ref[SUBLANES + ts:SUBLANES + ts + h, :] = jnp.where(
                more, block(dyn_ref, 1, cb, 0, h) * block(zn_ref, 3, cb, 0, h) * silu_gb, 0.0)

            totals = [None] * 4
            for r0 in range(0, ts, rc):
                q_main = shift_ref[q0 + r0:q0 + r0 + rc, :]
                q_1 = shift_ref[q0 + r0 - 1:q0 + r0 - 1 + rc, :]
                q_2 = shift_ref[q0 + r0 - 2:q0 + r0 - 2 + rc, :]
                cv = w2 * q_main + w1 * q_1 + w0 * q_2 + bias
                d0 = SUBLANES + r0
                d_cv0 = shift2_ref[d0:d0 + rc, :]
                d_q = w2 * d_cv0 + w1 * shift2_ref[d0 + 1:d0 + 1 + rc, :] + w0 * shift2_ref[d0 + 2:d0 + 2 + rc, :]
                silu_gb, dsilu_gb = _silu_and_grad(block(z_ref, 5, cb, r0, rc))
                d_yb_cv = block(dy_ref, 1, cb, r0, rc) * cv
                for k, val in ((2, d_q * block(z_ref, 4, cb, r0, rc)), (3, d_yb_cv * silu_gb),
                               (4, d_q * block(z_ref, 2, cb, r0, rc)),
                               (5, d_yb_cv * block(z_ref, 3, cb, r0, rc) * dsilu_gb)):
                    dz_ref[r0:r0 + rc, k * D + cb * LANE:k * D + (cb + 1) * LANE] = val.astype(BF16)
                for j, val in enumerate((d_cv0, d_cv0 * q_main, d_cv0 * q_1, d_cv0 * q_2)):
                    totals[j] = added(totals[j], val)
            dcb_ref[:, cols] += _colsum(totals[0])
            dcw_ref[2:3, cols] += _colsum(totals[1])
            dcw_ref[1:2, cols] += _colsum(totals[2])
            dcw_ref[0:1, cols] += _colsum(totals[3])

    fixed2 = lambda i: (0, 0)
    fixed3 = lambda i: (0, 0, 0)
    return pl.pallas_call(
        body, name=f"even_bwd_{l}", grid=(n_tiles,),
        in_specs=[pl.BlockSpec((ts, DZ), lambda i: (i, 0)), pl.BlockSpec((h, DZ), _prev_index(ts, h)),
                  pl.BlockSpec((h, DZ), _next_index(ts, h, s)),
                  pl.BlockSpec((ts, DY), lambda i: (i, 0)), pl.BlockSpec((h, DY), _next_index(ts, h, s)),
                  pl.BlockSpec((4, HEAD, HEAD), fixed3), pl.BlockSpec((4, HEAD, HEAD), fixed3),
                  pl.BlockSpec((1, D), fixed2), pl.BlockSpec((SHORT_K, D), fixed2), pl.BlockSpec((1, D), fixed2)],
        out_specs=[pl.BlockSpec((ts, DZ), lambda i: (i, 0)), pl.BlockSpec((4, HEAD, HEAD), fixed3),
                   pl.BlockSpec((1, D), fixed2), pl.BlockSpec((SHORT_K, D), fixed2), pl.BlockSpec((1, D), fixed2)],
        out_shape=[jax.ShapeDtypeStruct((s, DZ), BF16), jax.ShapeDtypeStruct((4, HEAD, HEAD), F32),
                   jax.ShapeDtypeStruct((1, D), F32), jax.ShapeDtypeStruct((SHORT_K, D), F32),
                   jax.ShapeDtypeStruct((1, D), F32)],
        scratch_shapes=[pltpu.VMEM((ts + h + 2 * SUBLANES, LANE), F32), pltpu.VMEM((ts, D), MXU_DTYPE),
                        pltpu.VMEM((ts, D), F32), pltpu.VMEM((ts + h, D), MXU_DTYPE), pltpu.VMEM((ts + h, D), F32),
                        pltpu.VMEM((ts + h + 2 * SUBLANES, LANE), F32)],
        compiler_params=_params(("arbitrary",), 56),
    )(z, z, z, dy, dy, pool_w, pool_wt, pool_scale, sconv_w, sconv_b)


def _to_blocks(ref, r0, val):
    n = val.shape[0]
    for cb in range(LANE_BLOCKS):
        ref[cb, r0:r0 + n, :] = val[:, cb * LANE:(cb + 1) * LANE]


def _from_blocks(ref):
    return jnp.concatenate([ref[cb] for cb in range(LANE_BLOCKS)], axis=1)


def _shift_copies(src_ref, sh_ref, n, causal):
    def block(cb, carry):
        for b in range(1, SUBLANES):
            if causal:
                sh_ref[cb, b - 1, SUBLANES:n, :] = src_ref[cb, SUBLANES - b:n - b, :]
            else:
                sh_ref[cb, b - 1, 0:n - SUBLANES, :] = src_ref[cb, b:n - SUBLANES + b, :]
        return carry

    lax.fori_loop(0, LANE_BLOCKS, block, 0)


def _tap(src_ref, sh_ref, cb, first, n, d, causal):
    whole, b = (d // SUBLANES) * SUBLANES, d % SUBLANES
    start = first - whole if causal else first + whole
    if b == 0:
        return src_ref[cb, start:start + n, :]
    return sh_ref[cb, b - 1, start:start + n, :]


def _chunk_rows(rows, most):
    return max(n for n in range(CONV_ROWS, most + 1, CONV_ROWS) if rows % n == 0)


def _conv31(src_ref, sh_ref, w_ref, dst_ref, base, rows, causal):
    n = _chunk_rows(rows, 4 * CONV_ROWS)

    def block(cb, carry):
        for r0 in range(0, rows, n):
            acc = None
            for d in range(CONV_K):
                term = w_ref[cb, CONV_K - 1 - d:CONV_K - d, :] * _tap(src_ref, sh_ref, cb, base + r0, n, d, causal)
                acc = term if acc is None else acc + term
            dst_ref[cb, r0:r0 + n, :] = acc
        return carry

    lax.fori_loop(0, LANE_BLOCKS, block, 0)


def _conv31_tap_grads(d_ref, src_ref, sh_ref, dw_ref, base, rows):
    n = _chunk_rows(rows, 2 * CONV_ROWS)

    def block(cb, carry):
        sums = [None] * CONV_K
        for r0 in range(0, rows, n):
            d_blk = d_ref[cb, r0:r0 + n, :]
            for d in range(CONV_K):
                prod = d_blk * _tap(src_ref, sh_ref, cb, base + r0, n, d, True)
                part = prod[0:SUBLANES]
                for k in range(1, n // SUBLANES):
                    part = part + prod[k * SUBLANES:(k + 1) * SUBLANES]
                sums[d] = part if sums[d] is None else sums[d] + part
        for d in range(CONV_K):
            j = CONV_K - 1 - d
            dw_ref[cb, j:j + 1, :] += _colsum(sums[d])
        return carry

    lax.fori_loop(0, LANE_BLOCKS, block, 0)


def _sgu_bias_rows(sgu_b):
    return jnp.repeat(jnp.transpose(sgu_b), HEAD, axis=1)


def _odd_fwd(z, sln_g, sln_b, ws, sbias, dconv_w, dconv_b, dn_g, dn_b, l):
    s = z.shape[0]
    ts = min(s, 256)
    h = CONV_HALO

    def body(z_ref, zp_ref, lg_ref, lb_ref, ws_ref, sb_ref, cw_ref, cb_ref, ng_ref, nb_ref, o_ref, zh_ref, rz_ref,
             zz_ref, zc_ref, sh_ref):
        i = pl.program_id(0)
        vhat, _ = _ln_stats(_section(z_ref, 1))
        vn = (vhat * lg_ref[...] + lb_ref[...]).astype(MXU_DTYPE)
        silu_gc, _ = _silu_and_grad(_section(z_ref, 2))
        for n in range(ts // SGU_BLOCK):
            rows = slice(n * SGU_BLOCK, (n + 1) * SGU_BLOCK)
            sv = jnp.concatenate(
                [_dot(ws_ref[hd], vn[rows, hd * HEAD:(hd + 1) * HEAD], NN) for hd in range(4)], axis=1)
            sv = sv + sb_ref[...]
            o_ref[rows, 0:D] = (z_ref[rows, 0:D].astype(F32) * sv * silu_gc[rows]).astype(BF16)

        _to_blocks(zz_ref, 0, jnp.where(i > 0, _section(zp_ref, 3) * _sigmoid(_section(zp_ref, 4)), 0.0))
        _to_blocks(zz_ref, h, _section(z_ref, 3) * _sigmoid(_section(z_ref, 4)))
        _shift_copies(zz_ref, sh_ref, h + ts, True)
        _conv31(zz_ref, sh_ref, cw_ref, zc_ref, h, ts, True)
        zhat, rstd_z = _ln_stats(_from_blocks(zc_ref) + cb_ref[...])
        zh_ref[...] = zhat
        rz_ref[...] = rstd_z
        silu_zn, _ = _silu_and_grad(zhat * ng_ref[...] + nb_ref[...])
        silu_gd, _ = _silu_and_grad(_section(z_ref, 5))
        o_ref[:, D:2 * D] = (silu_zn * silu_gd).astype(BF16)

    fixed2 = lambda i: (0, 0)
    vec = pl.BlockSpec((1, D), fixed2)
    return pl.pallas_call(
        body, name=f"odd_fwd_{l}", grid=(s // ts,),
        in_specs=[pl.BlockSpec((ts, DZ), lambda i: (i, 0)), pl.BlockSpec((h, DZ), _prev_index(ts, h)),
                  vec, vec, pl.BlockSpec((4, SGU_BLOCK, SGU_BLOCK), lambda i: (0, 0, 0)),
                  pl.BlockSpec((SGU_BLOCK, D), fixed2), pl.BlockSpec((LANE_BLOCKS, CONV_K, LANE), lambda i: (0, 0, 0)),
                  vec, vec, vec],
        out_specs=[pl.BlockSpec((ts, DY), lambda i: (i, 0)), pl.BlockSpec((ts, D), lambda i: (i, 0)),
                   pl.BlockSpec((ts, 1), lambda i: (i, 0))],
        out_shape=[jax.ShapeDtypeStruct((s, DY), BF16), jax.ShapeDtypeStruct((s, D), F32),
                   jax.ShapeDtypeStruct((s, 1), F32)],
        scratch_shapes=[pltpu.VMEM((LANE_BLOCKS, h + ts, LANE), F32), pltpu.VMEM((LANE_BLOCKS, ts, LANE), F32),
                        pltpu.VMEM((LANE_BLOCKS, SUBLANES - 1, h + ts, LANE), F32)],
        compiler_params=_params(("arbitrary",)),
    )(z, z, sln_g, sln_b, ws, sbias, dconv_w, dconv_b, dn_g, dn_b)


def _odd_bwd(z, dy, zhat_s, rstd_s, sln_g, sln_b, ws, wst, sbias, dconv_w, dn_g, dn_b, l):
    s = z.shape[0]
    ts = min(s, 256)
    h = CONV_HALO
    n_tiles = s // ts
    te = ts + h

    def body(z_ref, zp_ref, zn_ref, dy_ref, dyn_ref, zh_ref, zhn_ref, rz_ref, rzn_ref, lg_ref, lb_ref, ws_ref,
             wst_ref, sb_ref, cw_ref, ng_ref, nb_ref, dz_ref, dlg_ref, dlb_ref, dws_ref, dsb_ref, dcw_ref, dcb_ref,
             dng_ref, dnb_ref, zz_ref, dzc_ref, dzz_ref, dsb_acc, sh_ref):
        i = pl.program_id(0)
        more = i < n_tiles - 1

        @pl.when(i == 0)
        def _():
            for ref in (dlg_ref, dlb_ref, dws_ref, dsb_ref, dcw_ref, dcb_ref, dng_ref, dnb_ref, dsb_acc):
                ref[...] = jnp.zeros_like(ref)

        vhat, rstd_v = _ln_stats(_section(z_ref, 1))
        lg = lg_ref[...]
        vn = (vhat * lg + lb_ref[...]).astype(MXU_DTYPE)
        u = _section(z_ref, 0)
        silu_gc, dsilu_gc = _silu_and_grad(_section(z_ref, 2))
        d_yc = _section(dy_ref, 0)
        d_yc_u = d_yc * u
        d_sv = d_yc_u * silu_gc
        d_svb = d_sv.astype(MXU_DTYPE)
        sv_rows = []
        dvn_rows = []
        dsb = None
        for n in range(ts // SGU_BLOCK):
            rows = slice(n * SGU_BLOCK, (n + 1) * SGU_BLOCK)
            sv_parts = []
            dvn_parts = []
            for hd in range(4):
                cols = slice(hd * HEAD, (hd + 1) * HEAD)
                sv_parts.append(_dot(ws_ref[hd], vn[rows, cols], NN))
                dvn_parts.append(_dot(wst_ref[hd], d_svb[rows, cols], NN))
                dws_ref[hd] += _dot(d_svb[rows, cols], vn[rows, cols], NT)
            sv_rows.append(jnp.concatenate(sv_parts, axis=1) + sb_ref[...])
            dvn_rows.append(jnp.concatenate(dvn_parts, axis=1))
            dsb = d_sv[rows] if dsb is None else dsb + d_sv[rows]
        dsb_acc[...] += dsb

        @pl.when(i == n_tiles - 1)
        def _():
            for hd in range(4):
                blk = dsb_acc[:, hd * HEAD:(hd + 1) * HEAD]
                folded = blk[:, 0:LANE] + blk[:, LANE:HEAD]
                dsb_ref[hd:hd + 1, :] = _colsum(jnp.transpose(folded))
        sv = jnp.concatenate(sv_rows, axis=0)
        d_vn = jnp.concatenate(dvn_rows, axis=0)
        dz_ref[:, 0:D] = (d_yc * sv * silu_gc).astype(BF16)
        dz_ref[:, D:2 * D] = _ln_bwd_rows(d_vn * lg, vhat, rstd_v).astype(BF16)
        dz_ref[:, 2 * D:3 * D] = (d_yc_u * sv * dsilu_gc).astype(BF16)
        dlg_ref[...] += _colsum(d_vn * vhat)
        dlb_ref[...] += _colsum(d_vn)

        def gate(ref):
            return _section(ref, 3) * _sigmoid(_section(ref, 4))

        _to_blocks(zz_ref, 0, jnp.where(i > 0, gate(zp_ref), 0.0))
        _to_blocks(zz_ref, h, gate(z_ref))
        _shift_copies(zz_ref, sh_ref, h + ts, True)
        zhat = jnp.concatenate([zh_ref[...], zhn_ref[...]], axis=0)
        rstd_z = jnp.concatenate([rz_ref[...], rzn_ref[...]], axis=0)
        ng = ng_ref[...]
        silu_zn, dsilu_zn = _silu_and_grad(zhat * ng + nb_ref[...])
        gd = jnp.concatenate([_section(z_ref, 5), _section(zn_ref, 5)], axis=0)
        silu_gd, dsilu_gd = _silu_and_grad(gd)
        d_yd = jnp.concatenate([_section(dy_ref, 1), _section(dyn_ref, 1)], axis=0)
        d_zn = d_yd * silu_gd * dsilu_zn
        d_zc = _ln_bwd_rows(d_zn * ng, zhat, rstd_z)
        row = lax.broadcasted_iota(jnp.int32, (te, 1), 0)
        d_zc = jnp.where(jnp.logical_or(more, row < ts), d_zc, 0.0)
        _to_blocks(dzc_ref, 0, d_zc)
        dz_ref[:, 5 * D:6 * D] = (d_yd[:ts] * silu_zn[:ts] * dsilu_gd[:ts]).astype(BF16)
        dng_ref[...] += _colsum(d_zn[:ts] * zhat[:ts])
        dnb_ref[...] += _colsum(d_zn[:ts])
        dcb_ref[...] += _colsum(d_zc[:ts])
        _conv31_tap_grads(dzc_ref, zz_ref, sh_ref, dcw_ref, h, ts)
        _shift_copies(dzc_ref, sh_ref, te, False)
        _conv31(dzc_ref, sh_ref, cw_ref, dzz_ref, 0, ts, False)
        d_zz = _from_blocks(dzz_ref)
        a = _section(z_ref, 3)
        sig_b = _sigmoid(_section(z_ref, 4))
        dz_ref[:, 3 * D:4 * D] = (d_zz * sig_b).astype(BF16)
        dz_ref[:, 4 * D:5 * D] = (d_zz * a * sig_b * (1.0 - sig_b)).astype(BF16)

    fixed2 = lambda i: (0, 0)
    fixed3 = lambda i: (0, 0, 0)
    vec = pl.BlockSpec((1, D), fixed2)
    mat = pl.BlockSpec((4, SGU_BLOCK, SGU_BLOCK), fixed3)
    vec_shape = jax.ShapeDtypeStruct((1, D), F32)
    return pl.pallas_call(
        body, name=f"odd_bwd_{l}", grid=(n_tiles,),
        in_specs=[pl.BlockSpec((ts, DZ), lambda i: (i, 0)), pl.BlockSpec((h, DZ), _prev_index(ts, h)),
                  pl.BlockSpec((h, DZ), _next_index(ts, h, s)),
                  pl.BlockSpec((ts, DY), lambda i: (i, 0)), pl.BlockSpec((h, DY), _next_index(ts, h, s)),
                  pl.BlockSpec((ts, D), lambda i: (i, 0)), pl.BlockSpec((h, D), _next_index(ts, h, s)),
                  pl.BlockSpec((ts, 1), lambda i: (i, 0)), pl.BlockSpec((h, 1), _next_index(ts, h, s)),
                  vec, vec, mat, mat, pl.BlockSpec((SGU_BLOCK, D), fixed2),
                  pl.BlockSpec((LANE_BLOCKS, CONV_K, LANE), fixed3), vec, vec],
        out_specs=[pl.BlockSpec((ts, DZ), lambda i: (i, 0)), vec, vec, mat, pl.BlockSpec((4, SGU_BLOCK), fixed2),
                   pl.BlockSpec((LANE_BLOCKS, CONV_K, LANE), fixed3), vec, vec, vec],
        out_shape=[jax.ShapeDtypeStruct((s, DZ), BF16), vec_shape, vec_shape,
                   jax.ShapeDtypeStruct((4, SGU_BLOCK, SGU_BLOCK), F32), jax.ShapeDtypeStruct((4, SGU_BLOCK), F32),
                   jax.ShapeDtypeStruct((LANE_BLOCKS, CONV_K, LANE), F32), vec_shape, vec_shape, vec_shape],
        scratch_shapes=[pltpu.VMEM((LANE_BLOCKS, h + ts, LANE), F32), pltpu.VMEM((LANE_BLOCKS, te, LANE), F32),
                        pltpu.VMEM((LANE_BLOCKS, ts, LANE), F32), pltpu.VMEM((SGU_BLOCK, D), F32),
                        pltpu.VMEM((LANE_BLOCKS, SUBLANES - 1, te, LANE), F32)],
        compiler_params=_params(("arbitrary",), 60),
    )(z, z, z, dy, dy, zhat_s, zhat_s, rstd_s, rstd_s, sln_g, sln_b, ws, wst, sbias, dconv_w, dn_g, dn_b)


def _remote(src, dst, send_sems, recv_sems, k, to):
    return pltpu.make_async_remote_copy(src_ref=src, dst_ref=dst, send_sem=send_sems.at[k],
                                        recv_sem=recv_sems.at[k], device_id=to, device_id_type=MESH)


def _other_chips(x, y):
    return [(1 - x, y, 2 * (1 - x) + y), (x, 1 - y, 2 * x + 1 - y), (1 - x, 1 - y, 2 * (1 - x) + 1 - y)]


def _cast_own(w_stack, pos_arr, name):
    slots, rows, cols = w_stack.shape
    half = rows // 2

    def body(pos_ref, w_ref, o_ref):
        o_ref[...] = w_ref[...].astype(BF16)

    grid_spec = pltpu.PrefetchScalarGridSpec(
        num_scalar_prefetch=1, grid=(slots, 2),
        in_specs=[pl.BlockSpec((None, half, cols), lambda s, h, pos: (s, h, 0))],
        out_specs=pl.BlockSpec((None, None, None, half, cols), lambda s, h, pos: (pos[1], h, s, 0, 0)))
    return pl.pallas_call(
        body, name=name, grid_spec=grid_spec, out_shape=jax.ShapeDtypeStruct((NQ, 2, slots, half, cols), BF16),
        compiler_params=_params(("arbitrary",) * 2),
    )(pos_arr, w_stack)


def _gather_weights(win_g, wout_g, small_sh):
    def body(win_in, wout_in, small, win_g, wout_g, small_g, send_sems, recv_sems):
        del win_in, wout_in
        x, y, c = _mesh_pos()
        me = 2 * x + y
        sibling = (x, y, 1 - c)
        chips = _other_chips(x, y)

        sends = []
        for j, (cx, cy, _) in enumerate(chips):
            to = (cx, cy, c)
            sends.append(_remote(win_g.at[me, c], win_g.at[me, c], send_sems, recv_sems, j, to))
            sends.append(_remote(wout_g.at[me, c], wout_g.at[me, c], send_sems, recv_sems, 3 + j, to))
            sends.append(_remote(small, small_g.at[me], send_sems, recv_sems, 6 + j, to))
        for cp in sends:
            cp.start()
        passed = []
        for j, (_, _, q) in enumerate(chips):
            got_in = win_g.at[q, c]
            got_out = wout_g.at[q, c]
            _remote(got_in, got_in, send_sems, recv_sems, j, sibling).wait_recv()
            cp = _remote(got_in, got_in, send_sems, recv_sems, 9 + j, sibling)
            cp.start()
            passed.append(cp)
            _remote(got_out, got_out, send_sems, recv_sems, 3 + j, sibling).wait_recv()
            cp = _remote(got_out, got_out, send_sems, recv_sems, 12 + j, sibling)
            cp.start()
            passed.append(cp)
            _remote(small, small_g.at[q], send_sems, recv_sems, 6 + j, sibling).wait_recv()
        for j, (_, _, q) in enumerate(chips):
            from_in = win_g.at[q, 1 - c]
            from_out = wout_g.at[q, 1 - c]
            _remote(from_in, from_in, send_sems, recv_sems, 9 + j, sibling).wait_recv()
            _remote(from_out, from_out, send_sems, recv_sems, 12 + j, sibling).wait_recv()
        for cp in sends + passed:
            cp.wait_send()

    return pl.pallas_call(
        body, name="gather_weights",
        in_specs=[HBM_SPEC, HBM_SPEC, HBM_SPEC], out_specs=[HBM_SPEC, HBM_SPEC, HBM_SPEC],
        out_shape=[jax.ShapeDtypeStruct(win_g.shape, win_g.dtype), jax.ShapeDtypeStruct(wout_g.shape, wout_g.dtype),
                   jax.ShapeDtypeStruct((NQ,) + small_sh.shape, small_sh.dtype)],
        input_output_aliases={0: 0, 1: 1},
        scratch_shapes=[pltpu.SemaphoreType.DMA((15,)), pltpu.SemaphoreType.DMA((15,))],
    )(win_g, wout_g, small_sh)


def _hbm(a):
    return pltpu.with_memory_space_constraint(a, pltpu.HBM)


def _split_start(body, name, sources, landings):
    n_src, n_land = len(sources), len(landings)
    n_buf = n_src + n_land

    def kernel_body(*refs):
        ins, outs = refs[:n_buf], refs[n_buf:]
        send_sems, recv_sems, token = outs[0], outs[1], outs[2 + n_buf]
        body(ins[:n_src], ins[n_src:], send_sems, recv_sems)
        token[...] = jnp.zeros_like(token)

    bufs = [_hbm(a) for a in sources] + [
        _hbm(lax.empty(s.shape, s.dtype) if isinstance(s, jax.ShapeDtypeStruct) else s) for s in landings]
    n_sem = getattr(body, "n_copies")
    out = pl.pallas_call(
        kernel_body, name=name,
        out_shape=(pltpu.SemaphoreType.DMA((n_sem,)), pltpu.SemaphoreType.DMA((n_sem,)),
                   *[pltpu.HBM(b.shape, b.dtype) for b in bufs], jax.ShapeDtypeStruct((8, LANE), F32)),
        in_specs=(HBM_SPEC,) * n_buf,
        out_specs=(SEM_SPEC, SEM_SPEC, *([HBM_SPEC] * n_buf), pl.BlockSpec(memory_space=pltpu.VMEM)),
        input_output_aliases={k: 2 + k for k in range(n_buf)},
        compiler_params=pltpu.CompilerParams(has_side_effects=SIDE_EFFECT),
    )(*bufs)
    return out[0], out[1], list(out[2:2 + n_src]), list(out[2 + n_src:2 + n_buf]), out[2 + n_buf]


def _split_wait(body, name, send_sems, recv_sems, sources, landings, after):
    n_src, n_land = len(sources), len(landings)
    n_buf = n_src + n_land

    def kernel_body(*refs):
        ins = refs[:n_buf]
        body(ins[:n_src], ins[n_src:], refs[n_buf], refs[n_buf + 1])

    bufs = list(sources) + list(landings)
    out = pl.pallas_call(
        kernel_body, name=name,
        out_shape=tuple(pltpu.HBM(b.shape, b.dtype) for b in bufs),
        in_specs=(*([HBM_SPEC] * n_buf), SEM_SPEC, SEM_SPEC, pl.BlockSpec(memory_space=pl.ANY)),
        out_specs=(HBM_SPEC,) * n_buf,
        input_output_aliases={k: k for k in range(n_buf)},
        compiler_params=pltpu.CompilerParams(has_side_effects=SIDE_EFFECT),
    )(*bufs, send_sems, recv_sems, after)
    return list(out[:n_src]), list(out[n_src:])


def _gather_rest_copies(start):
    def body(srcs, lands, send_sems, recv_sems):
        del srcs
        x, y, c = _mesh_pos()
        me = 2 * x + y
        for j, (cx, cy, q) in enumerate(_other_chips(x, y)):
            to = (cx, cy, c)
            for k, gathered in enumerate(lands):
                if start:
                    _remote(gathered.at[me, c], gathered.at[me, c], send_sems, recv_sems, 3 * k + j, to).start()
                else:
                    cp = _remote(gathered.at[me, c], gathered.at[q, c], send_sems, recv_sems, 3 * k + j, to)
                    cp.wait_send()
                    cp.wait_recv()

    body.n_copies = 6
    return body


def _gather_rest_forward(win_g, wout_g):
    def body(win_in, wout_in, win_g, wout_g, send_sems, recv_sems):
        del win_in, wout_in
        x, y, c = _mesh_pos()
        sibling = (x, y, 1 - c)
        passed = []
        for j, (_, _, q) in enumerate(_other_chips(x, y)):
            got_in = win_g.at[q, c]
            got_out = wout_g.at[q, c]
            passed.append(_remote(got_in, got_in, send_sems, recv_sems, j, sibling))
            passed.append(_remote(got_out, got_out, send_sems, recv_sems, 3 + j, sibling))
        for cp in passed:
            cp.start()
        for j, (_, _, q) in enumerate(_other_chips(x, y)):
            from_in = win_g.at[q, 1 - c]
            from_out = wout_g.at[q, 1 - c]
            _remote(from_in, from_in, send_sems, recv_sems, j, sibling).wait_recv()
            _remote(from_out, from_out, send_sems, recv_sems, 3 + j, sibling).wait_recv()
        for cp in passed:
            cp.wait_send()

    return pl.pallas_call(
        body, name="gather_rest_forward",
        in_specs=[HBM_SPEC] * 2, out_specs=[HBM_SPEC] * 2,
        out_shape=[jax.ShapeDtypeStruct(win_g.shape, win_g.dtype), jax.ShapeDtypeStruct(wout_g.shape, wout_g.dtype)],
        input_output_aliases={0: 0, 1: 1},
        scratch_shapes=[pltpu.SemaphoreType.DMA((6,)), pltpu.SemaphoreType.DMA((6,))],
    )(win_g, wout_g)


def _allreduce_small(groups, after):
    pieces = [p for _, _, _, members in groups for _, p in members]
    n_in, n_g = len(pieces), len(groups)

    def body(*refs):
        ins = refs[:n_in]
        outs = refs[-(2 * n_g + 2):-(n_g + 2)]
        alls = refs[-(n_g + 2):-2]
        send_sems, recv_sems = refs[-2:]
        x, y, c = _mesh_pos()
        me = 4 * x + 2 * y + c
        sibling = (x, y, 1 - c)
        chips = _other_chips(x, y)

        k = 0
        for (rows, cols, dtype, members), all_ref in zip(groups, alls):
            if any(piece.shape[1] < cols for _, piece in members) or sum(p.shape[0] for _, p in members) < rows:
                all_ref[me] = jnp.zeros((rows, cols), dtype)
            for first, piece in members:
                n, width = piece.shape
                all_ref[me, first:first + n, 0:width] = ins[k][...].astype(dtype)
                k += 1

        sends, passed = [], []
        for g, all_ref in enumerate(alls):
            sends.append(_remote(all_ref.at[me], all_ref.at[me], send_sems, recv_sems, 7 * g, sibling))
            for j, (cx, cy, _) in enumerate(chips):
                sends.append(_remote(all_ref.at[me], all_ref.at[me], send_sems, recv_sems, 7 * g + 1 + j, (cx, cy, c)))
        for cp in sends:
            cp.start()
        for j, (cx, cy, _) in enumerate(chips):
            for g, all_ref in enumerate(alls):
                got = all_ref.at[4 * cx + 2 * cy + c]
                _remote(got, got, send_sems, recv_sems, 7 * g + 1 + j, sibling).wait_recv()
                cp = _remote(got, got, send_sems, recv_sems, 7 * g + 4 + j, sibling)
                cp.start()
                passed.append(cp)
        for g, all_ref in enumerate(alls):
            got = all_ref.at[4 * x + 2 * y + 1 - c]
            _remote(got, got, send_sems, recv_sems, 7 * g, sibling).wait_recv()
            for j, (cx, cy, _) in enumerate(chips):
                got = all_ref.at[4 * cx + 2 * cy + 1 - c]
                _remote(got, got, send_sems, recv_sems, 7 * g + 4 + j, sibling).wait_recv()
        for cp in sends + passed:
            cp.wait_send()
        for o_ref, all_ref in zip(outs, alls):
            total = all_ref[0].astype(F32)
            for dev in range(1, 8):
                total = total + all_ref[dev].astype(F32)
            o_ref[...] = total

    vmem = pl.BlockSpec(memory_space=pltpu.VMEM)
    return pl.pallas_call(
        body, name="allreduce_small",
        in_specs=[vmem] * n_in + _after_spec(after),
        out_specs=[vmem] * n_g,
        out_shape=[jax.ShapeDtypeStruct((rows, cols), F32) for rows, cols, _, _ in groups],
        scratch_shapes=[pltpu.VMEM((8, rows, cols), dtype) for rows, cols, dtype, _ in groups]
        + [pltpu.SemaphoreType.DMA((7 * n_g,)), pltpu.SemaphoreType.DMA((7 * n_g,))],
        compiler_params=_params(None, 56),
    )(*pieces, *_after_args(after))


def _pair_copies(start):
    def body(srcs, lands, send_sems, recv_sems):
        x, y, c = _mesh_pos()
        sibling = (x, y, 1 - c)
        for k, (g_ref, r_ref) in enumerate(zip(srcs, lands)):
            half = g_ref.shape[1] // 2
            cp = _remote(g_ref.at[:, pl.ds((1 - c) * half, half), :], r_ref, send_sems, recv_sems, k, sibling)
            if start:
                cp.start()
            else:
                cp.wait_send()
                cp.wait_recv()

    body.n_copies = 2
    return body


def _pair_sum(g, r, pos_arr, name):
    nq, rows, cols = r.shape
    tr = min(rows, 256)
    nt = rows // tr

    def body(pos_ref, g_ref, r_ref, ob_ref, own_ref):
        total = g_ref[...] + r_ref[...]
        ob_ref[...] = total.astype(BF16)

        @pl.when(pl.program_id(1) == pos_ref[1])
        def _():
            own_ref[...] = total

    blk = (None, tr, cols)
    grid_spec = pltpu.PrefetchScalarGridSpec(
        num_scalar_prefetch=1, grid=(nt, nq),
        in_specs=[pl.BlockSpec(blk, lambda t, q, pos: (q, pos[0] * nt + t, 0)),
                  pl.BlockSpec(blk, lambda t, q, pos: (q, t, 0))],
        out_specs=[pl.BlockSpec(blk, lambda t, q, pos: (q, t, 0)),
                   pl.BlockSpec((tr, cols), lambda t, q, pos: (t, 0))])
    return pl.pallas_call(
        body, name=name, grid_spec=grid_spec,
        out_shape=[jax.ShapeDtypeStruct(r.shape, BF16), jax.ShapeDtypeStruct((rows, cols), F32)],
        compiler_params=_params(("arbitrary",) * 2),
    )(pos_arr, g, r)


def _chip_copies(start):
    def body(srcs, lands, send_sems, recv_sems):
        pin, pout = srcs
        rin, rout = lands
        x, y, c = _mesh_pos()
        for j, (cx, cy, q) in enumerate(_other_chips(x, y)):
            to = (cx, cy, c)
            for k, (src, dst) in enumerate(((pin, rin), (pout, rout))):
                cp = _remote(src.at[q], dst.at[j], send_sems, recv_sems, 3 * k + j, to)
                if start:
                    cp.start()
                else:
                    cp.wait_send()
                    cp.wait_recv()

    body.n_copies = 6
    return body


def _chip_sum(own, r, pos_arr, name):
    rows, cols = own.shape
    tr = min(rows, 256)

    def body(pos_ref, p_ref, r0_ref, r1_ref, r2_ref, o_ref):
        o_ref[...] = ((p_ref[...] + r0_ref[...].astype(F32)) + r1_ref[...].astype(F32)) + r2_ref[...].astype(F32)

    def peer(j):
        return pl.BlockSpec((None, tr, cols), lambda t, pos: (j, t, 0))

    grid_spec = pltpu.PrefetchScalarGridSpec(
        num_scalar_prefetch=1, grid=(rows // tr,),
        in_specs=[pl.BlockSpec((tr, cols), lambda t, pos: (t, 0)), peer(0), peer(1), peer(2)],
        out_specs=pl.BlockSpec((None, tr, cols), lambda t, pos: (pos[0], t, 0)))
    return pl.pallas_call(
        body, name=name, grid_spec=grid_spec, out_shape=jax.ShapeDtypeStruct((2, rows, cols), F32),
        compiler_params=_params(("arbitrary",)),
    )(pos_arr, own, r, r, r)


def _pair_share(gin, gout, l):
    def body(gin_in, gout_in, gin_ref, gout_ref, send_sems, recv_sems):
        del gin_in, gout_in
        x, y, c = _mesh_pos()
        sibling = (x, y, 1 - c)
        sends = [_remote(gin_ref.at[c], gin_ref.at[c], send_sems, recv_sems, 0, sibling),
                 _remote(gout_ref.at[c], gout_ref.at[c], send_sems, recv_sems, 1, sibling)]
        for cp in sends:
            cp.start()
        _remote(gin_ref.at[1 - c], gin_ref.at[1 - c], send_sems, recv_sems, 0, sibling).wait_recv()
        _remote(gout_ref.at[1 - c], gout_ref.at[1 - c], send_sems, recv_sems, 1, sibling).wait_recv()
        for cp in sends:
            cp.wait_send()

    return pl.pallas_call(
        body, name=f"pair_share_{l}",
        in_specs=[HBM_SPEC, HBM_SPEC], out_specs=[HBM_SPEC, HBM_SPEC],
        out_shape=[jax.ShapeDtypeStruct(gin.shape, F32), jax.ShapeDtypeStruct(gout.shape, F32)],
        input_output_aliases={0: 0, 1: 1},
        scratch_shapes=[pltpu.SemaphoreType.DMA((2,)), pltpu.SemaphoreType.DMA((2,))],
    )(gin, gout)


def _adamw_large(w, m, v, g, i, prev, name):
    _, rows, cols = w.shape
    half = rows // 2
    tr = min(half, 256)
    nt = half // tr

    def body(w_ref, m_ref, v_ref, g_ref, *rest):
        go_ref, d_ref, mo_ref, vo_ref = rest[-4:]
        gv = g_ref[...]
        go_ref[...] = gv
        d_ref[...], mo_ref[...], vo_ref[...] = _adamw_math(w_ref[...], gv, m_ref[...], v_ref[...])

    full = pl.BlockSpec((None, tr, cols), lambda h, t: (i, h * nt + t, 0))
    out = jax.ShapeDtypeStruct(w.shape, F32)
    carried = [] if prev is None else list(prev)
    return pl.pallas_call(
        body, name=name, grid=(2, nt),
        in_specs=[full, full, full, pl.BlockSpec((None, tr, cols), lambda h, t: (h, t, 0))]
        + [pl.BlockSpec(memory_space=pl.ANY)] * len(carried),
        out_specs=[full] * 4, out_shape=[out] * 4,
        input_output_aliases={4 + k: k for k in range(len(carried))},
        compiler_params=_params(("arbitrary",) * 2),
    )(w, m, v, g, *carried)


def _adamw(w, g, m, v, name):
    shape = w.shape
    w2, g2, m2, v2 = (t.reshape(-1, shape[-1]) for t in (w, g, m, v))
    rows, cols = w2.shape
    tr = 256 if rows % 256 == 0 else rows

    def body(w_ref, g_ref, m_ref, v_ref, d_ref, mo_ref, vo_ref):
        d_ref[...], mo_ref[...], vo_ref[...] = _adamw_math(w_ref[...], g_ref[...], m_ref[...], v_ref[...])

    blk = pl.BlockSpec((tr, cols), lambda i: (i, 0))
    out = jax.ShapeDtypeStruct((rows, cols), F32)
    d, mo, vo = pl.pallas_call(
        body, name=name, grid=(rows // tr,), in_specs=[blk] * 4, out_specs=[blk] * 3, out_shape=[out] * 3,
        compiler_params=_params(("arbitrary",)),
    )(w2, g2, m2, v2)
    return d.reshape(shape), mo.reshape(shape), vo.reshape(shape)


def _layer_slot(l):
    return (l % 2) * 2 + l // 2


def kernel(x, ln_g, ln_b, w_in_even, w_out_even, pool_w, pool_scale, sconv_w, sconv_b, w_in_odd, w_out_odd, sgu_ln_g, sgu_ln_b, sgu_w, sgu_b, dconv_w, dconv_b, dnorm_g, dnorm_b, loss_target, m_ln_g, m_ln_b, m_w_in_even, m_w_out_even, m_pool_w, m_pool_scale, m_sconv_w, m_sconv_b, m_w_in_odd, m_w_out_odd, m_sgu_ln_g, m_sgu_ln_b, m_sgu_w, m_sgu_b, m_dconv_w, m_dconv_b, m_dnorm_g, m_dnorm_b, v_ln_g, v_ln_b, v_w_in_even, v_w_out_even, v_pool_w, v_pool_scale, v_sconv_w, v_sconv_b, v_w_in_odd, v_w_out_odd, v_sgu_ln_g, v_sgu_ln_b, v_sgu_w, v_sgu_b, v_dconv_w, v_dconv_b, v_dnorm_g, v_dnorm_b):
    weights = dict(ln_g=ln_g, ln_b=ln_b, w_in_even=w_in_even, w_out_even=w_out_even, pool_w=pool_w,
                   pool_scale=pool_scale, sconv_w=sconv_w, sconv_b=sconv_b, w_in_odd=w_in_odd, w_out_odd=w_out_odd,
                   sgu_ln_g=sgu_ln_g, sgu_ln_b=sgu_ln_b, sgu_w=sgu_w, sgu_b=sgu_b, dconv_w=dconv_w,
                   dconv_b=dconv_b, dnorm_g=dnorm_g, dnorm_b=dnorm_b)
    moments_m = dict(ln_g=m_ln_g, ln_b=m_ln_b, w_in_even=m_w_in_even, w_out_even=m_w_out_even, pool_w=m_pool_w,
                     pool_scale=m_pool_scale, sconv_w=m_sconv_w, sconv_b=m_sconv_b, w_in_odd=m_w_in_odd,
                     w_out_odd=m_w_out_odd, sgu_ln_g=m_sgu_ln_g, sgu_ln_b=m_sgu_ln_b, sgu_w=m_sgu_w, sgu_b=m_sgu_b,
                     dconv_w=m_dconv_w, dconv_b=m_dconv_b, dnorm_g=m_dnorm_g, dnorm_b=m_dnorm_b)
    moments_v = dict(ln_g=v_ln_g, ln_b=v_ln_b, w_in_even=v_w_in_even, w_out_even=v_w_out_even, pool_w=v_pool_w,
                     pool_scale=v_pool_scale, sconv_w=v_sconv_w, sconv_b=v_sconv_b, w_in_odd=v_w_in_odd,
                     w_out_odd=v_w_out_odd, sgu_ln_g=v_sgu_ln_g, sgu_ln_b=v_sgu_ln_b, sgu_w=v_sgu_w, sgu_b=v_sgu_b,
                     dconv_w=v_dconv_w, dconv_b=v_dconv_b, dnorm_g=v_dnorm_g, dnorm_b=v_dnorm_b)
    names = list(weights)

    xd, yd, cd = _mesh_pos()
    chip = 2 * xd + yd
    pos_arr = jnp.stack([cd, chip]).astype(jnp.int32)

    small_sh = jnp.concatenate(
        [sconv_w.reshape(6, HEAD), sgu_ln_g, sgu_ln_b, dconv_b, dnorm_g, dnorm_b, dconv_w.reshape(62, HEAD),
         jnp.zeros((2, HEAD), F32), pool_w.reshape(512, HEAD)], axis=0)
    win_first, wout_first, small_g = _gather_weights(
        _cast_own(w_in_even[0:1], pos_arr, "cast_win_first"), _cast_own(w_out_even[0:1], pos_arr, "cast_wout_first"),
        small_sh)
    small_g = lax.dynamic_update_slice(small_g, small_sh[None], (chip, 0, 0))
    later_in = jnp.concatenate([w_in_even[1:2], w_in_odd], axis=0)
    later_out = jnp.concatenate([w_out_even[1:2], w_out_odd], axis=0)
    g_send, g_recv, _, g_lands, g_token = _split_start(
        _gather_rest_copies(True), "gather_rest_start", [],
        [_cast_own(later_in, pos_arr, "cast_win_rest"), _cast_own(later_out, pos_arr, "cast_wout_rest")])

    def layer_weights(slot):
        return (win_first, wout_first, 0) if slot == 0 else (win_rest, wout_rest, slot - 1)

    def full_rows(lo, n):
        return jnp.transpose(small_g[:, lo:lo + n], (1, 0, 2)).reshape(n, D)

    sconv_w_f = full_rows(Q_SCONV_W, 6).reshape(2, SHORT_K, D)
    sln_g_f = full_rows(Q_SLN_G, 2)
    sln_b_f = full_rows(Q_SLN_B, 2)
    dconv_b_f = full_rows(Q_DCONV_B, 2)
    dn_g_f = full_rows(Q_DN_G, 2)
    dn_b_f = full_rows(Q_DN_B, 2)
    dconv_w_f = jnp.transpose(full_rows(Q_DCONV_W, 62).reshape(2, CONV_K, LANE_BLOCKS, LANE), (0, 2, 1, 3))
    pool_w_f = jnp.transpose(small_g[:, Q_POOL_W:].reshape(NQ, 2, 4, 64, HEAD), (1, 2, 0, 3, 4)).reshape(2, 4, HEAD, HEAD)
    pool_w_b = pool_w_f.astype(BF16)
    pool_wt_b = jnp.swapaxes(pool_w_f, 2, 3).astype(BF16)
    idx = jnp.arange(SGU_BLOCK)
    mask = (idx[None, :] // 64) <= (idx[:, None] // 64)
    ws_f = jnp.where(mask[None, None], sgu_w, 0.0)
    ws_b = ws_f.astype(BF16)
    wst_b = jnp.swapaxes(ws_f, 2, 3).astype(BF16)

    def row(a, i):
        return a[i:i + 1]

    residual = (x[0], jnp.ones((1, D), F32), jnp.zeros((1, D), F32))
    x_b = x[0].astype(BF16)
    saved = []
    conv_saved = {}
    for l in range(NL):
        i, slot = l // 2, _layer_slot(l)
        if l == 1:
            _, g_lands = _split_wait(_gather_rest_copies(False), "gather_rest_wait", g_send, g_recv, [], g_lands, x_b)
            win_rest, wout_rest = _gather_rest_forward(g_lands[0], g_lands[1])
        win_g, wout_g, k = layer_weights(slot)
        z = _proj_in(x_b, win_g, k, l, g_token if l == 0 else None)
        if l % 2 == 0:
            ycat = _even_fwd(z, pool_w_b[i], row(pool_scale, i), sconv_w_f[i], row(sconv_b, i), l)
        else:
            ycat, conv_hat, conv_rstd = _odd_fwd(
                z, row(sln_g_f, i), row(sln_b_f, i), ws_b[i], _sgu_bias_rows(sgu_b[i]),
                dconv_w_f[i], row(dconv_b_f, i), row(dn_g_f, i), row(dn_b_f, i), l)
            conv_saved[l] = (conv_hat, conv_rstd)
        x_next_b, xhat, rstd = _proj_out_ln(ycat, wout_g, k, l, *residual, row(ln_g, l), row(ln_b, l))
        saved.append((x_b, z, ycat, xhat, rstd))
        residual = (xhat, row(ln_g, l), row(ln_b, l))
        x_b = x_next_b

    small = {}
    d_ln_g = [None] * NL
    d_ln_b = [None] * NL
    large = {"w_in_even": None, "w_out_even": None, "w_in_odd": None, "w_out_odd": None}
    pending = None
    token = None

    def finish(exchange, after):
        lx, send, recv, srcs, lands, own_in, own_out = exchange
        _, (r_in, r_out) = _split_wait(_chip_copies(False), f"chip_wait_{lx}", send, recv, srcs, lands, after)
        fin = _chip_sum(own_in, r_in, pos_arr, f"chip_sum_in_{lx}")
        fout = _chip_sum(own_out, r_out, pos_arr, f"chip_sum_out_{lx}")
        gs_in, gs_out = _pair_share(fin, fout, lx)
        kind = "even" if lx % 2 == 0 else "odd"
        for nm, gs in ((f"w_in_{kind}", gs_in), (f"w_out_{kind}", gs_out)):
            large[nm] = _adamw_large(weights[nm], moments_m[nm], moments_v[nm], gs, lx // 2, large[nm],
                                     f"adamw_{nm}_{lx // 2}")
        return large[f"w_out_{kind}"][0]

    for l in reversed(range(NL)):
        i, slot = l // 2, _layer_slot(l)
        win_g, wout_g, k = layer_weights(slot)
        xin_b, z, ycat, xhat, rstd = saved[l]
        if l == NL - 1:
            loss_part, dr, dr_b, d_ln_g[l], d_ln_b[l] = _loss_ln_bwd(
                xhat, rstd, row(ln_g, l), row(ln_b, l), loss_target[0], l)
            loss = lax.psum(loss_part[0, 0], ("x", "y", "c"))
        else:
            dr, dr_b, d_ln_g[l], d_ln_b[l] = _ln_bwd(dxn, xhat, rstd, row(ln_g, l), l, token)
        dy = _dycat(dr_b, wout_g, k, l)
        gout = _dwout(ycat, dr_b, l).reshape(NQ, RQ, D)
        if l % 2 == 0:
            dz, d_pw, d_ps, d_cw, d_cb = _even_bwd(z, dy, pool_w_b[i], pool_wt_b[i], row(pool_scale, i),
                                                   sconv_w_f[i], row(sconv_b, i), l)
            small[("pool_w", i)] = d_pw
            small[("pool_scale", i)] = d_ps
            small[("sconv_w", i)] = d_cw
            small[("sconv_b", i)] = d_cb
        else:
            dz, d_lg, d_lb, d_ws, d_sb, d_cw, d_cb, d_ng, d_nb = _odd_bwd(
                z, dy, *conv_saved[l], row(sln_g_f, i), row(sln_b_f, i), ws_b[i], wst_b[i],
                _sgu_bias_rows(sgu_b[i]), dconv_w_f[i], row(dn_g_f, i), row(dn_b_f, i), l)
            small[("sgu_ln_g", i)] = d_lg
            small[("sgu_ln_b", i)] = d_lb
            small[("sgu_w", i)] = jnp.where(mask[None], d_ws, 0.0)
            small[("sgu_b", i)] = d_sb
            small[("dconv_w", i)] = jnp.transpose(d_cw, (1, 0, 2)).reshape(CONV_K, D)
            small[("dconv_b", i)] = d_cb
            small[("dnorm_g", i)] = d_ng
            small[("dnorm_b", i)] = d_nb
        gin = _dwin(xin_b, dz, l)
        p_send, p_recv, p_srcs, p_lands, p_token = _split_start(
            _pair_copies(True), f"pair_start_{l}", [gin, gout],
            [jax.ShapeDtypeStruct((NQ, D // 2, WQ), F32), jax.ShapeDtypeStruct((NQ, RQ // 2, D), F32)])
        if l > 0:
            dxn = _dx(dz, win_g, k, l, dr, p_token)
            if pending is not None:
                finish(pending, dxn)
            wait_after = dxn
        else:
            wait_after = finish(pending, p_token)
        (gin, gout), (rin, rout) = _split_wait(_pair_copies(False), f"pair_wait_{l}", p_send, p_recv,
                                               p_srcs, p_lands, wait_after)
        pin_b, pin_own = _pair_sum(gin, rin, pos_arr, f"pair_sum_in_{l}")
        pout_b, pout_own = _pair_sum(gout, rout, pos_arr, f"pair_sum_out_{l}")
        send, recv, srcs, lands, token = _split_start(
            _chip_copies(True), f"chip_start_{l}", [pin_b, pout_b],
            [jax.ShapeDtypeStruct((3,) + pin_b.shape[1:], BF16), jax.ShapeDtypeStruct((3,) + pout_b.shape[1:], BF16)])
        if l == 0:
            dxn = _dx(dz, win_g, k, l, dr, token)
        pending = (l, send, recv, srcs, lands, pin_own, pout_own)
    grad_x = dxn[None]

    def both(name, first, step):
        return [(first, small[(name, 0)]), (first + step, small[(name, 1)])]

    vectors = ([(R_LN_G + l, d_ln_g[l]) for l in range(NL)] + [(R_LN_B + l, d_ln_b[l]) for l in range(NL)]
               + both("pool_scale", R_PSCALE, 1) + both("sconv_b", R_SCONV_B, 1) + both("sconv_w", R_SCONV_W, SHORT_K)
               + both("sgu_ln_g", R_SLN_G, 1) + both("sgu_ln_b", R_SLN_B, 1) + both("dconv_b", R_DCONV_B, 1)
               + both("dnorm_g", R_DN_G, 1) + both("dnorm_b", R_DN_B, 1) + both("dconv_w", R_DCONV_W, CONV_K)
               + both("sgu_b", R_SGU_B, 4))
    sgu_w_rows = 4 * SGU_BLOCK
    pool_w_rows = 4 * HEAD
    total, total_sgu_w, total_pool_w = _allreduce_small(
        [(R_VECTORS, D, F32, vectors),
         (2 * sgu_w_rows, SGU_BLOCK, BF16,
          [(i * sgu_w_rows, small[("sgu_w", i)].reshape(sgu_w_rows, SGU_BLOCK)) for i in range(2)]),
         (2 * pool_w_rows, HEAD, BF16,
          [(i * pool_w_rows, small[("pool_w", i)].reshape(pool_w_rows, HEAD)) for i in range(2)])],
        dxn)
    finish(pending, total)

    def mine(a):
        return lax.dynamic_slice_in_dim(a, chip * HEAD, HEAD, axis=a.ndim - 1)

    grads = {
        "ln_g": total[R_LN_G:R_LN_G + 4],
        "ln_b": total[R_LN_B:R_LN_B + 4],
        "pool_scale": total[R_PSCALE:R_PSCALE + 2],
        "sconv_b": total[R_SCONV_B:R_SCONV_B + 2],
        "sconv_w": mine(total[R_SCONV_W:R_SCONV_W + 6].reshape(2, SHORT_K, D)),
        "sgu_ln_g": mine(total[R_SLN_G:R_SLN_G + 2]),
        "sgu_ln_b": mine(total[R_SLN_B:R_SLN_B + 2]),
        "dconv_b": mine(total[R_DCONV_B:R_DCONV_B + 2]),
        "dnorm_g": mine(total[R_DN_G:R_DN_G + 2]),
        "dnorm_b": mine(total[R_DN_B:R_DN_B + 2]),
        "dconv_w": mine(total[R_DCONV_W:R_DCONV_W + 62].reshape(2, CONV_K, D)),
        "sgu_b": total[R_SGU_B:R_SGU_B + 8, 0:SGU_BLOCK].reshape(2, 4, SGU_BLOCK),
        "sgu_w": total_sgu_w.reshape(2, 4, SGU_BLOCK, SGU_BLOCK),
        "pool_w": lax.dynamic_slice_in_dim(total_pool_w.reshape(2, 4, HEAD, HEAD), chip * 64, 64, axis=2),
    }

    deltas, new_m, new_v = {}, {}, {}
    for name in names:
        if name in large:
            grads[name], deltas[name], new_m[name], new_v[name] = large[name]
        else:
            deltas[name], new_m[name], new_v[name] = _adamw(
                weights[name], grads[name], moments_m[name], moments_v[name], f"adamw_{name}")

    return (loss, grad_x, *[grads[n] for n in names], *[deltas[n] for n in names],
            *[new_m[n] for n in names], *[new_v[n] for n in names])
```

```python
import jax
import jax.numpy as jnp
from jax import lax
from jax.experimental import pallas as pl
from jax.experimental.pallas import tpu as pltpu

F32 = jnp.float32
BF16 = jnp.bfloat16
MXU_DTYPE = BF16

D = 1024
DZ = 6144
DY = 2048
NQ = 4
WQ = DZ // NQ
RQ = DY // NQ
NL = 4
ALPHA = (2 * NL) ** 0.25
LN_EPS = 1e-5
CONV_K = 31
SHORT_K = 3
SGU_BLOCK = 128
HEAD = 256
POOL_HALO = 16
CONV_HALO = 32
LANE = 128
SUBLANES = 8
CONV_ROWS = 32
LANE_BLOCKS = 8
MIB = 1024 * 1024

ADAM_LR = 0.001
ADAM_B1 = 0.9
ADAM_B2 = 0.999
ADAM_EPS = 1e-08
ADAM_WD = 0.01
ADAM_STEP = 10

NN = ((1,), (0,))
NT = ((1,), (1,))
TN = ((0,), (0,))
MESH = pl.DeviceIdType.MESH
HBM_SPEC = pl.BlockSpec(memory_space=pltpu.HBM)
SEM_SPEC = pl.BlockSpec(memory_space=pltpu.SEMAPHORE)
SIDE_EFFECT = pltpu.SideEffectType.DATAFLOW_SIDE_EFFECTING

R_LN_G, R_LN_B, R_PSCALE, R_SCONV_B, R_SCONV_W = 0, 4, 8, 10, 12
R_SLN_G, R_SLN_B, R_DCONV_B, R_DN_G, R_DN_B, R_DCONV_W = 18, 20, 22, 24, 26, 28
R_SGU_B, R_VECTORS = 90, 104
Q_SCONV_W, Q_SLN_G, Q_SLN_B, Q_DCONV_B, Q_DN_G, Q_DN_B, Q_DCONV_W, Q_POOL_W, Q_ROWS = 0, 6, 8, 10, 12, 14, 16, 80, 592


def _dot(a, b, dims):
    return lax.dot_general(a.astype(MXU_DTYPE), b.astype(MXU_DTYPE), (dims, ((), ())),
                           preferred_element_type=F32)


def _params(semantics=None, vmem_mib=48):
    return pltpu.CompilerParams(dimension_semantics=semantics, vmem_limit_bytes=vmem_mib * MIB)


def _sigmoid(v):
    return 0.5 * jnp.tanh(0.5 * v) + 0.5


def _silu_and_grad(v):
    s = _sigmoid(v)
    return v * s, s * (1.0 + v * (1.0 - s))


def _row_mean(v):
    acc = v[:, 0:LANE]
    for j in range(1, v.shape[1] // LANE):
        acc = acc + v[:, j * LANE:(j + 1) * LANE]
    return jnp.sum(acc, axis=-1, keepdims=True) * (1.0 / v.shape[1])


def _ln_stats(v):
    mu = _row_mean(v)
    vc = v - mu
    var = _row_mean(vc * vc)
    rstd = lax.rsqrt(var + LN_EPS)
    return vc * rstd, rstd


def _ln_bwd_rows(dxhat, xhat, rstd):
    m1 = _row_mean(dxhat)
    m2 = _row_mean(dxhat * xhat)
    return rstd * (dxhat - m1 - xhat * m2)


def _colsum(v):
    return jnp.sum(v, axis=0, keepdims=True)


def _section(ref, k):
    return ref[:, k * D:(k + 1) * D].astype(F32)


def _adamw_math(w, g, m, v):
    m_new = ADAM_B1 * m + (1.0 - ADAM_B1) * g
    v_new = ADAM_B2 * v + (1.0 - ADAM_B2) * (g * g)
    m_hat = m_new / (1.0 - ADAM_B1 ** ADAM_STEP)
    v_hat = v_new / (1.0 - ADAM_B2 ** ADAM_STEP)
    return -ADAM_LR * (m_hat / (jnp.sqrt(v_hat) + ADAM_EPS) + ADAM_WD * w), m_new, v_new


def _mesh_pos():
    return lax.axis_index("x"), lax.axis_index("y"), lax.axis_index("c")


def _after_spec(after):
    return [] if after is None else [pl.BlockSpec(memory_space=pl.ANY)]


def _after_args(after):
    return [] if after is None else [after]


def _proj_in(xb, win_g, k, l, after=None):
    s = xb.shape[0]
    tm = min(s, 1024)

    def body(x_ref, w_ref, *rest):
        rest[-1][...] = _dot(x_ref[...], w_ref[...].reshape(D, WQ), NN).astype(BF16)

    return pl.pallas_call(
        body, name=f"proj_in_{l}", grid=(NQ, s // tm),
        in_specs=[pl.BlockSpec((tm, D), lambda q, m: (m, 0)),
                  pl.BlockSpec((None, 2, None, D // 2, WQ), lambda q, m: (q, 0, k, 0, 0))] + _after_spec(after),
        out_specs=pl.BlockSpec((tm, WQ), lambda q, m: (m, q)),
        out_shape=jax.ShapeDtypeStruct((s, DZ), BF16),
        compiler_params=_params(("arbitrary", "arbitrary")),
    )(xb, win_g, *_after_args(after))


def _proj_out_ln(ycat, wout_g, k, l, res, res_g, res_b, g, b):
    s = res.shape[0]
    tm = min(s, 512)

    def body(y_ref, w_ref, r_ref, rg_ref, rb_ref, g_ref, b_ref, xb_ref, xh_ref, rs_ref):
        y = _dot(y_ref[...], w_ref[...].reshape(DY, D), NN)
        x = r_ref[...] * rg_ref[...] + rb_ref[...]
        xhat, rstd = _ln_stats(ALPHA * x + y)
        xb_ref[...] = (xhat * g_ref[...] + b_ref[...]).astype(BF16)
        xh_ref[...] = xhat
        rs_ref[...] = rstd

    row = lambda m: (m, 0)
    vec = pl.BlockSpec((1, D), lambda m: (0, 0))
    return pl.pallas_call(
        body, name=f"proj_out_ln_{l}", grid=(s // tm,),
        in_specs=[pl.BlockSpec((tm, DY), row),
                  pl.BlockSpec((NQ, 2, None, RQ // 2, D), lambda m: (0, 0, k, 0, 0)),
                  pl.BlockSpec((tm, D), row), vec, vec, vec, vec],
        out_specs=[pl.BlockSpec((tm, D), row), pl.BlockSpec((tm, D), row), pl.BlockSpec((tm, 1), row)],
        out_shape=[jax.ShapeDtypeStruct((s, D), BF16), jax.ShapeDtypeStruct((s, D), F32),
                   jax.ShapeDtypeStruct((s, 1), F32)],
        compiler_params=_params(("arbitrary",)),
    )(ycat, wout_g, res, res_g, res_b, g, b)


def _loss_ln_bwd(xhat, rstd, g, b, target, l):
    s = xhat.shape[0]
    ts = min(s, 512)

    def body(xh_ref, rs_ref, g_ref, b_ref, t_ref, loss_ref, dr_ref, drb_ref, dg_ref, db_ref):
        @pl.when(pl.program_id(0) == 0)
        def _():
            loss_ref[...] = jnp.zeros_like(loss_ref)
            dg_ref[...] = jnp.zeros_like(dg_ref)
            db_ref[...] = jnp.zeros_like(db_ref)
        xhat_v = xh_ref[...]
        gain = g_ref[...]
        err = (xhat_v * gain + b_ref[...]) - t_ref[...]
        loss_ref[...] += 0.5 * jnp.sum(jnp.mean(err * err, axis=-1, keepdims=True), axis=0, keepdims=True)
        d = err * (1.0 / D)
        dr = _ln_bwd_rows(d * gain, xhat_v, rs_ref[...])
        dr_ref[...] = dr
        drb_ref[...] = dr.astype(BF16)
        dg_ref[...] += _colsum(d * xhat_v)
        db_ref[...] += _colsum(d)

    row = lambda m: (m, 0)
    fixed = lambda m: (0, 0)
    vec = pl.BlockSpec((1, D), fixed)
    return pl.pallas_call(
        body, name=f"loss_ln_bwd_{l}", grid=(s // ts,),
        in_specs=[pl.BlockSpec((ts, D), row), pl.BlockSpec((ts, 1), row), vec, vec, pl.BlockSpec((ts, D), row)],
        out_specs=[pl.BlockSpec((1, 1), fixed), pl.BlockSpec((ts, D), row), pl.BlockSpec((ts, D), row), vec, vec],
        out_shape=[jax.ShapeDtypeStruct((1, 1), F32), jax.ShapeDtypeStruct((s, D), F32),
                   jax.ShapeDtypeStruct((s, D), BF16), jax.ShapeDtypeStruct((1, D), F32),
                   jax.ShapeDtypeStruct((1, D), F32)],
        compiler_params=_params(("arbitrary",)),
    )(xhat, rstd, g, b, target)


def _ln_bwd(dxn, xhat, rstd, g, l, after=None):
    s = dxn.shape[0]
    ts = min(s, 512)

    def body(d_ref, xh_ref, rs_ref, g_ref, *rest):
        dr_ref, drb_ref, dg_ref, db_ref = rest[-4:]

        @pl.when(pl.program_id(0) == 0)
        def _():
            dg_ref[...] = jnp.zeros_like(dg_ref)
            db_ref[...] = jnp.zeros_like(db_ref)
        d = d_ref[...]
        xhat_v = xh_ref[...]
        dr = _ln_bwd_rows(d * g_ref[...], xhat_v, rs_ref[...])
        dr_ref[...] = dr
        drb_ref[...] = dr.astype(BF16)
        dg_ref[...] += _colsum(d * xhat_v)
        db_ref[...] += _colsum(d)

    row = lambda m: (m, 0)
    fixed = lambda m: (0, 0)
    return pl.pallas_call(
        body, name=f"ln_bwd_{l}", grid=(s // ts,),
        in_specs=[pl.BlockSpec((ts, D), row), pl.BlockSpec((ts, D), row), pl.BlockSpec((ts, 1), row),
                  pl.BlockSpec((1, D), fixed)] + _after_spec(after),
        out_specs=[pl.BlockSpec((ts, D), row), pl.BlockSpec((ts, D), row), pl.BlockSpec((1, D), fixed),
                   pl.BlockSpec((1, D), fixed)],
        out_shape=[jax.ShapeDtypeStruct((s, D), F32), jax.ShapeDtypeStruct((s, D), BF16),
                   jax.ShapeDtypeStruct((1, D), F32), jax.ShapeDtypeStruct((1, D), F32)],
        compiler_params=_params(("arbitrary",)),
    )(dxn, xhat, rstd, g, *_after_args(after))


def _dycat(drb, wout_g, k, l):
    s = drb.shape[0]
    tm = min(s, 512)

    def body(d_ref, w_ref, o_ref):
        o_ref[...] = _dot(d_ref[...], w_ref[...].reshape(DY, D), NT).astype(BF16)

    return pl.pallas_call(
        body, name=f"dycat_{l}", grid=(s // tm,),
        in_specs=[pl.BlockSpec((tm, D), lambda m: (m, 0)),
                  pl.BlockSpec((NQ, 2, None, RQ // 2, D), lambda m: (0, 0, k, 0, 0))],
        out_specs=pl.BlockSpec((tm, DY), lambda m: (m, 0)),
        out_shape=jax.ShapeDtypeStruct((s, DY), BF16),
        compiler_params=_params(("arbitrary",)),
    )(drb, wout_g)


def _dwout(ycat, drb, l):
    s = drb.shape[0]
    tk = min(s, 1024)

    def body(y_ref, d_ref, o_ref):
        @pl.when(pl.program_id(0) == 0)
        def _():
            o_ref[...] = jnp.zeros_like(o_ref)

        o_ref[...] += _dot(y_ref[...], d_ref[...], TN)

    return pl.pallas_call(
        body, name=f"dwout_{l}", grid=(s // tk,),
        in_specs=[pl.BlockSpec((tk, DY), lambda k: (k, 0)), pl.BlockSpec((tk, D), lambda k: (k, 0))],
        out_specs=pl.BlockSpec((DY, D), lambda k: (0, 0)),
        out_shape=jax.ShapeDtypeStruct((DY, D), F32),
        compiler_params=_params(("arbitrary",)),
    )(ycat, drb)


def _dwin(xb, dzb, l):
    s = xb.shape[0]
    tk = min(s, 2048)

    def body(x_ref, d_ref, o_ref):
        @pl.when(pl.program_id(1) == 0)
        def _():
            o_ref[...] = jnp.zeros_like(o_ref)

        o_ref[...] += _dot(x_ref[...], d_ref[...], TN)

    return pl.pallas_call(
        body, name=f"dwin_{l}", grid=(NQ, s // tk),
        in_specs=[pl.BlockSpec((tk, D), lambda q, k: (k, 0)), pl.BlockSpec((tk, WQ), lambda q, k: (k, q))],
        out_specs=pl.BlockSpec((None, D, WQ), lambda q, k: (q, 0, 0)),
        out_shape=jax.ShapeDtypeStruct((NQ, D, WQ), F32),
        compiler_params=_params(("arbitrary", "arbitrary")),
    )(xb, dzb)


def _dx(dzb, win_g, k, l, dr, after=None):
    s = dzb.shape[0]
    tm = min(s, 512)

    def body(d_ref, w_ref, r_ref, *rest):
        acc = ALPHA * r_ref[...]
        for q in range(NQ):
            acc = acc + _dot(d_ref[:, q * WQ:(q + 1) * WQ], w_ref[q].reshape(D, WQ), NT)
        rest[-1][...] = acc

    return pl.pallas_call(
        body, name=f"dx_{l}", grid=(s // tm,),
        in_specs=[pl.BlockSpec((tm, DZ), lambda m: (m, 0)),
                  pl.BlockSpec((NQ, 2, None, D // 2, WQ), lambda m: (0, 0, k, 0, 0)),
                  pl.BlockSpec((tm, D), lambda m: (m, 0))] + _after_spec(after),
        out_specs=pl.BlockSpec((tm, D), lambda m: (m, 0)),
        out_shape=jax.ShapeDtypeStruct((s, D), F32),
        compiler_params=_params(("arbitrary",), 56),
    )(dzb, win_g, dr, *_after_args(after))


class _RowShifts:
    def __init__(self, ref, most_rows):
        assert ref.shape[0] == most_rows + 2 * SUBLANES
        self.ref = ref
        ref[...] = jnp.zeros(ref.shape, F32)

    def put(self, block):
        self.rows = block.shape[0]
        self.ref[SUBLANES:SUBLANES + self.rows, :] = block

    def get(self, k, causal):
        start = SUBLANES - k if causal else SUBLANES + k
        return self.ref[start:start + self.rows, :]

    def window_sums(self, block, steps, causal):
        acc = block
        for k in (1, 2, 4, 8)[:steps]:
            self.put(acc)
            acc = acc + self.get(k, causal)
        return acc


def _inv_positions(first_pos, rows):
    t1 = (lax.broadcasted_iota(jnp.int32, (rows, 1), 0) + first_pos + 1).astype(F32)
    return t1, 1.0 / t1


def _pool_inv(positions, cb):
    t1, inv_t1 = positions
    window = float(2 << (cb // 2))
    return jnp.where(t1 < window, inv_t1, 1.0 / window)


def _prev_index(ts, halo):
    return lambda i: (jnp.maximum(i * (ts // halo) - 1, 0), 0)


def _next_index(ts, halo, s):
    return lambda i: (jnp.minimum((i + 1) * (ts // halo), s // halo - 1), 0)


def _even_fwd(z, pool_w, pool_scale, sconv_w, sconv_b, l):
    s = z.shape[0]
    ts = min(s, 256)
    h = POOL_HALO

    def body(z_ref, zp_ref, pw_ref, ps_ref, cw_ref, cb_ref, o_ref, shift_ref, pooled_ref):
        i = pl.program_id(0)
        inside = i > 0
        shifts = _RowShifts(shift_ref, h + ts)
        positions = _inv_positions(i * ts, ts)

        for cb in range(LANE_BLOCKS):
            cols = slice(cb * LANE, (cb + 1) * LANE)

            def section(ref, k):
                return ref[:, k * D + cb * LANE:k * D + (cb + 1) * LANE].astype(F32)

            xa = section(z_ref, 0)
            sums = shifts.window_sums(jnp.concatenate([jnp.where(inside, section(zp_ref, 0), 0.0), xa], axis=0),
                                      cb // 2 + 1, True)
            pooled_ref[:, cols] = (sums[h:] * _pool_inv(positions, cb) - xa).astype(MXU_DTYPE)

            q_prev = jnp.where(inside, section(zp_ref, 4) * section(zp_ref, 2), 0.0)
            q_main = section(z_ref, 4) * section(z_ref, 2)
            shifts.put(jnp.concatenate([q_prev, q_main], axis=0))
            cv = (cw_ref[2:3, cols] * q_main + cw_ref[1:2, cols] * shifts.get(1, True)[h:]
                  + cw_ref[0:1, cols] * shifts.get(2, True)[h:] + cb_ref[:, cols])
            silu_gb, _ = _silu_and_grad(section(z_ref, 5))
            o_ref[:, D + cb * LANE:D + (cb + 1) * LANE] = (section(z_ref, 3) * cv * silu_gb).astype(BF16)

        for g in range(4):
            cols = slice(g * HEAD, (g + 1) * HEAD)
            p = _dot(pooled_ref[:, cols], pw_ref[g], NN)
            silu_ga, _ = _silu_and_grad(z_ref[:, D + g * HEAD:D + (g + 1) * HEAD].astype(F32))
            o_ref[:, cols] = (p * ps_ref[:, cols] * silu_ga).astype(BF16)

    fixed2 = lambda i: (0, 0)
    return pl.pallas_call(
        body, name=f"even_fwd_{l}", grid=(s // ts,),
        in_specs=[pl.BlockSpec((ts, DZ), lambda i: (i, 0)), pl.BlockSpec((h, DZ), _prev_index(ts, h)),
                  pl.BlockSpec((4, HEAD, HEAD), lambda i: (0, 0, 0)), pl.BlockSpec((1, D), fixed2),
                  pl.BlockSpec((SHORT_K, D), fixed2), pl.BlockSpec((1, D), fixed2)],
        out_specs=pl.BlockSpec((ts, DY), lambda i: (i, 0)),
        out_shape=jax.ShapeDtypeStruct((s, DY), BF16),
        scratch_shapes=[pltpu.VMEM((h + ts + 2 * SUBLANES, LANE), F32), pltpu.VMEM((ts, D), MXU_DTYPE)],
        compiler_params=_params(("arbitrary",)),
    )(z, z, pool_w, pool_scale, sconv_w, sconv_b)


def _even_bwd(z, dy, pool_w, pool_wt, pool_scale, sconv_w, sconv_b, l):
    s = z.shape[0]
    ts = min(s, 256)
    h = POOL_HALO
    n_tiles = s // ts

    def body(z_ref, zp_ref, zn_ref, dy_ref, dyn_ref, pw_ref, pwt_ref, ps_ref, cw_ref, cb_ref,
             dz_ref, dpw_ref, dps_ref, dcw_ref, dcb_ref, shift_ref, pooled_ref, p_ref, dp_ref, dpooled_ref,
             shift2_ref):
        i = pl.program_id(0)
        inside = i > 0

        @pl.when(i == 0)
        def _():
            dpw_ref[...] = jnp.zeros_like(dpw_ref)
            dps_ref[...] = jnp.zeros_like(dps_ref)
            dcw_ref[...] = jnp.zeros_like(dcw_ref)
            dcb_ref[...] = jnp.zeros_like(dcb_ref)

        shifts = _RowShifts(shift_ref, ts + h)
        positions = _inv_positions(i * ts, ts + h)
        row = lax.broadcasted_iota(jnp.int32, (ts + h, 1), 0)
        live = jnp.logical_or(i < n_tiles - 1, row < ts)

        def section(ref, k, cb):
            return ref[:, k * D + cb * LANE:k * D + (cb + 1) * LANE].astype(F32)

        for cb in range(LANE_BLOCKS):
            xa = section(z_ref, 0, cb)
            sums = shifts.window_sums(
                jnp.concatenate([jnp.where(inside, section(zp_ref, 0, cb), 0.0), xa], axis=0), cb // 2 + 1, True)
            pooled_ref[:, cb * LANE:(cb + 1) * LANE] = (
                sums[h:] * _pool_inv(positions, cb)[:ts] - xa).astype(MXU_DTYPE)
        for g in range(4):
            cols = slice(g * HEAD, (g + 1) * HEAD)
            p_ref[:, cols] = _dot(pooled_ref[:, cols], pw_ref[g], NN)
        rc = CONV_ROWS

        def block(ref, k, cb, r0, n):
            return ref[r0:r0 + n, k * D + cb * LANE:k * D + (cb + 1) * LANE].astype(F32)

        def fold(v):
            part = v[0:SUBLANES]
            for j in range(1, v.shape[0] // SUBLANES):
                part = part + v[j * SUBLANES:(j + 1) * SUBLANES]
            return part

        def added(total, v):
            return fold(v) if total is None else total + fold(v)

        for cb in range(LANE_BLOCKS):
            cols = slice(cb * LANE, (cb + 1) * LANE)
            scale = ps_ref[:, cols]
            total = None
            for r0 in range(0, ts, rc):
                silu_ga, dsilu_ga = _silu_and_grad(block(z_ref, 1, cb, r0, rc))
                d_ya = block(dy_ref, 0, cb, r0, rc)
                dp_ref[r0:r0 + rc, cols] = (d_ya * scale * silu_ga).astype(MXU_DTYPE)
                d_ya_p = d_ya * p_ref[r0:r0 + rc, cols]
                dz_ref[r0:r0 + rc, D + cb * LANE:D + (cb + 1) * LANE] = (d_ya_p * scale * dsilu_ga).astype(BF16)
                total = added(total, d_ya_p * silu_ga)
            silu_ga, _ = _silu_and_grad(block(zn_ref, 1, cb, 0, h))
            dp_ref[ts:ts + h, cols] = (block(dyn_ref, 0, cb, 0, h) * scale * silu_ga).astype(MXU_DTYPE)
            dps_ref[:, cols] += _colsum(total)
        for g in range(4):
            cols = slice(g * HEAD, (g + 1) * HEAD)
            dpooled_ref[:, cols] = _dot(dp_ref[:, cols], pwt_ref[g], NN)
            dpw_ref[g] += _dot(pooled_ref[:, cols], dp_ref[0:ts, cols], TN)
        for cb in range(LANE_BLOCKS):
            d_pooled = jnp.where(live, dpooled_ref[:, cb * LANE:(cb + 1) * LANE], 0.0)
            sums = shifts.window_sums(d_pooled * _pool_inv(positions, cb), cb // 2 + 1, False)
            dz_ref[:, cb * LANE:(cb + 1) * LANE] = (sums[:ts] - d_pooled[:ts]).astype(BF16)

        more = i < n_tiles - 1
        q0 = SUBLANES + h
        for cb in range(LANE_BLOCKS):
            cols = slice(cb * LANE, (cb + 1) * LANE)
            w0, w1, w2 = cw_ref[0:1, cols], cw_ref[1:2, cols], cw_ref[2:3, cols]
            bias = cb_ref[:, cols]
            shift_ref[SUBLANES:q0, :] = jnp.where(inside, block(zp_ref, 4, cb, 0, h) * block(zp_ref, 2, cb, 0, h), 0.0)
            for r0 in range(0, ts, rc):
                shift_ref[q0 + r0:q0 + r0 + rc, :] = block(z_ref, 4, cb, r0, rc) * block(z_ref, 2, cb, r0, rc)
                silu_gb, _ = _silu_and_grad(block(z_ref, 5, cb, r0, rc))
                shift2_ref[SUBLANES + r0:SUBLANES + r0 + rc, :] = (
                    block(dy_ref, 1, cb, r0, rc) * block(z_ref, 3, cb, r0, rc) * silu_gb)
            silu_gb, _ = _silu_and_grad(block(zn_ref, 5, cb, 0, h))
            shift2_ref[SUBLANES + ts:SUBLANES + ts + h, :] = jnp.where(
                more, block(dyn_ref, 1, cb, 0, h) * block(zn_ref, 3, cb, 0, h) * silu_gb, 0.0)

            totals = [None] * 4
            for r0 in range(0, ts, rc):
                q_main = shift_ref[q0 + r0:q0 + r0 + rc, :]
                q_1 = shift_ref[q0 + r0 - 1:q0 + r0 - 1 + rc, :]
                q_2 = shift_ref[q0 + r0 - 2:q0 + r0 - 2 + rc, :]
                cv = w2 * q_main + w1 * q_1 + w0 * q_2 + bias
                d0 = SUBLANES + r0
                d_cv0 = shift2_ref[d0:d0 + rc, :]
                d_q = w2 * d_cv0 + w1 * shift2_ref[d0 + 1:d0 + 1 + rc, :] + w0 * shift2_ref[d0 + 2:d0 + 2 + rc, :]
                silu_gb, dsilu_gb = _silu_and_grad(block(z_ref, 5, cb, r0, rc))
                d_yb_cv = block(dy_ref, 1, cb, r0, rc) * cv
                for k, val in ((2, d_q * block(z_ref, 4, cb, r0, rc)), (3, d_yb_cv * silu_gb),
                               (4, d_q * block(z_ref, 2, cb, r0, rc)),
                               (5, d_yb_cv * block(z_ref, 3, cb, r0, rc) * dsilu_gb)):
                    dz_ref[r0:r0 + rc, k * D + cb * LANE:k * D + (cb + 1) * LANE] = val.astype(BF16)
                for j, val in enumerate((d_cv0, d_cv0 * q_main, d_cv0 * q_1, d_cv0 * q_2)):
                    totals[j] = added(totals[j], val)
            dcb_ref[:, cols] += _colsum(totals[0])
            dcw_ref[2:3, cols] += _colsum(totals[1])
            dcw_ref[1:2, cols] += _colsum(totals[2])
            dcw_ref[0:1, cols] += _colsum(totals[3])

    fixed2 = lambda i: (0, 0)
    fixed3 = lambda i: (0, 0, 0)
    return pl.pallas_call(
        body, name=f"even_bwd_{l}", grid=(n_tiles,),
        in_specs=[pl.BlockSpec((ts, DZ), lambda i: (i, 0)), pl.BlockSpec((h, DZ), _prev_index(ts, h)),
                  pl.BlockSpec((h, DZ), _next_index(ts, h, s)),
                  pl.BlockSpec((ts, DY), lambda i: (i, 0)), pl.BlockSpec((h, DY), _next_index(ts, h, s)),
                  pl.BlockSpec((4, HEAD, HEAD), fixed3), pl.BlockSpec((4, HEAD, HEAD), fixed3),
                  pl.BlockSpec((1, D), fixed2), pl.BlockSpec((SHORT_K, D), fixed2), pl.BlockSpec((1, D), fixed2)],
        out_specs=[pl.BlockSpec((ts, DZ), lambda i: (i, 0)), pl.BlockSpec((4, HEAD, HEAD), fixed3),
                   pl.BlockSpec((1, D), fixed2), pl.BlockSpec((SHORT_K, D), fixed2), pl.BlockSpec((1, D), fixed2)],
        out_shape=[jax.ShapeDtypeStruct((s, DZ), BF16), jax.ShapeDtypeStruct((4, HEAD, HEAD), F32),
                   jax.ShapeDtypeStruct((1, D), F32), jax.ShapeDtypeStruct((SHORT_K, D), F32),
                   jax.ShapeDtypeStruct((1, D), F32)],
        scratch_shapes=[pltpu.VMEM((ts + h + 2 * SUBLANES, LANE), F32), pltpu.VMEM((ts, D), MXU_DTYPE),
                        pltpu.VMEM((ts, D), F32), pltpu.VMEM((ts + h, D), MXU_DTYPE), pltpu.VMEM((ts + h, D), F32),
                        pltpu.VMEM((ts + h + 2 * SUBLANES, LANE), F32)],
        compiler_params=_params(("arbitrary",), 56),
    )(z, z, z, dy, dy, pool_w, pool_wt, pool_scale, sconv_w, sconv_b)


def _to_blocks(ref, r0, val):
    n = val.shape[0]
    for cb in range(LANE_BLOCKS):
        ref[cb, r0:r0 + n, :] = val[:, cb * LANE:(cb + 1) * LANE]


def _from_blocks(ref):
    return jnp.concatenate([ref[cb] for cb in range(LANE_BLOCKS)], axis=1)


def _shift_copies(src_ref, sh_ref, n, causal):
    def block(cb, carry):
        for b in range(1, SUBLANES):
            if causal:
                sh_ref[cb, b - 1, SUBLANES:n, :] = src_ref[cb, SUBLANES - b:n - b, :]
            else:
                sh_ref[cb, b - 1, 0:n - SUBLANES, :] = src_ref[cb, b:n - SUBLANES + b, :]
        return carry

    lax.fori_loop(0, LANE_BLOCKS, block, 0)


def _tap(src_ref, sh_ref, cb, first, n, d, causal):
    whole, b = (d // SUBLANES) * SUBLANES, d % SUBLANES
    start = first - whole if causal else first + whole
    if b == 0:
        return src_ref[cb, start:start + n, :]
    return sh_ref[cb, b - 1, start:start + n, :]


def _chunk_rows(rows, most):
    return max(n for n in range(CONV_ROWS, most + 1, CONV_ROWS) if rows % n == 0)


def _conv31(src_ref, sh_ref, w_ref, dst_ref, base, rows, causal):
    n = _chunk_rows(rows, 4 * CONV_ROWS)

    def block(cb, carry):
        for r0 in range(0, rows, n):
            acc = None
            for d in range(CONV_K):
                term = w_ref[cb, CONV_K - 1 - d:CONV_K - d, :] * _tap(src_ref, sh_ref, cb, base + r0, n, d, causal)
                acc = term if acc is None else acc + term
            dst_ref[cb, r0:r0 + n, :] = acc
        return carry

    lax.fori_loop(0, LANE_BLOCKS, block, 0)


def _conv31_tap_grads(d_ref, src_ref, sh_ref, dw_ref, base, rows):
    n = _chunk_rows(rows, 2 * CONV_ROWS)

    def block(cb, carry):
        sums = [None] * CONV_K
        for r0 in range(0, rows, n):
            d_blk = d_ref[cb, r0:r0 + n, :]
            for d in range(CONV_K):
                prod = d_blk * _tap(src_ref, sh_ref, cb, base + r0, n, d, True)
                part = prod[0:SUBLANES]
                for k in range(1, n // SUBLANES):
                    part = part + prod[k * SUBLANES:(k + 1) * SUBLANES]
                sums[d] = part if sums[d] is None else sums[d] + part
        for d in range(CONV_K):
            j = CONV_K - 1 - d
            dw_ref[cb, j:j + 1, :] += _colsum(sums[d])
        return carry

    lax.fori_loop(0, LANE_BLOCKS, block, 0)


def _sgu_bias_rows(sgu_b):
    return jnp.repeat(jnp.transpose(sgu_b), HEAD, axis=1)


def _odd_fwd(z, sln_g, sln_b, ws, sbias, dconv_w, dconv_b, dn_g, dn_b, l):
    s = z.shape[0]
    ts = min(s, 256)
    h = CONV_HALO

    def body(z_ref, zp_ref, lg_ref, lb_ref, ws_ref, sb_ref, cw_ref, cb_ref, ng_ref, nb_ref, o_ref, zh_ref, rz_ref,
             zz_ref, zc_ref, sh_ref):
        i = pl.program_id(0)
        vhat, _ = _ln_stats(_section(z_ref, 1))
        vn = (vhat * lg_ref[...] + lb_ref[...]).astype(MXU_DTYPE)
        silu_gc, _ = _silu_and_grad(_section(z_ref, 2))
        for n in range(ts // SGU_BLOCK):
            rows = slice(n * SGU_BLOCK, (n + 1) * SGU_BLOCK)
            sv = jnp.concatenate(
                [_dot(ws_ref[hd], vn[rows, hd * HEAD:(hd + 1) * HEAD], NN) for hd in range(4)], axis=1)
            sv = sv + sb_ref[...]
            o_ref[rows, 0:D] = (z_ref[rows, 0:D].astype(F32) * sv * silu_gc[rows]).astype(BF16)

        _to_blocks(zz_ref, 0, jnp.where(i > 0, _section(zp_ref, 3) * _sigmoid(_section(zp_ref, 4)), 0.0))
        _to_blocks(zz_ref, h, _section(z_ref, 3) * _sigmoid(_section(z_ref, 4)))
        _shift_copies(zz_ref, sh_ref, h + ts, True)
        _conv31(zz_ref, sh_ref, cw_ref, zc_ref, h, ts, True)
        zhat, rstd_z = _ln_stats(_from_blocks(zc_ref) + cb_ref[...])
        zh_ref[...] = zhat
        rz_ref[...] = rstd_z
        silu_zn, _ = _silu_and_grad(zhat * ng_ref[...] + nb_ref[...])
        silu_gd, _ = _silu_and_grad(_section(z_ref, 5))
        o_ref[:, D:2 * D] = (silu_zn * silu_gd).astype(BF16)

    fixed2 = lambda i: (0, 0)
    vec = pl.BlockSpec((1, D), fixed2)
    return pl.pallas_call(
        body, name=f"odd_fwd_{l}", grid=(s // ts,),
        in_specs=[pl.BlockSpec((ts, DZ), lambda i: (i, 0)), pl.BlockSpec((h, DZ), _prev_index(ts, h)),
                  vec, vec, pl.BlockSpec((4, SGU_BLOCK, SGU_BLOCK), lambda i: (0, 0, 0)),
                  pl.BlockSpec((SGU_BLOCK, D), fixed2), pl.BlockSpec((LANE_BLOCKS, CONV_K, LANE), lambda i: (0, 0, 0)),
                  vec, vec, vec],
        out_specs=[pl.BlockSpec((ts, DY), lambda i: (i, 0)), pl.BlockSpec((ts, D), lambda i: (i, 0)),
                   pl.BlockSpec((ts, 1), lambda i: (i, 0))],
        out_shape=[jax.ShapeDtypeStruct((s, DY), BF16), jax.ShapeDtypeStruct((s, D), F32),
                   jax.ShapeDtypeStruct((s, 1), F32)],
        scratch_shapes=[pltpu.VMEM((LANE_BLOCKS, h + ts, LANE), F32), pltpu.VMEM((LANE_BLOCKS, ts, LANE), F32),
                        pltpu.VMEM((LANE_BLOCKS, SUBLANES - 1, h + ts, LANE), F32)],
        compiler_params=_params(("arbitrary",)),
    )(z, z, sln_g, sln_b, ws, sbias, dconv_w, dconv_b, dn_g, dn_b)


def _odd_bwd(z, dy, zhat_s, rstd_s, sln_g, sln_b, ws, wst, sbias, dconv_w, dn_g, dn_b, l):
    s = z.shape[0]
    ts = min(s, 256)
    h = CONV_HALO
    n_tiles = s // ts
    te = ts + h

    def body(z_ref, zp_ref, zn_ref, dy_ref, dyn_ref, zh_ref, zhn_ref, rz_ref, rzn_ref, lg_ref, lb_ref, ws_ref,
             wst_ref, sb_ref, cw_ref, ng_ref, nb_ref, dz_ref, dlg_ref, dlb_ref, dws_ref, dsb_ref, dcw_ref, dcb_ref,
             dng_ref, dnb_ref, zz_ref, dzc_ref, dzz_ref, dsb_acc, sh_ref):
        i = pl.program_id(0)
        more = i < n_tiles - 1

        @pl.when(i == 0)
        def _():
            for ref in (dlg_ref, dlb_ref, dws_ref, dsb_ref, dcw_ref, dcb_ref, dng_ref, dnb_ref, dsb_acc):
                ref[...] = jnp.zeros_like(ref)

        vhat, rstd_v = _ln_stats(_section(z_ref, 1))
        lg = lg_ref[...]
        vn = (vhat * lg + lb_ref[...]).astype(MXU_DTYPE)
        u = _section(z_ref, 0)
        silu_gc, dsilu_gc = _silu_and_grad(_section(z_ref, 2))
        d_yc = _section(dy_ref, 0)
        d_yc_u = d_yc * u
        d_sv = d_yc_u * silu_gc
        d_svb = d_sv.astype(MXU_DTYPE)
        sv_rows = []
        dvn_rows = []
        dsb = None
        for n in range(ts // SGU_BLOCK):
            rows = slice(n * SGU_BLOCK, (n + 1) * SGU_BLOCK)
            sv_parts = []
            dvn_parts = []
            for hd in range(4):
                cols = slice(hd * HEAD, (hd + 1) * HEAD)
                sv_parts.append(_dot(ws_ref[hd], vn[rows, cols], NN))
                dvn_parts.append(_dot(wst_ref[hd], d_svb[rows, cols], NN))
                dws_ref[hd] += _dot(d_svb[rows, cols], vn[rows, cols], NT)
            sv_rows.append(jnp.concatenate(sv_parts, axis=1) + sb_ref[...])
            dvn_rows.append(jnp.concatenate(dvn_parts, axis=1))
            dsb = d_sv[rows] if dsb is None else dsb + d_sv[rows]
        dsb_acc[...] += dsb

        @pl.when(i == n_tiles - 1)
        def _():
            for hd in range(4):
                blk = dsb_acc[:, hd * HEAD:(hd + 1) * HEAD]
                folded = blk[:, 0:LANE] + blk[:, LANE:HEAD]
                dsb_ref[hd:hd + 1, :] = _colsum(jnp.transpose(folded))
        sv = jnp.concatenate(sv_rows, axis=0)
        d_vn = jnp.concatenate(dvn_rows, axis=0)
        dz_ref[:, 0:D] = (d_yc * sv * silu_gc).astype(BF16)
        dz_ref[:, D:2 * D] = _ln_bwd_rows(d_vn * lg, vhat, rstd_v).astype(BF16)
        dz_ref[:, 2 * D:3 * D] = (d_yc_u * sv * dsilu_gc).astype(BF16)
        dlg_ref[...] += _colsum(d_vn * vhat)
        dlb_ref[...] += _colsum(d_vn)

        def gate(ref):
            return _section(ref, 3) * _sigmoid(_section(ref, 4))

        _to_blocks(zz_ref, 0, jnp.where(i > 0, gate(zp_ref), 0.0))
        _to_blocks(zz_ref, h, gate(z_ref))
        _shift_copies(zz_ref, sh_ref, h + ts, True)
        zhat = jnp.concatenate([zh_ref[...], zhn_ref[...]], axis=0)
        rstd_z = jnp.concatenate([rz_ref[...], rzn_ref[...]], axis=0)
        ng = ng_ref[...]
        silu_zn, dsilu_zn = _silu_and_grad(zhat * ng + nb_ref[...])
        gd = jnp.concatenate([_section(z_ref, 5), _section(zn_ref, 5)], axis=0)
        silu_gd, dsilu_gd = _silu_and_grad(gd)
        d_yd = jnp.concatenate([_section(dy_ref, 1), _section(dyn_ref, 1)], axis=0)
        d_zn = d_yd * silu_gd * dsilu_zn
        d_zc = _ln_bwd_rows(d_zn * ng, zhat, rstd_z)
        row = lax.broadcasted_iota(jnp.int32, (te, 1), 0)
        d_zc = jnp.where(jnp.logical_or(more, row < ts), d_zc, 0.0)
        _to_blocks(dzc_ref, 0, d_zc)
        dz_ref[:, 5 * D:6 * D] = (d_yd[:ts] * silu_zn[:ts] * dsilu_gd[:ts]).astype(BF16)
        dng_ref[...] += _colsum(d_zn[:ts] * zhat[:ts])
        dnb_ref[...] += _colsum(d_zn[:ts])
        dcb_ref[...] += _colsum(d_zc[:ts])
        _conv31_tap_grads(dzc_ref, zz_ref, sh_ref, dcw_ref, h, ts)
        _shift_copies(dzc_ref, sh_ref, te, False)
        _conv31(dzc_ref, sh_ref, cw_ref, dzz_ref, 0, ts, False)
        d_zz = _from_blocks(dzz_ref)
        a = _section(z_ref, 3)
        sig_b = _sigmoid(_section(z_ref, 4))
        dz_ref[:, 3 * D:4 * D] = (d_zz * sig_b).astype(BF16)
        dz_ref[:, 4 * D:5 * D] = (d_zz * a * sig_b * (1.0 - sig_b)).astype(BF16)

    fixed2 = lambda i: (0, 0)
    fixed3 = lambda i: (0, 0, 0)
    vec = pl.BlockSpec((1, D), fixed2)
    mat = pl.BlockSpec((4, SGU_BLOCK, SGU_BLOCK), fixed3)
    vec_shape = jax.ShapeDtypeStruct((1, D), F32)
    return pl.pallas_call(
        body, name=f"odd_bwd_{l}", grid=(n_tiles,),
        in_specs=[pl.BlockSpec((ts, DZ), lambda i: (i, 0)), pl.BlockSpec((h, DZ), _prev_index(ts, h)),
                  pl.BlockSpec((h, DZ), _next_index(ts, h, s)),
                  pl.BlockSpec((ts, DY), lambda i: (i, 0)), pl.BlockSpec((h, DY), _next_index(ts, h, s)),
                  pl.BlockSpec((ts, D), lambda i: (i, 0)), pl.BlockSpec((h, D), _next_index(ts, h, s)),
                  pl.BlockSpec((ts, 1), lambda i: (i, 0)), pl.BlockSpec((h, 1), _next_index(ts, h, s)),
                  vec, vec, mat, mat, pl.BlockSpec((SGU_BLOCK, D), fixed2),
                  pl.BlockSpec((LANE_BLOCKS, CONV_K, LANE), fixed3), vec, vec],
        out_specs=[pl.BlockSpec((ts, DZ), lambda i: (i, 0)), vec, vec, mat, pl.BlockSpec((4, SGU_BLOCK), fixed2),
                   pl.BlockSpec((LANE_BLOCKS, CONV_K, LANE), fixed3), vec, vec, vec],
        out_shape=[jax.ShapeDtypeStruct((s, DZ), BF16), vec_shape, vec_shape,
                   jax.ShapeDtypeStruct((4, SGU_BLOCK, SGU_BLOCK), F32), jax.ShapeDtypeStruct((4, SGU_BLOCK), F32),
                   jax.ShapeDtypeStruct((LANE_BLOCKS, CONV_K, LANE), F32), vec_shape, vec_shape, vec_shape],
        scratch_shapes=[pltpu.VMEM((LANE_BLOCKS, h + ts, LANE), F32), pltpu.VMEM((LANE_BLOCKS, te, LANE), F32),
                        pltpu.VMEM((LANE_BLOCKS, ts, LANE), F32), pltpu.VMEM((SGU_BLOCK, D), F32),
                        pltpu.VMEM((LANE_BLOCKS, SUBLANES - 1, te, LANE), F32)],
        compiler_params=_params(("arbitrary",), 60),
    )(z, z, z, dy, dy, zhat_s, zhat_s, rstd_s, rstd_s, sln_g, sln_b, ws, wst, sbias, dconv_w, dn_g, dn_b)


def _remote(src, dst, send_sems, recv_sems, k, to):
    return pltpu.make_async_remote_copy(src_ref=src, dst_ref=dst, send_sem=send_sems.at[k],
                                        recv_sem=recv_sems.at[k], device_id=to, device_id_type=MESH)


def _other_chips(x, y):
    return [(1 - x, y, 2 * (1 - x) + y), (x, 1 - y, 2 * x + 1 - y), (1 - x, 1 - y, 2 * (1 - x) + 1 - y)]


def _cast_own(w_stack, pos_arr, name):
    slots, rows, cols = w_stack.shape
    half = rows // 2

    def body(pos_ref, w_ref, o_ref):
        o_ref[...] = w_ref[...].astype(BF16)

    grid_spec = pltpu.PrefetchScalarGridSpec(
        num_scalar_prefetch=1, grid=(slots, 2),
        in_specs=[pl.BlockSpec((None, half, cols), lambda s, h, pos: (s, h, 0))],
        out_specs=pl.BlockSpec((None, None, None, half, cols), lambda s, h, pos: (pos[1], h, s, 0, 0)))
    return pl.pallas_call(
        body, name=name, grid_spec=grid_spec, out_shape=jax.ShapeDtypeStruct((NQ, 2, slots, half, cols), BF16),
        compiler_params=_params(("arbitrary",) * 2),
    )(pos_arr, w_stack)


def _gather_weights(win_g, wout_g, small_sh):
    def body(win_in, wout_in, small, win_g, wout_g, small_g, send_sems, recv_sems):
        del win_in, wout_in
        x, y, c = _mesh_pos()
        me = 2 * x + y
        sibling = (x, y, 1 - c)
        chips = _other_chips(x, y)

        sends = []
        for j, (cx, cy, _) in enumerate(chips):
            to = (cx, cy, c)
            sends.append(_remote(win_g.at[me, c], win_g.at[me, c], send_sems, recv_sems, j, to))
            sends.append(_remote(wout_g.at[me, c], wout_g.at[me, c], send_sems, recv_sems, 3 + j, to))
            sends.append(_remote(small, small_g.at[me], send_sems, recv_sems, 6 + j, to))
        for cp in sends:
            cp.start()
        passed = []
        for j, (_, _, q) in enumerate(chips):
            got_in = win_g.at[q, c]
            got_out = wout_g.at[q, c]
            _remote(got_in, got_in, send_sems, recv_sems, j, sibling).wait_recv()
            cp = _remote(got_in, got_in, send_sems, recv_sems, 9 + j, sibling)
            cp.start()
            passed.append(cp)
            _remote(got_out, got_out, send_sems, recv_sems, 3 + j, sibling).wait_recv()
            cp = _remote(got_out, got_out, send_sems, recv_sems, 12 + j, sibling)
            cp.start()
            passed.append(cp)
            _remote(small, small_g.at[q], send_sems, recv_sems, 6 + j, sibling).wait_recv()
        for j, (_, _, q) in enumerate(chips):
            from_in = win_g.at[q, 1 - c]
            from_out = wout_g.at[q, 1 - c]
            _remote(from_in, from_in, send_sems, recv_sems, 9 + j, sibling).wait_recv()
            _remote(from_out, from_out, send_sems, recv_sems, 12 + j, sibling).wait_recv()
        for cp in sends + passed:
            cp.wait_send()

    return pl.pallas_call(
        body, name="gather_weights",
        in_specs=[HBM_SPEC, HBM_SPEC, HBM_SPEC], out_specs=[HBM_SPEC, HBM_SPEC, HBM_SPEC],
        out_shape=[jax.ShapeDtypeStruct(win_g.shape, win_g.dtype), jax.ShapeDtypeStruct(wout_g.shape, wout_g.dtype),
                   jax.ShapeDtypeStruct((NQ,) + small_sh.shape, small_sh.dtype)],
        input_output_aliases={0: 0, 1: 1},
        scratch_shapes=[pltpu.SemaphoreType.DMA((15,)), pltpu.SemaphoreType.DMA((15,))],
    )(win_g, wout_g, small_sh)


def _hbm(a):
    return pltpu.with_memory_space_constraint(a, pltpu.HBM)


def _split_start(body, name, sources, landings):
    n_src, n_land = len(sources), len(landings)
    n_buf = n_src + n_land

    def kernel_body(*refs):
        ins, outs = refs[:n_buf], refs[n_buf:]
        send_sems, recv_sems, token = outs[0], outs[1], outs[2 + n_buf]
        body(ins[:n_src], ins[n_src:], send_sems, recv_sems)
        token[...] = jnp.zeros_like(token)

    bufs = [_hbm(a) for a in sources] + [
        _hbm(lax.empty(s.shape, s.dtype) if isinstance(s, jax.ShapeDtypeStruct) else s) for s in landings]
    n_sem = getattr(body, "n_copies")
    out = pl.pallas_call(
        kernel_body, name=name,
        out_shape=(pltpu.SemaphoreType.DMA((n_sem,)), pltpu.SemaphoreType.DMA((n_sem,)),
                   *[pltpu.HBM(b.shape, b.dtype) for b in bufs], jax.ShapeDtypeStruct((8, LANE), F32)),
        in_specs=(HBM_SPEC,) * n_buf,
        out_specs=(SEM_SPEC, SEM_SPEC, *([HBM_SPEC] * n_buf), pl.BlockSpec(memory_space=pltpu.VMEM)),
        input_output_aliases={k: 2 + k for k in range(n_buf)},
        compiler_params=pltpu.CompilerParams(has_side_effects=SIDE_EFFECT),
    )(*bufs)
    return out[0], out[1], list(out[2:2 + n_src]), list(out[2 + n_src:2 + n_buf]), out[2 + n_buf]


def _split_wait(body, name, send_sems, recv_sems, sources, landings, after):
    n_src, n_land = len(sources), len(landings)
    n_buf = n_src + n_land

    def kernel_body(*refs):
        ins = refs[:n_buf]
        body(ins[:n_src], ins[n_src:], refs[n_buf], refs[n_buf + 1])

    bufs = list(sources) + list(landings)
    out = pl.pallas_call(
        kernel_body, name=name,
        out_shape=tuple(pltpu.HBM(b.shape, b.dtype) for b in bufs),
        in_specs=(*([HBM_SPEC] * n_buf), SEM_SPEC, SEM_SPEC, pl.BlockSpec(memory_space=pl.ANY)),
        out_specs=(HBM_SPEC,) * n_buf,
        input_output_aliases={k: k for k in range(n_buf)},
        compiler_params=pltpu.CompilerParams(has_side_effects=SIDE_EFFECT),
    )(*bufs, send_sems, recv_sems, after)
    return list(out[:n_src]), list(out[n_src:])


def _gather_rest_copies(start):
    def body(srcs, lands, send_sems, recv_sems):
        del srcs
        x, y, c = _mesh_pos()
        me = 2 * x + y
        for j, (cx, cy, q) in enumerate(_other_chips(x, y)):
            to = (cx, cy, c)
            for k, gathered in enumerate(lands):
                if start:
                    _remote(gathered.at[me, c], gathered.at[me, c], send_sems, recv_sems, 3 * k + j, to).start()
                else:
                    cp = _remote(gathered.at[me, c], gathered.at[q, c], send_sems, recv_sems, 3 * k + j, to)
                    cp.wait_send()
                    cp.wait_recv()

    body.n_copies = 6
    return body


def _gather_rest_forward(win_g, wout_g):
    def body(win_in, wout_in, win_g, wout_g, send_sems, recv_sems):
        del win_in, wout_in
        x, y, c = _mesh_pos()
        sibling = (x, y, 1 - c)
        passed = []
        for j, (_, _, q) in enumerate(_other_chips(x, y)):
            got_in = win_g.at[q, c]
            got_out = wout_g.at[q, c]
            passed.append(_remote(got_in, got_in, send_sems, recv_sems, j, sibling))
            passed.append(_remote(got_out, got_out, send_sems, recv_sems, 3 + j, sibling))
        for cp in passed:
            cp.start()
        for j, (_, _, q) in enumerate(_other_chips(x, y)):
            from_in = win_g.at[q, 1 - c]
            from_out = wout_g.at[q, 1 - c]
            _remote(from_in, from_in, send_sems, recv_sems, j, sibling).wait_recv()
            _remote(from_out, from_out, send_sems, recv_sems, 3 + j, sibling).wait_recv()
        for cp in passed:
            cp.wait_send()

    return pl.pallas_call(
        body, name="gather_rest_forward",
        in_specs=[HBM_SPEC] * 2, out_specs=[HBM_SPEC] * 2,
        out_shape=[jax.ShapeDtypeStruct(win_g.shape, win_g.dtype), jax.ShapeDtypeStruct(wout_g.shape, wout_g.dtype)],
        input_output_aliases={0: 0, 1: 1},
        scratch_shapes=[pltpu.SemaphoreType.DMA((6,)), pltpu.SemaphoreType.DMA((6,))],
    )(win_g, wout_g)


def _allreduce_small(groups, after):
    pieces = [p for _, _, _, members in groups for _, p in members]
    n_in, n_g = len(pieces), len(groups)

    def body(*refs):
        ins = refs[:n_in]
        outs = refs[-(2 * n_g + 2):-(n_g + 2)]
        alls = refs[-(n_g + 2):-2]
        send_sems, recv_sems = refs[-2:]
        x, y, c = _mesh_pos()
        me = 4 * x + 2 * y + c
        sibling = (x, y, 1 - c)
        chips = _other_chips(x, y)

        k = 0
        for (rows, cols, dtype, members), all_ref in zip(groups, alls):
            if any(piece.shape[1] < cols for _, piece in members) or sum(p.shape[0] for _, p in members) < rows:
                all_ref[me] = jnp.zeros((rows, cols), dtype)
            for first, piece in members:
                n, width = piece.shape
                all_ref[me, first:first + n, 0:width] = ins[k][...].astype(dtype)
                k += 1

        sends, passed = [], []
        for g, all_ref in enumerate(alls):
            sends.append(_remote(all_ref.at[me], all_ref.at[me], send_sems, recv_sems, 7 * g, sibling))
            for j, (cx, cy, _) in enumerate(chips):
                sends.append(_remote(all_ref.at[me], all_ref.at[me], send_sems, recv_sems, 7 * g + 1 + j, (cx, cy, c)))
        for cp in sends:
            cp.start()
        for j, (cx, cy, _) in enumerate(chips):
            for g, all_ref in enumerate(alls):
                got = all_ref.at[4 * cx + 2 * cy + c]
                _remote(got, got, send_sems, recv_sems, 7 * g + 1 + j, sibling).wait_recv()
                cp = _remote(got, got, send_sems, recv_sems, 7 * g + 4 + j, sibling)
                cp.start()
                passed.append(cp)
        for g, all_ref in enumerate(alls):
            got = all_ref.at[4 * x + 2 * y + 1 - c]
            _remote(got, got, send_sems, recv_sems, 7 * g, sibling).wait_recv()
            for j, (cx, cy, _) in enumerate(chips):
                got = all_ref.at[4 * cx + 2 * cy + 1 - c]
                _remote(got, got, send_sems, recv_sems, 7 * g + 4 + j, sibling).wait_recv()
        for cp in sends + passed:
            cp.wait_send()
        for o_ref, all_ref in zip(outs, alls):
            total = all_ref[0].astype(F32)
            for dev in range(1, 8):
                total = total + all_ref[dev].astype(F32)
            o_ref[...] = total

    vmem = pl.BlockSpec(memory_space=pltpu.VMEM)
    return pl.pallas_call(
        body, name="allreduce_small",
        in_specs=[vmem] * n_in + _after_spec(after),
        out_specs=[vmem] * n_g,
        out_shape=[jax.ShapeDtypeStruct((rows, cols), F32) for rows, cols, _, _ in groups],
        scratch_shapes=[pltpu.VMEM((8, rows, cols), dtype) for rows, cols, dtype, _ in groups]
        + [pltpu.SemaphoreType.DMA((7 * n_g,)), pltpu.SemaphoreType.DMA((7 * n_g,))],
        compiler_params=_params(None, 56),
    )(*pieces, *_after_args(after))


def _pair_copies(start):
    def body(srcs, lands, send_sems, recv_sems):
        x, y, c = _mesh_pos()
        sibling = (x, y, 1 - c)
        for k, (g_ref, r_ref) in enumerate(zip(srcs, lands)):
            half = g_ref.shape[1] // 2
            cp = _remote(g_ref.at[:, pl.ds((1 - c) * half, half), :], r_ref, send_sems, recv_sems, k, sibling)
            if start:
                cp.start()
            else:
                cp.wait_send()
                cp.wait_recv()

    body.n_copies = 2
    return body


def _pair_sum(g, r, pos_arr, name):
    nq, rows, cols = r.shape
    tr = min(rows, 256)
    nt = rows // tr

    def body(pos_ref, g_ref, r_ref, ob_ref, own_ref):
        total = g_ref[...] + r_ref[...]
        ob_ref[...] = total.astype(BF16)

        @pl.when(pl.program_id(1) == pos_ref[1])
        def _():
            own_ref[...] = total

    blk = (None, tr, cols)
    grid_spec = pltpu.PrefetchScalarGridSpec(
        num_scalar_prefetch=1, grid=(nt, nq),
        in_specs=[pl.BlockSpec(blk, lambda t, q, pos: (q, pos[0] * nt + t, 0)),
                  pl.BlockSpec(blk, lambda t, q, pos: (q, t, 0))],
        out_specs=[pl.BlockSpec(blk, lambda t, q, pos: (q, t, 0)),
                   pl.BlockSpec((tr, cols), lambda t, q, pos: (t, 0))])
    return pl.pallas_call(
        body, name=name, grid_spec=grid_spec,
        out_shape=[jax.ShapeDtypeStruct(r.shape, BF16), jax.ShapeDtypeStruct((rows, cols), F32)],
        compiler_params=_params(("arbitrary",) * 2),
    )(pos_arr, g, r)


def _chip_copies(start):
    def body(srcs, lands, send_sems, recv_sems):
        pin, pout = srcs
        rin, rout = lands
        x, y, c = _mesh_pos()
        for j, (cx, cy, q) in enumerate(_other_chips(x, y)):
            to = (cx, cy, c)
            for k, (src, dst) in enumerate(((pin, rin), (pout, rout))):
                cp = _remote(src.at[q], dst.at[j], send_sems, recv_sems, 3 * k + j, to)
                if start:
                    cp.start()
                else:
                    cp.wait_send()
                    cp.wait_recv()

    body.n_copies = 6
    return body


def _chip_sum(own, r, pos_arr, name):
    rows, cols = own.shape
    tr = min(rows, 256)

    def body(pos_ref, p_ref, r0_ref, r1_ref, r2_ref, o_ref):
        o_ref[...] = ((p_ref[...] + r0_ref[...].astype(F32)) + r1_ref[...].astype(F32)) + r2_ref[...].astype(F32)

    def peer(j):
        return pl.BlockSpec((None, tr, cols), lambda t, pos: (j, t, 0))

    grid_spec = pltpu.PrefetchScalarGridSpec(
        num_scalar_prefetch=1, grid=(rows // tr,),
        in_specs=[pl.BlockSpec((tr, cols), lambda t, pos: (t, 0)), peer(0), peer(1), peer(2)],
        out_specs=pl.BlockSpec((None, tr, cols), lambda t, pos: (pos[0], t, 0)))
    return pl.pallas_call(
        body, name=name, grid_spec=grid_spec, out_shape=jax.ShapeDtypeStruct((2, rows, cols), F32),
        compiler_params=_params(("arbitrary",)),
    )(pos_arr, own, r, r, r)


def _pair_share(gin, gout, l):
    def body(gin_in, gout_in, gin_ref, gout_ref, send_sems, recv_sems):
        del gin_in, gout_in
        x, y, c = _mesh_pos()
        sibling = (x, y, 1 - c)
        sends = [_remote(gin_ref.at[c], gin_ref.at[c], send_sems, recv_sems, 0, sibling),
                 _remote(gout_ref.at[c], gout_ref.at[c], send_sems, recv_sems, 1, sibling)]
        for cp in sends:
            cp.start()
        _remote(gin_ref.at[1 - c], gin_ref.at[1 - c], send_sems, recv_sems, 0, sibling).wait_recv()
        _remote(gout_ref.at[1 - c], gout_ref.at[1 - c], send_sems, recv_sems, 1, sibling).wait_recv()
        for cp in sends:
            cp.wait_send()

    return pl.pallas_call(
        body, name=f"pair_share_{l}",
        in_specs=[HBM_SPEC, HBM_SPEC], out_specs=[HBM_SPEC, HBM_SPEC],
        out_shape=[jax.ShapeDtypeStruct(gin.shape, F32), jax.ShapeDtypeStruct(gout.shape, F32)],
        input_output_aliases={0: 0, 1: 1},
        scratch_shapes=[pltpu.SemaphoreType.DMA((2,)), pltpu.SemaphoreType.DMA((2,))],
    )(gin, gout)


def _adamw_large(w, m, v, g, i, prev, name):
    _, rows, cols = w.shape
    half = rows // 2
    tr = min(half, 256)
    nt = half // tr

    def body(w_ref, m_ref, v_ref, g_ref, *rest):
        go_ref, d_ref, mo_ref, vo_ref = rest[-4:]
        gv = g_ref[...]
        go_ref[...] = gv
        d_ref[...], mo_ref[...], vo_ref[...] = _adamw_math(w_ref[...], gv, m_ref[...], v_ref[...])

    full = pl.BlockSpec((None, tr, cols), lambda h, t: (i, h * nt + t, 0))
    out = jax.ShapeDtypeStruct(w.shape, F32)
    carried = [] if prev is None else list(prev)
    return pl.pallas_call(
        body, name=name, grid=(2, nt),
        in_specs=[full, full, full, pl.BlockSpec((None, tr, cols), lambda h, t: (h, t, 0))]
        + [pl.BlockSpec(memory_space=pl.ANY)] * len(carried),
        out_specs=[full] * 4, out_shape=[out] * 4,
        input_output_aliases={4 + k: k for k in range(len(carried))},
        compiler_params=_params(("arbitrary",) * 2),
    )(w, m, v, g, *carried)


def _adamw(w, g, m, v, name):
    shape = w.shape
    w2, g2, m2, v2 = (t.reshape(-1, shape[-1]) for t in (w, g, m, v))
    rows, cols = w2.shape
    tr = 256 if rows % 256 == 0 else rows

    def body(w_ref, g_ref, m_ref, v_ref, d_ref, mo_ref, vo_ref):
        d_ref[...], mo_ref[...], vo_ref[...] = _adamw_math(w_ref[...], g_ref[...], m_ref[...], v_ref[...])

    blk = pl.BlockSpec((tr, cols), lambda i: (i, 0))
    out = jax.ShapeDtypeStruct((rows, cols), F32)
    d, mo, vo = pl.pallas_call(
        body, name=name, grid=(rows // tr,), in_specs=[blk] * 4, out_specs=[blk] * 3, out_shape=[out] * 3,
        compiler_params=_params(("arbitrary",)),
    )(w2, g2, m2, v2)
    return d.reshape(shape), mo.reshape(shape), vo.reshape(shape)


def _layer_slot(l):
    return (l % 2) * 2 + l // 2


def kernel(x, ln_g, ln_b, w_in_even, w_out_even, pool_w, pool_scale, sconv_w, sconv_b, w_in_odd, w_out_odd, sgu_ln_g, sgu_ln_b, sgu_w, sgu_b, dconv_w, dconv_b, dnorm_g, dnorm_b, loss_target, m_ln_g, m_ln_b, m_w_in_even, m_w_out_even, m_pool_w, m_pool_scale, m_sconv_w, m_sconv_b, m_w_in_odd, m_w_out_odd, m_sgu_ln_g, m_sgu_ln_b, m_sgu_w, m_sgu_b, m_dconv_w, m_dconv_b, m_dnorm_g, m_dnorm_b, v_ln_g, v_ln_b, v_w_in_even, v_w_out_even, v_pool_w, v_pool_scale, v_sconv_w, v_sconv_b, v_w_in_odd, v_w_out_odd, v_sgu_ln_g, v_sgu_ln_b, v_sgu_w, v_sgu_b, v_dconv_w, v_dconv_b, v_dnorm_g, v_dnorm_b):
    weights = dict(ln_g=ln_g, ln_b=ln_b, w_in_even=w_in_even, w_out_even=w_out_even, pool_w=pool_w,
                   pool_scale=pool_scale, sconv_w=sconv_w, sconv_b=sconv_b, w_in_odd=w_in_odd, w_out_odd=w_out_odd,
                   sgu_ln_g=sgu_ln_g, sgu_ln_b=sgu_ln_b, sgu_w=sgu_w, sgu_b=sgu_b, dconv_w=dconv_w,
                   dconv_b=dconv_b, dnorm_g=dnorm_g, dnorm_b=dnorm_b)
    moments_m = dict(ln_g=m_ln_g, ln_b=m_ln_b, w_in_even=m_w_in_even, w_out_even=m_w_out_even, pool_w=m_pool_w,
                     pool_scale=m_pool_scale, sconv_w=m_sconv_w, sconv_b=m_sconv_b, w_in_odd=m_w_in_odd,
                     w_out_odd=m_w_out_odd, sgu_ln_g=m_sgu_ln_g, sgu_ln_b=m_sgu_ln_b, sgu_w=m_sgu_w, sgu_b=m_sgu_b,
                     dconv_w=m_dconv_w, dconv_b=m_dconv_b, dnorm_g=m_dnorm_g, dnorm_b=m_dnorm_b)
    moments_v = dict(ln_g=v_ln_g, ln_b=v_ln_b, w_in_even=v_w_in_even, w_out_even=v_w_out_even, pool_w=v_pool_w,
                     pool_scale=v_pool_scale, sconv_w=v_sconv_w, sconv_b=v_sconv_b, w_in_odd=v_w_in_odd,
                     w_out_odd=v_w_out_odd, sgu_ln_g=v_sgu_ln_g, sgu_ln_b=v_sgu_ln_b, sgu_w=v_sgu_w, sgu_b=v_sgu_b,
                     dconv_w=v_dconv_w, dconv_b=v_dconv_b, dnorm_g=v_dnorm_g, dnorm_b=v_dnorm_b)
    names = list(weights)

    xd, yd, cd = _mesh_pos()
    chip = 2 * xd + yd
    pos_arr = jnp.stack([cd, chip]).astype(jnp.int32)

    small_sh = jnp.concatenate(
        [sconv_w.reshape(6, HEAD), sgu_ln_g, sgu_ln_b, dconv_b, dnorm_g, dnorm_b, dconv_w.reshape(62, HEAD),
         jnp.zeros((2, HEAD), F32), pool_w.reshape(512, HEAD)], axis=0)
    win_first, wout_first, small_g = _gather_weights(
        _cast_own(w_in_even[0:1], pos_arr, "cast_win_first"), _cast_own(w_out_even[0:1], pos_arr, "cast_wout_first"),
        small_sh)
    small_g = lax.dynamic_update_slice(small_g, small_sh[None], (chip, 0, 0))
    later_in = jnp.concatenate([w_in_even[1:2], w_in_odd], axis=0)
    later_out = jnp.concatenate([w_out_even[1:2], w_out_odd], axis=0)
    g_send, g_recv, _, g_lands, g_token = _split_start(
        _gather_rest_copies(True), "gather_rest_start", [],
        [_cast_own(later_in, pos_arr, "cast_win_rest"), _cast_own(later_out, pos_arr, "cast_wout_rest")])

    def layer_weights(slot):
        return (win_first, wout_first, 0) if slot == 0 else (win_rest, wout_rest, slot - 1)

    def full_rows(lo, n):
        return jnp.transpose(small_g[:, lo:lo + n], (1, 0, 2)).reshape(n, D)

    sconv_w_f = full_rows(Q_SCONV_W, 6).reshape(2, SHORT_K, D)
    sln_g_f = full_rows(Q_SLN_G, 2)
    sln_b_f = full_rows(Q_SLN_B, 2)
    dconv_b_f = full_rows(Q_DCONV_B, 2)
    dn_g_f = full_rows(Q_DN_G, 2)
    dn_b_f = full_rows(Q_DN_B, 2)
    dconv_w_f = jnp.transpose(full_rows(Q_DCONV_W, 62).reshape(2, CONV_K, LANE_BLOCKS, LANE), (0, 2, 1, 3))
    pool_w_f = jnp.transpose(small_g[:, Q_POOL_W:].reshape(NQ, 2, 4, 64, HEAD), (1, 2, 0, 3, 4)).reshape(2, 4, HEAD, HEAD)
    pool_w_b = pool_w_f.astype(BF16)
    pool_wt_b = jnp.swapaxes(pool_w_f, 2, 3).astype(BF16)
    idx = jnp.arange(SGU_BLOCK)
    mask = (idx[None, :] // 64) <= (idx[:, None] // 64)
    ws_f = jnp.where(mask[None, None], sgu_w, 0.0)
    ws_b = ws_f.astype(BF16)
    wst_b = jnp.swapaxes(ws_f, 2, 3).astype(BF16)

    def row(a, i):
        return a[i:i + 1]

    residual = (x[0], jnp.ones((1, D), F32), jnp.zeros((1, D), F32))
    x_b = x[0].astype(BF16)
    saved = []
    conv_saved = {}
    for l in range(NL):
        i, slot = l // 2, _layer_slot(l)
        if l == 1:
            _, g_lands = _split_wait(_gather_rest_copies(False), "gather_rest_wait", g_send, g_recv, [], g_lands, x_b)
            win_rest, wout_rest = _gather_rest_forward(g_lands[0], g_lands[1])
        win_g, wout_g, k = layer_weights(slot)
        z = _proj_in(x_b, win_g, k, l, g_token if l == 0 else None)
        if l % 2 == 0:
            ycat = _even_fwd(z, pool_w_b[i], row(pool_scale, i), sconv_w_f[i], row(sconv_b, i), l)
        else:
            ycat, conv_hat, conv_rstd = _odd_fwd(
                z, row(sln_g_f, i), row(sln_b_f, i), ws_b[i], _sgu_bias_rows(sgu_b[i]),
                dconv_w_f[i], row(dconv_b_f, i), row(dn_g_f, i), row(dn_b_f, i), l)
            conv_saved[l] = (conv_hat, conv_rstd)
        x_next_b, xhat, rstd = _proj_out_ln(ycat, wout_g, k, l, *residual, row(ln_g, l), row(ln_b, l))
        saved.append((x_b, z, ycat, xhat, rstd))
        residual = (xhat, row(ln_g, l), row(ln_b, l))
        x_b = x_next_b

    small = {}
    d_ln_g = [None] * NL
    d_ln_b = [None] * NL
    large = {"w_in_even": None, "w_out_even": None, "w_in_odd": None, "w_out_odd": None}
    pending = None
    token = None

    def finish(exchange, after):
        lx, send, recv, srcs, lands, own_in, own_out = exchange
        _, (r_in, r_out) = _split_wait(_chip_copies(False), f"chip_wait_{lx}", send, recv, srcs, lands, after)
        fin = _chip_sum(own_in, r_in, pos_arr, f"chip_sum_in_{lx}")
        fout = _chip_sum(own_out, r_out, pos_arr, f"chip_sum_out_{lx}")
        gs_in, gs_out = _pair_share(fin, fout, lx)
        kind = "even" if lx % 2 == 0 else "odd"
        for nm, gs in ((f"w_in_{kind}", gs_in), (f"w_out_{kind}", gs_out)):
            large[nm] = _adamw_large(weights[nm], moments_m[nm], moments_v[nm], gs, lx // 2, large[nm],
                                     f"adamw_{nm}_{lx // 2}")
        return large[f"w_out_{kind}"][0]

    for l in reversed(range(NL)):
        i, slot = l // 2, _layer_slot(l)
        win_g, wout_g, k = layer_weights(slot)
        xin_b, z, ycat, xhat, rstd = saved[l]
        if l == NL - 1:
            loss_part, dr, dr_b, d_ln_g[l], d_ln_b[l] = _loss_ln_bwd(
                xhat, rstd, row(ln_g, l), row(ln_b, l), loss_target[0], l)
            loss = lax.psum(loss_part[0, 0], ("x", "y", "c"))
        else:
            dr, dr_b, d_ln_g[l], d_ln_b[l] = _ln_bwd(dxn, xhat, rstd, row(ln_g, l), l, token)
        dy = _dycat(dr_b, wout_g, k, l)
        gout = _dwout(ycat, dr_b, l).reshape(NQ, RQ, D)
        if l % 2 == 0:
            dz, d_pw, d_ps, d_cw, d_cb = _even_bwd(z, dy, pool_w_b[i], pool_wt_b[i], row(pool_scale, i),
                                                   sconv_w_f[i], row(sconv_b, i), l)
            small[("pool_w", i)] = d_pw
            small[("pool_scale", i)] = d_ps
            small[("sconv_w", i)] = d_cw
            small[("sconv_b", i)] = d_cb
        else:
            dz, d_lg, d_lb, d_ws, d_sb, d_cw, d_cb, d_ng, d_nb = _odd_bwd(
                z, dy, *conv_saved[l], row(sln_g_f, i), row(sln_b_f, i), ws_b[i], wst_b[i],
                _sgu_bias_rows(sgu_b[i]), dconv_w_f[i], row(dn_g_f, i), row(dn_b_f, i), l)
            small[("sgu_ln_g", i)] = d_lg
            small[("sgu_ln_b", i)] = d_lb
            small[("sgu_w", i)] = jnp.where(mask[None], d_ws, 0.0)
            small[("sgu_b", i)] = d_sb
            small[("dconv_w", i)] = jnp.transpose(d_cw, (1, 0, 2)).reshape(CONV_K, D)
            small[("dconv_b", i)] = d_cb
            small[("dnorm_g", i)] = d_ng
            small[("dnorm_b", i)] = d_nb
        gin = _dwin(xin_b, dz, l)
        p_send, p_recv, p_srcs, p_lands, p_token = _split_start(
            _pair_copies(True), f"pair_start_{l}", [gin, gout],
            [jax.ShapeDtypeStruct((NQ, D // 2, WQ), F32), jax.ShapeDtypeStruct((NQ, RQ // 2, D), F32)])
        if l > 0:
            dxn = _dx(dz, win_g, k, l, dr, p_token)
            if pending is not None:
                finish(pending, dxn)
            wait_after = dxn
        else:
            wait_after = finish(pending, p_token)
        (gin, gout), (rin, rout) = _split_wait(_pair_copies(False), f"pair_wait_{l}", p_send, p_recv,
                                               p_srcs, p_lands, wait_after)
        pin_b, pin_own = _pair_sum(gin, rin, pos_arr, f"pair_sum_in_{l}")
        pout_b, pout_own = _pair_sum(gout, rout, pos_arr, f"pair_sum_out_{l}")
        send, recv, srcs, lands, token = _split_start(
            _chip_copies(True), f"chip_start_{l}", [pin_b, pout_b],
            [jax.ShapeDtypeStruct((3,) + pin_b.shape[1:], BF16), jax.ShapeDtypeStruct((3,) + pout_b.shape[1:], BF16)])
        if l == 0:
            dxn = _dx(dz, win_g, k, l, dr, token)
        pending = (l, send, recv, srcs, lands, pin_own, pout_own)
    grad_x = dxn[None]

    def both(name, first, step):
        return [(first, small[(name, 0)]), (first + step, small[(name, 1)])]

    vectors = ([(R_LN_G + l, d_ln_g[l]) for l in range(NL)] + [(R_LN_B + l, d_ln_b[l]) for l in range(NL)]
               + both("pool_scale", R_PSCALE, 1) + both("sconv_b", R_SCONV_B, 1) + both("sconv_w", R_SCONV_W, SHORT_K)
               + both("sgu_ln_g", R_SLN_G, 1) + both("sgu_ln_b", R_SLN_B, 1) + both("dconv_b", R_DCONV_B, 1)
               + both("dnorm_g", R_DN_G, 1) + both("dnorm_b", R_DN_B, 1) + both("dconv_w", R_DCONV_W, CONV_K)
               + both("sgu_b", R_SGU_B, 4))
    sgu_w_rows = 4 * SGU_BLOCK
    pool_w_rows = 4 * HEAD
    total, total_sgu_w, total_pool_w = _allreduce_small(
        [(R_VECTORS, D, F32, vectors),
         (2 * sgu_w_rows, SGU_BLOCK, BF16,
          [(i * sgu_w_rows, small[("sgu_w", i)].reshape(sgu_w_rows, SGU_BLOCK)) for i in range(2)]),
         (2 * pool_w_rows, HEAD, BF16,
          [(i * pool_w_rows, small[("pool_w", i)].reshape(pool_w_rows, HEAD)) for i in range(2)])],
        dxn)
    finish(pending, total)

    def mine(a):
        return lax.dynamic_slice_in_dim(a, chip * HEAD, HEAD, axis=a.ndim - 1)

    grads = {
        "ln_g": total[R_LN_G:R_LN_G + 4],
        "ln_b": total[R_LN_B:R_LN_B + 4],
        "pool_scale": total[R_PSCALE:R_PSCALE + 2],
        "sconv_b": total[R_SCONV_B:R_SCONV_B + 2],
        "sconv_w": mine(total[R_SCONV_W:R_SCONV_W + 6].reshape(2, SHORT_K, D)),
        "sgu_ln_g": mine(total[R_SLN_G:R_SLN_G + 2]),
        "sgu_ln_b": mine(total[R_SLN_B:R_SLN_B + 2]),
        "dconv_b": mine(total[R_DCONV_B:R_DCONV_B + 2]),
        "dnorm_g": mine(total[R_DN_G:R_DN_G + 2]),
        "dnorm_b": mine(total[R_DN_B:R_DN_B + 2]),
        "dconv_w": mine(total[R_DCONV_W:R_DCONV_W + 62].reshape(2, CONV_K, D)),
        "sgu_b": total[R_SGU_B:R_SGU_B + 8, 0:SGU_BLOCK].reshape(2, 4, SGU_BLOCK),
        "sgu_w": total_sgu_w.reshape(2, 4, SGU_BLOCK, SGU_BLOCK),
        "pool_w": lax.dynamic_slice_in_dim(total_pool_w.reshape(2, 4, HEAD, HEAD), chip * 64, 64, axis=2),
    }

    deltas, new_m, new_v = {}, {}, {}
    for name in names:
        if name in large:
            grads[name], deltas[name], new_m[name], new_v[name] = large[name]
        else:
            deltas[name], new_m[name], new_v[name] = _adamw(
                weights[name], grads[name], moments_m[name], moments_v[name], f"adamw_{name}")

    return (loss, grad_x, *[grads[n] for n in names], *[deltas[n] for n in names],
            *[new_m[n] for n in names], *[new_v[n] for n in names])
```

```python
import jax
import jax.numpy as jnp
from jax import lax
from jax.experimental import pallas as pl
from jax.experimental.pallas import tpu as pltpu

F32 = jnp.float32
BF16 = jnp.bfloat16
MXU_DTYPE = BF16

D = 1024
DZ = 6144
DY = 2048
NQ = 4
WQ = DZ // NQ
RQ = DY // NQ
NL = 4
ALPHA = (2 * NL) ** 0.25
LN_EPS = 1e-5
CONV_K = 31
SHORT_K = 3
SGU_BLOCK = 128
HEAD = 256
POOL_HALO = 16
CONV_HALO = 32
LANE = 128
SUBLANES = 8
CONV_ROWS = 32
LANE_BLOCKS = 8
MIB = 1024 * 1024

ADAM_LR = 0.001
ADAM_B1 = 0.9
ADAM_B2 = 0.999
ADAM_EPS = 1e-08
ADAM_WD = 0.01
ADAM_STEP = 10

NN = ((1,), (0,))
NT = ((1,), (1,))
TN = ((0,), (0,))
MESH = pl.DeviceIdType.MESH
HBM_SPEC = pl.BlockSpec(memory_space=pltpu.HBM)
SEM_SPEC = pl.BlockSpec(memory_space=pltpu.SEMAPHORE)
SIDE_EFFECT = pltpu.SideEffectType.DATAFLOW_SIDE_EFFECTING

R_LN_G, R_LN_B, R_PSCALE, R_SCONV_B, R_SCONV_W = 0, 4, 8, 10, 12
R_SLN_G, R_SLN_B, R_DCONV_B, R_DN_G, R_DN_B, R_DCONV_W = 18, 20, 22, 24, 26, 28
R_SGU_B, R_VECTORS = 90, 104
Q_SCONV_W, Q_SLN_G, Q_SLN_B, Q_DCONV_B, Q_DN_G, Q_DN_B, Q_DCONV_W, Q_POOL_W, Q_ROWS = 0, 6, 8, 10, 12, 14, 16, 80, 592


def _dot(a, b, dims):
    return lax.dot_general(a.astype(MXU_DTYPE), b.astype(MXU_DTYPE), (dims, ((), ())),
                           preferred_element_type=F32)


def _params(semantics=None, vmem_mib=48):
    return pltpu.CompilerParams(dimension_semantics=semantics, vmem_limit_bytes=vmem_mib * MIB)


def _sigmoid(v):
    return 0.5 * jnp.tanh(0.5 * v) + 0.5


def _silu_and_grad(v):
    s = _sigmoid(v)
    return v * s, s * (1.0 + v * (1.0 - s))


def _row_mean(v):
    acc = v[:, 0:LANE]
    for j in range(1, v.shape[1] // LANE):
        acc = acc + v[:, j * LANE:(j + 1) * LANE]
    return jnp.sum(acc, axis=-1, keepdims=True) * (1.0 / v.shape[1])


def _ln_stats(v):
    mu = _row_mean(v)
    vc = v - mu
    var = _row_mean(vc * vc)
    rstd = lax.rsqrt(var + LN_EPS)
    return vc * rstd, rstd


def _ln_bwd_rows(dxhat, xhat, rstd):
    m1 = _row_mean(dxhat)
    m2 = _row_mean(dxhat * xhat)
    return rstd * (dxhat - m1 - xhat * m2)


def _colsum(v):
    return jnp.sum(v, axis=0, keepdims=True)


def _section(ref, k):
    return ref[:, k * D:(k + 1) * D].astype(F32)


def _adamw_math(w, g, m, v):
    m_new = ADAM_B1 * m + (1.0 - ADAM_B1) * g
    v_new = ADAM_B2 * v + (1.0 - ADAM_B2) * (g * g)
    m_hat = m_new / (1.0 - ADAM_B1 ** ADAM_STEP)
    v_hat = v_new / (1.0 - ADAM_B2 ** ADAM_STEP)
    return -ADAM_LR * (m_hat / (jnp.sqrt(v_hat) + ADAM_EPS) + ADAM_WD * w), m_new, v_new


def _mesh_pos():
    return lax.axis_index("x"), lax.axis_index("y"), lax.axis_index("c")


def _after_spec(after):
    return [] if after is None else [pl.BlockSpec(memory_space=pl.ANY)]


def _after_args(after):
    return [] if after is None else [after]


def _proj_in(xb, win_g, k, l, after=None):
    s = xb.shape[0]
    tm = min(s, 1024)

    def body(x_ref, w_ref, *rest):
        rest[-1][...] = _dot(x_ref[...], w_ref[...].reshape(D, WQ), NN).astype(BF16)

    return pl.pallas_call(
        body, name=f"proj_in_{l}", grid=(NQ, s // tm),
        in_specs=[pl.BlockSpec((tm, D), lambda q, m: (m, 0)),
                  pl.BlockSpec((None, 2, None, D // 2, WQ), lambda q, m: (q, 0, k, 0, 0))] + _after_spec(after),
        out_specs=pl.BlockSpec((tm, WQ), lambda q, m: (m, q)),
        out_shape=jax.ShapeDtypeStruct((s, DZ), BF16),
        compiler_params=_params(("arbitrary", "arbitrary")),
    )(xb, win_g, *_after_args(after))


def _proj_out_ln(ycat, wout_g, k, l, res, res_g, res_b, g, b):
    s = res.shape[0]
    tm = min(s, 512)

    def body(y_ref, w_ref, r_ref, rg_ref, rb_ref, g_ref, b_ref, xb_ref, xh_ref, rs_ref):
        y = _dot(y_ref[...], w_ref[...].reshape(DY, D), NN)
        x = r_ref[...] * rg_ref[...] + rb_ref[...]
        xhat, rstd = _ln_stats(ALPHA * x + y)
        xb_ref[...] = (xhat * g_ref[...] + b_ref[...]).astype(BF16)
        xh_ref[...] = xhat
        rs_ref[...] = rstd

    row = lambda m: (m, 0)
    vec = pl.BlockSpec((1, D), lambda m: (0, 0))
    return pl.pallas_call(
        body, name=f"proj_out_ln_{l}", grid=(s // tm,),
        in_specs=[pl.BlockSpec((tm, DY), row),
                  pl.BlockSpec((NQ, 2, None, RQ // 2, D), lambda m: (0, 0, k, 0, 0)),
                  pl.BlockSpec((tm, D), row), vec, vec, vec, vec],
        out_specs=[pl.BlockSpec((tm, D), row), pl.BlockSpec((tm, D), row), pl.BlockSpec((tm, 1), row)],
        out_shape=[jax.ShapeDtypeStruct((s, D), BF16), jax.ShapeDtypeStruct((s, D), F32),
                   jax.ShapeDtypeStruct((s, 1), F32)],
        compiler_params=_params(("arbitrary",)),
    )(ycat, wout_g, res, res_g, res_b, g, b)


def _loss_ln_bwd(xhat, rstd, g, b, target, l):
    s = xhat.shape[0]
    ts = min(s, 512)

    def body(xh_ref, rs_ref, g_ref, b_ref, t_ref, loss_ref, dr_ref, drb_ref, dg_ref, db_ref):
        @pl.when(pl.program_id(0) == 0)
        def _():
            loss_ref[...] = jnp.zeros_like(loss_ref)
            dg_ref[...] = jnp.zeros_like(dg_ref)
            db_ref[...] = jnp.zeros_like(db_ref)
        xhat_v = xh_ref[...]
        gain = g_ref[...]
        err = (xhat_v * gain + b_ref[...]) - t_ref[...]
        loss_ref[...] += 0.5 * jnp.sum(jnp.mean(err * err, axis=-1, keepdims=True), axis=0, keepdims=True)
        d = err * (1.0 / D)
        dr = _ln_bwd_rows(d * gain, xhat_v, rs_ref[...])
        dr_ref[...] = dr
        drb_ref[...] = dr.astype(BF16)
        dg_ref[...] += _colsum(d * xhat_v)
        db_ref[...] += _colsum(d)

    row = lambda m: (m, 0)
    fixed = lambda m: (0, 0)
    vec = pl.BlockSpec((1, D), fixed)
    return pl.pallas_call(
        body, name=f"loss_ln_bwd_{l}", grid=(s // ts,),
        in_specs=[pl.BlockSpec((ts, D), row), pl.BlockSpec((ts, 1), row), vec, vec, pl.BlockSpec((ts, D), row)],
        out_specs=[pl.BlockSpec((1, 1), fixed), pl.BlockSpec((ts, D), row), pl.BlockSpec((ts, D), row), vec, vec],
        out_shape=[jax.ShapeDtypeStruct((1, 1), F32), jax.ShapeDtypeStruct((s, D), F32),
                   jax.ShapeDtypeStruct((s, D), BF16), jax.ShapeDtypeStruct((1, D), F32),
                   jax.ShapeDtypeStruct((1, D), F32)],
        compiler_params=_params(("arbitrary",)),
    )(xhat, rstd, g, b, target)


def _dx_ln_bwd(dzb, win_g, k, l, dr, xhat, rstd, g, after=None):
    s = dzb.shape[0]
    tm = min(s, 512)

    def body(d_ref, w_ref, r_ref, xh_ref, rs_ref, g_ref, *rest):
        dro_ref, drb_ref, dg_ref, db_ref = rest[-4:]

        @pl.when(pl.program_id(0) == 0)
        def _():
            dg_ref[...] = jnp.zeros_like(dg_ref)
            db_ref[...] = jnp.zeros_like(db_ref)

        dx = ALPHA * r_ref[...]
        for q in range(NQ):
            dx = dx + _dot(d_ref[:, q * WQ:(q + 1) * WQ], w_ref[q].reshape(D, WQ), NT)
        xhat_v = xh_ref[...]
        dr_new = _ln_bwd_rows(dx * g_ref[...], xhat_v, rs_ref[...])
        dro_ref[...] = dr_new
        drb_ref[...] = dr_new.astype(BF16)
        dg_ref[...] += _colsum(dx * xhat_v)
        db_ref[...] += _colsum(dx)

    row = lambda m: (m, 0)
    vec = pl.BlockSpec((1, D), lambda m: (0, 0))
    return pl.pallas_call(
        body, name=f"dx_ln_bwd_{l}", grid=(s // tm,),
        in_specs=[pl.BlockSpec((tm, DZ), row),
                  pl.BlockSpec((NQ, 2, None, D // 2, WQ), lambda m: (0, 0, k, 0, 0)),
                  pl.BlockSpec((tm, D), row), pl.BlockSpec((tm, D), row), pl.BlockSpec((tm, 1), row), vec]
        + _after_spec(after),
        out_specs=[pl.BlockSpec((tm, D), row), pl.BlockSpec((tm, D), row), vec, vec],
        out_shape=[jax.ShapeDtypeStruct((s, D), F32), jax.ShapeDtypeStruct((s, D), BF16),
                   jax.ShapeDtypeStruct((1, D), F32), jax.ShapeDtypeStruct((1, D), F32)],
        compiler_params=_params(("arbitrary",), 56),
    )(dzb, win_g, dr, xhat, rstd, g, *_after_args(after))


def _dycat(drb, wout_g, k, l, after=None):
    s = drb.shape[0]
    tm = min(s, 512)

    def body(d_ref, w_ref, *rest):
        rest[-1][...] = _dot(d_ref[...], w_ref[...].reshape(DY, D), NT).astype(BF16)

    return pl.pallas_call(
        body, name=f"dycat_{l}", grid=(s // tm,),
        in_specs=[pl.BlockSpec((tm, D), lambda m: (m, 0)),
                  pl.BlockSpec((NQ, 2, None, RQ // 2, D), lambda m: (0, 0, k, 0, 0))] + _after_spec(after),
        out_specs=pl.BlockSpec((tm, DY), lambda m: (m, 0)),
        out_shape=jax.ShapeDtypeStruct((s, DY), BF16),
        compiler_params=_params(("arbitrary",)),
    )(drb, wout_g, *_after_args(after))


def _dwout(ycat, drb, l):
    s = drb.shape[0]
    tk = min(s, 1024)

    def body(y_ref, d_ref, o_ref):
        @pl.when(pl.program_id(0) == 0)
        def _():
            o_ref[...] = jnp.zeros_like(o_ref)

        o_ref[...] += _dot(y_ref[...], d_ref[...], TN)

    return pl.pallas_call(
        body, name=f"dwout_{l}", grid=(s // tk,),
        in_specs=[pl.BlockSpec((tk, DY), lambda k: (k, 0)), pl.BlockSpec((tk, D), lambda k: (k, 0))],
        out_specs=pl.BlockSpec((DY, D), lambda k: (0, 0)),
        out_shape=jax.ShapeDtypeStruct((DY, D), F32),
        compiler_params=_params(("arbitrary",)),
    )(ycat, drb)


def _dwin(xb, dzb, l):
    s = xb.shape[0]
    tk = min(s, 2048)

    def body(x_ref, d_ref, o_ref):
        @pl.when(pl.program_id(1) == 0)
        def _():
            o_ref[...] = jnp.zeros_like(o_ref)

        o_ref[...] += _dot(x_ref[...], d_ref[...], TN)

    return pl.pallas_call(
        body, name=f"dwin_{l}", grid=(NQ, s // tk),
        in_specs=[pl.BlockSpec((tk, D), lambda q, k: (k, 0)), pl.BlockSpec((tk, WQ), lambda q, k: (k, q))],
        out_specs=pl.BlockSpec((None, D, WQ), lambda q, k: (q, 0, 0)),
        out_shape=jax.ShapeDtypeStruct((NQ, D, WQ), F32),
        compiler_params=_params(("arbitrary", "arbitrary")),
    )(xb, dzb)


def _dx(dzb, win_g, k, l, dr, after=None):
    s = dzb.shape[0]
    tm = min(s, 512)

    def body(d_ref, w_ref, r_ref, *rest):
        acc = ALPHA * r_ref[...]
        for q in range(NQ):
            acc = acc + _dot(d_ref[:, q * WQ:(q + 1) * WQ], w_ref[q].reshape(D, WQ), NT)
        rest[-1][...] = acc

    return pl.pallas_call(
        body, name=f"dx_{l}", grid=(s // tm,),
        in_specs=[pl.BlockSpec((tm, DZ), lambda m: (m, 0)),
                  pl.BlockSpec((NQ, 2, None, D // 2, WQ), lambda m: (0, 0, k, 0, 0)),
                  pl.BlockSpec((tm, D), lambda m: (m, 0))] + _after_spec(after),
        out_specs=pl.BlockSpec((tm, D), lambda m: (m, 0)),
        out_shape=jax.ShapeDtypeStruct((s, D), F32),
        compiler_params=_params(("arbitrary",), 56),
    )(dzb, win_g, dr, *_after_args(after))


class _RowShifts:
    def __init__(self, ref, most_rows):
        assert ref.shape[0] == most_rows + 2 * SUBLANES
        self.ref = ref
        ref[...] = jnp.zeros(ref.shape, F32)

    def put(self, block):
        self.rows = block.shape[0]
        self.ref[SUBLANES:SUBLANES + self.rows, :] = block

    def get(self, k, causal):
        start = SUBLANES - k if causal else SUBLANES + k
        return self.ref[start:start + self.rows, :]

    def window_sums(self, block, steps, causal):
        acc = block
        for k in (1, 2, 4, 8)[:steps]:
            self.put(acc)
            acc = acc + self.get(k, causal)
        return acc


def _inv_positions(first_pos, rows):
    t1 = (lax.broadcasted_iota(jnp.int32, (rows, 1), 0) + first_pos + 1).astype(F32)
    return t1, 1.0 / t1


def _pool_inv(positions, cb):
    t1, inv_t1 = positions
    window = float(2 << (cb // 2))
    return jnp.where(t1 < window, inv_t1, 1.0 / window)


def _prev_index(ts, halo):
    return lambda i: (jnp.maximum(i * (ts // halo) - 1, 0), 0)


def _next_index(ts, halo, s):
    return lambda i: (jnp.minimum((i + 1) * (ts // halo), s // halo - 1), 0)


def _even_fwd(z, pool_w, pool_scale, sconv_w, sconv_b, l):
    s = z.shape[0]
    ts = min(s, 256)
    h = POOL_HALO

    def body(z_ref, zp_ref, pw_ref, ps_ref, cw_ref, cb_ref, o_ref, shift_ref, pooled_ref):
        i = pl.program_id(0)
        inside = i > 0
        shifts = _RowShifts(shift_ref, h + ts)
        positions = _inv_positions(i * ts, ts)

        for cb in range(LANE_BLOCKS):
            cols = slice(cb * LANE, (cb + 1) * LANE)

            def section(ref, k):
                return ref[:, k * D + cb * LANE:k * D + (cb + 1) * LANE].astype(F32)

            xa = section(z_ref, 0)
            sums = shifts.window_sums(jnp.concatenate([jnp.where(inside, section(zp_ref, 0), 0.0), xa], axis=0),
                                      cb // 2 + 1, True)
            pooled_ref[:, cols] = (sums[h:] * _pool_inv(positions, cb) - xa).astype(MXU_DTYPE)

            q_prev = jnp.where(inside, section(zp_ref, 4) * section(zp_ref, 2), 0.0)
            q_main = section(z_ref, 4) * section(z_ref, 2)
            shifts.put(jnp.concatenate([q_prev, q_main], axis=0))
            cv = (cw_ref[2:3, cols] * q_main + cw_ref[1:2, cols] * shifts.get(1, True)[h:]
                  + cw_ref[0:1, cols] * shifts.get(2, True)[h:] + cb_ref[:, cols])
            silu_gb, _ = _silu_and_grad(section(z_ref, 5))
            o_ref[:, D + cb * LANE:D + (cb + 1) * LANE] = (section(z_ref, 3) * cv * silu_gb).astype(BF16)

        for g in range(4):
            cols = slice(g * HEAD, (g + 1) * HEAD)
            p = _dot(pooled_ref[:, cols], pw_ref[g], NN)
            silu_ga, _ = _silu_and_grad(z_ref[:, D + g * HEAD:D + (g + 1) * HEAD].astype(F32))
            o_ref[:, cols] = (p * ps_ref[:, cols] * silu_ga).astype(BF16)

    fixed2 = lambda i: (0, 0)
    return pl.pallas_call(
        body, name=f"even_fwd_{l}", grid=(s // ts,),
        in_specs=[pl.BlockSpec((ts, DZ), lambda i: (i, 0)), pl.BlockSpec((h, DZ), _prev_index(ts, h)),
                  pl.BlockSpec((4, HEAD, HEAD), lambda i: (0, 0, 0)), pl.BlockSpec((1, D), fixed2),
                  pl.BlockSpec((SHORT_K, D), fixed2), pl.BlockSpec((1, D), fixed2)],
        out_specs=pl.BlockSpec((ts, DY), lambda i: (i, 0)),
        out_shape=jax.ShapeDtypeStruct((s, DY), BF16),
        scratch_shapes=[pltpu.VMEM((h + ts + 2 * SUBLANES, LANE), F32), pltpu.VMEM((ts, D), MXU_DTYPE)],
        compiler_params=_params(("arbitrary",)),
    )(z, z, pool_w, pool_scale, sconv_w, sconv_b)


def _even_bwd(z, dy, pool_w, pool_wt, pool_scale, sconv_w, sconv_b, l):
    s = z.shape[0]
    ts = min(s, 256)
    h = POOL_HALO
    n_tiles = s // ts

    def body(z_ref, zp_ref, zn_ref, dy_ref, dyn_ref, pw_ref, pwt_ref, ps_ref, cw_ref, cb_ref,
             dz_ref, dpw_ref, dps_ref, dcw_ref, dcb_ref, shift_ref, pooled_ref, p_ref, dp_ref, dpooled_ref,
             shift2_ref):
        i = pl.program_id(0)
        inside = i > 0

        @pl.when(i == 0)
        def _():
            dpw_ref[...] = jnp.zeros_like(dpw_ref)
            dps_ref[...] = jnp.zeros_like(dps_ref)
            dcw_ref[...] = jnp.zeros_like(dcw_ref)
            dcb_ref[...] = jnp.zeros_like(dcb_ref)

        shifts = _RowShifts(shift_ref, ts + h)
        positions = _inv_positions(i * ts, ts + h)
        row = lax.broadcasted_iota(jnp.int32, (ts + h, 1), 0)
        live = jnp.logical_or(i < n_tiles - 1, row < ts)

        def section(ref, k, cb):
            return ref[:, k * D + cb * LANE:k * D + (cb + 1) * LANE].astype(F32)

        for cb in range(LANE_BLOCKS):
            xa = section(z_ref, 0, cb)
            sums = shifts.window_sums(
                jnp.concatenate([jnp.where(inside, section(zp_ref, 0, cb), 0.0), xa], axis=0), cb // 2 + 1, True)
            pooled_ref[:, cb * LANE:(cb + 1) * LANE] = (
                sums[h:] * _pool_inv(positions, cb)[:ts] - xa).astype(MXU_DTYPE)
        for g in range(4):
            cols = slice(g * HEAD, (g + 1) * HEAD)
            p_ref[:, cols] = _dot(pooled_ref[:, cols], pw_ref[g], NN)
        rc = CONV_ROWS

        def block(ref, k, cb, r0, n):
            return ref[r0:r0 + n, k * D + cb * LANE:k * D + (cb + 1) * LANE].astype(F32)

        def fold(v):
            part = v[0:SUBLANES]
            for j in range(1, v.shape[0] // SUBLANES):
                part = part + v[j * SUBLANES:(j + 1) * SUBLANES]
            return part

        def added(total, v):
            return fold(v) if total is None else total + fold(v)

        for cb in range(LANE_BLOCKS):
            cols = slice(cb * LANE, (cb + 1) * LANE)
            scale = ps_ref[:, cols]
            total = None
            for r0 in range(0, ts, rc):
                silu_ga, dsilu_ga = _silu_and_grad(block(z_ref, 1, cb, r0, rc))
                d_ya = block(dy_ref, 0, cb, r0, rc)
                dp_ref[r0:r0 + rc, cols] = (d_ya * scale * silu_ga).astype(MXU_DTYPE)
                d_ya_p = d_ya * p_ref[r0:r0 + rc, cols]
                dz_ref[r0:r0 + rc, D + cb * LANE:D + (cb + 1) * LANE] = (d_ya_p * scale * dsilu_ga).astype(BF16)
                total = added(total, d_ya_p * silu_ga)
            silu_ga, _ = _silu_and_grad(block(zn_ref, 1, cb, 0, h))
            dp_ref[ts:ts + h, cols] = (block(dyn_ref, 0, cb, 0, h) * scale * silu_ga).astype(MXU_DTYPE)
            dps_ref[:, cols] += _colsum(total)
        for g in range(4):
            cols = slice(g * HEAD, (g + 1) * HEAD)
            dpooled_ref[:, cols] = _dot(dp_ref[:, cols], pwt_ref[g], NN)
            dpw_ref[g] += _dot(pooled_ref[:, cols], dp_ref[0:ts, cols], TN)
        for cb in range(LANE_BLOCKS):
            d_pooled = jnp.where(live, dpooled_ref[:, cb * LANE:(cb + 1) * LANE], 0.0)
            sums = shifts.window_sums(d_pooled * _pool_inv(positions, cb), cb // 2 + 1, False)
            dz_ref[:, cb * LANE:(cb + 1) * LANE] = (sums[:ts] - d_pooled[:ts]).astype(BF16)

        more = i < n_tiles - 1
        q0 = SUBLANES + h
        for cb in range(LANE_BLOCKS):
            cols = slice(cb * LANE, (cb + 1) * LANE)
            w0, w1, w2 = cw_ref[0:1, cols], cw_ref[1:2, cols], cw_ref[2:3, cols]
            bias = cb_ref[:, cols]
            shift_ref[SUBLANES:q0, :] = jnp.where(inside, block(zp_ref, 4, cb, 0, h) * block(zp_ref, 2, cb, 0, h), 0.0)
            for r0 in range(0, ts, rc):
                shift_ref[q0 + r0:q0 + r0 + rc, :] = block(z_ref, 4, cb, r0, rc) * block(z_ref, 2, cb, r0, rc)
                silu_gb, _ = _silu_and_grad(block(z_ref, 5, cb, r0, rc))
                shift2_ref[SUBLANES + r0:SUBLANES + r0 + rc, :] = (
                    block(dy_ref, 1, cb, r0, rc) * block(z_ref, 3, cb, r0, rc) * silu_gb)
            silu_gb, _ = _silu_and_grad(block(zn_ref, 5, cb, 0, h))
            shift2_ref[SUBLANES + ts:SUBLANES + ts + h, :] = jnp.where(
                more, block(dyn_ref, 1, cb, 0, h) * block(zn_ref, 3, cb, 0, h) * silu_gb, 0.0)

            totals = [None] * 4
            for r0 in range(0, ts, rc):
                q_main = shift_ref[q0 + r0:q0 + r0 + rc, :]
                q_1 = shift_ref[q0 + r0 - 1:q0 + r0 - 1 + rc, :]
                q_2 = shift_ref[q0 + r0 - 2:q0 + r0 - 2 + rc, :]
                cv = w2 * q_main + w1 * q_1 + w0 * q_2 + bias
                d0 = SUBLANES + r0
                d_cv0 = shift2_ref[d0:d0 + rc, :]
                d_q = w2 * d_cv0 + w1 * shift2_ref[d0 + 1:d0 + 1 + rc, :] + w0 * shift2_ref[d0 + 2:d0 + 2 + rc, :]
                silu_gb, dsilu_gb = _silu_and_grad(block(z_ref, 5, cb, r0, rc))
                d_yb_cv = block(dy_ref, 1, cb, r0, rc) * cv
                for k, val in ((2, d_q * block(z_ref, 4, cb, r0, rc)), (3, d_yb_cv * silu_gb),
                               (4, d_q * block(z_ref, 2, cb, r0, rc)),
                               (5, d_yb_cv * block(z_ref, 3, cb, r0, rc) * dsilu_gb)):
                    dz_ref[r0:r0 + rc, k * D + cb * LANE:k * D + (cb + 1) * LANE] = val.astype(BF16)
                for j, val in enumerate((d_cv0, d_cv0 * q_main, d_cv0 * q_1, d_cv0 * q_2)):
                    totals[j] = added(totals[j], val)
            dcb_ref[:, cols] += _colsum(totals[0])
            dcw_ref[2:3, cols] += _colsum(totals[1])
            dcw_ref[1:2, cols] += _colsum(totals[2])
            dcw_ref[0:1, cols] += _colsum(totals[3])

    fixed2 = lambda i: (0, 0)
    fixed3 = lambda i: (0, 0, 0)
    return pl.pallas_call(
        body, name=f"even_bwd_{l}", grid=(n_tiles,),
        in_specs=[pl.BlockSpec((ts, DZ), lambda i: (i, 0)), pl.BlockSpec((h, DZ), _prev_index(ts, h)),
                  pl.BlockSpec((h, DZ), _next_index(ts, h, s)),
                  pl.BlockSpec((ts, DY), lambda i: (i, 0)), pl.BlockSpec((h, DY), _next_index(ts, h, s)),
                  pl.BlockSpec((4, HEAD, HEAD), fixed3), pl.BlockSpec((4, HEAD, HEAD), fixed3),
                  pl.BlockSpec((1, D), fixed2), pl.BlockSpec((SHORT_K, D), fixed2), pl.BlockSpec((1, D), fixed2)],
        out_specs=[pl.BlockSpec((ts, DZ), lambda i: (i, 0)), pl.BlockSpec((4, HEAD, HEAD), fixed3),
                   pl.BlockSpec((1, D), fixed2), pl.BlockSpec((SHORT_K, D), fixed2), pl.BlockSpec((1, D), fixed2)],
        out_shape=[jax.ShapeDtypeStruct((s, DZ), BF16), jax.ShapeDtypeStruct((4, HEAD, HEAD), F32),
                   jax.ShapeDtypeStruct((1, D), F32), jax.ShapeDtypeStruct((SHORT_K, D), F32),
                   jax.ShapeDtypeStruct((1, D), F32)],
        scratch_shapes=[pltpu.VMEM((ts + h + 2 * SUBLANES, LANE), F32), pltpu.VMEM((ts, D), MXU_DTYPE),
                        pltpu.VMEM((ts, D), F32), pltpu.VMEM((ts + h, D), MXU_DTYPE), pltpu.VMEM((ts + h, D), F32),
                        pltpu.VMEM((ts + h + 2 * SUBLANES, LANE), F32)],
        compiler_params=_params(("arbitrary",), 56),
    )(z, z, z, dy, dy, pool_w, pool_wt, pool_scale, sconv_w, sconv_b)


def _to_blocks(ref, r0, val):
    n = val.shape[0]
    for cb in range(LANE_BLOCKS):
        ref[cb, r0:r0 + n, :] = val[:, cb * LANE:(cb + 1) * LANE]


def _from_blocks(ref):
    return jnp.concatenate([ref[cb] for cb in range(LANE_BLOCKS)], axis=1)


def _shift_copies(src_ref, sh_ref, n, causal):
    def block(cb, carry):
        for b in range(1, SUBLANES):
            if causal:
                sh_ref[cb, b - 1, SUBLANES:n, :] = src_ref[cb, SUBLANES - b:n - b, :]
            else:
                sh_ref[cb, b - 1, 0:n - SUBLANES, :] = src_ref[cb, b:n - SUBLANES + b, :]
        return carry

    lax.fori_loop(0, LANE_BLOCKS, block, 0)


def _tap(src_ref, sh_ref, cb, first, n, d, causal):
    whole, b = (d // SUBLANES) * SUBLANES, d % SUBLANES
    start = first - whole if causal else first + whole
    if b == 0:
        return src_ref[cb, start:start + n, :]
    return sh_ref[cb, b - 1, start:start + n, :]


def _chunk_rows(rows, most):
    return max(n for n in range(CONV_ROWS, most + 1, CONV_ROWS) if rows % n == 0)


def _conv31(src_ref, sh_ref, w_ref, dst_ref, base, rows, causal):
    n = _chunk_rows(rows, 4 * CONV_ROWS)

    def block(cb, carry):
        for r0 in range(0, rows, n):
            acc = None
            for d in range(CONV_K):
                term = w_ref[cb, CONV_K - 1 - d:CONV_K - d, :] * _tap(src_ref, sh_ref, cb, base + r0, n, d, causal)
                acc = term if acc is None else acc + term
            dst_ref[cb, r0:r0 + n, :] = acc
        return carry

    lax.fori_loop(0, LANE_BLOCKS, block, 0)


def _conv31_tap_grads(d_ref, src_ref, sh_ref, dw_ref, base, rows):
    n = _chunk_rows(rows, 2 * CONV_ROWS)

    def block(cb, carry):
        sums = [None] * CONV_K
        for r0 in range(0, rows, n):
            d_blk = d_ref[cb, r0:r0 + n, :]
            for d in range(CONV_K):
                prod = d_blk * _tap(src_ref, sh_ref, cb, base + r0, n, d, True)
                part = prod[0:SUBLANES]
                for k in range(1, n // SUBLANES):
                    part = part + prod[k * SUBLANES:(k + 1) * SUBLANES]
                sums[d] = part if sums[d] is None else sums[d] + part
        for d in range(CONV_K):
            j = CONV_K - 1 - d
            dw_ref[cb, j:j + 1, :] += _colsum(sums[d])
        return carry

    lax.fori_loop(0, LANE_BLOCKS, block, 0)


def _sgu_bias_rows(sgu_b):
    return jnp.repeat(jnp.transpose(sgu_b), HEAD, axis=1)


def _odd_fwd(z, sln_g, sln_b, ws, sbias, dconv_w, dconv_b, dn_g, dn_b, l):
    s = z.shape[0]
    ts = min(s, 256)
    h = CONV_HALO

    def body(z_ref, zp_ref, lg_ref, lb_ref, ws_ref, sb_ref, cw_ref, cb_ref, ng_ref, nb_ref, o_ref, zh_ref, rz_ref,
             zz_ref, zc_ref, sh_ref):
        i = pl.program_id(0)
        vhat, _ = _ln_stats(_section(z_ref, 1))
        vn = (vhat * lg_ref[...] + lb_ref[...]).astype(MXU_DTYPE)
        silu_gc, _ = _silu_and_grad(_section(z_ref, 2))
        for n in range(ts // SGU_BLOCK):
            rows = slice(n * SGU_BLOCK, (n + 1) * SGU_BLOCK)
            sv = jnp.concatenate(
                [_dot(ws_ref[hd], vn[rows, hd * HEAD:(hd + 1) * HEAD], NN) for hd in range(4)], axis=1)
            sv = sv + sb_ref[...]
            o_ref[rows, 0:D] = (z_ref[rows, 0:D].astype(F32) * sv * silu_gc[rows]).astype(BF16)

        _to_blocks(zz_ref, 0, jnp.where(i > 0, _section(zp_ref, 3) * _sigmoid(_section(zp_ref, 4)), 0.0))
        _to_blocks(zz_ref, h, _section(z_ref, 3) * _sigmoid(_section(z_ref, 4)))
        _shift_copies(zz_ref, sh_ref, h + ts, True)
        _conv31(zz_ref, sh_ref, cw_ref, zc_ref, h, ts, True)
        zhat, rstd_z = _ln_stats(_from_blocks(zc_ref) + cb_ref[...])
        zh_ref[...] = zhat
        rz_ref[...] = rstd_z
        silu_zn, _ = _silu_and_grad(zhat * ng_ref[...] + nb_ref[...])
        silu_gd, _ = _silu_and_grad(_section(z_ref, 5))
        o_ref[:, D:2 * D] = (silu_zn * silu_gd).astype(BF16)

    fixed2 = lambda i: (0, 0)
    vec = pl.BlockSpec((1, D), fixed2)
    return pl.pallas_call(
        body, name=f"odd_fwd_{l}", grid=(s // ts,),
        in_specs=[pl.BlockSpec((ts, DZ), lambda i: (i, 0)), pl.BlockSpec((h, DZ), _prev_index(ts, h)),
                  vec, vec, pl.BlockSpec((4, SGU_BLOCK, SGU_BLOCK), lambda i: (0, 0, 0)),
                  pl.BlockSpec((SGU_BLOCK, D), fixed2), pl.BlockSpec((LANE_BLOCKS, CONV_K, LANE), lambda i: (0, 0, 0)),
                  vec, vec, vec],
        out_specs=[pl.BlockSpec((ts, DY), lambda i: (i, 0)), pl.BlockSpec((ts, D), lambda i: (i, 0)),
                   pl.BlockSpec((ts, 1), lambda i: (i, 0))],
        out_shape=[jax.ShapeDtypeStruct((s, DY), BF16), jax.ShapeDtypeStruct((s, D), F32),
                   jax.ShapeDtypeStruct((s, 1), F32)],
        scratch_shapes=[pltpu.VMEM((LANE_BLOCKS, h + ts, LANE), F32), pltpu.VMEM((LANE_BLOCKS, ts, LANE), F32),
                        pltpu.VMEM((LANE_BLOCKS, SUBLANES - 1, h + ts, LANE), F32)],
        compiler_params=_params(("arbitrary",)),
    )(z, z, sln_g, sln_b, ws, sbias, dconv_w, dconv_b, dn_g, dn_b)


def _odd_bwd(z, dy, zhat_s, rstd_s, sln_g, sln_b, ws, wst, sbias, dconv_w, dn_g, dn_b, l):
    s = z.shape[0]
    ts = min(s, 256)
    h = CONV_HALO
    n_tiles = s // ts
    te = ts + h

    def body(z_ref, zp_ref, zn_ref, dy_ref, dyn_ref, zh_ref, zhn_ref, rz_ref, rzn_ref, lg_ref, lb_ref, ws_ref,
             wst_ref, sb_ref, cw_ref, ng_ref, nb_ref, dz_ref, dlg_ref, dlb_ref, dws_ref, dsb_ref, dcw_ref, dcb_ref,
             dng_ref, dnb_ref, zz_ref, dzc_ref, dzz_ref, dsb_acc, sh_ref):
        i = pl.program_id(0)
        more = i < n_tiles - 1

        @pl.when(i == 0)
        def _():
            for ref in (dlg_ref, dlb_ref, dws_ref, dsb_ref, dcw_ref, dcb_ref, dng_ref, dnb_ref, dsb_acc):
                ref[...] = jnp.zeros_like(ref)

        vhat, rstd_v = _ln_stats(_section(z_ref, 1))
        lg = lg_ref[...]
        vn = (vhat * lg + lb_ref[...]).astype(MXU_DTYPE)
        u = _section(z_ref, 0)
        silu_gc, dsilu_gc = _silu_and_grad(_section(z_ref, 2))
        d_yc = _section(dy_ref, 0)
        d_yc_u = d_yc * u
        d_sv = d_yc_u * silu_gc
        d_svb = d_sv.astype(MXU_DTYPE)
        sv_rows = []
        dvn_rows = []
        dsb = None
        for n in range(ts // SGU_BLOCK):
            rows = slice(n * SGU_BLOCK, (n + 1) * SGU_BLOCK)
            sv_parts = []
            dvn_parts = []
            for hd in range(4):
                cols = slice(hd * HEAD, (hd + 1) * HEAD)
                sv_parts.append(_dot(ws_ref[hd], vn[rows, cols], NN))
                dvn_parts.append(_dot(wst_ref[hd], d_svb[rows, cols], NN))
                dws_ref[hd] += _dot(d_svb[rows, cols], vn[rows, cols], NT)
            sv_rows.append(jnp.concatenate(sv_parts, axis=1) + sb_ref[...])
            dvn_rows.append(jnp.concatenate(dvn_parts, axis=1))
            dsb = d_sv[rows] if dsb is None else dsb + d_sv[rows]
        dsb_acc[...] += dsb

        @pl.when(i == n_tiles - 1)
        def _():
            for hd in range(4):
                blk = dsb_acc[:, hd * HEAD:(hd + 1) * HEAD]
                folded = blk[:, 0:LANE] + blk[:, LANE:HEAD]
                dsb_ref[hd:hd + 1, :] = _colsum(jnp.transpose(folded))
        sv = jnp.concatenate(sv_rows, axis=0)
        d_vn = jnp.concatenate(dvn_rows, axis=0)
        dz_ref[:, 0:D] = (d_yc * sv * silu_gc).astype(BF16)
        dz_ref[:, D:2 * D] = _ln_bwd_rows(d_vn * lg, vhat, rstd_v).astype(BF16)
        dz_ref[:, 2 * D:3 * D] = (d_yc_u * sv * dsilu_gc).astype(BF16)
        dlg_ref[...] += _colsum(d_vn * vhat)
        dlb_ref[...] += _colsum(d_vn)

        def gate(ref):
            return _section(ref, 3) * _sigmoid(_section(ref, 4))

        _to_blocks(zz_ref, 0, jnp.where(i > 0, gate(zp_ref), 0.0))
        _to_blocks(zz_ref, h, gate(z_ref))
        _shift_copies(zz_ref, sh_ref, h + ts, True)
        zhat = jnp.concatenate([zh_ref[...], zhn_ref[...]], axis=0)
        rstd_z = jnp.concatenate([rz_ref[...], rzn_ref[...]], axis=0)
        ng = ng_ref[...]
        silu_zn, dsilu_zn = _silu_and_grad(zhat * ng + nb_ref[...])
        gd = jnp.concatenate([_section(z_ref, 5), _section(zn_ref, 5)], axis=0)
        silu_gd, dsilu_gd = _silu_and_grad(gd)
        d_yd = jnp.concatenate([_section(dy_ref, 1), _section(dyn_ref, 1)], axis=0)
        d_zn = d_yd * silu_gd * dsilu_zn
        d_zc = _ln_bwd_rows(d_zn * ng, zhat, rstd_z)
        row = lax.broadcasted_iota(jnp.int32, (te, 1), 0)
        d_zc = jnp.where(jnp.logical_or(more, row < ts), d_zc, 0.0)
        _to_blocks(dzc_ref, 0, d_zc)
        dz_ref[:, 5 * D:6 * D] = (d_yd[:ts] * silu_zn[:ts] * dsilu_gd[:ts]).astype(BF16)
        dng_ref[...] += _colsum(d_zn[:ts] * zhat[:ts])
        dnb_ref[...] += _colsum(d_zn[:ts])
        dcb_ref[...] += _colsum(d_zc[:ts])
        _conv31_tap_grads(dzc_ref, zz_ref, sh_ref, dcw_ref, h, ts)
        _shift_copies(dzc_ref, sh_ref, te, False)
        _conv31(dzc_ref, sh_ref, cw_ref, dzz_ref, 0, ts, False)
        d_zz = _from_blocks(dzz_ref)
        a = _section(z_ref, 3)
        sig_b = _sigmoid(_section(z_ref, 4))
        dz_ref[:, 3 * D:4 * D] = (d_zz * sig_b).astype(BF16)
        dz_ref[:, 4 * D:5 * D] = (d_zz * a * sig_b * (1.0 - sig_b)).astype(BF16)

    fixed2 = lambda i: (0, 0)
    fixed3 = lambda i: (0, 0, 0)
    vec = pl.BlockSpec((1, D), fixed2)
    mat = pl.BlockSpec((4, SGU_BLOCK, SGU_BLOCK), fixed3)
    vec_shape = jax.ShapeDtypeStruct((1, D), F32)
    return pl.pallas_call(
        body, name=f"odd_bwd_{l}", grid=(n_tiles,),
        in_specs=[pl.BlockSpec((ts, DZ), lambda i: (i, 0)), pl.BlockSpec((h, DZ), _prev_index(ts, h)),
                  pl.BlockSpec((h, DZ), _next_index(ts, h, s)),
                  pl.BlockSpec((ts, DY), lambda i: (i, 0)), pl.BlockSpec((h, DY), _next_index(ts, h, s)),
                  pl.BlockSpec((ts, D), lambda i: (i, 0)), pl.BlockSpec((h, D), _next_index(ts, h, s)),
                  pl.BlockSpec((ts, 1), lambda i: (i, 0)), pl.BlockSpec((h, 1), _next_index(ts, h, s)),
                  vec, vec, mat, mat, pl.BlockSpec((SGU_BLOCK, D), fixed2),
                  pl.BlockSpec((LANE_BLOCKS, CONV_K, LANE), fixed3), vec, vec],
        out_specs=[pl.BlockSpec((ts, DZ), lambda i: (i, 0)), vec, vec, mat, pl.BlockSpec((4, SGU_BLOCK), fixed2),
                   pl.BlockSpec((LANE_BLOCKS, CONV_K, LANE), fixed3), vec, vec, vec],
        out_shape=[jax.ShapeDtypeStruct((s, DZ), BF16), vec_shape, vec_shape,
                   jax.ShapeDtypeStruct((4, SGU_BLOCK, SGU_BLOCK), F32), jax.ShapeDtypeStruct((4, SGU_BLOCK), F32),
                   jax.ShapeDtypeStruct((LANE_BLOCKS, CONV_K, LANE), F32), vec_shape, vec_shape, vec_shape],
        scratch_shapes=[pltpu.VMEM((LANE_BLOCKS, h + ts, LANE), F32), pltpu.VMEM((LANE_BLOCKS, te, LANE), F32),
                        pltpu.VMEM((LANE_BLOCKS, ts, LANE), F32), pltpu.VMEM((SGU_BLOCK, D), F32),
                        pltpu.VMEM((LANE_BLOCKS, SUBLANES - 1, te, LANE), F32)],
        compiler_params=_params(("arbitrary",), 60),
    )(z, z, z, dy, dy, zhat_s, zhat_s, rstd_s, rstd_s, sln_g, sln_b, ws, wst, sbias, dconv_w, dn_g, dn_b)


def _remote(src, dst, send_sems, recv_sems, k, to):
    return pltpu.make_async_remote_copy(src_ref=src, dst_ref=dst, send_sem=send_sems.at[k],
                                        recv_sem=recv_sems.at[k], device_id=to, device_id_type=MESH)


def _other_chips(x, y):
    return [(1 - x, y, 2 * (1 - x) + y), (x, 1 - y, 2 * x + 1 - y), (1 - x, 1 - y, 2 * (1 - x) + 1 - y)]


def _cast_own(w_stack, pos_arr, name):
    slots, rows, cols = w_stack.shape
    half = rows // 2

    def body(pos_ref, w_ref, o_ref):
        o_ref[...] = w_ref[...].astype(BF16)

    grid_spec = pltpu.PrefetchScalarGridSpec(
        num_scalar_prefetch=1, grid=(slots, 2),
        in_specs=[pl.BlockSpec((None, half, cols), lambda s, h, pos: (s, h, 0))],
        out_specs=pl.BlockSpec((None, None, None, half, cols), lambda s, h, pos: (pos[1], h, s, 0, 0)))
    return pl.pallas_call(
        body, name=name, grid_spec=grid_spec, out_shape=jax.ShapeDtypeStruct((NQ, 2, slots, half, cols), BF16),
        compiler_params=_params(("arbitrary",) * 2),
    )(pos_arr, w_stack)


def _gather_weights(win_g, wout_g, small_sh):
    def body(win_in, wout_in, small, win_g, wout_g, small_g, send_sems, recv_sems):
        del win_in, wout_in
        x, y, c = _mesh_pos()
        me = 2 * x + y
        sibling = (x, y, 1 - c)
        chips = _other_chips(x, y)

        sends = []
        for j, (cx, cy, _) in enumerate(chips):
            to = (cx, cy, c)
            sends.append(_remote(win_g.at[me, c], win_g.at[me, c], send_sems, recv_sems, j, to))
            sends.append(_remote(wout_g.at[me, c], wout_g.at[me, c], send_sems, recv_sems, 3 + j, to))
            sends.append(_remote(small, small_g.at[me], send_sems, recv_sems, 6 + j, to))
        for cp in sends:
            cp.start()
        passed = []
        for j, (_, _, q) in enumerate(chips):
            got_in = win_g.at[q, c]
            got_out = wout_g.at[q, c]
            _remote(got_in, got_in, send_sems, recv_sems, j, sibling).wait_recv()
            cp = _remote(got_in, got_in, send_sems, recv_sems, 9 + j, sibling)
            cp.start()
            passed.append(cp)
            _remote(got_out, got_out, send_sems, recv_sems, 3 + j, sibling).wait_recv()
            cp = _remote(got_out, got_out, send_sems, recv_sems, 12 + j, sibling)
            cp.start()
            passed.append(cp)
            _remote(small, small_g.at[q], send_sems, recv_sems, 6 + j, sibling).wait_recv()
        for j, (_, _, q) in enumerate(chips):
            from_in = win_g.at[q, 1 - c]
            from_out = wout_g.at[q, 1 - c]
            _remote(from_in, from_in, send_sems, recv_sems, 9 + j, sibling).wait_recv()
            _remote(from_out, from_out, send_sems, recv_sems, 12 + j, sibling).wait_recv()
        for cp in sends + passed:
            cp.wait_send()

    return pl.pallas_call(
        body, name="gather_weights",
        in_specs=[HBM_SPEC, HBM_SPEC, HBM_SPEC], out_specs=[HBM_SPEC, HBM_SPEC, HBM_SPEC],
        out_shape=[jax.ShapeDtypeStruct(win_g.shape, win_g.dtype), jax.ShapeDtypeStruct(wout_g.shape, wout_g.dtype),
                   jax.ShapeDtypeStruct((NQ,) + small_sh.shape, small_sh.dtype)],
        input_output_aliases={0: 0, 1: 1},
        scratch_shapes=[pltpu.SemaphoreType.DMA((15,)), pltpu.SemaphoreType.DMA((15,))],
    )(win_g, wout_g, small_sh)


def _hbm(a):
    return pltpu.with_memory_space_constraint(a, pltpu.HBM)


def _split_start(body, name, sources, landings):
    n_src, n_land = len(sources), len(landings)
    n_buf = n_src + n_land

    def kernel_body(*refs):
        ins, outs = refs[:n_buf], refs[n_buf:]
        send_sems, recv_sems, token = outs[0], outs[1], outs[2 + n_buf]
        body(ins[:n_src], ins[n_src:], send_sems, recv_sems)
        token[...] = jnp.zeros_like(token)

    bufs = [_hbm(a) for a in sources] + [
        _hbm(lax.empty(s.shape, s.dtype) if isinstance(s, jax.ShapeDtypeStruct) else s) for s in landings]
    n_sem = getattr(body, "n_copies")
    out = pl.pallas_call(
        kernel_body, name=name,
        out_shape=(pltpu.SemaphoreType.DMA((n_sem,)), pltpu.SemaphoreType.DMA((n_sem,)),
                   *[pltpu.HBM(b.shape, b.dtype) for b in bufs], jax.ShapeDtypeStruct((8, LANE), F32)),
        in_specs=(HBM_SPEC,) * n_buf,
        out_specs=(SEM_SPEC, SEM_SPEC, *([HBM_SPEC] * n_buf), pl.BlockSpec(memory_space=pltpu.VMEM)),
        input_output_aliases={k: 2 + k for k in range(n_buf)},
        compiler_params=pltpu.CompilerParams(has_side_effects=SIDE_EFFECT),
    )(*bufs)
    return out[0], out[1], list(out[2:2 + n_src]), list(out[2 + n_src:2 + n_buf]), out[2 + n_buf]


def _split_wait(body, name, send_sems, recv_sems, sources, landings, after):
    n_src, n_land = len(sources), len(landings)
    n_buf = n_src + n_land

    def kernel_body(*refs):
        ins = refs[:n_buf]
        body(ins[:n_src], ins[n_src:], refs[n_buf], refs[n_buf + 1])

    bufs = list(sources) + list(landings)
    out = pl.pallas_call(
        kernel_body, name=name,
        out_shape=tuple(pltpu.HBM(b.shape, b.dtype) for b in bufs),
        in_specs=(*([HBM_SPEC] * n_buf), SEM_SPEC, SEM_SPEC, pl.BlockSpec(memory_space=pl.ANY)),
        out_specs=(HBM_SPEC,) * n_buf,
        input_output_aliases={k: k for k in range(n_buf)},
        compiler_params=pltpu.CompilerParams(has_side_effects=SIDE_EFFECT),
    )(*bufs, send_sems, recv_sems, after)
    return list(out[:n_src]), list(out[n_src:])


def _gather_rest_copies(start):
    def body(srcs, lands, send_sems, recv_sems):
        del srcs
        x, y, c = _mesh_pos()
        me = 2 * x + y
        for j, (cx, cy, q) in enumerate(_other_chips(x, y)):
            to = (cx, cy, c)
            for k, gathered in enumerate(lands):
                if start:
                    _remote(gathered.at[me, c], gathered.at[me, c], send_sems, recv_sems, 3 * k + j, to).start()
                else:
                    cp = _remote(gathered.at[me, c], gathered.at[q, c], send_sems, recv_sems, 3 * k + j, to)
                    cp.wait_send()
                    cp.wait_recv()

    body.n_copies = 6
    return body


def _gather_rest_forward(win_g, wout_g):
    def body(win_in, wout_in, win_g, wout_g, send_sems, recv_sems):
        del win_in, wout_in
        x, y, c = _mesh_pos()
        sibling = (x, y, 1 - c)
        passed = []
        for j, (_, _, q) in enumerate(_other_chips(x, y)):
            got_in = win_g.at[q, c]
            got_out = wout_g.at[q, c]
            passed.append(_remote(got_in, got_in, send_sems, recv_sems, j, sibling))
            passed.append(_remote(got_out, got_out, send_sems, recv_sems, 3 + j, sibling))
        for cp in passed:
            cp.start()
        for j, (_, _, q) in enumerate(_other_chips(x, y)):
            from_in = win_g.at[q, 1 - c]
            from_out = wout_g.at[q, 1 - c]
            _remote(from_in, from_in, send_sems, recv_sems, j, sibling).wait_recv()
            _remote(from_out, from_out, send_sems, recv_sems, 3 + j, sibling).wait_recv()
        for cp in passed:
            cp.wait_send()

    return pl.pallas_call(
        body, name="gather_rest_forward",
        in_specs=[HBM_SPEC] * 2, out_specs=[HBM_SPEC] * 2,
        out_shape=[jax.ShapeDtypeStruct(win_g.shape, win_g.dtype), jax.ShapeDtypeStruct(wout_g.shape, wout_g.dtype)],
        input_output_aliases={0: 0, 1: 1},
        scratch_shapes=[pltpu.SemaphoreType.DMA((6,)), pltpu.SemaphoreType.DMA((6,))],
    )(win_g, wout_g)


def _allreduce_small(groups, after):
    pieces = [p for _, _, _, members in groups for _, p in members]
    n_in, n_g = len(pieces), len(groups)

    def body(*refs):
        ins = refs[:n_in]
        outs = refs[-(2 * n_g + 2):-(n_g + 2)]
        alls = refs[-(n_g + 2):-2]
        send_sems, recv_sems = refs[-2:]
        x, y, c = _mesh_pos()
        me = 4 * x + 2 * y + c
        sibling = (x, y, 1 - c)
        chips = _other_chips(x, y)

        k = 0
        for (rows, cols, dtype, members), all_ref in zip(groups, alls):
            if any(piece.shape[1] < cols for _, piece in members) or sum(p.shape[0] for _, p in members) < rows:
                all_ref[me] = jnp.zeros((rows, cols), dtype)
            for first, piece in members:
                n, width = piece.shape
                all_ref[me, first:first + n, 0:width] = ins[k][...].astype(dtype)
                k += 1

        sends, passed = [], []
        for g, all_ref in enumerate(alls):
            sends.append(_remote(all_ref.at[me], all_ref.at[me], send_sems, recv_sems, 7 * g, sibling))
            for j, (cx, cy, _) in enumerate(chips):
                sends.append(_remote(all_ref.at[me], all_ref.at[me], send_sems, recv_sems, 7 * g + 1 + j, (cx, cy, c)))
        for cp in sends:
            cp.start()
        for j, (cx, cy, _) in enumerate(chips):
            for g, all_ref in enumerate(alls):
                got = all_ref.at[4 * cx + 2 * cy + c]
                _remote(got, got, send_sems, recv_sems, 7 * g + 1 + j, sibling).wait_recv()
                cp = _remote(got, got, send_sems, recv_sems, 7 * g + 4 + j, sibling)
                cp.start()
                passed.append(cp)
        for g, all_ref in enumerate(alls):
            got = all_ref.at[4 * x + 2 * y + 1 - c]
            _remote(got, got, send_sems, recv_sems, 7 * g, sibling).wait_recv()
            for j, (cx, cy, _) in enumerate(chips):
                got = all_ref.at[4 * cx + 2 * cy + 1 - c]
                _remote(got, got, send_sems, recv_sems, 7 * g + 4 + j, sibling).wait_recv()
        for cp in sends + passed:
            cp.wait_send()
        for o_ref, all_ref in zip(outs, alls):
            total = all_ref[0].astype(F32)
            for dev in range(1, 8):
                total = total + all_ref[dev].astype(F32)
            o_ref[...] = total

    vmem = pl.BlockSpec(memory_space=pltpu.VMEM)
    return pl.pallas_call(
        body, name="allreduce_small",
        in_specs=[vmem] * n_in + _after_spec(after),
        out_specs=[vmem] * n_g,
        out_shape=[jax.ShapeDtypeStruct((rows, cols), F32) for rows, cols, _, _ in groups],
        scratch_shapes=[pltpu.VMEM((8, rows, cols), dtype) for rows, cols, dtype, _ in groups]
        + [pltpu.SemaphoreType.DMA((7 * n_g,)), pltpu.SemaphoreType.DMA((7 * n_g,))],
        compiler_params=_params(None, 56),
    )(*pieces, *_after_args(after))


def _pair_copies(start):
    def body(srcs, lands, send_sems, recv_sems):
        x, y, c = _mesh_pos()
        sibling = (x, y, 1 - c)
        for k, (g_ref, r_ref) in enumerate(zip(srcs, lands)):
            half = g_ref.shape[1] // 2
            cp = _remote(g_ref.at[:, pl.ds((1 - c) * half, half), :], r_ref, send_sems, recv_sems, k, sibling)
            if start:
                cp.start()
            else:
                cp.wait_send()
                cp.wait_recv()

    body.n_copies = 2
    return body


def _pair_sum(g, r, pos_arr, name):
    nq, rows, cols = r.shape
    tr = min(rows, 256)
    nt = rows // tr

    def body(pos_ref, g_ref, r_ref, ob_ref, own_ref):
        total = g_ref[...] + r_ref[...]
        ob_ref[...] = total.astype(BF16)

        @pl.when(pl.program_id(1) == pos_ref[1])
        def _():
            own_ref[...] = total

    blk = (None, tr, cols)
    grid_spec = pltpu.PrefetchScalarGridSpec(
        num_scalar_prefetch=1, grid=(nt, nq),
        in_specs=[pl.BlockSpec(blk, lambda t, q, pos: (q, pos[0] * nt + t, 0)),
                  pl.BlockSpec(blk, lambda t, q, pos: (q, t, 0))],
        out_specs=[pl.BlockSpec(blk, lambda t, q, pos: (q, t, 0)),
                   pl.BlockSpec((tr, cols), lambda t, q, pos: (t, 0))])
    return pl.pallas_call(
        body, name=name, grid_spec=grid_spec,
        out_shape=[jax.ShapeDtypeStruct(r.shape, BF16), jax.ShapeDtypeStruct((rows, cols), F32)],
        compiler_params=_params(("arbitrary",) * 2),
    )(pos_arr, g, r)


def _chip_copies(start):
    def body(srcs, lands, send_sems, recv_sems):
        pin, pout = srcs
        rin, rout = lands
        x, y, c = _mesh_pos()
        for j, (cx, cy, q) in enumerate(_other_chips(x, y)):
            to = (cx, cy, c)
            for k, (src, dst) in enumerate(((pin, rin), (pout, rout))):
                cp = _remote(src.at[q], dst.at[j], send_sems, recv_sems, 3 * k + j, to)
                if start:
                    cp.start()
                else:
                    cp.wait_send()
                    cp.wait_recv()

    body.n_copies = 6
    return body


def _chip_sum(own, r, pos_arr, name):
    rows, cols = own.shape
    tr = min(rows, 256)

    def body(pos_ref, p_ref, r0_ref, r1_ref, r2_ref, o_ref):
        o_ref[...] = ((p_ref[...] + r0_ref[...].astype(F32)) + r1_ref[...].astype(F32)) + r2_ref[...].astype(F32)

    def peer(j):
        return pl.BlockSpec((None, tr, cols), lambda t, pos: (j, t, 0))

    grid_spec = pltpu.PrefetchScalarGridSpec(
        num_scalar_prefetch=1, grid=(rows // tr,),
        in_specs=[pl.BlockSpec((tr, cols), lambda t, pos: (t, 0)), peer(0), peer(1), peer(2)],
        out_specs=pl.BlockSpec((None, tr, cols), lambda t, pos: (pos[0], t, 0)))
    return pl.pallas_call(
        body, name=name, grid_spec=grid_spec, out_shape=jax.ShapeDtypeStruct((2, rows, cols), F32),
        compiler_params=_params(("arbitrary",)),
    )(pos_arr, own, r, r, r)


def _pair_share(gin, gout, l):
    def body(gin_in, gout_in, gin_ref, gout_ref, send_sems, recv_sems):
        del gin_in, gout_in
        x, y, c = _mesh_pos()
        sibling = (x, y, 1 - c)
        sends = [_remote(gin_ref.at[c], gin_ref.at[c], send_sems, recv_sems, 0, sibling),
                 _remote(gout_ref.at[c], gout_ref.at[c], send_sems, recv_sems, 1, sibling)]
        for cp in sends:
            cp.start()
        _remote(gin_ref.at[1 - c], gin_ref.at[1 - c], send_sems, recv_sems, 0, sibling).wait_recv()
        _remote(gout_ref.at[1 - c], gout_ref.at[1 - c], send_sems, recv_sems, 1, sibling).wait_recv()
        for cp in sends:
            cp.wait_send()

    return pl.pallas_call(
        body, name=f"pair_share_{l}",
        in_specs=[HBM_SPEC, HBM_SPEC], out_specs=[HBM_SPEC, HBM_SPEC],
        out_shape=[jax.ShapeDtypeStruct(gin.shape, F32), jax.ShapeDtypeStruct(gout.shape, F32)],
        input_output_aliases={0: 0, 1: 1},
        scratch_shapes=[pltpu.SemaphoreType.DMA((2,)), pltpu.SemaphoreType.DMA((2,))],
    )(gin, gout)


def _adamw_large(w, m, v, g, i, prev, name):
    _, rows, cols = w.shape
    half = rows // 2
    tr = min(half, 256)
    nt = half // tr

    def body(w_ref, m_ref, v_ref, g_ref, *rest):
        go_ref, d_ref, mo_ref, vo_ref = rest[-4:]
        gv = g_ref[...]
        go_ref[...] = gv
        d_ref[...], mo_ref[...], vo_ref[...] = _adamw_math(w_ref[...], gv, m_ref[...], v_ref[...])

    full = pl.BlockSpec((None, tr, cols), lambda h, t: (i, h * nt + t, 0))
    out = jax.ShapeDtypeStruct(w.shape, F32)
    carried = [] if prev is None else list(prev)
    return pl.pallas_call(
        body, name=name, grid=(2, nt),
        in_specs=[full, full, full, pl.BlockSpec((None, tr, cols), lambda h, t: (h, t, 0))]
        + [pl.BlockSpec(memory_space=pl.ANY)] * len(carried),
        out_specs=[full] * 4, out_shape=[out] * 4,
        input_output_aliases={4 + k: k for k in range(len(carried))},
        compiler_params=_params(("arbitrary",) * 2),
    )(w, m, v, g, *carried)


def _adamw(w, g, m, v, name):
    shape = w.shape
    w2, g2, m2, v2 = (t.reshape(-1, shape[-1]) for t in (w, g, m, v))
    rows, cols = w2.shape
    tr = 256 if rows % 256 == 0 else rows

    def body(w_ref, g_ref, m_ref, v_ref, d_ref, mo_ref, vo_ref):
        d_ref[...], mo_ref[...], vo_ref[...] = _adamw_math(w_ref[...], g_ref[...], m_ref[...], v_ref[...])

    blk = pl.BlockSpec((tr, cols), lambda i: (i, 0))
    out = jax.ShapeDtypeStruct((rows, cols), F32)
    d, mo, vo = pl.pallas_call(
        body, name=name, grid=(rows // tr,), in_specs=[blk] * 4, out_specs=[blk] * 3, out_shape=[out] * 3,
        compiler_params=_params(("arbitrary",)),
    )(w2, g2, m2, v2)
    return d.reshape(shape), mo.reshape(shape), vo.reshape(shape)


def _layer_slot(l):
    return (l % 2) * 2 + l // 2


def kernel(x, ln_g, ln_b, w_in_even, w_out_even, pool_w, pool_scale, sconv_w, sconv_b, w_in_odd, w_out_odd, sgu_ln_g, sgu_ln_b, sgu_w, sgu_b, dconv_w, dconv_b, dnorm_g, dnorm_b, loss_target, m_ln_g, m_ln_b, m_w_in_even, m_w_out_even, m_pool_w, m_pool_scale, m_sconv_w, m_sconv_b, m_w_in_odd, m_w_out_odd, m_sgu_ln_g, m_sgu_ln_b, m_sgu_w, m_sgu_b, m_dconv_w, m_dconv_b, m_dnorm_g, m_dnorm_b, v_ln_g, v_ln_b, v_w_in_even, v_w_out_even, v_pool_w, v_pool_scale, v_sconv_w, v_sconv_b, v_w_in_odd, v_w_out_odd, v_sgu_ln_g, v_sgu_ln_b, v_sgu_w, v_sgu_b, v_dconv_w, v_dconv_b, v_dnorm_g, v_dnorm_b):
    weights = dict(ln_g=ln_g, ln_b=ln_b, w_in_even=w_in_even, w_out_even=w_out_even, pool_w=pool_w,
                   pool_scale=pool_scale, sconv_w=sconv_w, sconv_b=sconv_b, w_in_odd=w_in_odd, w_out_odd=w_out_odd,
                   sgu_ln_g=sgu_ln_g, sgu_ln_b=sgu_ln_b, sgu_w=sgu_w, sgu_b=sgu_b, dconv_w=dconv_w,
                   dconv_b=dconv_b, dnorm_g=dnorm_g, dnorm_b=dnorm_b)
    moments_m = dict(ln_g=m_ln_g, ln_b=m_ln_b, w_in_even=m_w_in_even, w_out_even=m_w_out_even, pool_w=m_pool_w,
                     pool_scale=m_pool_scale, sconv_w=m_sconv_w, sconv_b=m_sconv_b, w_in_odd=m_w_in_odd,
                     w_out_odd=m_w_out_odd, sgu_ln_g=m_sgu_ln_g, sgu_ln_b=m_sgu_ln_b, sgu_w=m_sgu_w, sgu_b=m_sgu_b,
                     dconv_w=m_dconv_w, dconv_b=m_dconv_b, dnorm_g=m_dnorm_g, dnorm_b=m_dnorm_b)
    moments_v = dict(ln_g=v_ln_g, ln_b=v_ln_b, w_in_even=v_w_in_even, w_out_even=v_w_out_even, pool_w=v_pool_w,
                     pool_scale=v_pool_scale, sconv_w=v_sconv_w, sconv_b=v_sconv_b, w_in_odd=v_w_in_odd,
                     w_out_odd=v_w_out_odd, sgu_ln_g=v_sgu_ln_g, sgu_ln_b=v_sgu_ln_b, sgu_w=v_sgu_w, sgu_b=v_sgu_b,
                     dconv_w=v_dconv_w, dconv_b=v_dconv_b, dnorm_g=v_dnorm_g, dnorm_b=v_dnorm_b)
    names = list(weights)

    xd, yd, cd = _mesh_pos()
    chip = 2 * xd + yd
    pos_arr = jnp.stack([cd, chip]).astype(jnp.int32)

    small_sh = jnp.concatenate(
        [sconv_w.reshape(6, HEAD), sgu_ln_g, sgu_ln_b, dconv_b, dnorm_g, dnorm_b, dconv_w.reshape(62, HEAD),
         jnp.zeros((2, HEAD), F32), pool_w.reshape(512, HEAD)], axis=0)
    win_first, wout_first, small_g = _gather_weights(
        _cast_own(w_in_even[0:1], pos_arr, "cast_win_first"), _cast_own(w_out_even[0:1], pos_arr, "cast_wout_first"),
        small_sh)
    small_g = lax.dynamic_update_slice(small_g, small_sh[None], (chip, 0, 0))
    later_in = jnp.concatenate([w_in_even[1:2], w_in_odd], axis=0)
    later_out = jnp.concatenate([w_out_even[1:2], w_out_odd], axis=0)
    g_send, g_recv, _, g_lands, g_token = _split_start(
        _gather_rest_copies(True), "gather_rest_start", [],
        [_cast_own(later_in, pos_arr, "cast_win_rest"), _cast_own(later_out, pos_arr, "cast_wout_rest")])

    def layer_weights(slot):
        return (win_first, wout_first, 0) if slot == 0 else (win_rest, wout_rest, slot - 1)

    def full_rows(lo, n):
        return jnp.transpose(small_g[:, lo:lo + n], (1, 0, 2)).reshape(n, D)

    sconv_w_f = full_rows(Q_SCONV_W, 6).reshape(2, SHORT_K, D)
    sln_g_f = full_rows(Q_SLN_G, 2)
    sln_b_f = full_rows(Q_SLN_B, 2)
    dconv_b_f = full_rows(Q_DCONV_B, 2)
    dn_g_f = full_rows(Q_DN_G, 2)
    dn_b_f = full_rows(Q_DN_B, 2)
    dconv_w_f = jnp.transpose(full_rows(Q_DCONV_W, 62).reshape(2, CONV_K, LANE_BLOCKS, LANE), (0, 2, 1, 3))
    pool_w_f = jnp.transpose(small_g[:, Q_POOL_W:].reshape(NQ, 2, 4, 64, HEAD), (1, 2, 0, 3, 4)).reshape(2, 4, HEAD, HEAD)
    pool_w_b = pool_w_f.astype(BF16)
    pool_wt_b = jnp.swapaxes(pool_w_f, 2, 3).astype(BF16)
    idx = jnp.arange(SGU_BLOCK)
    mask = (idx[None, :] // 64) <= (idx[:, None] // 64)
    ws_f = jnp.where(mask[None, None], sgu_w, 0.0)
    ws_b = ws_f.astype(BF16)
    wst_b = jnp.swapaxes(ws_f, 2, 3).astype(BF16)

    def row(a, i):
        return a[i:i + 1]

    residual = (x[0], jnp.ones((1, D), F32), jnp.zeros((1, D), F32))
    x_b = x[0].astype(BF16)
    saved = []
    conv_saved = {}
    for l in range(NL):
        i, slot = l // 2, _layer_slot(l)
        if l == 1:
            _, g_lands = _split_wait(_gather_rest_copies(False), "gather_rest_wait", g_send, g_recv, [], g_lands, x_b)
            win_rest, wout_rest = _gather_rest_forward(g_lands[0], g_lands[1])
        win_g, wout_g, k = layer_weights(slot)
        z = _proj_in(x_b, win_g, k, l, g_token if l == 0 else None)
        if l % 2 == 0:
            ycat = _even_fwd(z, pool_w_b[i], row(pool_scale, i), sconv_w_f[i], row(sconv_b, i), l)
        else:
            ycat, conv_hat, conv_rstd = _odd_fwd(
                z, row(sln_g_f, i), row(sln_b_f, i), ws_b[i], _sgu_bias_rows(sgu_b[i]),
                dconv_w_f[i], row(dconv_b_f, i), row(dn_g_f, i), row(dn_b_f, i), l)
            conv_saved[l] = (conv_hat, conv_rstd)
        x_next_b, xhat, rstd = _proj_out_ln(ycat, wout_g, k, l, *residual, row(ln_g, l), row(ln_b, l))
        saved.append((x_b, z, ycat, xhat, rstd))
        residual = (xhat, row(ln_g, l), row(ln_b, l))
        x_b = x_next_b

    small = {}
    d_ln_g = [None] * NL
    d_ln_b = [None] * NL
    large = {"w_in_even": None, "w_out_even": None, "w_in_odd": None, "w_out_odd": None}
    pending = None
    token = None

    def finish(exchange, after):
        lx, send, recv, srcs, lands, own_in, own_out = exchange
        _, (r_in, r_out) = _split_wait(_chip_copies(False), f"chip_wait_{lx}", send, recv, srcs, lands, after)
        fin = _chip_sum(own_in, r_in, pos_arr, f"chip_sum_in_{lx}")
        fout = _chip_sum(own_out, r_out, pos_arr, f"chip_sum_out_{lx}")
        gs_in, gs_out = _pair_share(fin, fout, lx)
        kind = "even" if lx % 2 == 0 else "odd"
        for nm, gs in ((f"w_in_{kind}", gs_in), (f"w_out_{kind}", gs_out)):
            large[nm] = _adamw_large(weights[nm], moments_m[nm], moments_v[nm], gs, lx // 2, large[nm],
                                     f"adamw_{nm}_{lx // 2}")
        return large[f"w_out_{kind}"][0]

    for l in reversed(range(NL)):
        i, slot = l // 2, _layer_slot(l)
        win_g, wout_g, k = layer_weights(slot)
        xin_b, z, ycat, xhat, rstd = saved[l]
        if l == NL - 1:
            loss_part, dr, dr_b, d_ln_g[l], d_ln_b[l] = _loss_ln_bwd(
                xhat, rstd, row(ln_g, l), row(ln_b, l), loss_target[0], l)
            loss = lax.psum(loss_part[0, 0], ("x", "y", "c"))
        else:
            dr, dr_b, d_ln_g[l], d_ln_b[l] = ln_done
        dy = _dycat(dr_b, wout_g, k, l, token)
        gout = _dwout(ycat, dr_b, l).reshape(NQ, RQ, D)
        if l % 2 == 0:
            dz, d_pw, d_ps, d_cw, d_cb = _even_bwd(z, dy, pool_w_b[i], pool_wt_b[i], row(pool_scale, i),
                                                   sconv_w_f[i], row(sconv_b, i), l)
            small[("pool_w", i)] = d_pw
            small[("pool_scale", i)] = d_ps
            small[("sconv_w", i)] = d_cw
            small[("sconv_b", i)] = d_cb
        else:
            dz, d_lg, d_lb, d_ws, d_sb, d_cw, d_cb, d_ng, d_nb = _odd_bwd(
                z, dy, *conv_saved[l], row(sln_g_f, i), row(sln_b_f, i), ws_b[i], wst_b[i],
                _sgu_bias_rows(sgu_b[i]), dconv_w_f[i], row(dn_g_f, i), row(dn_b_f, i), l)
            small[("sgu_ln_g", i)] = d_lg
            small[("sgu_ln_b", i)] = d_lb
            small[("sgu_w", i)] = jnp.where(mask[None], d_ws, 0.0)
            small[("sgu_b", i)] = d_sb
            small[("dconv_w", i)] = jnp.transpose(d_cw, (1, 0, 2)).reshape(CONV_K, D)
            small[("dconv_b", i)] = d_cb
            small[("dnorm_g", i)] = d_ng
            small[("dnorm_b", i)] = d_nb
        gin = _dwin(xin_b, dz, l)
        p_send, p_recv, p_srcs, p_lands, p_token = _split_start(
            _pair_copies(True), f"pair_start_{l}", [gin, gout],
            [jax.ShapeDtypeStruct((NQ, D // 2, WQ), F32), jax.ShapeDtypeStruct((NQ, RQ // 2, D), F32)])
        if l > 0:
            ln_done = _dx_ln_bwd(dz, win_g, k, l, dr, saved[l - 1][3], saved[l - 1][4], row(ln_g, l - 1), p_token)
            wait_after = ln_done[0]
            if pending is not None:
                finish(pending, wait_after)
        else:
            wait_after = finish(pending, p_token)
        (gin, gout), (rin, rout) = _split_wait(_pair_copies(False), f"pair_wait_{l}", p_send, p_recv,
                                               p_srcs, p_lands, wait_after)
        pin_b, pin_own = _pair_sum(gin, rin, pos_arr, f"pair_sum_in_{l}")
        pout_b, pout_own = _pair_sum(gout, rout, pos_arr, f"pair_sum_out_{l}")
        send, recv, srcs, lands, token = _split_start(
            _chip_copies(True), f"chip_start_{l}", [pin_b, pout_b],
            [jax.ShapeDtypeStruct((3,) + pin_b.shape[1:], BF16), jax.ShapeDtypeStruct((3,) + pout_b.shape[1:], BF16)])
        if l == 0:
            dxn = _dx(dz, win_g, k, l, dr, token)
        pending = (l, send, recv, srcs, lands, pin_own, pout_own)
    grad_x = dxn[None]

    def both(name, first, step):
        return [(first, small[(name, 0)]), (first + step, small[(name, 1)])]

    vectors = ([(R_LN_G + l, d_ln_g[l]) for l in range(NL)] + [(R_LN_B + l, d_ln_b[l]) for l in range(NL)]
               + both("pool_scale", R_PSCALE, 1) + both("sconv_b", R_SCONV_B, 1) + both("sconv_w", R_SCONV_W, SHORT_K)
               + both("sgu_ln_g", R_SLN_G, 1) + both("sgu_ln_b", R_SLN_B, 1) + both("dconv_b", R_DCONV_B, 1)
               + both("dnorm_g", R_DN_G, 1) + both("dnorm_b", R_DN_B, 1) + both("dconv_w", R_DCONV_W, CONV_K)
               + both("sgu_b", R_SGU_B, 4))
    sgu_w_rows = 4 * SGU_BLOCK
    pool_w_rows = 4 * HEAD
    total, total_sgu_w, total_pool_w = _allreduce_small(
        [(R_VECTORS, D, F32, vectors),
         (2 * sgu_w_rows, SGU_BLOCK, BF16,
          [(i * sgu_w_rows, small[("sgu_w", i)].reshape(sgu_w_rows, SGU_BLOCK)) for i in range(2)]),
         (2 * pool_w_rows, HEAD, BF16,
          [(i * pool_w_rows, small[("pool_w", i)].reshape(pool_w_rows, HEAD)) for i in range(2)])],
        dxn)
    finish(pending, total)

    def mine(a):
        return lax.dynamic_slice_in_dim(a, chip * HEAD, HEAD, axis=a.ndim - 1)

    grads = {
        "ln_g": total[R_LN_G:R_LN_G + 4],
        "ln_b": total[R_LN_B:R_LN_B + 4],
        "pool_scale": total[R_PSCALE:R_PSCALE + 2],
        "sconv_b": total[R_SCONV_B:R_SCONV_B + 2],
        "sconv_w": mine(total[R_SCONV_W:R_SCONV_W + 6].reshape(2, SHORT_K, D)),
        "sgu_ln_g": mine(total[R_SLN_G:R_SLN_G + 2]),
        "sgu_ln_b": mine(total[R_SLN_B:R_SLN_B + 2]),
        "dconv_b": mine(total[R_DCONV_B:R_DCONV_B + 2]),
        "dnorm_g": mine(total[R_DN_G:R_DN_G + 2]),
        "dnorm_b": mine(total[R_DN_B:R_DN_B + 2]),
        "dconv_w": mine(total[R_DCONV_W:R_DCONV_W + 62].reshape(2, CONV_K, D)),
        "sgu_b": total[R_SGU_B:R_SGU_B + 8, 0:SGU_BLOCK].reshape(2, 4, SGU_BLOCK),
        "sgu_w": total_sgu_w.reshape(2, 4, SGU_BLOCK, SGU_BLOCK),
        "pool_w": lax.dynamic_slice_in_dim(total_pool_w.reshape(2, 4, HEAD, HEAD), chip * 64, 64, axis=2),
    }

    deltas, new_m, new_v = {}, {}, {}
    for name in names:
        if name in large:
            grads[name], deltas[name], new_m[name], new_v[name] = large[name]
        else:
            deltas[name], new_m[name], new_v[name] = _adamw(
                weights[name], grads[name], moments_m[name], moments_v[name], f"adamw_{name}")

    return (loss, grad_x, *[grads[n] for n in names], *[deltas[n] for n in names],
            *[new_m[n] for n in names], *[new_v[n] for n in names])
```

```python
import jax
import jax.numpy as jnp
from jax import lax
from jax.experimental import pallas as pl
from jax.experimental.pallas import tpu as pltpu

F32 = jnp.float32
BF16 = jnp.bfloat16
MXU_DTYPE = BF16

D = 1024
DZ = 6144
DY = 2048
NQ = 4
WQ = DZ // NQ
RQ = DY // NQ
NL = 4
ALPHA = (2 * NL) ** 0.25
LN_EPS = 1e-5
CONV_K = 31
SHORT_K = 3
SGU_BLOCK = 128
HEAD = 256
POOL_HALO = 16
CONV_HALO = 32
LANE = 128
SUBLANES = 8
CONV_ROWS = 32
LANE_BLOCKS = 8
MIB = 1024 * 1024

ADAM_LR = 0.001
ADAM_B1 = 0.9
ADAM_B2 = 0.999
ADAM_EPS = 1e-08
ADAM_WD = 0.01
ADAM_STEP = 10

NN = ((1,), (0,))
NT = ((1,), (1,))
TN = ((0,), (0,))
MESH = pl.DeviceIdType.MESH
HBM_SPEC = pl.BlockSpec(memory_space=pltpu.HBM)
SEM_SPEC = pl.BlockSpec(memory_space=pltpu.SEMAPHORE)
SIDE_EFFECT = pltpu.SideEffectType.DATAFLOW_SIDE_EFFECTING

R_LN_G, R_LN_B, R_PSCALE, R_SCONV_B, R_SCONV_W = 0, 4, 8, 10, 12
R_SLN_G, R_SLN_B, R_DCONV_B, R_DN_G, R_DN_B, R_DCONV_W = 18, 20, 22, 24, 26, 28
R_SGU_B, R_VECTORS = 90, 104
Q_SCONV_W, Q_SLN_G, Q_SLN_B, Q_DCONV_B, Q_DN_G, Q_DN_B, Q_DCONV_W, Q_POOL_W, Q_ROWS = 0, 6, 8, 10, 12, 14, 16, 80, 592


def _dot(a, b, dims):
    return lax.dot_general(a.astype(MXU_DTYPE), b.astype(MXU_DTYPE), (dims, ((), ())),
                           preferred_element_type=F32)


def _params(semantics=None, vmem_mib=48):
    return pltpu.CompilerParams(dimension_semantics=semantics, vmem_limit_bytes=vmem_mib * MIB)


def _sigmoid(v):
    return 0.5 * jnp.tanh(0.5 * v) + 0.5


def _silu_and_grad(v):
    s = _sigmoid(v)
    return v * s, s * (1.0 + v * (1.0 - s))


def _row_mean(v):
    acc = v[:, 0:LANE]
    for j in range(1, v.shape[1] // LANE):
        acc = acc + v[:, j * LANE:(j + 1) * LANE]
    return jnp.sum(acc, axis=-1, keepdims=True) * (1.0 / v.shape[1])


def _ln_stats(v):
    mu = _row_mean(v)
    vc = v - mu
    var = _row_mean(vc * vc)
    rstd = lax.rsqrt(var + LN_EPS)
    return vc * rstd, rstd


def _ln_bwd_rows(dxhat, xhat, rstd):
    m1 = _row_mean(dxhat)
    m2 = _row_mean(dxhat * xhat)
    return rstd * (dxhat - m1 - xhat * m2)


def _colsum(v):
    return jnp.sum(v, axis=0, keepdims=True)


def _section(ref, k):
    return ref[:, k * D:(k + 1) * D].astype(F32)


def _adamw_math(w, g, m, v):
    m_new = ADAM_B1 * m + (1.0 - ADAM_B1) * g
    v_new = ADAM_B2 * v + (1.0 - ADAM_B2) * (g * g)
    m_hat = m_new / (1.0 - ADAM_B1 ** ADAM_STEP)
    v_hat = v_new / (1.0 - ADAM_B2 ** ADAM_STEP)
    return -ADAM_LR * (m_hat / (jnp.sqrt(v_hat) + ADAM_EPS) + ADAM_WD * w), m_new, v_new


def _mesh_pos():
    return lax.axis_index("x"), lax.axis_index("y"), lax.axis_index("c")


def _after_spec(after):
    return [] if after is None else [pl.BlockSpec(memory_space=pl.ANY)]


def _after_args(after):
    return [] if after is None else [after]


def _proj_in(xb, win_g, k, l, after=None):
    s = xb.shape[0]
    tm = min(s, 1024)

    def body(x_ref, w_ref, *rest):
        rest[-1][...] = _dot(x_ref[...], w_ref[...].reshape(D, WQ), NN).astype(BF16)

    return pl.pallas_call(
        body, name=f"proj_in_{l}", grid=(NQ, s // tm),
        in_specs=[pl.BlockSpec((tm, D), lambda q, m: (m, 0)),
                  pl.BlockSpec((None, 2, None, D // 2, WQ), lambda q, m: (q, 0, k, 0, 0))] + _after_spec(after),
        out_specs=pl.BlockSpec((tm, WQ), lambda q, m: (m, q)),
        out_shape=jax.ShapeDtypeStruct((s, DZ), BF16),
        compiler_params=_params(("arbitrary", "arbitrary")),
    )(xb, win_g, *_after_args(after))


def _proj_out_ln(ycat, wout_g, k, l, res, res_g, res_b, g, b):
    s = res.shape[0]
    tm = min(s, 512)

    def body(y_ref, w_ref, r_ref, rg_ref, rb_ref, g_ref, b_ref, xb_ref, xh_ref, rs_ref):
        y = _dot(y_ref[...], w_ref[...].reshape(DY, D), NN)
        x = r_ref[...] * rg_ref[...] + rb_ref[...]
        xhat, rstd = _ln_stats(ALPHA * x + y)
        xb_ref[...] = (xhat * g_ref[...] + b_ref[...]).astype(BF16)
        xh_ref[...] = xhat
        rs_ref[...] = rstd

    row = lambda m: (m, 0)
    vec = pl.BlockSpec((1, D), lambda m: (0, 0))
    return pl.pallas_call(
        body, name=f"proj_out_ln_{l}", grid=(s // tm,),
        in_specs=[pl.BlockSpec((tm, DY), row),
                  pl.BlockSpec((NQ, 2, None, RQ // 2, D), lambda m: (0, 0, k, 0, 0)),
                  pl.BlockSpec((tm, D), row), vec, vec, vec, vec],
        out_specs=[pl.BlockSpec((tm, D), row), pl.BlockSpec((tm, D), row), pl.BlockSpec((tm, 1), row)],
        out_shape=[jax.ShapeDtypeStruct((s, D), BF16), jax.ShapeDtypeStruct((s, D), F32),
                   jax.ShapeDtypeStruct((s, 1), F32)],
        compiler_params=_params(("arbitrary",)),
    )(ycat, wout_g, res, res_g, res_b, g, b)


def _loss_ln_bwd(xhat, rstd, g, b, target, l):
    s = xhat.shape[0]
    ts = min(s, 512)

    def body(xh_ref, rs_ref, g_ref, b_ref, t_ref, loss_ref, dr_ref, drb_ref, dg_ref, db_ref):
        @pl.when(pl.program_id(0) == 0)
        def _():
            loss_ref[...] = jnp.zeros_like(loss_ref)
            dg_ref[...] = jnp.zeros_like(dg_ref)
            db_ref[...] = jnp.zeros_like(db_ref)
        xhat_v = xh_ref[...]
        gain = g_ref[...]
        err = (xhat_v * gain + b_ref[...]) - t_ref[...]
        loss_ref[...] += 0.5 * jnp.sum(jnp.mean(err * err, axis=-1, keepdims=True), axis=0, keepdims=True)
        d = err * (1.0 / D)
        dr = _ln_bwd_rows(d * gain, xhat_v, rs_ref[...])
        dr_ref[...] = dr
        drb_ref[...] = dr.astype(BF16)
        dg_ref[...] += _colsum(d * xhat_v)
        db_ref[...] += _colsum(d)

    row = lambda m: (m, 0)
    fixed = lambda m: (0, 0)
    vec = pl.BlockSpec((1, D), fixed)
    return pl.pallas_call(
        body, name=f"loss_ln_bwd_{l}", grid=(s // ts,),
        in_specs=[pl.BlockSpec((ts, D), row), pl.BlockSpec((ts, 1), row), vec, vec, pl.BlockSpec((ts, D), row)],
        out_specs=[pl.BlockSpec((1, 1), fixed), pl.BlockSpec((ts, D), row), pl.BlockSpec((ts, D), row), vec, vec],
        out_shape=[jax.ShapeDtypeStruct((1, 1), F32), jax.ShapeDtypeStruct((s, D), F32),
                   jax.ShapeDtypeStruct((s, D), BF16), jax.ShapeDtypeStruct((1, D), F32),
                   jax.ShapeDtypeStruct((1, D), F32)],
        compiler_params=_params(("arbitrary",)),
    )(xhat, rstd, g, b, target)


def _dx_ln_bwd(dzb, win_g, k, l, dr, xhat, rstd, g, after=None):
    s = dzb.shape[0]
    tm = min(s, 512)

    def body(d_ref, w_ref, r_ref, xh_ref, rs_ref, g_ref, *rest):
        dro_ref, drb_ref, dg_ref, db_ref = rest[-4:]

        @pl.when(pl.program_id(0) == 0)
        def _():
            dg_ref[...] = jnp.zeros_like(dg_ref)
            db_ref[...] = jnp.zeros_like(db_ref)

        dx = ALPHA * r_ref[...]
        for q in range(NQ):
            dx = dx + _dot(d_ref[:, q * WQ:(q + 1) * WQ], w_ref[q].reshape(D, WQ), NT)
        xhat_v = xh_ref[...]
        dr_new = _ln_bwd_rows(dx * g_ref[...], xhat_v, rs_ref[...])
        dro_ref[...] = dr_new
        drb_ref[...] = dr_new.astype(BF16)
        dg_ref[...] += _colsum(dx * xhat_v)
        db_ref[...] += _colsum(dx)

    row = lambda m: (m, 0)
    vec = pl.BlockSpec((1, D), lambda m: (0, 0))
    return pl.pallas_call(
        body, name=f"dx_ln_bwd_{l}", grid=(s // tm,),
        in_specs=[pl.BlockSpec((tm, DZ), row),
                  pl.BlockSpec((NQ, 2, None, D // 2, WQ), lambda m: (0, 0, k, 0, 0)),
                  pl.BlockSpec((tm, D), row), pl.BlockSpec((tm, D), row), pl.BlockSpec((tm, 1), row), vec]
        + _after_spec(after),
        out_specs=[pl.BlockSpec((tm, D), row), pl.BlockSpec((tm, D), row), vec, vec],
        out_shape=[jax.ShapeDtypeStruct((s, D), F32), jax.ShapeDtypeStruct((s, D), BF16),
                   jax.ShapeDtypeStruct((1, D), F32), jax.ShapeDtypeStruct((1, D), F32)],
        compiler_params=_params(("arbitrary",), 56),
    )(dzb, win_g, dr, xhat, rstd, g, *_after_args(after))


def _dycat_into(dy_s, drb_ref, drbn_ref, wo_ref, ts, h):
    w_out = wo_ref[...].reshape(DY, D)
    dy_s[0:ts, :] = _dot(drb_ref[...], w_out, NT).astype(BF16)
    dy_s[ts:ts + h, :] = _dot(drbn_ref[...], w_out, NT).astype(BF16)
    return dy_s.at[pl.ds(0, ts)], dy_s.at[pl.ds(ts, h)]


def _dwout(ycat, drb, l, after=None):
    s = drb.shape[0]
    tk = min(s, 1024)

    def body(y_ref, d_ref, *rest):
        o_ref = rest[-1]

        @pl.when(pl.program_id(0) == 0)
        def _():
            o_ref[...] = jnp.zeros_like(o_ref)

        o_ref[...] += _dot(y_ref[...], d_ref[...], TN)

    return pl.pallas_call(
        body, name=f"dwout_{l}", grid=(s // tk,),
        in_specs=[pl.BlockSpec((tk, DY), lambda k: (k, 0)), pl.BlockSpec((tk, D), lambda k: (k, 0))]
        + _after_spec(after),
        out_specs=pl.BlockSpec((DY, D), lambda k: (0, 0)),
        out_shape=jax.ShapeDtypeStruct((DY, D), F32),
        compiler_params=_params(("arbitrary",)),
    )(ycat, drb, *_after_args(after))


def _dwin(xb, dzb, l):
    s = xb.shape[0]
    tk = min(s, 2048)

    def body(x_ref, d_ref, o_ref):
        @pl.when(pl.program_id(1) == 0)
        def _():
            o_ref[...] = jnp.zeros_like(o_ref)

        o_ref[...] += _dot(x_ref[...], d_ref[...], TN)

    return pl.pallas_call(
        body, name=f"dwin_{l}", grid=(NQ, s // tk),
        in_specs=[pl.BlockSpec((tk, D), lambda q, k: (k, 0)), pl.BlockSpec((tk, WQ), lambda q, k: (k, q))],
        out_specs=pl.BlockSpec((None, D, WQ), lambda q, k: (q, 0, 0)),
        out_shape=jax.ShapeDtypeStruct((NQ, D, WQ), F32),
        compiler_params=_params(("arbitrary", "arbitrary")),
    )(xb, dzb)


def _dx(dzb, win_g, k, l, dr, after=None):
    s = dzb.shape[0]
    tm = min(s, 512)

    def body(d_ref, w_ref, r_ref, *rest):
        acc = ALPHA * r_ref[...]
        for q in range(NQ):
            acc = acc + _dot(d_ref[:, q * WQ:(q + 1) * WQ], w_ref[q].reshape(D, WQ), NT)
        rest[-1][...] = acc

    return pl.pallas_call(
        body, name=f"dx_{l}", grid=(s // tm,),
        in_specs=[pl.BlockSpec((tm, DZ), lambda m: (m, 0)),
                  pl.BlockSpec((NQ, 2, None, D // 2, WQ), lambda m: (0, 0, k, 0, 0)),
                  pl.BlockSpec((tm, D), lambda m: (m, 0))] + _after_spec(after),
        out_specs=pl.BlockSpec((tm, D), lambda m: (m, 0)),
        out_shape=jax.ShapeDtypeStruct((s, D), F32),
        compiler_params=_params(("arbitrary",), 56),
    )(dzb, win_g, dr, *_after_args(after))


class _RowShifts:
    def __init__(self, ref, most_rows):
        assert ref.shape[0] == most_rows + 2 * SUBLANES
        self.ref = ref
        ref[...] = jnp.zeros(ref.shape, F32)

    def put(self, block):
        self.rows = block.shape[0]
        self.ref[SUBLANES:SUBLANES + self.rows, :] = block

    def get(self, k, causal):
        start = SUBLANES - k if causal else SUBLANES + k
        return self.ref[start:start + self.rows, :]

    def window_sums(self, block, steps, causal):
        acc = block
        for k in (1, 2, 4, 8)[:steps]:
            self.put(acc)
            acc = acc + self.get(k, causal)
        return acc


def _inv_positions(first_pos, rows):
    t1 = (lax.broadcasted_iota(jnp.int32, (rows, 1), 0) + first_pos + 1).astype(F32)
    return t1, 1.0 / t1


def _pool_inv(positions, cb):
    t1, inv_t1 = positions
    window = float(2 << (cb // 2))
    return jnp.where(t1 < window, inv_t1, 1.0 / window)


def _prev_index(ts, halo):
    return lambda i: (jnp.maximum(i * (ts // halo) - 1, 0), 0)


def _next_index(ts, halo, s):
    return lambda i: (jnp.minimum((i + 1) * (ts // halo), s // halo - 1), 0)


def _even_fwd(z, pool_w, pool_scale, sconv_w, sconv_b, l):
    s = z.shape[0]
    ts = min(s, 256)
    h = POOL_HALO

    def body(z_ref, zp_ref, pw_ref, ps_ref, cw_ref, cb_ref, o_ref, shift_ref, pooled_ref):
        i = pl.program_id(0)
        inside = i > 0
        shifts = _RowShifts(shift_ref, h + ts)
        positions = _inv_positions(i * ts, ts)

        for cb in range(LANE_BLOCKS):
            cols = slice(cb * LANE, (cb + 1) * LANE)

            def section(ref, k):
                return ref[:, k * D + cb * LANE:k * D + (cb + 1) * LANE].astype(F32)

            xa = section(z_ref, 0)
            sums = shifts.window_sums(jnp.concatenate([jnp.where(inside, section(zp_ref, 0), 0.0), xa], axis=0),
                                      cb // 2 + 1, True)
            pooled_ref[:, cols] = (sums[h:] * _pool_inv(positions, cb) - xa).astype(MXU_DTYPE)

            q_prev = jnp.where(inside, section(zp_ref, 4) * section(zp_ref, 2), 0.0)
            q_main = section(z_ref, 4) * section(z_ref, 2)
            shifts.put(jnp.concatenate([q_prev, q_main], axis=0))
            cv = (cw_ref[2:3, cols] * q_main + cw_ref[1:2, cols] * shifts.get(1, True)[h:]
                  + cw_ref[0:1, cols] * shifts.get(2, True)[h:] + cb_ref[:, cols])
            silu_gb, _ = _silu_and_grad(section(z_ref, 5))
            o_ref[:, D + cb * LANE:D + (cb + 1) * LANE] = (section(z_ref, 3) * cv * silu_gb).astype(BF16)

        for g in range(4):
            cols = slice(g * HEAD, (g + 1) * HEAD)
            p = _dot(pooled_ref[:, cols], pw_ref[g], NN)
            silu_ga, _ = _silu_and_grad(z_ref[:, D + g * HEAD:D + (g + 1) * HEAD].astype(F32))
            o_ref[:, cols] = (p * ps_ref[:, cols] * silu_ga).astype(BF16)

    fixed2 = lambda i: (0, 0)
    return pl.pallas_call(
        body, name=f"even_fwd_{l}", grid=(s // ts,),
        in_specs=[pl.BlockSpec((ts, DZ), lambda i: (i, 0)), pl.BlockSpec((h, DZ), _prev_index(ts, h)),
                  pl.BlockSpec((4, HEAD, HEAD), lambda i: (0, 0, 0)), pl.BlockSpec((1, D), fixed2),
                  pl.BlockSpec((SHORT_K, D), fixed2), pl.BlockSpec((1, D), fixed2)],
        out_specs=pl.BlockSpec((ts, DY), lambda i: (i, 0)),
        out_shape=jax.ShapeDtypeStruct((s, DY), BF16),
        scratch_shapes=[pltpu.VMEM((h + ts + 2 * SUBLANES, LANE), F32), pltpu.VMEM((ts, D), MXU_DTYPE)],
        compiler_params=_params(("arbitrary",)),
    )(z, z, pool_w, pool_scale, sconv_w, sconv_b)


def _even_bwd(z, drb, wout_g, k, pool_w, pool_wt, pool_scale, sconv_w, sconv_b, l):
    s = z.shape[0]
    ts = min(s, 256)
    h = POOL_HALO
    n_tiles = s // ts

    def body(z_ref, zp_ref, zn_ref, drb_ref, drbn_ref, wo_ref, pw_ref, pwt_ref, ps_ref, cw_ref, cb_ref,
             dz_ref, dpw_ref, dps_ref, dcw_ref, dcb_ref, shift_ref, pooled_ref, p_ref, dp_ref, dpooled_ref,
             shift2_ref, dy_s):
        i = pl.program_id(0)
        inside = i > 0
        dy_ref, dyn_ref = _dycat_into(dy_s, drb_ref, drbn_ref, wo_ref, ts, h)

        @pl.when(i == 0)
        def _():
            dpw_ref[...] = jnp.zeros_like(dpw_ref)
            dps_ref[...] = jnp.zeros_like(dps_ref)
            dcw_ref[...] = jnp.zeros_like(dcw_ref)
            dcb_ref[...] = jnp.zeros_like(dcb_ref)

        shifts = _RowShifts(shift_ref, ts + h)
        positions = _inv_positions(i * ts, ts + h)
        row = lax.broadcasted_iota(jnp.int32, (ts + h, 1), 0)
        live = jnp.logical_or(i < n_tiles - 1, row < ts)

        def section(ref, k, cb):
            return ref[:, k * D + cb * LANE:k * D + (cb + 1) * LANE].astype(F32)

        for cb in range(LANE_BLOCKS):
            xa = section(z_ref, 0, cb)
            sums = shifts.window_sums(
                jnp.concatenate([jnp.where(inside, section(zp_ref, 0, cb), 0.0), xa], axis=0), cb // 2 + 1, True)
            pooled_ref[:, cb * LANE:(cb + 1) * LANE] = (
                sums[h:] * _pool_inv(positions, cb)[:ts] - xa).astype(MXU_DTYPE)
        for g in range(4):
            cols = slice(g * HEAD, (g + 1) * HEAD)
            p_ref[:, cols] = _dot(pooled_ref[:, cols], pw_ref[g], NN)
        rc = CONV_ROWS

        def block(ref, k, cb, r0, n):
            return ref[r0:r0 + n, k * D + cb * LANE:k * D + (cb + 1) * LANE].astype(F32)

        def fold(v):
            part = v[0:SUBLANES]
            for j in range(1, v.shape[0] // SUBLANES):
                part = part + v[j * SUBLANES:(j + 1) * SUBLANES]
            return part

        def added(total, v):
            return fold(v) if total is None else total + fold(v)

        for cb in range(LANE_BLOCKS):
            cols = slice(cb * LANE, (cb + 1) * LANE)
            scale = ps_ref[:, cols]
            total = None
            for r0 in range(0, ts, rc):
                silu_ga, dsilu_ga = _silu_and_grad(block(z_ref, 1, cb, r0, rc))
                d_ya = block(dy_ref, 0, cb, r0, rc)
                dp_ref[r0:r0 + rc, cols] = (d_ya * scale * silu_ga).astype(MXU_DTYPE)
                d_ya_p = d_ya * p_ref[r0:r0 + rc, cols]
                dz_ref[r0:r0 + rc, D + cb * LANE:D + (cb + 1) * LANE] = (d_ya_p * scale * dsilu_ga).astype(BF16)
                total = added(total, d_ya_p * silu_ga)
            silu_ga, _ = _silu_and_grad(block(zn_ref, 1, cb, 0, h))
            dp_ref[ts:ts + h, cols] = (block(dyn_ref, 0, cb, 0, h) * scale * silu_ga).astype(MXU_DTYPE)
            dps_ref[:, cols] += _colsum(total)
        for g in range(4):
            cols = slice(g * HEAD, (g + 1) * HEAD)
            dpooled_ref[:, cols] = _dot(dp_ref[:, cols], pwt_ref[g], NN)
            dpw_ref[g] += _dot(pooled_ref[:, cols], dp_ref[0:ts, cols], TN)
        for cb in range(LANE_BLOCKS):
            d_pooled = jnp.where(live, dpooled_ref[:, cb * LANE:(cb + 1) * LANE], 0.0)
            sums = shifts.window_sums(d_pooled * _pool_inv(positions, cb), cb // 2 + 1, False)
            dz_ref[:, cb * LANE:(cb + 1) * LANE] = (sums[:ts] - d_pooled[:ts]).astype(BF16)

        more = i < n_tiles - 1
        q0 = SUBLANES + h
        for cb in range(LANE_BLOCKS):
            cols = slice(cb * LANE, (cb + 1) * LANE)
            w0, w1, w2 = cw_ref[0:1, cols], cw_ref[1:2, cols], cw_ref[2:3, cols]
            bias = cb_ref[:, cols]
            shift_ref[SUBLANES:q0, :] = jnp.where(inside, block(zp_ref, 4, cb, 0, h) * block(zp_ref, 2, cb, 0, h), 0.0)
            for r0 in range(0, ts, rc):
                shift_ref[q0 + r0:q0 + r0 + rc, :] = block(z_ref, 4, cb, r0, rc) * block(z_ref, 2, cb, r0, rc)
                silu_gb, _ = _silu_and_grad(block(z_ref, 5, cb, r0, rc))
                shift2_ref[SUBLANES + r0:SUBLANES + r0 + rc, :] = (
                    block(dy_ref, 1, cb, r0, rc) * block(z_ref, 3, cb, r0, rc) * silu_gb)
            silu_gb, _ = _silu_and_grad(block(zn_ref, 5, cb, 0, h))
            shift2_ref[SUBLANES + ts:SUBLANES + ts + h, :] = jnp.where(
                more, block(dyn_ref, 1, cb, 0, h) * block(zn_ref, 3, cb, 0, h) * silu_gb, 0.0)

            totals = [None] * 4
            for r0 in range(0, ts, rc):
                q_main = shift_ref[q0 + r0:q0 + r0 + rc, :]
                q_1 = shift_ref[q0 + r0 - 1:q0 + r0 - 1 + rc, :]
                q_2 = shift_ref[q0 + r0 - 2:q0 + r0 - 2 + rc, :]
                cv = w2 * q_main + w1 * q_1 + w0 * q_2 + bias
                d0 = SUBLANES + r0
                d_cv0 = shift2_ref[d0:d0 + rc, :]
                d_q = w2 * d_cv0 + w1 * shift2_ref[d0 + 1:d0 + 1 + rc, :] + w0 * shift2_ref[d0 + 2:d0 + 2 + rc, :]
                silu_gb, dsilu_gb = _silu_and_grad(block(z_ref, 5, cb, r0, rc))
                d_yb_cv = block(dy_ref, 1, cb, r0, rc) * cv
                for k, val in ((2, d_q * block(z_ref, 4, cb, r0, rc)), (3, d_yb_cv * silu_gb),
                               (4, d_q * block(z_ref, 2, cb, r0, rc)),
                               (5, d_yb_cv * block(z_ref, 3, cb, r0, rc) * dsilu_gb)):
                    dz_ref[r0:r0 + rc, k * D + cb * LANE:k * D + (cb + 1) * LANE] = val.astype(BF16)
                for j, val in enumerate((d_cv0, d_cv0 * q_main, d_cv0 * q_1, d_cv0 * q_2)):
                    totals[j] = added(totals[j], val)
            dcb_ref[:, cols] += _colsum(totals[0])
            dcw_ref[2:3, cols] += _colsum(totals[1])
            dcw_ref[1:2, cols] += _colsum(totals[2])
            dcw_ref[0:1, cols] += _colsum(totals[3])

    fixed2 = lambda i: (0, 0)
    fixed3 = lambda i: (0, 0, 0)
    return pl.pallas_call(
        body, name=f"even_bwd_{l}", grid=(n_tiles,),
        in_specs=[pl.BlockSpec((ts, DZ), lambda i: (i, 0)), pl.BlockSpec((h, DZ), _prev_index(ts, h)),
                  pl.BlockSpec((h, DZ), _next_index(ts, h, s)),
                  pl.BlockSpec((ts, D), lambda i: (i, 0)), pl.BlockSpec((h, D), _next_index(ts, h, s)),
                  pl.BlockSpec((NQ, 2, None, RQ // 2, D), lambda i: (0, 0, k, 0, 0)),
                  pl.BlockSpec((4, HEAD, HEAD), fixed3), pl.BlockSpec((4, HEAD, HEAD), fixed3),
                  pl.BlockSpec((1, D), fixed2), pl.BlockSpec((SHORT_K, D), fixed2), pl.BlockSpec((1, D), fixed2)],
        out_specs=[pl.BlockSpec((ts, DZ), lambda i: (i, 0)), pl.BlockSpec((4, HEAD, HEAD), fixed3),
                   pl.BlockSpec((1, D), fixed2), pl.BlockSpec((SHORT_K, D), fixed2), pl.BlockSpec((1, D), fixed2)],
        out_shape=[jax.ShapeDtypeStruct((s, DZ), BF16), jax.ShapeDtypeStruct((4, HEAD, HEAD), F32),
                   jax.ShapeDtypeStruct((1, D), F32), jax.ShapeDtypeStruct((SHORT_K, D), F32),
                   jax.ShapeDtypeStruct((1, D), F32)],
        scratch_shapes=[pltpu.VMEM((ts + h + 2 * SUBLANES, LANE), F32), pltpu.VMEM((ts, D), MXU_DTYPE),
                        pltpu.VMEM((ts, D), F32), pltpu.VMEM((ts + h, D), MXU_DTYPE), pltpu.VMEM((ts + h, D), F32),
                        pltpu.VMEM((ts + h + 2 * SUBLANES, LANE), F32), pltpu.VMEM((ts + h, DY), BF16)],
        compiler_params=_params(("arbitrary",), 56),
    )(z, z, z, drb, drb, wout_g, pool_w, pool_wt, pool_scale, sconv_w, sconv_b)


def _to_blocks(ref, r0, val):
    n = val.shape[0]
    for cb in range(LANE_BLOCKS):
        ref[cb, r0:r0 + n, :] = val[:, cb * LANE:(cb + 1) * LANE]


def _from_blocks(ref):
    return jnp.concatenate([ref[cb] for cb in range(LANE_BLOCKS)], axis=1)


def _shift_copies(src_ref, sh_ref, n, causal):
    def block(cb, carry):
        for b in range(1, SUBLANES):
            if causal:
                sh_ref[cb, b - 1, SUBLANES:n, :] = src_ref[cb, SUBLANES - b:n - b, :]
            else:
                sh_ref[cb, b - 1, 0:n - SUBLANES, :] = src_ref[cb, b:n - SUBLANES + b, :]
        return carry

    lax.fori_loop(0, LANE_BLOCKS, block, 0)


def _tap(src_ref, sh_ref, cb, first, n, d, causal):
    whole, b = (d // SUBLANES) * SUBLANES, d % SUBLANES
    start = first - whole if causal else first + whole
    if b == 0:
        return src_ref[cb, start:start + n, :]
    return sh_ref[cb, b - 1, start:start + n, :]


def _chunk_rows(rows, most):
    return max(n for n in range(CONV_ROWS, most + 1, CONV_ROWS) if rows % n == 0)


def _conv31(src_ref, sh_ref, w_ref, dst_ref, base, rows, causal):
    n = _chunk_rows(rows, 4 * CONV_ROWS)

    def block(cb, carry):
        for r0 in range(0, rows, n):
            acc = None
            for d in range(CONV_K):
                term = w_ref[cb, CONV_K - 1 - d:CONV_K - d, :] * _tap(src_ref, sh_ref, cb, base + r0, n, d, causal)
                acc = term if acc is None else acc + term
            dst_ref[cb, r0:r0 + n, :] = acc
        return carry

    lax.fori_loop(0, LANE_BLOCKS, block, 0)


def _conv31_tap_grads(d_ref, src_ref, sh_ref, dw_ref, base, rows):
    n = _chunk_rows(rows, 2 * CONV_ROWS)

    def block(cb, carry):
        sums = [None] * CONV_K
        for r0 in range(0, rows, n):
            d_blk = d_ref[cb, r0:r0 + n, :]
            for d in range(CONV_K):
                prod = d_blk * _tap(src_ref, sh_ref, cb, base + r0, n, d, True)
                part = prod[0:SUBLANES]
                for k in range(1, n // SUBLANES):
                    part = part + prod[k * SUBLANES:(k + 1) * SUBLANES]
                sums[d] = part if sums[d] is None else sums[d] + part
        for d in range(CONV_K):
            j = CONV_K - 1 - d
            dw_ref[cb, j:j + 1, :] += _colsum(sums[d])
        return carry

    lax.fori_loop(0, LANE_BLOCKS, block, 0)


def _sgu_bias_rows(sgu_b):
    return jnp.repeat(jnp.transpose(sgu_b), HEAD, axis=1)


def _odd_fwd(z, sln_g, sln_b, ws, sbias, dconv_w, dconv_b, dn_g, dn_b, l):
    s = z.shape[0]
    ts = min(s, 256)
    h = CONV_HALO

    def body(z_ref, zp_ref, lg_ref, lb_ref, ws_ref, sb_ref, cw_ref, cb_ref, ng_ref, nb_ref, o_ref, zh_ref, rz_ref,
             zz_ref, zc_ref, sh_ref):
        i = pl.program_id(0)
        vhat, _ = _ln_stats(_section(z_ref, 1))
        vn = (vhat * lg_ref[...] + lb_ref[...]).astype(MXU_DTYPE)
        silu_gc, _ = _silu_and_grad(_section(z_ref, 2))
        for n in range(ts // SGU_BLOCK):
            rows = slice(n * SGU_BLOCK, (n + 1) * SGU_BLOCK)
            sv = jnp.concatenate(
                [_dot(ws_ref[hd], vn[rows, hd * HEAD:(hd + 1) * HEAD], NN) for hd in range(4)], axis=1)
            sv = sv + sb_ref[...]
            o_ref[rows, 0:D] = (z_ref[rows, 0:D].astype(F32) * sv * silu_gc[rows]).astype(BF16)

        _to_blocks(zz_ref, 0, jnp.where(i > 0, _section(zp_ref, 3) * _sigmoid(_section(zp_ref, 4)), 0.0))
        _to_blocks(zz_ref, h, _section(z_ref, 3) * _sigmoid(_section(z_ref, 4)))
        _shift_copies(zz_ref, sh_ref, h + ts, True)
        _conv31(zz_ref, sh_ref, cw_ref, zc_ref, h, ts, True)
        zhat, rstd_z = _ln_stats(_from_blocks(zc_ref) + cb_ref[...])
        zh_ref[...] = zhat
        rz_ref[...] = rstd_z
        silu_zn, _ = _silu_and_grad(zhat * ng_ref[...] + nb_ref[...])
        silu_gd, _ = _silu_and_grad(_section(z_ref, 5))
        o_ref[:, D:2 * D] = (silu_zn * silu_gd).astype(BF16)

    fixed2 = lambda i: (0, 0)
    vec = pl.BlockSpec((1, D), fixed2)
    return pl.pallas_call(
        body, name=f"odd_fwd_{l}", grid=(s // ts,),
        in_specs=[pl.BlockSpec((ts, DZ), lambda i: (i, 0)), pl.BlockSpec((h, DZ), _prev_index(ts, h)),
                  vec, vec, pl.BlockSpec((4, SGU_BLOCK, SGU_BLOCK), lambda i: (0, 0, 0)),
                  pl.BlockSpec((SGU_BLOCK, D), fixed2), pl.BlockSpec((LANE_BLOCKS, CONV_K, LANE), lambda i: (0, 0, 0)),
                  vec, vec, vec],
        out_specs=[pl.BlockSpec((ts, DY), lambda i: (i, 0)), pl.BlockSpec((ts, D), lambda i: (i, 0)),
                   pl.BlockSpec((ts, 1), lambda i: (i, 0))],
        out_shape=[jax.ShapeDtypeStruct((s, DY), BF16), jax.ShapeDtypeStruct((s, D), F32),
                   jax.ShapeDtypeStruct((s, 1), F32)],
        scratch_shapes=[pltpu.VMEM((LANE_BLOCKS, h + ts, LANE), F32), pltpu.VMEM((LANE_BLOCKS, ts, LANE), F32),
                        pltpu.VMEM((LANE_BLOCKS, SUBLANES - 1, h + ts, LANE), F32)],
        compiler_params=_params(("arbitrary",)),
    )(z, z, sln_g, sln_b, ws, sbias, dconv_w, dconv_b, dn_g, dn_b)


def _odd_bwd(z, drb, wout_g, k, zhat_s, rstd_s, sln_g, sln_b, ws, wst, sbias, dconv_w, dn_g, dn_b, l):
    s = z.shape[0]
    ts = min(s, 256)
    h = CONV_HALO
    n_tiles = s // ts
    te = ts + h

    def body(z_ref, zp_ref, zn_ref, drb_ref, drbn_ref, wo_ref, zh_ref, zhn_ref, rz_ref, rzn_ref, lg_ref, lb_ref,
             ws_ref, wst_ref, sb_ref, cw_ref, ng_ref, nb_ref, dz_ref, dlg_ref, dlb_ref, dws_ref, dsb_ref, dcw_ref,
             dcb_ref, dng_ref, dnb_ref, zz_ref, dzc_ref, dzz_ref, dsb_acc, sh_ref, dy_s):
        i = pl.program_id(0)
        more = i < n_tiles - 1
        dy_ref, dyn_ref = _dycat_into(dy_s, drb_ref, drbn_ref, wo_ref, ts, h)

        @pl.when(i == 0)
        def _():
            for ref in (dlg_ref, dlb_ref, dws_ref, dsb_ref, dcw_ref, dcb_ref, dng_ref, dnb_ref, dsb_acc):
                ref[...] = jnp.zeros_like(ref)

        vhat, rstd_v = _ln_stats(_section(z_ref, 1))
        lg = lg_ref[...]
        vn = (vhat * lg + lb_ref[...]).astype(MXU_DTYPE)
        u = _section(z_ref, 0)
        silu_gc, dsilu_gc = _silu_and_grad(_section(z_ref, 2))
        d_yc = _section(dy_ref, 0)
        d_yc_u = d_yc * u
        d_sv = d_yc_u * silu_gc
        d_svb = d_sv.astype(MXU_DTYPE)
        sv_rows = []
        dvn_rows = []
        dsb = None
        for n in range(ts // SGU_BLOCK):
            rows = slice(n * SGU_BLOCK, (n + 1) * SGU_BLOCK)
            sv_parts = []
            dvn_parts = []
            for hd in range(4):
                cols = slice(hd * HEAD, (hd + 1) * HEAD)
                sv_parts.append(_dot(ws_ref[hd], vn[rows, cols], NN))
                dvn_parts.append(_dot(wst_ref[hd], d_svb[rows, cols], NN))
                dws_ref[hd] += _dot(d_svb[rows, cols], vn[rows, cols], NT)
            sv_rows.append(jnp.concatenate(sv_parts, axis=1) + sb_ref[...])
            dvn_rows.append(jnp.concatenate(dvn_parts, axis=1))
            dsb = d_sv[rows] if dsb is None else dsb + d_sv[rows]
        dsb_acc[...] += dsb

        @pl.when(i == n_tiles - 1)
        def _():
            for hd in range(4):
                blk = dsb_acc[:, hd * HEAD:(hd + 1) * HEAD]
                folded = blk[:, 0:LANE] + blk[:, LANE:HEAD]
                dsb_ref[hd:hd + 1, :] = _colsum(jnp.transpose(folded))
        sv = jnp.concatenate(sv_rows, axis=0)
        d_vn = jnp.concatenate(dvn_rows, axis=0)
        dz_ref[:, 0:D] = (d_yc * sv * silu_gc).astype(BF16)
        dz_ref[:, D:2 * D] = _ln_bwd_rows(d_vn * lg, vhat, rstd_v).astype(BF16)
        dz_ref[:, 2 * D:3 * D] = (d_yc_u * sv * dsilu_gc).astype(BF16)
        dlg_ref[...] += _colsum(d_vn * vhat)
        dlb_ref[...] += _colsum(d_vn)

        def gate(ref):
            return _section(ref, 3) * _sigmoid(_section(ref, 4))

        _to_blocks(zz_ref, 0, jnp.where(i > 0, gate(zp_ref), 0.0))
        _to_blocks(zz_ref, h, gate(z_ref))
        _shift_copies(zz_ref, sh_ref, h + ts, True)
        zhat = jnp.concatenate([zh_ref[...], zhn_ref[...]], axis=0)
        rstd_z = jnp.concatenate([rz_ref[...], rzn_ref[...]], axis=0)
        ng = ng_ref[...]
        silu_zn, dsilu_zn = _silu_and_grad(zhat * ng + nb_ref[...])
        gd = jnp.concatenate([_section(z_ref, 5), _section(zn_ref, 5)], axis=0)
        silu_gd, dsilu_gd = _silu_and_grad(gd)
        d_yd = jnp.concatenate([_section(dy_ref, 1), _section(dyn_ref, 1)], axis=0)
        d_zn = d_yd * silu_gd * dsilu_zn
        d_zc = _ln_bwd_rows(d_zn * ng, zhat, rstd_z)
        row = lax.broadcasted_iota(jnp.int32, (te, 1), 0)
        d_zc = jnp.where(jnp.logical_or(more, row < ts), d_zc, 0.0)
        _to_blocks(dzc_ref, 0, d_zc)
        dz_ref[:, 5 * D:6 * D] = (d_yd[:ts] * silu_zn[:ts] * dsilu_gd[:ts]).astype(BF16)
        dng_ref[...] += _colsum(d_zn[:ts] * zhat[:ts])
        dnb_ref[...] += _colsum(d_zn[:ts])
        dcb_ref[...] += _colsum(d_zc[:ts])
        _conv31_tap_grads(dzc_ref, zz_ref, sh_ref, dcw_ref, h, ts)
        _shift_copies(dzc_ref, sh_ref, te, False)
        _conv31(dzc_ref, sh_ref, cw_ref, dzz_ref, 0, ts, False)
        d_zz = _from_blocks(dzz_ref)
        a = _section(z_ref, 3)
        sig_b = _sigmoid(_section(z_ref, 4))
        dz_ref[:, 3 * D:4 * D] = (d_zz * sig_b).astype(BF16)
        dz_ref[:, 4 * D:5 * D] = (d_zz * a * sig_b * (1.0 - sig_b)).astype(BF16)

    fixed2 = lambda i: (0, 0)
    fixed3 = lambda i: (0, 0, 0)
    vec = pl.BlockSpec((1, D), fixed2)
    mat = pl.BlockSpec((4, SGU_BLOCK, SGU_BLOCK), fixed3)
    vec_shape = jax.ShapeDtypeStruct((1, D), F32)
    return pl.pallas_call(
        body, name=f"odd_bwd_{l}", grid=(n_tiles,),
        in_specs=[pl.BlockSpec((ts, DZ), lambda i: (i, 0)), pl.BlockSpec((h, DZ), _prev_index(ts, h)),
                  pl.BlockSpec((h, DZ), _next_index(ts, h, s)),
                  pl.BlockSpec((ts, D), lambda i: (i, 0)), pl.BlockSpec((h, D), _next_index(ts, h, s)),
                  pl.BlockSpec((NQ, 2, None, RQ // 2, D), lambda i: (0, 0, k, 0, 0)),
                  pl.BlockSpec((ts, D), lambda i: (i, 0)), pl.BlockSpec((h, D), _next_index(ts, h, s)),
                  pl.BlockSpec((ts, 1), lambda i: (i, 0)), pl.BlockSpec((h, 1), _next_index(ts, h, s)),
                  vec, vec, mat, mat, pl.BlockSpec((SGU_BLOCK, D), fixed2),
                  pl.BlockSpec((LANE_BLOCKS, CONV_K, LANE), fixed3), vec, vec],
        out_specs=[pl.BlockSpec((ts, DZ), lambda i: (i, 0)), vec, vec, mat, pl.BlockSpec((4, SGU_BLOCK), fixed2),
                   pl.BlockSpec((LANE_BLOCKS, CONV_K, LANE), fixed3), vec, vec, vec],
        out_shape=[jax.ShapeDtypeStruct((s, DZ), BF16), vec_shape, vec_shape,
                   jax.ShapeDtypeStruct((4, SGU_BLOCK, SGU_BLOCK), F32), jax.ShapeDtypeStruct((4, SGU_BLOCK), F32),
                   jax.ShapeDtypeStruct((LANE_BLOCKS, CONV_K, LANE), F32), vec_shape, vec_shape, vec_shape],
        scratch_shapes=[pltpu.VMEM((LANE_BLOCKS, h + ts, LANE), F32), pltpu.VMEM((LANE_BLOCKS, te, LANE), F32),
                        pltpu.VMEM((LANE_BLOCKS, ts, LANE), F32), pltpu.VMEM((SGU_BLOCK, D), F32),
                        pltpu.VMEM((LANE_BLOCKS, SUBLANES - 1, te, LANE), F32), pltpu.VMEM((te, DY), BF16)],
        compiler_params=_params(("arbitrary",), 60),
    )(z, z, z, drb, drb, wout_g, zhat_s, zhat_s, rstd_s, rstd_s, sln_g, sln_b, ws, wst, sbias, dconv_w, dn_g, dn_b)


def _remote(src, dst, send_sems, recv_sems, k, to):
    return pltpu.make_async_remote_copy(src_ref=src, dst_ref=dst, send_sem=send_sems.at[k],
                                        recv_sem=recv_sems.at[k], device_id=to, device_id_type=MESH)


def _other_chips(x, y):
    return [(1 - x, y, 2 * (1 - x) + y), (x, 1 - y, 2 * x + 1 - y), (1 - x, 1 - y, 2 * (1 - x) + 1 - y)]


def _cast_own(w_stack, pos_arr, name):
    slots, rows, cols = w_stack.shape
    half = rows // 2

    def body(pos_ref, w_ref, o_ref):
        o_ref[...] = w_ref[...].astype(BF16)

    grid_spec = pltpu.PrefetchScalarGridSpec(
        num_scalar_prefetch=1, grid=(slots, 2),
        in_specs=[pl.BlockSpec((None, half, cols), lambda s, h, pos: (s, h, 0))],
        out_specs=pl.BlockSpec((None, None, None, half, cols), lambda s, h, pos: (pos[1], h, s, 0, 0)))
    return pl.pallas_call(
        body, name=name, grid_spec=grid_spec, out_shape=jax.ShapeDtypeStruct((NQ, 2, slots, half, cols), BF16),
        compiler_params=_params(("arbitrary",) * 2),
    )(pos_arr, w_stack)


def _gather_weights(win_g, wout_g, small_sh):
    def body(win_in, wout_in, small, win_g, wout_g, small_g, send_sems, recv_sems):
        del win_in, wout_in
        x, y, c = _mesh_pos()
        me = 2 * x + y
        sibling = (x, y, 1 - c)
        chips = _other_chips(x, y)

        sends = []
        for j, (cx, cy, _) in enumerate(chips):
            to = (cx, cy, c)
            sends.append(_remote(win_g.at[me, c], win_g.at[me, c], send_sems, recv_sems, j, to))
            sends.append(_remote(wout_g.at[me, c], wout_g.at[me, c], send_sems, recv_sems, 3 + j, to))
            sends.append(_remote(small, small_g.at[me], send_sems, recv_sems, 6 + j, to))
        for cp in sends:
            cp.start()
        passed = []
        for j, (_, _, q) in enumerate(chips):
            got_in = win_g.at[q, c]
            got_out = wout_g.at[q, c]
            _remote(got_in, got_in, send_sems, recv_sems, j, sibling).wait_recv()
            cp = _remote(got_in, got_in, send_sems, recv_sems, 9 + j, sibling)
            cp.start()
            passed.append(cp)
            _remote(got_out, got_out, send_sems, recv_sems, 3 + j, sibling).wait_recv()
            cp = _remote(got_out, got_out, send_sems, recv_sems, 12 + j, sibling)
            cp.start()
            passed.append(cp)
            _remote(small, small_g.at[q], send_sems, recv_sems, 6 + j, sibling).wait_recv()
        for j, (_, _, q) in enumerate(chips):
            from_in = win_g.at[q, 1 - c]
            from_out = wout_g.at[q, 1 - c]
            _remote(from_in, from_in, send_sems, recv_sems, 9 + j, sibling).wait_recv()
            _remote(from_out, from_out, send_sems, recv_sems, 12 + j, sibling).wait_recv()
        for cp in sends + passed:
            cp.wait_send()

    return pl.pallas_call(
        body, name="gather_weights",
        in_specs=[HBM_SPEC, HBM_SPEC, HBM_SPEC], out_specs=[HBM_SPEC, HBM_SPEC, HBM_SPEC],
        out_shape=[jax.ShapeDtypeStruct(win_g.shape, win_g.dtype), jax.ShapeDtypeStruct(wout_g.shape, wout_g.dtype),
                   jax.ShapeDtypeStruct((NQ,) + small_sh.shape, small_sh.dtype)],
        input_output_aliases={0: 0, 1: 1},
        scratch_shapes=[pltpu.SemaphoreType.DMA((15,)), pltpu.SemaphoreType.DMA((15,))],
    )(win_g, wout_g, small_sh)


def _hbm(a):
    return pltpu.with_memory_space_constraint(a, pltpu.HBM)


def _split_start(body, name, sources, landings):
    n_src, n_land = len(sources), len(landings)
    n_buf = n_src + n_land

    def kernel_body(*refs):
        ins, outs = refs[:n_buf], refs[n_buf:]
        send_sems, recv_sems, token = outs[0], outs[1], outs[2 + n_buf]
        body(ins[:n_src], ins[n_src:], send_sems, recv_sems)
        token[...] = jnp.zeros_like(token)

    bufs = [_hbm(a) for a in sources] + [
        _hbm(lax.empty(s.shape, s.dtype) if isinstance(s, jax.ShapeDtypeStruct) else s) for s in landings]
    n_sem = getattr(body, "n_copies")
    out = pl.pallas_call(
        kernel_body, name=name,
        out_shape=(pltpu.SemaphoreType.DMA((n_sem,)), pltpu.SemaphoreType.DMA((n_sem,)),
                   *[pltpu.HBM(b.shape, b.dtype) for b in bufs], jax.ShapeDtypeStruct((8, LANE), F32)),
        in_specs=(HBM_SPEC,) * n_buf,
        out_specs=(SEM_SPEC, SEM_SPEC, *([HBM_SPEC] * n_buf), pl.BlockSpec(memory_space=pltpu.VMEM)),
        input_output_aliases={k: 2 + k for k in range(n_buf)},
        compiler_params=pltpu.CompilerParams(has_side_effects=SIDE_EFFECT),
    )(*bufs)
    return out[0], out[1], list(out[2:2 + n_src]), list(out[2 + n_src:2 + n_buf]), out[2 + n_buf]


def _split_wait(body, name, send_sems, recv_sems, sources, landings, after):
    n_src, n_land = len(sources), len(landings)
    n_buf = n_src + n_land

    def kernel_body(*refs):
        ins = refs[:n_buf]
        body(ins[:n_src], ins[n_src:], refs[n_buf], refs[n_buf + 1])

    bufs = list(sources) + list(landings)
    out = pl.pallas_call(
        kernel_body, name=name,
        out_shape=tuple(pltpu.HBM(b.shape, b.dtype) for b in bufs),
        in_specs=(*([HBM_SPEC] * n_buf), SEM_SPEC, SEM_SPEC, pl.BlockSpec(memory_space=pl.ANY)),
        out_specs=(HBM_SPEC,) * n_buf,
        input_output_aliases={k: k for k in range(n_buf)},
        compiler_params=pltpu.CompilerParams(has_side_effects=SIDE_EFFECT),
    )(*bufs, send_sems, recv_sems, after)
    return list(out[:n_src]), list(out[n_src:])


def _gather_rest_copies(start):
    def body(srcs, lands, send_sems, recv_sems):
        del srcs
        x, y, c = _mesh_pos()
        me = 2 * x + y
        for j, (cx, cy, q) in enumerate(_other_chips(x, y)):
            to = (cx, cy, c)
            for k, gathered in enumerate(lands):
                if start:
                    _remote(gathered.at[me, c], gathered.at[me, c], send_sems, recv_sems, 3 * k + j, to).start()
                else:
                    cp = _remote(gathered.at[me, c], gathered.at[q, c], send_sems, recv_sems, 3 * k + j, to)
                    cp.wait_send()
                    cp.wait_recv()

    body.n_copies = 6
    return body


def _gather_rest_forward(win_g, wout_g):
    def body(win_in, wout_in, win_g, wout_g, send_sems, recv_sems):
        del win_in, wout_in
        x, y, c = _mesh_pos()
        sibling = (x, y, 1 - c)
        passed = []
        for j, (_, _, q) in enumerate(_other_chips(x, y)):
            got_in = win_g.at[q, c]
            got_out = wout_g.at[q, c]
            passed.append(_remote(got_in, got_in, send_sems, recv_sems, j, sibling))
            passed.append(_remote(got_out, got_out, send_sems, recv_sems, 3 + j, sibling))
        for cp in passed:
            cp.start()
        for j, (_, _, q) in enumerate(_other_chips(x, y)):
            from_in = win_g.at[q, 1 - c]
            from_out = wout_g.at[q, 1 - c]
            _remote(from_in, from_in, send_sems, recv_sems, j, sibling).wait_recv()
            _remote(from_out, from_out, send_sems, recv_sems, 3 + j, sibling).wait_recv()
        for cp in passed:
            cp.wait_send()

    return pl.pallas_call(
        body, name="gather_rest_forward",
        in_specs=[HBM_SPEC] * 2, out_specs=[HBM_SPEC] * 2,
        out_shape=[jax.ShapeDtypeStruct(win_g.shape, win_g.dtype), jax.ShapeDtypeStruct(wout_g.shape, wout_g.dtype)],
        input_output_aliases={0: 0, 1: 1},
        scratch_shapes=[pltpu.SemaphoreType.DMA((6,)), pltpu.SemaphoreType.DMA((6,))],
    )(win_g, wout_g)


def _allreduce_small(groups, after):
    pieces = [p for _, _, _, members in groups for _, p in members]
    n_in, n_g = len(pieces), len(groups)

    def body(*refs):
        ins = refs[:n_in]
        outs = refs[-(2 * n_g + 2):-(n_g + 2)]
        alls = refs[-(n_g + 2):-2]
        send_sems, recv_sems = refs[-2:]
        x, y, c = _mesh_pos()
        me = 4 * x + 2 * y + c
        sibling = (x, y, 1 - c)
        chips = _other_chips(x, y)

        k = 0
        for (rows, cols, dtype, members), all_ref in zip(groups, alls):
            if any(piece.shape[1] < cols for _, piece in members) or sum(p.shape[0] for _, p in members) < rows:
                all_ref[me] = jnp.zeros((rows, cols), dtype)
            for first, piece in members:
                n, width = piece.shape
                all_ref[me, first:first + n, 0:width] = ins[k][...].astype(dtype)
                k += 1

        sends, passed = [], []
        for g, all_ref in enumerate(alls):
            sends.append(_remote(all_ref.at[me], all_ref.at[me], send_sems, recv_sems, 7 * g, sibling))
            for j, (cx, cy, _) in enumerate(chips):
                sends.append(_remote(all_ref.at[me], all_ref.at[me], send_sems, recv_sems, 7 * g + 1 + j, (cx, cy, c)))
        for cp in sends:
            cp.start()
        for j, (cx, cy, _) in enumerate(chips):
            for g, all_ref in enumerate(alls):
                got = all_ref.at[4 * cx + 2 * cy + c]
                _remote(got, got, send_sems, recv_sems, 7 * g + 1 + j, sibling).wait_recv()
                cp = _remote(got, got, send_sems, recv_sems, 7 * g + 4 + j, sibling)
                cp.start()
                passed.append(cp)
        for g, all_ref in enumerate(alls):
            got = all_ref.at[4 * x + 2 * y + 1 - c]
            _remote(got, got, send_sems, recv_sems, 7 * g, sibling).wait_recv()
            for j, (cx, cy, _) in enumerate(chips):
                got = all_ref.at[4 * cx + 2 * cy + 1 - c]
                _remote(got, got, send_sems, recv_sems, 7 * g + 4 + j, sibling).wait_recv()
        for cp in sends + passed:
            cp.wait_send()
        for o_ref, all_ref in zip(outs, alls):
            total = all_ref[0].astype(F32)
            for dev in range(1, 8):
                total = total + all_ref[dev].astype(F32)
            o_ref[...] = total

    vmem = pl.BlockSpec(memory_space=pltpu.VMEM)
    return pl.pallas_call(
        body, name="allreduce_small",
        in_specs=[vmem] * n_in + _after_spec(after),
        out_specs=[vmem] * n_g,
        out_shape=[jax.ShapeDtypeStruct((rows, cols), F32) for rows, cols, _, _ in groups],
        scratch_shapes=[pltpu.VMEM((8, rows, cols), dtype) for rows, cols, dtype, _ in groups]
        + [pltpu.SemaphoreType.DMA((7 * n_g,)), pltpu.SemaphoreType.DMA((7 * n_g,))],
        compiler_params=_params(None, 56),
    )(*pieces, *_after_args(after))


def _pair_copies(start):
    def body(srcs, lands, send_sems, recv_sems):
        x, y, c = _mesh_pos()
        sibling = (x, y, 1 - c)
        for k, (g_ref, r_ref) in enumerate(zip(srcs, lands)):
            half = g_ref.shape[1] // 2
            cp = _remote(g_ref.at[:, pl.ds((1 - c) * half, half), :], r_ref, send_sems, recv_sems, k, sibling)
            if start:
                cp.start()
            else:
                cp.wait_send()
                cp.wait_recv()

    body.n_copies = 2
    return body


def _pair_sum(g, r, pos_arr, name):
    nq, rows, cols = r.shape
    tr = min(rows, 256)
    nt = rows // tr

    def body(pos_ref, g_ref, r_ref, ob_ref, own_ref):
        total = g_ref[...] + r_ref[...]
        ob_ref[...] = total.astype(BF16)

        @pl.when(pl.program_id(1) == pos_ref[1])
        def _():
            own_ref[...] = total

    blk = (None, tr, cols)
    grid_spec = pltpu.PrefetchScalarGridSpec(
        num_scalar_prefetch=1, grid=(nt, nq),
        in_specs=[pl.BlockSpec(blk, lambda t, q, pos: (q, pos[0] * nt + t, 0)),
                  pl.BlockSpec(blk, lambda t, q, pos: (q, t, 0))],
        out_specs=[pl.BlockSpec(blk, lambda t, q, pos: (q, t, 0)),
                   pl.BlockSpec((tr, cols), lambda t, q, pos: (t, 0))])
    return pl.pallas_call(
        body, name=name, grid_spec=grid_spec,
        out_shape=[jax.ShapeDtypeStruct(r.shape, BF16), jax.ShapeDtypeStruct((rows, cols), F32)],
        compiler_params=_params(("arbitrary",) * 2),
    )(pos_arr, g, r)


def _chip_copies(start):
    def body(srcs, lands, send_sems, recv_sems):
        pin, pout = srcs
        rin, rout = lands
        x, y, c = _mesh_pos()
        for j, (cx, cy, q) in enumerate(_other_chips(x, y)):
            to = (cx, cy, c)
            for k, (src, dst) in enumerate(((pin, rin), (pout, rout))):
                cp = _remote(src.at[q], dst.at[j], send_sems, recv_sems, 3 * k + j, to)
                if start:
                    cp.start()
                else:
                    cp.wait_send()
                    cp.wait_recv()

    body.n_copies = 6
    return body


def _chip_sum(own, r, pos_arr, name):
    rows, cols = own.shape
    tr = min(rows, 256)

    def body(pos_ref, p_ref, r0_ref, r1_ref, r2_ref, o_ref):
        o_ref[...] = ((p_ref[...] + r0_ref[...].astype(F32)) + r1_ref[...].astype(F32)) + r2_ref[...].astype(F32)

    def peer(j):
        return pl.BlockSpec((None, tr, cols), lambda t, pos: (j, t, 0))

    grid_spec = pltpu.PrefetchScalarGridSpec(
        num_scalar_prefetch=1, grid=(rows // tr,),
        in_specs=[pl.BlockSpec((tr, cols), lambda t, pos: (t, 0)), peer(0), peer(1), peer(2)],
        out_specs=pl.BlockSpec((None, tr, cols), lambda t, pos: (pos[0], t, 0)))
    return pl.pallas_call(
        body, name=name, grid_spec=grid_spec, out_shape=jax.ShapeDtypeStruct((2, rows, cols), F32),
        compiler_params=_params(("arbitrary",)),
    )(pos_arr, own, r, r, r)


def _pair_share(gin, gout, l):
    def body(gin_in, gout_in, gin_ref, gout_ref, send_sems, recv_sems):
        del gin_in, gout_in
        x, y, c = _mesh_pos()
        sibling = (x, y, 1 - c)
        sends = [_remote(gin_ref.at[c], gin_ref.at[c], send_sems, recv_sems, 0, sibling),
                 _remote(gout_ref.at[c], gout_ref.at[c], send_sems, recv_sems, 1, sibling)]
        for cp in sends:
            cp.start()
        _remote(gin_ref.at[1 - c], gin_ref.at[1 - c], send_sems, recv_sems, 0, sibling).wait_recv()
        _remote(gout_ref.at[1 - c], gout_ref.at[1 - c], send_sems, recv_sems, 1, sibling).wait_recv()
        for cp in sends:
            cp.wait_send()

    return pl.pallas_call(
        body, name=f"pair_share_{l}",
        in_specs=[HBM_SPEC, HBM_SPEC], out_specs=[HBM_SPEC, HBM_SPEC],
        out_shape=[jax.ShapeDtypeStruct(gin.shape, F32), jax.ShapeDtypeStruct(gout.shape, F32)],
        input_output_aliases={0: 0, 1: 1},
        scratch_shapes=[pltpu.SemaphoreType.DMA((2,)), pltpu.SemaphoreType.DMA((2,))],
    )(gin, gout)


def _adamw_large(w, m, v, g, i, prev, name):
    _, rows, cols = w.shape
    half = rows // 2
    tr = min(half, 256)
    nt = half // tr

    def body(w_ref, m_ref, v_ref, g_ref, *rest):
        go_ref, d_ref, mo_ref, vo_ref = rest[-4:]
        gv = g_ref[...]
        go_ref[...] = gv
        d_ref[...], mo_ref[...], vo_ref[...] = _adamw_math(w_ref[...], gv, m_ref[...], v_ref[...])

    full = pl.BlockSpec((None, tr, cols), lambda h, t: (i, h * nt + t, 0))
    out = jax.ShapeDtypeStruct(w.shape, F32)
    carried = [] if prev is None else list(prev)
    return pl.pallas_call(
        body, name=name, grid=(2, nt),
        in_specs=[full, full, full, pl.BlockSpec((None, tr, cols), lambda h, t: (h, t, 0))]
        + [pl.BlockSpec(memory_space=pl.ANY)] * len(carried),
        out_specs=[full] * 4, out_shape=[out] * 4,
        input_output_aliases={4 + k: k for k in range(len(carried))},
        compiler_params=_params(("arbitrary",) * 2),
    )(w, m, v, g, *carried)


def _adamw(w, g, m, v, name):
    shape = w.shape
    w2, g2, m2, v2 = (t.reshape(-1, shape[-1]) for t in (w, g, m, v))
    rows, cols = w2.shape
    tr = 256 if rows % 256 == 0 else rows

    def body(w_ref, g_ref, m_ref, v_ref, d_ref, mo_ref, vo_ref):
        d_ref[...], mo_ref[...], vo_ref[...] = _adamw_math(w_ref[...], g_ref[...], m_ref[...], v_ref[...])

    blk = pl.BlockSpec((tr, cols), lambda i: (i, 0))
    out = jax.ShapeDtypeStruct((rows, cols), F32)
    d, mo, vo = pl.pallas_call(
        body, name=name, grid=(rows // tr,), in_specs=[blk] * 4, out_specs=[blk] * 3, out_shape=[out] * 3,
        compiler_params=_params(("arbitrary",)),
    )(w2, g2, m2, v2)
    return d.reshape(shape), mo.reshape(shape), vo.reshape(shape)


def _layer_slot(l):
    return (l % 2) * 2 + l // 2


def kernel(x, ln_g, ln_b, w_in_even, w_out_even, pool_w, pool_scale, sconv_w, sconv_b, w_in_odd, w_out_odd, sgu_ln_g, sgu_ln_b, sgu_w, sgu_b, dconv_w, dconv_b, dnorm_g, dnorm_b, loss_target, m_ln_g, m_ln_b, m_w_in_even, m_w_out_even, m_pool_w, m_pool_scale, m_sconv_w, m_sconv_b, m_w_in_odd, m_w_out_odd, m_sgu_ln_g, m_sgu_ln_b, m_sgu_w, m_sgu_b, m_dconv_w, m_dconv_b, m_dnorm_g, m_dnorm_b, v_ln_g, v_ln_b, v_w_in_even, v_w_out_even, v_pool_w, v_pool_scale, v_sconv_w, v_sconv_b, v_w_in_odd, v_w_out_odd, v_sgu_ln_g, v_sgu_ln_b, v_sgu_w, v_sgu_b, v_dconv_w, v_dconv_b, v_dnorm_g, v_dnorm_b):
    weights = dict(ln_g=ln_g, ln_b=ln_b, w_in_even=w_in_even, w_out_even=w_out_even, pool_w=pool_w,
                   pool_scale=pool_scale, sconv_w=sconv_w, sconv_b=sconv_b, w_in_odd=w_in_odd, w_out_odd=w_out_odd,
                   sgu_ln_g=sgu_ln_g, sgu_ln_b=sgu_ln_b, sgu_w=sgu_w, sgu_b=sgu_b, dconv_w=dconv_w,
                   dconv_b=dconv_b, dnorm_g=dnorm_g, dnorm_b=dnorm_b)
    moments_m = dict(ln_g=m_ln_g, ln_b=m_ln_b, w_in_even=m_w_in_even, w_out_even=m_w_out_even, pool_w=m_pool_w,
                     pool_scale=m_pool_scale, sconv_w=m_sconv_w, sconv_b=m_sconv_b, w_in_odd=m_w_in_odd,
                     w_out_odd=m_w_out_odd, sgu_ln_g=m_sgu_ln_g, sgu_ln_b=m_sgu_ln_b, sgu_w=m_sgu_w, sgu_b=m_sgu_b,
                     dconv_w=m_dconv_w, dconv_b=m_dconv_b, dnorm_g=m_dnorm_g, dnorm_b=m_dnorm_b)
    moments_v = dict(ln_g=v_ln_g, ln_b=v_ln_b, w_in_even=v_w_in_even, w_out_even=v_w_out_even, pool_w=v_pool_w,
                     pool_scale=v_pool_scale, sconv_w=v_sconv_w, sconv_b=v_sconv_b, w_in_odd=v_w_in_odd,
                     w_out_odd=v_w_out_odd, sgu_ln_g=v_sgu_ln_g, sgu_ln_b=v_sgu_ln_b, sgu_w=v_sgu_w, sgu_b=v_sgu_b,
                     dconv_w=v_dconv_w, dconv_b=v_dconv_b, dnorm_g=v_dnorm_g, dnorm_b=v_dnorm_b)
    names = list(weights)

    xd, yd, cd = _mesh_pos()
    chip = 2 * xd + yd
    pos_arr = jnp.stack([cd, chip]).astype(jnp.int32)

    small_sh = jnp.concatenate(
        [sconv_w.reshape(6, HEAD), sgu_ln_g, sgu_ln_b, dconv_b, dnorm_g, dnorm_b, dconv_w.reshape(62, HEAD),
         jnp.zeros((2, HEAD), F32), pool_w.reshape(512, HEAD)], axis=0)
    win_first, wout_first, small_g = _gather_weights(
        _cast_own(w_in_even[0:1], pos_arr, "cast_win_first"), _cast_own(w_out_even[0:1], pos_arr, "cast_wout_first"),
        small_sh)
    small_g = lax.dynamic_update_slice(small_g, small_sh[None], (chip, 0, 0))
    later_in = jnp.concatenate([w_in_even[1:2], w_in_odd], axis=0)
    later_out = jnp.concatenate([w_out_even[1:2], w_out_odd], axis=0)
    g_send, g_recv, _, g_lands, g_token = _split_start(
        _gather_rest_copies(True), "gather_rest_start", [],
        [_cast_own(later_in, pos_arr, "cast_win_rest"), _cast_own(later_out, pos_arr, "cast_wout_rest")])

    def layer_weights(slot):
        return (win_first, wout_first, 0) if slot == 0 else (win_rest, wout_rest, slot - 1)

    def full_rows(lo, n):
        return jnp.transpose(small_g[:, lo:lo + n], (1, 0, 2)).reshape(n, D)

    sconv_w_f = full_rows(Q_SCONV_W, 6).reshape(2, SHORT_K, D)
    sln_g_f = full_rows(Q_SLN_G, 2)
    sln_b_f = full_rows(Q_SLN_B, 2)
    dconv_b_f = full_rows(Q_DCONV_B, 2)
    dn_g_f = full_rows(Q_DN_G, 2)
    dn_b_f = full_rows(Q_DN_B, 2)
    dconv_w_f = jnp.transpose(full_rows(Q_DCONV_W, 62).reshape(2, CONV_K, LANE_BLOCKS, LANE), (0, 2, 1, 3))
    pool_w_f = jnp.transpose(small_g[:, Q_POOL_W:].reshape(NQ, 2, 4, 64, HEAD), (1, 2, 0, 3, 4)).reshape(2, 4, HEAD, HEAD)
    pool_w_b = pool_w_f.astype(BF16)
    pool_wt_b = jnp.swapaxes(pool_w_f, 2, 3).astype(BF16)
    idx = jnp.arange(SGU_BLOCK)
    mask = (idx[None, :] // 64) <= (idx[:, None] // 64)
    ws_f = jnp.where(mask[None, None], sgu_w, 0.0)
    ws_b = ws_f.astype(BF16)
    wst_b = jnp.swapaxes(ws_f, 2, 3).astype(BF16)

    def row(a, i):
        return a[i:i + 1]

    residual = (x[0], jnp.ones((1, D), F32), jnp.zeros((1, D), F32))
    x_b = x[0].astype(BF16)
    saved = []
    conv_saved = {}
    for l in range(NL):
        i, slot = l // 2, _layer_slot(l)
        if l == 1:
            _, g_lands = _split_wait(_gather_rest_copies(False), "gather_rest_wait", g_send, g_recv, [], g_lands, x_b)
            win_rest, wout_rest = _gather_rest_forward(g_lands[0], g_lands[1])
        win_g, wout_g, k = layer_weights(slot)
        z = _proj_in(x_b, win_g, k, l, g_token if l == 0 else None)
        if l % 2 == 0:
            ycat = _even_fwd(z, pool_w_b[i], row(pool_scale, i), sconv_w_f[i], row(sconv_b, i), l)
        else:
            ycat, conv_hat, conv_rstd = _odd_fwd(
                z, row(sln_g_f, i), row(sln_b_f, i), ws_b[i], _sgu_bias_rows(sgu_b[i]),
                dconv_w_f[i], row(dconv_b_f, i), row(dn_g_f, i), row(dn_b_f, i), l)
            conv_saved[l] = (conv_hat, conv_rstd)
        x_next_b, xhat, rstd = _proj_out_ln(ycat, wout_g, k, l, *residual, row(ln_g, l), row(ln_b, l))
        saved.append((x_b, z, ycat, xhat, rstd))
        residual = (xhat, row(ln_g, l), row(ln_b, l))
        x_b = x_next_b

    small = {}
    d_ln_g = [None] * NL
    d_ln_b = [None] * NL
    large = {"w_in_even": None, "w_out_even": None, "w_in_odd": None, "w_out_odd": None}
    pending = None
    token = None

    def finish(exchange, after):
        lx, send, recv, srcs, lands, own_in, own_out = exchange
        _, (r_in, r_out) = _split_wait(_chip_copies(False), f"chip_wait_{lx}", send, recv, srcs, lands, after)
        fin = _chip_sum(own_in, r_in, pos_arr, f"chip_sum_in_{lx}")
        fout = _chip_sum(own_out, r_out, pos_arr, f"chip_sum_out_{lx}")
        gs_in, gs_out = _pair_share(fin, fout, lx)
        kind = "even" if lx % 2 == 0 else "odd"
        for nm, gs in ((f"w_in_{kind}", gs_in), (f"w_out_{kind}", gs_out)):
            large[nm] = _adamw_large(weights[nm], moments_m[nm], moments_v[nm], gs, lx // 2, large[nm],
                                     f"adamw_{nm}_{lx // 2}")
        return large[f"w_out_{kind}"][0]

    for l in reversed(range(NL)):
        i, slot = l // 2, _layer_slot(l)
        win_g, wout_g, k = layer_weights(slot)
        xin_b, z, ycat, xhat, rstd = saved[l]
        if l == NL - 1:
            loss_part, dr, dr_b, d_ln_g[l], d_ln_b[l] = _loss_ln_bwd(
                xhat, rstd, row(ln_g, l), row(ln_b, l), loss_target[0], l)
            loss = lax.psum(loss_part[0, 0], ("x", "y", "c"))
        else:
            dr, dr_b, d_ln_g[l], d_ln_b[l] = ln_done
        gout = _dwout(ycat, dr_b, l, token).reshape(NQ, RQ, D)
        if l % 2 == 0:
            dz, d_pw, d_ps, d_cw, d_cb = _even_bwd(z, dr_b, wout_g, k, pool_w_b[i], pool_wt_b[i],
                                                   row(pool_scale, i), sconv_w_f[i], row(sconv_b, i), l)
            small[("pool_w", i)] = d_pw
            small[("pool_scale", i)] = d_ps
            small[("sconv_w", i)] = d_cw
            small[("sconv_b", i)] = d_cb
        else:
            dz, d_lg, d_lb, d_ws, d_sb, d_cw, d_cb, d_ng, d_nb = _odd_bwd(
                z, dr_b, wout_g, k, *conv_saved[l], row(sln_g_f, i), row(sln_b_f, i), ws_b[i], wst_b[i],
                _sgu_bias_rows(sgu_b[i]), dconv_w_f[i], row(dn_g_f, i), row(dn_b_f, i), l)
            small[("sgu_ln_g", i)] = d_lg
            small[("sgu_ln_b", i)] = d_lb
            small[("sgu_w", i)] = jnp.where(mask[None], d_ws, 0.0)
            small[("sgu_b", i)] = d_sb
            small[("dconv_w", i)] = jnp.transpose(d_cw, (1, 0, 2)).reshape(CONV_K, D)
            small[("dconv_b", i)] = d_cb
            small[("dnorm_g", i)] = d_ng
            small[("dnorm_b", i)] = d_nb
        gin = _dwin(xin_b, dz, l)
        p_send, p_recv, p_srcs, p_lands, p_token = _split_start(
            _pair_copies(True), f"pair_start_{l}", [gin, gout],
            [jax.ShapeDtypeStruct((NQ, D // 2, WQ), F32), jax.ShapeDtypeStruct((NQ, RQ // 2, D), F32)])
        if l > 0:
            ln_done = _dx_ln_bwd(dz, win_g, k, l, dr, saved[l - 1][3], saved[l - 1][4], row(ln_g, l - 1), p_token)
            wait_after = ln_done[0]
            if pending is not None:
                finish(pending, wait_after)
        else:
            wait_after = finish(pending, p_token)
        (gin, gout), (rin, rout) = _split_wait(_pair_copies(False), f"pair_wait_{l}", p_send, p_recv,
                                               p_srcs, p_lands, wait_after)
        pin_b, pin_own = _pair_sum(gin, rin, pos_arr, f"pair_sum_in_{l}")
        pout_b, pout_own = _pair_sum(gout, rout, pos_arr, f"pair_sum_out_{l}")
        send, recv, srcs, lands, token = _split_start(
            _chip_copies(True), f"chip_start_{l}", [pin_b, pout_b],
            [jax.ShapeDtypeStruct((3,) + pin_b.shape[1:], BF16), jax.ShapeDtypeStruct((3,) + pout_b.shape[1:], BF16)])
        if l == 0:
            dxn = _dx(dz, win_g, k, l, dr, token)
        pending = (l, send, recv, srcs, lands, pin_own, pout_own)
    grad_x = dxn[None]

    def both(name, first, step):
        return [(first, small[(name, 0)]), (first + step, small[(name, 1)])]

    vectors = ([(R_LN_G + l, d_ln_g[l]) for l in range(NL)] + [(R_LN_B + l, d_ln_b[l]) for l in range(NL)]
               + both("pool_scale", R_PSCALE, 1) + both("sconv_b", R_SCONV_B, 1) + both("sconv_w", R_SCONV_W, SHORT_K)
               + both("sgu_ln_g", R_SLN_G, 1) + both("sgu_ln_b", R_SLN_B, 1) + both("dconv_b", R_DCONV_B, 1)
               + both("dnorm_g", R_DN_G, 1) + both("dnorm_b", R_DN_B, 1) + both("dconv_w", R_DCONV_W, CONV_K)
               + both("sgu_b", R_SGU_B, 4))
    sgu_w_rows = 4 * SGU_BLOCK
    pool_w_rows = 4 * HEAD
    total, total_sgu_w, total_pool_w = _allreduce_small(
        [(R_VECTORS, D, F32, vectors),
         (2 * sgu_w_rows, SGU_BLOCK, BF16,
          [(i * sgu_w_rows, small[("sgu_w", i)].reshape(sgu_w_rows, SGU_BLOCK)) for i in range(2)]),
         (2 * pool_w_rows, HEAD, BF16,
          [(i * pool_w_rows, small[("pool_w", i)].reshape(pool_w_rows, HEAD)) for i in range(2)])],
        dxn)
    finish(pending, total)

    def mine(a):
        return lax.dynamic_slice_in_dim(a, chip * HEAD, HEAD, axis=a.ndim - 1)

    grads = {
        "ln_g": total[R_LN_G:R_LN_G + 4],
        "ln_b": total[R_LN_B:R_LN_B + 4],
        "pool_scale": total[R_PSCALE:R_PSCALE + 2],
        "sconv_b": total[R_SCONV_B:R_SCONV_B + 2],
        "sconv_w": mine(total[R_SCONV_W:R_SCONV_W + 6].reshape(2, SHORT_K, D)),
        "sgu_ln_g": mine(total[R_SLN_G:R_SLN_G + 2]),
        "sgu_ln_b": mine(total[R_SLN_B:R_SLN_B + 2]),
        "dconv_b": mine(total[R_DCONV_B:R_DCONV_B + 2]),
        "dnorm_g": mine(total[R_DN_G:R_DN_G + 2]),
        "dnorm_b": mine(total[R_DN_B:R_DN_B + 2]),
        "dconv_w": mine(total[R_DCONV_W:R_DCONV_W + 62].reshape(2, CONV_K, D)),
        "sgu_b": total[R_SGU_B:R_SGU_B + 8, 0:SGU_BLOCK].reshape(2, 4, SGU_BLOCK),
        "sgu_w": total_sgu_w.reshape(2, 4, SGU_BLOCK, SGU_BLOCK),
        "pool_w": lax.dynamic_slice_in_dim(total_pool_w.reshape(2, 4, HEAD, HEAD), chip * 64, 64, axis=2),
    }

    deltas, new_m, new_v = {}, {}, {}
    for name in names:
        if name in large:
            grads[name], deltas[name], new_m[name], new_v[name] = large[name]
        else:
            deltas[name], new_m[name], new_v[name] = _adamw(
                weights[name], grads[name], moments_m[name], moments_v[name], f"adamw_{name}")

    return (loss, grad_x, *[grads[n] for n in names], *[deltas[n] for n in names],
            *[new_m[n] for n in names], *[new_v[n] for n in names])
```
